```python
import jax, jax.numpy as jnp
from jax import lax
import numpy as np

D_MODEL = 1024
BATCH = 16
SEQ = 256
DEPTH = 1
DEC_BATCH = 4
DEC_SEQ = 1024
PAST_LEN = 512

GRID_W = 64
HG_HEADS = 4
HG_DK = 128
HG_DV = 128
HG_WIDTH = HG_HEADS * HG_DK
HG_CHUNK = 32
N_HEADS = 8
N_KV_HEADS = 2
HEAD_DIM = 64
ATT_WIDTH = N_HEADS * HEAD_DIM
KV_WIDTH = N_KV_HEADS * HEAD_DIM
Q_BLOCK = 128
ROPE_THETA = 10000.0
N_EXPERTS = 256
TOP_K = 8
N_GROUPS = 8
TOPK_GROUPS = 4
EXPERT_DIM = 256
SHARED_DIM = 256
ROUTED_SCALE = 2.5
MOE_BLOCK = 128
NORM_EPS = 1e-6
DEEPNORM_ALPHA = (2 * DEPTH) ** 0.25
DEEPNORM_BETA = (8 * DEPTH) ** -0.25
IN_SIZES = (HG_WIDTH, HG_HEADS * HG_DV, HG_WIDTH, HG_WIDTH, HG_HEADS * HG_DV, ATT_WIDTH, KV_WIDTH, KV_WIDTH, D_MODEL, D_MODEL)
IN_WIDTH = sum(IN_SIZES)

kernel_name = 'hybrid_hgrn2_gqa_moe_diffusion_step'


def layer_norm(x, g, b):
    xf = x.astype(jnp.float32)
    mu = jnp.mean(xf, axis=-1, keepdims=True)
    var = jnp.mean(jnp.square(xf - mu), axis=-1, keepdims=True)
    return ((xf - mu) * lax.rsqrt(var + NORM_EPS) * g.astype(jnp.float32) + b.astype(jnp.float32)).astype(x.dtype)


def rms_norm(x, g):
    xf = x.astype(jnp.float32)
    y = xf * lax.rsqrt(jnp.mean(jnp.square(xf), axis=-1, keepdims=True) + NORM_EPS)
    return (y * g.astype(jnp.float32)).astype(x.dtype)


def axial_rope(x):
    L, Dh = x.shape[1], x.shape[-1]
    rows = L // GRID_W
    row = jnp.repeat(jnp.arange(rows), GRID_W).astype(jnp.float32)
    col = jnp.tile(jnp.arange(GRID_W), rows).astype(jnp.float32)
    half = Dh // 2
    quarter = half // 2
    inv_freq = ROPE_THETA ** (-jnp.arange(quarter, dtype=jnp.float32) / quarter)

    def rotate(xa, pos):
        ang = pos[:, None] * inv_freq[None, :]
        cos = jnp.cos(ang)[None, :, None, :].astype(x.dtype)
        sin = jnp.sin(ang)[None, :, None, :].astype(x.dtype)
        x1, x2 = xa[..., :quarter], xa[..., quarter:]
        return jnp.concatenate([x1 * cos - x2 * sin, x1 * sin + x2 * cos], axis=-1)

    return jnp.concatenate([rotate(x[..., :half], row), rotate(x[..., half:], col)], axis=-1)


def blocked_attention(q, k, v):
    B, Lq, H, Dh = q.shape
    kvh = k.shape[2]
    groups = H // kvh
    nb = Lq // Q_BLOCK
    qb = jnp.moveaxis(q.reshape(B, nb, Q_BLOCK, kvh, groups, Dh), 1, 0)
    scale = Dh ** -0.5

    def one_block(qblk):
        s = jnp.einsum('bqkgd,bskd->bkgqs', qblk, k, preferred_element_type=jnp.float32) * scale
        p = jax.nn.softmax(s, axis=-1)
        return jnp.einsum('bkgqs,bskd->bqkgd', p.astype(v.dtype), v)

    o = lax.map(one_block, qb)
    return jnp.moveaxis(o, 0, 1).reshape(B, Lq, H * Dh)


def hgrn_chunk_scan(q, k, v, g, s0):
    B, L, H, DK = q.shape
    DV = v.shape[-1]
    n = L // HG_CHUNK

    def chunks(a):
        return a.astype(jnp.float32).reshape(B, n, HG_CHUNK, H, a.shape[-1])

    q, k, v, g = chunks(q), chunks(k), chunks(v), chunks(g)
    b = jnp.cumsum(g, axis=2)
    causal = jnp.tril(jnp.ones((HG_CHUNK, HG_CHUNK), dtype=bool))
    diff = b[:, :, :, None] - b[:, :, None, :]
    decay = jnp.exp(jnp.where(causal[None, None, :, :, None, None], diff, -jnp.inf))
    scores = jnp.einsum('bnthd,bnshd,bntshd->bnhts', q, k, decay)
    o_intra = jnp.einsum('bnhts,bnshv->bnthv', scores, v)
    b_last = b[:, :, -1]
    k_to_end = k * jnp.exp(b_last[:, :, None] - b)
    u = jnp.einsum('bnshd,bnshv->bnhdv', k_to_end, v)
    chunk_decay = jnp.exp(b_last)

    def step(s, inp):
        dec, uc = inp
        return dec[..., None] * s + uc, s

    s_final, s_before = lax.scan(step, s0.astype(jnp.float32), (jnp.moveaxis(chunk_decay, 1, 0), jnp.moveaxis(u, 1, 0)))
    s_before = jnp.moveaxis(s_before, 0, 1)
    o_inter = jnp.einsum('bnthd,bnhdv->bnthv', q * jnp.exp(b), s_before)
    return (o_intra + o_inter).reshape(B, L, H, DV), s_final


def hgrn_branch(hq, hi, hff, hfb, hgate, lb, norm_g, s0):
    B, L, _ = hq.shape

    def heads(a, d):
        return a.reshape(B, L, HG_HEADS, d)

    q = heads(jax.nn.silu(hq), HG_DK)
    v = heads(hi, HG_DV)

    def gates(f_raw, lbd):
        f = lbd + (1.0 - lbd) * jax.nn.sigmoid(f_raw.astype(jnp.float32))
        return heads(1.0 - f, HG_DK), heads(jnp.log(f), HG_DK)

    k_f, g_f = gates(hff, lb[0])
    k_b, g_b = gates(hfb, lb[1])
    flip = lambda a: jnp.flip(a, axis=1)
    o_f, s_f = hgrn_chunk_scan(q, k_f, v, g_f, s0[:, 0])
    o_b, s_b = hgrn_chunk_scan(flip(q), flip(k_b), flip(v), flip(g_b), s0[:, 1])
    o = rms_norm(o_f + flip(o_b), norm_g).reshape(B, L, HG_HEADS * HG_DV)
    o = o * jax.nn.silu(hgate.astype(jnp.float32))
    return o.astype(hq.dtype), jnp.stack([s_f, s_b], axis=1)


def token_mixer(h, lp, lb, state0, ctx_k, ctx_v):
    B, L, _ = h.shape
    z = jnp.einsum('bld,de->ble', h, lp['w_in'])
    split_points = np.cumsum(IN_SIZES)[:-1].tolist()
    hq, hi, hff, hfb, hgate, aq, ak, av, gate_a, gate_b = jnp.split(z, split_points, axis=-1)
    o_a, states = hgrn_branch(hq, hi, hff, hfb, hgate, lb, lp['hg_norm_g'], state0)
    q = rms_norm(aq.reshape(B, L, N_HEADS, HEAD_DIM), lp['q_norm_g'])
    k = rms_norm(ak.reshape(B, L, N_KV_HEADS, HEAD_DIM), lp['k_norm_g'])
    v = av.reshape(B, L, N_KV_HEADS, HEAD_DIM)
    if ctx_k is None:
        o_b = blocked_attention(q, k, v)
    else:
        k_all = jnp.concatenate([ctx_k.astype(k.dtype), axial_rope(k)], axis=1)
        v_all = jnp.concatenate([ctx_v.astype(v.dtype), v], axis=1)
        o_b = blocked_attention(axial_rope(q), k_all, v_all)
    branch_a = jnp.einsum('blc,cd->bld', o_a, lp['w_branch_a'])
    branch_b = jnp.einsum('blc,cd->bld', o_b.astype(h.dtype), lp['w_branch_b'])
    merged = jax.nn.sigmoid(gate_a) * branch_a + jax.nn.sigmoid(gate_b) * branch_b
    out = jnp.einsum('bld,de->ble', merged, lp['w_out'])
    return out, k, v, states


def swiglu(x, w_gate, w_up, w_down):
    return (jax.nn.silu(x @ w_gate) * (x @ w_up)) @ w_down


def routed_experts(t, idx, w, w_gate, w_up, w_down):
    T, D = t.shape
    S = T * TOP_K
    flat_e = idx.reshape(S)
    flat_tok = jnp.arange(S, dtype=jnp.int32) // TOP_K
    flat_w = w.reshape(S)
    order = jnp.argsort(flat_e)
    sorted_e = flat_e[order]
    counts = jnp.bincount(flat_e, length=N_EXPERTS)
    padded = (counts + MOE_BLOCK - 1) // MOE_BLOCK * MOE_BLOCK
    start = jnp.cumsum(counts) - counts
    pad_end = jnp.cumsum(padded)
    pad_start = pad_end - padded
    dest = pad_start[sorted_e] + (jnp.arange(S, dtype=jnp.int32) - start[sorted_e])
    n_blocks = -(-S // MOE_BLOCK) + N_EXPERTS
    P = n_blocks * MOE_BLOCK
    buf_tok = jnp.zeros((P,), jnp.int32).at[dest].set(flat_tok[order])
    buf_w = jnp.zeros((P,), jnp.float32).at[dest].set(flat_w[order])
    block_start = jnp.arange(n_blocks, dtype=jnp.int32) * MOE_BLOCK
    block_e = jnp.minimum(jnp.searchsorted(pad_end, block_start, side='right'), N_EXPERTS - 1)
    xb = t[buf_tok].reshape(n_blocks, MOE_BLOCK, D)

    def expert_block(args):
        xblk, e = args
        return swiglu(xblk, w_gate[e], w_up[e], w_down[e])

    yb = lax.map(expert_block, (xb, block_e)).reshape(P, D)
    y = jax.ops.segment_sum(yb.astype(jnp.float32) * buf_w[:, None], buf_tok, num_segments=T)
    return y


def moe_ffn(h, lp):
    B, L, D = h.shape
    t = h.reshape(B * L, D)
    T = t.shape[0]
    scores = jax.nn.sigmoid(jnp.einsum('td,de->te', t, lp['w_router'], preferred_element_type=jnp.float32))
    sel = scores + lp['router_bias'].astype(jnp.float32)
    per_group = N_EXPERTS // N_GROUPS
    group_score = jnp.sum(lax.top_k(sel.reshape(T, N_GROUPS, per_group), 2)[0], axis=-1)
    _, top_groups = lax.top_k(group_score, TOPK_GROUPS)
    group_mask = jnp.any(top_groups[:, :, None] == jnp.arange(N_GROUPS)[None, None, :], axis=1)
    sel = jnp.where(jnp.repeat(group_mask, per_group, axis=1), sel, -jnp.inf)
    _, idx = lax.top_k(sel, TOP_K)
    w = jnp.take_along_axis(scores, idx, axis=1)
    w = w / jnp.sum(w, axis=1, keepdims=True) * ROUTED_SCALE
    routed = routed_experts(t, idx, w, lp['w_e_gate'], lp['w_e_up'], lp['w_e_down'])
    shared = swiglu(t, lp['w_s_gate'], lp['w_s_up'], lp['w_s_down']).astype(jnp.float32)
    return (routed + shared).astype(h.dtype).reshape(B, L, D)


def trunk_layer(x, cond, lp, lb, state0, ctx_k, ctx_v):
    mod = jnp.einsum('bd,de->be', jax.nn.silu(cond), lp['w_mod']) + lp['b_mod']
    shift1, scale1, gate1, shift2, scale2, gate2 = jnp.split(mod[:, None, :], 6, axis=-1)
    mix, k, v, states = token_mixer(x * (1.0 + scale1) + shift1, lp, lb, state0, ctx_k, ctx_v)
    x = layer_norm(DEEPNORM_ALPHA * x + gate1 * mix, lp['ln1_g'], lp['ln1_b'])
    ffn = moe_ffn(x * (1.0 + scale2) + shift2, lp)
    x = layer_norm(DEEPNORM_ALPHA * x + gate2 * ffn, lp['ln2_g'], lp['ln2_b'])
    return x, k, v, states


def setup_inputs(seed: int = 0) -> dict:
    key = jax.random.key(seed)
    ks = jax.random.split(key, 32)
    f32 = jnp.float32

    def nrm(k, shape, s):
        return jax.random.normal(k, shape, f32) * s

    d_in = D_MODEL ** -0.5
    return {
        'x_prompt': nrm(ks[0], (BATCH, SEQ, D_MODEL), 1.0),
        'x_sample': nrm(ks[1], (DEC_BATCH, DEC_SEQ, D_MODEL), 1.0),
        'cache_k': nrm(ks[2], (DEC_BATCH, DEPTH, PAST_LEN, N_KV_HEADS, HEAD_DIM), 1.0),
        'cache_v': nrm(ks[3], (DEC_BATCH, DEPTH, PAST_LEN, N_KV_HEADS, HEAD_DIM), 1.0),
        'state_hgrn': nrm(ks[4], (DEC_BATCH, DEPTH, 2, HG_HEADS, HG_DK, HG_DV), 0.3),
        'c': nrm(ks[5], (DEC_BATCH, D_MODEL), 1.0),
        'c_ctx': nrm(ks[6], (D_MODEL,), 1.0),
        'w_mod': nrm(ks[7], (DEPTH, D_MODEL, 6 * D_MODEL), d_in),
        'b_mod': nrm(ks[8], (DEPTH, 6 * D_MODEL), 0.02),
        'w_in': nrm(ks[9], (DEPTH, D_MODEL, IN_WIDTH), d_in),
        'hg_lb': nrm(ks[10], (DEPTH + 1, 2, HG_WIDTH), 0.5),
        'hg_norm_g': 1.0 + nrm(ks[11], (DEPTH, HG_DV), 0.1),
        'q_norm_g': 1.0 + nrm(ks[12], (DEPTH, HEAD_DIM), 0.1),
        'k_norm_g': 1.0 + nrm(ks[13], (DEPTH, HEAD_DIM), 0.1),
        'w_branch_a': nrm(ks[14], (DEPTH, HG_HEADS * HG_DV, D_MODEL), (HG_HEADS * HG_DV) ** -0.5),
        'w_branch_b': nrm(ks[15], (DEPTH, ATT_WIDTH, D_MODEL), ATT_WIDTH ** -0.5),
        'w_out': nrm(ks[16], (DEPTH, D_MODEL, D_MODEL), d_in * DEEPNORM_BETA),
        'ln1_g': 1.0 + nrm(ks[17], (DEPTH, D_MODEL), 0.1),
        'ln1_b': nrm(ks[18], (DEPTH, D_MODEL), 0.02),
        'w_router': nrm(ks[19], (DEPTH, D_MODEL, N_EXPERTS), d_in),
        'router_bias': nrm(ks[20], (DEPTH, N_EXPERTS), 0.01),
        'w_e_gate': nrm(ks[21], (DEPTH, N_EXPERTS, D_MODEL, EXPERT_DIM), d_in),
        'w_e_up': nrm(ks[22], (DEPTH, N_EXPERTS, D_MODEL, EXPERT_DIM), d_in),
        'w_e_down': nrm(ks[23], (DEPTH, N_EXPERTS, EXPERT_DIM, D_MODEL), EXPERT_DIM ** -0.5 * DEEPNORM_BETA),
        'w_s_gate': nrm(ks[24], (DEPTH, D_MODEL, SHARED_DIM), d_in),
        'w_s_up': nrm(ks[25], (DEPTH, D_MODEL, SHARED_DIM), d_in),
        'w_s_down': nrm(ks[26], (DEPTH, SHARED_DIM, D_MODEL), SHARED_DIM ** -0.5 * DEEPNORM_BETA),
        'ln2_g': 1.0 + nrm(ks[27], (DEPTH, D_MODEL), 0.1),
        'ln2_b': nrm(ks[28], (DEPTH, D_MODEL), 0.02),
    }


def reference(x_prompt, x_sample, cache_k, cache_v, state_hgrn, c, c_ctx, w_mod, b_mod, w_in, hg_lb, hg_norm_g, q_norm_g, k_norm_g, w_branch_a, w_branch_b, w_out, ln1_g, ln1_b, w_router, router_bias, w_e_gate, w_e_up, w_e_down, w_s_gate, w_s_up, w_s_down, ln2_g, ln2_b):
    lb_all = jnp.cumsum(jax.nn.softmax(hg_lb.astype(jnp.float32), axis=0), axis=0)
    n_prompt = x_prompt.shape[0]
    cond_ctx = jnp.broadcast_to(c_ctx, (n_prompt, c_ctx.shape[0]))
    zero_state = jnp.zeros((n_prompt, 2, HG_HEADS, HG_DK, HG_DV), jnp.float32)
    y_prompt = x_prompt
    y_sample = x_sample
    new_k, new_v, new_s = [], [], []
    for l in range(DEPTH):
        lp = {
            'w_mod': w_mod[l], 'b_mod': b_mod[l], 'w_in': w_in[l], 'hg_norm_g': hg_norm_g[l],
            'q_norm_g': q_norm_g[l], 'k_norm_g': k_norm_g[l], 'w_branch_a': w_branch_a[l],
            'w_branch_b': w_branch_b[l], 'w_out': w_out[l], 'ln1_g': ln1_g[l], 'ln1_b': ln1_b[l],
            'w_router': w_router[l], 'router_bias': router_bias[l], 'w_e_gate': w_e_gate[l],
            'w_e_up': w_e_up[l], 'w_e_down': w_e_down[l], 'w_s_gate': w_s_gate[l],
            'w_s_up': w_s_up[l], 'w_s_down': w_s_down[l], 'ln2_g': ln2_g[l], 'ln2_b': ln2_b[l],
        }
        y_prompt, k_l, v_l, s_l = trunk_layer(y_prompt, cond_ctx, lp, lb_all[l], zero_state, None, None)
        new_k.append(k_l)
        new_v.append(v_l)
        new_s.append(s_l)
        y_sample, _, _, _ = trunk_layer(y_sample, c, lp, lb_all[l], state_hgrn[:, l], cache_k[:, l], cache_v[:, l])
    new_cache_k = jnp.stack(new_k, axis=1)
    new_cache_v = jnp.stack(new_v, axis=1)
    new_state_hgrn = jnp.stack(new_s, axis=1)
    return (y_prompt, y_sample, new_cache_k, new_cache_v, new_state_hgrn)
```

```python
import functools

import jax
import jax.numpy as jnp
from jax import lax
from jax.experimental import pallas as pl
from jax.experimental.pallas import tpu as pltpu

F32 = jnp.float32
BF16 = jnp.bfloat16

GRID_W = 64
HG_HEADS = 4
HG_DK = 128
HG_DV = 128
HG_WIDTH = HG_HEADS * HG_DK
N_HEADS = 8
N_KV_HEADS = 2
HEAD_DIM = 64
ATT_WIDTH = N_HEADS * HEAD_DIM
KV_WIDTH = N_KV_HEADS * HEAD_DIM
ROPE_THETA = 10000.0
N_EXPERTS = 256
TOP_K = 8
N_GROUPS = 8
TOPK_GROUPS = 4
GROUP_SIZE = N_EXPERTS // N_GROUPS
ROUTED_SCALE = 2.5
NORM_EPS = 1e-6
DEPTH = 1
DEEPNORM_ALPHA = (2 * DEPTH) ** 0.25

LANES = 128
SUBLANES = 8
VMEM_LIMIT = 56 * 1024 * 1024

TOKEN_TILE = 256
HG_CHUNK = 32
ATT_Q_BLOCK = 128
MOE_BLOCK = 128
ROW_STRIDE = MOE_BLOCK + 1


def _sigmoid(x):
    return 1.0 / (1.0 + jnp.exp(-x))


def _silu(x):
    return x * _sigmoid(x)


def _params(sem=None):
    return pltpu.CompilerParams(dimension_semantics=sem, vmem_limit_bytes=VMEM_LIMIT)


def _mod_kernel(c_ref, w_ref, b_ref, o_ref):
    s = _silu(c_ref[...]).astype(BF16)
    o_ref[...] = jnp.dot(s, w_ref[...], preferred_element_type=F32) + b_ref[...]


def _modulation(cond, w_mod, b_mod):
    n, d = cond.shape
    width = w_mod.shape[1]
    tn = width // 4
    return pl.pallas_call(
        _mod_kernel,
        grid=(4,),
        in_specs=[
            pl.BlockSpec((n, d), lambda j: (0, 0)),
            pl.BlockSpec((d, tn), lambda j: (0, j)),
            pl.BlockSpec((1, tn), lambda j: (0, j)),
        ],
        out_specs=pl.BlockSpec((n, tn), lambda j: (0, j)),
        out_shape=jax.ShapeDtypeStruct((n, width), F32),
        compiler_params=_params(("arbitrary",)),
    )(cond, w_mod, b_mod)


def _inproj_kernel(x_ref, mod_ref, w_ref, lb_ref, zh_ref, za_ref, zg_ref, *, d):
    shift = mod_ref[0, :, 0:d]
    scale = mod_ref[0, :, d:2 * d]
    h = (x_ref[...] * (1.0 + scale) + shift).astype(BF16)

    def proj(lo, hi):
        return jnp.dot(h, w_ref[:, lo:hi], preferred_element_type=F32)

    w = HG_WIDTH
    zh_ref[:, 0:w] = _silu(proj(0, w))
    zh_ref[:, w:2 * w] = proj(w, 2 * w)
    for i in range(2):
        lb = lb_ref[i:i + 1, :]
        zh_ref[:, (2 + i) * w:(3 + i) * w] = lb + (1.0 - lb) * _sigmoid(proj((2 + i) * w, (3 + i) * w))
    zh_ref[:, 4 * w:5 * w] = _silu(proj(4 * w, 5 * w))
    a0 = 5 * w
    a1 = a0 + ATT_WIDTH + 2 * KV_WIDTH
    za_ref[...] = proj(a0, a1)
    for i in range(4):
        lo = a1 + i * (d // 2)
        zg_ref[:, i * (d // 2):(i + 1) * (d // 2)] = _sigmoid(proj(lo, lo + d // 2))


def _mod_row_map(n_ctx_tiles, tiles_per_latent):
    def index_map(i):
        row = jnp.where(i < n_ctx_tiles, 0, 1 + (i - n_ctx_tiles) // tiles_per_latent)
        return (row, 0, 0)
    return index_map


def _input_projection(x_all, mod3, w_in, lb, mod_map):
    t, d = x_all.shape
    width = w_in.shape[1]
    zh_w = 5 * HG_WIDTH
    za_w = ATT_WIDTH + 2 * KV_WIDTH
    zg_w = 2 * d
    assert width == zh_w + za_w + zg_w
    tm = TOKEN_TILE
    return pl.pallas_call(
        functools.partial(_inproj_kernel, d=d),
        grid=(t // tm,),
        in_specs=[
            pl.BlockSpec((tm, d), lambda i: (i, 0)),
            pl.BlockSpec((1, 1, mod3.shape[2]), mod_map),
            pl.BlockSpec((d, width), lambda i: (0, 0)),
            pl.BlockSpec((2, HG_WIDTH), lambda i: (0, 0)),
        ],
        out_specs=[
            pl.BlockSpec((tm, zh_w), lambda i: (i, 0)),
            pl.BlockSpec((tm, za_w), lambda i: (i, 0)),
            pl.BlockSpec((tm, zg_w), lambda i: (i, 0)),
        ],
        out_shape=[
            jax.ShapeDtypeStruct((t, zh_w), F32),
            jax.ShapeDtypeStruct((t, za_w), F32),
            jax.ShapeDtypeStruct((t, zg_w), F32),
        ],
        compiler_params=_params(("arbitrary",)),
    )(x_all, mod3, w_in, lb)


def _hgrn_kernel(*refs, seq, has_state):
    if has_state:
        q_ref, v_ref, ff_ref, fb_ref, gs_ref, ng_ref, s0_ref, o_ref, sout_ref = refs
    else:
        q_ref, v_ref, ff_ref, fb_ref, gs_ref, ng_ref, o_ref, sout_ref = refs
        s0_ref = None
    c = HG_CHUNK
    n = seq // c
    q3 = q_ref[...].reshape(n, c, HG_DK)
    v3 = v_ref[...].astype(BF16).reshape(n, c, HG_DV)
    pos = lax.broadcasted_iota(jnp.int32, (seq, HG_DK), 0) % c
    t_idx = lax.broadcasted_iota(jnp.int32, (c, c), 0)
    s_idx = lax.broadcasted_iota(jnp.int32, (c, c), 1)
    o_sum = None
    for direction, f_ref in enumerate((ff_ref, fb_ref)):
        reverse = direction == 1
        f = f_ref[...]
        k3 = (1.0 - f).reshape(n, c, HG_DK)
        b = jnp.log(f)
        step = 1
        while step < c:
            if reverse:
                b = b + jnp.where(pos < c - step, pltpu.roll(b, seq - step, axis=0), 0.0)
            else:
                b = b + jnp.where(pos >= step, pltpu.roll(b, step, axis=0), 0.0)
            step *= 2
        b3 = b.reshape(n, c, HG_DK)
        edge = b3[:, 0:1, :] if reverse else b3[:, c - 1:c, :]
        mid = b3[:, c // 2:c // 2 + 1, :]
        q_in = (q3 * jnp.exp(b3)).astype(BF16)
        k_end = (k3 * jnp.exp(edge - b3)).astype(BF16)
        q_mid = (q3 * jnp.exp(b3 - mid)).astype(BF16)
        k_mid = (k3 * jnp.exp(mid - b3)).astype(BF16)
        scores = jnp.einsum('ntd,nsd->nts', q_mid, k_mid, preferred_element_type=F32)
        keep = (s_idx >= t_idx) if reverse else (s_idx <= t_idx)
        scores = jnp.where(keep[None], scores, 0.0).astype(BF16)
        o_intra = jnp.einsum('nts,nsv->ntv', scores, v3, preferred_element_type=F32)
        upd = jnp.einsum('nsv,nsd->nvd', v3, k_end, preferred_element_type=F32)
        dec = jnp.exp(edge)
        if has_state:
            st = s0_ref[0, direction, 0].T
        else:
            st = jnp.zeros((HG_DV, HG_DK), F32)
        before = [None] * n
        for ci in (range(n - 1, -1, -1) if reverse else range(n)):
            before[ci] = st.astype(BF16)
            st = st * dec[ci] + upd[ci]
        st_before = jnp.stack(before, axis=0)
        o_inter = jnp.einsum('ntd,nvd->ntv', q_in, st_before, preferred_element_type=F32)
        o_dir = (o_intra + o_inter).reshape(seq, HG_DV)
        o_sum = o_dir if o_sum is None else o_sum + o_dir
        sout_ref[0, direction, 0] = st.T
    ms = jnp.mean(o_sum * o_sum, axis=-1, keepdims=True)
    o = o_sum * lax.rsqrt(ms + NORM_EPS) * ng_ref[...]
    o_ref[...] = (o * gs_ref[...]).astype(o_ref.dtype)


def _hgrn_scan(zh, norm_g, s0, *, batch, seq, row_block0):
    has_state = s0 is not None
    h = HG_HEADS

    def col(section):
        return pl.BlockSpec((seq, HG_DK), lambda b, j: (row_block0 + b, section * h + j))

    in_specs = [col(0), col(1), col(2), col(3), col(4), pl.BlockSpec((1, HG_DV), lambda b, j: (0, 0))]
    args = [zh, zh, zh, zh, zh, norm_g]
    state_spec = pl.BlockSpec((1, 2, 1, HG_DK, HG_DV), lambda b, j: (b, 0, j, 0, 0))
    if has_state:
        in_specs.append(state_spec)
        args.append(s0)
    return pl.pallas_call(
        functools.partial(_hgrn_kernel, seq=seq, has_state=has_state),
        grid=(batch, h),
        in_specs=in_specs,
        out_specs=[pl.BlockSpec((seq, HG_DV), lambda b, j: (b, j)), state_spec],
        out_shape=[
            jax.ShapeDtypeStruct((batch * seq, h * HG_DV), BF16),
            jax.ShapeDtypeStruct((batch, 2, h, HG_DK, HG_DV), F32),
        ],
        compiler_params=_params(("arbitrary", "arbitrary")),
    )(*args)


def _group_rms_norm(x, group_ones, gain):
    sq = x * x
    hi = sq.astype(BF16)
    lo = (sq - hi.astype(F32)).astype(BF16)
    total = (jnp.dot(hi, group_ones, preferred_element_type=F32)
             + jnp.dot(lo, group_ones, preferred_element_type=F32))
    return x * lax.rsqrt(total * (1.0 / HEAD_DIM) + NORM_EPS) * gain


def _rope(x, cos, sin_signed):
    width = x.shape[1]
    quarter = HEAD_DIM // 4
    lane = lax.broadcasted_iota(jnp.int32, x.shape, 1)
    partner = jnp.where(lane % (2 * quarter) < quarter,
                        pltpu.roll(x, width - quarter, axis=1),
                        pltpu.roll(x, quarter, axis=1))
    return x * cos + partner * sin_signed


def _attn_kernel(*refs, seq, latent):
    if latent:
        za_ref, qg_ref, kg_ref, gm_ref, cos_ref, sin_ref, ck_ref, cv_ref, o_ref = refs
    else:
        za_ref, qg_ref, kg_ref, gm_ref, o_ref, kout_ref, vout_ref = refs
    q = za_ref[:, 0:ATT_WIDTH]
    k = za_ref[:, ATT_WIDTH:ATT_WIDTH + KV_WIDTH]
    v = za_ref[:, ATT_WIDTH + KV_WIDTH:ATT_WIDTH + 2 * KV_WIDTH]
    qn = _group_rms_norm(q, gm_ref[...], qg_ref[...])
    kn = _group_rms_norm(k, gm_ref[0:KV_WIDTH, 0:KV_WIDTH], kg_ref[...])
    if latent:
        qn = _rope(qn, cos_ref[...], sin_ref[...])
        kr = _rope(kn, cos_ref[:, 0:KV_WIDTH], sin_ref[:, 0:KV_WIDTH])
    else:
        kout_ref[...] = kn
        vout_ref[...] = v
        kr = kn
    qb = (qn * (HEAD_DIM ** -0.5)).astype(BF16)
    kb = kr.astype(BF16)
    vb = v.astype(BF16)
    if latent:
        ckb = ck_ref[0].astype(BF16)
        cvb = cv_ref[0].astype(BF16)
    groups = N_HEADS // N_KV_HEADS
    tq = ATT_Q_BLOCK if latent else seq
    nt = (((1,), (1,)), ((), ()))
    for kh in range(N_KV_HEADS):
        ksl = slice(kh * HEAD_DIM, (kh + 1) * HEAD_DIM)
        k_new = kb[:, ksl]
        v_new = vb[:, ksl]
        for blk in range(seq // tq):
            rows = slice(blk * tq, (blk + 1) * tq)
            heads = [kh * groups + g for g in range(groups)]
            q_st = jnp.concatenate([qb[rows, hd * HEAD_DIM:(hd + 1) * HEAD_DIM] for hd in heads], axis=0)
            s_new = lax.dot_general(q_st, k_new, nt, preferred_element_type=F32)
            m = jnp.max(s_new, axis=-1, keepdims=True)
            if latent:
                s_ctx = lax.dot_general(q_st, ckb[:, ksl], nt, preferred_element_type=F32)
                m = jnp.maximum(m, jnp.max(s_ctx, axis=-1, keepdims=True))
            p_new = jnp.exp(s_new - m)
            denom = jnp.sum(p_new, axis=-1, keepdims=True)
            acc = jnp.dot(p_new.astype(BF16), v_new, preferred_element_type=F32)
            if latent:
                p_ctx = jnp.exp(s_ctx - m)
                denom = denom + jnp.sum(p_ctx, axis=-1, keepdims=True)
                acc = acc + jnp.dot(p_ctx.astype(BF16), cvb[:, ksl], preferred_element_type=F32)
            out = acc / denom
            for g in range(0, groups, 2):
                pair = jnp.concatenate([out[g * tq:(g + 1) * tq], out[(g + 1) * tq:(g + 2) * tq]], axis=1)
                lane0 = heads[g] * HEAD_DIM
                o_ref[rows, lane0:lane0 + 2 * HEAD_DIM] = pair.astype(o_ref.dtype)


def _attention(za, q_gain, k_gain, group_ones, *, batch, seq, row_block0, rope=None, cache=None):
    latent = cache is not None
    za_w = za.shape[1]
    in_specs = [
        pl.BlockSpec((seq, za_w), lambda b: (row_block0 + b, 0)),
        pl.BlockSpec((1, ATT_WIDTH), lambda b: (0, 0)),
        pl.BlockSpec((1, KV_WIDTH), lambda b: (0, 0)),
        pl.BlockSpec((ATT_WIDTH, ATT_WIDTH), lambda b: (0, 0)),
    ]
    args = [za, q_gain, k_gain, group_ones]
    o_spec = pl.BlockSpec((seq, ATT_WIDTH), lambda b: (b, 0))
    o_shape = jax.ShapeDtypeStruct((batch * seq, ATT_WIDTH), BF16)
    if latent:
        cos, sin_signed = rope
        ck, cv = cache
        past = ck.shape[1]
        in_specs += [
            pl.BlockSpec((seq, ATT_WIDTH), lambda b: (0, 0)),
            pl.BlockSpec((seq, ATT_WIDTH), lambda b: (0, 0)),
            pl.BlockSpec((1, past, KV_WIDTH), lambda b: (b, 0, 0)),
            pl.BlockSpec((1, past, KV_WIDTH), lambda b: (b, 0, 0)),
        ]
        args += [cos, sin_signed, ck, cv]
        out_specs = o_spec
        out_shape = o_shape
    else:
        kv_spec = pl.BlockSpec((seq, KV_WIDTH), lambda b: (b, 0))
        kv_shape = jax.ShapeDtypeStruct((batch * seq, KV_WIDTH), F32)
        out_specs = [o_spec, kv_spec, kv_spec]
        out_shape = [o_shape, kv_shape, kv_shape]
    return pl.pallas_call(
        functools.partial(_attn_kernel, seq=seq, latent=latent),
        grid=(batch,),
        in_specs=in_specs,
        out_specs=out_specs,
        out_shape=out_shape,
        compiler_params=_params(("arbitrary",)),
    )(*args)


def _rope_tables(seq):
    quarter = HEAD_DIM // 4
    t = jnp.arange(seq)
    row = (t // GRID_W).astype(F32)
    colp = (t % GRID_W).astype(F32)
    inv_freq = ROPE_THETA ** (-jnp.arange(quarter, dtype=F32) / quarter)
    lane = jnp.arange(HEAD_DIM)
    pos = jnp.where((lane < HEAD_DIM // 2)[None, :], row[:, None], colp[:, None])
    ang = pos * inv_freq[lane % quarter][None, :]
    sign = jnp.where(lane % (2 * quarter) < quarter, -1.0, 1.0)[None, :]
    cos = jnp.tile(jnp.cos(ang), (1, N_HEADS))
    sin_signed = jnp.tile(jnp.sin(ang) * sign, (1, N_HEADS))
    return cos, sin_signed


def _layer_norm(y, g, b):
    mu = jnp.mean(y, axis=-1, keepdims=True)
    yc = y - mu
    var = jnp.mean(yc * yc, axis=-1, keepdims=True)
    return yc * lax.rsqrt(var + NORM_EPS) * g + b


def _post_mixer_kernel(x_ref, mod_ref, oa_ref, ob_ref, zg_ref, wa_ref, wb_ref, wo_ref, g_ref, b_ref,
                       x1_ref, h2_ref, *, d):
    gate1 = mod_ref[0, :, 2 * d:3 * d]
    shift2 = mod_ref[0, :, 3 * d:4 * d]
    scale2 = mod_ref[0, :, 4 * d:5 * d]
    branch_a = jnp.dot(oa_ref[...], wa_ref[...], preferred_element_type=F32)
    branch_b = jnp.dot(ob_ref[...], wb_ref[...], preferred_element_type=F32)
    merged = zg_ref[:, 0:d] * branch_a + zg_ref[:, d:2 * d] * branch_b
    mix = jnp.dot(merged.astype(BF16), wo_ref[...], preferred_element_type=F32)
    x1 = _layer_norm(DEEPNORM_ALPHA * x_ref[...] + gate1 * mix, g_ref[...], b_ref[...])
    x1_ref[...] = x1
    h2_ref[...] = x1 * (1.0 + scale2) + shift2


def _post_mixer(x_all, mod3, o_a, o_b, zg, w_a, w_b, w_o, ln_g, ln_b, mod_map):
    t, d = x_all.shape
    tm = TOKEN_TILE
    row = lambda i: (i, 0)
    full = lambda i: (0, 0)
    return pl.pallas_call(
        functools.partial(_post_mixer_kernel, d=d),
        grid=(t // tm,),
        in_specs=[
            pl.BlockSpec((tm, d), row),
            pl.BlockSpec((1, 1, mod3.shape[2]), mod_map),
            pl.BlockSpec((tm, o_a.shape[1]), row),
            pl.BlockSpec((tm, o_b.shape[1]), row),
            pl.BlockSpec((tm, 2 * d), row),
            pl.BlockSpec(w_a.shape, full),
            pl.BlockSpec(w_b.shape, full),
            pl.BlockSpec(w_o.shape, full),
            pl.BlockSpec((1, d), full),
            pl.BlockSpec((1, d), full),
        ],
        out_specs=[pl.BlockSpec((tm, d), row), pl.BlockSpec((tm, d), row)],
        out_shape=[jax.ShapeDtypeStruct((t, d), F32), jax.ShapeDtypeStruct((t, d), F32)],
        compiler_params=_params(("arbitrary",)),
    )(x_all, mod3, o_a, o_b, zg, w_a, w_b, w_o, ln_g, ln_b)


def _first_index_of_max(x, idx, sentinel):
    m = jnp.max(x, axis=0, keepdims=True)
    first = jnp.min(jnp.where(x == m, idx, sentinel), axis=0, keepdims=True)
    return m, first


def _router_kernel(h_ref, wh_ref, wl_ref, bias_ref, idx_ref, w_ref):
    h = h_ref[...]
    hh = h.astype(BF16)
    hl = (h - hh.astype(F32)).astype(BF16)
    nt = (((1,), (1,)), ((), ()))
    wh = wh_ref[...]
    logits = (lax.dot_general(wh, hh, nt, preferred_element_type=F32)
              + lax.dot_general(wh, hl, nt, preferred_element_type=F32)
              + lax.dot_general(wl_ref[...], hh, nt, preferred_element_type=F32))
    scores = _sigmoid(logits)
    sel = scores + bias_ref[...]
    tm = sel.shape[1]
    neg = -jnp.inf
    gidx = lax.broadcasted_iota(jnp.int32, (GROUP_SIZE, tm), 0)
    group_scores = []
    for g in range(N_GROUPS):
        sg = sel[g * GROUP_SIZE:(g + 1) * GROUP_SIZE, :]
        m1, first = _first_index_of_max(sg, gidx, GROUP_SIZE)
        m2 = jnp.max(jnp.where(gidx == first, neg, sg), axis=0, keepdims=True)
        group_scores.append(m1 + m2)
    gs = jnp.concatenate(group_scores, axis=0)
    nidx = lax.broadcasted_iota(jnp.int32, (N_GROUPS, tm), 0)
    chosen = jnp.zeros((N_GROUPS, tm), jnp.bool_)
    for _ in range(TOPK_GROUPS):
        _, first = _first_index_of_max(gs, nidx, N_GROUPS)
        hit = nidx == first
        chosen = jnp.logical_or(chosen, hit)
        gs = jnp.where(hit, neg, gs)
    masked = jnp.concatenate(
        [jnp.where(chosen[g:g + 1, :], sel[g * GROUP_SIZE:(g + 1) * GROUP_SIZE, :], neg) for g in range(N_GROUPS)],
        axis=0)
    eidx = lax.broadcasted_iota(jnp.int32, (N_EXPERTS, tm), 0)
    picks, weights = [], []
    for _ in range(TOP_K):
        _, first = _first_index_of_max(masked, eidx, N_EXPERTS)
        hit = eidx == first
        picks.append(first)
        weights.append(jnp.sum(jnp.where(hit, scores, 0.0), axis=0, keepdims=True))
        masked = jnp.where(hit, neg, masked)
    wk = jnp.concatenate(weights, axis=0)
    idx_ref[...] = jnp.concatenate(picks, axis=0)
    w_ref[...] = wk / jnp.sum(wk, axis=0, keepdims=True) * ROUTED_SCALE


def _router(h2, wr_hi, wr_lo, bias_col):
    t, d = h2.shape
    tm = TOKEN_TILE
    return pl.pallas_call(
        _router_kernel,
        grid=(t // tm,),
        in_specs=[
            pl.BlockSpec((tm, d), lambda i: (i, 0)),
            pl.BlockSpec((N_EXPERTS, d), lambda i: (0, 0)),
            pl.BlockSpec((N_EXPERTS, d), lambda i: (0, 0)),
            pl.BlockSpec((N_EXPERTS, 1), lambda i: (0, 0)),
        ],
        out_specs=[pl.BlockSpec((TOP_K, tm), lambda i: (0, i)), pl.BlockSpec((TOP_K, tm), lambda i: (0, i))],
        out_shape=[jax.ShapeDtypeStruct((TOP_K, t), jnp.int32), jax.ShapeDtypeStruct((TOP_K, t), F32)],
        compiler_params=_params(("arbitrary",)),
    )(h2, wr_hi, wr_lo, bias_col)


def _dispatch_lists(idx_t, w_t):
    k, t = idx_t.shape
    s = k * t
    n_blocks = s // MOE_BLOCK + N_EXPERTS
    flat_e = idx_t.reshape(s)
    flat_w = w_t.reshape(s)
    flat_tok = jnp.arange(s, dtype=jnp.int32) % t
    order = jnp.argsort(flat_e, stable=True)
    sorted_e = flat_e[order]
    counts = jnp.zeros((N_EXPERTS,), jnp.int32).at[flat_e].add(1)
    padded = (counts + MOE_BLOCK - 1) // MOE_BLOCK * MOE_BLOCK
    start = jnp.cumsum(counts) - counts
    pad_end = jnp.cumsum(padded)
    pad_start = pad_end - padded
    dest = pad_start[sorted_e] + (jnp.arange(s, dtype=jnp.int32) - start[sorted_e])
    p = n_blocks * MOE_BLOCK
    buf_tok = jnp.full((p,), t, jnp.int32).at[dest].set(flat_tok[order])
    buf_w = jnp.zeros((p,), F32).at[dest].set(flat_w[order])
    n_used = (pad_end[-1] // MOE_BLOCK).astype(jnp.int32)
    block_start = jnp.arange(n_blocks, dtype=jnp.int32) * MOE_BLOCK
    block_e = jnp.minimum(jnp.searchsorted(pad_end, block_start, side='right'), N_EXPERTS - 1).astype(jnp.int32)
    last_e = block_e[jnp.maximum(n_used - 1, 0)]
    block_e = jnp.where(jnp.arange(n_blocks) < n_used, block_e, last_e)
    return block_e, n_used.reshape(1), buf_tok, buf_w.reshape(p, 1)


def _moe_kernel(be_ref, nu_ref, tok_ref, x_hbm, wg_ref, wu_ref, wd_ref, bw_ref, y_hbm,
                xs, ys, xt, ot, sem, *, t, n_blocks):
    b = pl.program_id(0)
    rows = t * SUBLANES
    chunks = wg_ref.shape[1] // LANES

    @pl.when(b == 0)
    def _load():
        cp = pltpu.make_async_copy(x_hbm, xs.at[pl.ds(0, rows)], sem)
        cp.start()
        cp.wait()
        xs[pl.ds(rows, SUBLANES), :] = jnp.zeros((SUBLANES, LANES), F32)
        ys[...] = jnp.zeros(ys.shape, F32)

    @pl.when(b < nu_ref[0])
    def _block():
        base = b * MOE_BLOCK
        for r in range(MOE_BLOCK):
            tok = tok_ref[base + r]
            slab = xs[pl.ds(pl.multiple_of(tok * SUBLANES, SUBLANES), SUBLANES), :]
            xt[pl.ds(r, chunks, stride=ROW_STRIDE), :] = slab
        x = jnp.concatenate(
            [xt[j * ROW_STRIDE:j * ROW_STRIDE + MOE_BLOCK, :] for j in range(chunks)], axis=1).astype(BF16)
        gate = jnp.dot(x, wg_ref[0].astype(BF16), preferred_element_type=F32)
        up = jnp.dot(x, wu_ref[0].astype(BF16), preferred_element_type=F32)
        hidden = (_silu(gate) * up).astype(BF16)
        out = jnp.dot(hidden, wd_ref[0].astype(BF16), preferred_element_type=F32) * bw_ref[...]
        for j in range(chunks):
            ot[j * ROW_STRIDE:j * ROW_STRIDE + MOE_BLOCK, :] = out[:, j * LANES:(j + 1) * LANES]
        batch = 8
        for r0 in range(0, MOE_BLOCK, batch):
            toks = [pl.multiple_of(tok_ref[base + r0 + i] * SUBLANES, SUBLANES) for i in range(batch)]
            vals = [ys[pl.ds(toks[i], SUBLANES), :] + ot[pl.ds(r0 + i, chunks, stride=ROW_STRIDE), :]
                    for i in range(batch)]
            for i in range(batch):
                ys[pl.ds(toks[i], SUBLANES), :] = vals[i]

    @pl.when(b == n_blocks - 1)
    def _store():
        cp = pltpu.make_async_copy(ys.at[pl.ds(0, rows)], y_hbm, sem)
        cp.start()
        cp.wait()


def _routed_experts(h2_rows, block_e, n_used, buf_tok, buf_w, w_gate, w_up, w_down):
    rows, lanes = h2_rows.shape
    t = rows // SUBLANES
    n_blocks = block_e.shape[0]
    d, e_dim = w_gate.shape[1], w_gate.shape[2]
    assert d == SUBLANES * LANES and lanes == LANES
    grid_spec = pltpu.PrefetchScalarGridSpec(
        num_scalar_prefetch=3,
        grid=(n_blocks,),
        in_specs=[
            pl.BlockSpec(memory_space=pl.ANY),
            pl.BlockSpec((1, d, e_dim), lambda b, be, nu, tok: (be[b], 0, 0)),
            pl.BlockSpec((1, d, e_dim), lambda b, be, nu, tok: (be[b], 0, 0)),
            pl.BlockSpec((1, e_dim, d), lambda b, be, nu, tok: (be[b], 0, 0)),
            pl.BlockSpec((MOE_BLOCK, 1), lambda b, be, nu, tok: (b, 0)),
        ],
        out_specs=pl.BlockSpec(memory_space=pl.ANY),
        scratch_shapes=[
            pltpu.VMEM((rows + SUBLANES, LANES), F32),
            pltpu.VMEM((rows + SUBLANES, LANES), F32),
            pltpu.VMEM((SUBLANES * ROW_STRIDE, LANES), F32),
            pltpu.VMEM((SUBLANES * ROW_STRIDE, LANES), F32),
            pltpu.SemaphoreType.DMA(()),
        ],
    )
    return pl.pallas_call(
        functools.partial(_moe_kernel, t=t, n_blocks=n_blocks),
        grid_spec=grid_spec,
        out_shape=jax.ShapeDtypeStruct((rows, LANES), F32),
        compiler_params=_params(("arbitrary",)),
    )(block_e, n_used, buf_tok, h2_rows, w_gate, w_up, w_down, buf_w)


def _final_kernel(x1_ref, h2_ref, routed_ref, mod_ref, wg_ref, wu_ref, wd_ref, g_ref, b_ref, y_ref, *, d):
    gate2 = mod_ref[0, :, 5 * d:6 * d]
    h = h2_ref[...].astype(BF16)
    gate = jnp.dot(h, wg_ref[...], preferred_element_type=F32)
    up = jnp.dot(h, wu_ref[...], preferred_element_type=F32)
    shared = jnp.dot((_silu(gate) * up).astype(BF16), wd_ref[...], preferred_element_type=F32)
    ffn = routed_ref[...] + shared
    y_ref[...] = _layer_norm(DEEPNORM_ALPHA * x1_ref[...] + gate2 * ffn, g_ref[...], b_ref[...])


def _final(x1, h2, routed, mod3, w_g, w_u, w_d, ln_g, ln_b, mod_map):
    t, d = x1.shape
    tm = TOKEN_TILE
    row = lambda i: (i, 0)
    full = lambda i: (0, 0)
    return pl.pallas_call(
        functools.partial(_final_kernel, d=d),
        grid=(t // tm,),
        in_specs=[
            pl.BlockSpec((tm, d), row),
            pl.BlockSpec((tm, d), row),
            pl.BlockSpec((tm, d), row),
            pl.BlockSpec((1, 1, mod3.shape[2]), mod_map),
            pl.BlockSpec(w_g.shape, full),
            pl.BlockSpec(w_u.shape, full),
            pl.BlockSpec(w_d.shape, full),
            pl.BlockSpec((1, d), full),
            pl.BlockSpec((1, d), full),
        ],
        out_specs=pl.BlockSpec((tm, d), row),
        out_shape=jax.ShapeDtypeStruct((t, d), F32),
        compiler_params=_params(("arbitrary",)),
    )(x1, h2, routed, mod3, w_g, w_u, w_d, ln_g, ln_b)


def kernel(x_prompt, x_sample, cache_k, cache_v, state_hgrn, c, c_ctx, w_mod, b_mod, w_in, hg_lb, hg_norm_g, q_norm_g, k_norm_g, w_branch_a, w_branch_b, w_out, ln1_g, ln1_b, w_router, router_bias, w_e_gate, w_e_up, w_e_down, w_s_gate, w_s_up, w_s_down, ln2_g, ln2_b):
    assert w_mod.shape[0] == DEPTH
    n_ctx, seq_ctx, d = x_prompt.shape
    n_lat, seq_lat, _ = x_sample.shape
    t_ctx = n_ctx * seq_ctx
    t_lat = n_lat * seq_lat
    assert seq_ctx == TOKEN_TILE and seq_lat % TOKEN_TILE == 0
    layer = 0

    lb = jnp.cumsum(jax.nn.softmax(hg_lb.astype(F32), axis=0), axis=0)[layer]

    cond = jnp.concatenate([c_ctx[None, :], c], axis=0)
    cond = jnp.pad(cond, ((0, (-cond.shape[0]) % SUBLANES), (0, 0)))
    mod = _modulation(cond, w_mod[layer].astype(BF16), b_mod[layer][None, :])
    mod3 = mod.reshape(mod.shape[0], 1, mod.shape[1])
    mod_map = _mod_row_map(t_ctx // TOKEN_TILE, seq_lat // TOKEN_TILE)

    x_all = jnp.concatenate([x_prompt.reshape(t_ctx, d), x_sample.reshape(t_lat, d)], axis=0)
    zh, za, zg = _input_projection(x_all, mod3, w_in[layer].astype(BF16), lb, mod_map)

    norm_g = hg_norm_g[layer][None, :]
    oa_ctx, s_ctx = _hgrn_scan(zh, norm_g, None, batch=n_ctx, seq=seq_ctx, row_block0=0)
    oa_lat, _ = _hgrn_scan(zh, norm_g, state_hgrn[:, layer], batch=n_lat, seq=seq_lat,
                           row_block0=t_ctx // seq_lat)

    q_gain = jnp.tile(q_norm_g[layer], N_HEADS)[None, :]
    k_gain = jnp.tile(k_norm_g[layer], N_KV_HEADS)[None, :]
    lane = jnp.arange(ATT_WIDTH)
    group_ones = (lane[:, None] // HEAD_DIM == lane[None, :] // HEAD_DIM).astype(BF16)
    ob_ctx, k_ctx, v_ctx = _attention(za, q_gain, k_gain, group_ones, batch=n_ctx, seq=seq_ctx, row_block0=0)
    past = cache_k.shape[2]
    ob_lat = _attention(
        za, q_gain, k_gain, group_ones, batch=n_lat, seq=seq_lat, row_block0=t_ctx // seq_lat,
        rope=_rope_tables(seq_lat),
        cache=(cache_k[:, layer].reshape(n_lat, past, KV_WIDTH), cache_v[:, layer].reshape(n_lat, past, KV_WIDTH)))

    o_a = jnp.concatenate([oa_ctx, oa_lat], axis=0)
    o_b = jnp.concatenate([ob_ctx, ob_lat], axis=0)
    x1, h2 = _post_mixer(x_all, mod3, o_a, o_b, zg, w_branch_a[layer].astype(BF16),
                         w_branch_b[layer].astype(BF16), w_out[layer].astype(BF16),
                         ln1_g[layer][None, :], ln1_b[layer][None, :], mod_map)

    wr_t = w_router[layer].T
    wr_hi = wr_t.astype(BF16)
    wr_lo = (wr_t - wr_hi.astype(F32)).astype(BF16)
    idx_t, w_t = _router(h2, wr_hi, wr_lo, router_bias[layer][:, None])

    routed = []
    for lo, hi in ((0, t_ctx), (t_ctx, t_ctx + t_lat)):
        lists = _dispatch_lists(idx_t[:, lo:hi], w_t[:, lo:hi])
        rows = h2[lo:hi].reshape((hi - lo) * SUBLANES, LANES)
        y = _routed_experts(rows, *lists, w_e_gate[layer], w_e_up[layer], w_e_down[layer])
        routed.append(y.reshape(hi - lo, d))
    routed = jnp.concatenate(routed, axis=0)

    y_all = _final(x1, h2, routed, mod3, w_s_gate[layer].astype(BF16), w_s_up[layer].astype(BF16),
                   w_s_down[layer].astype(BF16), ln2_g[layer][None, :], ln2_b[layer][None, :], mod_map)

    y_prompt = y_all[:t_ctx].reshape(n_ctx, seq_ctx, d)
    y_sample = y_all[t_ctx:].reshape(n_lat, seq_lat, d)
    new_cache_k = k_ctx.reshape(n_ctx, 1, seq_ctx, N_KV_HEADS, HEAD_DIM)
    new_cache_v = v_ctx.reshape(n_ctx, 1, seq_ctx, N_KV_HEADS, HEAD_DIM)
    new_state = s_ctx[:, None]
    return (y_prompt, y_sample, new_cache_k, new_cache_v, new_state)
```

```python
import functools

import jax
import jax.numpy as jnp
from jax import lax
from jax.experimental import pallas as pl
from jax.experimental.pallas import tpu as pltpu

F32 = jnp.float32
BF16 = jnp.bfloat16

GRID_W = 64
HG_HEADS = 4
HG_DK = 128
HG_DV = 128
HG_WIDTH = HG_HEADS * HG_DK
N_HEADS = 8
N_KV_HEADS = 2
HEAD_DIM = 64
ATT_WIDTH = N_HEADS * HEAD_DIM
KV_WIDTH = N_KV_HEADS * HEAD_DIM
ROPE_THETA = 10000.0
N_EXPERTS = 256
TOP_K = 8
N_GROUPS = 8
TOPK_GROUPS = 4
GROUP_SIZE = N_EXPERTS // N_GROUPS
ROUTED_SCALE = 2.5
NORM_EPS = 1e-6
DEPTH = 1
DEEPNORM_ALPHA = (2 * DEPTH) ** 0.25

LANES = 128
SUBLANES = 8
VMEM_LIMIT = 56 * 1024 * 1024

TOKEN_TILE = 256
HG_CHUNK = 32
ATT_Q_BLOCK = 128
MOE_BLOCK = 128
ROW_STRIDE = MOE_BLOCK + 1


def _sigmoid(x):
    return 1.0 / (1.0 + jnp.exp(-x))


def _silu(x):
    return x * _sigmoid(x)


def _params(sem=None):
    return pltpu.CompilerParams(dimension_semantics=sem, vmem_limit_bytes=VMEM_LIMIT)


def _mod_kernel(c_ref, w_ref, b_ref, o_ref):
    s = _silu(c_ref[...]).astype(BF16)
    o_ref[...] = jnp.dot(s, w_ref[...], preferred_element_type=F32) + b_ref[...]


def _modulation(cond, w_mod, b_mod):
    n, d = cond.shape
    width = w_mod.shape[1]
    tn = width // 4
    return pl.pallas_call(
        _mod_kernel,
        grid=(4,),
        in_specs=[
            pl.BlockSpec((n, d), lambda j: (0, 0)),
            pl.BlockSpec((d, tn), lambda j: (0, j)),
            pl.BlockSpec((1, tn), lambda j: (0, j)),
        ],
        out_specs=pl.BlockSpec((n, tn), lambda j: (0, j)),
        out_shape=jax.ShapeDtypeStruct((n, width), F32),
        compiler_params=_params(("arbitrary",)),
    )(cond, w_mod, b_mod)


def _inproj_kernel(x_ref, mod_ref, w_ref, lb_ref, zh_ref, za_ref, zg_ref, *, d):
    shift = mod_ref[0, :, 0:d]
    scale = mod_ref[0, :, d:2 * d]
    h = (x_ref[...] * (1.0 + scale) + shift).astype(BF16)

    def proj(lo, hi):
        return jnp.dot(h, w_ref[:, lo:hi], preferred_element_type=F32)

    w = HG_WIDTH
    zh_ref[:, 0:w] = _silu(proj(0, w))
    zh_ref[:, w:2 * w] = proj(w, 2 * w)
    for i in range(2):
        lb = lb_ref[i:i + 1, :]
        zh_ref[:, (2 + i) * w:(3 + i) * w] = lb + (1.0 - lb) * _sigmoid(proj((2 + i) * w, (3 + i) * w))
    zh_ref[:, 4 * w:5 * w] = _silu(proj(4 * w, 5 * w))
    a0 = 5 * w
    a1 = a0 + ATT_WIDTH + 2 * KV_WIDTH
    za_ref[...] = proj(a0, a1)
    for i in range(4):
        lo = a1 + i * (d // 2)
        zg_ref[:, i * (d // 2):(i + 1) * (d // 2)] = _sigmoid(proj(lo, lo + d // 2))


def _mod_row_map(n_ctx_tiles, tiles_per_latent):
    def index_map(i):
        row = jnp.where(i < n_ctx_tiles, 0, 1 + (i - n_ctx_tiles) // tiles_per_latent)
        return (row, 0, 0)
    return index_map


def _input_projection(x_all, mod3, w_in, lb, mod_map):
    t, d = x_all.shape
    width = w_in.shape[1]
    zh_w = 5 * HG_WIDTH
    za_w = ATT_WIDTH + 2 * KV_WIDTH
    zg_w = 2 * d
    assert width == zh_w + za_w + zg_w
    tm = TOKEN_TILE
    return pl.pallas_call(
        functools.partial(_inproj_kernel, d=d),
        grid=(t // tm,),
        in_specs=[
            pl.BlockSpec((tm, d), lambda i: (i, 0)),
            pl.BlockSpec((1, 1, mod3.shape[2]), mod_map),
            pl.BlockSpec((d, width), lambda i: (0, 0)),
            pl.BlockSpec((2, HG_WIDTH), lambda i: (0, 0)),
        ],
        out_specs=[
            pl.BlockSpec((tm, zh_w), lambda i: (i, 0)),
            pl.BlockSpec((tm, za_w), lambda i: (i, 0)),
            pl.BlockSpec((tm, zg_w), lambda i: (i, 0)),
        ],
        out_shape=[
            jax.ShapeDtypeStruct((t, zh_w), F32),
            jax.ShapeDtypeStruct((t, za_w), F32),
            jax.ShapeDtypeStruct((t, zg_w), F32),
        ],
        compiler_params=_params(("arbitrary",)),
    )(x_all, mod3, w_in, lb)


def _hgrn_kernel(*refs, seq, has_state):
    if has_state:
        q_ref, v_ref, ff_ref, fb_ref, gs_ref, ng_ref, s0_ref, o_ref, sout_ref = refs
    else:
        q_ref, v_ref, ff_ref, fb_ref, gs_ref, ng_ref, o_ref, sout_ref = refs
        s0_ref = None
    c = HG_CHUNK
    n = seq // c
    q3 = q_ref[...].reshape(n, c, HG_DK)
    v3 = v_ref[...].astype(BF16).reshape(n, c, HG_DV)
    pos = lax.broadcasted_iota(jnp.int32, (seq, HG_DK), 0) % c
    t_idx = lax.broadcasted_iota(jnp.int32, (c, c), 0)
    s_idx = lax.broadcasted_iota(jnp.int32, (c, c), 1)
    o_sum = None
    for direction, f_ref in enumerate((ff_ref, fb_ref)):
        reverse = direction == 1
        f = f_ref[...]
        k3 = (1.0 - f).reshape(n, c, HG_DK)
        b = jnp.log(f)
        step = 1
        while step < c:
            if reverse:
                b = b + jnp.where(pos < c - step, pltpu.roll(b, seq - step, axis=0), 0.0)
            else:
                b = b + jnp.where(pos >= step, pltpu.roll(b, step, axis=0), 0.0)
            step *= 2
        b3 = b.reshape(n, c, HG_DK)
        edge = b3[:, 0:1, :] if reverse else b3[:, c - 1:c, :]
        mid = b3[:, c // 2:c // 2 + 1, :]
        q_in = (q3 * jnp.exp(b3)).astype(BF16)
        k_end = (k3 * jnp.exp(edge - b3)).astype(BF16)
        q_mid = (q3 * jnp.exp(b3 - mid)).astype(BF16)
        k_mid = (k3 * jnp.exp(mid - b3)).astype(BF16)
        scores = jnp.einsum('ntd,nsd->nts', q_mid, k_mid, preferred_element_type=F32)
        keep = (s_idx >= t_idx) if reverse else (s_idx <= t_idx)
        scores = jnp.where(keep[None], scores, 0.0).astype(BF16)
        o_intra = jnp.einsum('nts,nsv->ntv', scores, v3, preferred_element_type=F32)
        upd = jnp.einsum('nsv,nsd->nvd', v3, k_end, preferred_element_type=F32)
        dec = jnp.exp(edge)
        if has_state:
            st = s0_ref[0, direction, 0].T
        else:
            st = jnp.zeros((HG_DV, HG_DK), F32)
        before = [None] * n
        for ci in (range(n - 1, -1, -1) if reverse else range(n)):
            before[ci] = st.astype(BF16)
            st = st * dec[ci] + upd[ci]
        st_before = jnp.stack(before, axis=0)
        o_inter = jnp.einsum('ntd,nvd->ntv', q_in, st_before, preferred_element_type=F32)
        o_dir = (o_intra + o_inter).reshape(seq, HG_DV)
        o_sum = o_dir if o_sum is None else o_sum + o_dir
        sout_ref[0, direction, 0] = st.T
    ms = jnp.mean(o_sum * o_sum, axis=-1, keepdims=True)
    o = o_sum * lax.rsqrt(ms + NORM_EPS) * ng_ref[...]
    o_ref[...] = (o * gs_ref[...]).astype(o_ref.dtype)


def _hgrn_scan(zh, norm_g, s0, *, batch, seq, row_block0):
    has_state = s0 is not None
    h = HG_HEADS

    def col(section):
        return pl.BlockSpec((seq, HG_DK), lambda b, j: (row_block0 + b, section * h + j))

    in_specs = [col(0), col(1), col(2), col(3), col(4), pl.BlockSpec((1, HG_DV), lambda b, j: (0, 0))]
    args = [zh, zh, zh, zh, zh, norm_g]
    state_spec = pl.BlockSpec((1, 2, 1, HG_DK, HG_DV), lambda b, j: (b, 0, j, 0, 0))
    if has_state:
        in_specs.append(state_spec)
        args.append(s0)
    return pl.pallas_call(
        functools.partial(_hgrn_kernel, seq=seq, has_state=has_state),
        grid=(batch, h),
        in_specs=in_specs,
        out_specs=[pl.BlockSpec((seq, HG_DV), lambda b, j: (b, j)), state_spec],
        out_shape=[
            jax.ShapeDtypeStruct((batch * seq, h * HG_DV), BF16),
            jax.ShapeDtypeStruct((batch, 2, h, HG_DK, HG_DV), F32),
        ],
        compiler_params=_params(("arbitrary", "arbitrary")),
    )(*args)


def _group_rms_norm(x, group_ones, gain):
    sq = x * x
    hi = sq.astype(BF16)
    lo = (sq - hi.astype(F32)).astype(BF16)
    total = (jnp.dot(hi, group_ones, preferred_element_type=F32)
             + jnp.dot(lo, group_ones, preferred_element_type=F32))
    return x * lax.rsqrt(total * (1.0 / HEAD_DIM) + NORM_EPS) * gain


def _rope(x, cos, sin_signed):
    width = x.shape[1]
    quarter = HEAD_DIM // 4
    lane = lax.broadcasted_iota(jnp.int32, x.shape, 1)
    partner = jnp.where(lane % (2 * quarter) < quarter,
                        pltpu.roll(x, width - quarter, axis=1),
                        pltpu.roll(x, quarter, axis=1))
    return x * cos + partner * sin_signed


def _attn_kernel(*refs, seq, latent):
    if latent:
        za_ref, qg_ref, kg_ref, gm_ref, cos_ref, sin_ref, ck_ref, cv_ref, o_ref = refs
    else:
        za_ref, qg_ref, kg_ref, gm_ref, o_ref, kout_ref, vout_ref = refs
    q = za_ref[:, 0:ATT_WIDTH]
    k = za_ref[:, ATT_WIDTH:ATT_WIDTH + KV_WIDTH]
    v = za_ref[:, ATT_WIDTH + KV_WIDTH:ATT_WIDTH + 2 * KV_WIDTH]
    qn = _group_rms_norm(q, gm_ref[...], qg_ref[...])
    kn = _group_rms_norm(k, gm_ref[0:KV_WIDTH, 0:KV_WIDTH], kg_ref[...])
    if latent:
        qn = _rope(qn, cos_ref[...], sin_ref[...])
        kr = _rope(kn, cos_ref[:, 0:KV_WIDTH], sin_ref[:, 0:KV_WIDTH])
    else:
        kout_ref[...] = kn
        vout_ref[...] = v
        kr = kn
    qb = (qn * (HEAD_DIM ** -0.5)).astype(BF16)
    kb = kr.astype(BF16)
    vb = v.astype(BF16)
    if latent:
        ckb = ck_ref[0].astype(BF16)
        cvb = cv_ref[0].astype(BF16)
    groups = N_HEADS // N_KV_HEADS
    tq = ATT_Q_BLOCK if latent else seq
    nt = (((1,), (1,)), ((), ()))
    for kh in range(N_KV_HEADS):
        ksl = slice(kh * HEAD_DIM, (kh + 1) * HEAD_DIM)
        k_new = kb[:, ksl]
        v_new = vb[:, ksl]
        for blk in range(seq // tq):
            rows = slice(blk * tq, (blk + 1) * tq)
            heads = [kh * groups + g for g in range(groups)]
            q_st = jnp.concatenate([qb[rows, hd * HEAD_DIM:(hd + 1) * HEAD_DIM] for hd in heads], axis=0)
            s_new = lax.dot_general(q_st, k_new, nt, preferred_element_type=F32)
            m = jnp.max(s_new, axis=-1, keepdims=True)
            if latent:
                s_ctx = lax.dot_general(q_st, ckb[:, ksl], nt, preferred_element_type=F32)
                m = jnp.maximum(m, jnp.max(s_ctx, axis=-1, keepdims=True))
            p_new = jnp.exp(s_new - m)
            denom = jnp.sum(p_new, axis=-1, keepdims=True)
            acc = jnp.dot(p_new.astype(BF16), v_new, preferred_element_type=F32)
            if latent:
                p_ctx = jnp.exp(s_ctx - m)
                denom = denom + jnp.sum(p_ctx, axis=-1, keepdims=True)
                acc = acc + jnp.dot(p_ctx.astype(BF16), cvb[:, ksl], preferred_element_type=F32)
            out = acc / denom
            for g in range(0, groups, 2):
                pair = jnp.concatenate([out[g * tq:(g + 1) * tq], out[(g + 1) * tq:(g + 2) * tq]], axis=1)
                lane0 = heads[g] * HEAD_DIM
                o_ref[rows, lane0:lane0 + 2 * HEAD_DIM] = pair.astype(o_ref.dtype)


def _attention(za, q_gain, k_gain, group_ones, *, batch, seq, row_block0, rope=None, cache=None):
    latent = cache is not None
    za_w = za.shape[1]
    in_specs = [
        pl.BlockSpec((seq, za_w), lambda b: (row_block0 + b, 0)),
        pl.BlockSpec((1, ATT_WIDTH), lambda b: (0, 0)),
        pl.BlockSpec((1, KV_WIDTH), lambda b: (0, 0)),
        pl.BlockSpec((ATT_WIDTH, ATT_WIDTH), lambda b: (0, 0)),
    ]
    args = [za, q_gain, k_gain, group_ones]
    o_spec = pl.BlockSpec((seq, ATT_WIDTH), lambda b: (b, 0))
    o_shape = jax.ShapeDtypeStruct((batch * seq, ATT_WIDTH), BF16)
    if latent:
        cos, sin_signed = rope
        ck, cv = cache
        past = ck.shape[1]
        in_specs += [
            pl.BlockSpec((seq, ATT_WIDTH), lambda b: (0, 0)),
            pl.BlockSpec((seq, ATT_WIDTH), lambda b: (0, 0)),
            pl.BlockSpec((1, past, KV_WIDTH), lambda b: (b, 0, 0)),
            pl.BlockSpec((1, past, KV_WIDTH), lambda b: (b, 0, 0)),
        ]
        args += [cos, sin_signed, ck, cv]
        out_specs = o_spec
        out_shape = o_shape
    else:
        kv_spec = pl.BlockSpec((seq, KV_WIDTH), lambda b: (b, 0))
        kv_shape = jax.ShapeDtypeStruct((batch * seq, KV_WIDTH), F32)
        out_specs = [o_spec, kv_spec, kv_spec]
        out_shape = [o_shape, kv_shape, kv_shape]
    return pl.pallas_call(
        functools.partial(_attn_kernel, seq=seq, latent=latent),
        grid=(batch,),
        in_specs=in_specs,
        out_specs=out_specs,
        out_shape=out_shape,
        compiler_params=_params(("arbitrary",)),
    )(*args)


def _rope_tables(seq):
    quarter = HEAD_DIM // 4
    t = jnp.arange(seq)
    row = (t // GRID_W).astype(F32)
    colp = (t % GRID_W).astype(F32)
    inv_freq = ROPE_THETA ** (-jnp.arange(quarter, dtype=F32) / quarter)
    lane = jnp.arange(HEAD_DIM)
    pos = jnp.where((lane < HEAD_DIM // 2)[None, :], row[:, None], colp[:, None])
    ang = pos * inv_freq[lane % quarter][None, :]
    sign = jnp.where(lane % (2 * quarter) < quarter, -1.0, 1.0)[None, :]
    cos = jnp.tile(jnp.cos(ang), (1, N_HEADS))
    sin_signed = jnp.tile(jnp.sin(ang) * sign, (1, N_HEADS))
    return cos, sin_signed


def _layer_norm(y, g, b):
    mu = jnp.mean(y, axis=-1, keepdims=True)
    yc = y - mu
    var = jnp.mean(yc * yc, axis=-1, keepdims=True)
    return yc * lax.rsqrt(var + NORM_EPS) * g + b


def _post_mixer_kernel(x_ref, mod_ref, oa_ref, ob_ref, zg_ref, wa_ref, wb_ref, wo_ref, g_ref, b_ref,
                       x1_ref, h2_ref, *, d):
    gate1 = mod_ref[0, :, 2 * d:3 * d]
    shift2 = mod_ref[0, :, 3 * d:4 * d]
    scale2 = mod_ref[0, :, 4 * d:5 * d]
    branch_a = jnp.dot(oa_ref[...], wa_ref[...], preferred_element_type=F32)
    branch_b = jnp.dot(ob_ref[...], wb_ref[...], preferred_element_type=F32)
    merged = zg_ref[:, 0:d] * branch_a + zg_ref[:, d:2 * d] * branch_b
    mix = jnp.dot(merged.astype(BF16), wo_ref[...], preferred_element_type=F32)
    x1 = _layer_norm(DEEPNORM_ALPHA * x_ref[...] + gate1 * mix, g_ref[...], b_ref[...])
    x1_ref[...] = x1
    h2_ref[...] = x1 * (1.0 + scale2) + shift2


def _post_mixer(x_all, mod3, o_a, o_b, zg, w_a, w_b, w_o, ln_g, ln_b, mod_map):
    t, d = x_all.shape
    tm = TOKEN_TILE
    row = lambda i: (i, 0)
    full = lambda i: (0, 0)
    return pl.pallas_call(
        functools.partial(_post_mixer_kernel, d=d),
        grid=(t // tm,),
        in_specs=[
            pl.BlockSpec((tm, d), row),
            pl.BlockSpec((1, 1, mod3.shape[2]), mod_map),
            pl.BlockSpec((tm, o_a.shape[1]), row),
            pl.BlockSpec((tm, o_b.shape[1]), row),
            pl.BlockSpec((tm, 2 * d), row),
            pl.BlockSpec(w_a.shape, full),
            pl.BlockSpec(w_b.shape, full),
            pl.BlockSpec(w_o.shape, full),
            pl.BlockSpec((1, d), full),
            pl.BlockSpec((1, d), full),
        ],
        out_specs=[pl.BlockSpec((tm, d), row), pl.BlockSpec((tm, d), row)],
        out_shape=[jax.ShapeDtypeStruct((t, d), F32), jax.ShapeDtypeStruct((t, d), F32)],
        compiler_params=_params(("arbitrary",)),
    )(x_all, mod3, o_a, o_b, zg, w_a, w_b, w_o, ln_g, ln_b)


def _first_index_of_max(x, idx, sentinel):
    m = jnp.max(x, axis=0, keepdims=True)
    first = jnp.min(jnp.where(x == m, idx, sentinel), axis=0, keepdims=True)
    return m, first


def _router_kernel(h_ref, wh_ref, wl_ref, bias_ref, idx_ref, w_ref):
    h = h_ref[...]
    hh = h.astype(BF16)
    hl = (h - hh.astype(F32)).astype(BF16)
    nt = (((1,), (1,)), ((), ()))
    wh = wh_ref[...]
    logits = (lax.dot_general(wh, hh, nt, preferred_element_type=F32)
              + lax.dot_general(wh, hl, nt, preferred_element_type=F32)
              + lax.dot_general(wl_ref[...], hh, nt, preferred_element_type=F32))
    scores = _sigmoid(logits)
    sel = scores + bias_ref[...]
    tm = sel.shape[1]
    neg = -jnp.inf
    gidx = lax.broadcasted_iota(jnp.int32, (GROUP_SIZE, tm), 0)
    group_scores = []
    for g in range(N_GROUPS):
        sg = sel[g * GROUP_SIZE:(g + 1) * GROUP_SIZE, :]
        m1, first = _first_index_of_max(sg, gidx, GROUP_SIZE)
        m2 = jnp.max(jnp.where(gidx == first, neg, sg), axis=0, keepdims=True)
        group_scores.append(m1 + m2)
    gs = jnp.concatenate(group_scores, axis=0)
    nidx = lax.broadcasted_iota(jnp.int32, (N_GROUPS, tm), 0)
    chosen = jnp.zeros((N_GROUPS, tm), jnp.bool_)
    for _ in range(TOPK_GROUPS):
        _, first = _first_index_of_max(gs, nidx, N_GROUPS)
        hit = nidx == first
        chosen = jnp.logical_or(chosen, hit)
        gs = jnp.where(hit, neg, gs)
    masked = jnp.concatenate(
        [jnp.where(chosen[g:g + 1, :], sel[g * GROUP_SIZE:(g + 1) * GROUP_SIZE, :], neg) for g in range(N_GROUPS)],
        axis=0)
    eidx = lax.broadcasted_iota(jnp.int32, (N_EXPERTS, tm), 0)
    picks, weights = [], []
    for _ in range(TOP_K):
        _, first = _first_index_of_max(masked, eidx, N_EXPERTS)
        hit = eidx == first
        picks.append(first)
        weights.append(jnp.sum(jnp.where(hit, scores, 0.0), axis=0, keepdims=True))
        masked = jnp.where(hit, neg, masked)
    wk = jnp.concatenate(weights, axis=0)
    idx_ref[...] = jnp.concatenate(picks, axis=0)
    w_ref[...] = wk / jnp.sum(wk, axis=0, keepdims=True) * ROUTED_SCALE


def _router(h2, wr_hi, wr_lo, bias_col):
    t, d = h2.shape
    tm = TOKEN_TILE
    return pl.pallas_call(
        _router_kernel,
        grid=(t // tm,),
        in_specs=[
            pl.BlockSpec((tm, d), lambda i: (i, 0)),
            pl.BlockSpec((N_EXPERTS, d), lambda i: (0, 0)),
            pl.BlockSpec((N_EXPERTS, d), lambda i: (0, 0)),
            pl.BlockSpec((N_EXPERTS, 1), lambda i: (0, 0)),
        ],
        out_specs=[pl.BlockSpec((TOP_K, tm), lambda i: (0, i)), pl.BlockSpec((TOP_K, tm), lambda i: (0, i))],
        out_shape=[jax.ShapeDtypeStruct((TOP_K, t), jnp.int32), jax.ShapeDtypeStruct((TOP_K, t), F32)],
        compiler_params=_params(("arbitrary",)),
    )(h2, wr_hi, wr_lo, bias_col)


def _dispatch_lists(idx_t, w_t):
    k, t = idx_t.shape
    s = k * t
    n_blocks = s // MOE_BLOCK + N_EXPERTS
    flat_tok = jnp.arange(s, dtype=jnp.int32) % t
    sorted_e, sorted_tok, sorted_w = lax.sort((idx_t.reshape(s), flat_tok, w_t.reshape(s)), num_keys=1)
    experts = jnp.arange(N_EXPERTS + 1, dtype=jnp.int32)
    start = jnp.sum(sorted_e[None, :] < experts[:, None], axis=1, dtype=jnp.int32)
    counts = start[1:] - start[:-1]
    blocks_per_e = (counts + MOE_BLOCK - 1) // MOE_BLOCK
    block_end = jnp.cumsum(blocks_per_e)
    n_used = block_end[-1]
    blk = jnp.arange(n_blocks, dtype=jnp.int32)
    block_e = jnp.sum(block_end[None, :] <= blk[:, None], axis=1, dtype=jnp.int32)
    block_e = jnp.minimum(block_e, N_EXPERTS - 1)
    onehot = (block_e[:, None] == experts[None, :N_EXPERTS]).astype(jnp.int32)
    pick = lambda a: jnp.sum(onehot * a[None, :], axis=1)
    offset = (blk - (pick(block_end) - pick(blocks_per_e))) * MOE_BLOCK
    row_start = pick(start[:-1]) + offset
    n_rows = jnp.clip(pick(counts) - offset, 0, MOE_BLOCK)
    used = blk < n_used
    last_e = jnp.sum(jnp.where(blk == n_used - 1, block_e, 0))
    block_e = jnp.where(used, block_e, last_e)
    row_start = jnp.where(used, row_start, 0)
    n_rows = jnp.where(used, n_rows, 0)
    pad = jnp.full((MOE_BLOCK,), t, jnp.int32)
    sorted_tok = jnp.concatenate([sorted_tok, pad])
    sorted_w = jnp.concatenate([sorted_w, jnp.zeros((MOE_BLOCK,), F32)])
    return block_e, row_start, n_rows, sorted_tok, sorted_w


def _moe_kernel(be_ref, rs_ref, nr_ref, tok_ref, w_ref, x_hbm, wg_ref, wu_ref, wd_ref, y_hbm,
                xs, ys, xt, ot, sem, *, t, n_blocks):
    b = pl.program_id(0)
    rows = t * SUBLANES
    chunks = wg_ref.shape[1] // LANES
    n_rows = nr_ref[b]

    @pl.when(b == 0)
    def _load():
        cp = pltpu.make_async_copy(x_hbm, xs.at[pl.ds(0, rows)], sem)
        cp.start()
        cp.wait()
        xs[pl.ds(rows, SUBLANES), :] = jnp.zeros((SUBLANES, LANES), F32)
        ys[...] = jnp.zeros(ys.shape, F32)

    @pl.when(n_rows > 0)
    def _block():
        base = rs_ref[b]

        def token_row(r):
            tok = jnp.where(r < n_rows, tok_ref[base + r], t)
            return pl.multiple_of(tok * SUBLANES, SUBLANES)

        for r in range(MOE_BLOCK):
            slab = xs[pl.ds(token_row(r), SUBLANES), :]
            xt[pl.ds(r, chunks, stride=ROW_STRIDE), :] = slab
        x = jnp.concatenate(
            [xt[j * ROW_STRIDE:j * ROW_STRIDE + MOE_BLOCK, :] for j in range(chunks)], axis=1).astype(BF16)
        gate = jnp.dot(x, wg_ref[0].astype(BF16), preferred_element_type=F32)
        up = jnp.dot(x, wu_ref[0].astype(BF16), preferred_element_type=F32)
        hidden = (_silu(gate) * up).astype(BF16)
        out = jnp.dot(hidden, wd_ref[0].astype(BF16), preferred_element_type=F32)
        for j in range(chunks):
            ot[j * ROW_STRIDE:j * ROW_STRIDE + MOE_BLOCK, :] = out[:, j * LANES:(j + 1) * LANES]
        batch = 8
        for r0 in range(0, MOE_BLOCK, batch):
            toks = [token_row(r0 + i) for i in range(batch)]
            wts = [jnp.where(r0 + i < n_rows, w_ref[base + r0 + i], 0.0) for i in range(batch)]
            vals = [ys[pl.ds(toks[i], SUBLANES), :] + wts[i] * ot[pl.ds(r0 + i, chunks, stride=ROW_STRIDE), :]
                    for i in range(batch)]
            for i in range(batch):
                ys[pl.ds(toks[i], SUBLANES), :] = vals[i]

    @pl.when(b == n_blocks - 1)
    def _store():
        cp = pltpu.make_async_copy(ys.at[pl.ds(0, rows)], y_hbm, sem)
        cp.start()
        cp.wait()


def _routed_experts(h2_rows, block_e, row_start, n_rows, sorted_tok, sorted_w, w_gate, w_up, w_down):
    rows, lanes = h2_rows.shape
    t = rows // SUBLANES
    n_blocks = block_e.shape[0]
    d, e_dim = w_gate.shape[1], w_gate.shape[2]
    assert d == SUBLANES * LANES and lanes == LANES
    grid_spec = pltpu.PrefetchScalarGridSpec(
        num_scalar_prefetch=5,
        grid=(n_blocks,),
        in_specs=[
            pl.BlockSpec(memory_space=pl.ANY),
            pl.BlockSpec((1, d, e_dim), lambda b, be, *_: (be[b], 0, 0)),
            pl.BlockSpec((1, d, e_dim), lambda b, be, *_: (be[b], 0, 0)),
            pl.BlockSpec((1, e_dim, d), lambda b, be, *_: (be[b], 0, 0)),
        ],
        out_specs=pl.BlockSpec(memory_space=pl.ANY),
        scratch_shapes=[
            pltpu.VMEM((rows + SUBLANES, LANES), F32),
            pltpu.VMEM((rows + SUBLANES, LANES), F32),
            pltpu.VMEM((SUBLANES * ROW_STRIDE, LANES), F32),
            pltpu.VMEM((SUBLANES * ROW_STRIDE, LANES), F32),
            pltpu.SemaphoreType.DMA(()),
        ],
    )
    return pl.pallas_call(
        functools.partial(_moe_kernel, t=t, n_blocks=n_blocks),
        grid_spec=grid_spec,
        out_shape=jax.ShapeDtypeStruct((rows, LANES), F32),
        compiler_params=_params(("arbitrary",)),
    )(block_e, row_start, n_rows, sorted_tok, sorted_w, h2_rows, w_gate, w_up, w_down)


def _final_kernel(x1_ref, h2_ref, routed_ref, mod_ref, wg_ref, wu_ref, wd_ref, g_ref, b_ref, y_ref, *, d):
    gate2 = mod_ref[0, :, 5 * d:6 * d]
    h = h2_ref[...].astype(BF16)
    gate = jnp.dot(h, wg_ref[...], preferred_element_type=F32)
    up = jnp.dot(h, wu_ref[...], preferred_element_type=F32)
    shared = jnp.dot((_silu(gate) * up).astype(BF16), wd_ref[...], preferred_element_type=F32)
    ffn = routed_ref[...] + shared
    y_ref[...] = _layer_norm(DEEPNORM_ALPHA * x1_ref[...] + gate2 * ffn, g_ref[...], b_ref[...])


def _final(x1, h2, routed, mod3, w_g, w_u, w_d, ln_g, ln_b, mod_map):
    t, d = x1.shape
    tm = TOKEN_TILE
    row = lambda i: (i, 0)
    full = lambda i: (0, 0)
    return pl.pallas_call(
        functools.partial(_final_kernel, d=d),
        grid=(t // tm,),
        in_specs=[
            pl.BlockSpec((tm, d), row),
            pl.BlockSpec((tm, d), row),
            pl.BlockSpec((tm, d), row),
            pl.BlockSpec((1, 1, mod3.shape[2]), mod_map),
            pl.BlockSpec(w_g.shape, full),
            pl.BlockSpec(w_u.shape, full),
            pl.BlockSpec(w_d.shape, full),
            pl.BlockSpec((1, d), full),
            pl.BlockSpec((1, d), full),
        ],
        out_specs=pl.BlockSpec((tm, d), row),
        out_shape=jax.ShapeDtypeStruct((t, d), F32),
        compiler_params=_params(("arbitrary",)),
    )(x1, h2, routed, mod3, w_g, w_u, w_d, ln_g, ln_b)


def kernel(x_prompt, x_sample, cache_k, cache_v, state_hgrn, c, c_ctx, w_mod, b_mod, w_in, hg_lb, hg_norm_g, q_norm_g, k_norm_g, w_branch_a, w_branch_b, w_out, ln1_g, ln1_b, w_router, router_bias, w_e_gate, w_e_up, w_e_down, w_s_gate, w_s_up, w_s_down, ln2_g, ln2_b):
    assert w_mod.shape[0] == DEPTH
    n_ctx, seq_ctx, d = x_prompt.shape
    n_lat, seq_lat, _ = x_sample.shape
    t_ctx = n_ctx * seq_ctx
    t_lat = n_lat * seq_lat
    assert seq_ctx == TOKEN_TILE and seq_lat % TOKEN_TILE == 0
    layer = 0

    lb = jnp.cumsum(jax.nn.softmax(hg_lb.astype(F32), axis=0), axis=0)[layer]

    cond = jnp.concatenate([c_ctx[None, :], c], axis=0)
    cond = jnp.pad(cond, ((0, (-cond.shape[0]) % SUBLANES), (0, 0)))
    mod = _modulation(cond, w_mod[layer].astype(BF16), b_mod[layer][None, :])
    mod3 = mod.reshape(mod.shape[0], 1, mod.shape[1])
    mod_map = _mod_row_map(t_ctx // TOKEN_TILE, seq_lat // TOKEN_TILE)

    x_all = jnp.concatenate([x_prompt.reshape(t_ctx, d), x_sample.reshape(t_lat, d)], axis=0)
    zh, za, zg = _input_projection(x_all, mod3, w_in[layer].astype(BF16), lb, mod_map)

    norm_g = hg_norm_g[layer][None, :]
    oa_ctx, s_ctx = _hgrn_scan(zh, norm_g, None, batch=n_ctx, seq=seq_ctx, row_block0=0)
    oa_lat, _ = _hgrn_scan(zh, norm_g, state_hgrn[:, layer], batch=n_lat, seq=seq_lat,
                           row_block0=t_ctx // seq_lat)

    q_gain = jnp.tile(q_norm_g[layer], N_HEADS)[None, :]
    k_gain = jnp.tile(k_norm_g[layer], N_KV_HEADS)[None, :]
    lane = jnp.arange(ATT_WIDTH)
    group_ones = (lane[:, None] // HEAD_DIM == lane[None, :] // HEAD_DIM).astype(BF16)
    ob_ctx, k_ctx, v_ctx = _attention(za, q_gain, k_gain, group_ones, batch=n_ctx, seq=seq_ctx, row_block0=0)
    past = cache_k.shape[2]
    ob_lat = _attention(
        za, q_gain, k_gain, group_ones, batch=n_lat, seq=seq_lat, row_block0=t_ctx // seq_lat,
        rope=_rope_tables(seq_lat),
        cache=(cache_k[:, layer].reshape(n_lat, past, KV_WIDTH), cache_v[:, layer].reshape(n_lat, past, KV_WIDTH)))

    o_a = jnp.concatenate([oa_ctx, oa_lat], axis=0)
    o_b = jnp.concatenate([ob_ctx, ob_lat], axis=0)
    x1, h2 = _post_mixer(x_all, mod3, o_a, o_b, zg, w_branch_a[layer].astype(BF16),
                         w_branch_b[layer].astype(BF16), w_out[layer].astype(BF16),
                         ln1_g[layer][None, :], ln1_b[layer][None, :], mod_map)

    wr_t = w_router[layer].T
    wr_hi = wr_t.astype(BF16)
    wr_lo = (wr_t - wr_hi.astype(F32)).astype(BF16)
    idx_t, w_t = _router(h2, wr_hi, wr_lo, router_bias[layer][:, None])

    routed = []
    for lo, hi in ((0, t_ctx), (t_ctx, t_ctx + t_lat)):
        lists = _dispatch_lists(idx_t[:, lo:hi], w_t[:, lo:hi])
        rows = h2[lo:hi].reshape((hi - lo) * SUBLANES, LANES)
        y = _routed_experts(rows, *lists, w_e_gate[layer], w_e_up[layer], w_e_down[layer])
        routed.append(y.reshape(hi - lo, d))
    routed = jnp.concatenate(routed, axis=0)

    y_all = _final(x1, h2, routed, mod3, w_s_gate[layer].astype(BF16), w_s_up[layer].astype(BF16),
                   w_s_down[layer].astype(BF16), ln2_g[layer][None, :], ln2_b[layer][None, :], mod_map)

    y_prompt = y_all[:t_ctx].reshape(n_ctx, seq_ctx, d)
    y_sample = y_all[t_ctx:].reshape(n_lat, seq_lat, d)
    new_cache_k = k_ctx.reshape(n_ctx, 1, seq_ctx, N_KV_HEADS, HEAD_DIM)
    new_cache_v = v_ctx.reshape(n_ctx, 1, seq_ctx, N_KV_HEADS, HEAD_DIM)
    new_state = s_ctx[:, None]
    return (y_prompt, y_sample, new_cache_k, new_cache_v, new_state)
```

```python
import functools

import jax
import jax.numpy as jnp
from jax import lax
from jax.experimental import pallas as pl
from jax.experimental.pallas import tpu as pltpu

F32 = jnp.float32
BF16 = jnp.bfloat16

GRID_W = 64
HG_HEADS = 4
HG_DK = 128
HG_DV = 128
HG_WIDTH = HG_HEADS * HG_DK
N_HEADS = 8
N_KV_HEADS = 2
HEAD_DIM = 64
ATT_WIDTH = N_HEADS * HEAD_DIM
KV_WIDTH = N_KV_HEADS * HEAD_DIM
ROPE_THETA = 10000.0
N_EXPERTS = 256
TOP_K = 8
N_GROUPS = 8
TOPK_GROUPS = 4
GROUP_SIZE = N_EXPERTS // N_GROUPS
ROUTED_SCALE = 2.5
NORM_EPS = 1e-6
DEPTH = 1
DEEPNORM_ALPHA = (2 * DEPTH) ** 0.25

LANES = 128
SUBLANES = 8
VMEM_LIMIT = 56 * 1024 * 1024

TOKEN_TILE = 256
HG_CHUNK = 32
ATT_Q_BLOCK = 128
MOE_BLOCK = 128
ROW_STRIDE = MOE_BLOCK + 1
ROW_GROUP = 8


def _sigmoid(x):
    return 1.0 / (1.0 + jnp.exp(-x))


def _silu(x):
    return x * _sigmoid(x)


def _params(sem=None):
    return pltpu.CompilerParams(dimension_semantics=sem, vmem_limit_bytes=VMEM_LIMIT)


def _mod_kernel(c_ref, w_ref, b_ref, o_ref):
    s = _silu(c_ref[...]).astype(BF16)
    o_ref[...] = jnp.dot(s, w_ref[...], preferred_element_type=F32) + b_ref[...]


def _modulation(cond, w_mod, b_mod):
    n, d = cond.shape
    width = w_mod.shape[1]
    tn = width // 4
    return pl.pallas_call(
        _mod_kernel,
        grid=(4,),
        in_specs=[
            pl.BlockSpec((n, d), lambda j: (0, 0)),
            pl.BlockSpec((d, tn), lambda j: (0, j)),
            pl.BlockSpec((1, tn), lambda j: (0, j)),
        ],
        out_specs=pl.BlockSpec((n, tn), lambda j: (0, j)),
        out_shape=jax.ShapeDtypeStruct((n, width), F32),
        compiler_params=_params(("arbitrary",)),
    )(cond, w_mod, b_mod)


def _inproj_kernel(xc_ref, xl_ref, mod_ref, w_ref, lb_ref, zh_ref, za_ref, zg_ref, *, d, n_ctx_tiles):
    shift = mod_ref[0, :, 0:d]
    scale = mod_ref[0, :, d:2 * d]
    x = _pick(pl.program_id(0) < n_ctx_tiles, xc_ref, xl_ref)
    h = (x * (1.0 + scale) + shift).astype(BF16)

    def proj(lo, hi):
        return jnp.dot(h, w_ref[:, lo:hi], preferred_element_type=F32)

    w = HG_WIDTH
    zh_ref[:, 0:w] = _silu(proj(0, w))
    zh_ref[:, w:2 * w] = proj(w, 2 * w)
    for i in range(2):
        lb = lb_ref[i:i + 1, :]
        zh_ref[:, (2 + i) * w:(3 + i) * w] = lb + (1.0 - lb) * _sigmoid(proj((2 + i) * w, (3 + i) * w))
    zh_ref[:, 4 * w:5 * w] = _silu(proj(4 * w, 5 * w))
    a0 = 5 * w
    a1 = a0 + ATT_WIDTH + 2 * KV_WIDTH
    za_ref[...] = proj(a0, a1)
    for i in range(4):
        lo = a1 + i * (d // 2)
        zg_ref[:, i * (d // 2):(i + 1) * (d // 2)] = _sigmoid(proj(lo, lo + d // 2))


def _mod_row_map(n_ctx_tiles, tiles_per_latent):
    def index_map(i):
        row = jnp.where(i < n_ctx_tiles, 0, 1 + (i - n_ctx_tiles) // tiles_per_latent)
        return (row, 0, 0)
    return index_map


def _pair_specs(width, n_ctx_tiles):
    return [pl.BlockSpec((TOKEN_TILE, width), lambda i: (jnp.minimum(i, n_ctx_tiles - 1), 0)),
            pl.BlockSpec((TOKEN_TILE, width), lambda i: (jnp.maximum(i - n_ctx_tiles, 0), 0))]


def _pick(is_ctx, ctx_ref, lat_ref):
    return jnp.where(is_ctx, ctx_ref[...], lat_ref[...])


def _input_projection(x_ctx, x_lat, mod3, w_in, lb, mod_map):
    d = x_ctx.shape[1]
    n_ctx_tiles = x_ctx.shape[0] // TOKEN_TILE
    t = x_ctx.shape[0] + x_lat.shape[0]
    width = w_in.shape[1]
    zh_w = 5 * HG_WIDTH
    za_w = ATT_WIDTH + 2 * KV_WIDTH
    zg_w = 2 * d
    assert width == zh_w + za_w + zg_w
    tm = TOKEN_TILE
    return pl.pallas_call(
        functools.partial(_inproj_kernel, d=d, n_ctx_tiles=n_ctx_tiles),
        grid=(t // tm,),
        in_specs=_pair_specs(d, n_ctx_tiles) + [
            pl.BlockSpec((1, 1, mod3.shape[2]), mod_map),
            pl.BlockSpec((d, width), lambda i: (0, 0)),
            pl.BlockSpec((2, HG_WIDTH), lambda i: (0, 0)),
        ],
        out_specs=[
            pl.BlockSpec((tm, zh_w), lambda i: (i, 0)),
            pl.BlockSpec((tm, za_w), lambda i: (i, 0)),
            pl.BlockSpec((tm, zg_w), lambda i: (i, 0)),
        ],
        out_shape=[
            jax.ShapeDtypeStruct((t, zh_w), F32),
            jax.ShapeDtypeStruct((t, za_w), F32),
            jax.ShapeDtypeStruct((t, zg_w), F32),
        ],
        compiler_params=_params(("arbitrary",)),
    )(x_ctx, x_lat, mod3, w_in, lb)


def _hgrn_kernel(*refs, seq, has_state):
    if has_state:
        q_ref, v_ref, ff_ref, fb_ref, gs_ref, ng_ref, s0_ref, o_ref, sout_ref = refs
    else:
        q_ref, v_ref, ff_ref, fb_ref, gs_ref, ng_ref, o_ref, sout_ref = refs
        s0_ref = None
    c = HG_CHUNK
    n = seq // c
    q3 = q_ref[...].reshape(n, c, HG_DK)
    v3 = v_ref[...].astype(BF16).reshape(n, c, HG_DV)
    pos = lax.broadcasted_iota(jnp.int32, (seq, HG_DK), 0) % c
    t_idx = lax.broadcasted_iota(jnp.int32, (c, c), 0)
    s_idx = lax.broadcasted_iota(jnp.int32, (c, c), 1)
    o_sum = None
    for direction, f_ref in enumerate((ff_ref, fb_ref)):
        reverse = direction == 1
        f = f_ref[...]
        k3 = (1.0 - f).reshape(n, c, HG_DK)
        b = jnp.log(f)
        step = 1
        while step < c:
            if reverse:
                b = b + jnp.where(pos < c - step, pltpu.roll(b, seq - step, axis=0), 0.0)
            else:
                b = b + jnp.where(pos >= step, pltpu.roll(b, step, axis=0), 0.0)
            step *= 2
        b3 = b.reshape(n, c, HG_DK)
        edge = b3[:, 0:1, :] if reverse else b3[:, c - 1:c, :]
        mid = b3[:, c // 2:c // 2 + 1, :]
        q_in = (q3 * jnp.exp(b3)).astype(BF16)
        k_end = (k3 * jnp.exp(edge - b3)).astype(BF16)
        q_mid = (q3 * jnp.exp(b3 - mid)).astype(BF16)
        k_mid = (k3 * jnp.exp(mid - b3)).astype(BF16)
        scores = jnp.einsum('ntd,nsd->nts', q_mid, k_mid, preferred_element_type=F32)
        keep = (s_idx >= t_idx) if reverse else (s_idx <= t_idx)
        scores = jnp.where(keep[None], scores, 0.0).astype(BF16)
        o_intra = jnp.einsum('nts,nsv->ntv', scores, v3, preferred_element_type=F32)
        upd = jnp.einsum('nsv,nsd->nvd', v3, k_end, preferred_element_type=F32)
        dec = jnp.exp(edge)
        if has_state:
            st = s0_ref[0, direction, 0].T
        else:
            st = jnp.zeros((HG_DV, HG_DK), F32)
        before = [None] * n
        for ci in (range(n - 1, -1, -1) if reverse else range(n)):
            before[ci] = st.astype(BF16)
            st = st * dec[ci] + upd[ci]
        st_before = jnp.stack(before, axis=0)
        o_inter = jnp.einsum('ntd,nvd->ntv', q_in, st_before, preferred_element_type=F32)
        o_dir = (o_intra + o_inter).reshape(seq, HG_DV)
        o_sum = o_dir if o_sum is None else o_sum + o_dir
        sout_ref[0, direction, 0] = st.T
    ms = jnp.mean(o_sum * o_sum, axis=-1, keepdims=True)
    o = o_sum * lax.rsqrt(ms + NORM_EPS) * ng_ref[...]
    o_ref[...] = (o * gs_ref[...]).astype(o_ref.dtype)


def _hgrn_scan(zh, norm_g, s0, *, batch, seq, row_block0):
    has_state = s0 is not None
    h = HG_HEADS

    def col(section):
        return pl.BlockSpec((seq, HG_DK), lambda b, j: (row_block0 + b, section * h + j))

    in_specs = [col(0), col(1), col(2), col(3), col(4), pl.BlockSpec((1, HG_DV), lambda b, j: (0, 0))]
    args = [zh, zh, zh, zh, zh, norm_g]
    state_spec = pl.BlockSpec((1, 2, 1, HG_DK, HG_DV), lambda b, j: (b, 0, j, 0, 0))
    if has_state:
        in_specs.append(state_spec)
        args.append(s0)
    return pl.pallas_call(
        functools.partial(_hgrn_kernel, seq=seq, has_state=has_state),
        grid=(batch, h),
        in_specs=in_specs,
        out_specs=[pl.BlockSpec((seq, HG_DV), lambda b, j: (b, j)), state_spec],
        out_shape=[
            jax.ShapeDtypeStruct((batch * seq, h * HG_DV), BF16),
            jax.ShapeDtypeStruct((batch, 2, h, HG_DK, HG_DV), F32),
        ],
        compiler_params=_params(("arbitrary", "arbitrary")),
    )(*args)


def _group_rms_norm(x, group_ones, gain):
    sq = x * x
    hi = sq.astype(BF16)
    lo = (sq - hi.astype(F32)).astype(BF16)
    total = (jnp.dot(hi, group_ones, preferred_element_type=F32)
             + jnp.dot(lo, group_ones, preferred_element_type=F32))
    return x * lax.rsqrt(total * (1.0 / HEAD_DIM) + NORM_EPS) * gain


def _rope(x, cos, sin_signed):
    width = x.shape[1]
    quarter = HEAD_DIM // 4
    lane = lax.broadcasted_iota(jnp.int32, x.shape, 1)
    partner = jnp.where(lane % (2 * quarter) < quarter,
                        pltpu.roll(x, width - quarter, axis=1),
                        pltpu.roll(x, quarter, axis=1))
    return x * cos + partner * sin_signed


def _attn_kernel(*refs, seq, latent):
    if latent:
        za_ref, qg_ref, kg_ref, gm_ref, cos_ref, sin_ref, ck_ref, cv_ref, o_ref = refs
    else:
        za_ref, qg_ref, kg_ref, gm_ref, o_ref, kout_ref, vout_ref = refs
    q = za_ref[:, 0:ATT_WIDTH]
    k = za_ref[:, ATT_WIDTH:ATT_WIDTH + KV_WIDTH]
    v = za_ref[:, ATT_WIDTH + KV_WIDTH:ATT_WIDTH + 2 * KV_WIDTH]
    qn = _group_rms_norm(q, gm_ref[...], qg_ref[...])
    kn = _group_rms_norm(k, gm_ref[0:KV_WIDTH, 0:KV_WIDTH], kg_ref[...])
    if latent:
        qn = _rope(qn, cos_ref[...], sin_ref[...])
        kr = _rope(kn, cos_ref[:, 0:KV_WIDTH], sin_ref[:, 0:KV_WIDTH])
    else:
        kout_ref[...] = kn
        vout_ref[...] = v
        kr = kn
    qb = (qn * (HEAD_DIM ** -0.5)).astype(BF16)
    kb = kr.astype(BF16)
    vb = v.astype(BF16)
    if latent:
        ckb = ck_ref[0].astype(BF16)
        cvb = cv_ref[0].astype(BF16)
    groups = N_HEADS // N_KV_HEADS
    tq = ATT_Q_BLOCK if latent else seq
    nt = (((1,), (1,)), ((), ()))
    for kh in range(N_KV_HEADS):
        ksl = slice(kh * HEAD_DIM, (kh + 1) * HEAD_DIM)
        k_new = kb[:, ksl]
        v_new = vb[:, ksl]
        for blk in range(seq // tq):
            rows = slice(blk * tq, (blk + 1) * tq)
            heads = [kh * groups + g for g in range(groups)]
            q_st = jnp.concatenate([qb[rows, hd * HEAD_DIM:(hd + 1) * HEAD_DIM] for hd in heads], axis=0)
            s_new = lax.dot_general(q_st, k_new, nt, preferred_element_type=F32)
            m = jnp.max(s_new, axis=-1, keepdims=True)
            if latent:
                s_ctx = lax.dot_general(q_st, ckb[:, ksl], nt, preferred_element_type=F32)
                m = jnp.maximum(m, jnp.max(s_ctx, axis=-1, keepdims=True))
            p_new = jnp.exp(s_new - m)
            denom = jnp.sum(p_new, axis=-1, keepdims=True)
            acc = jnp.dot(p_new.astype(BF16), v_new, preferred_element_type=F32)
            if latent:
                p_ctx = jnp.exp(s_ctx - m)
                denom = denom + jnp.sum(p_ctx, axis=-1, keepdims=True)
                acc = acc + jnp.dot(p_ctx.astype(BF16), cvb[:, ksl], preferred_element_type=F32)
            out = acc / denom
            for g in range(0, groups, 2):
                pair = jnp.concatenate([out[g * tq:(g + 1) * tq], out[(g + 1) * tq:(g + 2) * tq]], axis=1)
                lane0 = heads[g] * HEAD_DIM
                o_ref[rows, lane0:lane0 + 2 * HEAD_DIM] = pair.astype(o_ref.dtype)


def _attention(za, q_gain, k_gain, group_ones, *, batch, seq, row_block0, rope=None, cache=None):
    latent = cache is not None
    za_w = za.shape[1]
    in_specs = [
        pl.BlockSpec((seq, za_w), lambda b: (row_block0 + b, 0)),
        pl.BlockSpec((1, ATT_WIDTH), lambda b: (0, 0)),
        pl.BlockSpec((1, KV_WIDTH), lambda b: (0, 0)),
        pl.BlockSpec((ATT_WIDTH, ATT_WIDTH), lambda b: (0, 0)),
    ]
    args = [za, q_gain, k_gain, group_ones]
    o_spec = pl.BlockSpec((seq, ATT_WIDTH), lambda b: (b, 0))
    o_shape = jax.ShapeDtypeStruct((batch * seq, ATT_WIDTH), BF16)
    if latent:
        cos, sin_signed = rope
        ck, cv = cache
        past = ck.shape[1]
        in_specs += [
            pl.BlockSpec((seq, ATT_WIDTH), lambda b: (0, 0)),
            pl.BlockSpec((seq, ATT_WIDTH), lambda b: (0, 0)),
            pl.BlockSpec((1, past, KV_WIDTH), lambda b: (b, 0, 0)),
            pl.BlockSpec((1, past, KV_WIDTH), lambda b: (b, 0, 0)),
        ]
        args += [cos, sin_signed, ck, cv]
        out_specs = o_spec
        out_shape = o_shape
    else:
        kv_spec = pl.BlockSpec((seq, KV_WIDTH), lambda b: (b, 0))
        kv_shape = jax.ShapeDtypeStruct((batch * seq, KV_WIDTH), F32)
        out_specs = [o_spec, kv_spec, kv_spec]
        out_shape = [o_shape, kv_shape, kv_shape]
    return pl.pallas_call(
        functools.partial(_attn_kernel, seq=seq, latent=latent),
        grid=(batch,),
        in_specs=in_specs,
        out_specs=out_specs,
        out_shape=out_shape,
        compiler_params=_params(("arbitrary",)),
    )(*args)


def _rope_tables(seq):
    quarter = HEAD_DIM // 4
    t = jnp.arange(seq)
    row = (t // GRID_W).astype(F32)
    colp = (t % GRID_W).astype(F32)
    inv_freq = ROPE_THETA ** (-jnp.arange(quarter, dtype=F32) / quarter)
    lane = jnp.arange(HEAD_DIM)
    pos = jnp.where((lane < HEAD_DIM // 2)[None, :], row[:, None], colp[:, None])
    ang = pos * inv_freq[lane % quarter][None, :]
    sign = jnp.where(lane % (2 * quarter) < quarter, -1.0, 1.0)[None, :]
    cos = jnp.tile(jnp.cos(ang), (1, N_HEADS))
    sin_signed = jnp.tile(jnp.sin(ang) * sign, (1, N_HEADS))
    return cos, sin_signed


def _layer_norm(y, g, b):
    mu = jnp.mean(y, axis=-1, keepdims=True)
    yc = y - mu
    var = jnp.mean(yc * yc, axis=-1, keepdims=True)
    return yc * lax.rsqrt(var + NORM_EPS) * g + b


def _post_mixer_kernel(xc_ref, xl_ref, mod_ref, oac_ref, oal_ref, obc_ref, obl_ref, zg_ref, wa_ref, wb_ref,
                       wo_ref, g_ref, b_ref, x1_ref, h2_ref, *, d, n_ctx_tiles):
    is_ctx = pl.program_id(0) < n_ctx_tiles
    gate1 = mod_ref[0, :, 2 * d:3 * d]
    shift2 = mod_ref[0, :, 3 * d:4 * d]
    scale2 = mod_ref[0, :, 4 * d:5 * d]
    branch_a = jnp.dot(_pick(is_ctx, oac_ref, oal_ref), wa_ref[...], preferred_element_type=F32)
    branch_b = jnp.dot(_pick(is_ctx, obc_ref, obl_ref), wb_ref[...], preferred_element_type=F32)
    merged = zg_ref[:, 0:d] * branch_a + zg_ref[:, d:2 * d] * branch_b
    mix = jnp.dot(merged.astype(BF16), wo_ref[...], preferred_element_type=F32)
    x = _pick(is_ctx, xc_ref, xl_ref)
    x1 = _layer_norm(DEEPNORM_ALPHA * x + gate1 * mix, g_ref[...], b_ref[...])
    x1_ref[...] = x1
    h2_ref[...] = x1 * (1.0 + scale2) + shift2


def _post_mixer(x_ctx, x_lat, mod3, oa_ctx, oa_lat, ob_ctx, ob_lat, zg, w_a, w_b, w_o, ln_g, ln_b, mod_map):
    d = x_ctx.shape[1]
    n_ctx_tiles = x_ctx.shape[0] // TOKEN_TILE
    t = x_ctx.shape[0] + x_lat.shape[0]
    tm = TOKEN_TILE
    row = lambda i: (i, 0)
    full = lambda i: (0, 0)
    return pl.pallas_call(
        functools.partial(_post_mixer_kernel, d=d, n_ctx_tiles=n_ctx_tiles),
        grid=(t // tm,),
        in_specs=_pair_specs(d, n_ctx_tiles) + [
            pl.BlockSpec((1, 1, mod3.shape[2]), mod_map),
        ] + _pair_specs(oa_ctx.shape[1], n_ctx_tiles) + _pair_specs(ob_ctx.shape[1], n_ctx_tiles) + [
            pl.BlockSpec((tm, 2 * d), row),
            pl.BlockSpec(w_a.shape, full),
            pl.BlockSpec(w_b.shape, full),
            pl.BlockSpec(w_o.shape, full),
            pl.BlockSpec((1, d), full),
            pl.BlockSpec((1, d), full),
        ],
        out_specs=[pl.BlockSpec((tm, d), row), pl.BlockSpec((tm, d), row)],
        out_shape=[jax.ShapeDtypeStruct((t, d), F32), jax.ShapeDtypeStruct((t, d), F32)],
        compiler_params=_params(("arbitrary",)),
    )(x_ctx, x_lat, mod3, oa_ctx, oa_lat, ob_ctx, ob_lat, zg, w_a, w_b, w_o, ln_g, ln_b)


def _first_index_of_max(x, idx, sentinel):
    m = jnp.max(x, axis=0, keepdims=True)
    first = jnp.min(jnp.where(x == m, idx, sentinel), axis=0, keepdims=True)
    return m, first


def _router_kernel(h_ref, wh_ref, wl_ref, bias_ref, idx_ref, w_ref):
    h = h_ref[...]
    hh = h.astype(BF16)
    hl = (h - hh.astype(F32)).astype(BF16)
    nt = (((1,), (1,)), ((), ()))
    wh = wh_ref[...]
    logits = (lax.dot_general(wh, hh, nt, preferred_element_type=F32)
              + lax.dot_general(wh, hl, nt, preferred_element_type=F32)
              + lax.dot_general(wl_ref[...], hh, nt, preferred_element_type=F32))
    scores = _sigmoid(logits)
    sel = scores + bias_ref[...]
    tm = sel.shape[1]
    neg = -jnp.inf
    gidx = lax.broadcasted_iota(jnp.int32, (GROUP_SIZE, tm), 0)
    group_scores = []
    for g in range(N_GROUPS):
        sg = sel[g * GROUP_SIZE:(g + 1) * GROUP_SIZE, :]
        m1, first = _first_index_of_max(sg, gidx, GROUP_SIZE)
        m2 = jnp.max(jnp.where(gidx == first, neg, sg), axis=0, keepdims=True)
        group_scores.append(m1 + m2)
    gs = jnp.concatenate(group_scores, axis=0)
    nidx = lax.broadcasted_iota(jnp.int32, (N_GROUPS, tm), 0)
    chosen = jnp.zeros((N_GROUPS, tm), jnp.bool_)
    for _ in range(TOPK_GROUPS):
        _, first = _first_index_of_max(gs, nidx, N_GROUPS)
        hit = nidx == first
        chosen = jnp.logical_or(chosen, hit)
        gs = jnp.where(hit, neg, gs)
    masked = jnp.concatenate(
        [jnp.where(chosen[g:g + 1, :], sel[g * GROUP_SIZE:(g + 1) * GROUP_SIZE, :], neg) for g in range(N_GROUPS)],
        axis=0)
    eidx = lax.broadcasted_iota(jnp.int32, (N_EXPERTS, tm), 0)
    picks, weights = [], []
    for _ in range(TOP_K):
        _, first = _first_index_of_max(masked, eidx, N_EXPERTS)
        hit = eidx == first
        picks.append(first)
        weights.append(jnp.sum(jnp.where(hit, scores, 0.0), axis=0, keepdims=True))
        masked = jnp.where(hit, neg, masked)
    wk = jnp.concatenate(weights, axis=0)
    idx_ref[...] = jnp.concatenate(picks, axis=0)
    w_ref[...] = wk / jnp.sum(wk, axis=0, keepdims=True) * ROUTED_SCALE


def _router(h2, wr_hi, wr_lo, bias_col):
    t, d = h2.shape
    tm = TOKEN_TILE
    return pl.pallas_call(
        _router_kernel,
        grid=(t // tm,),
        in_specs=[
            pl.BlockSpec((tm, d), lambda i: (i, 0)),
            pl.BlockSpec((N_EXPERTS, d), lambda i: (0, 0)),
            pl.BlockSpec((N_EXPERTS, d), lambda i: (0, 0)),
            pl.BlockSpec((N_EXPERTS, 1), lambda i: (0, 0)),
        ],
        out_specs=[pl.BlockSpec((TOP_K, tm), lambda i: (0, i)), pl.BlockSpec((TOP_K, tm), lambda i: (0, i))],
        out_shape=[jax.ShapeDtypeStruct((TOP_K, t), jnp.int32), jax.ShapeDtypeStruct((TOP_K, t), F32)],
        compiler_params=_params(("arbitrary",)),
    )(h2, wr_hi, wr_lo, bias_col)


def _dispatch_lists(idx_t, w_t):
    k, t = idx_t.shape
    s = k * t
    n_blocks = s // MOE_BLOCK + N_EXPERTS
    flat_tok = jnp.arange(s, dtype=jnp.int32) % t
    sorted_e, sorted_tok, sorted_w = lax.sort((idx_t.reshape(s), flat_tok, w_t.reshape(s)), num_keys=1)
    experts = jnp.arange(N_EXPERTS + 1, dtype=jnp.int32)
    start = jnp.sum(sorted_e[None, :] < experts[:, None], axis=1, dtype=jnp.int32)
    counts = start[1:] - start[:-1]
    blocks_per_e = (counts + MOE_BLOCK - 1) // MOE_BLOCK
    block_end = jnp.cumsum(blocks_per_e)
    n_used = block_end[-1]
    blk = jnp.arange(n_blocks, dtype=jnp.int32)
    block_e = jnp.sum(block_end[None, :] <= blk[:, None], axis=1, dtype=jnp.int32)
    block_e = jnp.minimum(block_e, N_EXPERTS - 1)
    onehot = (block_e[:, None] == experts[None, :N_EXPERTS]).astype(jnp.int32)
    pick = lambda a: jnp.sum(onehot * a[None, :], axis=1)
    offset = (blk - (pick(block_end) - pick(blocks_per_e))) * MOE_BLOCK
    row_start = pick(start[:-1]) + offset
    n_rows = jnp.clip(pick(counts) - offset, 0, MOE_BLOCK)
    used = blk < n_used
    last_e = jnp.sum(jnp.where(blk == n_used - 1, block_e, 0))
    block_e = jnp.where(used, block_e, last_e)
    row_start = jnp.where(used, row_start, 0)
    n_rows = jnp.where(used, n_rows, 0)
    pad = jnp.full((MOE_BLOCK,), t, jnp.int32)
    sorted_tok = jnp.concatenate([sorted_tok, pad])
    sorted_w = jnp.concatenate([sorted_w, jnp.zeros((MOE_BLOCK,), F32)])
    return block_e, row_start, n_rows, sorted_tok, sorted_w


def _moe_kernel(be_ref, rs_ref, nr_ref, tok_ref, w_ref, x_hbm, wg_ref, wu_ref, wd_ref, y_hbm,
                xs, ys, xt, ot, wgb, wub, wdb, sem, *, t, n_blocks):
    b = pl.program_id(0)
    rows = t * SUBLANES
    chunks = wg_ref.shape[1] // LANES
    n_rows = nr_ref[b]
    group = ROW_GROUP

    @pl.when(b == 0)
    def _load():
        cp = pltpu.make_async_copy(x_hbm, xs.at[pl.ds(0, rows)], sem)
        cp.start()
        cp.wait()
        xs[pl.ds(rows, SUBLANES), :] = jnp.zeros((SUBLANES, LANES), F32)
        ys[...] = jnp.zeros(ys.shape, F32)
        xt[...] = jnp.zeros(xt.shape, F32)

    @pl.when(n_rows > 0)
    def _block():
        base = rs_ref[b]

        @pl.when(jnp.logical_or(b == 0, be_ref[b] != be_ref[jnp.maximum(b - 1, 0)]))
        def _new_expert():
            wgb[...] = wg_ref[0].astype(BF16)
            wub[...] = wu_ref[0].astype(BF16)
            wdb[...] = wd_ref[0].astype(BF16)

        def slab_row(tok):
            return pl.multiple_of(tok * SUBLANES, SUBLANES)

        def gather_group(g, carry):
            r0 = g * group
            for i in range(group):
                slab = xs[pl.ds(slab_row(tok_ref[base + r0 + i]), SUBLANES), :]
                xt[pl.ds(r0 + i, chunks, stride=ROW_STRIDE), :] = slab
            return carry

        lax.fori_loop(0, (n_rows + group - 1) // group, gather_group, 0)
        x = jnp.concatenate(
            [xt[j * ROW_STRIDE:j * ROW_STRIDE + MOE_BLOCK, :] for j in range(chunks)], axis=1).astype(BF16)
        gate = jnp.dot(x, wgb[...], preferred_element_type=F32)
        up = jnp.dot(x, wub[...], preferred_element_type=F32)
        hidden = (_silu(gate) * up).astype(BF16)
        out = jnp.dot(hidden, wdb[...], preferred_element_type=F32)
        for j in range(chunks):
            ot[j * ROW_STRIDE:j * ROW_STRIDE + MOE_BLOCK, :] = out[:, j * LANES:(j + 1) * LANES]

        def scatter_rows(r0, toks, wts):
            vals = [ys[pl.ds(toks[i], SUBLANES), :] + wts[i] * ot[pl.ds(r0 + i, chunks, stride=ROW_STRIDE), :]
                    for i in range(group)]
            for i in range(group):
                ys[pl.ds(toks[i], SUBLANES), :] = vals[i]

        def scatter_group(g, carry):
            r0 = g * group
            scatter_rows(r0, [slab_row(tok_ref[base + r0 + i]) for i in range(group)],
                         [w_ref[base + r0 + i] for i in range(group)])
            return carry

        full_groups = n_rows // group
        lax.fori_loop(0, full_groups, scatter_group, 0)
        tail = n_rows - full_groups * group

        @pl.when(tail > 0)
        def _tail():
            r0 = full_groups * group
            toks = [slab_row(jnp.where(i < tail, tok_ref[base + r0 + i], t)) for i in range(group)]
            wts = [jnp.where(i < tail, w_ref[base + r0 + i], 0.0) for i in range(group)]
            scatter_rows(r0, toks, wts)

    @pl.when(b == n_blocks - 1)
    def _store():
        cp = pltpu.make_async_copy(ys.at[pl.ds(0, rows)], y_hbm, sem)
        cp.start()
        cp.wait()


def _routed_experts(h2_rows, block_e, row_start, n_rows, sorted_tok, sorted_w, w_gate, w_up, w_down):
    rows, lanes = h2_rows.shape
    t = rows // SUBLANES
    n_blocks = block_e.shape[0]
    d, e_dim = w_gate.shape[1], w_gate.shape[2]
    assert d == SUBLANES * LANES and lanes == LANES
    grid_spec = pltpu.PrefetchScalarGridSpec(
        num_scalar_prefetch=5,
        grid=(n_blocks,),
        in_specs=[
            pl.BlockSpec(memory_space=pl.ANY),
            pl.BlockSpec((1, d, e_dim), lambda b, be, *_: (be[b], 0, 0)),
            pl.BlockSpec((1, d, e_dim), lambda b, be, *_: (be[b], 0, 0)),
            pl.BlockSpec((1, e_dim, d), lambda b, be, *_: (be[b], 0, 0)),
        ],
        out_specs=pl.BlockSpec(memory_space=pl.ANY),
        scratch_shapes=[
            pltpu.VMEM((rows + SUBLANES, LANES), F32),
            pltpu.VMEM((rows + SUBLANES, LANES), F32),
            pltpu.VMEM((SUBLANES * ROW_STRIDE, LANES), F32),
            pltpu.VMEM((SUBLANES * ROW_STRIDE, LANES), F32),
            pltpu.VMEM((d, e_dim), BF16),
            pltpu.VMEM((d, e_dim), BF16),
            pltpu.VMEM((e_dim, d), BF16),
            pltpu.SemaphoreType.DMA(()),
        ],
    )
    return pl.pallas_call(
        functools.partial(_moe_kernel, t=t, n_blocks=n_blocks),
        grid_spec=grid_spec,
        out_shape=jax.ShapeDtypeStruct((rows, LANES), F32),
        compiler_params=_params(("arbitrary",)),
    )(block_e, row_start, n_rows, sorted_tok, sorted_w, h2_rows, w_gate, w_up, w_down)


def _final_kernel(x1_ref, h2_ref, rc_ref, rl_ref, mod_ref, wg_ref, wu_ref, wd_ref, g_ref, b_ref,
                  yc_ref, yl_ref, *, d, n_ctx_tiles):
    is_ctx = pl.program_id(0) < n_ctx_tiles
    gate2 = mod_ref[0, :, 5 * d:6 * d]
    h = h2_ref[...].astype(BF16)
    gate = jnp.dot(h, wg_ref[...], preferred_element_type=F32)
    up = jnp.dot(h, wu_ref[...], preferred_element_type=F32)
    shared = jnp.dot((_silu(gate) * up).astype(BF16), wd_ref[...], preferred_element_type=F32)
    ffn = _pick(is_ctx, rc_ref, rl_ref) + shared
    y = _layer_norm(DEEPNORM_ALPHA * x1_ref[...] + gate2 * ffn, g_ref[...], b_ref[...])

    @pl.when(is_ctx)
    def _():
        yc_ref[...] = y

    @pl.when(jnp.logical_not(is_ctx))
    def _():
        yl_ref[...] = y


def _final(x1, h2, routed_ctx, routed_lat, mod3, w_g, w_u, w_d, ln_g, ln_b, mod_map):
    t, d = x1.shape
    t_ctx, t_lat = routed_ctx.shape[0], routed_lat.shape[0]
    n_ctx_tiles = t_ctx // TOKEN_TILE
    tm = TOKEN_TILE
    row = lambda i: (i, 0)
    full = lambda i: (0, 0)
    return pl.pallas_call(
        functools.partial(_final_kernel, d=d, n_ctx_tiles=n_ctx_tiles),
        grid=(t // tm,),
        in_specs=[
            pl.BlockSpec((tm, d), row),
            pl.BlockSpec((tm, d), row),
        ] + _pair_specs(d, n_ctx_tiles) + [
            pl.BlockSpec((1, 1, mod3.shape[2]), mod_map),
            pl.BlockSpec(w_g.shape, full),
            pl.BlockSpec(w_u.shape, full),
            pl.BlockSpec(w_d.shape, full),
            pl.BlockSpec((1, d), full),
            pl.BlockSpec((1, d), full),
        ],
        out_specs=_pair_specs(d, n_ctx_tiles),
        out_shape=[jax.ShapeDtypeStruct((t_ctx, d), F32), jax.ShapeDtypeStruct((t_lat, d), F32)],
        compiler_params=_params(("arbitrary",)),
    )(x1, h2, routed_ctx, routed_lat, mod3, w_g, w_u, w_d, ln_g, ln_b)


def kernel(x_prompt, x_sample, cache_k, cache_v, state_hgrn, c, c_ctx, w_mod, b_mod, w_in, hg_lb, hg_norm_g, q_norm_g, k_norm_g, w_branch_a, w_branch_b, w_out, ln1_g, ln1_b, w_router, router_bias, w_e_gate, w_e_up, w_e_down, w_s_gate, w_s_up, w_s_down, ln2_g, ln2_b):
    assert w_mod.shape[0] == DEPTH
    n_ctx, seq_ctx, d = x_prompt.shape
    n_lat, seq_lat, _ = x_sample.shape
    t_ctx = n_ctx * seq_ctx
    t_lat = n_lat * seq_lat
    assert seq_ctx == TOKEN_TILE and seq_lat % TOKEN_TILE == 0
    layer = 0

    lb = jnp.cumsum(jax.nn.softmax(hg_lb.astype(F32), axis=0), axis=0)[layer]

    cond = jnp.concatenate([c_ctx[None, :], c], axis=0)
    cond = jnp.pad(cond, ((0, (-cond.shape[0]) % SUBLANES), (0, 0)))
    mod = _modulation(cond, w_mod[layer].astype(BF16), b_mod[layer][None, :])
    mod3 = mod.reshape(mod.shape[0], 1, mod.shape[1])
    mod_map = _mod_row_map(t_ctx // TOKEN_TILE, seq_lat // TOKEN_TILE)

    x_ctx = x_prompt.reshape(t_ctx, d)
    x_lat = x_sample.reshape(t_lat, d)
    zh, za, zg = _input_projection(x_ctx, x_lat, mod3, w_in[layer].astype(BF16), lb, mod_map)

    norm_g = hg_norm_g[layer][None, :]
    oa_ctx, s_ctx = _hgrn_scan(zh, norm_g, None, batch=n_ctx, seq=seq_ctx, row_block0=0)
    oa_lat, _ = _hgrn_scan(zh, norm_g, state_hgrn[:, layer], batch=n_lat, seq=seq_lat,
                           row_block0=t_ctx // seq_lat)

    q_gain = jnp.tile(q_norm_g[layer], N_HEADS)[None, :]
    k_gain = jnp.tile(k_norm_g[layer], N_KV_HEADS)[None, :]
    lane = jnp.arange(ATT_WIDTH)
    group_ones = (lane[:, None] // HEAD_DIM == lane[None, :] // HEAD_DIM).astype(BF16)
    ob_ctx, k_ctx, v_ctx = _attention(za, q_gain, k_gain, group_ones, batch=n_ctx, seq=seq_ctx, row_block0=0)
    past = cache_k.shape[2]
    ob_lat = _attention(
        za, q_gain, k_gain, group_ones, batch=n_lat, seq=seq_lat, row_block0=t_ctx // seq_lat,
        rope=_rope_tables(seq_lat),
        cache=(cache_k[:, layer].reshape(n_lat, past, KV_WIDTH), cache_v[:, layer].reshape(n_lat, past, KV_WIDTH)))

    x1, h2 = _post_mixer(x_ctx, x_lat, mod3, oa_ctx, oa_lat, ob_ctx, ob_lat, zg, w_branch_a[layer].astype(BF16),
                         w_branch_b[layer].astype(BF16), w_out[layer].astype(BF16),
                         ln1_g[layer][None, :], ln1_b[layer][None, :], mod_map)

    wr_t = w_router[layer].T
    wr_hi = wr_t.astype(BF16)
    wr_lo = (wr_t - wr_hi.astype(F32)).astype(BF16)
    idx_t, w_t = _router(h2, wr_hi, wr_lo, router_bias[layer][:, None])

    routed = []
    for lo, hi in ((0, t_ctx), (t_ctx, t_ctx + t_lat)):
        lists = _dispatch_lists(idx_t[:, lo:hi], w_t[:, lo:hi])
        rows = h2[lo:hi].reshape((hi - lo) * SUBLANES, LANES)
        y = _routed_experts(rows, *lists, w_e_gate[layer], w_e_up[layer], w_e_down[layer])
        routed.append(y.reshape(hi - lo, d))

    y_ctx, y_lat = _final(x1, h2, routed[0], routed[1], mod3, w_s_gate[layer].astype(BF16),
                          w_s_up[layer].astype(BF16), w_s_down[layer].astype(BF16),
                          ln2_g[layer][None, :], ln2_b[layer][None, :], mod_map)

    y_prompt = y_ctx.reshape(n_ctx, seq_ctx, d)
    y_sample = y_lat.reshape(n_lat, seq_lat, d)
    new_cache_k = k_ctx.reshape(n_ctx, 1, seq_ctx, N_KV_HEADS, HEAD_DIM)
    new_cache_v = v_ctx.reshape(n_ctx, 1, seq_ctx, N_KV_HEADS, HEAD_DIM)
    new_state = s_ctx[:, None]
    return (y_prompt, y_sample, new_cache_k, new_cache_v, new_state)
```

```python
import functools

import jax
import jax.numpy as jnp
from jax import lax
from jax.experimental import pallas as pl
from jax.experimental.pallas import tpu as pltpu

F32 = jnp.float32
BF16 = jnp.bfloat16

GRID_W = 64
HG_HEADS = 4
HG_DK = 128
HG_DV = 128
HG_WIDTH = HG_HEADS * HG_DK
N_HEADS = 8
N_KV_HEADS = 2
HEAD_DIM = 64
ATT_WIDTH = N_HEADS * HEAD_DIM
KV_WIDTH = N_KV_HEADS * HEAD_DIM
ROPE_THETA = 10000.0
N_EXPERTS = 256
TOP_K = 8
N_GROUPS = 8
TOPK_GROUPS = 4
GROUP_SIZE = N_EXPERTS // N_GROUPS
ROUTED_SCALE = 2.5
NORM_EPS = 1e-6
DEPTH = 1
DEEPNORM_ALPHA = (2 * DEPTH) ** 0.25

LANES = 128
SUBLANES = 8
VMEM_LIMIT = 56 * 1024 * 1024

TOKEN_TILE = 256
HG_CHUNK = 32
ATT_Q_BLOCK = 128
MOE_BLOCK = 128
ROW_STRIDE = MOE_BLOCK + 1
ROW_GROUP = 8
ROW_SEGMENT = 32


def _sigmoid(x):
    return 1.0 / (1.0 + jnp.exp(-x))


def _silu(x):
    return x * _sigmoid(x)


def _params(sem=None):
    return pltpu.CompilerParams(dimension_semantics=sem, vmem_limit_bytes=VMEM_LIMIT)


def _mod_kernel(c_ref, w_ref, b_ref, o_ref):
    s = _silu(c_ref[...]).astype(BF16)
    o_ref[...] = jnp.dot(s, w_ref[...], preferred_element_type=F32) + b_ref[...]


def _modulation(cond, w_mod, b_mod):
    n, d = cond.shape
    width = w_mod.shape[1]
    tn = width // 4
    return pl.pallas_call(
        _mod_kernel,
        grid=(4,),
        in_specs=[
            pl.BlockSpec((n, d), lambda j: (0, 0)),
            pl.BlockSpec((d, tn), lambda j: (0, j)),
            pl.BlockSpec((1, tn), lambda j: (0, j)),
        ],
        out_specs=pl.BlockSpec((n, tn), lambda j: (0, j)),
        out_shape=jax.ShapeDtypeStruct((n, width), F32),
        compiler_params=_params(("arbitrary",)),
    )(cond, w_mod, b_mod)


def _inproj_kernel(xc_ref, xl_ref, mod_ref, w_ref, lb_ref, zh_ref, za_ref, zg_ref, *, d, n_ctx_tiles):
    shift = mod_ref[0, :, 0:d]
    scale = mod_ref[0, :, d:2 * d]
    x = _pick(pl.program_id(0) < n_ctx_tiles, xc_ref, xl_ref)
    h = (x * (1.0 + scale) + shift).astype(BF16)

    def proj(lo, hi):
        return jnp.dot(h, w_ref[:, lo:hi], preferred_element_type=F32)

    w = HG_WIDTH
    zh_ref[:, 0:w] = _silu(proj(0, w))
    zh_ref[:, w:2 * w] = proj(w, 2 * w)
    for i in range(2):
        lb = lb_ref[i:i + 1, :]
        zh_ref[:, (2 + i) * w:(3 + i) * w] = lb + (1.0 - lb) * _sigmoid(proj((2 + i) * w, (3 + i) * w))
    zh_ref[:, 4 * w:5 * w] = _silu(proj(4 * w, 5 * w))
    a0 = 5 * w
    a1 = a0 + ATT_WIDTH + 2 * KV_WIDTH
    za_ref[...] = proj(a0, a1)
    for i in range(4):
        lo = a1 + i * (d // 2)
        zg_ref[:, i * (d // 2):(i + 1) * (d // 2)] = _sigmoid(proj(lo, lo + d // 2))


def _mod_row_map(n_ctx_tiles, tiles_per_latent):
    def index_map(i):
        row = jnp.where(i < n_ctx_tiles, 0, 1 + (i - n_ctx_tiles) // tiles_per_latent)
        return (row, 0, 0)
    return index_map


def _pair_specs(width, n_ctx_tiles):
    return [pl.BlockSpec((TOKEN_TILE, width), lambda i: (jnp.minimum(i, n_ctx_tiles - 1), 0)),
            pl.BlockSpec((TOKEN_TILE, width), lambda i: (jnp.maximum(i - n_ctx_tiles, 0), 0))]


def _pick(is_ctx, ctx_ref, lat_ref):
    return jnp.where(is_ctx, ctx_ref[...], lat_ref[...])


def _input_projection(x_ctx, x_lat, mod3, w_in, lb, mod_map):
    d = x_ctx.shape[1]
    n_ctx_tiles = x_ctx.shape[0] // TOKEN_TILE
    t = x_ctx.shape[0] + x_lat.shape[0]
    width = w_in.shape[1]
    zh_w = 5 * HG_WIDTH
    za_w = ATT_WIDTH + 2 * KV_WIDTH
    zg_w = 2 * d
    assert width == zh_w + za_w + zg_w
    tm = TOKEN_TILE
    return pl.pallas_call(
        functools.partial(_inproj_kernel, d=d, n_ctx_tiles=n_ctx_tiles),
        grid=(t // tm,),
        in_specs=_pair_specs(d, n_ctx_tiles) + [
            pl.BlockSpec((1, 1, mod3.shape[2]), mod_map),
            pl.BlockSpec((d, width), lambda i: (0, 0)),
            pl.BlockSpec((2, HG_WIDTH), lambda i: (0, 0)),
        ],
        out_specs=[
            pl.BlockSpec((tm, zh_w), lambda i: (i, 0)),
            pl.BlockSpec((tm, za_w), lambda i: (i, 0)),
            pl.BlockSpec((tm, zg_w), lambda i: (i, 0)),
        ],
        out_shape=[
            jax.ShapeDtypeStruct((t, zh_w), F32),
            jax.ShapeDtypeStruct((t, za_w), F32),
            jax.ShapeDtypeStruct((t, zg_w), F32),
        ],
        compiler_params=_params(("arbitrary",)),
    )(x_ctx, x_lat, mod3, w_in, lb)


def _hgrn_kernel(*refs, seq, has_state):
    if has_state:
        q_ref, v_ref, ff_ref, fb_ref, gs_ref, ng_ref, s0_ref, o_ref, sout_ref = refs
    else:
        q_ref, v_ref, ff_ref, fb_ref, gs_ref, ng_ref, o_ref, sout_ref = refs
        s0_ref = None
    c = HG_CHUNK
    n = seq // c
    q3 = q_ref[...].reshape(n, c, HG_DK)
    v3 = v_ref[...].astype(BF16).reshape(n, c, HG_DV)
    pos = lax.broadcasted_iota(jnp.int32, (seq, HG_DK), 0) % c
    t_idx = lax.broadcasted_iota(jnp.int32, (c, c), 0)
    s_idx = lax.broadcasted_iota(jnp.int32, (c, c), 1)
    o_sum = None
    for direction, f_ref in enumerate((ff_ref, fb_ref)):
        reverse = direction == 1
        f = f_ref[...]
        k3 = (1.0 - f).reshape(n, c, HG_DK)
        b = jnp.log(f)
        step = 1
        while step < c:
            if reverse:
                b = b + jnp.where(pos < c - step, pltpu.roll(b, seq - step, axis=0), 0.0)
            else:
                b = b + jnp.where(pos >= step, pltpu.roll(b, step, axis=0), 0.0)
            step *= 2
        b3 = b.reshape(n, c, HG_DK)
        edge = b3[:, 0:1, :] if reverse else b3[:, c - 1:c, :]
        mid = b3[:, c // 2:c // 2 + 1, :]
        q_in = (q3 * jnp.exp(b3)).astype(BF16)
        k_end = (k3 * jnp.exp(edge - b3)).astype(BF16)
        q_mid = (q3 * jnp.exp(b3 - mid)).astype(BF16)
        k_mid = (k3 * jnp.exp(mid - b3)).astype(BF16)
        scores = jnp.einsum('ntd,nsd->nts', q_mid, k_mid, preferred_element_type=F32)
        keep = (s_idx >= t_idx) if reverse else (s_idx <= t_idx)
        scores = jnp.where(keep[None], scores, 0.0).astype(BF16)
        o_intra = jnp.einsum('nts,nsv->ntv', scores, v3, preferred_element_type=F32)
        upd = jnp.einsum('nsv,nsd->nvd', v3, k_end, preferred_element_type=F32)
        dec = jnp.exp(edge)
        if has_state:
            st = s0_ref[0, direction, 0].T
        else:
            st = jnp.zeros((HG_DV, HG_DK), F32)
        before = [None] * n
        for ci in (range(n - 1, -1, -1) if reverse else range(n)):
            before[ci] = st.astype(BF16)
            st = st * dec[ci] + upd[ci]
        st_before = jnp.stack(before, axis=0)
        o_inter = jnp.einsum('ntd,nvd->ntv', q_in, st_before, preferred_element_type=F32)
        o_dir = (o_intra + o_inter).reshape(seq, HG_DV)
        o_sum = o_dir if o_sum is None else o_sum + o_dir
        sout_ref[0, direction, 0] = st.T
    ms = jnp.mean(o_sum * o_sum, axis=-1, keepdims=True)
    o = o_sum * lax.rsqrt(ms + NORM_EPS) * ng_ref[...]
    o_ref[...] = (o * gs_ref[...]).astype(o_ref.dtype)


def _hgrn_scan(zh, norm_g, s0, *, batch, seq, row_block0):
    has_state = s0 is not None
    h = HG_HEADS

    def col(section):
        return pl.BlockSpec((seq, HG_DK), lambda b, j: (row_block0 + b, section * h + j))

    in_specs = [col(0), col(1), col(2), col(3), col(4), pl.BlockSpec((1, HG_DV), lambda b, j: (0, 0))]
    args = [zh, zh, zh, zh, zh, norm_g]
    state_spec = pl.BlockSpec((1, 2, 1, HG_DK, HG_DV), lambda b, j: (b, 0, j, 0, 0))
    if has_state:
        in_specs.append(state_spec)
        args.append(s0)
    return pl.pallas_call(
        functools.partial(_hgrn_kernel, seq=seq, has_state=has_state),
        grid=(batch, h),
        in_specs=in_specs,
        out_specs=[pl.BlockSpec((seq, HG_DV), lambda b, j: (b, j)), state_spec],
        out_shape=[
            jax.ShapeDtypeStruct((batch * seq, h * HG_DV), BF16),
            jax.ShapeDtypeStruct((batch, 2, h, HG_DK, HG_DV), F32),
        ],
        compiler_params=_params(("arbitrary", "arbitrary")),
    )(*args)


def _group_rms_norm(x, group_ones, gain):
    sq = x * x
    hi = sq.astype(BF16)
    lo = (sq - hi.astype(F32)).astype(BF16)
    total = (jnp.dot(hi, group_ones, preferred_element_type=F32)
             + jnp.dot(lo, group_ones, preferred_element_type=F32))
    return x * lax.rsqrt(total * (1.0 / HEAD_DIM) + NORM_EPS) * gain


def _rope(x, cos, sin_signed):
    width = x.shape[1]
    quarter = HEAD_DIM // 4
    lane = lax.broadcasted_iota(jnp.int32, x.shape, 1)
    partner = jnp.where(lane % (2 * quarter) < quarter,
                        pltpu.roll(x, width - quarter, axis=1),
                        pltpu.roll(x, quarter, axis=1))
    return x * cos + partner * sin_signed


def _attn_kernel(*refs, seq, latent):
    if latent:
        za_ref, qg_ref, kg_ref, gm_ref, cos_ref, sin_ref, ck_ref, cv_ref, o_ref = refs
    else:
        za_ref, qg_ref, kg_ref, gm_ref, o_ref, kout_ref, vout_ref = refs
    q = za_ref[:, 0:ATT_WIDTH]
    k = za_ref[:, ATT_WIDTH:ATT_WIDTH + KV_WIDTH]
    v = za_ref[:, ATT_WIDTH + KV_WIDTH:ATT_WIDTH + 2 * KV_WIDTH]
    qn = _group_rms_norm(q, gm_ref[...], qg_ref[...])
    kn = _group_rms_norm(k, gm_ref[0:KV_WIDTH, 0:KV_WIDTH], kg_ref[...])
    if latent:
        qn = _rope(qn, cos_ref[...], sin_ref[...])
        kr = _rope(kn, cos_ref[:, 0:KV_WIDTH], sin_ref[:, 0:KV_WIDTH])
    else:
        kout_ref[...] = kn
        vout_ref[...] = v
        kr = kn
    qb = (qn * (HEAD_DIM ** -0.5)).astype(BF16)
    kb = kr.astype(BF16)
    vb = v.astype(BF16)
    if latent:
        ckb = ck_ref[0].astype(BF16)
        cvb = cv_ref[0].astype(BF16)
    groups = N_HEADS // N_KV_HEADS
    tq = ATT_Q_BLOCK if latent else seq
    nt = (((1,), (1,)), ((), ()))
    for kh in range(N_KV_HEADS):
        ksl = slice(kh * HEAD_DIM, (kh + 1) * HEAD_DIM)
        k_new = kb[:, ksl]
        v_new = vb[:, ksl]
        for blk in range(seq // tq):
            rows = slice(blk * tq, (blk + 1) * tq)
            heads = [kh * groups + g for g in range(groups)]
            q_st = jnp.concatenate([qb[rows, hd * HEAD_DIM:(hd + 1) * HEAD_DIM] for hd in heads], axis=0)
            s_new = lax.dot_general(q_st, k_new, nt, preferred_element_type=F32)
            m = jnp.max(s_new, axis=-1, keepdims=True)
            if latent:
                s_ctx = lax.dot_general(q_st, ckb[:, ksl], nt, preferred_element_type=F32)
                m = jnp.maximum(m, jnp.max(s_ctx, axis=-1, keepdims=True))
            p_new = jnp.exp(s_new - m)
            denom = jnp.sum(p_new, axis=-1, keepdims=True)
            acc = jnp.dot(p_new.astype(BF16), v_new, preferred_element_type=F32)
            if latent:
                p_ctx = jnp.exp(s_ctx - m)
                denom = denom + jnp.sum(p_ctx, axis=-1, keepdims=True)
                acc = acc + jnp.dot(p_ctx.astype(BF16), cvb[:, ksl], preferred_element_type=F32)
            out = acc / denom
            for g in range(0, groups, 2):
                pair = jnp.concatenate([out[g * tq:(g + 1) * tq], out[(g + 1) * tq:(g + 2) * tq]], axis=1)
                lane0 = heads[g] * HEAD_DIM
                o_ref[rows, lane0:lane0 + 2 * HEAD_DIM] = pair.astype(o_ref.dtype)


def _attention(za, q_gain, k_gain, group_ones, *, batch, seq, row_block0, rope=None, cache=None):
    latent = cache is not None
    za_w = za.shape[1]
    in_specs = [
        pl.BlockSpec((seq, za_w), lambda b: (row_block0 + b, 0)),
        pl.BlockSpec((1, ATT_WIDTH), lambda b: (0, 0)),
        pl.BlockSpec((1, KV_WIDTH), lambda b: (0, 0)),
        pl.BlockSpec((ATT_WIDTH, ATT_WIDTH), lambda b: (0, 0)),
    ]
    args = [za, q_gain, k_gain, group_ones]
    o_spec = pl.BlockSpec((seq, ATT_WIDTH), lambda b: (b, 0))
    o_shape = jax.ShapeDtypeStruct((batch * seq, ATT_WIDTH), BF16)
    if latent:
        cos, sin_signed = rope
        ck, cv = cache
        past = ck.shape[1]
        in_specs += [
            pl.BlockSpec((seq, ATT_WIDTH), lambda b: (0, 0)),
            pl.BlockSpec((seq, ATT_WIDTH), lambda b: (0, 0)),
            pl.BlockSpec((1, past, KV_WIDTH), lambda b: (b, 0, 0)),
            pl.BlockSpec((1, past, KV_WIDTH), lambda b: (b, 0, 0)),
        ]
        args += [cos, sin_signed, ck, cv]
        out_specs = o_spec
        out_shape = o_shape
    else:
        kv_spec = pl.BlockSpec((seq, KV_WIDTH), lambda b: (b, 0))
        kv_shape = jax.ShapeDtypeStruct((batch * seq, KV_WIDTH), F32)
        out_specs = [o_spec, kv_spec, kv_spec]
        out_shape = [o_shape, kv_shape, kv_shape]
    return pl.pallas_call(
        functools.partial(_attn_kernel, seq=seq, latent=latent),
        grid=(batch,),
        in_specs=in_specs,
        out_specs=out_specs,
        out_shape=out_shape,
        compiler_params=_params(("arbitrary",)),
    )(*args)


def _rope_tables(seq):
    quarter = HEAD_DIM // 4
    t = jnp.arange(seq)
    row = (t // GRID_W).astype(F32)
    colp = (t % GRID_W).astype(F32)
    inv_freq = ROPE_THETA ** (-jnp.arange(quarter, dtype=F32) / quarter)
    lane = jnp.arange(HEAD_DIM)
    pos = jnp.where((lane < HEAD_DIM // 2)[None, :], row[:, None], colp[:, None])
    ang = pos * inv_freq[lane % quarter][None, :]
    sign = jnp.where(lane % (2 * quarter) < quarter, -1.0, 1.0)[None, :]
    cos = jnp.tile(jnp.cos(ang), (1, N_HEADS))
    sin_signed = jnp.tile(jnp.sin(ang) * sign, (1, N_HEADS))
    return cos, sin_signed


def _layer_norm(y, g, b):
    mu = jnp.mean(y, axis=-1, keepdims=True)
    yc = y - mu
    var = jnp.mean(yc * yc, axis=-1, keepdims=True)
    return yc * lax.rsqrt(var + NORM_EPS) * g + b


def _post_mixer_kernel(xc_ref, xl_ref, mod_ref, oac_ref, oal_ref, obc_ref, obl_ref, zg_ref, wa_ref, wb_ref,
                       wo_ref, g_ref, b_ref, x1_ref, h2_ref, *, d, n_ctx_tiles):
    is_ctx = pl.program_id(0) < n_ctx_tiles
    gate1 = mod_ref[0, :, 2 * d:3 * d]
    shift2 = mod_ref[0, :, 3 * d:4 * d]
    scale2 = mod_ref[0, :, 4 * d:5 * d]
    branch_a = jnp.dot(_pick(is_ctx, oac_ref, oal_ref), wa_ref[...], preferred_element_type=F32)
    branch_b = jnp.dot(_pick(is_ctx, obc_ref, obl_ref), wb_ref[...], preferred_element_type=F32)
    merged = zg_ref[:, 0:d] * branch_a + zg_ref[:, d:2 * d] * branch_b
    mix = jnp.dot(merged.astype(BF16), wo_ref[...], preferred_element_type=F32)
    x = _pick(is_ctx, xc_ref, xl_ref)
    x1 = _layer_norm(DEEPNORM_ALPHA * x + gate1 * mix, g_ref[...], b_ref[...])
    x1_ref[...] = x1
    h2_ref[...] = x1 * (1.0 + scale2) + shift2


def _post_mixer(x_ctx, x_lat, mod3, oa_ctx, oa_lat, ob_ctx, ob_lat, zg, w_a, w_b, w_o, ln_g, ln_b, mod_map):
    d = x_ctx.shape[1]
    n_ctx_tiles = x_ctx.shape[0] // TOKEN_TILE
    t = x_ctx.shape[0] + x_lat.shape[0]
    tm = TOKEN_TILE
    row = lambda i: (i, 0)
    full = lambda i: (0, 0)
    return pl.pallas_call(
        functools.partial(_post_mixer_kernel, d=d, n_ctx_tiles=n_ctx_tiles),
        grid=(t // tm,),
        in_specs=_pair_specs(d, n_ctx_tiles) + [
            pl.BlockSpec((1, 1, mod3.shape[2]), mod_map),
        ] + _pair_specs(oa_ctx.shape[1], n_ctx_tiles) + _pair_specs(ob_ctx.shape[1], n_ctx_tiles) + [
            pl.BlockSpec((tm, 2 * d), row),
            pl.BlockSpec(w_a.shape, full),
            pl.BlockSpec(w_b.shape, full),
            pl.BlockSpec(w_o.shape, full),
            pl.BlockSpec((1, d), full),
            pl.BlockSpec((1, d), full),
        ],
        out_specs=[pl.BlockSpec((tm, d), row), pl.BlockSpec((tm, d), row)],
        out_shape=[jax.ShapeDtypeStruct((t, d), F32), jax.ShapeDtypeStruct((t, d), F32)],
        compiler_params=_params(("arbitrary",)),
    )(x_ctx, x_lat, mod3, oa_ctx, oa_lat, ob_ctx, ob_lat, zg, w_a, w_b, w_o, ln_g, ln_b)


def _first_index_of_max(x, idx, sentinel):
    m = jnp.max(x, axis=0, keepdims=True)
    first = jnp.min(jnp.where(x == m, idx, sentinel), axis=0, keepdims=True)
    return m, first


def _router_kernel(h_ref, wh_ref, wl_ref, bias_ref, idx_ref, w_ref):
    h = h_ref[...]
    hh = h.astype(BF16)
    hl = (h - hh.astype(F32)).astype(BF16)
    nt = (((1,), (1,)), ((), ()))
    wh = wh_ref[...]
    logits = (lax.dot_general(wh, hh, nt, preferred_element_type=F32)
              + lax.dot_general(wh, hl, nt, preferred_element_type=F32)
              + lax.dot_general(wl_ref[...], hh, nt, preferred_element_type=F32))
    scores = _sigmoid(logits)
    sel = scores + bias_ref[...]
    tm = sel.shape[1]
    neg = -jnp.inf
    gidx = lax.broadcasted_iota(jnp.int32, (GROUP_SIZE, tm), 0)
    group_scores = []
    for g in range(N_GROUPS):
        sg = sel[g * GROUP_SIZE:(g + 1) * GROUP_SIZE, :]
        m1, first = _first_index_of_max(sg, gidx, GROUP_SIZE)
        m2 = jnp.max(jnp.where(gidx == first, neg, sg), axis=0, keepdims=True)
        group_scores.append(m1 + m2)
    gs = jnp.concatenate(group_scores, axis=0)
    nidx = lax.broadcasted_iota(jnp.int32, (N_GROUPS, tm), 0)
    chosen = jnp.zeros((N_GROUPS, tm), jnp.bool_)
    for _ in range(TOPK_GROUPS):
        _, first = _first_index_of_max(gs, nidx, N_GROUPS)
        hit = nidx == first
        chosen = jnp.logical_or(chosen, hit)
        gs = jnp.where(hit, neg, gs)
    masked = jnp.concatenate(
        [jnp.where(chosen[g:g + 1, :], sel[g * GROUP_SIZE:(g + 1) * GROUP_SIZE, :], neg) for g in range(N_GROUPS)],
        axis=0)
    eidx = lax.broadcasted_iota(jnp.int32, (N_EXPERTS, tm), 0)
    picks, weights = [], []
    for _ in range(TOP_K):
        _, first = _first_index_of_max(masked, eidx, N_EXPERTS)
        hit = eidx == first
        picks.append(first)
        weights.append(jnp.sum(jnp.where(hit, scores, 0.0), axis=0, keepdims=True))
        masked = jnp.where(hit, neg, masked)
    wk = jnp.concatenate(weights, axis=0)
    idx_ref[...] = jnp.concatenate(picks, axis=0)
    w_ref[...] = wk / jnp.sum(wk, axis=0, keepdims=True) * ROUTED_SCALE


def _router(h2, wr_hi, wr_lo, bias_col):
    t, d = h2.shape
    tm = TOKEN_TILE
    return pl.pallas_call(
        _router_kernel,
        grid=(t // tm,),
        in_specs=[
            pl.BlockSpec((tm, d), lambda i: (i, 0)),
            pl.BlockSpec((N_EXPERTS, d), lambda i: (0, 0)),
            pl.BlockSpec((N_EXPERTS, d), lambda i: (0, 0)),
            pl.BlockSpec((N_EXPERTS, 1), lambda i: (0, 0)),
        ],
        out_specs=[pl.BlockSpec((TOP_K, tm), lambda i: (0, i)), pl.BlockSpec((TOP_K, tm), lambda i: (0, i))],
        out_shape=[jax.ShapeDtypeStruct((TOP_K, t), jnp.int32), jax.ShapeDtypeStruct((TOP_K, t), F32)],
        compiler_params=_params(("arbitrary",)),
    )(h2, wr_hi, wr_lo, bias_col)


def _dispatch_lists(idx_t, w_t):
    k, t = idx_t.shape
    s = k * t
    n_blocks = s // MOE_BLOCK + N_EXPERTS
    flat_tok = jnp.arange(s, dtype=jnp.int32) % t
    sorted_e, sorted_tok, sorted_w = lax.sort((idx_t.reshape(s), flat_tok, w_t.reshape(s)), num_keys=1)
    experts = jnp.arange(N_EXPERTS + 1, dtype=jnp.int32)
    start = jnp.sum(sorted_e[None, :] < experts[:, None], axis=1, dtype=jnp.int32)
    counts = start[1:] - start[:-1]
    blocks_per_e = (counts + MOE_BLOCK - 1) // MOE_BLOCK
    block_end = jnp.cumsum(blocks_per_e)
    n_used = block_end[-1]
    blk = jnp.arange(n_blocks, dtype=jnp.int32)
    block_e = jnp.sum(block_end[None, :] <= blk[:, None], axis=1, dtype=jnp.int32)
    block_e = jnp.minimum(block_e, N_EXPERTS - 1)
    onehot = (block_e[:, None] == experts[None, :N_EXPERTS]).astype(jnp.int32)
    pick = lambda a: jnp.sum(onehot * a[None, :], axis=1)
    offset = (blk - (pick(block_end) - pick(blocks_per_e))) * MOE_BLOCK
    row_start = pick(start[:-1]) + offset
    n_rows = jnp.clip(pick(counts) - offset, 0, MOE_BLOCK)
    used = blk < n_used
    last_e = jnp.sum(jnp.where(blk == n_used - 1, block_e, 0))
    block_e = jnp.where(used, block_e, last_e)
    row_start = jnp.where(used, row_start, 0)
    n_rows = jnp.where(used, n_rows, 0)
    pad = jnp.full((MOE_BLOCK,), t, jnp.int32)
    sorted_tok = jnp.concatenate([sorted_tok, pad])
    sorted_w = jnp.concatenate([sorted_w, jnp.zeros((MOE_BLOCK,), F32)])
    return block_e, row_start, n_rows, sorted_tok, sorted_w


def _moe_kernel(be_ref, rs_ref, nr_ref, tok_ref, w_ref, x_hbm, wg_ref, wu_ref, wd_ref, y_hbm,
                xs, ys, xt, ot, sem, *, t, n_blocks):
    b = pl.program_id(0)
    rows = t * SUBLANES
    chunks = wg_ref.shape[1] // LANES
    n_rows = nr_ref[b]
    group = ROW_GROUP

    @pl.when(b == 0)
    def _load():
        cp = pltpu.make_async_copy(x_hbm, xs.at[pl.ds(0, rows)], sem)
        cp.start()
        cp.wait()
        xs[pl.ds(rows, SUBLANES), :] = jnp.zeros((SUBLANES, LANES), F32)
        ys[...] = jnp.zeros(ys.shape, F32)
        xt[...] = jnp.zeros(xt.shape, F32)

    @pl.when(n_rows > 0)
    def _block():
        base = rs_ref[b]

        def slab_row(tok):
            return pl.multiple_of(tok * SUBLANES, SUBLANES)

        for seg in range(MOE_BLOCK // ROW_SEGMENT):
            @pl.when(n_rows > seg * ROW_SEGMENT)
            def _gather(seg=seg):
                for r in range(seg * ROW_SEGMENT, (seg + 1) * ROW_SEGMENT):
                    slab = xs[pl.ds(slab_row(tok_ref[base + r]), SUBLANES), :]
                    xt[pl.ds(r, chunks, stride=ROW_STRIDE), :] = slab

        x = jnp.concatenate(
            [xt[j * ROW_STRIDE:j * ROW_STRIDE + MOE_BLOCK, :] for j in range(chunks)], axis=1).astype(BF16)
        gate = jnp.dot(x, wg_ref[0].astype(BF16), preferred_element_type=F32)
        up = jnp.dot(x, wu_ref[0].astype(BF16), preferred_element_type=F32)
        hidden = (_silu(gate) * up).astype(BF16)
        out = jnp.dot(hidden, wd_ref[0].astype(BF16), preferred_element_type=F32)
        for j in range(chunks):
            ot[j * ROW_STRIDE:j * ROW_STRIDE + MOE_BLOCK, :] = out[:, j * LANES:(j + 1) * LANES]

        for seg in range(MOE_BLOCK // ROW_SEGMENT):
            @pl.when(n_rows > seg * ROW_SEGMENT)
            def _scatter(seg=seg):
                for r0 in range(seg * ROW_SEGMENT, (seg + 1) * ROW_SEGMENT, group):
                    toks = [slab_row(jnp.where(r0 + i < n_rows, tok_ref[base + r0 + i], t)) for i in range(group)]
                    vals = [ys[pl.ds(toks[i], SUBLANES), :]
                            + w_ref[base + r0 + i] * ot[pl.ds(r0 + i, chunks, stride=ROW_STRIDE), :]
                            for i in range(group)]
                    for i in range(group):
                        ys[pl.ds(toks[i], SUBLANES), :] = vals[i]

    @pl.when(b == n_blocks - 1)
    def _store():
        cp = pltpu.make_async_copy(ys.at[pl.ds(0, rows)], y_hbm, sem)
        cp.start()
        cp.wait()


def _routed_experts(h2_rows, block_e, row_start, n_rows, sorted_tok, sorted_w, w_gate, w_up, w_down):
    rows, lanes = h2_rows.shape
    t = rows // SUBLANES
    n_blocks = block_e.shape[0]
    d, e_dim = w_gate.shape[1], w_gate.shape[2]
    assert d == SUBLANES * LANES and lanes == LANES
    grid_spec = pltpu.PrefetchScalarGridSpec(
        num_scalar_prefetch=5,
        grid=(n_blocks,),
        in_specs=[
            pl.BlockSpec(memory_space=pl.ANY),
            pl.BlockSpec((1, d, e_dim), lambda b, be, *_: (be[b], 0, 0)),
            pl.BlockSpec((1, d, e_dim), lambda b, be, *_: (be[b], 0, 0)),
            pl.BlockSpec((1, e_dim, d), lambda b, be, *_: (be[b], 0, 0)),
        ],
        out_specs=pl.BlockSpec(memory_space=pl.ANY),
        scratch_shapes=[
            pltpu.VMEM((rows + SUBLANES, LANES), F32),
            pltpu.VMEM((rows + SUBLANES, LANES), F32),
            pltpu.VMEM((SUBLANES * ROW_STRIDE, LANES), F32),
            pltpu.VMEM((SUBLANES * ROW_STRIDE, LANES), F32),
            pltpu.SemaphoreType.DMA(()),
        ],
    )
    return pl.pallas_call(
        functools.partial(_moe_kernel, t=t, n_blocks=n_blocks),
        grid_spec=grid_spec,
        out_shape=jax.ShapeDtypeStruct((rows, LANES), F32),
        compiler_params=_params(("arbitrary",)),
    )(block_e, row_start, n_rows, sorted_tok, sorted_w, h2_rows, w_gate, w_up, w_down)


def _final_kernel(x1_ref, h2_ref, rc_ref, rl_ref, mod_ref, wg_ref, wu_ref, wd_ref, g_ref, b_ref,
                  yc_ref, yl_ref, *, d, n_ctx_tiles):
    is_ctx = pl.program_id(0) < n_ctx_tiles
    gate2 = mod_ref[0, :, 5 * d:6 * d]
    h = h2_ref[...].astype(BF16)
    gate = jnp.dot(h, wg_ref[...], preferred_element_type=F32)
    up = jnp.dot(h, wu_ref[...], preferred_element_type=F32)
    shared = jnp.dot((_silu(gate) * up).astype(BF16), wd_ref[...], preferred_element_type=F32)
    ffn = _pick(is_ctx, rc_ref, rl_ref) + shared
    y = _layer_norm(DEEPNORM_ALPHA * x1_ref[...] + gate2 * ffn, g_ref[...], b_ref[...])

    @pl.when(is_ctx)
    def _():
        yc_ref[...] = y

    @pl.when(jnp.logical_not(is_ctx))
    def _():
        yl_ref[...] = y


def _final(x1, h2, routed_ctx, routed_lat, mod3, w_g, w_u, w_d, ln_g, ln_b, mod_map):
    t, d = x1.shape
    t_ctx, t_lat = routed_ctx.shape[0], routed_lat.shape[0]
    n_ctx_tiles = t_ctx // TOKEN_TILE
    tm = TOKEN_TILE
    row = lambda i: (i, 0)
    full = lambda i: (0, 0)
    return pl.pallas_call(
        functools.partial(_final_kernel, d=d, n_ctx_tiles=n_ctx_tiles),
        grid=(t // tm,),
        in_specs=[
            pl.BlockSpec((tm, d), row),
            pl.BlockSpec((tm, d), row),
        ] + _pair_specs(d, n_ctx_tiles) + [
            pl.BlockSpec((1, 1, mod3.shape[2]), mod_map),
            pl.BlockSpec(w_g.shape, full),
            pl.BlockSpec(w_u.shape, full),
            pl.BlockSpec(w_d.shape, full),
            pl.BlockSpec((1, d), full),
            pl.BlockSpec((1, d), full),
        ],
        out_specs=_pair_specs(d, n_ctx_tiles),
        out_shape=[jax.ShapeDtypeStruct((t_ctx, d), F32), jax.ShapeDtypeStruct((t_lat, d), F32)],
        compiler_params=_params(("arbitrary",)),
    )(x1, h2, routed_ctx, routed_lat, mod3, w_g, w_u, w_d, ln_g, ln_b)


def kernel(x_prompt, x_sample, cache_k, cache_v, state_hgrn, c, c_ctx, w_mod, b_mod, w_in, hg_lb, hg_norm_g, q_norm_g, k_norm_g, w_branch_a, w_branch_b, w_out, ln1_g, ln1_b, w_router, router_bias, w_e_gate, w_e_up, w_e_down, w_s_gate, w_s_up, w_s_down, ln2_g, ln2_b):
    assert w_mod.shape[0] == DEPTH
    n_ctx, seq_ctx, d = x_prompt.shape
    n_lat, seq_lat, _ = x_sample.shape
    t_ctx = n_ctx * seq_ctx
    t_lat = n_lat * seq_lat
    assert seq_ctx == TOKEN_TILE and seq_lat % TOKEN_TILE == 0
    layer = 0

    lb = jnp.cumsum(jax.nn.softmax(hg_lb.astype(F32), axis=0), axis=0)[layer]

    cond = jnp.concatenate([c_ctx[None, :], c], axis=0)
    cond = jnp.pad(cond, ((0, (-cond.shape[0]) % SUBLANES), (0, 0)))
    mod = _modulation(cond, w_mod[layer].astype(BF16), b_mod[layer][None, :])
    mod3 = mod.reshape(mod.shape[0], 1, mod.shape[1])
    mod_map = _mod_row_map(t_ctx // TOKEN_TILE, seq_lat // TOKEN_TILE)

    x_ctx = x_prompt.reshape(t_ctx, d)
    x_lat = x_sample.reshape(t_lat, d)
    zh, za, zg = _input_projection(x_ctx, x_lat, mod3, w_in[layer].astype(BF16), lb, mod_map)

    norm_g = hg_norm_g[layer][None, :]
    oa_ctx, s_ctx = _hgrn_scan(zh, norm_g, None, batch=n_ctx, seq=seq_ctx, row_block0=0)
    oa_lat, _ = _hgrn_scan(zh, norm_g, state_hgrn[:, layer], batch=n_lat, seq=seq_lat,
                           row_block0=t_ctx // seq_lat)

    q_gain = jnp.tile(q_norm_g[layer], N_HEADS)[None, :]
    k_gain = jnp.tile(k_norm_g[layer], N_KV_HEADS)[None, :]
    lane = jnp.arange(ATT_WIDTH)
    group_ones = (lane[:, None] // HEAD_DIM == lane[None, :] // HEAD_DIM).astype(BF16)
    ob_ctx, k_ctx, v_ctx = _attention(za, q_gain, k_gain, group_ones, batch=n_ctx, seq=seq_ctx, row_block0=0)
    past = cache_k.shape[2]
    ob_lat = _attention(
        za, q_gain, k_gain, group_ones, batch=n_lat, seq=seq_lat, row_block0=t_ctx // seq_lat,
        rope=_rope_tables(seq_lat),
        cache=(cache_k[:, layer].reshape(n_lat, past, KV_WIDTH), cache_v[:, layer].reshape(n_lat, past, KV_WIDTH)))

    x1, h2 = _post_mixer(x_ctx, x_lat, mod3, oa_ctx, oa_lat, ob_ctx, ob_lat, zg, w_branch_a[layer].astype(BF16),
                         w_branch_b[layer].astype(BF16), w_out[layer].astype(BF16),
                         ln1_g[layer][None, :], ln1_b[layer][None, :], mod_map)

    wr_t = w_router[layer].T
    wr_hi = wr_t.astype(BF16)
    wr_lo = (wr_t - wr_hi.astype(F32)).astype(BF16)
    idx_t, w_t = _router(h2, wr_hi, wr_lo, router_bias[layer][:, None])

    routed = []
    for lo, hi in ((0, t_ctx), (t_ctx, t_ctx + t_lat)):
        lists = _dispatch_lists(idx_t[:, lo:hi], w_t[:, lo:hi])
        rows = h2[lo:hi].reshape((hi - lo) * SUBLANES, LANES)
        y = _routed_experts(rows, *lists, w_e_gate[layer], w_e_up[layer], w_e_down[layer])
        routed.append(y.reshape(hi - lo, d))

    y_ctx, y_lat = _final(x1, h2, routed[0], routed[1], mod3, w_s_gate[layer].astype(BF16),
                          w_s_up[layer].astype(BF16), w_s_down[layer].astype(BF16),
                          ln2_g[layer][None, :], ln2_b[layer][None, :], mod_map)

    y_prompt = y_ctx.reshape(n_ctx, seq_ctx, d)
    y_sample = y_lat.reshape(n_lat, seq_lat, d)
    new_cache_k = k_ctx.reshape(n_ctx, 1, seq_ctx, N_KV_HEADS, HEAD_DIM)
    new_cache_v = v_ctx.reshape(n_ctx, 1, seq_ctx, N_KV_HEADS, HEAD_DIM)
    new_state = s_ctx[:, None]
    return (y_prompt, y_sample, new_cache_k, new_cache_v, new_state)
```

```python
import functools

import jax
import jax.numpy as jnp
from jax import lax
from jax.experimental import pallas as pl
from jax.experimental.pallas import tpu as pltpu

F32 = jnp.float32
BF16 = jnp.bfloat16

GRID_W = 64
HG_HEADS = 4
HG_DK = 128
HG_DV = 128
HG_WIDTH = HG_HEADS * HG_DK
N_HEADS = 8
N_KV_HEADS = 2
HEAD_DIM = 64
ATT_WIDTH = N_HEADS * HEAD_DIM
KV_WIDTH = N_KV_HEADS * HEAD_DIM
ROPE_THETA = 10000.0
N_EXPERTS = 256
TOP_K = 8
N_GROUPS = 8
TOPK_GROUPS = 4
GROUP_SIZE = N_EXPERTS // N_GROUPS
ROUTED_SCALE = 2.5
NORM_EPS = 1e-6
DEPTH = 1
DEEPNORM_ALPHA = (2 * DEPTH) ** 0.25

LANES = 128
SUBLANES = 8
VMEM_LIMIT = 56 * 1024 * 1024

TOKEN_TILE = 256
HG_CHUNK = 32
ATT_Q_BLOCK = 128
MOE_BLOCK = 320
ROW_STRIDE = MOE_BLOCK + 1
ROW_GROUP = 8
ROW_SEGMENT = 32


def _sigmoid(x):
    return 1.0 / (1.0 + jnp.exp(-x))


def _silu(x):
    return x * _sigmoid(x)


def _params(sem=None):
    return pltpu.CompilerParams(dimension_semantics=sem, vmem_limit_bytes=VMEM_LIMIT)


def _mod_kernel(c_ref, w_ref, b_ref, o_ref):
    s = _silu(c_ref[...]).astype(BF16)
    o_ref[...] = jnp.dot(s, w_ref[...], preferred_element_type=F32) + b_ref[...]


def _modulation(cond, w_mod, b_mod):
    n, d = cond.shape
    width = w_mod.shape[1]
    tn = width // 4
    return pl.pallas_call(
        _mod_kernel,
        grid=(4,),
        in_specs=[
            pl.BlockSpec((n, d), lambda j: (0, 0)),
            pl.BlockSpec((d, tn), lambda j: (0, j)),
            pl.BlockSpec((1, tn), lambda j: (0, j)),
        ],
        out_specs=pl.BlockSpec((n, tn), lambda j: (0, j)),
        out_shape=jax.ShapeDtypeStruct((n, width), F32),
        compiler_params=_params(("arbitrary",)),
    )(cond, w_mod, b_mod)


def _inproj_kernel(xc_ref, xl_ref, mod_ref, w_ref, lb_ref, zh_ref, za_ref, zg_ref, *, d, n_ctx_tiles):
    shift = mod_ref[0, :, 0:d]
    scale = mod_ref[0, :, d:2 * d]
    x = _pick(pl.program_id(0) < n_ctx_tiles, xc_ref, xl_ref)
    h = (x * (1.0 + scale) + shift).astype(BF16)

    def proj(lo, hi):
        return jnp.dot(h, w_ref[:, lo:hi], preferred_element_type=F32)

    w = HG_WIDTH
    zh_ref[:, 0:w] = _silu(proj(0, w))
    zh_ref[:, w:2 * w] = proj(w, 2 * w)
    for i in range(2):
        lb = lb_ref[i:i + 1, :]
        zh_ref[:, (2 + i) * w:(3 + i) * w] = lb + (1.0 - lb) * _sigmoid(proj((2 + i) * w, (3 + i) * w))
    zh_ref[:, 4 * w:5 * w] = _silu(proj(4 * w, 5 * w))
    a0 = 5 * w
    a1 = a0 + ATT_WIDTH + 2 * KV_WIDTH
    za_ref[...] = proj(a0, a1)
    for i in range(4):
        lo = a1 + i * (d // 2)
        zg_ref[:, i * (d // 2):(i + 1) * (d // 2)] = _sigmoid(proj(lo, lo + d // 2))


def _mod_row_map(n_ctx_tiles, tiles_per_latent):
    def index_map(i):
        row = jnp.where(i < n_ctx_tiles, 0, 1 + (i - n_ctx_tiles) // tiles_per_latent)
        return (row, 0, 0)
    return index_map


def _pair_specs(width, n_ctx_tiles):
    return [pl.BlockSpec((TOKEN_TILE, width), lambda i: (jnp.minimum(i, n_ctx_tiles - 1), 0)),
            pl.BlockSpec((TOKEN_TILE, width), lambda i: (jnp.maximum(i - n_ctx_tiles, 0), 0))]


def _pick(is_ctx, ctx_ref, lat_ref):
    return jnp.where(is_ctx, ctx_ref[...], lat_ref[...])


def _input_projection(x_ctx, x_lat, mod3, w_in, lb, mod_map):
    d = x_ctx.shape[1]
    n_ctx_tiles = x_ctx.shape[0] // TOKEN_TILE
    t = x_ctx.shape[0] + x_lat.shape[0]
    width = w_in.shape[1]
    zh_w = 5 * HG_WIDTH
    za_w = ATT_WIDTH + 2 * KV_WIDTH
    zg_w = 2 * d
    assert width == zh_w + za_w + zg_w
    tm = TOKEN_TILE
    return pl.pallas_call(
        functools.partial(_inproj_kernel, d=d, n_ctx_tiles=n_ctx_tiles),
        grid=(t // tm,),
        in_specs=_pair_specs(d, n_ctx_tiles) + [
            pl.BlockSpec((1, 1, mod3.shape[2]), mod_map),
            pl.BlockSpec((d, width), lambda i: (0, 0)),
            pl.BlockSpec((2, HG_WIDTH), lambda i: (0, 0)),
        ],
        out_specs=[
            pl.BlockSpec((tm, zh_w), lambda i: (i, 0)),
            pl.BlockSpec((tm, za_w), lambda i: (i, 0)),
            pl.BlockSpec((tm, zg_w), lambda i: (i, 0)),
        ],
        out_shape=[
            jax.ShapeDtypeStruct((t, zh_w), F32),
            jax.ShapeDtypeStruct((t, za_w), F32),
            jax.ShapeDtypeStruct((t, zg_w), F32),
        ],
        compiler_params=_params(("arbitrary",)),
    )(x_ctx, x_lat, mod3, w_in, lb)


def _hgrn_kernel(*refs, seq, has_state):
    if has_state:
        q_ref, v_ref, ff_ref, fb_ref, gs_ref, ng_ref, s0_ref, o_ref, sout_ref = refs
    else:
        q_ref, v_ref, ff_ref, fb_ref, gs_ref, ng_ref, o_ref, sout_ref = refs
        s0_ref = None
    c = HG_CHUNK
    n = seq // c
    q3 = q_ref[...].reshape(n, c, HG_DK)
    v3 = v_ref[...].astype(BF16).reshape(n, c, HG_DV)
    pos = lax.broadcasted_iota(jnp.int32, (seq, HG_DK), 0) % c
    t_idx = lax.broadcasted_iota(jnp.int32, (c, c), 0)
    s_idx = lax.broadcasted_iota(jnp.int32, (c, c), 1)
    o_sum = None
    for direction, f_ref in enumerate((ff_ref, fb_ref)):
        reverse = direction == 1
        f = f_ref[...]
        k3 = (1.0 - f).reshape(n, c, HG_DK)
        b = jnp.log(f)
        step = 1
        while step < c:
            if reverse:
                b = b + jnp.where(pos < c - step, pltpu.roll(b, seq - step, axis=0), 0.0)
            else:
                b = b + jnp.where(pos >= step, pltpu.roll(b, step, axis=0), 0.0)
            step *= 2
        b3 = b.reshape(n, c, HG_DK)
        edge = b3[:, 0:1, :] if reverse else b3[:, c - 1:c, :]
        mid = b3[:, c // 2:c // 2 + 1, :]
        q_in = (q3 * jnp.exp(b3)).astype(BF16)
        k_end = (k3 * jnp.exp(edge - b3)).astype(BF16)
        q_mid = (q3 * jnp.exp(b3 - mid)).astype(BF16)
        k_mid = (k3 * jnp.exp(mid - b3)).astype(BF16)
        scores = jnp.einsum('ntd,nsd->nts', q_mid, k_mid, preferred_element_type=F32)
        keep = (s_idx >= t_idx) if reverse else (s_idx <= t_idx)
        scores = jnp.where(keep[None], scores, 0.0).astype(BF16)
        o_intra = jnp.einsum('nts,nsv->ntv', scores, v3, preferred_element_type=F32)
        upd = jnp.einsum('nsv,nsd->nvd', v3, k_end, preferred_element_type=F32)
        dec = jnp.exp(edge)
        if has_state:
            st = s0_ref[0, direction, 0].T
        else:
            st = jnp.zeros((HG_DV, HG_DK), F32)
        before = [None] * n
        for ci in (range(n - 1, -1, -1) if reverse else range(n)):
            before[ci] = st.astype(BF16)
            st = st * dec[ci] + upd[ci]
        st_before = jnp.stack(before, axis=0)
        o_inter = jnp.einsum('ntd,nvd->ntv', q_in, st_before, preferred_element_type=F32)
        o_dir = (o_intra + o_inter).reshape(seq, HG_DV)
        o_sum = o_dir if o_sum is None else o_sum + o_dir
        sout_ref[0, direction, 0] = st.T
    ms = jnp.mean(o_sum * o_sum, axis=-1, keepdims=True)
    o = o_sum * lax.rsqrt(ms + NORM_EPS) * ng_ref[...]
    o_ref[...] = (o * gs_ref[...]).astype(o_ref.dtype)


def _hgrn_scan(zh, norm_g, s0, *, batch, seq, row_block0):
    has_state = s0 is not None
    h = HG_HEADS

    def col(section):
        return pl.BlockSpec((seq, HG_DK), lambda b, j: (row_block0 + b, section * h + j))

    in_specs = [col(0), col(1), col(2), col(3), col(4), pl.BlockSpec((1, HG_DV), lambda b, j: (0, 0))]
    args = [zh, zh, zh, zh, zh, norm_g]
    state_spec = pl.BlockSpec((1, 2, 1, HG_DK, HG_DV), lambda b, j: (b, 0, j, 0, 0))
    if has_state:
        in_specs.append(state_spec)
        args.append(s0)
    return pl.pallas_call(
        functools.partial(_hgrn_kernel, seq=seq, has_state=has_state),
        grid=(batch, h),
        in_specs=in_specs,
        out_specs=[pl.BlockSpec((seq, HG_DV), lambda b, j: (b, j)), state_spec],
        out_shape=[
            jax.ShapeDtypeStruct((batch * seq, h * HG_DV), BF16),
            jax.ShapeDtypeStruct((batch, 2, h, HG_DK, HG_DV), F32),
        ],
        compiler_params=_params(("arbitrary", "arbitrary")),
    )(*args)


def _group_rms_norm(x, group_ones, gain):
    sq = x * x
    hi = sq.astype(BF16)
    lo = (sq - hi.astype(F32)).astype(BF16)
    total = (jnp.dot(hi, group_ones, preferred_element_type=F32)
             + jnp.dot(lo, group_ones, preferred_element_type=F32))
    return x * lax.rsqrt(total * (1.0 / HEAD_DIM) + NORM_EPS) * gain


def _rope(x, cos, sin_signed):
    width = x.shape[1]
    quarter = HEAD_DIM // 4
    lane = lax.broadcasted_iota(jnp.int32, x.shape, 1)
    partner = jnp.where(lane % (2 * quarter) < quarter,
                        pltpu.roll(x, width - quarter, axis=1),
                        pltpu.roll(x, quarter, axis=1))
    return x * cos + partner * sin_signed


def _attn_kernel(*refs, seq, latent):
    if latent:
        za_ref, qg_ref, kg_ref, gm_ref, cos_ref, sin_ref, ck_ref, cv_ref, o_ref = refs
    else:
        za_ref, qg_ref, kg_ref, gm_ref, o_ref, kout_ref, vout_ref = refs
    q = za_ref[:, 0:ATT_WIDTH]
    k = za_ref[:, ATT_WIDTH:ATT_WIDTH + KV_WIDTH]
    v = za_ref[:, ATT_WIDTH + KV_WIDTH:ATT_WIDTH + 2 * KV_WIDTH]
    qn = _group_rms_norm(q, gm_ref[...], qg_ref[...])
    kn = _group_rms_norm(k, gm_ref[0:KV_WIDTH, 0:KV_WIDTH], kg_ref[...])
    if latent:
        qn = _rope(qn, cos_ref[...], sin_ref[...])
        kr = _rope(kn, cos_ref[:, 0:KV_WIDTH], sin_ref[:, 0:KV_WIDTH])
    else:
        kout_ref[...] = kn
        vout_ref[...] = v
        kr = kn
    qb = (qn * (HEAD_DIM ** -0.5)).astype(BF16)
    kb = kr.astype(BF16)
    vb = v.astype(BF16)
    if latent:
        ckb = ck_ref[0].astype(BF16)
        cvb = cv_ref[0].astype(BF16)
    groups = N_HEADS // N_KV_HEADS
    tq = ATT_Q_BLOCK if latent else seq
    nt = (((1,), (1,)), ((), ()))
    for kh in range(N_KV_HEADS):
        ksl = slice(kh * HEAD_DIM, (kh + 1) * HEAD_DIM)
        k_new = kb[:, ksl]
        v_new = vb[:, ksl]
        for blk in range(seq // tq):
            rows = slice(blk * tq, (blk + 1) * tq)
            heads = [kh * groups + g for g in range(groups)]
            q_st = jnp.concatenate([qb[rows, hd * HEAD_DIM:(hd + 1) * HEAD_DIM] for hd in heads], axis=0)
            s_new = lax.dot_general(q_st, k_new, nt, preferred_element_type=F32)
            m = jnp.max(s_new, axis=-1, keepdims=True)
            if latent:
                s_ctx = lax.dot_general(q_st, ckb[:, ksl], nt, preferred_element_type=F32)
                m = jnp.maximum(m, jnp.max(s_ctx, axis=-1, keepdims=True))
            p_new = jnp.exp(s_new - m)
            denom = jnp.sum(p_new, axis=-1, keepdims=True)
            acc = jnp.dot(p_new.astype(BF16), v_new, preferred_element_type=F32)
            if latent:
                p_ctx = jnp.exp(s_ctx - m)
                denom = denom + jnp.sum(p_ctx, axis=-1, keepdims=True)
                acc = acc + jnp.dot(p_ctx.astype(BF16), cvb[:, ksl], preferred_element_type=F32)
            out = acc / denom
            for g in range(0, groups, 2):
                pair = jnp.concatenate([out[g * tq:(g + 1) * tq], out[(g + 1) * tq:(g + 2) * tq]], axis=1)
                lane0 = heads[g] * HEAD_DIM
                o_ref[rows, lane0:lane0 + 2 * HEAD_DIM] = pair.astype(o_ref.dtype)


def _attention(za, q_gain, k_gain, group_ones, *, batch, seq, row_block0, rope=None, cache=None):
    latent = cache is not None
    za_w = za.shape[1]
    in_specs = [
        pl.BlockSpec((seq, za_w), lambda b: (row_block0 + b, 0)),
        pl.BlockSpec((1, ATT_WIDTH), lambda b: (0, 0)),
        pl.BlockSpec((1, KV_WIDTH), lambda b: (0, 0)),
        pl.BlockSpec((ATT_WIDTH, ATT_WIDTH), lambda b: (0, 0)),
    ]
    args = [za, q_gain, k_gain, group_ones]
    o_spec = pl.BlockSpec((seq, ATT_WIDTH), lambda b: (b, 0))
    o_shape = jax.ShapeDtypeStruct((batch * seq, ATT_WIDTH), BF16)
    if latent:
        cos, sin_signed = rope
        ck, cv = cache
        past = ck.shape[1]
        in_specs += [
            pl.BlockSpec((seq, ATT_WIDTH), lambda b: (0, 0)),
            pl.BlockSpec((seq, ATT_WIDTH), lambda b: (0, 0)),
            pl.BlockSpec((1, past, KV_WIDTH), lambda b: (b, 0, 0)),
            pl.BlockSpec((1, past, KV_WIDTH), lambda b: (b, 0, 0)),
        ]
        args += [cos, sin_signed, ck, cv]
        out_specs = o_spec
        out_shape = o_shape
    else:
        kv_spec = pl.BlockSpec((seq, KV_WIDTH), lambda b: (b, 0))
        kv_shape = jax.ShapeDtypeStruct((batch * seq, KV_WIDTH), F32)
        out_specs = [o_spec, kv_spec, kv_spec]
        out_shape = [o_shape, kv_shape, kv_shape]
    return pl.pallas_call(
        functools.partial(_attn_kernel, seq=seq, latent=latent),
        grid=(batch,),
        in_specs=in_specs,
        out_specs=out_specs,
        out_shape=out_shape,
        compiler_params=_params(("arbitrary",)),
    )(*args)


def _rope_tables(seq):
    quarter = HEAD_DIM // 4
    t = jnp.arange(seq)
    row = (t // GRID_W).astype(F32)
    colp = (t % GRID_W).astype(F32)
    inv_freq = ROPE_THETA ** (-jnp.arange(quarter, dtype=F32) / quarter)
    lane = jnp.arange(HEAD_DIM)
    pos = jnp.where((lane < HEAD_DIM // 2)[None, :], row[:, None], colp[:, None])
    ang = pos * inv_freq[lane % quarter][None, :]
    sign = jnp.where(lane % (2 * quarter) < quarter, -1.0, 1.0)[None, :]
    cos = jnp.tile(jnp.cos(ang), (1, N_HEADS))
    sin_signed = jnp.tile(jnp.sin(ang) * sign, (1, N_HEADS))
    return cos, sin_signed


def _layer_norm(y, g, b):
    mu = jnp.mean(y, axis=-1, keepdims=True)
    yc = y - mu
    var = jnp.mean(yc * yc, axis=-1, keepdims=True)
    return yc * lax.rsqrt(var + NORM_EPS) * g + b


def _post_mixer_kernel(xc_ref, xl_ref, mod_ref, oac_ref, oal_ref, obc_ref, obl_ref, zg_ref, wa_ref, wb_ref,
                       wo_ref, g_ref, b_ref, x1_ref, h2_ref, *, d, n_ctx_tiles):
    is_ctx = pl.program_id(0) < n_ctx_tiles
    gate1 = mod_ref[0, :, 2 * d:3 * d]
    shift2 = mod_ref[0, :, 3 * d:4 * d]
    scale2 = mod_ref[0, :, 4 * d:5 * d]
    branch_a = jnp.dot(_pick(is_ctx, oac_ref, oal_ref), wa_ref[...], preferred_element_type=F32)
    branch_b = jnp.dot(_pick(is_ctx, obc_ref, obl_ref), wb_ref[...], preferred_element_type=F32)
    merged = zg_ref[:, 0:d] * branch_a + zg_ref[:, d:2 * d] * branch_b
    mix = jnp.dot(merged.astype(BF16), wo_ref[...], preferred_element_type=F32)
    x = _pick(is_ctx, xc_ref, xl_ref)
    x1 = _layer_norm(DEEPNORM_ALPHA * x + gate1 * mix, g_ref[...], b_ref[...])
    x1_ref[...] = x1
    h2_ref[...] = x1 * (1.0 + scale2) + shift2


def _post_mixer(x_ctx, x_lat, mod3, oa_ctx, oa_lat, ob_ctx, ob_lat, zg, w_a, w_b, w_o, ln_g, ln_b, mod_map):
    d = x_ctx.shape[1]
    n_ctx_tiles = x_ctx.shape[0] // TOKEN_TILE
    t = x_ctx.shape[0] + x_lat.shape[0]
    tm = TOKEN_TILE
    row = lambda i: (i, 0)
    full = lambda i: (0, 0)
    return pl.pallas_call(
        functools.partial(_post_mixer_kernel, d=d, n_ctx_tiles=n_ctx_tiles),
        grid=(t // tm,),
        in_specs=_pair_specs(d, n_ctx_tiles) + [
            pl.BlockSpec((1, 1, mod3.shape[2]), mod_map),
        ] + _pair_specs(oa_ctx.shape[1], n_ctx_tiles) + _pair_specs(ob_ctx.shape[1], n_ctx_tiles) + [
            pl.BlockSpec((tm, 2 * d), row),
            pl.BlockSpec(w_a.shape, full),
            pl.BlockSpec(w_b.shape, full),
            pl.BlockSpec(w_o.shape, full),
            pl.BlockSpec((1, d), full),
            pl.BlockSpec((1, d), full),
        ],
        out_specs=[pl.BlockSpec((tm, d), row), pl.BlockSpec((tm, d), row)],
        out_shape=[jax.ShapeDtypeStruct((t, d), F32), jax.ShapeDtypeStruct((t, d), F32)],
        compiler_params=_params(("arbitrary",)),
    )(x_ctx, x_lat, mod3, oa_ctx, oa_lat, ob_ctx, ob_lat, zg, w_a, w_b, w_o, ln_g, ln_b)


def _first_index_of_max(x, idx, sentinel):
    m = jnp.max(x, axis=0, keepdims=True)
    first = jnp.min(jnp.where(x == m, idx, sentinel), axis=0, keepdims=True)
    return m, first


def _router_kernel(h_ref, wh_ref, wl_ref, bias_ref, idx_ref, w_ref):
    h = h_ref[...]
    hh = h.astype(BF16)
    hl = (h - hh.astype(F32)).astype(BF16)
    nt = (((1,), (1,)), ((), ()))
    wh = wh_ref[...]
    logits = (lax.dot_general(wh, hh, nt, preferred_element_type=F32)
              + lax.dot_general(wh, hl, nt, preferred_element_type=F32)
              + lax.dot_general(wl_ref[...], hh, nt, preferred_element_type=F32))
    scores = _sigmoid(logits)
    sel = scores + bias_ref[...]
    tm = sel.shape[1]
    neg = -jnp.inf
    gidx = lax.broadcasted_iota(jnp.int32, (GROUP_SIZE, tm), 0)
    group_scores = []
    for g in range(N_GROUPS):
        sg = sel[g * GROUP_SIZE:(g + 1) * GROUP_SIZE, :]
        m1, first = _first_index_of_max(sg, gidx, GROUP_SIZE)
        m2 = jnp.max(jnp.where(gidx == first, neg, sg), axis=0, keepdims=True)
        group_scores.append(m1 + m2)
    gs = jnp.concatenate(group_scores, axis=0)
    nidx = lax.broadcasted_iota(jnp.int32, (N_GROUPS, tm), 0)
    chosen = jnp.zeros((N_GROUPS, tm), jnp.bool_)
    for _ in range(TOPK_GROUPS):
        _, first = _first_index_of_max(gs, nidx, N_GROUPS)
        hit = nidx == first
        chosen = jnp.logical_or(chosen, hit)
        gs = jnp.where(hit, neg, gs)
    masked = jnp.concatenate(
        [jnp.where(chosen[g:g + 1, :], sel[g * GROUP_SIZE:(g + 1) * GROUP_SIZE, :], neg) for g in range(N_GROUPS)],
        axis=0)
    eidx = lax.broadcasted_iota(jnp.int32, (N_EXPERTS, tm), 0)
    picks, weights = [], []
    for _ in range(TOP_K):
        _, first = _first_index_of_max(masked, eidx, N_EXPERTS)
        hit = eidx == first
        picks.append(first)
        weights.append(jnp.sum(jnp.where(hit, scores, 0.0), axis=0, keepdims=True))
        masked = jnp.where(hit, neg, masked)
    wk = jnp.concatenate(weights, axis=0)
    idx_ref[...] = jnp.concatenate(picks, axis=0)
    w_ref[...] = wk / jnp.sum(wk, axis=0, keepdims=True) * ROUTED_SCALE


def _router(h2, wr_hi, wr_lo, bias_col):
    t, d = h2.shape
    tm = TOKEN_TILE
    return pl.pallas_call(
        _router_kernel,
        grid=(t // tm,),
        in_specs=[
            pl.BlockSpec((tm, d), lambda i: (i, 0)),
            pl.BlockSpec((N_EXPERTS, d), lambda i: (0, 0)),
            pl.BlockSpec((N_EXPERTS, d), lambda i: (0, 0)),
            pl.BlockSpec((N_EXPERTS, 1), lambda i: (0, 0)),
        ],
        out_specs=[pl.BlockSpec((TOP_K, tm), lambda i: (0, i)), pl.BlockSpec((TOP_K, tm), lambda i: (0, i))],
        out_shape=[jax.ShapeDtypeStruct((TOP_K, t), jnp.int32), jax.ShapeDtypeStruct((TOP_K, t), F32)],
        compiler_params=_params(("arbitrary",)),
    )(h2, wr_hi, wr_lo, bias_col)


def _dispatch_lists(idx_t, w_t):
    k, t = idx_t.shape
    s = k * t
    n_blocks = s // MOE_BLOCK + N_EXPERTS
    flat_tok = jnp.arange(s, dtype=jnp.int32) % t
    sorted_e, sorted_tok, sorted_w = lax.sort((idx_t.reshape(s), flat_tok, w_t.reshape(s)), num_keys=1)
    experts = jnp.arange(N_EXPERTS + 1, dtype=jnp.int32)
    start = jnp.sum(sorted_e[None, :] < experts[:, None], axis=1, dtype=jnp.int32)
    counts = start[1:] - start[:-1]
    blocks_per_e = (counts + MOE_BLOCK - 1) // MOE_BLOCK
    block_end = jnp.cumsum(blocks_per_e)
    n_used = block_end[-1]
    blk = jnp.arange(n_blocks, dtype=jnp.int32)
    block_e = jnp.sum(block_end[None, :] <= blk[:, None], axis=1, dtype=jnp.int32)
    block_e = jnp.minimum(block_e, N_EXPERTS - 1)
    onehot = (block_e[:, None] == experts[None, :N_EXPERTS]).astype(jnp.int32)
    pick = lambda a: jnp.sum(onehot * a[None, :], axis=1)
    offset = (blk - (pick(block_end) - pick(blocks_per_e))) * MOE_BLOCK
    row_start = pick(start[:-1]) + offset
    n_rows = jnp.clip(pick(counts) - offset, 0, MOE_BLOCK)
    used = blk < n_used
    last_e = jnp.sum(jnp.where(blk == n_used - 1, block_e, 0))
    block_e = jnp.where(used, block_e, last_e)
    row_start = jnp.where(used, row_start, 0)
    n_rows = jnp.where(used, n_rows, 0)
    return block_e, row_start, n_rows, (n_used - 1).reshape(1), sorted_tok, sorted_w


def _slab_row(tok):
    return pl.multiple_of(tok * SUBLANES, SUBLANES)


def _expert_kernel(be_ref, rs_ref, nr_ref, last_ref, tok_ref, x_hbm, wg_ref, wu_ref, wd_ref, out_ref,
                   xs, xt, sem, *, n_slots):
    b = pl.program_id(0)
    chunks = wg_ref.shape[1] // LANES
    n_rows = nr_ref[b]

    @pl.when(b == 0)
    def _load():
        cp = pltpu.make_async_copy(x_hbm, xs, sem)
        cp.start()
        cp.wait()
        xt[...] = jnp.zeros(xt.shape, F32)

    @pl.when(n_rows > 0)
    def _block():
        base = rs_ref[b]
        for seg in range(MOE_BLOCK // ROW_SEGMENT):
            @pl.when(n_rows > seg * ROW_SEGMENT)
            def _gather(seg=seg):
                for r in range(seg * ROW_SEGMENT, (seg + 1) * ROW_SEGMENT):
                    tok = tok_ref[jnp.minimum(base + r, n_slots - 1)]
                    xt[pl.ds(r, chunks, stride=ROW_STRIDE), :] = xs[pl.ds(_slab_row(tok), SUBLANES), :]

        x = jnp.concatenate(
            [xt[j * ROW_STRIDE:j * ROW_STRIDE + MOE_BLOCK, :] for j in range(chunks)], axis=1).astype(BF16)
        gate = jnp.dot(x, wg_ref[0].astype(BF16), preferred_element_type=F32)
        up = jnp.dot(x, wu_ref[0].astype(BF16), preferred_element_type=F32)
        hidden = (_silu(gate) * up).astype(BF16)
        out_ref[...] = jnp.dot(hidden, wd_ref[0].astype(BF16), preferred_element_type=F32)


def _combine_kernel(rs_ref, nr_ref, last_ref, tok_ref, w_ref, o_ref, y_hbm, ys, ot, sem, *, t, n_slots, n_blocks):
    b = pl.program_id(0)
    rows = t * SUBLANES
    chunks = o_ref.shape[1] // LANES
    n_rows = nr_ref[b]
    group = ROW_GROUP

    @pl.when(b == 0)
    def _init():
        ys[...] = jnp.zeros(ys.shape, F32)

    @pl.when(n_rows > 0)
    def _block():
        base = rs_ref[b]
        for j in range(chunks):
            ot[j * ROW_STRIDE:j * ROW_STRIDE + MOE_BLOCK, :] = o_ref[:, j * LANES:(j + 1) * LANES]
        for seg in range(MOE_BLOCK // ROW_SEGMENT):
            @pl.when(n_rows > seg * ROW_SEGMENT)
            def _scatter(seg=seg):
                for r0 in range(seg * ROW_SEGMENT, (seg + 1) * ROW_SEGMENT, group):
                    slot = [jnp.minimum(base + r0 + i, n_slots - 1) for i in range(group)]
                    toks = [_slab_row(jnp.where(r0 + i < n_rows, tok_ref[slot[i]], t)) for i in range(group)]
                    vals = [ys[pl.ds(toks[i], SUBLANES), :]
                            + w_ref[slot[i]] * ot[pl.ds(r0 + i, chunks, stride=ROW_STRIDE), :]
                            for i in range(group)]
                    for i in range(group):
                        ys[pl.ds(toks[i], SUBLANES), :] = vals[i]

    @pl.when(b == n_blocks - 1)
    def _store():
        cp = pltpu.make_async_copy(ys.at[pl.ds(0, rows)], y_hbm, sem)
        cp.start()
        cp.wait()


def _routed_experts(h2_rows, block_e, row_start, n_rows, last_block, sorted_tok, sorted_w, w_gate, w_up, w_down):
    rows, lanes = h2_rows.shape
    t = rows // SUBLANES
    n_blocks = block_e.shape[0]
    n_slots = sorted_tok.shape[0]
    d, e_dim = w_gate.shape[1], w_gate.shape[2]
    assert d == SUBLANES * LANES and lanes == LANES
    staging = pltpu.VMEM((SUBLANES * ROW_STRIDE, LANES), F32)
    expert_out = pl.pallas_call(
        functools.partial(_expert_kernel, n_slots=n_slots),
        grid_spec=pltpu.PrefetchScalarGridSpec(
            num_scalar_prefetch=5,
            grid=(n_blocks,),
            in_specs=[
                pl.BlockSpec(memory_space=pl.ANY),
                pl.BlockSpec((1, d, e_dim), lambda b, be, *_: (be[b], 0, 0)),
                pl.BlockSpec((1, d, e_dim), lambda b, be, *_: (be[b], 0, 0)),
                pl.BlockSpec((1, e_dim, d), lambda b, be, *_: (be[b], 0, 0)),
            ],
            out_specs=pl.BlockSpec((MOE_BLOCK, d), lambda b, be, rs, nr, last, tok: (jnp.minimum(b, last[0]), 0)),
            scratch_shapes=[pltpu.VMEM((rows, LANES), F32), staging, pltpu.SemaphoreType.DMA(())],
        ),
        out_shape=jax.ShapeDtypeStruct((n_blocks * MOE_BLOCK, d), F32),
        compiler_params=_params(("arbitrary",)),
    )(block_e, row_start, n_rows, last_block, sorted_tok, h2_rows, w_gate, w_up, w_down)
    return pl.pallas_call(
        functools.partial(_combine_kernel, t=t, n_slots=n_slots, n_blocks=n_blocks),
        grid_spec=pltpu.PrefetchScalarGridSpec(
            num_scalar_prefetch=5,
            grid=(n_blocks,),
            in_specs=[pl.BlockSpec((MOE_BLOCK, d), lambda b, rs, nr, last, tok, w: (jnp.minimum(b, last[0]), 0))],
            out_specs=pl.BlockSpec(memory_space=pl.ANY),
            scratch_shapes=[pltpu.VMEM((rows + SUBLANES, LANES), F32), staging, pltpu.SemaphoreType.DMA(())],
        ),
        out_shape=jax.ShapeDtypeStruct((rows, LANES), F32),
        compiler_params=_params(("arbitrary",)),
    )(row_start, n_rows, last_block, sorted_tok, sorted_w, expert_out)


def _final_kernel(x1_ref, h2_ref, routed_ref, mod_ref, wg_ref, wu_ref, wd_ref, g_ref, b_ref,
                  yc_ref, yl_ref, *, d, n_ctx_tiles):
    is_ctx = pl.program_id(0) < n_ctx_tiles
    gate2 = mod_ref[0, :, 5 * d:6 * d]
    h = h2_ref[...].astype(BF16)
    gate = jnp.dot(h, wg_ref[...], preferred_element_type=F32)
    up = jnp.dot(h, wu_ref[...], preferred_element_type=F32)
    shared = jnp.dot((_silu(gate) * up).astype(BF16), wd_ref[...], preferred_element_type=F32)
    ffn = routed_ref[...] + shared
    y = _layer_norm(DEEPNORM_ALPHA * x1_ref[...] + gate2 * ffn, g_ref[...], b_ref[...])

    @pl.when(is_ctx)
    def _():
        yc_ref[...] = y

    @pl.when(jnp.logical_not(is_ctx))
    def _():
        yl_ref[...] = y


def _final(x1, h2, routed, mod3, w_g, w_u, w_d, ln_g, ln_b, mod_map, t_ctx):
    t, d = x1.shape
    t_lat = t - t_ctx
    n_ctx_tiles = t_ctx // TOKEN_TILE
    tm = TOKEN_TILE
    row = lambda i: (i, 0)
    full = lambda i: (0, 0)
    return pl.pallas_call(
        functools.partial(_final_kernel, d=d, n_ctx_tiles=n_ctx_tiles),
        grid=(t // tm,),
        in_specs=[
            pl.BlockSpec((tm, d), row),
            pl.BlockSpec((tm, d), row),
            pl.BlockSpec((tm, d), row),
            pl.BlockSpec((1, 1, mod3.shape[2]), mod_map),
            pl.BlockSpec(w_g.shape, full),
            pl.BlockSpec(w_u.shape, full),
            pl.BlockSpec(w_d.shape, full),
            pl.BlockSpec((1, d), full),
            pl.BlockSpec((1, d), full),
        ],
        out_specs=_pair_specs(d, n_ctx_tiles),
        out_shape=[jax.ShapeDtypeStruct((t_ctx, d), F32), jax.ShapeDtypeStruct((t_lat, d), F32)],
        compiler_params=_params(("arbitrary",)),
    )(x1, h2, routed, mod3, w_g, w_u, w_d, ln_g, ln_b)


def kernel(x_prompt, x_sample, cache_k, cache_v, state_hgrn, c, c_ctx, w_mod, b_mod, w_in, hg_lb, hg_norm_g, q_norm_g, k_norm_g, w_branch_a, w_branch_b, w_out, ln1_g, ln1_b, w_router, router_bias, w_e_gate, w_e_up, w_e_down, w_s_gate, w_s_up, w_s_down, ln2_g, ln2_b):
    assert w_mod.shape[0] == DEPTH
    n_ctx, seq_ctx, d = x_prompt.shape
    n_lat, seq_lat, _ = x_sample.shape
    t_ctx = n_ctx * seq_ctx
    t_lat = n_lat * seq_lat
    assert seq_ctx == TOKEN_TILE and seq_lat % TOKEN_TILE == 0
    layer = 0

    lb = jnp.cumsum(jax.nn.softmax(hg_lb.astype(F32), axis=0), axis=0)[layer]

    cond = jnp.concatenate([c_ctx[None, :], c], axis=0)
    cond = jnp.pad(cond, ((0, (-cond.shape[0]) % SUBLANES), (0, 0)))
    mod = _modulation(cond, w_mod[layer].astype(BF16), b_mod[layer][None, :])
    mod3 = mod.reshape(mod.shape[0], 1, mod.shape[1])
    mod_map = _mod_row_map(t_ctx // TOKEN_TILE, seq_lat // TOKEN_TILE)

    x_ctx = x_prompt.reshape(t_ctx, d)
    x_lat = x_sample.reshape(t_lat, d)
    zh, za, zg = _input_projection(x_ctx, x_lat, mod3, w_in[layer].astype(BF16), lb, mod_map)

    norm_g = hg_norm_g[layer][None, :]
    oa_ctx, s_ctx = _hgrn_scan(zh, norm_g, None, batch=n_ctx, seq=seq_ctx, row_block0=0)
    oa_lat, _ = _hgrn_scan(zh, norm_g, state_hgrn[:, layer], batch=n_lat, seq=seq_lat,
                           row_block0=t_ctx // seq_lat)

    q_gain = jnp.tile(q_norm_g[layer], N_HEADS)[None, :]
    k_gain = jnp.tile(k_norm_g[layer], N_KV_HEADS)[None, :]
    lane = jnp.arange(ATT_WIDTH)
    group_ones = (lane[:, None] // HEAD_DIM == lane[None, :] // HEAD_DIM).astype(BF16)
    ob_ctx, k_ctx, v_ctx = _attention(za, q_gain, k_gain, group_ones, batch=n_ctx, seq=seq_ctx, row_block0=0)
    past = cache_k.shape[2]
    ob_lat = _attention(
        za, q_gain, k_gain, group_ones, batch=n_lat, seq=seq_lat, row_block0=t_ctx // seq_lat,
        rope=_rope_tables(seq_lat),
        cache=(cache_k[:, layer].reshape(n_lat, past, KV_WIDTH), cache_v[:, layer].reshape(n_lat, past, KV_WIDTH)))

    x1, h2 = _post_mixer(x_ctx, x_lat, mod3, oa_ctx, oa_lat, ob_ctx, ob_lat, zg, w_branch_a[layer].astype(BF16),
                         w_branch_b[layer].astype(BF16), w_out[layer].astype(BF16),
                         ln1_g[layer][None, :], ln1_b[layer][None, :], mod_map)

    wr_t = w_router[layer].T
    wr_hi = wr_t.astype(BF16)
    wr_lo = (wr_t - wr_hi.astype(F32)).astype(BF16)
    idx_t, w_t = _router(h2, wr_hi, wr_lo, router_bias[layer][:, None])

    t_all = t_ctx + t_lat
    routed = _routed_experts(h2.reshape(t_all * SUBLANES, LANES), *_dispatch_lists(idx_t, w_t),
                             w_e_gate[layer], w_e_up[layer], w_e_down[layer]).reshape(t_all, d)

    y_ctx, y_lat = _final(x1, h2, routed, mod3, w_s_gate[layer].astype(BF16),
                          w_s_up[layer].astype(BF16), w_s_down[layer].astype(BF16),
                          ln2_g[layer][None, :], ln2_b[layer][None, :], mod_map, t_ctx)

    y_prompt = y_ctx.reshape(n_ctx, seq_ctx, d)
    y_sample = y_lat.reshape(n_lat, seq_lat, d)
    new_cache_k = k_ctx.reshape(n_ctx, 1, seq_ctx, N_KV_HEADS, HEAD_DIM)
    new_cache_v = v_ctx.reshape(n_ctx, 1, seq_ctx, N_KV_HEADS, HEAD_DIM)
    new_state = s_ctx[:, None]
    return (y_prompt, y_sample, new_cache_k, new_cache_v, new_state)
```

```python
import functools

import jax
import jax.numpy as jnp
from jax import lax
from jax.experimental import pallas as pl
from jax.experimental.pallas import tpu as pltpu

F32 = jnp.float32
BF16 = jnp.bfloat16

GRID_W = 64
HG_HEADS = 4
HG_DK = 128
HG_DV = 128
HG_WIDTH = HG_HEADS * HG_DK
N_HEADS = 8
N_KV_HEADS = 2
HEAD_DIM = 64
ATT_WIDTH = N_HEADS * HEAD_DIM
KV_WIDTH = N_KV_HEADS * HEAD_DIM
ROPE_THETA = 10000.0
N_EXPERTS = 256
TOP_K = 8
N_GROUPS = 8
TOPK_GROUPS = 4
GROUP_SIZE = N_EXPERTS // N_GROUPS
ROUTED_SCALE = 2.5
NORM_EPS = 1e-6
DEPTH = 1
DEEPNORM_ALPHA = (2 * DEPTH) ** 0.25

LANES = 128
SUBLANES = 8
VMEM_LIMIT = 56 * 1024 * 1024

TOKEN_TILE = 256
TILE_STRIDE = TOKEN_TILE + 1
HG_CHUNK = 32
ATT_Q_BLOCK = 128
MOE_BLOCK = 320
ROW_STRIDE = MOE_BLOCK + 1
ROW_GROUP = 8
ROW_SEGMENT = 32
WEIGHT_BUFFERS = 3


def _sigmoid(x):
    return 1.0 / (1.0 + jnp.exp(-x))


def _silu(x):
    return x * _sigmoid(x)


def _params(sem=None):
    return pltpu.CompilerParams(dimension_semantics=sem, vmem_limit_bytes=VMEM_LIMIT)


def _mod_kernel(c_ref, w_ref, b_ref, o_ref):
    s = _silu(c_ref[...]).astype(BF16)
    o_ref[...] = jnp.dot(s, w_ref[...], preferred_element_type=F32) + b_ref[...]


def _modulation(cond, w_mod, b_mod):
    n, d = cond.shape
    width = w_mod.shape[1]
    tn = width // 4
    return pl.pallas_call(
        _mod_kernel,
        grid=(4,),
        in_specs=[
            pl.BlockSpec((n, d), lambda j: (0, 0)),
            pl.BlockSpec((d, tn), lambda j: (0, j)),
            pl.BlockSpec((1, tn), lambda j: (0, j)),
        ],
        out_specs=pl.BlockSpec((n, tn), lambda j: (0, j)),
        out_shape=jax.ShapeDtypeStruct((n, width), F32),
        compiler_params=_params(("arbitrary",)),
    )(cond, w_mod, b_mod)


def _inproj_kernel(xc_ref, xl_ref, mod_ref, w_ref, lb_ref, zh_ref, za_ref, zg_ref, *, d, n_ctx_tiles):
    shift = mod_ref[0, :, 0:d]
    scale = mod_ref[0, :, d:2 * d]
    x = _pick(pl.program_id(0) < n_ctx_tiles, xc_ref, xl_ref)
    h = (x * (1.0 + scale) + shift).astype(BF16)

    def proj(lo, hi):
        return jnp.dot(h, w_ref[:, lo:hi], preferred_element_type=F32)

    w = HG_WIDTH
    zh_ref[:, 0:w] = _silu(proj(0, w))
    zh_ref[:, w:2 * w] = proj(w, 2 * w)
    for i in range(2):
        lb = lb_ref[i:i + 1, :]
        zh_ref[:, (2 + i) * w:(3 + i) * w] = lb + (1.0 - lb) * _sigmoid(proj((2 + i) * w, (3 + i) * w))
    zh_ref[:, 4 * w:5 * w] = _silu(proj(4 * w, 5 * w))
    a0 = 5 * w
    a1 = a0 + ATT_WIDTH + 2 * KV_WIDTH
    za_ref[...] = proj(a0, a1)
    for i in range(4):
        lo = a1 + i * (d // 2)
        zg_ref[:, i * (d // 2):(i + 1) * (d // 2)] = _sigmoid(proj(lo, lo + d // 2))


def _mod_row_map(n_ctx_tiles, tiles_per_latent):
    def index_map(i):
        row = jnp.where(i < n_ctx_tiles, 0, 1 + (i - n_ctx_tiles) // tiles_per_latent)
        return (row, 0, 0)
    return index_map


def _pair_specs(width, n_ctx_tiles):
    return [pl.BlockSpec((TOKEN_TILE, width), lambda i: (jnp.minimum(i, n_ctx_tiles - 1), 0)),
            pl.BlockSpec((TOKEN_TILE, width), lambda i: (jnp.maximum(i - n_ctx_tiles, 0), 0))]


def _pick(is_ctx, ctx_ref, lat_ref):
    return jnp.where(is_ctx, ctx_ref[...], lat_ref[...])


def _input_projection(x_ctx, x_lat, mod3, w_in, lb, mod_map):
    d = x_ctx.shape[1]
    n_ctx_tiles = x_ctx.shape[0] // TOKEN_TILE
    t = x_ctx.shape[0] + x_lat.shape[0]
    width = w_in.shape[1]
    zh_w = 5 * HG_WIDTH
    za_w = ATT_WIDTH + 2 * KV_WIDTH
    zg_w = 2 * d
    assert width == zh_w + za_w + zg_w
    tm = TOKEN_TILE
    return pl.pallas_call(
        functools.partial(_inproj_kernel, d=d, n_ctx_tiles=n_ctx_tiles),
        grid=(t // tm,),
        in_specs=_pair_specs(d, n_ctx_tiles) + [
            pl.BlockSpec((1, 1, mod3.shape[2]), mod_map),
            pl.BlockSpec((d, width), lambda i: (0, 0)),
            pl.BlockSpec((2, HG_WIDTH), lambda i: (0, 0)),
        ],
        out_specs=[
            pl.BlockSpec((tm, zh_w), lambda i: (i, 0)),
            pl.BlockSpec((tm, za_w), lambda i: (i, 0)),
            pl.BlockSpec((tm, zg_w), lambda i: (i, 0)),
        ],
        out_shape=[
            jax.ShapeDtypeStruct((t, zh_w), F32),
            jax.ShapeDtypeStruct((t, za_w), F32),
            jax.ShapeDtypeStruct((t, zg_w), F32),
        ],
        compiler_params=_params(("arbitrary",)),
    )(x_ctx, x_lat, mod3, w_in, lb)


def _hgrn_kernel(*refs, seq, has_state):
    if has_state:
        q_ref, v_ref, ff_ref, fb_ref, gs_ref, ng_ref, s0_ref, o_ref, sout_ref = refs
    else:
        q_ref, v_ref, ff_ref, fb_ref, gs_ref, ng_ref, o_ref, sout_ref = refs
        s0_ref = None
    c = HG_CHUNK
    n = seq // c
    q3 = q_ref[...].reshape(n, c, HG_DK)
    v3 = v_ref[...].astype(BF16).reshape(n, c, HG_DV)
    pos = lax.broadcasted_iota(jnp.int32, (seq, HG_DK), 0) % c
    t_idx = lax.broadcasted_iota(jnp.int32, (c, c), 0)
    s_idx = lax.broadcasted_iota(jnp.int32, (c, c), 1)
    o_sum = None
    for direction, f_ref in enumerate((ff_ref, fb_ref)):
        reverse = direction == 1
        f = f_ref[...]
        k3 = (1.0 - f).reshape(n, c, HG_DK)
        b = jnp.log(f)
        step = 1
        while step < c:
            if reverse:
                b = b + jnp.where(pos < c - step, pltpu.roll(b, seq - step, axis=0), 0.0)
            else:
                b = b + jnp.where(pos >= step, pltpu.roll(b, step, axis=0), 0.0)
            step *= 2
        b3 = b.reshape(n, c, HG_DK)
        edge = b3[:, 0:1, :] if reverse else b3[:, c - 1:c, :]
        mid = b3[:, c // 2:c // 2 + 1, :]
        q_in = (q3 * jnp.exp(b3)).astype(BF16)
        k_end = (k3 * jnp.exp(edge - b3)).astype(BF16)
        q_mid = (q3 * jnp.exp(b3 - mid)).astype(BF16)
        k_mid = (k3 * jnp.exp(mid - b3)).astype(BF16)
        scores = jnp.einsum('ntd,nsd->nts', q_mid, k_mid, preferred_element_type=F32)
        keep = (s_idx >= t_idx) if reverse else (s_idx <= t_idx)
        scores = jnp.where(keep[None], scores, 0.0).astype(BF16)
        o_intra = jnp.einsum('nts,nsv->ntv', scores, v3, preferred_element_type=F32)
        upd = jnp.einsum('nsv,nsd->nvd', v3, k_end, preferred_element_type=F32)
        dec = jnp.exp(edge)
        if has_state:
            st = s0_ref[0, direction, 0].T
        else:
            st = jnp.zeros((HG_DV, HG_DK), F32)
        before = [None] * n
        for ci in (range(n - 1, -1, -1) if reverse else range(n)):
            before[ci] = st.astype(BF16)
            st = st * dec[ci] + upd[ci]
        st_before = jnp.stack(before, axis=0)
        o_inter = jnp.einsum('ntd,nvd->ntv', q_in, st_before, preferred_element_type=F32)
        o_dir = (o_intra + o_inter).reshape(seq, HG_DV)
        o_sum = o_dir if o_sum is None else o_sum + o_dir
        sout_ref[0, direction, 0] = st.T
    ms = jnp.mean(o_sum * o_sum, axis=-1, keepdims=True)
    o = o_sum * lax.rsqrt(ms + NORM_EPS) * ng_ref[...]
    o_ref[...] = (o * gs_ref[...]).astype(o_ref.dtype)


def _hgrn_scan(zh, norm_g, s0, *, batch, seq, row_block0):
    has_state = s0 is not None
    h = HG_HEADS

    def col(section):
        return pl.BlockSpec((seq, HG_DK), lambda b, j: (row_block0 + b, section * h + j))

    in_specs = [col(0), col(1), col(2), col(3), col(4), pl.BlockSpec((1, HG_DV), lambda b, j: (0, 0))]
    args = [zh, zh, zh, zh, zh, norm_g]
    state_spec = pl.BlockSpec((1, 2, 1, HG_DK, HG_DV), lambda b, j: (b, 0, j, 0, 0))
    if has_state:
        in_specs.append(state_spec)
        args.append(s0)
    return pl.pallas_call(
        functools.partial(_hgrn_kernel, seq=seq, has_state=has_state),
        grid=(batch, h),
        in_specs=in_specs,
        out_specs=[pl.BlockSpec((seq, HG_DV), lambda b, j: (b, j)), state_spec],
        out_shape=[
            jax.ShapeDtypeStruct((batch * seq, h * HG_DV), BF16),
            jax.ShapeDtypeStruct((batch, 2, h, HG_DK, HG_DV), F32),
        ],
        compiler_params=_params(("arbitrary", "arbitrary")),
    )(*args)


def _group_rms_norm(x, group_ones, gain):
    sq = x * x
    hi = sq.astype(BF16)
    lo = (sq - hi.astype(F32)).astype(BF16)
    total = (jnp.dot(hi, group_ones, preferred_element_type=F32)
             + jnp.dot(lo, group_ones, preferred_element_type=F32))
    return x * lax.rsqrt(total * (1.0 / HEAD_DIM) + NORM_EPS) * gain


def _rope(x, cos, sin_signed):
    width = x.shape[1]
    quarter = HEAD_DIM // 4
    lane = lax.broadcasted_iota(jnp.int32, x.shape, 1)
    partner = jnp.where(lane % (2 * quarter) < quarter,
                        pltpu.roll(x, width - quarter, axis=1),
                        pltpu.roll(x, quarter, axis=1))
    return x * cos + partner * sin_signed


def _attn_kernel(*refs, seq, latent):
    if latent:
        za_ref, qg_ref, kg_ref, gm_ref, cos_ref, sin_ref, ck_ref, cv_ref, o_ref = refs
    else:
        za_ref, qg_ref, kg_ref, gm_ref, o_ref, kout_ref, vout_ref = refs
    q = za_ref[:, 0:ATT_WIDTH]
    k = za_ref[:, ATT_WIDTH:ATT_WIDTH + KV_WIDTH]
    v = za_ref[:, ATT_WIDTH + KV_WIDTH:ATT_WIDTH + 2 * KV_WIDTH]
    qn = _group_rms_norm(q, gm_ref[...], qg_ref[...])
    kn = _group_rms_norm(k, gm_ref[0:KV_WIDTH, 0:KV_WIDTH], kg_ref[...])
    if latent:
        qn = _rope(qn, cos_ref[...], sin_ref[...])
        kr = _rope(kn, cos_ref[:, 0:KV_WIDTH], sin_ref[:, 0:KV_WIDTH])
    else:
        kout_ref[...] = kn
        vout_ref[...] = v
        kr = kn
    qb = (qn * (HEAD_DIM ** -0.5)).astype(BF16)
    kb = kr.astype(BF16)
    vb = v.astype(BF16)
    if latent:
        ckb = ck_ref[0].astype(BF16)
        cvb = cv_ref[0].astype(BF16)
    groups = N_HEADS // N_KV_HEADS
    tq = ATT_Q_BLOCK if latent else seq
    nt = (((1,), (1,)), ((), ()))
    for kh in range(N_KV_HEADS):
        ksl = slice(kh * HEAD_DIM, (kh + 1) * HEAD_DIM)
        k_new = kb[:, ksl]
        v_new = vb[:, ksl]
        for blk in range(seq // tq):
            rows = slice(blk * tq, (blk + 1) * tq)
            heads = [kh * groups + g for g in range(groups)]
            q_st = jnp.concatenate([qb[rows, hd * HEAD_DIM:(hd + 1) * HEAD_DIM] for hd in heads], axis=0)
            s_new = lax.dot_general(q_st, k_new, nt, preferred_element_type=F32)
            m = jnp.max(s_new, axis=-1, keepdims=True)
            if latent:
                s_ctx = lax.dot_general(q_st, ckb[:, ksl], nt, preferred_element_type=F32)
                m = jnp.maximum(m, jnp.max(s_ctx, axis=-1, keepdims=True))
            p_new = jnp.exp(s_new - m)
            denom = jnp.sum(p_new, axis=-1, keepdims=True)
            acc = jnp.dot(p_new.astype(BF16), v_new, preferred_element_type=F32)
            if latent:
                p_ctx = jnp.exp(s_ctx - m)
                denom = denom + jnp.sum(p_ctx, axis=-1, keepdims=True)
                acc = acc + jnp.dot(p_ctx.astype(BF16), cvb[:, ksl], preferred_element_type=F32)
            out = acc / denom
            for g in range(0, groups, 2):
                pair = jnp.concatenate([out[g * tq:(g + 1) * tq], out[(g + 1) * tq:(g + 2) * tq]], axis=1)
                lane0 = heads[g] * HEAD_DIM
                o_ref[rows, lane0:lane0 + 2 * HEAD_DIM] = pair.astype(o_ref.dtype)


def _attention(za, q_gain, k_gain, group_ones, *, batch, seq, row_block0, rope=None, cache=None):
    latent = cache is not None
    za_w = za.shape[1]
    in_specs = [
        pl.BlockSpec((seq, za_w), lambda b: (row_block0 + b, 0)),
        pl.BlockSpec((1, ATT_WIDTH), lambda b: (0, 0)),
        pl.BlockSpec((1, KV_WIDTH), lambda b: (0, 0)),
        pl.BlockSpec((ATT_WIDTH, ATT_WIDTH), lambda b: (0, 0)),
    ]
    args = [za, q_gain, k_gain, group_ones]
    o_spec = pl.BlockSpec((seq, ATT_WIDTH), lambda b: (b, 0))
    o_shape = jax.ShapeDtypeStruct((batch * seq, ATT_WIDTH), BF16)
    if latent:
        cos, sin_signed = rope
        ck, cv = cache
        past = ck.shape[1]
        in_specs += [
            pl.BlockSpec((seq, ATT_WIDTH), lambda b: (0, 0)),
            pl.BlockSpec((seq, ATT_WIDTH), lambda b: (0, 0)),
            pl.BlockSpec((1, past, KV_WIDTH), lambda b: (b, 0, 0)),
            pl.BlockSpec((1, past, KV_WIDTH), lambda b: (b, 0, 0)),
        ]
        args += [cos, sin_signed, ck, cv]
        out_specs = o_spec
        out_shape = o_shape
    else:
        kv_spec = pl.BlockSpec((seq, KV_WIDTH), lambda b: (b, 0))
        kv_shape = jax.ShapeDtypeStruct((batch * seq, KV_WIDTH), F32)
        out_specs = [o_spec, kv_spec, kv_spec]
        out_shape = [o_shape, kv_shape, kv_shape]
    return pl.pallas_call(
        functools.partial(_attn_kernel, seq=seq, latent=latent),
        grid=(batch,),
        in_specs=in_specs,
        out_specs=out_specs,
        out_shape=out_shape,
        compiler_params=_params(("arbitrary",)),
    )(*args)


def _rope_tables(seq):
    quarter = HEAD_DIM // 4
    t = jnp.arange(seq)
    row = (t // GRID_W).astype(F32)
    colp = (t % GRID_W).astype(F32)
    inv_freq = ROPE_THETA ** (-jnp.arange(quarter, dtype=F32) / quarter)
    lane = jnp.arange(HEAD_DIM)
    pos = jnp.where((lane < HEAD_DIM // 2)[None, :], row[:, None], colp[:, None])
    ang = pos * inv_freq[lane % quarter][None, :]
    sign = jnp.where(lane % (2 * quarter) < quarter, -1.0, 1.0)[None, :]
    cos = jnp.tile(jnp.cos(ang), (1, N_HEADS))
    sin_signed = jnp.tile(jnp.sin(ang) * sign, (1, N_HEADS))
    return cos, sin_signed


def _layer_norm(y, g, b):
    mu = jnp.mean(y, axis=-1, keepdims=True)
    yc = y - mu
    var = jnp.mean(yc * yc, axis=-1, keepdims=True)
    return yc * lax.rsqrt(var + NORM_EPS) * g + b


def _rows_to_slabs(tile, st_ref, slab_ref):
    rows, d = tile.shape
    chunks = d // LANES
    for j in range(chunks):
        st_ref[j * TILE_STRIDE:j * TILE_STRIDE + rows, :] = tile[:, j * LANES:(j + 1) * LANES]
    for r in range(rows):
        slab_ref[r * chunks:(r + 1) * chunks, :] = st_ref[pl.ds(r, chunks, stride=TILE_STRIDE), :]


def _slabs_to_rows(slab_ref, st_ref, rows):
    chunks = SUBLANES
    for r in range(rows):
        st_ref[pl.ds(r, chunks, stride=TILE_STRIDE), :] = slab_ref[r * chunks:(r + 1) * chunks, :]
    return jnp.concatenate([st_ref[j * TILE_STRIDE:j * TILE_STRIDE + rows, :] for j in range(chunks)], axis=1)


def _post_mixer_kernel(xc_ref, xl_ref, mod_ref, oac_ref, oal_ref, obc_ref, obl_ref, zg_ref, wa_ref, wb_ref,
                       wo_ref, g_ref, b_ref, x1_ref, h2_ref, slab_ref, st_ref, *, d, n_ctx_tiles):
    is_ctx = pl.program_id(0) < n_ctx_tiles
    gate1 = mod_ref[0, :, 2 * d:3 * d]
    shift2 = mod_ref[0, :, 3 * d:4 * d]
    scale2 = mod_ref[0, :, 4 * d:5 * d]
    branch_a = jnp.dot(_pick(is_ctx, oac_ref, oal_ref), wa_ref[...], preferred_element_type=F32)
    branch_b = jnp.dot(_pick(is_ctx, obc_ref, obl_ref), wb_ref[...], preferred_element_type=F32)
    merged = zg_ref[:, 0:d] * branch_a + zg_ref[:, d:2 * d] * branch_b
    mix = jnp.dot(merged.astype(BF16), wo_ref[...], preferred_element_type=F32)
    x = _pick(is_ctx, xc_ref, xl_ref)
    x1 = _layer_norm(DEEPNORM_ALPHA * x + gate1 * mix, g_ref[...], b_ref[...])
    x1_ref[...] = x1
    h2 = x1 * (1.0 + scale2) + shift2
    h2_ref[...] = h2
    _rows_to_slabs(h2, st_ref, slab_ref)


def _post_mixer(x_ctx, x_lat, mod3, oa_ctx, oa_lat, ob_ctx, ob_lat, zg, w_a, w_b, w_o, ln_g, ln_b, mod_map):
    d = x_ctx.shape[1]
    n_ctx_tiles = x_ctx.shape[0] // TOKEN_TILE
    t = x_ctx.shape[0] + x_lat.shape[0]
    tm = TOKEN_TILE
    row = lambda i: (i, 0)
    full = lambda i: (0, 0)
    return pl.pallas_call(
        functools.partial(_post_mixer_kernel, d=d, n_ctx_tiles=n_ctx_tiles),
        grid=(t // tm,),
        in_specs=_pair_specs(d, n_ctx_tiles) + [
            pl.BlockSpec((1, 1, mod3.shape[2]), mod_map),
        ] + _pair_specs(oa_ctx.shape[1], n_ctx_tiles) + _pair_specs(ob_ctx.shape[1], n_ctx_tiles) + [
            pl.BlockSpec((tm, 2 * d), row),
            pl.BlockSpec(w_a.shape, full),
            pl.BlockSpec(w_b.shape, full),
            pl.BlockSpec(w_o.shape, full),
            pl.BlockSpec((1, d), full),
            pl.BlockSpec((1, d), full),
        ],
        out_specs=[pl.BlockSpec((tm, d), row), pl.BlockSpec((tm, d), row),
                   pl.BlockSpec((tm * SUBLANES, LANES), row)],
        out_shape=[jax.ShapeDtypeStruct((t, d), F32), jax.ShapeDtypeStruct((t, d), F32),
                   jax.ShapeDtypeStruct((t * SUBLANES, LANES), F32)],
        scratch_shapes=[pltpu.VMEM((SUBLANES * TILE_STRIDE, LANES), F32)],
        compiler_params=_params(("arbitrary",)),
    )(x_ctx, x_lat, mod3, oa_ctx, oa_lat, ob_ctx, ob_lat, zg, w_a, w_b, w_o, ln_g, ln_b)


def _first_index_of_max(x, idx, sentinel):
    m = jnp.max(x, axis=0, keepdims=True)
    first = jnp.min(jnp.where(x == m, idx, sentinel), axis=0, keepdims=True)
    return m, first


def _router_kernel(h_ref, wh_ref, wl_ref, bias_ref, idx_ref, w_ref):
    h = h_ref[...]
    hh = h.astype(BF16)
    hl = (h - hh.astype(F32)).astype(BF16)
    nt = (((1,), (1,)), ((), ()))
    wh = wh_ref[...]
    logits = (lax.dot_general(wh, hh, nt, preferred_element_type=F32)
              + lax.dot_general(wh, hl, nt, preferred_element_type=F32)
              + lax.dot_general(wl_ref[...], hh, nt, preferred_element_type=F32))
    scores = _sigmoid(logits)
    sel = scores + bias_ref[...]
    tm = sel.shape[1]
    neg = -jnp.inf
    gidx = lax.broadcasted_iota(jnp.int32, (GROUP_SIZE, tm), 0)
    group_scores = []
    for g in range(N_GROUPS):
        sg = sel[g * GROUP_SIZE:(g + 1) * GROUP_SIZE, :]
        m1, first = _first_index_of_max(sg, gidx, GROUP_SIZE)
        m2 = jnp.max(jnp.where(gidx == first, neg, sg), axis=0, keepdims=True)
        group_scores.append(m1 + m2)
    gs = jnp.concatenate(group_scores, axis=0)
    nidx = lax.broadcasted_iota(jnp.int32, (N_GROUPS, tm), 0)
    chosen = jnp.zeros((N_GROUPS, tm), jnp.bool_)
    for _ in range(TOPK_GROUPS):
        _, first = _first_index_of_max(gs, nidx, N_GROUPS)
        hit = nidx == first
        chosen = jnp.logical_or(chosen, hit)
        gs = jnp.where(hit, neg, gs)
    masked = jnp.concatenate(
        [jnp.where(chosen[g:g + 1, :], sel[g * GROUP_SIZE:(g + 1) * GROUP_SIZE, :], neg) for g in range(N_GROUPS)],
        axis=0)
    eidx = lax.broadcasted_iota(jnp.int32, (N_EXPERTS, tm), 0)
    picks, weights = [], []
    for _ in range(TOP_K):
        _, first = _first_index_of_max(masked, eidx, N_EXPERTS)
        hit = eidx == first
        picks.append(first)
        weights.append(jnp.sum(jnp.where(hit, scores, 0.0), axis=0, keepdims=True))
        masked = jnp.where(hit, neg, masked)
    wk = jnp.concatenate(weights, axis=0)
    idx_ref[...] = jnp.concatenate(picks, axis=0)
    w_ref[...] = wk / jnp.sum(wk, axis=0, keepdims=True) * ROUTED_SCALE


def _router(h2, wr_hi, wr_lo, bias_col):
    t, d = h2.shape
    tm = TOKEN_TILE
    return pl.pallas_call(
        _router_kernel,
        grid=(t // tm,),
        in_specs=[
            pl.BlockSpec((tm, d), lambda i: (i, 0)),
            pl.BlockSpec((N_EXPERTS, d), lambda i: (0, 0)),
            pl.BlockSpec((N_EXPERTS, d), lambda i: (0, 0)),
            pl.BlockSpec((N_EXPERTS, 1), lambda i: (0, 0)),
        ],
        out_specs=[pl.BlockSpec((TOP_K, tm), lambda i: (0, i)), pl.BlockSpec((TOP_K, tm), lambda i: (0, i))],
        out_shape=[jax.ShapeDtypeStruct((TOP_K, t), jnp.int32), jax.ShapeDtypeStruct((TOP_K, t), F32)],
        compiler_params=_params(("arbitrary",)),
    )(h2, wr_hi, wr_lo, bias_col)


def _dispatch_lists(idx_t, w_t):
    k, t = idx_t.shape
    s = k * t
    n_blocks = s // MOE_BLOCK + N_EXPERTS
    flat_tok = jnp.arange(s, dtype=jnp.int32) % t
    sorted_e, sorted_tok, sorted_w = lax.sort((idx_t.reshape(s), flat_tok, w_t.reshape(s)), num_keys=1)
    experts = jnp.arange(N_EXPERTS + 1, dtype=jnp.int32)
    start = jnp.sum(sorted_e[None, :] < experts[:, None], axis=1, dtype=jnp.int32)
    counts = start[1:] - start[:-1]
    blocks_per_e = (counts + MOE_BLOCK - 1) // MOE_BLOCK
    block_end = jnp.cumsum(blocks_per_e)
    n_used = block_end[-1]
    blk = jnp.arange(n_blocks, dtype=jnp.int32)
    block_e = jnp.sum(block_end[None, :] <= blk[:, None], axis=1, dtype=jnp.int32)
    block_e = jnp.minimum(block_e, N_EXPERTS - 1)
    onehot = (block_e[:, None] == experts[None, :N_EXPERTS]).astype(jnp.int32)
    pick = lambda a: jnp.sum(onehot * a[None, :], axis=1)
    offset = (blk - (pick(block_end) - pick(blocks_per_e))) * MOE_BLOCK
    row_start = pick(start[:-1]) + offset
    n_rows = jnp.clip(pick(counts) - offset, 0, MOE_BLOCK)
    used = blk < n_used
    last_e = jnp.sum(jnp.where(blk == n_used - 1, block_e, 0))
    block_e = jnp.where(used, block_e, last_e)
    row_start = jnp.where(used, row_start, 0)
    n_rows = jnp.where(used, n_rows, 0)
    return block_e, row_start, n_rows, (n_used - 1).reshape(1), sorted_tok, sorted_w


def _slab_row(tok):
    return pl.multiple_of(tok * SUBLANES, SUBLANES)


def _expert_kernel(be_ref, rs_ref, nr_ref, last_ref, tok_ref, x_hbm, wg_hbm, wu_hbm, wd_hbm, out_ref,
                   xs, xt, wgb, wub, wdb, sem, wsem, *, n_slots):
    b = pl.program_id(0)
    chunks = wg_hbm.shape[1] // LANES
    n_rows = nr_ref[b]
    last = last_ref[0]
    ahead = WEIGHT_BUFFERS - 1

    def weight_copies(blk):
        e = be_ref[blk]
        slot = blk % WEIGHT_BUFFERS
        return [pltpu.make_async_copy(src.at[e], dst.at[slot], wsem.at[slot])
                for src, dst in ((wg_hbm, wgb), (wu_hbm, wub), (wd_hbm, wdb))]

    @pl.when(b == 0)
    def _load():
        for blk in range(ahead):
            @pl.when(blk <= last)
            def _(blk=blk):
                for cp in weight_copies(blk):
                    cp.start()
        cp = pltpu.make_async_copy(x_hbm, xs, sem)
        cp.start()
        cp.wait()
        xt[...] = jnp.zeros(xt.shape, F32)

    @pl.when(b + ahead <= last)
    def _prefetch():
        for cp in weight_copies(b + ahead):
            cp.start()

    @pl.when(b <= last)
    def _block():
        base = rs_ref[b]
        slot = b % WEIGHT_BUFFERS
        for cp in weight_copies(b):
            cp.wait()
        for seg in range(MOE_BLOCK // ROW_SEGMENT):
            @pl.when(n_rows > seg * ROW_SEGMENT)
            def _gather(seg=seg):
                for r in range(seg * ROW_SEGMENT, (seg + 1) * ROW_SEGMENT):
                    tok = tok_ref[jnp.minimum(base + r, n_slots - 1)]
                    xt[pl.ds(r, chunks, stride=ROW_STRIDE), :] = xs[pl.ds(_slab_row(tok), SUBLANES), :]

        x = jnp.concatenate(
            [xt[j * ROW_STRIDE:j * ROW_STRIDE + MOE_BLOCK, :] for j in range(chunks)], axis=1).astype(BF16)
        gate = jnp.dot(x, wgb[slot].astype(BF16), preferred_element_type=F32)
        up = jnp.dot(x, wub[slot].astype(BF16), preferred_element_type=F32)
        hidden = (_silu(gate) * up).astype(BF16)
        out_ref[...] = jnp.dot(hidden, wdb[slot].astype(BF16), preferred_element_type=F32)


def _combine_kernel(rs_ref, nr_ref, last_ref, tok_ref, w_ref, o_ref, y_hbm, ys, ot, sem, *, t, n_slots, n_blocks):
    b = pl.program_id(0)
    rows = t * SUBLANES
    chunks = o_ref.shape[1] // LANES
    n_rows = nr_ref[b]
    group = ROW_GROUP

    @pl.when(b == 0)
    def _init():
        ys[...] = jnp.zeros(ys.shape, F32)

    @pl.when(n_rows > 0)
    def _block():
        base = rs_ref[b]
        for j in range(chunks):
            ot[j * ROW_STRIDE:j * ROW_STRIDE + MOE_BLOCK, :] = o_ref[:, j * LANES:(j + 1) * LANES]
        for seg in range(MOE_BLOCK // ROW_SEGMENT):
            @pl.when(n_rows > seg * ROW_SEGMENT)
            def _scatter(seg=seg):
                for r0 in range(seg * ROW_SEGMENT, (seg + 1) * ROW_SEGMENT, group):
                    slot = [jnp.minimum(base + r0 + i, n_slots - 1) for i in range(group)]
                    toks = [_slab_row(jnp.where(r0 + i < n_rows, tok_ref[slot[i]], t)) for i in range(group)]
                    vals = [ys[pl.ds(toks[i], SUBLANES), :]
                            + w_ref[slot[i]] * ot[pl.ds(r0 + i, chunks, stride=ROW_STRIDE), :]
                            for i in range(group)]
                    for i in range(group):
                        ys[pl.ds(toks[i], SUBLANES), :] = vals[i]

    @pl.when(b == n_blocks - 1)
    def _store():
        cp = pltpu.make_async_copy(ys.at[pl.ds(0, rows)], y_hbm, sem)
        cp.start()
        cp.wait()


def _routed_experts(h2_rows, block_e, row_start, n_rows, last_block, sorted_tok, sorted_w, w_gate, w_up, w_down):
    rows, lanes = h2_rows.shape
    t = rows // SUBLANES
    n_blocks = block_e.shape[0]
    n_slots = sorted_tok.shape[0]
    d, e_dim = w_gate.shape[1], w_gate.shape[2]
    assert d == SUBLANES * LANES and lanes == LANES
    staging = pltpu.VMEM((SUBLANES * ROW_STRIDE, LANES), F32)
    expert_out = pl.pallas_call(
        functools.partial(_expert_kernel, n_slots=n_slots),
        grid_spec=pltpu.PrefetchScalarGridSpec(
            num_scalar_prefetch=5,
            grid=(n_blocks,),
            in_specs=[
                pl.BlockSpec(memory_space=pl.ANY),
                pl.BlockSpec(memory_space=pl.ANY),
                pl.BlockSpec(memory_space=pl.ANY),
                pl.BlockSpec(memory_space=pl.ANY),
            ],
            out_specs=pl.BlockSpec((MOE_BLOCK, d), lambda b, be, rs, nr, last, tok: (jnp.minimum(b, last[0]), 0)),
            scratch_shapes=[
                pltpu.VMEM((rows, LANES), F32),
                staging,
                pltpu.VMEM((WEIGHT_BUFFERS, d, e_dim), F32),
                pltpu.VMEM((WEIGHT_BUFFERS, d, e_dim), F32),
                pltpu.VMEM((WEIGHT_BUFFERS, e_dim, d), F32),
                pltpu.SemaphoreType.DMA(()),
                pltpu.SemaphoreType.DMA((WEIGHT_BUFFERS,)),
            ],
        ),
        out_shape=jax.ShapeDtypeStruct((n_blocks * MOE_BLOCK, d), F32),
        compiler_params=_params(("arbitrary",)),
    )(block_e, row_start, n_rows, last_block, sorted_tok, h2_rows, w_gate, w_up, w_down)
    return pl.pallas_call(
        functools.partial(_combine_kernel, t=t, n_slots=n_slots, n_blocks=n_blocks),
        grid_spec=pltpu.PrefetchScalarGridSpec(
            num_scalar_prefetch=5,
            grid=(n_blocks,),
            in_specs=[pl.BlockSpec((MOE_BLOCK, d), lambda b, rs, nr, last, tok, w: (jnp.minimum(b, last[0]), 0))],
            out_specs=pl.BlockSpec(memory_space=pl.ANY),
            scratch_shapes=[pltpu.VMEM((rows + SUBLANES, LANES), F32), staging, pltpu.SemaphoreType.DMA(())],
        ),
        out_shape=jax.ShapeDtypeStruct((rows, LANES), F32),
        compiler_params=_params(("arbitrary",)),
    )(row_start, n_rows, last_block, sorted_tok, sorted_w, expert_out)


def _final_kernel(x1_ref, h2_ref, routed_ref, mod_ref, wg_ref, wu_ref, wd_ref, g_ref, b_ref,
                  yc_ref, yl_ref, st_ref, *, d, n_ctx_tiles):
    is_ctx = pl.program_id(0) < n_ctx_tiles
    gate2 = mod_ref[0, :, 5 * d:6 * d]
    h = h2_ref[...].astype(BF16)
    gate = jnp.dot(h, wg_ref[...], preferred_element_type=F32)
    up = jnp.dot(h, wu_ref[...], preferred_element_type=F32)
    shared = jnp.dot((_silu(gate) * up).astype(BF16), wd_ref[...], preferred_element_type=F32)
    ffn = _slabs_to_rows(routed_ref, st_ref, h.shape[0]) + shared
    y = _layer_norm(DEEPNORM_ALPHA * x1_ref[...] + gate2 * ffn, g_ref[...], b_ref[...])

    @pl.when(is_ctx)
    def _():
        yc_ref[...] = y

    @pl.when(jnp.logical_not(is_ctx))
    def _():
        yl_ref[...] = y


def _final(x1, h2, routed, mod3, w_g, w_u, w_d, ln_g, ln_b, mod_map, t_ctx):
    t, d = x1.shape
    t_lat = t - t_ctx
    n_ctx_tiles = t_ctx // TOKEN_TILE
    tm = TOKEN_TILE
    row = lambda i: (i, 0)
    full = lambda i: (0, 0)
    return pl.pallas_call(
        functools.partial(_final_kernel, d=d, n_ctx_tiles=n_ctx_tiles),
        grid=(t // tm,),
        in_specs=[
            pl.BlockSpec((tm, d), row),
            pl.BlockSpec((tm, d), row),
            pl.BlockSpec((tm * SUBLANES, LANES), row),
            pl.BlockSpec((1, 1, mod3.shape[2]), mod_map),
            pl.BlockSpec(w_g.shape, full),
            pl.BlockSpec(w_u.shape, full),
            pl.BlockSpec(w_d.shape, full),
            pl.BlockSpec((1, d), full),
            pl.BlockSpec((1, d), full),
        ],
        out_specs=_pair_specs(d, n_ctx_tiles),
        out_shape=[jax.ShapeDtypeStruct((t_ctx, d), F32), jax.ShapeDtypeStruct((t_lat, d), F32)],
        scratch_shapes=[pltpu.VMEM((SUBLANES * TILE_STRIDE, LANES), F32)],
        compiler_params=_params(("arbitrary",)),
    )(x1, h2, routed, mod3, w_g, w_u, w_d, ln_g, ln_b)


def kernel(x_prompt, x_sample, cache_k, cache_v, state_hgrn, c, c_ctx, w_mod, b_mod, w_in, hg_lb, hg_norm_g, q_norm_g, k_norm_g, w_branch_a, w_branch_b, w_out, ln1_g, ln1_b, w_router, router_bias, w_e_gate, w_e_up, w_e_down, w_s_gate, w_s_up, w_s_down, ln2_g, ln2_b):
    assert w_mod.shape[0] == DEPTH
    n_ctx, seq_ctx, d = x_prompt.shape
    n_lat, seq_lat, _ = x_sample.shape
    t_ctx = n_ctx * seq_ctx
    t_lat = n_lat * seq_lat
    assert seq_ctx == TOKEN_TILE and seq_lat % TOKEN_TILE == 0
    layer = 0

    lb = jnp.cumsum(jax.nn.softmax(hg_lb.astype(F32), axis=0), axis=0)[layer]

    cond = jnp.concatenate([c_ctx[None, :], c], axis=0)
    cond = jnp.pad(cond, ((0, (-cond.shape[0]) % SUBLANES), (0, 0)))
    mod = _modulation(cond, w_mod[layer].astype(BF16), b_mod[layer][None, :])
    mod3 = mod.reshape(mod.shape[0], 1, mod.shape[1])
    mod_map = _mod_row_map(t_ctx // TOKEN_TILE, seq_lat // TOKEN_TILE)

    x_ctx = x_prompt.reshape(t_ctx, d)
    x_lat = x_sample.reshape(t_lat, d)
    zh, za, zg = _input_projection(x_ctx, x_lat, mod3, w_in[layer].astype(BF16), lb, mod_map)

    norm_g = hg_norm_g[layer][None, :]
    oa_ctx, s_ctx = _hgrn_scan(zh, norm_g, None, batch=n_ctx, seq=seq_ctx, row_block0=0)
    oa_lat, _ = _hgrn_scan(zh, norm_g, state_hgrn[:, layer], batch=n_lat, seq=seq_lat,
                           row_block0=t_ctx // seq_lat)

    q_gain = jnp.tile(q_norm_g[layer], N_HEADS)[None, :]
    k_gain = jnp.tile(k_norm_g[layer], N_KV_HEADS)[None, :]
    lane = jnp.arange(ATT_WIDTH)
    group_ones = (lane[:, None] // HEAD_DIM == lane[None, :] // HEAD_DIM).astype(BF16)
    ob_ctx, k_ctx, v_ctx = _attention(za, q_gain, k_gain, group_ones, batch=n_ctx, seq=seq_ctx, row_block0=0)
    past = cache_k.shape[2]
    ob_lat = _attention(
        za, q_gain, k_gain, group_ones, batch=n_lat, seq=seq_lat, row_block0=t_ctx // seq_lat,
        rope=_rope_tables(seq_lat),
        cache=(cache_k[:, layer].reshape(n_lat, past, KV_WIDTH), cache_v[:, layer].reshape(n_lat, past, KV_WIDTH)))

    x1, h2, h2_slabs = _post_mixer(x_ctx, x_lat, mod3, oa_ctx, oa_lat, ob_ctx, ob_lat, zg,
                                   w_branch_a[layer].astype(BF16), w_branch_b[layer].astype(BF16),
                                   w_out[layer].astype(BF16), ln1_g[layer][None, :], ln1_b[layer][None, :], mod_map)

    wr_t = w_router[layer].T
    wr_hi = wr_t.astype(BF16)
    wr_lo = (wr_t - wr_hi.astype(F32)).astype(BF16)
    idx_t, w_t = _router(h2, wr_hi, wr_lo, router_bias[layer][:, None])

    routed = _routed_experts(h2_slabs, *_dispatch_lists(idx_t, w_t), w_e_gate[layer], w_e_up[layer], w_e_down[layer])

    y_ctx, y_lat = _final(x1, h2, routed, mod3, w_s_gate[layer].astype(BF16),
                          w_s_up[layer].astype(BF16), w_s_down[layer].astype(BF16),
                          ln2_g[layer][None, :], ln2_b[layer][None, :], mod_map, t_ctx)

    y_prompt = y_ctx.reshape(n_ctx, seq_ctx, d)
    y_sample = y_lat.reshape(n_lat, seq_lat, d)
    new_cache_k = k_ctx.reshape(n_ctx, 1, seq_ctx, N_KV_HEADS, HEAD_DIM)
    new_cache_v = v_ctx.reshape(n_ctx, 1, seq_ctx, N_KV_HEADS, HEAD_DIM)
    new_state = s_ctx[:, None]
    return (y_prompt, y_sample, new_cache_k, new_cache_v, new_state)
```

```python
import functools

import jax
import jax.numpy as jnp
from jax import lax
from jax.experimental import pallas as pl
from jax.experimental.pallas import tpu as pltpu

F32 = jnp.float32
BF16 = jnp.bfloat16

GRID_W = 64
HG_HEADS = 4
HG_DK = 128
HG_DV = 128
HG_WIDTH = HG_HEADS * HG_DK
N_HEADS = 8
N_KV_HEADS = 2
HEAD_DIM = 64
ATT_WIDTH = N_HEADS * HEAD_DIM
KV_WIDTH = N_KV_HEADS * HEAD_DIM
ROPE_THETA = 10000.0
N_EXPERTS = 256
TOP_K = 8
N_GROUPS = 8
TOPK_GROUPS = 4
GROUP_SIZE = N_EXPERTS // N_GROUPS
ROUTED_SCALE = 2.5
NORM_EPS = 1e-6
DEPTH = 1
DEEPNORM_ALPHA = (2 * DEPTH) ** 0.25

LANES = 128
SUBLANES = 8
VMEM_LIMIT = 56 * 1024 * 1024

TOKEN_TILE = 256
TILE_STRIDE = TOKEN_TILE + 1
HG_CHUNK = 32
ATT_Q_BLOCK = 128
MOE_BLOCK = 320
ROW_STRIDE = MOE_BLOCK + 1
ROW_GROUP = 8
ROW_SEGMENT = 32
WEIGHT_BUFFERS = 3


def _sigmoid(x):
    return 1.0 / (1.0 + jnp.exp(-x))


def _silu(x):
    return x * _sigmoid(x)


def _params(sem=None):
    return pltpu.CompilerParams(dimension_semantics=sem, vmem_limit_bytes=VMEM_LIMIT)


def _mod_kernel(c_ref, w_ref, b_ref, o_ref):
    s = _silu(c_ref[...]).astype(BF16)
    o_ref[...] = jnp.dot(s, w_ref[...], preferred_element_type=F32) + b_ref[...]


def _modulation(cond, w_mod, b_mod):
    n, d = cond.shape
    width = w_mod.shape[1]
    tn = width // 4
    return pl.pallas_call(
        _mod_kernel,
        grid=(4,),
        in_specs=[
            pl.BlockSpec((n, d), lambda j: (0, 0)),
            pl.BlockSpec((d, tn), lambda j: (0, j)),
            pl.BlockSpec((1, tn), lambda j: (0, j)),
        ],
        out_specs=pl.BlockSpec((n, tn), lambda j: (0, j)),
        out_shape=jax.ShapeDtypeStruct((n, width), F32),
        compiler_params=_params(("arbitrary",)),
    )(cond, w_mod, b_mod)


def _inproj_kernel(xc_ref, xl_ref, mod_ref, w_ref, lb_ref, zh_ref, zf_ref, za_ref, zg_ref, *, d, n_ctx_tiles):
    shift = mod_ref[0, :, 0:d]
    scale = mod_ref[0, :, d:2 * d]
    x = _pick(pl.program_id(0) < n_ctx_tiles, xc_ref, xl_ref)
    h = (x * (1.0 + scale) + shift).astype(BF16)

    def proj(lo, hi):
        return jnp.dot(h, w_ref[:, lo:hi], preferred_element_type=F32)

    w = HG_WIDTH
    zh_ref[:, 0:w] = _silu(proj(0, w)).astype(zh_ref.dtype)
    zh_ref[:, w:2 * w] = proj(w, 2 * w).astype(zh_ref.dtype)
    for i in range(2):
        lb = lb_ref[i:i + 1, :]
        zf_ref[:, i * w:(i + 1) * w] = lb + (1.0 - lb) * _sigmoid(proj((2 + i) * w, (3 + i) * w))
    zh_ref[:, 2 * w:3 * w] = _silu(proj(4 * w, 5 * w)).astype(zh_ref.dtype)
    a0 = 5 * w
    a1 = a0 + ATT_WIDTH + 2 * KV_WIDTH
    za_ref[...] = proj(a0, a1)
    for i in range(4):
        lo = a1 + i * (d // 2)
        zg_ref[:, i * (d // 2):(i + 1) * (d // 2)] = _sigmoid(proj(lo, lo + d // 2)).astype(zg_ref.dtype)


def _mod_row_map(n_ctx_tiles, tiles_per_latent):
    def index_map(i):
        row = jnp.where(i < n_ctx_tiles, 0, 1 + (i - n_ctx_tiles) // tiles_per_latent)
        return (row, 0, 0)
    return index_map


def _pair_specs(width, n_ctx_tiles):
    return [pl.BlockSpec((TOKEN_TILE, width), lambda i: (jnp.minimum(i, n_ctx_tiles - 1), 0)),
            pl.BlockSpec((TOKEN_TILE, width), lambda i: (jnp.maximum(i - n_ctx_tiles, 0), 0))]


def _pick(is_ctx, ctx_ref, lat_ref):
    return jnp.where(is_ctx, ctx_ref[...], lat_ref[...])


def _input_projection(x_ctx, x_lat, mod3, w_in, lb, mod_map):
    d = x_ctx.shape[1]
    n_ctx_tiles = x_ctx.shape[0] // TOKEN_TILE
    t = x_ctx.shape[0] + x_lat.shape[0]
    width = w_in.shape[1]
    zh_w = 3 * HG_WIDTH
    zf_w = 2 * HG_WIDTH
    za_w = ATT_WIDTH + 2 * KV_WIDTH
    zg_w = 2 * d
    assert width == zh_w + zf_w + za_w + zg_w
    tm = TOKEN_TILE
    return pl.pallas_call(
        functools.partial(_inproj_kernel, d=d, n_ctx_tiles=n_ctx_tiles),
        grid=(t // tm,),
        in_specs=_pair_specs(d, n_ctx_tiles) + [
            pl.BlockSpec((1, 1, mod3.shape[2]), mod_map),
            pl.BlockSpec((d, width), lambda i: (0, 0)),
            pl.BlockSpec((2, HG_WIDTH), lambda i: (0, 0)),
        ],
        out_specs=[
            pl.BlockSpec((tm, zh_w), lambda i: (i, 0)),
            pl.BlockSpec((tm, zf_w), lambda i: (i, 0)),
            pl.BlockSpec((tm, za_w), lambda i: (i, 0)),
            pl.BlockSpec((tm, zg_w), lambda i: (i, 0)),
        ],
        out_shape=[
            jax.ShapeDtypeStruct((t, zh_w), BF16),
            jax.ShapeDtypeStruct((t, zf_w), F32),
            jax.ShapeDtypeStruct((t, za_w), F32),
            jax.ShapeDtypeStruct((t, zg_w), BF16),
        ],
        compiler_params=_params(("arbitrary",)),
    )(x_ctx, x_lat, mod3, w_in, lb)


def _hgrn_kernel(*refs, seq, has_state):
    if has_state:
        q_ref, v_ref, ff_ref, fb_ref, gs_ref, ng_ref, s0_ref, o_ref, sout_ref = refs
    else:
        q_ref, v_ref, ff_ref, fb_ref, gs_ref, ng_ref, o_ref, sout_ref = refs
        s0_ref = None
    c = HG_CHUNK
    n = seq // c
    q3 = q_ref[...].astype(F32).reshape(n, c, HG_DK)
    v3 = v_ref[...].reshape(n, c, HG_DV)
    pos = lax.broadcasted_iota(jnp.int32, (seq, HG_DK), 0) % c
    t_idx = lax.broadcasted_iota(jnp.int32, (c, c), 0)
    s_idx = lax.broadcasted_iota(jnp.int32, (c, c), 1)
    o_sum = None
    for direction, f_ref in enumerate((ff_ref, fb_ref)):
        reverse = direction == 1
        f = f_ref[...]
        k3 = (1.0 - f).reshape(n, c, HG_DK)
        b = jnp.log(f)
        step = 1
        while step < c:
            if reverse:
                b = b + jnp.where(pos < c - step, pltpu.roll(b, seq - step, axis=0), 0.0)
            else:
                b = b + jnp.where(pos >= step, pltpu.roll(b, step, axis=0), 0.0)
            step *= 2
        b3 = b.reshape(n, c, HG_DK)
        edge = b3[:, 0:1, :] if reverse else b3[:, c - 1:c, :]
        mid = b3[:, c // 2:c // 2 + 1, :]
        q_in = (q3 * jnp.exp(b3)).astype(BF16)
        k_end = (k3 * jnp.exp(edge - b3)).astype(BF16)
        q_mid = (q3 * jnp.exp(b3 - mid)).astype(BF16)
        k_mid = (k3 * jnp.exp(mid - b3)).astype(BF16)
        scores = jnp.einsum('ntd,nsd->nts', q_mid, k_mid, preferred_element_type=F32)
        keep = (s_idx >= t_idx) if reverse else (s_idx <= t_idx)
        scores = jnp.where(keep[None], scores, 0.0).astype(BF16)
        o_intra = jnp.einsum('nts,nsv->ntv', scores, v3, preferred_element_type=F32)
        upd = jnp.einsum('nsv,nsd->nvd', v3, k_end, preferred_element_type=F32)
        dec = jnp.exp(edge)
        if has_state:
            st = s0_ref[0, direction, 0].T
        else:
            st = jnp.zeros((HG_DV, HG_DK), F32)
        before = [None] * n
        for ci in (range(n - 1, -1, -1) if reverse else range(n)):
            before[ci] = st.astype(BF16)
            st = st * dec[ci] + upd[ci]
        st_before = jnp.stack(before, axis=0)
        o_inter = jnp.einsum('ntd,nvd->ntv', q_in, st_before, preferred_element_type=F32)
        o_dir = (o_intra + o_inter).reshape(seq, HG_DV)
        o_sum = o_dir if o_sum is None else o_sum + o_dir
        sout_ref[0, direction, 0] = st.T
    ms = jnp.mean(o_sum * o_sum, axis=-1, keepdims=True)
    o = o_sum * lax.rsqrt(ms + NORM_EPS) * ng_ref[...]
    o_ref[...] = (o * gs_ref[...].astype(F32)).astype(o_ref.dtype)


def _hgrn_scan(zh, zf, norm_g, s0, *, batch, seq, row_block0):
    has_state = s0 is not None
    h = HG_HEADS

    def col(section):
        return pl.BlockSpec((seq, HG_DK), lambda b, j: (row_block0 + b, section * h + j))

    in_specs = [col(0), col(1), col(0), col(1), col(2), pl.BlockSpec((1, HG_DV), lambda b, j: (0, 0))]
    args = [zh, zh, zf, zf, zh, norm_g]
    state_spec = pl.BlockSpec((1, 2, 1, HG_DK, HG_DV), lambda b, j: (b, 0, j, 0, 0))
    if has_state:
        in_specs.append(state_spec)
        args.append(s0)
    return pl.pallas_call(
        functools.partial(_hgrn_kernel, seq=seq, has_state=has_state),
        grid=(batch, h),
        in_specs=in_specs,
        out_specs=[pl.BlockSpec((seq, HG_DV), lambda b, j: (b, j)), state_spec],
        out_shape=[
            jax.ShapeDtypeStruct((batch * seq, h * HG_DV), BF16),
            jax.ShapeDtypeStruct((batch, 2, h, HG_DK, HG_DV), F32),
        ],
        compiler_params=_params(("arbitrary", "arbitrary")),
    )(*args)


def _group_rms_norm(x, group_ones, gain):
    sq = x * x
    hi = sq.astype(BF16)
    lo = (sq - hi.astype(F32)).astype(BF16)
    total = (jnp.dot(hi, group_ones, preferred_element_type=F32)
             + jnp.dot(lo, group_ones, preferred_element_type=F32))
    return x * lax.rsqrt(total * (1.0 / HEAD_DIM) + NORM_EPS) * gain


def _rope(x, cos, sin_signed):
    width = x.shape[1]
    quarter = HEAD_DIM // 4
    lane = lax.broadcasted_iota(jnp.int32, x.shape, 1)
    partner = jnp.where(lane % (2 * quarter) < quarter,
                        pltpu.roll(x, width - quarter, axis=1),
                        pltpu.roll(x, quarter, axis=1))
    return x * cos + partner * sin_signed


def _attn_kernel(*refs, seq, latent):
    if latent:
        za_ref, qg_ref, kg_ref, gm_ref, cos_ref, sin_ref, ck_ref, cv_ref, o_ref = refs
    else:
        za_ref, qg_ref, kg_ref, gm_ref, o_ref, kout_ref, vout_ref = refs
    q = za_ref[:, 0:ATT_WIDTH]
    k = za_ref[:, ATT_WIDTH:ATT_WIDTH + KV_WIDTH]
    v = za_ref[:, ATT_WIDTH + KV_WIDTH:ATT_WIDTH + 2 * KV_WIDTH]
    qn = _group_rms_norm(q, gm_ref[...], qg_ref[...])
    kn = _group_rms_norm(k, gm_ref[0:KV_WIDTH, 0:KV_WIDTH], kg_ref[...])
    if latent:
        qn = _rope(qn, cos_ref[...], sin_ref[...])
        kr = _rope(kn, cos_ref[:, 0:KV_WIDTH], sin_ref[:, 0:KV_WIDTH])
    else:
        kout_ref[...] = kn
        vout_ref[...] = v
        kr = kn
    qb = (qn * (HEAD_DIM ** -0.5)).astype(BF16)
    kb = kr.astype(BF16)
    vb = v.astype(BF16)
    if latent:
        ckb = ck_ref[0].astype(BF16)
        cvb = cv_ref[0].astype(BF16)
    groups = N_HEADS // N_KV_HEADS
    tq = ATT_Q_BLOCK if latent else seq
    nt = (((1,), (1,)), ((), ()))
    for kh in range(N_KV_HEADS):
        ksl = slice(kh * HEAD_DIM, (kh + 1) * HEAD_DIM)
        k_new = kb[:, ksl]
        v_new = vb[:, ksl]
        for blk in range(seq // tq):
            rows = slice(blk * tq, (blk + 1) * tq)
            heads = [kh * groups + g for g in range(groups)]
            q_st = jnp.concatenate([qb[rows, hd * HEAD_DIM:(hd + 1) * HEAD_DIM] for hd in heads], axis=0)
            s_new = lax.dot_general(q_st, k_new, nt, preferred_element_type=F32)
            m = jnp.max(s_new, axis=-1, keepdims=True)
            if latent:
                s_ctx = lax.dot_general(q_st, ckb[:, ksl], nt, preferred_element_type=F32)
                m = jnp.maximum(m, jnp.max(s_ctx, axis=-1, keepdims=True))
            p_new = jnp.exp(s_new - m)
            denom = jnp.sum(p_new, axis=-1, keepdims=True)
            acc = jnp.dot(p_new.astype(BF16), v_new, preferred_element_type=F32)
            if latent:
                p_ctx = jnp.exp(s_ctx - m)
                denom = denom + jnp.sum(p_ctx, axis=-1, keepdims=True)
                acc = acc + jnp.dot(p_ctx.astype(BF16), cvb[:, ksl], preferred_element_type=F32)
            out = acc / denom
            for g in range(0, groups, 2):
                pair = jnp.concatenate([out[g * tq:(g + 1) * tq], out[(g + 1) * tq:(g + 2) * tq]], axis=1)
                lane0 = heads[g] * HEAD_DIM
                o_ref[rows, lane0:lane0 + 2 * HEAD_DIM] = pair.astype(o_ref.dtype)


def _attention(za, q_gain, k_gain, group_ones, *, batch, seq, row_block0, rope=None, cache=None):
    latent = cache is not None
    za_w = za.shape[1]
    in_specs = [
        pl.BlockSpec((seq, za_w), lambda b: (row_block0 + b, 0)),
        pl.BlockSpec((1, ATT_WIDTH), lambda b: (0, 0)),
        pl.BlockSpec((1, KV_WIDTH), lambda b: (0, 0)),
        pl.BlockSpec((ATT_WIDTH, ATT_WIDTH), lambda b: (0, 0)),
    ]
    args = [za, q_gain, k_gain, group_ones]
    o_spec = pl.BlockSpec((seq, ATT_WIDTH), lambda b: (b, 0))
    o_shape = jax.ShapeDtypeStruct((batch * seq, ATT_WIDTH), BF16)
    if latent:
        cos, sin_signed = rope
        ck, cv = cache
        past = ck.shape[1]
        in_specs += [
            pl.BlockSpec((seq, ATT_WIDTH), lambda b: (0, 0)),
            pl.BlockSpec((seq, ATT_WIDTH), lambda b: (0, 0)),
            pl.BlockSpec((1, past, KV_WIDTH), lambda b: (b, 0, 0)),
            pl.BlockSpec((1, past, KV_WIDTH), lambda b: (b, 0, 0)),
        ]
        args += [cos, sin_signed, ck, cv]
        out_specs = o_spec
        out_shape = o_shape
    else:
        kv_spec = pl.BlockSpec((seq, KV_WIDTH), lambda b: (b, 0))
        kv_shape = jax.ShapeDtypeStruct((batch * seq, KV_WIDTH), F32)
        out_specs = [o_spec, kv_spec, kv_spec]
        out_shape = [o_shape, kv_shape, kv_shape]
    return pl.pallas_call(
        functools.partial(_attn_kernel, seq=seq, latent=latent),
        grid=(batch,),
        in_specs=in_specs,
        out_specs=out_specs,
        out_shape=out_shape,
        compiler_params=_params(("arbitrary",)),
    )(*args)


def _rope_tables(seq):
    quarter = HEAD_DIM // 4
    t = jnp.arange(seq)
    row = (t // GRID_W).astype(F32)
    colp = (t % GRID_W).astype(F32)
    inv_freq = ROPE_THETA ** (-jnp.arange(quarter, dtype=F32) / quarter)
    lane = jnp.arange(HEAD_DIM)
    pos = jnp.where((lane < HEAD_DIM // 2)[None, :], row[:, None], colp[:, None])
    ang = pos * inv_freq[lane % quarter][None, :]
    sign = jnp.where(lane % (2 * quarter) < quarter, -1.0, 1.0)[None, :]
    cos = jnp.tile(jnp.cos(ang), (1, N_HEADS))
    sin_signed = jnp.tile(jnp.sin(ang) * sign, (1, N_HEADS))
    return cos, sin_signed


def _layer_norm(y, g, b):
    mu = jnp.mean(y, axis=-1, keepdims=True)
    yc = y - mu
    var = jnp.mean(yc * yc, axis=-1, keepdims=True)
    return yc * lax.rsqrt(var + NORM_EPS) * g + b


def _rows_to_slabs(tile, st_ref, slab_ref):
    rows, d = tile.shape
    chunks = d // LANES
    for j in range(chunks):
        st_ref[j * TILE_STRIDE:j * TILE_STRIDE + rows, :] = tile[:, j * LANES:(j + 1) * LANES]
    for r in range(rows):
        slab_ref[r * chunks:(r + 1) * chunks, :] = st_ref[pl.ds(r, chunks, stride=TILE_STRIDE), :]


def _slabs_to_rows(slab_ref, st_ref, rows):
    chunks = SUBLANES
    for r in range(rows):
        st_ref[pl.ds(r, chunks, stride=TILE_STRIDE), :] = slab_ref[r * chunks:(r + 1) * chunks, :]
    return jnp.concatenate([st_ref[j * TILE_STRIDE:j * TILE_STRIDE + rows, :] for j in range(chunks)], axis=1)


def _post_mixer_kernel(xc_ref, xl_ref, mod_ref, oac_ref, oal_ref, obc_ref, obl_ref, zg_ref, wa_ref, wb_ref,
                       wo_ref, g_ref, b_ref, wrh_ref, wrl_ref, rb_ref,
                       x1_ref, h2_ref, slab_ref, idx_ref, rw_ref, st_ref, *, d, n_ctx_tiles):
    is_ctx = pl.program_id(0) < n_ctx_tiles
    gate1 = mod_ref[0, :, 2 * d:3 * d]
    shift2 = mod_ref[0, :, 3 * d:4 * d]
    scale2 = mod_ref[0, :, 4 * d:5 * d]
    branch_a = jnp.dot(_pick(is_ctx, oac_ref, oal_ref), wa_ref[...], preferred_element_type=F32)
    branch_b = jnp.dot(_pick(is_ctx, obc_ref, obl_ref), wb_ref[...], preferred_element_type=F32)
    merged = zg_ref[:, 0:d].astype(F32) * branch_a + zg_ref[:, d:2 * d].astype(F32) * branch_b
    mix = jnp.dot(merged.astype(BF16), wo_ref[...], preferred_element_type=F32)
    x = _pick(is_ctx, xc_ref, xl_ref)
    x1 = _layer_norm(DEEPNORM_ALPHA * x + gate1 * mix, g_ref[...], b_ref[...])
    x1_ref[...] = x1
    h2 = x1 * (1.0 + scale2) + shift2
    h2_ref[...] = h2.astype(h2_ref.dtype)
    _rows_to_slabs(h2, st_ref, slab_ref)
    idx_ref[...], rw_ref[...] = _route_tile(h2, wrh_ref[...], wrl_ref[...], rb_ref[...])


def _post_mixer(x_ctx, x_lat, mod3, oa_ctx, oa_lat, ob_ctx, ob_lat, zg, w_a, w_b, w_o, ln_g, ln_b,
                wr_hi, wr_lo, router_bias, mod_map):
    d = x_ctx.shape[1]
    n_ctx_tiles = x_ctx.shape[0] // TOKEN_TILE
    t = x_ctx.shape[0] + x_lat.shape[0]
    tm = TOKEN_TILE
    row = lambda i: (i, 0)
    full = lambda i: (0, 0)
    return pl.pallas_call(
        functools.partial(_post_mixer_kernel, d=d, n_ctx_tiles=n_ctx_tiles),
        grid=(t // tm,),
        in_specs=_pair_specs(d, n_ctx_tiles) + [
            pl.BlockSpec((1, 1, mod3.shape[2]), mod_map),
        ] + _pair_specs(oa_ctx.shape[1], n_ctx_tiles) + _pair_specs(ob_ctx.shape[1], n_ctx_tiles) + [
            pl.BlockSpec((tm, 2 * d), row),
            pl.BlockSpec(w_a.shape, full),
            pl.BlockSpec(w_b.shape, full),
            pl.BlockSpec(w_o.shape, full),
            pl.BlockSpec((1, d), full),
            pl.BlockSpec((1, d), full),
            pl.BlockSpec((N_EXPERTS, d), full),
            pl.BlockSpec((N_EXPERTS, d), full),
            pl.BlockSpec((N_EXPERTS, 1), full),
        ],
        out_specs=[pl.BlockSpec((tm, d), row), pl.BlockSpec((tm, d), row),
                   pl.BlockSpec((tm * SUBLANES, LANES), row),
                   pl.BlockSpec((TOP_K, tm), lambda i: (0, i)), pl.BlockSpec((TOP_K, tm), lambda i: (0, i))],
        out_shape=[jax.ShapeDtypeStruct((t, d), F32), jax.ShapeDtypeStruct((t, d), BF16),
                   jax.ShapeDtypeStruct((t * SUBLANES, LANES), F32),
                   jax.ShapeDtypeStruct((TOP_K, t), jnp.int32), jax.ShapeDtypeStruct((TOP_K, t), F32)],
        scratch_shapes=[pltpu.VMEM((SUBLANES * TILE_STRIDE, LANES), F32)],
        compiler_params=_params(("arbitrary",)),
    )(x_ctx, x_lat, mod3, oa_ctx, oa_lat, ob_ctx, ob_lat, zg, w_a, w_b, w_o, ln_g, ln_b, wr_hi, wr_lo, router_bias)


def _first_index_of_max(x, idx, sentinel):
    m = jnp.max(x, axis=0, keepdims=True)
    first = jnp.min(jnp.where(x == m, idx, sentinel), axis=0, keepdims=True)
    return m, first


def _route_tile(h, wh, wl, bias):
    hh = h.astype(BF16)
    hl = (h - hh.astype(F32)).astype(BF16)
    nt = (((1,), (1,)), ((), ()))
    logits = (lax.dot_general(wh, hh, nt, preferred_element_type=F32)
              + lax.dot_general(wh, hl, nt, preferred_element_type=F32)
              + lax.dot_general(wl, hh, nt, preferred_element_type=F32))
    scores = _sigmoid(logits)
    sel = scores + bias
    tm = sel.shape[1]
    neg = -jnp.inf
    gidx = lax.broadcasted_iota(jnp.int32, (GROUP_SIZE, tm), 0)
    group_scores = []
    for g in range(N_GROUPS):
        sg = sel[g * GROUP_SIZE:(g + 1) * GROUP_SIZE, :]
        m1, first = _first_index_of_max(sg, gidx, GROUP_SIZE)
        m2 = jnp.max(jnp.where(gidx == first, neg, sg), axis=0, keepdims=True)
        group_scores.append(m1 + m2)
    gs = jnp.concatenate(group_scores, axis=0)
    nidx = lax.broadcasted_iota(jnp.int32, (N_GROUPS, tm), 0)
    chosen = jnp.zeros((N_GROUPS, tm), jnp.bool_)
    for _ in range(TOPK_GROUPS):
        _, first = _first_index_of_max(gs, nidx, N_GROUPS)
        hit = nidx == first
        chosen = jnp.logical_or(chosen, hit)
        gs = jnp.where(hit, neg, gs)
    masked = jnp.concatenate(
        [jnp.where(chosen[g:g + 1, :], sel[g * GROUP_SIZE:(g + 1) * GROUP_SIZE, :], neg) for g in range(N_GROUPS)],
        axis=0)
    eidx = lax.broadcasted_iota(jnp.int32, (N_EXPERTS, tm), 0)
    picks, weights = [], []
    for _ in range(TOP_K):
        _, first = _first_index_of_max(masked, eidx, N_EXPERTS)
        hit = eidx == first
        picks.append(first)
        weights.append(jnp.sum(jnp.where(hit, scores, 0.0), axis=0, keepdims=True))
        masked = jnp.where(hit, neg, masked)
    wk = jnp.concatenate(weights, axis=0)
    return jnp.concatenate(picks, axis=0), wk / jnp.sum(wk, axis=0, keepdims=True) * ROUTED_SCALE


def _dispatch_lists(idx_t, w_t):
    k, t = idx_t.shape
    s = k * t
    n_blocks = s // MOE_BLOCK + N_EXPERTS
    flat_tok = jnp.arange(s, dtype=jnp.int32) % t
    sorted_e, sorted_tok, sorted_w = lax.sort((idx_t.reshape(s), flat_tok, w_t.reshape(s)), num_keys=1)
    experts = jnp.arange(N_EXPERTS + 1, dtype=jnp.int32)
    start = jnp.sum(sorted_e[None, :] < experts[:, None], axis=1, dtype=jnp.int32)
    counts = start[1:] - start[:-1]
    blocks_per_e = (counts + MOE_BLOCK - 1) // MOE_BLOCK
    block_end = jnp.cumsum(blocks_per_e)
    n_used = block_end[-1]
    blk = jnp.arange(n_blocks, dtype=jnp.int32)
    block_e = jnp.sum(block_end[None, :] <= blk[:, None], axis=1, dtype=jnp.int32)
    block_e = jnp.minimum(block_e, N_EXPERTS - 1)
    onehot = (block_e[:, None] == experts[None, :N_EXPERTS]).astype(jnp.int32)
    pick = lambda a: jnp.sum(onehot * a[None, :], axis=1)
    offset = (blk - (pick(block_end) - pick(blocks_per_e))) * MOE_BLOCK
    row_start = pick(start[:-1]) + offset
    n_rows = jnp.clip(pick(counts) - offset, 0, MOE_BLOCK)
    used = blk < n_used
    last_e = jnp.sum(jnp.where(blk == n_used - 1, block_e, 0))
    block_e = jnp.where(used, block_e, last_e)
    row_start = jnp.where(used, row_start, 0)
    n_rows = jnp.where(used, n_rows, 0)
    return block_e, row_start, n_rows, (n_used - 1).reshape(1), sorted_tok, sorted_w


def _slab_row(tok):
    return pl.multiple_of(tok * SUBLANES, SUBLANES)


def _expert_kernel(be_ref, rs_ref, nr_ref, last_ref, tok_ref, x_hbm, wg_hbm, wu_hbm, wd_hbm, out_ref,
                   xs, xt, wgb, wub, wdb, sem, wsem, *, n_slots):
    b = pl.program_id(0)
    chunks = wg_hbm.shape[1] // LANES
    n_rows = nr_ref[b]
    last = last_ref[0]
    ahead = WEIGHT_BUFFERS - 1

    def weight_copies(blk):
        e = be_ref[blk]
        slot = blk % WEIGHT_BUFFERS
        return [pltpu.make_async_copy(src.at[e], dst.at[slot], wsem.at[slot])
                for src, dst in ((wg_hbm, wgb), (wu_hbm, wub), (wd_hbm, wdb))]

    @pl.when(b == 0)
    def _load():
        for blk in range(ahead):
            @pl.when(blk <= last)
            def _(blk=blk):
                for cp in weight_copies(blk):
                    cp.start()
        cp = pltpu.make_async_copy(x_hbm, xs, sem)
        cp.start()
        cp.wait()
        xt[...] = jnp.zeros(xt.shape, F32)

    @pl.when(b + ahead <= last)
    def _prefetch():
        for cp in weight_copies(b + ahead):
            cp.start()

    @pl.when(b <= last)
    def _block():
        base = rs_ref[b]
        slot = b % WEIGHT_BUFFERS
        for cp in weight_copies(b):
            cp.wait()
        for seg in range(MOE_BLOCK // ROW_SEGMENT):
            @pl.when(n_rows > seg * ROW_SEGMENT)
            def _gather(seg=seg):
                for r in range(seg * ROW_SEGMENT, (seg + 1) * ROW_SEGMENT):
                    tok = tok_ref[jnp.minimum(base + r, n_slots - 1)]
                    xt[pl.ds(r, chunks, stride=ROW_STRIDE), :] = xs[pl.ds(_slab_row(tok), SUBLANES), :]

        x = jnp.concatenate(
            [xt[j * ROW_STRIDE:j * ROW_STRIDE + MOE_BLOCK, :] for j in range(chunks)], axis=1).astype(BF16)
        gate = jnp.dot(x, wgb[slot].astype(BF16), preferred_element_type=F32)
        up = jnp.dot(x, wub[slot].astype(BF16), preferred_element_type=F32)
        hidden = (_silu(gate) * up).astype(BF16)
        out_ref[...] = jnp.dot(hidden, wdb[slot].astype(BF16), preferred_element_type=F32).astype(out_ref.dtype)


def _combine_kernel(rs_ref, nr_ref, last_ref, tok_ref, w_ref, o_ref, y_hbm, ys, ot, sem, *, t, n_slots, n_blocks):
    b = pl.program_id(0)
    rows = t * SUBLANES
    chunks = o_ref.shape[1] // LANES
    n_rows = nr_ref[b]
    group = ROW_GROUP

    @pl.when(b == 0)
    def _init():
        ys[...] = jnp.zeros(ys.shape, F32)

    @pl.when(n_rows > 0)
    def _block():
        base = rs_ref[b]
        for j in range(chunks):
            ot[j * ROW_STRIDE:j * ROW_STRIDE + MOE_BLOCK, :] = o_ref[:, j * LANES:(j + 1) * LANES].astype(F32)
        for seg in range(MOE_BLOCK // ROW_SEGMENT):
            @pl.when(n_rows > seg * ROW_SEGMENT)
            def _scatter(seg=seg):
                for r0 in range(seg * ROW_SEGMENT, (seg + 1) * ROW_SEGMENT, group):
                    slot = [jnp.minimum(base + r0 + i, n_slots - 1) for i in range(group)]
                    toks = [_slab_row(jnp.where(r0 + i < n_rows, tok_ref[slot[i]], t)) for i in range(group)]
                    vals = [ys[pl.ds(toks[i], SUBLANES), :]
                            + w_ref[slot[i]] * ot[pl.ds(r0 + i, chunks, stride=ROW_STRIDE), :]
                            for i in range(group)]
                    for i in range(group):
                        ys[pl.ds(toks[i], SUBLANES), :] = vals[i]

    @pl.when(b == n_blocks - 1)
    def _store():
        cp = pltpu.make_async_copy(ys.at[pl.ds(0, rows)], y_hbm, sem)
        cp.start()
        cp.wait()


def _routed_experts(h2_rows, block_e, row_start, n_rows, last_block, sorted_tok, sorted_w, w_gate, w_up, w_down):
    rows, lanes = h2_rows.shape
    t = rows // SUBLANES
    n_blocks = block_e.shape[0]
    n_slots = sorted_tok.shape[0]
    d, e_dim = w_gate.shape[1], w_gate.shape[2]
    assert d == SUBLANES * LANES and lanes == LANES
    staging = pltpu.VMEM((SUBLANES * ROW_STRIDE, LANES), F32)
    expert_out = pl.pallas_call(
        functools.partial(_expert_kernel, n_slots=n_slots),
        grid_spec=pltpu.PrefetchScalarGridSpec(
            num_scalar_prefetch=5,
            grid=(n_blocks,),
            in_specs=[
                pl.BlockSpec(memory_space=pl.ANY),
                pl.BlockSpec(memory_space=pl.ANY),
                pl.BlockSpec(memory_space=pl.ANY),
                pl.BlockSpec(memory_space=pl.ANY),
            ],
            out_specs=pl.BlockSpec((MOE_BLOCK, d), lambda b, be, rs, nr, last, tok: (jnp.minimum(b, last[0]), 0)),
            scratch_shapes=[
                pltpu.VMEM((rows, LANES), F32),
                staging,
                pltpu.VMEM((WEIGHT_BUFFERS, d, e_dim), F32),
                pltpu.VMEM((WEIGHT_BUFFERS, d, e_dim), F32),
                pltpu.VMEM((WEIGHT_BUFFERS, e_dim, d), F32),
                pltpu.SemaphoreType.DMA(()),
                pltpu.SemaphoreType.DMA((WEIGHT_BUFFERS,)),
            ],
        ),
        out_shape=jax.ShapeDtypeStruct((n_blocks * MOE_BLOCK, d), BF16),
        compiler_params=_params(("arbitrary",)),
    )(block_e, row_start, n_rows, last_block, sorted_tok, h2_rows, w_gate, w_up, w_down)
    return pl.pallas_call(
        functools.partial(_combine_kernel, t=t, n_slots=n_slots, n_blocks=n_blocks),
        grid_spec=pltpu.PrefetchScalarGridSpec(
            num_scalar_prefetch=5,
            grid=(n_blocks,),
            in_specs=[pl.BlockSpec((MOE_BLOCK, d), lambda b, rs, nr, last, tok, w: (jnp.minimum(b, last[0]), 0))],
            out_specs=pl.BlockSpec(memory_space=pl.ANY),
            scratch_shapes=[pltpu.VMEM((rows + SUBLANES, LANES), F32), staging, pltpu.SemaphoreType.DMA(())],
        ),
        out_shape=jax.ShapeDtypeStruct((rows, LANES), F32),
        compiler_params=_params(("arbitrary",)),
    )(row_start, n_rows, last_block, sorted_tok, sorted_w, expert_out)


def _final_kernel(x1_ref, h2_ref, routed_ref, mod_ref, wg_ref, wu_ref, wd_ref, g_ref, b_ref,
                  yc_ref, yl_ref, st_ref, *, d, n_ctx_tiles):
    is_ctx = pl.program_id(0) < n_ctx_tiles
    gate2 = mod_ref[0, :, 5 * d:6 * d]
    h = h2_ref[...].astype(BF16)
    gate = jnp.dot(h, wg_ref[...], preferred_element_type=F32)
    up = jnp.dot(h, wu_ref[...], preferred_element_type=F32)
    shared = jnp.dot((_silu(gate) * up).astype(BF16), wd_ref[...], preferred_element_type=F32)
    ffn = _slabs_to_rows(routed_ref, st_ref, h.shape[0]) + shared
    y = _layer_norm(DEEPNORM_ALPHA * x1_ref[...] + gate2 * ffn, g_ref[...], b_ref[...])

    @pl.when(is_ctx)
    def _():
        yc_ref[...] = y

    @pl.when(jnp.logical_not(is_ctx))
    def _():
        yl_ref[...] = y


def _final(x1, h2, routed, mod3, w_g, w_u, w_d, ln_g, ln_b, mod_map, t_ctx):
    t, d = x1.shape
    t_lat = t - t_ctx
    n_ctx_tiles = t_ctx // TOKEN_TILE
    tm = TOKEN_TILE
    row = lambda i: (i, 0)
    full = lambda i: (0, 0)
    return pl.pallas_call(
        functools.partial(_final_kernel, d=d, n_ctx_tiles=n_ctx_tiles),
        grid=(t // tm,),
        in_specs=[
            pl.BlockSpec((tm, d), row),
            pl.BlockSpec((tm, d), row),
            pl.BlockSpec((tm * SUBLANES, LANES), row),
            pl.BlockSpec((1, 1, mod3.shape[2]), mod_map),
            pl.BlockSpec(w_g.shape, full),
            pl.BlockSpec(w_u.shape, full),
            pl.BlockSpec(w_d.shape, full),
            pl.BlockSpec((1, d), full),
            pl.BlockSpec((1, d), full),
        ],
        out_specs=_pair_specs(d, n_ctx_tiles),
        out_shape=[jax.ShapeDtypeStruct((t_ctx, d), F32), jax.ShapeDtypeStruct((t_lat, d), F32)],
        scratch_shapes=[pltpu.VMEM((SUBLANES * TILE_STRIDE, LANES), F32)],
        compiler_params=_params(("arbitrary",)),
    )(x1, h2, routed, mod3, w_g, w_u, w_d, ln_g, ln_b)


def kernel(x_prompt, x_sample, cache_k, cache_v, state_hgrn, c, c_ctx, w_mod, b_mod, w_in, hg_lb, hg_norm_g, q_norm_g, k_norm_g, w_branch_a, w_branch_b, w_out, ln1_g, ln1_b, w_router, router_bias, w_e_gate, w_e_up, w_e_down, w_s_gate, w_s_up, w_s_down, ln2_g, ln2_b):
    assert w_mod.shape[0] == DEPTH
    n_ctx, seq_ctx, d = x_prompt.shape
    n_lat, seq_lat, _ = x_sample.shape
    t_ctx = n_ctx * seq_ctx
    t_lat = n_lat * seq_lat
    assert seq_ctx == TOKEN_TILE and seq_lat % TOKEN_TILE == 0
    layer = 0

    lb = jnp.cumsum(jax.nn.softmax(hg_lb.astype(F32), axis=0), axis=0)[layer]

    cond = jnp.concatenate([c_ctx[None, :], c], axis=0)
    cond = jnp.pad(cond, ((0, (-cond.shape[0]) % SUBLANES), (0, 0)))
    mod = _modulation(cond, w_mod[layer].astype(BF16), b_mod[layer][None, :])
    mod3 = mod.reshape(mod.shape[0], 1, mod.shape[1])
    mod_map = _mod_row_map(t_ctx // TOKEN_TILE, seq_lat // TOKEN_TILE)

    x_ctx = x_prompt.reshape(t_ctx, d)
    x_lat = x_sample.reshape(t_lat, d)
    zh, zf, za, zg = _input_projection(x_ctx, x_lat, mod3, w_in[layer].astype(BF16), lb, mod_map)

    norm_g = hg_norm_g[layer][None, :]
    oa_ctx, s_ctx = _hgrn_scan(zh, zf, norm_g, None, batch=n_ctx, seq=seq_ctx, row_block0=0)
    oa_lat, _ = _hgrn_scan(zh, zf, norm_g, state_hgrn[:, layer], batch=n_lat, seq=seq_lat,
                           row_block0=t_ctx // seq_lat)

    q_gain = jnp.tile(q_norm_g[layer], N_HEADS)[None, :]
    k_gain = jnp.tile(k_norm_g[layer], N_KV_HEADS)[None, :]
    lane = jnp.arange(ATT_WIDTH)
    group_ones = (lane[:, None] // HEAD_DIM == lane[None, :] // HEAD_DIM).astype(BF16)
    ob_ctx, k_ctx, v_ctx = _attention(za, q_gain, k_gain, group_ones, batch=n_ctx, seq=seq_ctx, row_block0=0)
    past = cache_k.shape[2]
    ob_lat = _attention(
        za, q_gain, k_gain, group_ones, batch=n_lat, seq=seq_lat, row_block0=t_ctx // seq_lat,
        rope=_rope_tables(seq_lat),
        cache=(cache_k[:, layer].reshape(n_lat, past, KV_WIDTH), cache_v[:, layer].reshape(n_lat, past, KV_WIDTH)))

    wr_t = w_router[layer].T
    wr_hi = wr_t.astype(BF16)
    wr_lo = (wr_t - wr_hi.astype(F32)).astype(BF16)
    x1, h2, h2_slabs, idx_t, w_t = _post_mixer(
        x_ctx, x_lat, mod3, oa_ctx, oa_lat, ob_ctx, ob_lat, zg, w_branch_a[layer].astype(BF16),
        w_branch_b[layer].astype(BF16), w_out[layer].astype(BF16), ln1_g[layer][None, :], ln1_b[layer][None, :],
        wr_hi, wr_lo, router_bias[layer][:, None], mod_map)

    routed = _routed_experts(h2_slabs, *_dispatch_lists(idx_t, w_t), w_e_gate[layer], w_e_up[layer], w_e_down[layer])

    y_ctx, y_lat = _final(x1, h2, routed, mod3, w_s_gate[layer].astype(BF16),
                          w_s_up[layer].astype(BF16), w_s_down[layer].astype(BF16),
                          ln2_g[layer][None, :], ln2_b[layer][None, :], mod_map, t_ctx)

    y_prompt = y_ctx.reshape(n_ctx, seq_ctx, d)
    y_sample = y_lat.reshape(n_lat, seq_lat, d)
    new_cache_k = k_ctx.reshape(n_ctx, 1, seq_ctx, N_KV_HEADS, HEAD_DIM)
    new_cache_v = v_ctx.reshape(n_ctx, 1, seq_ctx, N_KV_HEADS, HEAD_DIM)
    new_state = s_ctx[:, None]
    return (y_prompt, y_sample, new_cache_k, new_cache_v, new_state)
```

```python
import functools

import jax
import jax.numpy as jnp
from jax import lax
from jax.experimental import pallas as pl
from jax.experimental.pallas import tpu as pltpu

F32 = jnp.float32
BF16 = jnp.bfloat16

GRID_W = 64
HG_HEADS = 4
HG_DK = 128
HG_DV = 128
HG_WIDTH = HG_HEADS * HG_DK
N_HEADS = 8
N_KV_HEADS = 2
HEAD_DIM = 64
ATT_WIDTH = N_HEADS * HEAD_DIM
KV_WIDTH = N_KV_HEADS * HEAD_DIM
ROPE_THETA = 10000.0
N_EXPERTS = 256
TOP_K = 8
N_GROUPS = 8
TOPK_GROUPS = 4
GROUP_SIZE = N_EXPERTS // N_GROUPS
ROUTED_SCALE = 2.5
NORM_EPS = 1e-6
DEPTH = 1
DEEPNORM_ALPHA = (2 * DEPTH) ** 0.25

LANES = 128
SUBLANES = 8
VMEM_LIMIT = 56 * 1024 * 1024

TOKEN_TILE = 256
TILE_STRIDE = TOKEN_TILE + 1
HG_CHUNK = 32
ATT_Q_BLOCK = 128
MOE_BLOCK = 320
ROW_STRIDE = MOE_BLOCK + 1
ROW_GROUP = 16
ROW_SEGMENT = 32
WEIGHT_BUFFERS = 4


def _sigmoid(x):
    return 1.0 / (1.0 + jnp.exp(-x))


def _silu(x):
    return x * _sigmoid(x)


def _params(sem=None):
    return pltpu.CompilerParams(dimension_semantics=sem, vmem_limit_bytes=VMEM_LIMIT)


def _mod_kernel(c_ref, w_ref, b_ref, o_ref):
    s = _silu(c_ref[...]).astype(BF16)
    o_ref[...] = jnp.dot(s, w_ref[...], preferred_element_type=F32) + b_ref[...]


def _modulation(cond, w_mod, b_mod):
    n, d = cond.shape
    width = w_mod.shape[1]
    tn = width // 4
    return pl.pallas_call(
        _mod_kernel,
        grid=(4,),
        in_specs=[
            pl.BlockSpec((n, d), lambda j: (0, 0)),
            pl.BlockSpec((d, tn), lambda j: (0, j)),
            pl.BlockSpec((1, tn), lambda j: (0, j)),
        ],
        out_specs=pl.BlockSpec((n, tn), lambda j: (0, j)),
        out_shape=jax.ShapeDtypeStruct((n, width), F32),
        compiler_params=_params(("arbitrary",)),
    )(cond, w_mod, b_mod)


def _inproj_kernel(xc_ref, xl_ref, mod_ref, w_ref, lb_ref, zh_ref, zf_ref, za_ref, zg_ref, *, d, n_ctx_tiles):
    shift = mod_ref[0, :, 0:d]
    scale = mod_ref[0, :, d:2 * d]
    x = _pick(pl.program_id(0) < n_ctx_tiles, xc_ref, xl_ref)
    h = (x * (1.0 + scale) + shift).astype(BF16)

    def proj(lo, hi):
        return jnp.dot(h, w_ref[:, lo:hi], preferred_element_type=F32)

    w = HG_WIDTH
    zh_ref[:, 0:w] = _silu(proj(0, w)).astype(zh_ref.dtype)
    zh_ref[:, w:2 * w] = proj(w, 2 * w).astype(zh_ref.dtype)
    for i in range(2):
        lb = lb_ref[i:i + 1, :]
        zf_ref[:, i * w:(i + 1) * w] = lb + (1.0 - lb) * _sigmoid(proj((2 + i) * w, (3 + i) * w))
    zh_ref[:, 2 * w:3 * w] = _silu(proj(4 * w, 5 * w)).astype(zh_ref.dtype)
    a0 = 5 * w
    a1 = a0 + ATT_WIDTH + 2 * KV_WIDTH
    za_ref[...] = proj(a0, a1)
    for i in range(4):
        lo = a1 + i * (d // 2)
        zg_ref[:, i * (d // 2):(i + 1) * (d // 2)] = _sigmoid(proj(lo, lo + d // 2)).astype(zg_ref.dtype)


def _mod_row_map(n_ctx_tiles, tiles_per_latent):
    def index_map(i):
        row = jnp.where(i < n_ctx_tiles, 0, 1 + (i - n_ctx_tiles) // tiles_per_latent)
        return (row, 0, 0)
    return index_map


def _pair_specs(width, n_ctx_tiles):
    return [pl.BlockSpec((TOKEN_TILE, width), lambda i: (jnp.minimum(i, n_ctx_tiles - 1), 0)),
            pl.BlockSpec((TOKEN_TILE, width), lambda i: (jnp.maximum(i - n_ctx_tiles, 0), 0))]


def _pick(is_ctx, ctx_ref, lat_ref):
    return jnp.where(is_ctx, ctx_ref[...], lat_ref[...])


def _input_projection(x_ctx, x_lat, mod3, w_in, lb, mod_map):
    d = x_ctx.shape[1]
    n_ctx_tiles = x_ctx.shape[0] // TOKEN_TILE
    t = x_ctx.shape[0] + x_lat.shape[0]
    width = w_in.shape[1]
    zh_w = 3 * HG_WIDTH
    zf_w = 2 * HG_WIDTH
    za_w = ATT_WIDTH + 2 * KV_WIDTH
    zg_w = 2 * d
    assert width == zh_w + zf_w + za_w + zg_w
    tm = TOKEN_TILE
    return pl.pallas_call(
        functools.partial(_inproj_kernel, d=d, n_ctx_tiles=n_ctx_tiles),
        grid=(t // tm,),
        in_specs=_pair_specs(d, n_ctx_tiles) + [
            pl.BlockSpec((1, 1, mod3.shape[2]), mod_map),
            pl.BlockSpec((d, width), lambda i: (0, 0)),
            pl.BlockSpec((2, HG_WIDTH), lambda i: (0, 0)),
        ],
        out_specs=[
            pl.BlockSpec((tm, zh_w), lambda i: (i, 0)),
            pl.BlockSpec((tm, zf_w), lambda i: (i, 0)),
            pl.BlockSpec((tm, za_w), lambda i: (i, 0)),
            pl.BlockSpec((tm, zg_w), lambda i: (i, 0)),
        ],
        out_shape=[
            jax.ShapeDtypeStruct((t, zh_w), BF16),
            jax.ShapeDtypeStruct((t, zf_w), F32),
            jax.ShapeDtypeStruct((t, za_w), F32),
            jax.ShapeDtypeStruct((t, zg_w), BF16),
        ],
        compiler_params=_params(("arbitrary",)),
    )(x_ctx, x_lat, mod3, w_in, lb)


def _hgrn_kernel(*refs, seq, has_state):
    if has_state:
        q_ref, v_ref, ff_ref, fb_ref, gs_ref, ng_ref, s0_ref, o_ref, sout_ref = refs
    else:
        q_ref, v_ref, ff_ref, fb_ref, gs_ref, ng_ref, o_ref, sout_ref = refs
        s0_ref = None
    c = HG_CHUNK
    n = seq // c
    q3 = q_ref[...].astype(F32).reshape(n, c, HG_DK)
    v3 = v_ref[...].reshape(n, c, HG_DV)
    pos = lax.broadcasted_iota(jnp.int32, (seq, HG_DK), 0) % c
    t_idx = lax.broadcasted_iota(jnp.int32, (c, c), 0)
    s_idx = lax.broadcasted_iota(jnp.int32, (c, c), 1)
    o_sum = None
    for direction, f_ref in enumerate((ff_ref, fb_ref)):
        reverse = direction == 1
        f = f_ref[...]
        k3 = (1.0 - f).reshape(n, c, HG_DK)
        b = jnp.log(f)
        step = 1
        while step < c:
            if reverse:
                b = b + jnp.where(pos < c - step, pltpu.roll(b, seq - step, axis=0), 0.0)
            else:
                b = b + jnp.where(pos >= step, pltpu.roll(b, step, axis=0), 0.0)
            step *= 2
        b3 = b.reshape(n, c, HG_DK)
        edge = b3[:, 0:1, :] if reverse else b3[:, c - 1:c, :]
        mid = b3[:, c // 2:c // 2 + 1, :]
        q_rel = q3 * jnp.exp(b3 - mid)
        k_rel = k3 * jnp.exp(mid - b3)
        q_mid = q_rel.astype(BF16)
        k_mid = k_rel.astype(BF16)
        q_in = (q_rel * jnp.exp(mid)).astype(BF16)
        k_end = (k_rel * jnp.exp(edge - mid)).astype(BF16)
        scores = jnp.einsum('ntd,nsd->nts', q_mid, k_mid, preferred_element_type=F32)
        keep = (s_idx >= t_idx) if reverse else (s_idx <= t_idx)
        scores = jnp.where(keep[None], scores, 0.0).astype(BF16)
        o_intra = jnp.einsum('nts,nsv->ntv', scores, v3, preferred_element_type=F32)
        upd = jnp.einsum('nsv,nsd->nvd', v3, k_end, preferred_element_type=F32)
        dec = jnp.exp(edge)
        if has_state:
            st = s0_ref[0, direction, 0].T
        else:
            st = jnp.zeros((HG_DV, HG_DK), F32)
        before = [None] * n
        for ci in (range(n - 1, -1, -1) if reverse else range(n)):
            before[ci] = st.astype(BF16)
            st = st * dec[ci] + upd[ci]
        st_before = jnp.stack(before, axis=0)
        o_inter = jnp.einsum('ntd,nvd->ntv', q_in, st_before, preferred_element_type=F32)
        o_dir = (o_intra + o_inter).reshape(seq, HG_DV)
        o_sum = o_dir if o_sum is None else o_sum + o_dir
        sout_ref[0, direction, 0] = st.T
    ms = jnp.mean(o_sum * o_sum, axis=-1, keepdims=True)
    o = o_sum * lax.rsqrt(ms + NORM_EPS) * ng_ref[...]
    o_ref[...] = (o * gs_ref[...].astype(F32)).astype(o_ref.dtype)


def _hgrn_scan(zh, zf, norm_g, s0, *, batch, seq, row_block0):
    has_state = s0 is not None
    h = HG_HEADS

    def col(section):
        return pl.BlockSpec((seq, HG_DK), lambda b, j: (row_block0 + b, section * h + j))

    in_specs = [col(0), col(1), col(0), col(1), col(2), pl.BlockSpec((1, HG_DV), lambda b, j: (0, 0))]
    args = [zh, zh, zf, zf, zh, norm_g]
    state_spec = pl.BlockSpec((1, 2, 1, HG_DK, HG_DV), lambda b, j: (b, 0, j, 0, 0))
    if has_state:
        in_specs.append(state_spec)
        args.append(s0)
    return pl.pallas_call(
        functools.partial(_hgrn_kernel, seq=seq, has_state=has_state),
        grid=(batch, h),
        in_specs=in_specs,
        out_specs=[pl.BlockSpec((seq, HG_DV), lambda b, j: (b, j)), state_spec],
        out_shape=[
            jax.ShapeDtypeStruct((batch * seq, h * HG_DV), BF16),
            jax.ShapeDtypeStruct((batch, 2, h, HG_DK, HG_DV), F32),
        ],
        compiler_params=_params(("arbitrary", "arbitrary")),
    )(*args)


def _group_rms_norm(x, group_ones, gain):
    sq = x * x
    hi = sq.astype(BF16)
    lo = (sq - hi.astype(F32)).astype(BF16)
    total = (jnp.dot(hi, group_ones, preferred_element_type=F32)
             + jnp.dot(lo, group_ones, preferred_element_type=F32))
    return x * lax.rsqrt(total * (1.0 / HEAD_DIM) + NORM_EPS) * gain


def _rope(x, cos, sin_signed):
    width = x.shape[1]
    quarter = HEAD_DIM // 4
    lane = lax.broadcasted_iota(jnp.int32, x.shape, 1)
    partner = jnp.where(lane % (2 * quarter) < quarter,
                        pltpu.roll(x, width - quarter, axis=1),
                        pltpu.roll(x, quarter, axis=1))
    return x * cos + partner * sin_signed


def _attn_kernel(*refs, seq, latent):
    if latent:
        za_ref, qg_ref, kg_ref, gm_ref, cos_ref, sin_ref, ck_ref, cv_ref, o_ref = refs
    else:
        za_ref, qg_ref, kg_ref, gm_ref, o_ref, kout_ref, vout_ref = refs
    q = za_ref[:, 0:ATT_WIDTH]
    k = za_ref[:, ATT_WIDTH:ATT_WIDTH + KV_WIDTH]
    v = za_ref[:, ATT_WIDTH + KV_WIDTH:ATT_WIDTH + 2 * KV_WIDTH]
    qn = _group_rms_norm(q, gm_ref[...], qg_ref[...])
    kn = _group_rms_norm(k, gm_ref[0:KV_WIDTH, 0:KV_WIDTH], kg_ref[...])
    if latent:
        qn = _rope(qn, cos_ref[...], sin_ref[...])
        kr = _rope(kn, cos_ref[:, 0:KV_WIDTH], sin_ref[:, 0:KV_WIDTH])
    else:
        kout_ref[...] = kn
        vout_ref[...] = v
        kr = kn
    qb = (qn * (HEAD_DIM ** -0.5)).astype(BF16)
    kb = kr.astype(BF16)
    vb = v.astype(BF16)
    if latent:
        ckb = ck_ref[0].astype(BF16)
        cvb = cv_ref[0].astype(BF16)
    groups = N_HEADS // N_KV_HEADS
    tq = ATT_Q_BLOCK if latent else seq
    nt = (((1,), (1,)), ((), ()))
    for kh in range(N_KV_HEADS):
        ksl = slice(kh * HEAD_DIM, (kh + 1) * HEAD_DIM)
        k_new = kb[:, ksl]
        v_new = vb[:, ksl]
        for blk in range(seq // tq):
            rows = slice(blk * tq, (blk + 1) * tq)
            heads = [kh * groups + g for g in range(groups)]
            q_st = jnp.concatenate([qb[rows, hd * HEAD_DIM:(hd + 1) * HEAD_DIM] for hd in heads], axis=0)
            s_new = lax.dot_general(q_st, k_new, nt, preferred_element_type=F32)
            m = jnp.max(s_new, axis=-1, keepdims=True)
            if latent:
                s_ctx = lax.dot_general(q_st, ckb[:, ksl], nt, preferred_element_type=F32)
                m = jnp.maximum(m, jnp.max(s_ctx, axis=-1, keepdims=True))
            p_new = jnp.exp(s_new - m)
            denom = jnp.sum(p_new, axis=-1, keepdims=True)
            acc = jnp.dot(p_new.astype(BF16), v_new, preferred_element_type=F32)
            if latent:
                p_ctx = jnp.exp(s_ctx - m)
                denom = denom + jnp.sum(p_ctx, axis=-1, keepdims=True)
                acc = acc + jnp.dot(p_ctx.astype(BF16), cvb[:, ksl], preferred_element_type=F32)
            out = acc / denom
            for g in range(0, groups, 2):
                pair = jnp.concatenate([out[g * tq:(g + 1) * tq], out[(g + 1) * tq:(g + 2) * tq]], axis=1)
                lane0 = heads[g] * HEAD_DIM
                o_ref[rows, lane0:lane0 + 2 * HEAD_DIM] = pair.astype(o_ref.dtype)


def _attention(za, q_gain, k_gain, group_ones, *, batch, seq, row_block0, rope=None, cache=None):
    latent = cache is not None
    za_w = za.shape[1]
    in_specs = [
        pl.BlockSpec((seq, za_w), lambda b: (row_block0 + b, 0)),
        pl.BlockSpec((1, ATT_WIDTH), lambda b: (0, 0)),
        pl.BlockSpec((1, KV_WIDTH), lambda b: (0, 0)),
        pl.BlockSpec((ATT_WIDTH, ATT_WIDTH), lambda b: (0, 0)),
    ]
    args = [za, q_gain, k_gain, group_ones]
    o_spec = pl.BlockSpec((seq, ATT_WIDTH), lambda b: (b, 0))
    o_shape = jax.ShapeDtypeStruct((batch * seq, ATT_WIDTH), BF16)
    if latent:
        cos, sin_signed = rope
        ck, cv = cache
        past = ck.shape[1]
        in_specs += [
            pl.BlockSpec((seq, ATT_WIDTH), lambda b: (0, 0)),
            pl.BlockSpec((seq, ATT_WIDTH), lambda b: (0, 0)),
            pl.BlockSpec((1, past, KV_WIDTH), lambda b: (b, 0, 0)),
            pl.BlockSpec((1, past, KV_WIDTH), lambda b: (b, 0, 0)),
        ]
        args += [cos, sin_signed, ck, cv]
        out_specs = o_spec
        out_shape = o_shape
    else:
        kv_spec = pl.BlockSpec((seq, KV_WIDTH), lambda b: (b, 0))
        kv_shape = jax.ShapeDtypeStruct((batch * seq, KV_WIDTH), F32)
        out_specs = [o_spec, kv_spec, kv_spec]
        out_shape = [o_shape, kv_shape, kv_shape]
    return pl.pallas_call(
        functools.partial(_attn_kernel, seq=seq, latent=latent),
        grid=(batch,),
        in_specs=in_specs,
        out_specs=out_specs,
        out_shape=out_shape,
        compiler_params=_params(("arbitrary",)),
    )(*args)


def _rope_tables(seq):
    quarter = HEAD_DIM // 4
    t = jnp.arange(seq)
    row = (t // GRID_W).astype(F32)
    colp = (t % GRID_W).astype(F32)
    inv_freq = ROPE_THETA ** (-jnp.arange(quarter, dtype=F32) / quarter)
    lane = jnp.arange(HEAD_DIM)
    pos = jnp.where((lane < HEAD_DIM // 2)[None, :], row[:, None], colp[:, None])
    ang = pos * inv_freq[lane % quarter][None, :]
    sign = jnp.where(lane % (2 * quarter) < quarter, -1.0, 1.0)[None, :]
    cos = jnp.tile(jnp.cos(ang), (1, N_HEADS))
    sin_signed = jnp.tile(jnp.sin(ang) * sign, (1, N_HEADS))
    return cos, sin_signed


def _layer_norm(y, g, b):
    mu = jnp.mean(y, axis=-1, keepdims=True)
    yc = y - mu
    var = jnp.mean(yc * yc, axis=-1, keepdims=True)
    return yc * lax.rsqrt(var + NORM_EPS) * g + b


def _rows_to_slabs(tile, st_ref, slab_ref):
    rows, d = tile.shape
    chunks = d // LANES
    for j in range(chunks):
        st_ref[j * TILE_STRIDE:j * TILE_STRIDE + rows, :] = tile[:, j * LANES:(j + 1) * LANES]
    for r in range(rows):
        slab_ref[r * chunks:(r + 1) * chunks, :] = st_ref[pl.ds(r, chunks, stride=TILE_STRIDE), :]


def _slabs_to_rows(slab_ref, st_ref, rows):
    chunks = SUBLANES
    for r in range(rows):
        st_ref[pl.ds(r, chunks, stride=TILE_STRIDE), :] = slab_ref[r * chunks:(r + 1) * chunks, :]
    return jnp.concatenate([st_ref[j * TILE_STRIDE:j * TILE_STRIDE + rows, :] for j in range(chunks)], axis=1)


def _post_mixer_kernel(xc_ref, xl_ref, mod_ref, oac_ref, oal_ref, obc_ref, obl_ref, zg_ref, wa_ref, wb_ref,
                       wo_ref, g_ref, b_ref, wrh_ref, wrl_ref, rb_ref,
                       x1_ref, h2_ref, slab_ref, idx_ref, rw_ref, st_ref, *, d, n_ctx_tiles):
    is_ctx = pl.program_id(0) < n_ctx_tiles
    gate1 = mod_ref[0, :, 2 * d:3 * d]
    shift2 = mod_ref[0, :, 3 * d:4 * d]
    scale2 = mod_ref[0, :, 4 * d:5 * d]
    branch_a = jnp.dot(_pick(is_ctx, oac_ref, oal_ref), wa_ref[...], preferred_element_type=F32)
    branch_b = jnp.dot(_pick(is_ctx, obc_ref, obl_ref), wb_ref[...], preferred_element_type=F32)
    merged = zg_ref[:, 0:d].astype(F32) * branch_a + zg_ref[:, d:2 * d].astype(F32) * branch_b
    mix = jnp.dot(merged.astype(BF16), wo_ref[...], preferred_element_type=F32)
    x = _pick(is_ctx, xc_ref, xl_ref)
    x1 = _layer_norm(DEEPNORM_ALPHA * x + gate1 * mix, g_ref[...], b_ref[...])
    x1_ref[...] = x1
    h2 = x1 * (1.0 + scale2) + shift2
    h2_ref[...] = h2.astype(h2_ref.dtype)
    _rows_to_slabs(h2, st_ref, slab_ref)
    idx_ref[...], rw_ref[...] = _route_tile(h2, wrh_ref[...], wrl_ref[...], rb_ref[...])


def _post_mixer(x_ctx, x_lat, mod3, oa_ctx, oa_lat, ob_ctx, ob_lat, zg, w_a, w_b, w_o, ln_g, ln_b,
                wr_hi, wr_lo, router_bias, mod_map):
    d = x_ctx.shape[1]
    n_ctx_tiles = x_ctx.shape[0] // TOKEN_TILE
    t = x_ctx.shape[0] + x_lat.shape[0]
    tm = TOKEN_TILE
    row = lambda i: (i, 0)
    full = lambda i: (0, 0)
    return pl.pallas_call(
        functools.partial(_post_mixer_kernel, d=d, n_ctx_tiles=n_ctx_tiles),
        grid=(t // tm,),
        in_specs=_pair_specs(d, n_ctx_tiles) + [
            pl.BlockSpec((1, 1, mod3.shape[2]), mod_map),
        ] + _pair_specs(oa_ctx.shape[1], n_ctx_tiles) + _pair_specs(ob_ctx.shape[1], n_ctx_tiles) + [
            pl.BlockSpec((tm, 2 * d), row),
            pl.BlockSpec(w_a.shape, full),
            pl.BlockSpec(w_b.shape, full),
            pl.BlockSpec(w_o.shape, full),
            pl.BlockSpec((1, d), full),
            pl.BlockSpec((1, d), full),
            pl.BlockSpec((N_EXPERTS, d), full),
            pl.BlockSpec((N_EXPERTS, d), full),
            pl.BlockSpec((N_EXPERTS, 1), full),
        ],
        out_specs=[pl.BlockSpec((tm, d), row), pl.BlockSpec((tm, d), row),
                   pl.BlockSpec((tm * SUBLANES, LANES), row),
                   pl.BlockSpec((TOP_K, tm), lambda i: (0, i)), pl.BlockSpec((TOP_K, tm), lambda i: (0, i))],
        out_shape=[jax.ShapeDtypeStruct((t, d), F32), jax.ShapeDtypeStruct((t, d), BF16),
                   jax.ShapeDtypeStruct((t * SUBLANES, LANES), F32),
                   jax.ShapeDtypeStruct((TOP_K, t), jnp.int32), jax.ShapeDtypeStruct((TOP_K, t), F32)],
        scratch_shapes=[pltpu.VMEM((SUBLANES * TILE_STRIDE, LANES), F32)],
        compiler_params=_params(("arbitrary",)),
    )(x_ctx, x_lat, mod3, oa_ctx, oa_lat, ob_ctx, ob_lat, zg, w_a, w_b, w_o, ln_g, ln_b, wr_hi, wr_lo, router_bias)


def _first_index_of_max(x, idx, sentinel):
    m = jnp.max(x, axis=0, keepdims=True)
    first = jnp.min(jnp.where(x == m, idx, sentinel), axis=0, keepdims=True)
    return m, first


def _route_tile(h, wh, wl, bias):
    hh = h.astype(BF16)
    hl = (h - hh.astype(F32)).astype(BF16)
    nt = (((1,), (1,)), ((), ()))
    logits = (lax.dot_general(wh, hh, nt, preferred_element_type=F32)
              + lax.dot_general(wh, hl, nt, preferred_element_type=F32)
              + lax.dot_general(wl, hh, nt, preferred_element_type=F32))
    scores = _sigmoid(logits)
    sel = scores + bias
    tm = sel.shape[1]
    neg = -jnp.inf
    gidx = lax.broadcasted_iota(jnp.int32, (GROUP_SIZE, tm), 0)
    group_scores = []
    for g in range(N_GROUPS):
        sg = sel[g * GROUP_SIZE:(g + 1) * GROUP_SIZE, :]
        m1, first = _first_index_of_max(sg, gidx, GROUP_SIZE)
        m2 = jnp.max(jnp.where(gidx == first, neg, sg), axis=0, keepdims=True)
        group_scores.append(m1 + m2)
    gs = jnp.concatenate(group_scores, axis=0)
    nidx = lax.broadcasted_iota(jnp.int32, (N_GROUPS, tm), 0)
    chosen = jnp.zeros((N_GROUPS, tm), jnp.bool_)
    for _ in range(TOPK_GROUPS):
        _, first = _first_index_of_max(gs, nidx, N_GROUPS)
        hit = nidx == first
        chosen = jnp.logical_or(chosen, hit)
        gs = jnp.where(hit, neg, gs)
    masked = jnp.concatenate(
        [jnp.where(chosen[g:g + 1, :], sel[g * GROUP_SIZE:(g + 1) * GROUP_SIZE, :], neg) for g in range(N_GROUPS)],
        axis=0)
    eidx = lax.broadcasted_iota(jnp.int32, (N_EXPERTS, tm), 0)
    picks, weights = [], []
    for _ in range(TOP_K):
        _, first = _first_index_of_max(masked, eidx, N_EXPERTS)
        hit = eidx == first
        picks.append(first)
        weights.append(jnp.sum(jnp.where(hit, scores, 0.0), axis=0, keepdims=True))
        masked = jnp.where(hit, neg, masked)
    wk = jnp.concatenate(weights, axis=0)
    return jnp.concatenate(picks, axis=0), wk / jnp.sum(wk, axis=0, keepdims=True) * ROUTED_SCALE


def _dispatch_lists(idx_t, w_t):
    k, t = idx_t.shape
    s = k * t
    n_blocks = s // MOE_BLOCK + N_EXPERTS
    flat_tok = jnp.arange(s, dtype=jnp.int32) % t
    sorted_e, sorted_tok, sorted_w = lax.sort((idx_t.reshape(s), flat_tok, w_t.reshape(s)), num_keys=1)
    experts = jnp.arange(N_EXPERTS + 1, dtype=jnp.int32)
    start = jnp.sum(sorted_e[None, :] < experts[:, None], axis=1, dtype=jnp.int32)
    counts = start[1:] - start[:-1]
    blocks_per_e = (counts + MOE_BLOCK - 1) // MOE_BLOCK
    block_end = jnp.cumsum(blocks_per_e)
    n_used = block_end[-1]
    blk = jnp.arange(n_blocks, dtype=jnp.int32)
    block_e = jnp.sum(block_end[None, :] <= blk[:, None], axis=1, dtype=jnp.int32)
    block_e = jnp.minimum(block_e, N_EXPERTS - 1)
    onehot = (block_e[:, None] == experts[None, :N_EXPERTS]).astype(jnp.int32)
    pick = lambda a: jnp.sum(onehot * a[None, :], axis=1)
    offset = (blk - (pick(block_end) - pick(blocks_per_e))) * MOE_BLOCK
    row_start = pick(start[:-1]) + offset
    n_rows = jnp.clip(pick(counts) - offset, 0, MOE_BLOCK)
    used = blk < n_used
    last_e = jnp.sum(jnp.where(blk == n_used - 1, block_e, 0))
    block_e = jnp.where(used, block_e, last_e)
    row_start = jnp.where(used, row_start, 0)
    n_rows = jnp.where(used, n_rows, 0)
    return block_e, row_start, n_rows, (n_used - 1).reshape(1), sorted_tok, sorted_w


def _slab_row(tok):
    return pl.multiple_of(tok * SUBLANES, SUBLANES)


def _expert_kernel(be_ref, rs_ref, nr_ref, last_ref, tok_ref, x_hbm, wg_hbm, wu_hbm, wd_hbm, out_ref,
                   xs, xt, wgb, wub, wdb, sem, wsem, *, n_slots):
    b = pl.program_id(0)
    chunks = wg_hbm.shape[1] // LANES
    n_rows = nr_ref[b]
    last = last_ref[0]
    ahead = WEIGHT_BUFFERS - 1

    def weight_copies(blk):
        e = be_ref[blk]
        slot = blk % WEIGHT_BUFFERS
        return [pltpu.make_async_copy(src.at[e], dst.at[slot], wsem.at[slot])
                for src, dst in ((wg_hbm, wgb), (wu_hbm, wub), (wd_hbm, wdb))]

    @pl.when(b == 0)
    def _load():
        for blk in range(ahead):
            @pl.when(blk <= last)
            def _(blk=blk):
                for cp in weight_copies(blk):
                    cp.start()
        cp = pltpu.make_async_copy(x_hbm, xs, sem)
        cp.start()
        cp.wait()
        xt[...] = jnp.zeros(xt.shape, F32)

    @pl.when(b + ahead <= last)
    def _prefetch():
        for cp in weight_copies(b + ahead):
            cp.start()

    @pl.when(b <= last)
    def _block():
        base = rs_ref[b]
        slot = b % WEIGHT_BUFFERS
        for cp in weight_copies(b):
            cp.wait()
        for seg in range(MOE_BLOCK // ROW_SEGMENT):
            @pl.when(n_rows > seg * ROW_SEGMENT)
            def _gather(seg=seg):
                for r in range(seg * ROW_SEGMENT, (seg + 1) * ROW_SEGMENT):
                    tok = tok_ref[jnp.minimum(base + r, n_slots - 1)]
                    xt[pl.ds(r, chunks, stride=ROW_STRIDE), :] = xs[pl.ds(_slab_row(tok), SUBLANES), :]

        x = jnp.concatenate(
            [xt[j * ROW_STRIDE:j * ROW_STRIDE + MOE_BLOCK, :] for j in range(chunks)], axis=1).astype(BF16)
        gate = jnp.dot(x, wgb[slot].astype(BF16), preferred_element_type=F32)
        up = jnp.dot(x, wub[slot].astype(BF16), preferred_element_type=F32)
        hidden = (_silu(gate) * up).astype(BF16)
        out_ref[...] = jnp.dot(hidden, wdb[slot].astype(BF16), preferred_element_type=F32).astype(out_ref.dtype)


def _combine_kernel(rs_ref, nr_ref, last_ref, tok_ref, w_ref, o_ref, y_hbm, ys, ot, sem, *, t, n_slots, n_blocks):
    b = pl.program_id(0)
    rows = t * SUBLANES
    chunks = o_ref.shape[1] // LANES
    n_rows = nr_ref[b]
    group = ROW_GROUP

    @pl.when(b == 0)
    def _init():
        ys[...] = jnp.zeros(ys.shape, F32)

    @pl.when(n_rows > 0)
    def _block():
        base = rs_ref[b]
        for j in range(chunks):
            ot[j * ROW_STRIDE:j * ROW_STRIDE + MOE_BLOCK, :] = o_ref[:, j * LANES:(j + 1) * LANES].astype(F32)
        for seg in range(MOE_BLOCK // ROW_SEGMENT):
            @pl.when(n_rows > seg * ROW_SEGMENT)
            def _scatter(seg=seg):
                for r0 in range(seg * ROW_SEGMENT, (seg + 1) * ROW_SEGMENT, group):
                    slot = [jnp.minimum(base + r0 + i, n_slots - 1) for i in range(group)]
                    toks = [_slab_row(jnp.where(r0 + i < n_rows, tok_ref[slot[i]], t)) for i in range(group)]
                    vals = [ys[pl.ds(toks[i], SUBLANES), :]
                            + w_ref[slot[i]] * ot[pl.ds(r0 + i, chunks, stride=ROW_STRIDE), :]
                            for i in range(group)]
                    for i in range(group):
                        ys[pl.ds(toks[i], SUBLANES), :] = vals[i]

    @pl.when(b == n_blocks - 1)
    def _store():
        cp = pltpu.make_async_copy(ys.at[pl.ds(0, rows)], y_hbm, sem)
        cp.start()
        cp.wait()


def _routed_experts(h2_rows, block_e, row_start, n_rows, last_block, sorted_tok, sorted_w, w_gate, w_up, w_down):
    rows, lanes = h2_rows.shape
    t = rows // SUBLANES
    n_blocks = block_e.shape[0]
    n_slots = sorted_tok.shape[0]
    d, e_dim = w_gate.shape[1], w_gate.shape[2]
    assert d == SUBLANES * LANES and lanes == LANES
    staging = pltpu.VMEM((SUBLANES * ROW_STRIDE, LANES), F32)
    expert_out = pl.pallas_call(
        functools.partial(_expert_kernel, n_slots=n_slots),
        grid_spec=pltpu.PrefetchScalarGridSpec(
            num_scalar_prefetch=5,
            grid=(n_blocks,),
            in_specs=[
                pl.BlockSpec(memory_space=pl.ANY),
                pl.BlockSpec(memory_space=pl.ANY),
                pl.BlockSpec(memory_space=pl.ANY),
                pl.BlockSpec(memory_space=pl.ANY),
            ],
            out_specs=pl.BlockSpec((MOE_BLOCK, d), lambda b, be, rs, nr, last, tok: (jnp.minimum(b, last[0]), 0)),
            scratch_shapes=[
                pltpu.VMEM((rows, LANES), F32),
                staging,
                pltpu.VMEM((WEIGHT_BUFFERS, d, e_dim), F32),
                pltpu.VMEM((WEIGHT_BUFFERS, d, e_dim), F32),
                pltpu.VMEM((WEIGHT_BUFFERS, e_dim, d), F32),
                pltpu.SemaphoreType.DMA(()),
                pltpu.SemaphoreType.DMA((WEIGHT_BUFFERS,)),
            ],
        ),
        out_shape=jax.ShapeDtypeStruct((n_blocks * MOE_BLOCK, d), BF16),
        compiler_params=_params(("arbitrary",)),
    )(block_e, row_start, n_rows, last_block, sorted_tok, h2_rows, w_gate, w_up, w_down)
    return pl.pallas_call(
        functools.partial(_combine_kernel, t=t, n_slots=n_slots, n_blocks=n_blocks),
        grid_spec=pltpu.PrefetchScalarGridSpec(
            num_scalar_prefetch=5,
            grid=(n_blocks,),
            in_specs=[pl.BlockSpec((MOE_BLOCK, d), lambda b, rs, nr, last, tok, w: (jnp.minimum(b, last[0]), 0))],
            out_specs=pl.BlockSpec(memory_space=pl.ANY),
            scratch_shapes=[pltpu.VMEM((rows + SUBLANES, LANES), F32), staging, pltpu.SemaphoreType.DMA(())],
        ),
        out_shape=jax.ShapeDtypeStruct((rows, LANES), F32),
        compiler_params=_params(("arbitrary",)),
    )(row_start, n_rows, last_block, sorted_tok, sorted_w, expert_out)


def _final_kernel(x1_ref, h2_ref, routed_ref, mod_ref, wg_ref, wu_ref, wd_ref, g_ref, b_ref,
                  yc_ref, yl_ref, st_ref, *, d, n_ctx_tiles):
    is_ctx = pl.program_id(0) < n_ctx_tiles
    gate2 = mod_ref[0, :, 5 * d:6 * d]
    h = h2_ref[...].astype(BF16)
    gate = jnp.dot(h, wg_ref[...], preferred_element_type=F32)
    up = jnp.dot(h, wu_ref[...], preferred_element_type=F32)
    shared = jnp.dot((_silu(gate) * up).astype(BF16), wd_ref[...], preferred_element_type=F32)
    ffn = _slabs_to_rows(routed_ref, st_ref, h.shape[0]) + shared
    y = _layer_norm(DEEPNORM_ALPHA * x1_ref[...] + gate2 * ffn, g_ref[...], b_ref[...])

    @pl.when(is_ctx)
    def _():
        yc_ref[...] = y

    @pl.when(jnp.logical_not(is_ctx))
    def _():
        yl_ref[...] = y


def _final(x1, h2, routed, mod3, w_g, w_u, w_d, ln_g, ln_b, mod_map, t_ctx):
    t, d = x1.shape
    t_lat = t - t_ctx
    n_ctx_tiles = t_ctx // TOKEN_TILE
    tm = TOKEN_TILE
    row = lambda i: (i, 0)
    full = lambda i: (0, 0)
    return pl.pallas_call(
        functools.partial(_final_kernel, d=d, n_ctx_tiles=n_ctx_tiles),
        grid=(t // tm,),
        in_specs=[
            pl.BlockSpec((tm, d), row),
            pl.BlockSpec((tm, d), row),
            pl.BlockSpec((tm * SUBLANES, LANES), row),
            pl.BlockSpec((1, 1, mod3.shape[2]), mod_map),
            pl.BlockSpec(w_g.shape, full),
            pl.BlockSpec(w_u.shape, full),
            pl.BlockSpec(w_d.shape, full),
            pl.BlockSpec((1, d), full),
            pl.BlockSpec((1, d), full),
        ],
        out_specs=_pair_specs(d, n_ctx_tiles),
        out_shape=[jax.ShapeDtypeStruct((t_ctx, d), F32), jax.ShapeDtypeStruct((t_lat, d), F32)],
        scratch_shapes=[pltpu.VMEM((SUBLANES * TILE_STRIDE, LANES), F32)],
        compiler_params=_params(("arbitrary",)),
    )(x1, h2, routed, mod3, w_g, w_u, w_d, ln_g, ln_b)


def kernel(x_prompt, x_sample, cache_k, cache_v, state_hgrn, c, c_ctx, w_mod, b_mod, w_in, hg_lb, hg_norm_g, q_norm_g, k_norm_g, w_branch_a, w_branch_b, w_out, ln1_g, ln1_b, w_router, router_bias, w_e_gate, w_e_up, w_e_down, w_s_gate, w_s_up, w_s_down, ln2_g, ln2_b):
    assert w_mod.shape[0] == DEPTH
    n_ctx, seq_ctx, d = x_prompt.shape
    n_lat, seq_lat, _ = x_sample.shape
    t_ctx = n_ctx * seq_ctx
    t_lat = n_lat * seq_lat
    assert seq_ctx == TOKEN_TILE and seq_lat % TOKEN_TILE == 0
    layer = 0

    lb = jnp.cumsum(jax.nn.softmax(hg_lb.astype(F32), axis=0), axis=0)[layer]

    cond = jnp.concatenate([c_ctx[None, :], c], axis=0)
    cond = jnp.pad(cond, ((0, (-cond.shape[0]) % SUBLANES), (0, 0)))
    mod = _modulation(cond, w_mod[layer].astype(BF16), b_mod[layer][None, :])
    mod3 = mod.reshape(mod.shape[0], 1, mod.shape[1])
    mod_map = _mod_row_map(t_ctx // TOKEN_TILE, seq_lat // TOKEN_TILE)

    x_ctx = x_prompt.reshape(t_ctx, d)
    x_lat = x_sample.reshape(t_lat, d)
    zh, zf, za, zg = _input_projection(x_ctx, x_lat, mod3, w_in[layer].astype(BF16), lb, mod_map)

    norm_g = hg_norm_g[layer][None, :]
    oa_ctx, s_ctx = _hgrn_scan(zh, zf, norm_g, None, batch=n_ctx, seq=seq_ctx, row_block0=0)
    oa_lat, _ = _hgrn_scan(zh, zf, norm_g, state_hgrn[:, layer], batch=n_lat, seq=seq_lat,
                           row_block0=t_ctx // seq_lat)

    q_gain = jnp.tile(q_norm_g[layer], N_HEADS)[None, :]
    k_gain = jnp.tile(k_norm_g[layer], N_KV_HEADS)[None, :]
    lane = jnp.arange(ATT_WIDTH)
    group_ones = (lane[:, None] // HEAD_DIM == lane[None, :] // HEAD_DIM).astype(BF16)
    ob_ctx, k_ctx, v_ctx = _attention(za, q_gain, k_gain, group_ones, batch=n_ctx, seq=seq_ctx, row_block0=0)
    past = cache_k.shape[2]
    ob_lat = _attention(
        za, q_gain, k_gain, group_ones, batch=n_lat, seq=seq_lat, row_block0=t_ctx // seq_lat,
        rope=_rope_tables(seq_lat),
        cache=(cache_k[:, layer].reshape(n_lat, past, KV_WIDTH), cache_v[:, layer].reshape(n_lat, past, KV_WIDTH)))

    wr_t = w_router[layer].T
    wr_hi = wr_t.astype(BF16)
    wr_lo = (wr_t - wr_hi.astype(F32)).astype(BF16)
    x1, h2, h2_slabs, idx_t, w_t = _post_mixer(
        x_ctx, x_lat, mod3, oa_ctx, oa_lat, ob_ctx, ob_lat, zg, w_branch_a[layer].astype(BF16),
        w_branch_b[layer].astype(BF16), w_out[layer].astype(BF16), ln1_g[layer][None, :], ln1_b[layer][None, :],
        wr_hi, wr_lo, router_bias[layer][:, None], mod_map)

    routed = _routed_experts(h2_slabs, *_dispatch_lists(idx_t, w_t), w_e_gate[layer], w_e_up[layer], w_e_down[layer])

    y_ctx, y_lat = _final(x1, h2, routed, mod3, w_s_gate[layer].astype(BF16),
                          w_s_up[layer].astype(BF16), w_s_down[layer].astype(BF16),
                          ln2_g[layer][None, :], ln2_b[layer][None, :], mod_map, t_ctx)

    y_prompt = y_ctx.reshape(n_ctx, seq_ctx, d)
    y_sample = y_lat.reshape(n_lat, seq_lat, d)
    new_cache_k = k_ctx.reshape(n_ctx, 1, seq_ctx, N_KV_HEADS, HEAD_DIM)
    new_cache_v = v_ctx.reshape(n_ctx, 1, seq_ctx, N_KV_HEADS, HEAD_DIM)
    new_state = s_ctx[:, None]
    return (y_prompt, y_sample, new_cache_k, new_cache_v, new_state)
```

```python
import functools

import jax
import jax.numpy as jnp
from jax import lax
from jax.experimental import pallas as pl
from jax.experimental.pallas import tpu as pltpu

F32 = jnp.float32
BF16 = jnp.bfloat16

GRID_W = 64
HG_HEADS = 4
HG_DK = 128
HG_DV = 128
HG_WIDTH = HG_HEADS * HG_DK
N_HEADS = 8
N_KV_HEADS = 2
HEAD_DIM = 64
ATT_WIDTH = N_HEADS * HEAD_DIM
KV_WIDTH = N_KV_HEADS * HEAD_DIM
ROPE_THETA = 10000.0
N_EXPERTS = 256
TOP_K = 8
N_GROUPS = 8
TOPK_GROUPS = 4
GROUP_SIZE = N_EXPERTS // N_GROUPS
ROUTED_SCALE = 2.5
NORM_EPS = 1e-6
DEPTH = 1
DEEPNORM_ALPHA = (2 * DEPTH) ** 0.25

LANES = 128
SUBLANES = 8
VMEM_LIMIT = 56 * 1024 * 1024

TOKEN_TILE = 256
TILE_STRIDE = TOKEN_TILE + 1
HG_CHUNK = 32
ATT_Q_BLOCK = 128
MOE_BLOCK = 320
ROW_STRIDE = MOE_BLOCK + 1
LIST_LEN = 512
LIST_ROWS = LIST_LEN // LANES
ROW_GROUP = 8
ROW_SEGMENT = 32
WEIGHT_BUFFERS = 3


def _sigmoid(x):
    return 1.0 / (1.0 + jnp.exp(-x))


def _silu(x):
    return x * _sigmoid(x)


def _params(sem=None):
    return pltpu.CompilerParams(dimension_semantics=sem, vmem_limit_bytes=VMEM_LIMIT)


def _mod_kernel(c_ref, w_ref, b_ref, o_ref):
    s = _silu(c_ref[...]).astype(BF16)
    o_ref[...] = jnp.dot(s, w_ref[...], preferred_element_type=F32) + b_ref[...]


def _modulation(cond, w_mod, b_mod):
    n, d = cond.shape
    width = w_mod.shape[1]
    tn = width // 4
    return pl.pallas_call(
        _mod_kernel,
        grid=(4,),
        in_specs=[
            pl.BlockSpec((n, d), lambda j: (0, 0)),
            pl.BlockSpec((d, tn), lambda j: (0, j)),
            pl.BlockSpec((1, tn), lambda j: (0, j)),
        ],
        out_specs=pl.BlockSpec((n, tn), lambda j: (0, j)),
        out_shape=jax.ShapeDtypeStruct((n, width), F32),
        compiler_params=_params(("arbitrary",)),
    )(cond, w_mod, b_mod)


def _inproj_kernel(xc_ref, xl_ref, mod_ref, w_ref, lb_ref, zh_ref, zf_ref, za_ref, zg_ref, *, d, n_ctx_tiles):
    shift = mod_ref[0, :, 0:d]
    scale = mod_ref[0, :, d:2 * d]
    x = _pick(pl.program_id(0) < n_ctx_tiles, xc_ref, xl_ref)
    h = (x * (1.0 + scale) + shift).astype(BF16)

    def proj(lo, hi):
        return jnp.dot(h, w_ref[:, lo:hi], preferred_element_type=F32)

    w = HG_WIDTH
    zh_ref[:, 0:w] = _silu(proj(0, w)).astype(zh_ref.dtype)
    zh_ref[:, w:2 * w] = proj(w, 2 * w).astype(zh_ref.dtype)
    for i in range(2):
        lb = lb_ref[i:i + 1, :]
        zf_ref[:, i * w:(i + 1) * w] = lb + (1.0 - lb) * _sigmoid(proj((2 + i) * w, (3 + i) * w))
    zh_ref[:, 2 * w:3 * w] = _silu(proj(4 * w, 5 * w)).astype(zh_ref.dtype)
    a0 = 5 * w
    a1 = a0 + ATT_WIDTH + 2 * KV_WIDTH
    za_ref[...] = proj(a0, a1)
    for i in range(4):
        lo = a1 + i * (d // 2)
        zg_ref[:, i * (d // 2):(i + 1) * (d // 2)] = _sigmoid(proj(lo, lo + d // 2)).astype(zg_ref.dtype)


def _mod_row_map(n_ctx_tiles, tiles_per_latent):
    def index_map(i):
        row = jnp.where(i < n_ctx_tiles, 0, 1 + (i - n_ctx_tiles) // tiles_per_latent)
        return (row, 0, 0)
    return index_map


def _pair_specs(width, n_ctx_tiles):
    return [pl.BlockSpec((TOKEN_TILE, width), lambda i: (jnp.minimum(i, n_ctx_tiles - 1), 0)),
            pl.BlockSpec((TOKEN_TILE, width), lambda i: (jnp.maximum(i - n_ctx_tiles, 0), 0))]


def _pick(is_ctx, ctx_ref, lat_ref):
    return jnp.where(is_ctx, ctx_ref[...], lat_ref[...])


def _input_projection(x_ctx, x_lat, mod3, w_in, lb, mod_map):
    d = x_ctx.shape[1]
    n_ctx_tiles = x_ctx.shape[0] // TOKEN_TILE
    t = x_ctx.shape[0] + x_lat.shape[0]
    width = w_in.shape[1]
    zh_w = 3 * HG_WIDTH
    zf_w = 2 * HG_WIDTH
    za_w = ATT_WIDTH + 2 * KV_WIDTH
    zg_w = 2 * d
    assert width == zh_w + zf_w + za_w + zg_w
    tm = TOKEN_TILE
    return pl.pallas_call(
        functools.partial(_inproj_kernel, d=d, n_ctx_tiles=n_ctx_tiles),
        grid=(t // tm,),
        in_specs=_pair_specs(d, n_ctx_tiles) + [
            pl.BlockSpec((1, 1, mod3.shape[2]), mod_map),
            pl.BlockSpec((d, width), lambda i: (0, 0)),
            pl.BlockSpec((2, HG_WIDTH), lambda i: (0, 0)),
        ],
        out_specs=[
            pl.BlockSpec((tm, zh_w), lambda i: (i, 0)),
            pl.BlockSpec((tm, zf_w), lambda i: (i, 0)),
            pl.BlockSpec((tm, za_w), lambda i: (i, 0)),
            pl.BlockSpec((tm, zg_w), lambda i: (i, 0)),
        ],
        out_shape=[
            jax.ShapeDtypeStruct((t, zh_w), BF16),
            jax.ShapeDtypeStruct((t, zf_w), F32),
            jax.ShapeDtypeStruct((t, za_w), F32),
            jax.ShapeDtypeStruct((t, zg_w), BF16),
        ],
        compiler_params=_params(("arbitrary",)),
    )(x_ctx, x_lat, mod3, w_in, lb)


def _hgrn_kernel(*refs, seq, has_state):
    if has_state:
        q_ref, v_ref, ff_ref, fb_ref, gs_ref, ng_ref, s0_ref, o_ref, sout_ref = refs
    else:
        q_ref, v_ref, ff_ref, fb_ref, gs_ref, ng_ref, o_ref, sout_ref = refs
        s0_ref = None
    c = HG_CHUNK
    n = seq // c
    q3 = q_ref[...].astype(F32).reshape(n, c, HG_DK)
    v3 = v_ref[...].reshape(n, c, HG_DV)
    pos = lax.broadcasted_iota(jnp.int32, (seq, HG_DK), 0) % c
    t_idx = lax.broadcasted_iota(jnp.int32, (c, c), 0)
    s_idx = lax.broadcasted_iota(jnp.int32, (c, c), 1)
    o_sum = None
    for direction, f_ref in enumerate((ff_ref, fb_ref)):
        reverse = direction == 1
        f = f_ref[...]
        k3 = (1.0 - f).reshape(n, c, HG_DK)
        b = jnp.log(f)
        step = 1
        while step < c:
            if reverse:
                b = b + jnp.where(pos < c - step, pltpu.roll(b, seq - step, axis=0), 0.0)
            else:
                b = b + jnp.where(pos >= step, pltpu.roll(b, step, axis=0), 0.0)
            step *= 2
        b3 = b.reshape(n, c, HG_DK)
        edge = b3[:, 0:1, :] if reverse else b3[:, c - 1:c, :]
        mid = b3[:, c // 2:c // 2 + 1, :]
        q_rel = q3 * jnp.exp(b3 - mid)
        k_rel = k3 * jnp.exp(mid - b3)
        q_mid = q_rel.astype(BF16)
        k_mid = k_rel.astype(BF16)
        q_in = (q_rel * jnp.exp(mid)).astype(BF16)
        k_end = (k_rel * jnp.exp(edge - mid)).astype(BF16)
        scores = jnp.einsum('ntd,nsd->nts', q_mid, k_mid, preferred_element_type=F32)
        keep = (s_idx >= t_idx) if reverse else (s_idx <= t_idx)
        scores = jnp.where(keep[None], scores, 0.0).astype(BF16)
        o_intra = jnp.einsum('nts,nsv->ntv', scores, v3, preferred_element_type=F32)
        upd = jnp.einsum('nsv,nsd->nvd', v3, k_end, preferred_element_type=F32)
        dec = jnp.exp(edge)
        if has_state:
            st = s0_ref[0, direction, 0].T
        else:
            st = jnp.zeros((HG_DV, HG_DK), F32)
        before = [None] * n
        for ci in (range(n - 1, -1, -1) if reverse else range(n)):
            before[ci] = st.astype(BF16)
            st = st * dec[ci] + upd[ci]
        st_before = jnp.stack(before, axis=0)
        o_inter = jnp.einsum('ntd,nvd->ntv', q_in, st_before, preferred_element_type=F32)
        o_dir = (o_intra + o_inter).reshape(seq, HG_DV)
        o_sum = o_dir if o_sum is None else o_sum + o_dir
        sout_ref[0, direction, 0] = st.T
    ms = jnp.mean(o_sum * o_sum, axis=-1, keepdims=True)
    o = o_sum * lax.rsqrt(ms + NORM_EPS) * ng_ref[...]
    o_ref[...] = (o * gs_ref[...].astype(F32)).astype(o_ref.dtype)


def _hgrn_scan(zh, zf, norm_g, s0, *, batch, seq, row_block0):
    has_state = s0 is not None
    h = HG_HEADS

    def col(section):
        return pl.BlockSpec((seq, HG_DK), lambda b, j: (row_block0 + b, section * h + j))

    in_specs = [col(0), col(1), col(0), col(1), col(2), pl.BlockSpec((1, HG_DV), lambda b, j: (0, 0))]
    args = [zh, zh, zf, zf, zh, norm_g]
    state_spec = pl.BlockSpec((1, 2, 1, HG_DK, HG_DV), lambda b, j: (b, 0, j, 0, 0))
    if has_state:
        in_specs.append(state_spec)
        args.append(s0)
    return pl.pallas_call(
        functools.partial(_hgrn_kernel, seq=seq, has_state=has_state),
        grid=(batch, h),
        in_specs=in_specs,
        out_specs=[pl.BlockSpec((seq, HG_DV), lambda b, j: (b, j)), state_spec],
        out_shape=[
            jax.ShapeDtypeStruct((batch * seq, h * HG_DV), BF16),
            jax.ShapeDtypeStruct((batch, 2, h, HG_DK, HG_DV), F32),
        ],
        compiler_params=_params(("arbitrary", "arbitrary")),
    )(*args)


def _group_rms_norm(x, group_ones, gain):
    sq = x * x
    hi = sq.astype(BF16)
    lo = (sq - hi.astype(F32)).astype(BF16)
    total = (jnp.dot(hi, group_ones, preferred_element_type=F32)
             + jnp.dot(lo, group_ones, preferred_element_type=F32))
    return x * lax.rsqrt(total * (1.0 / HEAD_DIM) + NORM_EPS) * gain


def _rope(x, cos, sin_signed):
    width = x.shape[1]
    quarter = HEAD_DIM // 4
    lane = lax.broadcasted_iota(jnp.int32, x.shape, 1)
    partner = jnp.where(lane % (2 * quarter) < quarter,
                        pltpu.roll(x, width - quarter, axis=1),
                        pltpu.roll(x, quarter, axis=1))
    return x * cos + partner * sin_signed


def _attn_kernel(*refs, seq, latent):
    if latent:
        za_ref, qg_ref, kg_ref, gm_ref, cos_ref, sin_ref, ck_ref, cv_ref, o_ref = refs
    else:
        za_ref, qg_ref, kg_ref, gm_ref, o_ref, kout_ref, vout_ref = refs
    q = za_ref[:, 0:ATT_WIDTH]
    k = za_ref[:, ATT_WIDTH:ATT_WIDTH + KV_WIDTH]
    v = za_ref[:, ATT_WIDTH + KV_WIDTH:ATT_WIDTH + 2 * KV_WIDTH]
    qn = _group_rms_norm(q, gm_ref[...], qg_ref[...])
    kn = _group_rms_norm(k, gm_ref[0:KV_WIDTH, 0:KV_WIDTH], kg_ref[...])
    if latent:
        qn = _rope(qn, cos_ref[...], sin_ref[...])
        kr = _rope(kn, cos_ref[:, 0:KV_WIDTH], sin_ref[:, 0:KV_WIDTH])
    else:
        kout_ref[...] = kn
        vout_ref[...] = v
        kr = kn
    qb = (qn * (HEAD_DIM ** -0.5)).astype(BF16)
    kb = kr.astype(BF16)
    vb = v.astype(BF16)
    if latent:
        ckb = ck_ref[0].astype(BF16)
        cvb = cv_ref[0].astype(BF16)
    groups = N_HEADS // N_KV_HEADS
    tq = ATT_Q_BLOCK if latent else seq
    nt = (((1,), (1,)), ((), ()))
    for kh in range(N_KV_HEADS):
        ksl = slice(kh * HEAD_DIM, (kh + 1) * HEAD_DIM)
        k_new = kb[:, ksl]
        v_new = vb[:, ksl]
        for blk in range(seq // tq):
            rows = slice(blk * tq, (blk + 1) * tq)
            heads = [kh * groups + g for g in range(groups)]
            q_st = jnp.concatenate([qb[rows, hd * HEAD_DIM:(hd + 1) * HEAD_DIM] for hd in heads], axis=0)
            s_new = lax.dot_general(q_st, k_new, nt, preferred_element_type=F32)
            m = jnp.max(s_new, axis=-1, keepdims=True)
            if latent:
                s_ctx = lax.dot_general(q_st, ckb[:, ksl], nt, preferred_element_type=F32)
                m = jnp.maximum(m, jnp.max(s_ctx, axis=-1, keepdims=True))
            p_new = jnp.exp(s_new - m)
            denom = jnp.sum(p_new, axis=-1, keepdims=True)
            acc = jnp.dot(p_new.astype(BF16), v_new, preferred_element_type=F32)
            if latent:
                p_ctx = jnp.exp(s_ctx - m)
                denom = denom + jnp.sum(p_ctx, axis=-1, keepdims=True)
                acc = acc + jnp.dot(p_ctx.astype(BF16), cvb[:, ksl], preferred_element_type=F32)
            out = acc / denom
            for g in range(0, groups, 2):
                pair = jnp.concatenate([out[g * tq:(g + 1) * tq], out[(g + 1) * tq:(g + 2) * tq]], axis=1)
                lane0 = heads[g] * HEAD_DIM
                o_ref[rows, lane0:lane0 + 2 * HEAD_DIM] = pair.astype(o_ref.dtype)


def _attention(za, q_gain, k_gain, group_ones, *, batch, seq, row_block0, rope=None, cache=None):
    latent = cache is not None
    za_w = za.shape[1]
    in_specs = [
        pl.BlockSpec((seq, za_w), lambda b: (row_block0 + b, 0)),
        pl.BlockSpec((1, ATT_WIDTH), lambda b: (0, 0)),
        pl.BlockSpec((1, KV_WIDTH), lambda b: (0, 0)),
        pl.BlockSpec((ATT_WIDTH, ATT_WIDTH), lambda b: (0, 0)),
    ]
    args = [za, q_gain, k_gain, group_ones]
    o_spec = pl.BlockSpec((seq, ATT_WIDTH), lambda b: (b, 0))
    o_shape = jax.ShapeDtypeStruct((batch * seq, ATT_WIDTH), BF16)
    if latent:
        cos, sin_signed = rope
        ck, cv = cache
        past = ck.shape[1]
        in_specs += [
            pl.BlockSpec((seq, ATT_WIDTH), lambda b: (0, 0)),
            pl.BlockSpec((seq, ATT_WIDTH), lambda b: (0, 0)),
            pl.BlockSpec((1, past, KV_WIDTH), lambda b: (b, 0, 0)),
            pl.BlockSpec((1, past, KV_WIDTH), lambda b: (b, 0, 0)),
        ]
        args += [cos, sin_signed, ck, cv]
        out_specs = o_spec
        out_shape = o_shape
    else:
        kv_spec = pl.BlockSpec((seq, KV_WIDTH), lambda b: (b, 0))
        kv_shape = jax.ShapeDtypeStruct((batch * seq, KV_WIDTH), F32)
        out_specs = [o_spec, kv_spec, kv_spec]
        out_shape = [o_shape, kv_shape, kv_shape]
    return pl.pallas_call(
        functools.partial(_attn_kernel, seq=seq, latent=latent),
        grid=(batch,),
        in_specs=in_specs,
        out_specs=out_specs,
        out_shape=out_shape,
        compiler_params=_params(("arbitrary",)),
    )(*args)


def _rope_tables(seq):
    quarter = HEAD_DIM // 4
    t = jnp.arange(seq)
    row = (t // GRID_W).astype(F32)
    colp = (t % GRID_W).astype(F32)
    inv_freq = ROPE_THETA ** (-jnp.arange(quarter, dtype=F32) / quarter)
    lane = jnp.arange(HEAD_DIM)
    pos = jnp.where((lane < HEAD_DIM // 2)[None, :], row[:, None], colp[:, None])
    ang = pos * inv_freq[lane % quarter][None, :]
    sign = jnp.where(lane % (2 * quarter) < quarter, -1.0, 1.0)[None, :]
    cos = jnp.tile(jnp.cos(ang), (1, N_HEADS))
    sin_signed = jnp.tile(jnp.sin(ang) * sign, (1, N_HEADS))
    return cos, sin_signed


def _layer_norm(y, g, b):
    mu = jnp.mean(y, axis=-1, keepdims=True)
    yc = y - mu
    var = jnp.mean(yc * yc, axis=-1, keepdims=True)
    return yc * lax.rsqrt(var + NORM_EPS) * g + b


def _rows_to_slabs(tile, st_ref, slab_ref):
    rows, d = tile.shape
    chunks = d // LANES
    for j in range(chunks):
        st_ref[j * TILE_STRIDE:j * TILE_STRIDE + rows, :] = tile[:, j * LANES:(j + 1) * LANES]
    for r in range(rows):
        slab_ref[r * chunks:(r + 1) * chunks, :] = st_ref[pl.ds(r, chunks, stride=TILE_STRIDE), :]


def _slabs_to_rows(slab_ref, st_ref, rows):
    chunks = SUBLANES
    for r in range(rows):
        st_ref[pl.ds(r, chunks, stride=TILE_STRIDE), :] = slab_ref[r * chunks:(r + 1) * chunks, :]
    return jnp.concatenate([st_ref[j * TILE_STRIDE:j * TILE_STRIDE + rows, :] for j in range(chunks)], axis=1)


def _post_mixer_kernel(xc_ref, xl_ref, mod_ref, oac_ref, oal_ref, obc_ref, obl_ref, zg_ref, wa_ref, wb_ref,
                       wo_ref, g_ref, b_ref, wrh_ref, wrl_ref, rb_ref,
                       x1_ref, h2_ref, slab_ref, idx_ref, rw_ref, st_ref, *, d, n_ctx_tiles):
    is_ctx = pl.program_id(0) < n_ctx_tiles
    gate1 = mod_ref[0, :, 2 * d:3 * d]
    shift2 = mod_ref[0, :, 3 * d:4 * d]
    scale2 = mod_ref[0, :, 4 * d:5 * d]
    branch_a = jnp.dot(_pick(is_ctx, oac_ref, oal_ref), wa_ref[...], preferred_element_type=F32)
    branch_b = jnp.dot(_pick(is_ctx, obc_ref, obl_ref), wb_ref[...], preferred_element_type=F32)
    merged = zg_ref[:, 0:d].astype(F32) * branch_a + zg_ref[:, d:2 * d].astype(F32) * branch_b
    mix = jnp.dot(merged.astype(BF16), wo_ref[...], preferred_element_type=F32)
    x = _pick(is_ctx, xc_ref, xl_ref)
    x1 = _layer_norm(DEEPNORM_ALPHA * x + gate1 * mix, g_ref[...], b_ref[...])
    x1_ref[...] = x1
    h2 = x1 * (1.0 + scale2) + shift2
    h2_ref[...] = h2.astype(h2_ref.dtype)
    _rows_to_slabs(h2, st_ref, slab_ref)
    idx_ref[...], rw_ref[...] = _route_tile(h2, wrh_ref[...], wrl_ref[...], rb_ref[...])


def _post_mixer(x_ctx, x_lat, mod3, oa_ctx, oa_lat, ob_ctx, ob_lat, zg, w_a, w_b, w_o, ln_g, ln_b,
                wr_hi, wr_lo, router_bias, mod_map):
    d = x_ctx.shape[1]
    n_ctx_tiles = x_ctx.shape[0] // TOKEN_TILE
    t = x_ctx.shape[0] + x_lat.shape[0]
    tm = TOKEN_TILE
    row = lambda i: (i, 0)
    full = lambda i: (0, 0)
    return pl.pallas_call(
        functools.partial(_post_mixer_kernel, d=d, n_ctx_tiles=n_ctx_tiles),
        grid=(t // tm,),
        in_specs=_pair_specs(d, n_ctx_tiles) + [
            pl.BlockSpec((1, 1, mod3.shape[2]), mod_map),
        ] + _pair_specs(oa_ctx.shape[1], n_ctx_tiles) + _pair_specs(ob_ctx.shape[1], n_ctx_tiles) + [
            pl.BlockSpec((tm, 2 * d), row),
            pl.BlockSpec(w_a.shape, full),
            pl.BlockSpec(w_b.shape, full),
            pl.BlockSpec(w_o.shape, full),
            pl.BlockSpec((1, d), full),
            pl.BlockSpec((1, d), full),
            pl.BlockSpec((N_EXPERTS, d), full),
            pl.BlockSpec((N_EXPERTS, d), full),
            pl.BlockSpec((N_EXPERTS, 1), full),
        ],
        out_specs=[pl.BlockSpec((tm, d), row), pl.BlockSpec((tm, d), row),
                   pl.BlockSpec((tm * SUBLANES, LANES), row),
                   pl.BlockSpec((TOP_K, tm), lambda i: (0, i)), pl.BlockSpec((TOP_K, tm), lambda i: (0, i))],
        out_shape=[jax.ShapeDtypeStruct((t, d), F32), jax.ShapeDtypeStruct((t, d), BF16),
                   jax.ShapeDtypeStruct((t * SUBLANES, LANES), F32),
                   jax.ShapeDtypeStruct((TOP_K, t), jnp.int32), jax.ShapeDtypeStruct((TOP_K, t), F32)],
        scratch_shapes=[pltpu.VMEM((SUBLANES * TILE_STRIDE, LANES), F32)],
        compiler_params=_params(("arbitrary",)),
    )(x_ctx, x_lat, mod3, oa_ctx, oa_lat, ob_ctx, ob_lat, zg, w_a, w_b, w_o, ln_g, ln_b, wr_hi, wr_lo, router_bias)


def _first_index_of_max(x, idx, sentinel):
    m = jnp.max(x, axis=0, keepdims=True)
    first = jnp.min(jnp.where(x == m, idx, sentinel), axis=0, keepdims=True)
    return m, first


def _route_tile(h, wh, wl, bias):
    hh = h.astype(BF16)
    hl = (h - hh.astype(F32)).astype(BF16)
    nt = (((1,), (1,)), ((), ()))
    logits = (lax.dot_general(wh, hh, nt, preferred_element_type=F32)
              + lax.dot_general(wh, hl, nt, preferred_element_type=F32)
              + lax.dot_general(wl, hh, nt, preferred_element_type=F32))
    scores = _sigmoid(logits)
    sel = scores + bias
    tm = sel.shape[1]
    neg = -jnp.inf
    gidx = lax.broadcasted_iota(jnp.int32, (GROUP_SIZE, tm), 0)
    group_scores = []
    for g in range(N_GROUPS):
        sg = sel[g * GROUP_SIZE:(g + 1) * GROUP_SIZE, :]
        m1, first = _first_index_of_max(sg, gidx, GROUP_SIZE)
        m2 = jnp.max(jnp.where(gidx == first, neg, sg), axis=0, keepdims=True)
        group_scores.append(m1 + m2)
    gs = jnp.concatenate(group_scores, axis=0)
    nidx = lax.broadcasted_iota(jnp.int32, (N_GROUPS, tm), 0)
    chosen = jnp.zeros((N_GROUPS, tm), jnp.bool_)
    for _ in range(TOPK_GROUPS):
        _, first = _first_index_of_max(gs, nidx, N_GROUPS)
        hit = nidx == first
        chosen = jnp.logical_or(chosen, hit)
        gs = jnp.where(hit, neg, gs)
    masked = jnp.concatenate(
        [jnp.where(chosen[g:g + 1, :], sel[g * GROUP_SIZE:(g + 1) * GROUP_SIZE, :], neg) for g in range(N_GROUPS)],
        axis=0)
    eidx = lax.broadcasted_iota(jnp.int32, (N_EXPERTS, tm), 0)
    picks, weights = [], []
    for _ in range(TOP_K):
        _, first = _first_index_of_max(masked, eidx, N_EXPERTS)
        hit = eidx == first
        picks.append(first)
        weights.append(jnp.sum(jnp.where(hit, scores, 0.0), axis=0, keepdims=True))
        masked = jnp.where(hit, neg, masked)
    wk = jnp.concatenate(weights, axis=0)
    return jnp.concatenate(picks, axis=0), wk / jnp.sum(wk, axis=0, keepdims=True) * ROUTED_SCALE


def _dispatch_lists(idx_t, w_t):
    k, t = idx_t.shape
    s = k * t
    n_blocks = s // MOE_BLOCK + N_EXPERTS
    flat_tok = jnp.arange(s, dtype=jnp.int32) % t
    sorted_e, sorted_tok, sorted_w = lax.sort((idx_t.reshape(s), flat_tok, w_t.reshape(s)), num_keys=1)
    experts = jnp.arange(N_EXPERTS + 1, dtype=jnp.int32)
    start = jnp.sum(sorted_e[None, :] < experts[:, None], axis=1, dtype=jnp.int32)
    counts = start[1:] - start[:-1]
    blocks_per_e = (counts + MOE_BLOCK - 1) // MOE_BLOCK
    block_end = jnp.cumsum(blocks_per_e)
    n_used = block_end[-1]
    blk = jnp.arange(n_blocks, dtype=jnp.int32)
    block_e = jnp.sum(block_end[None, :] <= blk[:, None], axis=1, dtype=jnp.int32)
    block_e = jnp.minimum(block_e, N_EXPERTS - 1)
    onehot = (block_e[:, None] == experts[None, :N_EXPERTS]).astype(jnp.int32)
    pick = lambda a: jnp.sum(onehot * a[None, :], axis=1)
    offset = (blk - (pick(block_end) - pick(blocks_per_e))) * MOE_BLOCK
    row_start = pick(start[:-1]) + offset
    n_rows = jnp.clip(pick(counts) - offset, 0, MOE_BLOCK)
    used = blk < n_used
    last_e = jnp.sum(jnp.where(blk == n_used - 1, block_e, 0))
    block_e = jnp.where(used, block_e, last_e)
    row_start = jnp.where(used, row_start, 0)
    n_rows = jnp.where(used, n_rows, 0)
    return block_e, row_start, n_rows, (n_used - 1).reshape(1), sorted_tok, sorted_w


def _block_lists_kernel(rs_ref, nr_ref, tok_ref, w_ref, off_ref, wout_ref, *, t):
    b = pl.program_id(0)
    start = rs_ref[b]
    n_rows = nr_ref[b]
    q = start // LANES
    lane0 = start % LANES
    q8 = pl.multiple_of((q // SUBLANES) * SUBLANES, SUBLANES)
    lane = lax.broadcasted_iota(jnp.int32, (LIST_ROWS, LANES), 1)
    row = lax.broadcasted_iota(jnp.int32, (LIST_ROWS, LANES), 0)

    def window(ref):
        a = ref[pl.ds(q8, 2 * SUBLANES), :]
        a = pltpu.roll(a, 2 * SUBLANES - (q - q8), axis=0)
        lo = pltpu.roll(a[0:LIST_ROWS], LANES - lane0, axis=1)
        hi = pltpu.roll(a[1:LIST_ROWS + 1], LANES - lane0, axis=1)
        return jnp.where(lane < LANES - lane0, lo, hi)

    valid = row * LANES + lane < n_rows
    off_ref[0] = jnp.where(valid, window(tok_ref) * SUBLANES, t * SUBLANES)
    wout_ref[0] = jnp.where(valid, window(w_ref), 0.0)


def _block_lists(row_start, n_rows, sorted_tok, sorted_w, t):
    n_blocks = row_start.shape[0]
    n_slots = sorted_tok.shape[0]
    list_rows = n_slots // LANES + 2 * SUBLANES
    pad = list_rows * LANES - n_slots
    tok2d = jnp.pad(sorted_tok, (0, pad)).reshape(list_rows, LANES)
    w2d = jnp.pad(sorted_w, (0, pad)).reshape(list_rows, LANES)
    full = lambda b, rs, nr: (0, 0)
    blk = pl.BlockSpec((1, LIST_ROWS, LANES), lambda b, rs, nr: (b, 0, 0))
    off, wts = pl.pallas_call(
        functools.partial(_block_lists_kernel, t=t),
        grid_spec=pltpu.PrefetchScalarGridSpec(
            num_scalar_prefetch=2,
            grid=(n_blocks,),
            in_specs=[pl.BlockSpec((list_rows, LANES), full), pl.BlockSpec((list_rows, LANES), full)],
            out_specs=[blk, blk],
        ),
        out_shape=[jax.ShapeDtypeStruct((n_blocks, LIST_ROWS, LANES), jnp.int32),
                   jax.ShapeDtypeStruct((n_blocks, LIST_ROWS, LANES), F32)],
        compiler_params=_params(("arbitrary",)),
    )(row_start, n_rows, tok2d, w2d)
    return off.reshape(n_blocks * LIST_LEN), wts.reshape(n_blocks * LIST_LEN)


def _expert_kernel(be_ref, nr_ref, last_ref, x_hbm, wg_hbm, wu_hbm, wd_hbm, off_ref, out_ref,
                   xs, xt, wgb, wub, wdb, sem, wsem):
    b = pl.program_id(0)
    chunks = wg_hbm.shape[1] // LANES
    n_rows = nr_ref[b]
    last = last_ref[0]
    ahead = WEIGHT_BUFFERS - 1

    def weight_copies(blk):
        e = be_ref[blk]
        slot = blk % WEIGHT_BUFFERS
        return [pltpu.make_async_copy(src.at[e], dst.at[slot], wsem.at[slot])
                for src, dst in ((wg_hbm, wgb), (wu_hbm, wub), (wd_hbm, wdb))]

    @pl.when(b == 0)
    def _load():
        for blk in range(ahead):
            @pl.when(blk <= last)
            def _(blk=blk):
                for cp in weight_copies(blk):
                    cp.start()
        rows = x_hbm.shape[0]
        cp = pltpu.make_async_copy(x_hbm, xs.at[pl.ds(0, rows)], sem)
        cp.start()
        cp.wait()
        xs[pl.ds(rows, SUBLANES), :] = jnp.zeros((SUBLANES, LANES), F32)
        xt[...] = jnp.zeros(xt.shape, F32)

    @pl.when(b + ahead <= last)
    def _prefetch():
        for cp in weight_copies(b + ahead):
            cp.start()

    @pl.when(b <= last)
    def _block():
        slot = b % WEIGHT_BUFFERS
        for cp in weight_copies(b):
            cp.wait()
        for seg in range(MOE_BLOCK // ROW_SEGMENT):
            @pl.when(n_rows > seg * ROW_SEGMENT)
            def _gather(seg=seg):
                for r in range(seg * ROW_SEGMENT, (seg + 1) * ROW_SEGMENT):
                    src = pl.multiple_of(off_ref[r], SUBLANES)
                    xt[pl.ds(r, chunks, stride=ROW_STRIDE), :] = xs[pl.ds(src, SUBLANES), :]

        x = jnp.concatenate(
            [xt[j * ROW_STRIDE:j * ROW_STRIDE + MOE_BLOCK, :] for j in range(chunks)], axis=1).astype(BF16)
        gate = jnp.dot(x, wgb[slot].astype(BF16), preferred_element_type=F32)
        up = jnp.dot(x, wub[slot].astype(BF16), preferred_element_type=F32)
        hidden = (_silu(gate) * up).astype(BF16)
        out_ref[...] = jnp.dot(hidden, wdb[slot].astype(BF16), preferred_element_type=F32).astype(out_ref.dtype)


def _combine_kernel(nr_ref, last_ref, o_ref, off_ref, w_ref, y_hbm, ys, ot, sem, *, t, n_blocks):
    b = pl.program_id(0)
    rows = t * SUBLANES
    chunks = o_ref.shape[1] // LANES
    n_rows = nr_ref[b]
    group = ROW_GROUP

    @pl.when(b == 0)
    def _init():
        ys[...] = jnp.zeros(ys.shape, F32)

    @pl.when(n_rows > 0)
    def _block():
        for j in range(chunks):
            ot[j * ROW_STRIDE:j * ROW_STRIDE + MOE_BLOCK, :] = o_ref[:, j * LANES:(j + 1) * LANES].astype(F32)
        for seg in range(MOE_BLOCK // ROW_SEGMENT):
            @pl.when(n_rows > seg * ROW_SEGMENT)
            def _scatter(seg=seg):
                for r0 in range(seg * ROW_SEGMENT, (seg + 1) * ROW_SEGMENT, group):
                    dst = [pl.multiple_of(off_ref[r0 + i], SUBLANES) for i in range(group)]
                    vals = [ys[pl.ds(dst[i], SUBLANES), :]
                            + w_ref[r0 + i] * ot[pl.ds(r0 + i, chunks, stride=ROW_STRIDE), :]
                            for i in range(group)]
                    for i in range(group):
                        ys[pl.ds(dst[i], SUBLANES), :] = vals[i]

    @pl.when(b == n_blocks - 1)
    def _store():
        cp = pltpu.make_async_copy(ys.at[pl.ds(0, rows)], y_hbm, sem)
        cp.start()
        cp.wait()


def _routed_experts(h2_rows, block_e, row_start, n_rows, last_block, sorted_tok, sorted_w, w_gate, w_up, w_down):
    rows, lanes = h2_rows.shape
    t = rows // SUBLANES
    n_blocks = block_e.shape[0]
    d, e_dim = w_gate.shape[1], w_gate.shape[2]
    assert d == SUBLANES * LANES and lanes == LANES
    offsets, weights = _block_lists(row_start, n_rows, sorted_tok, sorted_w, t)
    staging = pltpu.VMEM((SUBLANES * ROW_STRIDE, LANES), F32)
    resident = pltpu.VMEM((rows + SUBLANES, LANES), F32)
    expert_out = pl.pallas_call(
        _expert_kernel,
        grid_spec=pltpu.PrefetchScalarGridSpec(
            num_scalar_prefetch=3,
            grid=(n_blocks,),
            in_specs=[
                pl.BlockSpec(memory_space=pl.ANY),
                pl.BlockSpec(memory_space=pl.ANY),
                pl.BlockSpec(memory_space=pl.ANY),
                pl.BlockSpec(memory_space=pl.ANY),
                pl.BlockSpec((LIST_LEN,), lambda b, be, nr, last: (jnp.minimum(b, last[0]),),
                             memory_space=pltpu.SMEM),
            ],
            out_specs=pl.BlockSpec((MOE_BLOCK, d), lambda b, be, nr, last: (jnp.minimum(b, last[0]), 0)),
            scratch_shapes=[
                resident,
                staging,
                pltpu.VMEM((WEIGHT_BUFFERS, d, e_dim), F32),
                pltpu.VMEM((WEIGHT_BUFFERS, d, e_dim), F32),
                pltpu.VMEM((WEIGHT_BUFFERS, e_dim, d), F32),
                pltpu.SemaphoreType.DMA(()),
                pltpu.SemaphoreType.DMA((WEIGHT_BUFFERS,)),
            ],
        ),
        out_shape=jax.ShapeDtypeStruct((n_blocks * MOE_BLOCK, d), BF16),
        compiler_params=_params(("arbitrary",)),
    )(block_e, n_rows, last_block, h2_rows, w_gate, w_up, w_down, offsets)
    list_spec = pl.BlockSpec((LIST_LEN,), lambda b, nr, last: (jnp.minimum(b, last[0]),), memory_space=pltpu.SMEM)
    return pl.pallas_call(
        functools.partial(_combine_kernel, t=t, n_blocks=n_blocks),
        grid_spec=pltpu.PrefetchScalarGridSpec(
            num_scalar_prefetch=2,
            grid=(n_blocks,),
            in_specs=[pl.BlockSpec((MOE_BLOCK, d), lambda b, nr, last: (jnp.minimum(b, last[0]), 0)),
                      list_spec, list_spec],
            out_specs=pl.BlockSpec(memory_space=pl.ANY),
            scratch_shapes=[resident, staging, pltpu.SemaphoreType.DMA(())],
        ),
        out_shape=jax.ShapeDtypeStruct((rows, LANES), F32),
        compiler_params=_params(("arbitrary",)),
    )(n_rows, last_block, expert_out, offsets, weights)


def _final_kernel(x1_ref, h2_ref, routed_ref, mod_ref, wg_ref, wu_ref, wd_ref, g_ref, b_ref,
                  yc_ref, yl_ref, st_ref, *, d, n_ctx_tiles):
    is_ctx = pl.program_id(0) < n_ctx_tiles
    gate2 = mod_ref[0, :, 5 * d:6 * d]
    h = h2_ref[...].astype(BF16)
    gate = jnp.dot(h, wg_ref[...], preferred_element_type=F32)
    up = jnp.dot(h, wu_ref[...], preferred_element_type=F32)
    shared = jnp.dot((_silu(gate) * up).astype(BF16), wd_ref[...], preferred_element_type=F32)
    ffn = _slabs_to_rows(routed_ref, st_ref, h.shape[0]) + shared
    y = _layer_norm(DEEPNORM_ALPHA * x1_ref[...] + gate2 * ffn, g_ref[...], b_ref[...])

    @pl.when(is_ctx)
    def _():
        yc_ref[...] = y

    @pl.when(jnp.logical_not(is_ctx))
    def _():
        yl_ref[...] = y


def _final(x1, h2, routed, mod3, w_g, w_u, w_d, ln_g, ln_b, mod_map, t_ctx):
    t, d = x1.shape
    t_lat = t - t_ctx
    n_ctx_tiles = t_ctx // TOKEN_TILE
    tm = TOKEN_TILE
    row = lambda i: (i, 0)
    full = lambda i: (0, 0)
    return pl.pallas_call(
        functools.partial(_final_kernel, d=d, n_ctx_tiles=n_ctx_tiles),
        grid=(t // tm,),
        in_specs=[
            pl.BlockSpec((tm, d), row),
            pl.BlockSpec((tm, d), row),
            pl.BlockSpec((tm * SUBLANES, LANES), row),
            pl.BlockSpec((1, 1, mod3.shape[2]), mod_map),
            pl.BlockSpec(w_g.shape, full),
            pl.BlockSpec(w_u.shape, full),
            pl.BlockSpec(w_d.shape, full),
            pl.BlockSpec((1, d), full),
            pl.BlockSpec((1, d), full),
        ],
        out_specs=_pair_specs(d, n_ctx_tiles),
        out_shape=[jax.ShapeDtypeStruct((t_ctx, d), F32), jax.ShapeDtypeStruct((t_lat, d), F32)],
        scratch_shapes=[pltpu.VMEM((SUBLANES * TILE_STRIDE, LANES), F32)],
        compiler_params=_params(("arbitrary",)),
    )(x1, h2, routed, mod3, w_g, w_u, w_d, ln_g, ln_b)


def kernel(x_prompt, x_sample, cache_k, cache_v, state_hgrn, c, c_ctx, w_mod, b_mod, w_in, hg_lb, hg_norm_g, q_norm_g, k_norm_g, w_branch_a, w_branch_b, w_out, ln1_g, ln1_b, w_router, router_bias, w_e_gate, w_e_up, w_e_down, w_s_gate, w_s_up, w_s_down, ln2_g, ln2_b):
    assert w_mod.shape[0] == DEPTH
    n_ctx, seq_ctx, d = x_prompt.shape
    n_lat, seq_lat, _ = x_sample.shape
    t_ctx = n_ctx * seq_ctx
    t_lat = n_lat * seq_lat
    assert seq_ctx == TOKEN_TILE and seq_lat % TOKEN_TILE == 0
    layer = 0

    lb = jnp.cumsum(jax.nn.softmax(hg_lb.astype(F32), axis=0), axis=0)[layer]

    cond = jnp.concatenate([c_ctx[None, :], c], axis=0)
    cond = jnp.pad(cond, ((0, (-cond.shape[0]) % SUBLANES), (0, 0)))
    mod = _modulation(cond, w_mod[layer].astype(BF16), b_mod[layer][None, :])
    mod3 = mod.reshape(mod.shape[0], 1, mod.shape[1])
    mod_map = _mod_row_map(t_ctx // TOKEN_TILE, seq_lat // TOKEN_TILE)

    x_ctx = x_prompt.reshape(t_ctx, d)
    x_lat = x_sample.reshape(t_lat, d)
    zh, zf, za, zg = _input_projection(x_ctx, x_lat, mod3, w_in[layer].astype(BF16), lb, mod_map)

    norm_g = hg_norm_g[layer][None, :]
    oa_ctx, s_ctx = _hgrn_scan(zh, zf, norm_g, None, batch=n_ctx, seq=seq_ctx, row_block0=0)
    oa_lat, _ = _hgrn_scan(zh, zf, norm_g, state_hgrn[:, layer], batch=n_lat, seq=seq_lat,
                           row_block0=t_ctx // seq_lat)

    q_gain = jnp.tile(q_norm_g[layer], N_HEADS)[None, :]
    k_gain = jnp.tile(k_norm_g[layer], N_KV_HEADS)[None, :]
    lane = jnp.arange(ATT_WIDTH)
    group_ones = (lane[:, None] // HEAD_DIM == lane[None, :] // HEAD_DIM).astype(BF16)
    ob_ctx, k_ctx, v_ctx = _attention(za, q_gain, k_gain, group_ones, batch=n_ctx, seq=seq_ctx, row_block0=0)
    past = cache_k.shape[2]
    ob_lat = _attention(
        za, q_gain, k_gain, group_ones, batch=n_lat, seq=seq_lat, row_block0=t_ctx // seq_lat,
        rope=_rope_tables(seq_lat),
        cache=(cache_k[:, layer].reshape(n_lat, past, KV_WIDTH), cache_v[:, layer].reshape(n_lat, past, KV_WIDTH)))

    wr_t = w_router[layer].T
    wr_hi = wr_t.astype(BF16)
    wr_lo = (wr_t - wr_hi.astype(F32)).astype(BF16)
    x1, h2, h2_slabs, idx_t, w_t = _post_mixer(
        x_ctx, x_lat, mod3, oa_ctx, oa_lat, ob_ctx, ob_lat, zg, w_branch_a[layer].astype(BF16),
        w_branch_b[layer].astype(BF16), w_out[layer].astype(BF16), ln1_g[layer][None, :], ln1_b[layer][None, :],
        wr_hi, wr_lo, router_bias[layer][:, None], mod_map)

    routed = _routed_experts(h2_slabs, *_dispatch_lists(idx_t, w_t), w_e_gate[layer], w_e_up[layer], w_e_down[layer])

    y_ctx, y_lat = _final(x1, h2, routed, mod3, w_s_gate[layer].astype(BF16),
                          w_s_up[layer].astype(BF16), w_s_down[layer].astype(BF16),
                          ln2_g[layer][None, :], ln2_b[layer][None, :], mod_map, t_ctx)

    y_prompt = y_ctx.reshape(n_ctx, seq_ctx, d)
    y_sample = y_lat.reshape(n_lat, seq_lat, d)
    new_cache_k = k_ctx.reshape(n_ctx, 1, seq_ctx, N_KV_HEADS, HEAD_DIM)
    new_cache_v = v_ctx.reshape(n_ctx, 1, seq_ctx, N_KV_HEADS, HEAD_DIM)
    new_state = s_ctx[:, None]
    return (y_prompt, y_sample, new_cache_k, new_cache_v, new_state)
```

```python
import functools

import jax
import jax.numpy as jnp
from jax import lax
from jax.experimental import pallas as pl
from jax.experimental.pallas import tpu as pltpu

F32 = jnp.float32
BF16 = jnp.bfloat16

GRID_W = 64
HG_HEADS = 4
HG_DK = 128
HG_DV = 128
HG_WIDTH = HG_HEADS * HG_DK
N_HEADS = 8
N_KV_HEADS = 2
HEAD_DIM = 64
ATT_WIDTH = N_HEADS * HEAD_DIM
KV_WIDTH = N_KV_HEADS * HEAD_DIM
ROPE_THETA = 10000.0
N_EXPERTS = 256
TOP_K = 8
N_GROUPS = 8
TOPK_GROUPS = 4
GROUP_SIZE = N_EXPERTS // N_GROUPS
ROUTED_SCALE = 2.5
NORM_EPS = 1e-6
DEPTH = 1
DEEPNORM_ALPHA = (2 * DEPTH) ** 0.25

LANES = 128
SUBLANES = 8
VMEM_LIMIT = 56 * 1024 * 1024

TOKEN_TILE = 256
TILE_STRIDE = TOKEN_TILE + 1
HG_CHUNK = 32
ATT_Q_BLOCK = 128
MOE_BLOCK = 320
ROW_STRIDE = MOE_BLOCK + 1
LIST_LEN = 512
LIST_ROWS = LIST_LEN // LANES
ROW_GROUP = 8
ROW_SEGMENT = 32
WEIGHT_BUFFERS = 3


def _sigmoid(x):
    return 1.0 / (1.0 + jnp.exp(-x))


def _silu(x):
    return x * _sigmoid(x)


def _params(sem=None):
    return pltpu.CompilerParams(dimension_semantics=sem, vmem_limit_bytes=VMEM_LIMIT)


def _mod_kernel(c_ref, w_ref, b_ref, o_ref):
    s = _silu(c_ref[...]).astype(BF16)
    o_ref[...] = jnp.dot(s, w_ref[...], preferred_element_type=F32) + b_ref[...]


def _modulation(cond, w_mod, b_mod):
    n, d = cond.shape
    width = w_mod.shape[1]
    tn = width // 4
    return pl.pallas_call(
        _mod_kernel,
        grid=(4,),
        in_specs=[
            pl.BlockSpec((n, d), lambda j: (0, 0)),
            pl.BlockSpec((d, tn), lambda j: (0, j)),
            pl.BlockSpec((1, tn), lambda j: (0, j)),
        ],
        out_specs=pl.BlockSpec((n, tn), lambda j: (0, j)),
        out_shape=jax.ShapeDtypeStruct((n, width), F32),
        compiler_params=_params(("arbitrary",)),
    )(cond, w_mod, b_mod)


def _inproj_kernel(xc_ref, xl_ref, mod_ref, w_ref, lb_ref, zh_ref, zf_ref, za_ref, zg_ref, *, d, n_ctx_tiles):
    shift = mod_ref[0, :, 0:d]
    scale = mod_ref[0, :, d:2 * d]
    x = _pick(pl.program_id(0) < n_ctx_tiles, xc_ref, xl_ref)
    h = (x * (1.0 + scale) + shift).astype(BF16)

    def proj(lo, hi):
        return jnp.dot(h, w_ref[:, lo:hi], preferred_element_type=F32)

    w = HG_WIDTH
    zh_ref[:, 0:w] = _silu(proj(0, w)).astype(zh_ref.dtype)
    zh_ref[:, w:2 * w] = proj(w, 2 * w).astype(zh_ref.dtype)
    for i in range(2):
        lb = lb_ref[i:i + 1, :]
        zf_ref[:, i * w:(i + 1) * w] = lb + (1.0 - lb) * _sigmoid(proj((2 + i) * w, (3 + i) * w))
    zh_ref[:, 2 * w:3 * w] = _silu(proj(4 * w, 5 * w)).astype(zh_ref.dtype)
    a0 = 5 * w
    a1 = a0 + ATT_WIDTH + 2 * KV_WIDTH
    za_ref[...] = proj(a0, a1)
    for i in range(4):
        lo = a1 + i * (d // 2)
        zg_ref[:, i * (d // 2):(i + 1) * (d // 2)] = _sigmoid(proj(lo, lo + d // 2)).astype(zg_ref.dtype)


def _mod_row_map(n_ctx_tiles, tiles_per_latent):
    def index_map(i):
        row = jnp.where(i < n_ctx_tiles, 0, 1 + (i - n_ctx_tiles) // tiles_per_latent)
        return (row, 0, 0)
    return index_map


def _pair_specs(width, n_ctx_tiles):
    return [pl.BlockSpec((TOKEN_TILE, width), lambda i: (jnp.minimum(i, n_ctx_tiles - 1), 0)),
            pl.BlockSpec((TOKEN_TILE, width), lambda i: (jnp.maximum(i - n_ctx_tiles, 0), 0))]


def _pick(is_ctx, ctx_ref, lat_ref):
    return jnp.where(is_ctx, ctx_ref[...], lat_ref[...])


def _input_projection(x_ctx, x_lat, mod3, w_in, lb, mod_map):
    d = x_ctx.shape[1]
    n_ctx_tiles = x_ctx.shape[0] // TOKEN_TILE
    t = x_ctx.shape[0] + x_lat.shape[0]
    width = w_in.shape[1]
    zh_w = 3 * HG_WIDTH
    zf_w = 2 * HG_WIDTH
    za_w = ATT_WIDTH + 2 * KV_WIDTH
    zg_w = 2 * d
    assert width == zh_w + zf_w + za_w + zg_w
    tm = TOKEN_TILE
    return pl.pallas_call(
        functools.partial(_inproj_kernel, d=d, n_ctx_tiles=n_ctx_tiles),
        grid=(t // tm,),
        in_specs=_pair_specs(d, n_ctx_tiles) + [
            pl.BlockSpec((1, 1, mod3.shape[2]), mod_map),
            pl.BlockSpec((d, width), lambda i: (0, 0)),
            pl.BlockSpec((2, HG_WIDTH), lambda i: (0, 0)),
        ],
        out_specs=[
            pl.BlockSpec((tm, zh_w), lambda i: (i, 0)),
            pl.BlockSpec((tm, zf_w), lambda i: (i, 0)),
            pl.BlockSpec((tm, za_w), lambda i: (i, 0)),
            pl.BlockSpec((tm, zg_w), lambda i: (i, 0)),
        ],
        out_shape=[
            jax.ShapeDtypeStruct((t, zh_w), BF16),
            jax.ShapeDtypeStruct((t, zf_w), F32),
            jax.ShapeDtypeStruct((t, za_w), F32),
            jax.ShapeDtypeStruct((t, zg_w), BF16),
        ],
        compiler_params=_params(("arbitrary",)),
    )(x_ctx, x_lat, mod3, w_in, lb)


def _hgrn_kernel(*refs, seq, has_state):
    if has_state:
        q_ref, v_ref, ff_ref, fb_ref, gs_ref, ng_ref, s0_ref, o_ref, sout_ref = refs
    else:
        q_ref, v_ref, ff_ref, fb_ref, gs_ref, ng_ref, o_ref, sout_ref = refs
        s0_ref = None
    c = HG_CHUNK
    n = seq // c
    q3 = q_ref[...].astype(F32).reshape(n, c, HG_DK)
    v3 = v_ref[...].reshape(n, c, HG_DV)
    pos = lax.broadcasted_iota(jnp.int32, (seq, HG_DK), 0) % c
    t_idx = lax.broadcasted_iota(jnp.int32, (c, c), 0)
    s_idx = lax.broadcasted_iota(jnp.int32, (c, c), 1)
    o_sum = None
    for direction, f_ref in enumerate((ff_ref, fb_ref)):
        reverse = direction == 1
        f = f_ref[...]
        k3 = (1.0 - f).reshape(n, c, HG_DK)
        b = jnp.log(f)
        step = 1
        while step < c:
            if reverse:
                b = b + jnp.where(pos < c - step, pltpu.roll(b, seq - step, axis=0), 0.0)
            else:
                b = b + jnp.where(pos >= step, pltpu.roll(b, step, axis=0), 0.0)
            step *= 2
        b3 = b.reshape(n, c, HG_DK)
        edge = b3[:, 0:1, :] if reverse else b3[:, c - 1:c, :]
        mid = b3[:, c // 2:c // 2 + 1, :]
        q_rel = q3 * jnp.exp(b3 - mid)
        k_rel = k3 * jnp.exp(mid - b3)
        q_mid = q_rel.astype(BF16)
        k_mid = k_rel.astype(BF16)
        q_in = (q_rel * jnp.exp(mid)).astype(BF16)
        k_end = (k_rel * jnp.exp(edge - mid)).astype(BF16)
        scores = jnp.einsum('ntd,nsd->nts', q_mid, k_mid, preferred_element_type=F32)
        keep = (s_idx >= t_idx) if reverse else (s_idx <= t_idx)
        scores = jnp.where(keep[None], scores, 0.0).astype(BF16)
        o_intra = jnp.einsum('nts,nsv->ntv', scores, v3, preferred_element_type=F32)
        upd = jnp.einsum('nsv,nsd->nvd', v3, k_end, preferred_element_type=F32)
        dec = jnp.exp(edge)
        if has_state:
            st = s0_ref[0, direction, 0].T
        else:
            st = jnp.zeros((HG_DV, HG_DK), F32)
        before = [None] * n
        for ci in (range(n - 1, -1, -1) if reverse else range(n)):
            before[ci] = st.astype(BF16)
            st = st * dec[ci] + upd[ci]
        st_before = jnp.stack(before, axis=0)
        o_inter = jnp.einsum('ntd,nvd->ntv', q_in, st_before, preferred_element_type=F32)
        o_dir = (o_intra + o_inter).reshape(seq, HG_DV)
        o_sum = o_dir if o_sum is None else o_sum + o_dir
        sout_ref[0, direction, 0] = st.T
    ms = jnp.mean(o_sum * o_sum, axis=-1, keepdims=True)
    o = o_sum * lax.rsqrt(ms + NORM_EPS) * ng_ref[...]
    o_ref[...] = (o * gs_ref[...].astype(F32)).astype(o_ref.dtype)


def _hgrn_scan(zh, zf, norm_g, s0, *, batch, seq, row_block0):
    has_state = s0 is not None
    h = HG_HEADS

    def col(section):
        return pl.BlockSpec((seq, HG_DK), lambda b, j: (row_block0 + b, section * h + j))

    in_specs = [col(0), col(1), col(0), col(1), col(2), pl.BlockSpec((1, HG_DV), lambda b, j: (0, 0))]
    args = [zh, zh, zf, zf, zh, norm_g]
    state_spec = pl.BlockSpec((1, 2, 1, HG_DK, HG_DV), lambda b, j: (b, 0, j, 0, 0))
    if has_state:
        in_specs.append(state_spec)
        args.append(s0)
    return pl.pallas_call(
        functools.partial(_hgrn_kernel, seq=seq, has_state=has_state),
        grid=(batch, h),
        in_specs=in_specs,
        out_specs=[pl.BlockSpec((seq, HG_DV), lambda b, j: (b, j)), state_spec],
        out_shape=[
            jax.ShapeDtypeStruct((batch * seq, h * HG_DV), BF16),
            jax.ShapeDtypeStruct((batch, 2, h, HG_DK, HG_DV), F32),
        ],
        compiler_params=_params(("arbitrary", "arbitrary")),
    )(*args)


def _group_rms_norm(x, group_ones, gain):
    sq = x * x
    hi = sq.astype(BF16)
    lo = (sq - hi.astype(F32)).astype(BF16)
    total = (jnp.dot(hi, group_ones, preferred_element_type=F32)
             + jnp.dot(lo, group_ones, preferred_element_type=F32))
    return x * lax.rsqrt(total * (1.0 / HEAD_DIM) + NORM_EPS) * gain


def _rope(x, cos, sin_signed):
    width = x.shape[1]
    quarter = HEAD_DIM // 4
    lane = lax.broadcasted_iota(jnp.int32, x.shape, 1)
    partner = jnp.where(lane % (2 * quarter) < quarter,
                        pltpu.roll(x, width - quarter, axis=1),
                        pltpu.roll(x, quarter, axis=1))
    return x * cos + partner * sin_signed


def _attn_kernel(*refs, seq, latent):
    if latent:
        za_ref, qg_ref, kg_ref, gm_ref, cos_ref, sin_ref, ck_ref, cv_ref, o_ref = refs
    else:
        za_ref, qg_ref, kg_ref, gm_ref, o_ref, kout_ref, vout_ref = refs
    q = za_ref[:, 0:ATT_WIDTH]
    k = za_ref[:, ATT_WIDTH:ATT_WIDTH + KV_WIDTH]
    v = za_ref[:, ATT_WIDTH + KV_WIDTH:ATT_WIDTH + 2 * KV_WIDTH]
    qn = _group_rms_norm(q, gm_ref[...], qg_ref[...])
    kn = _group_rms_norm(k, gm_ref[0:KV_WIDTH, 0:KV_WIDTH], kg_ref[...])
    if latent:
        qn = _rope(qn, cos_ref[...], sin_ref[...])
        kr = _rope(kn, cos_ref[:, 0:KV_WIDTH], sin_ref[:, 0:KV_WIDTH])
    else:
        kout_ref[...] = kn
        vout_ref[...] = v
        kr = kn
    qb = (qn * (HEAD_DIM ** -0.5)).astype(BF16)
    kb = kr.astype(BF16)
    vb = v.astype(BF16)
    if latent:
        ckb = ck_ref[0].astype(BF16)
        cvb = cv_ref[0].astype(BF16)
    groups = N_HEADS // N_KV_HEADS
    tq = ATT_Q_BLOCK if latent else seq
    nt = (((1,), (1,)), ((), ()))
    for kh in range(N_KV_HEADS):
        ksl = slice(kh * HEAD_DIM, (kh + 1) * HEAD_DIM)
        k_new = kb[:, ksl]
        v_new = vb[:, ksl]
        for blk in range(seq // tq):
            rows = slice(blk * tq, (blk + 1) * tq)
            heads = [kh * groups + g for g in range(groups)]
            q_st = jnp.concatenate([qb[rows, hd * HEAD_DIM:(hd + 1) * HEAD_DIM] for hd in heads], axis=0)
            s_new = lax.dot_general(q_st, k_new, nt, preferred_element_type=F32)
            m = jnp.max(s_new, axis=-1, keepdims=True)
            if latent:
                s_ctx = lax.dot_general(q_st, ckb[:, ksl], nt, preferred_element_type=F32)
                m = jnp.maximum(m, jnp.max(s_ctx, axis=-1, keepdims=True))
            p_new = jnp.exp(s_new - m)
            denom = jnp.sum(p_new, axis=-1, keepdims=True)
            acc = jnp.dot(p_new.astype(BF16), v_new, preferred_element_type=F32)
            if latent:
                p_ctx = jnp.exp(s_ctx - m)
                denom = denom + jnp.sum(p_ctx, axis=-1, keepdims=True)
                acc = acc + jnp.dot(p_ctx.astype(BF16), cvb[:, ksl], preferred_element_type=F32)
            out = acc / denom
            for g in range(0, groups, 2):
                pair = jnp.concatenate([out[g * tq:(g + 1) * tq], out[(g + 1) * tq:(g + 2) * tq]], axis=1)
                lane0 = heads[g] * HEAD_DIM
                o_ref[rows, lane0:lane0 + 2 * HEAD_DIM] = pair.astype(o_ref.dtype)


def _attention(za, q_gain, k_gain, group_ones, *, batch, seq, row_block0, rope=None, cache=None):
    latent = cache is not None
    za_w = za.shape[1]
    in_specs = [
        pl.BlockSpec((seq, za_w), lambda b: (row_block0 + b, 0)),
        pl.BlockSpec((1, ATT_WIDTH), lambda b: (0, 0)),
        pl.BlockSpec((1, KV_WIDTH), lambda b: (0, 0)),
        pl.BlockSpec((ATT_WIDTH, ATT_WIDTH), lambda b: (0, 0)),
    ]
    args = [za, q_gain, k_gain, group_ones]
    o_spec = pl.BlockSpec((seq, ATT_WIDTH), lambda b: (b, 0))
    o_shape = jax.ShapeDtypeStruct((batch * seq, ATT_WIDTH), BF16)
    if latent:
        cos, sin_signed = rope
        ck, cv = cache
        past = ck.shape[1]
        in_specs += [
            pl.BlockSpec((seq, ATT_WIDTH), lambda b: (0, 0)),
            pl.BlockSpec((seq, ATT_WIDTH), lambda b: (0, 0)),
            pl.BlockSpec((1, past, KV_WIDTH), lambda b: (b, 0, 0)),
            pl.BlockSpec((1, past, KV_WIDTH), lambda b: (b, 0, 0)),
        ]
        args += [cos, sin_signed, ck, cv]
        out_specs = o_spec
        out_shape = o_shape
    else:
        kv_spec = pl.BlockSpec((seq, KV_WIDTH), lambda b: (b, 0))
        kv_shape = jax.ShapeDtypeStruct((batch * seq, KV_WIDTH), F32)
        out_specs = [o_spec, kv_spec, kv_spec]
        out_shape = [o_shape, kv_shape, kv_shape]
    return pl.pallas_call(
        functools.partial(_attn_kernel, seq=seq, latent=latent),
        grid=(batch,),
        in_specs=in_specs,
        out_specs=out_specs,
        out_shape=out_shape,
        compiler_params=_params(("arbitrary",)),
    )(*args)


def _rope_tables(seq):
    quarter = HEAD_DIM // 4
    t = jnp.arange(seq)
    row = (t // GRID_W).astype(F32)
    colp = (t % GRID_W).astype(F32)
    inv_freq = ROPE_THETA ** (-jnp.arange(quarter, dtype=F32) / quarter)
    lane = jnp.arange(HEAD_DIM)
    pos = jnp.where((lane < HEAD_DIM // 2)[None, :], row[:, None], colp[:, None])
    ang = pos * inv_freq[lane % quarter][None, :]
    sign = jnp.where(lane % (2 * quarter) < quarter, -1.0, 1.0)[None, :]
    cos = jnp.tile(jnp.cos(ang), (1, N_HEADS))
    sin_signed = jnp.tile(jnp.sin(ang) * sign, (1, N_HEADS))
    return cos, sin_signed


def _layer_norm(y, g, b):
    mu = jnp.mean(y, axis=-1, keepdims=True)
    yc = y - mu
    var = jnp.mean(yc * yc, axis=-1, keepdims=True)
    return yc * lax.rsqrt(var + NORM_EPS) * g + b


def _rows_to_slabs(tile, st_ref, slab_ref):
    rows, d = tile.shape
    chunks = d // LANES
    for j in range(chunks):
        st_ref[j * TILE_STRIDE:j * TILE_STRIDE + rows, :] = tile[:, j * LANES:(j + 1) * LANES]
    for r in range(rows):
        slab_ref[r * chunks:(r + 1) * chunks, :] = st_ref[pl.ds(r, chunks, stride=TILE_STRIDE), :]


def _slabs_to_rows(slab_ref, st_ref, rows):
    chunks = SUBLANES
    for r in range(rows):
        st_ref[pl.ds(r, chunks, stride=TILE_STRIDE), :] = slab_ref[r * chunks:(r + 1) * chunks, :]
    return jnp.concatenate([st_ref[j * TILE_STRIDE:j * TILE_STRIDE + rows, :] for j in range(chunks)], axis=1)


def _post_mixer_kernel(xc_ref, xl_ref, mod_ref, oac_ref, oal_ref, obc_ref, obl_ref, zg_ref, wa_ref, wb_ref,
                       wo_ref, g_ref, b_ref, wrh_ref, wrl_ref, rb_ref,
                       x1_ref, h2_ref, slab_ref, idx_ref, rw_ref, st_ref, *, d, n_ctx_tiles):
    is_ctx = pl.program_id(0) < n_ctx_tiles
    gate1 = mod_ref[0, :, 2 * d:3 * d]
    shift2 = mod_ref[0, :, 3 * d:4 * d]
    scale2 = mod_ref[0, :, 4 * d:5 * d]
    branch_a = jnp.dot(_pick(is_ctx, oac_ref, oal_ref), wa_ref[...], preferred_element_type=F32)
    branch_b = jnp.dot(_pick(is_ctx, obc_ref, obl_ref), wb_ref[...], preferred_element_type=F32)
    merged = zg_ref[:, 0:d].astype(F32) * branch_a + zg_ref[:, d:2 * d].astype(F32) * branch_b
    mix = jnp.dot(merged.astype(BF16), wo_ref[...], preferred_element_type=F32)
    x = _pick(is_ctx, xc_ref, xl_ref)
    x1 = _layer_norm(DEEPNORM_ALPHA * x + gate1 * mix, g_ref[...], b_ref[...])
    x1_ref[...] = x1
    h2 = x1 * (1.0 + scale2) + shift2
    h2_ref[...] = h2.astype(h2_ref.dtype)
    _rows_to_slabs(h2, st_ref, slab_ref)
    idx_ref[...], rw_ref[...] = _route_tile(h2, wrh_ref[...], wrl_ref[...], rb_ref[...])


def _post_mixer(x_ctx, x_lat, mod3, oa_ctx, oa_lat, ob_ctx, ob_lat, zg, w_a, w_b, w_o, ln_g, ln_b,
                wr_hi, wr_lo, router_bias, mod_map):
    d = x_ctx.shape[1]
    n_ctx_tiles = x_ctx.shape[0] // TOKEN_TILE
    t = x_ctx.shape[0] + x_lat.shape[0]
    tm = TOKEN_TILE
    row = lambda i: (i, 0)
    full = lambda i: (0, 0)
    return pl.pallas_call(
        functools.partial(_post_mixer_kernel, d=d, n_ctx_tiles=n_ctx_tiles),
        grid=(t // tm,),
        in_specs=_pair_specs(d, n_ctx_tiles) + [
            pl.BlockSpec((1, 1, mod3.shape[2]), mod_map),
        ] + _pair_specs(oa_ctx.shape[1], n_ctx_tiles) + _pair_specs(ob_ctx.shape[1], n_ctx_tiles) + [
            pl.BlockSpec((tm, 2 * d), row),
            pl.BlockSpec(w_a.shape, full),
            pl.BlockSpec(w_b.shape, full),
            pl.BlockSpec(w_o.shape, full),
            pl.BlockSpec((1, d), full),
            pl.BlockSpec((1, d), full),
            pl.BlockSpec((N_EXPERTS, d), full),
            pl.BlockSpec((N_EXPERTS, d), full),
            pl.BlockSpec((N_EXPERTS, 1), full),
        ],
        out_specs=[pl.BlockSpec((tm, d), row), pl.BlockSpec((tm, d), row),
                   pl.BlockSpec((tm * SUBLANES, LANES), row),
                   pl.BlockSpec((TOP_K, tm), lambda i: (0, i)), pl.BlockSpec((TOP_K, tm), lambda i: (0, i))],
        out_shape=[jax.ShapeDtypeStruct((t, d), F32), jax.ShapeDtypeStruct((t, d), BF16),
                   jax.ShapeDtypeStruct((t * SUBLANES, LANES), F32),
                   jax.ShapeDtypeStruct((TOP_K, t), jnp.int32), jax.ShapeDtypeStruct((TOP_K, t), F32)],
        scratch_shapes=[pltpu.VMEM((SUBLANES * TILE_STRIDE, LANES), F32)],
        compiler_params=_params(("arbitrary",)),
    )(x_ctx, x_lat, mod3, oa_ctx, oa_lat, ob_ctx, ob_lat, zg, w_a, w_b, w_o, ln_g, ln_b, wr_hi, wr_lo, router_bias)


def _first_index_of_max(x, idx, sentinel):
    m = jnp.max(x, axis=0, keepdims=True)
    first = jnp.min(jnp.where(x == m, idx, sentinel), axis=0, keepdims=True)
    return m, first


def _route_tile(h, wh, wl, bias):
    hh = h.astype(BF16)
    hl = (h - hh.astype(F32)).astype(BF16)
    nt = (((1,), (1,)), ((), ()))
    logits = (lax.dot_general(wh, hh, nt, preferred_element_type=F32)
              + lax.dot_general(wh, hl, nt, preferred_element_type=F32)
              + lax.dot_general(wl, hh, nt, preferred_element_type=F32))
    scores = _sigmoid(logits)
    sel = scores + bias
    tm = sel.shape[1]
    neg = -jnp.inf
    gidx = lax.broadcasted_iota(jnp.int32, (GROUP_SIZE, tm), 0)
    group_scores = []
    for g in range(N_GROUPS):
        sg = sel[g * GROUP_SIZE:(g + 1) * GROUP_SIZE, :]
        m1, first = _first_index_of_max(sg, gidx, GROUP_SIZE)
        m2 = jnp.max(jnp.where(gidx == first, neg, sg), axis=0, keepdims=True)
        group_scores.append(m1 + m2)
    gs = jnp.concatenate(group_scores, axis=0)
    nidx = lax.broadcasted_iota(jnp.int32, (N_GROUPS, tm), 0)
    chosen = jnp.zeros((N_GROUPS, tm), jnp.bool_)
    for _ in range(TOPK_GROUPS):
        _, first = _first_index_of_max(gs, nidx, N_GROUPS)
        hit = nidx == first
        chosen = jnp.logical_or(chosen, hit)
        gs = jnp.where(hit, neg, gs)
    masked = jnp.concatenate(
        [jnp.where(chosen[g:g + 1, :], sel[g * GROUP_SIZE:(g + 1) * GROUP_SIZE, :], neg) for g in range(N_GROUPS)],
        axis=0)
    eidx = lax.broadcasted_iota(jnp.int32, (N_EXPERTS, tm), 0)
    picks, weights = [], []
    for _ in range(TOP_K):
        _, first = _first_index_of_max(masked, eidx, N_EXPERTS)
        hit = eidx == first
        picks.append(first)
        weights.append(jnp.sum(jnp.where(hit, scores, 0.0), axis=0, keepdims=True))
        masked = jnp.where(hit, neg, masked)
    wk = jnp.concatenate(weights, axis=0)
    return jnp.concatenate(picks, axis=0), wk / jnp.sum(wk, axis=0, keepdims=True) * ROUTED_SCALE


def _dispatch_lists(idx_t, w_t):
    k, t = idx_t.shape
    s = k * t
    n_blocks = s // MOE_BLOCK + N_EXPERTS
    flat_tok = jnp.arange(s, dtype=jnp.int32) % t
    sorted_e, sorted_tok, sorted_w = lax.sort((idx_t.reshape(s), flat_tok, w_t.reshape(s)), num_keys=1)
    experts = jnp.arange(N_EXPERTS + 1, dtype=jnp.int32)
    start = jnp.sum(sorted_e[None, :] < experts[:, None], axis=1, dtype=jnp.int32)
    counts = start[1:] - start[:-1]
    blocks_per_e = (counts + MOE_BLOCK - 1) // MOE_BLOCK
    block_end = jnp.cumsum(blocks_per_e)
    n_used = block_end[-1]
    blk = jnp.arange(n_blocks, dtype=jnp.int32)
    block_e = jnp.sum(block_end[None, :] <= blk[:, None], axis=1, dtype=jnp.int32)
    block_e = jnp.minimum(block_e, N_EXPERTS - 1)
    onehot = (block_e[:, None] == experts[None, :N_EXPERTS]).astype(jnp.int32)
    pick = lambda a: jnp.sum(onehot * a[None, :], axis=1)
    offset = (blk - (pick(block_end) - pick(blocks_per_e))) * MOE_BLOCK
    row_start = pick(start[:-1]) + offset
    n_rows = jnp.clip(pick(counts) - offset, 0, MOE_BLOCK)
    used = blk < n_used
    last_e = jnp.sum(jnp.where(blk == n_used - 1, block_e, 0))
    block_e = jnp.where(used, block_e, last_e)
    row_start = jnp.where(used, row_start, 0)
    n_rows = jnp.where(used, n_rows, 0)
    return block_e, row_start, n_rows, (n_used - 1).reshape(1), sorted_tok, sorted_w


def _block_lists_kernel(rs_ref, nr_ref, tok_ref, w_ref, off_ref, wout_ref, *, t):
    lane = lax.broadcasted_iota(jnp.int32, (LIST_ROWS, LANES), 1)
    row = lax.broadcasted_iota(jnp.int32, (LIST_ROWS, LANES), 0)

    def one_block(b, carry):
        start = rs_ref[b]
        n_rows = nr_ref[b]
        q = start // LANES
        lane0 = start % LANES
        q8 = pl.multiple_of((q // SUBLANES) * SUBLANES, SUBLANES)

        def window(ref):
            a = ref[pl.ds(q8, 2 * SUBLANES), :]
            a = pltpu.roll(a, 2 * SUBLANES - (q - q8), axis=0)
            lo = pltpu.roll(a[0:LIST_ROWS], LANES - lane0, axis=1)
            hi = pltpu.roll(a[1:LIST_ROWS + 1], LANES - lane0, axis=1)
            return jnp.where(lane < LANES - lane0, lo, hi)

        valid = row * LANES + lane < n_rows
        off_ref[b] = jnp.where(valid, window(tok_ref) * SUBLANES, t * SUBLANES)
        wout_ref[b] = jnp.where(valid, window(w_ref), 0.0)
        return carry

    lax.fori_loop(0, off_ref.shape[0], one_block, 0)


def _block_lists(row_start, n_rows, sorted_tok, sorted_w, t):
    n_blocks = row_start.shape[0]
    n_slots = sorted_tok.shape[0]
    list_rows = n_slots // LANES + 2 * SUBLANES
    pad = list_rows * LANES - n_slots
    tok2d = jnp.pad(sorted_tok, (0, pad)).reshape(list_rows, LANES)
    w2d = jnp.pad(sorted_w, (0, pad)).reshape(list_rows, LANES)
    full = lambda i, rs, nr: (0, 0)
    blk = pl.BlockSpec((n_blocks, LIST_ROWS, LANES), lambda i, rs, nr: (0, 0, 0))
    off, wts = pl.pallas_call(
        functools.partial(_block_lists_kernel, t=t),
        grid_spec=pltpu.PrefetchScalarGridSpec(
            num_scalar_prefetch=2,
            grid=(1,),
            in_specs=[pl.BlockSpec((list_rows, LANES), full), pl.BlockSpec((list_rows, LANES), full)],
            out_specs=[blk, blk],
        ),
        out_shape=[jax.ShapeDtypeStruct((n_blocks, LIST_ROWS, LANES), jnp.int32),
                   jax.ShapeDtypeStruct((n_blocks, LIST_ROWS, LANES), F32)],
        compiler_params=_params(("arbitrary",)),
    )(row_start, n_rows, tok2d, w2d)
    return off.reshape(n_blocks * LIST_LEN), wts.reshape(n_blocks * LIST_LEN)


def _expert_kernel(be_ref, nr_ref, last_ref, x_hbm, wg_hbm, wu_hbm, wd_hbm, off_ref, out_ref,
                   xs, xt, wgb, wub, wdb, sem, wsem):
    b = pl.program_id(0)
    chunks = wg_hbm.shape[1] // LANES
    n_rows = nr_ref[b]
    last = last_ref[0]
    ahead = WEIGHT_BUFFERS - 1

    def weight_copies(blk):
        e = be_ref[blk]
        slot = blk % WEIGHT_BUFFERS
        return [pltpu.make_async_copy(src.at[e], dst.at[slot], wsem.at[slot])
                for src, dst in ((wg_hbm, wgb), (wu_hbm, wub), (wd_hbm, wdb))]

    @pl.when(b == 0)
    def _load():
        for blk in range(ahead):
            @pl.when(blk <= last)
            def _(blk=blk):
                for cp in weight_copies(blk):
                    cp.start()
        rows = x_hbm.shape[0]
        cp = pltpu.make_async_copy(x_hbm, xs.at[pl.ds(0, rows)], sem)
        cp.start()
        cp.wait()
        xs[pl.ds(rows, SUBLANES), :] = jnp.zeros((SUBLANES, LANES), F32)
        xt[...] = jnp.zeros(xt.shape, F32)

    @pl.when(b + ahead <= last)
    def _prefetch():
        for cp in weight_copies(b + ahead):
            cp.start()

    @pl.when(b <= last)
    def _block():
        slot = b % WEIGHT_BUFFERS
        for cp in weight_copies(b):
            cp.wait()
        for seg in range(MOE_BLOCK // ROW_SEGMENT):
            @pl.when(n_rows > seg * ROW_SEGMENT)
            def _gather(seg=seg):
                for r in range(seg * ROW_SEGMENT, (seg + 1) * ROW_SEGMENT):
                    src = pl.multiple_of(off_ref[r], SUBLANES)
                    xt[pl.ds(r, chunks, stride=ROW_STRIDE), :] = xs[pl.ds(src, SUBLANES), :]

        x = jnp.concatenate(
            [xt[j * ROW_STRIDE:j * ROW_STRIDE + MOE_BLOCK, :] for j in range(chunks)], axis=1).astype(BF16)
        gate = jnp.dot(x, wgb[slot].astype(BF16), preferred_element_type=F32)
        up = jnp.dot(x, wub[slot].astype(BF16), preferred_element_type=F32)
        hidden = (_silu(gate) * up).astype(BF16)
        out_ref[...] = jnp.dot(hidden, wdb[slot].astype(BF16), preferred_element_type=F32).astype(out_ref.dtype)


def _combine_kernel(nr_ref, last_ref, o_ref, off_ref, w_ref, y_hbm, ys, ot, sem, *, t, n_blocks):
    b = pl.program_id(0)
    rows = t * SUBLANES
    chunks = o_ref.shape[1] // LANES
    n_rows = nr_ref[b]
    group = ROW_GROUP

    @pl.when(b == 0)
    def _init():
        ys[...] = jnp.zeros(ys.shape, F32)

    @pl.when(n_rows > 0)
    def _block():
        for j in range(chunks):
            ot[j * ROW_STRIDE:j * ROW_STRIDE + MOE_BLOCK, :] = o_ref[:, j * LANES:(j + 1) * LANES].astype(F32)
        for seg in range(MOE_BLOCK // ROW_SEGMENT):
            @pl.when(n_rows > seg * ROW_SEGMENT)
            def _scatter(seg=seg):
                for r0 in range(seg * ROW_SEGMENT, (seg + 1) * ROW_SEGMENT, group):
                    dst = [pl.multiple_of(off_ref[r0 + i], SUBLANES) for i in range(group)]
                    vals = [ys[pl.ds(dst[i], SUBLANES), :]
                            + w_ref[r0 + i] * ot[pl.ds(r0 + i, chunks, stride=ROW_STRIDE), :]
                            for i in range(group)]
                    for i in range(group):
                        ys[pl.ds(dst[i], SUBLANES), :] = vals[i]

    @pl.when(b == n_blocks - 1)
    def _store():
        cp = pltpu.make_async_copy(ys.at[pl.ds(0, rows)], y_hbm, sem)
        cp.start()
        cp.wait()


def _routed_experts(h2_rows, block_e, row_start, n_rows, last_block, sorted_tok, sorted_w, w_gate, w_up, w_down):
    rows, lanes = h2_rows.shape
    t = rows // SUBLANES
    n_blocks = block_e.shape[0]
    d, e_dim = w_gate.shape[1], w_gate.shape[2]
    assert d == SUBLANES * LANES and lanes == LANES
    offsets, weights = _block_lists(row_start, n_rows, sorted_tok, sorted_w, t)
    staging = pltpu.VMEM((SUBLANES * ROW_STRIDE, LANES), F32)
    resident = pltpu.VMEM((rows + SUBLANES, LANES), F32)
    expert_out = pl.pallas_call(
        _expert_kernel,
        grid_spec=pltpu.PrefetchScalarGridSpec(
            num_scalar_prefetch=3,
            grid=(n_blocks,),
            in_specs=[
                pl.BlockSpec(memory_space=pl.ANY),
                pl.BlockSpec(memory_space=pl.ANY),
                pl.BlockSpec(memory_space=pl.ANY),
                pl.BlockSpec(memory_space=pl.ANY),
                pl.BlockSpec((LIST_LEN,), lambda b, be, nr, last: (jnp.minimum(b, last[0]),),
                             memory_space=pltpu.SMEM),
            ],
            out_specs=pl.BlockSpec((MOE_BLOCK, d), lambda b, be, nr, last: (jnp.minimum(b, last[0]), 0)),
            scratch_shapes=[
                resident,
                staging,
                pltpu.VMEM((WEIGHT_BUFFERS, d, e_dim), F32),
                pltpu.VMEM((WEIGHT_BUFFERS, d, e_dim), F32),
                pltpu.VMEM((WEIGHT_BUFFERS, e_dim, d), F32),
                pltpu.SemaphoreType.DMA(()),
                pltpu.SemaphoreType.DMA((WEIGHT_BUFFERS,)),
            ],
        ),
        out_shape=jax.ShapeDtypeStruct((n_blocks * MOE_BLOCK, d), BF16),
        compiler_params=_params(("arbitrary",)),
    )(block_e, n_rows, last_block, h2_rows, w_gate, w_up, w_down, offsets)
    list_spec = pl.BlockSpec((LIST_LEN,), lambda b, nr, last: (jnp.minimum(b, last[0]),), memory_space=pltpu.SMEM)
    return pl.pallas_call(
        functools.partial(_combine_kernel, t=t, n_blocks=n_blocks),
        grid_spec=pltpu.PrefetchScalarGridSpec(
            num_scalar_prefetch=2,
            grid=(n_blocks,),
            in_specs=[pl.BlockSpec((MOE_BLOCK, d), lambda b, nr, last: (jnp.minimum(b, last[0]), 0)),
                      list_spec, list_spec],
            out_specs=pl.BlockSpec(memory_space=pl.ANY),
            scratch_shapes=[resident, staging, pltpu.SemaphoreType.DMA(())],
        ),
        out_shape=jax.ShapeDtypeStruct((rows, LANES), F32),
        compiler_params=_params(("arbitrary",)),
    )(n_rows, last_block, expert_out, offsets, weights)


def _final_kernel(x1_ref, h2_ref, routed_ref, mod_ref, wg_ref, wu_ref, wd_ref, g_ref, b_ref,
                  yc_ref, yl_ref, st_ref, *, d, n_ctx_tiles):
    is_ctx = pl.program_id(0) < n_ctx_tiles
    gate2 = mod_ref[0, :, 5 * d:6 * d]
    h = h2_ref[...].astype(BF16)
    gate = jnp.dot(h, wg_ref[...], preferred_element_type=F32)
    up = jnp.dot(h, wu_ref[...], preferred_element_type=F32)
    shared = jnp.dot((_silu(gate) * up).astype(BF16), wd_ref[...], preferred_element_type=F32)
    ffn = _slabs_to_rows(routed_ref, st_ref, h.shape[0]) + shared
    y = _layer_norm(DEEPNORM_ALPHA * x1_ref[...] + gate2 * ffn, g_ref[...], b_ref[...])

    @pl.when(is_ctx)
    def _():
        yc_ref[...] = y

    @pl.when(jnp.logical_not(is_ctx))
    def _():
        yl_ref[...] = y


def _final(x1, h2, routed, mod3, w_g, w_u, w_d, ln_g, ln_b, mod_map, t_ctx):
    t, d = x1.shape
    t_lat = t - t_ctx
    n_ctx_tiles = t_ctx // TOKEN_TILE
    tm = TOKEN_TILE
    row = lambda i: (i, 0)
    full = lambda i: (0, 0)
    return pl.pallas_call(
        functools.partial(_final_kernel, d=d, n_ctx_tiles=n_ctx_tiles),
        grid=(t // tm,),
        in_specs=[
            pl.BlockSpec((tm, d), row),
            pl.BlockSpec((tm, d), row),
            pl.BlockSpec((tm * SUBLANES, LANES), row),
            pl.BlockSpec((1, 1, mod3.shape[2]), mod_map),
            pl.BlockSpec(w_g.shape, full),
            pl.BlockSpec(w_u.shape, full),
            pl.BlockSpec(w_d.shape, full),
            pl.BlockSpec((1, d), full),
            pl.BlockSpec((1, d), full),
        ],
        out_specs=_pair_specs(d, n_ctx_tiles),
        out_shape=[jax.ShapeDtypeStruct((t_ctx, d), F32), jax.ShapeDtypeStruct((t_lat, d), F32)],
        scratch_shapes=[pltpu.VMEM((SUBLANES * TILE_STRIDE, LANES), F32)],
        compiler_params=_params(("arbitrary",)),
    )(x1, h2, routed, mod3, w_g, w_u, w_d, ln_g, ln_b)


def kernel(x_prompt, x_sample, cache_k, cache_v, state_hgrn, c, c_ctx, w_mod, b_mod, w_in, hg_lb, hg_norm_g, q_norm_g, k_norm_g, w_branch_a, w_branch_b, w_out, ln1_g, ln1_b, w_router, router_bias, w_e_gate, w_e_up, w_e_down, w_s_gate, w_s_up, w_s_down, ln2_g, ln2_b):
    assert w_mod.shape[0] == DEPTH
    n_ctx, seq_ctx, d = x_prompt.shape
    n_lat, seq_lat, _ = x_sample.shape
    t_ctx = n_ctx * seq_ctx
    t_lat = n_lat * seq_lat
    assert seq_ctx == TOKEN_TILE and seq_lat % TOKEN_TILE == 0
    layer = 0

    lb = jnp.cumsum(jax.nn.softmax(hg_lb.astype(F32), axis=0), axis=0)[layer]

    cond = jnp.concatenate([c_ctx[None, :], c], axis=0)
    cond = jnp.pad(cond, ((0, (-cond.shape[0]) % SUBLANES), (0, 0)))
    mod = _modulation(cond, w_mod[layer].astype(BF16), b_mod[layer][None, :])
    mod3 = mod.reshape(mod.shape[0], 1, mod.shape[1])
    mod_map = _mod_row_map(t_ctx // TOKEN_TILE, seq_lat // TOKEN_TILE)

    x_ctx = x_prompt.reshape(t_ctx, d)
    x_lat = x_sample.reshape(t_lat, d)
    zh, zf, za, zg = _input_projection(x_ctx, x_lat, mod3, w_in[layer].astype(BF16), lb, mod_map)

    norm_g = hg_norm_g[layer][None, :]
    oa_ctx, s_ctx = _hgrn_scan(zh, zf, norm_g, None, batch=n_ctx, seq=seq_ctx, row_block0=0)
    oa_lat, _ = _hgrn_scan(zh, zf, norm_g, state_hgrn[:, layer], batch=n_lat, seq=seq_lat,
                           row_block0=t_ctx // seq_lat)

    q_gain = jnp.tile(q_norm_g[layer], N_HEADS)[None, :]
    k_gain = jnp.tile(k_norm_g[layer], N_KV_HEADS)[None, :]
    lane = jnp.arange(ATT_WIDTH)
    group_ones = (lane[:, None] // HEAD_DIM == lane[None, :] // HEAD_DIM).astype(BF16)
    ob_ctx, k_ctx, v_ctx = _attention(za, q_gain, k_gain, group_ones, batch=n_ctx, seq=seq_ctx, row_block0=0)
    past = cache_k.shape[2]
    ob_lat = _attention(
        za, q_gain, k_gain, group_ones, batch=n_lat, seq=seq_lat, row_block0=t_ctx // seq_lat,
        rope=_rope_tables(seq_lat),
        cache=(cache_k[:, layer].reshape(n_lat, past, KV_WIDTH), cache_v[:, layer].reshape(n_lat, past, KV_WIDTH)))

    wr_t = w_router[layer].T
    wr_hi = wr_t.astype(BF16)
    wr_lo = (wr_t - wr_hi.astype(F32)).astype(BF16)
    x1, h2, h2_slabs, idx_t, w_t = _post_mixer(
        x_ctx, x_lat, mod3, oa_ctx, oa_lat, ob_ctx, ob_lat, zg, w_branch_a[layer].astype(BF16),
        w_branch_b[layer].astype(BF16), w_out[layer].astype(BF16), ln1_g[layer][None, :], ln1_b[layer][None, :],
        wr_hi, wr_lo, router_bias[layer][:, None], mod_map)

    routed = _routed_experts(h2_slabs, *_dispatch_lists(idx_t, w_t), w_e_gate[layer], w_e_up[layer], w_e_down[layer])

    y_ctx, y_lat = _final(x1, h2, routed, mod3, w_s_gate[layer].astype(BF16),
                          w_s_up[layer].astype(BF16), w_s_down[layer].astype(BF16),
                          ln2_g[layer][None, :], ln2_b[layer][None, :], mod_map, t_ctx)

    y_prompt = y_ctx.reshape(n_ctx, seq_ctx, d)
    y_sample = y_lat.reshape(n_lat, seq_lat, d)
    new_cache_k = k_ctx.reshape(n_ctx, 1, seq_ctx, N_KV_HEADS, HEAD_DIM)
    new_cache_v = v_ctx.reshape(n_ctx, 1, seq_ctx, N_KV_HEADS, HEAD_DIM)
    new_state = s_ctx[:, None]
    return (y_prompt, y_sample, new_cache_k, new_cache_v, new_state)
```

```python
import functools

import jax
import jax.numpy as jnp
from jax import lax
from jax.experimental import pallas as pl
from jax.experimental.pallas import tpu as pltpu

F32 = jnp.float32
BF16 = jnp.bfloat16

GRID_W = 64
HG_HEADS = 4
HG_DK = 128
HG_DV = 128
HG_WIDTH = HG_HEADS * HG_DK
N_HEADS = 8
N_KV_HEADS = 2
HEAD_DIM = 64
ATT_WIDTH = N_HEADS * HEAD_DIM
KV_WIDTH = N_KV_HEADS * HEAD_DIM
ROPE_THETA = 10000.0
N_EXPERTS = 256
TOP_K = 8
N_GROUPS = 8
TOPK_GROUPS = 4
GROUP_SIZE = N_EXPERTS // N_GROUPS
ROUTED_SCALE = 2.5
NORM_EPS = 1e-6
DEPTH = 1
DEEPNORM_ALPHA = (2 * DEPTH) ** 0.25

LANES = 128
SUBLANES = 8
VMEM_LIMIT = 56 * 1024 * 1024

TOKEN_TILE = 256
TILE_STRIDE = TOKEN_TILE + 1
HG_CHUNK = 32
ATT_Q_BLOCK = 128
MOE_BLOCK = 320
ROW_STRIDE = MOE_BLOCK + 1
LIST_LEN = 512
LIST_ROWS = LIST_LEN // LANES
ROW_GROUP = 8
SCATTER_DEPTH = 2
ROW_SEGMENT = 32
WEIGHT_BUFFERS = 3


def _sigmoid(x):
    return 1.0 / (1.0 + jnp.exp(-x))


def _silu(x):
    return x * _sigmoid(x)


def _params(sem=None):
    return pltpu.CompilerParams(dimension_semantics=sem, vmem_limit_bytes=VMEM_LIMIT)


def _mod_kernel(c_ref, w_ref, b_ref, o_ref):
    s = _silu(c_ref[...]).astype(BF16)
    o_ref[...] = jnp.dot(s, w_ref[...], preferred_element_type=F32) + b_ref[...]


def _modulation(cond, w_mod, b_mod):
    n, d = cond.shape
    width = w_mod.shape[1]
    tn = width // 4
    return pl.pallas_call(
        _mod_kernel,
        grid=(4,),
        in_specs=[
            pl.BlockSpec((n, d), lambda j: (0, 0)),
            pl.BlockSpec((d, tn), lambda j: (0, j)),
            pl.BlockSpec((1, tn), lambda j: (0, j)),
        ],
        out_specs=pl.BlockSpec((n, tn), lambda j: (0, j)),
        out_shape=jax.ShapeDtypeStruct((n, width), F32),
        compiler_params=_params(("arbitrary",)),
    )(cond, w_mod, b_mod)


def _inproj_kernel(xc_ref, xl_ref, mod_ref, w_ref, lb_ref, zh_ref, zf_ref, za_ref, zg_ref, *, d, n_ctx_tiles):
    shift = mod_ref[0, :, 0:d]
    scale = mod_ref[0, :, d:2 * d]
    x = _pick(pl.program_id(0) < n_ctx_tiles, xc_ref, xl_ref)
    h = (x * (1.0 + scale) + shift).astype(BF16)

    def proj(lo, hi):
        return jnp.dot(h, w_ref[:, lo:hi], preferred_element_type=F32)

    w = HG_WIDTH
    zh_ref[:, 0:w] = _silu(proj(0, w)).astype(zh_ref.dtype)
    zh_ref[:, w:2 * w] = proj(w, 2 * w).astype(zh_ref.dtype)
    for i in range(2):
        lb = lb_ref[i:i + 1, :]
        zf_ref[:, i * w:(i + 1) * w] = lb + (1.0 - lb) * _sigmoid(proj((2 + i) * w, (3 + i) * w))
    zh_ref[:, 2 * w:3 * w] = _silu(proj(4 * w, 5 * w)).astype(zh_ref.dtype)
    a0 = 5 * w
    a1 = a0 + ATT_WIDTH + 2 * KV_WIDTH
    za_ref[...] = proj(a0, a1)
    for i in range(4):
        lo = a1 + i * (d // 2)
        zg_ref[:, i * (d // 2):(i + 1) * (d // 2)] = _sigmoid(proj(lo, lo + d // 2)).astype(zg_ref.dtype)


def _mod_row_map(n_ctx_tiles, tiles_per_latent):
    def index_map(i):
        row = jnp.where(i < n_ctx_tiles, 0, 1 + (i - n_ctx_tiles) // tiles_per_latent)
        return (row, 0, 0)
    return index_map


def _pair_specs(width, n_ctx_tiles):
    return [pl.BlockSpec((TOKEN_TILE, width), lambda i: (jnp.minimum(i, n_ctx_tiles - 1), 0)),
            pl.BlockSpec((TOKEN_TILE, width), lambda i: (jnp.maximum(i - n_ctx_tiles, 0), 0))]


def _pick(is_ctx, ctx_ref, lat_ref):
    return jnp.where(is_ctx, ctx_ref[...], lat_ref[...])


def _input_projection(x_ctx, x_lat, mod3, w_in, lb, mod_map):
    d = x_ctx.shape[1]
    n_ctx_tiles = x_ctx.shape[0] // TOKEN_TILE
    t = x_ctx.shape[0] + x_lat.shape[0]
    width = w_in.shape[1]
    zh_w = 3 * HG_WIDTH
    zf_w = 2 * HG_WIDTH
    za_w = ATT_WIDTH + 2 * KV_WIDTH
    zg_w = 2 * d
    assert width == zh_w + zf_w + za_w + zg_w
    tm = TOKEN_TILE
    return pl.pallas_call(
        functools.partial(_inproj_kernel, d=d, n_ctx_tiles=n_ctx_tiles),
        grid=(t // tm,),
        in_specs=_pair_specs(d, n_ctx_tiles) + [
            pl.BlockSpec((1, 1, mod3.shape[2]), mod_map),
            pl.BlockSpec((d, width), lambda i: (0, 0)),
            pl.BlockSpec((2, HG_WIDTH), lambda i: (0, 0)),
        ],
        out_specs=[
            pl.BlockSpec((tm, zh_w), lambda i: (i, 0)),
            pl.BlockSpec((tm, zf_w), lambda i: (i, 0)),
            pl.BlockSpec((tm, za_w), lambda i: (i, 0)),
            pl.BlockSpec((tm, zg_w), lambda i: (i, 0)),
        ],
        out_shape=[
            jax.ShapeDtypeStruct((t, zh_w), BF16),
            jax.ShapeDtypeStruct((t, zf_w), F32),
            jax.ShapeDtypeStruct((t, za_w), F32),
            jax.ShapeDtypeStruct((t, zg_w), BF16),
        ],
        compiler_params=_params(("arbitrary",)),
    )(x_ctx, x_lat, mod3, w_in, lb)


def _hgrn_kernel(*refs, seq, has_state):
    if has_state:
        q_ref, v_ref, ff_ref, fb_ref, gs_ref, ng_ref, s0_ref, o_ref, sout_ref = refs
    else:
        q_ref, v_ref, ff_ref, fb_ref, gs_ref, ng_ref, o_ref, sout_ref = refs
        s0_ref = None
    c = HG_CHUNK
    n = seq // c
    q3 = q_ref[...].astype(F32).reshape(n, c, HG_DK)
    v3 = v_ref[...].reshape(n, c, HG_DV)
    pos = lax.broadcasted_iota(jnp.int32, (seq, HG_DK), 0) % c
    t_idx = lax.broadcasted_iota(jnp.int32, (c, c), 0)
    s_idx = lax.broadcasted_iota(jnp.int32, (c, c), 1)
    o_sum = None
    for direction, f_ref in enumerate((ff_ref, fb_ref)):
        reverse = direction == 1
        f = f_ref[...]
        k3 = (1.0 - f).reshape(n, c, HG_DK)
        b = jnp.log(f)
        step = 1
        while step < c:
            if reverse:
                b = b + jnp.where(pos < c - step, pltpu.roll(b, seq - step, axis=0), 0.0)
            else:
                b = b + jnp.where(pos >= step, pltpu.roll(b, step, axis=0), 0.0)
            step *= 2
        b3 = b.reshape(n, c, HG_DK)
        edge = b3[:, 0:1, :] if reverse else b3[:, c - 1:c, :]
        mid = b3[:, c // 2:c // 2 + 1, :]
        q_rel = q3 * jnp.exp(b3 - mid)
        k_rel = k3 * jnp.exp(mid - b3)
        q_mid = q_rel.astype(BF16)
        k_mid = k_rel.astype(BF16)
        q_in = (q_rel * jnp.exp(mid)).astype(BF16)
        k_end = (k_rel * jnp.exp(edge - mid)).astype(BF16)
        scores = jnp.einsum('ntd,nsd->nts', q_mid, k_mid, preferred_element_type=F32)
        keep = (s_idx >= t_idx) if reverse else (s_idx <= t_idx)
        scores = jnp.where(keep[None], scores, 0.0).astype(BF16)
        o_intra = jnp.einsum('nts,nsv->ntv', scores, v3, preferred_element_type=F32)
        upd = jnp.einsum('nsv,nsd->nvd', v3, k_end, preferred_element_type=F32)
        dec = jnp.exp(edge)
        if has_state:
            st = s0_ref[0, direction, 0].T
        else:
            st = jnp.zeros((HG_DV, HG_DK), F32)
        before = [None] * n
        for ci in (range(n - 1, -1, -1) if reverse else range(n)):
            before[ci] = st.astype(BF16)
            st = st * dec[ci] + upd[ci]
        st_before = jnp.stack(before, axis=0)
        o_inter = jnp.einsum('ntd,nvd->ntv', q_in, st_before, preferred_element_type=F32)
        o_dir = (o_intra + o_inter).reshape(seq, HG_DV)
        o_sum = o_dir if o_sum is None else o_sum + o_dir
        sout_ref[0, direction, 0] = st.T
    ms = jnp.mean(o_sum * o_sum, axis=-1, keepdims=True)
    o = o_sum * lax.rsqrt(ms + NORM_EPS) * ng_ref[...]
    o_ref[...] = (o * gs_ref[...].astype(F32)).astype(o_ref.dtype)


def _hgrn_scan(zh, zf, norm_g, s0, *, batch, seq, row_block0):
    has_state = s0 is not None
    h = HG_HEADS

    def col(section):
        return pl.BlockSpec((seq, HG_DK), lambda b, j: (row_block0 + b, section * h + j))

    in_specs = [col(0), col(1), col(0), col(1), col(2), pl.BlockSpec((1, HG_DV), lambda b, j: (0, 0))]
    args = [zh, zh, zf, zf, zh, norm_g]
    state_spec = pl.BlockSpec((1, 2, 1, HG_DK, HG_DV), lambda b, j: (b, 0, j, 0, 0))
    if has_state:
        in_specs.append(state_spec)
        args.append(s0)
    return pl.pallas_call(
        functools.partial(_hgrn_kernel, seq=seq, has_state=has_state),
        grid=(batch, h),
        in_specs=in_specs,
        out_specs=[pl.BlockSpec((seq, HG_DV), lambda b, j: (b, j)), state_spec],
        out_shape=[
            jax.ShapeDtypeStruct((batch * seq, h * HG_DV), BF16),
            jax.ShapeDtypeStruct((batch, 2, h, HG_DK, HG_DV), F32),
        ],
        compiler_params=_params(("arbitrary", "arbitrary")),
    )(*args)


def _group_rms_norm(x, group_ones, gain):
    sq = x * x
    hi = sq.astype(BF16)
    lo = (sq - hi.astype(F32)).astype(BF16)
    total = (jnp.dot(hi, group_ones, preferred_element_type=F32)
             + jnp.dot(lo, group_ones, preferred_element_type=F32))
    return x * lax.rsqrt(total * (1.0 / HEAD_DIM) + NORM_EPS) * gain


def _rope(x, cos, sin_signed):
    width = x.shape[1]
    quarter = HEAD_DIM // 4
    lane = lax.broadcasted_iota(jnp.int32, x.shape, 1)
    partner = jnp.where(lane % (2 * quarter) < quarter,
                        pltpu.roll(x, width - quarter, axis=1),
                        pltpu.roll(x, quarter, axis=1))
    return x * cos + partner * sin_signed


def _attn_kernel(*refs, seq, latent):
    if latent:
        za_ref, qg_ref, kg_ref, gm_ref, cos_ref, sin_ref, ck_ref, cv_ref, o_ref = refs
    else:
        za_ref, qg_ref, kg_ref, gm_ref, o_ref, kout_ref, vout_ref = refs
    q = za_ref[:, 0:ATT_WIDTH]
    k = za_ref[:, ATT_WIDTH:ATT_WIDTH + KV_WIDTH]
    v = za_ref[:, ATT_WIDTH + KV_WIDTH:ATT_WIDTH + 2 * KV_WIDTH]
    qn = _group_rms_norm(q, gm_ref[...], qg_ref[...])
    kn = _group_rms_norm(k, gm_ref[0:KV_WIDTH, 0:KV_WIDTH], kg_ref[...])
    if latent:
        qn = _rope(qn, cos_ref[...], sin_ref[...])
        kr = _rope(kn, cos_ref[:, 0:KV_WIDTH], sin_ref[:, 0:KV_WIDTH])
    else:
        kout_ref[...] = kn
        vout_ref[...] = v
        kr = kn
    qb = (qn * (HEAD_DIM ** -0.5)).astype(BF16)
    kb = kr.astype(BF16)
    vb = v.astype(BF16)
    if latent:
        ckb = ck_ref[0].astype(BF16)
        cvb = cv_ref[0].astype(BF16)
    groups = N_HEADS // N_KV_HEADS
    tq = ATT_Q_BLOCK if latent else seq
    nt = (((1,), (1,)), ((), ()))
    for kh in range(N_KV_HEADS):
        ksl = slice(kh * HEAD_DIM, (kh + 1) * HEAD_DIM)
        k_new = kb[:, ksl]
        v_new = vb[:, ksl]
        for blk in range(seq // tq):
            rows = slice(blk * tq, (blk + 1) * tq)
            heads = [kh * groups + g for g in range(groups)]
            q_st = jnp.concatenate([qb[rows, hd * HEAD_DIM:(hd + 1) * HEAD_DIM] for hd in heads], axis=0)
            s_new = lax.dot_general(q_st, k_new, nt, preferred_element_type=F32)
            m = jnp.max(s_new, axis=-1, keepdims=True)
            if latent:
                s_ctx = lax.dot_general(q_st, ckb[:, ksl], nt, preferred_element_type=F32)
                m = jnp.maximum(m, jnp.max(s_ctx, axis=-1, keepdims=True))
            p_new = jnp.exp(s_new - m)
            denom = jnp.sum(p_new, axis=-1, keepdims=True)
            acc = jnp.dot(p_new.astype(BF16), v_new, preferred_element_type=F32)
            if latent:
                p_ctx = jnp.exp(s_ctx - m)
                denom = denom + jnp.sum(p_ctx, axis=-1, keepdims=True)
                acc = acc + jnp.dot(p_ctx.astype(BF16), cvb[:, ksl], preferred_element_type=F32)
            out = acc / denom
            for g in range(0, groups, 2):
                pair = jnp.concatenate([out[g * tq:(g + 1) * tq], out[(g + 1) * tq:(g + 2) * tq]], axis=1)
                lane0 = heads[g] * HEAD_DIM
                o_ref[rows, lane0:lane0 + 2 * HEAD_DIM] = pair.astype(o_ref.dtype)


def _attention(za, q_gain, k_gain, group_ones, *, batch, seq, row_block0, rope=None, cache=None):
    latent = cache is not None
    za_w = za.shape[1]
    in_specs = [
        pl.BlockSpec((seq, za_w), lambda b: (row_block0 + b, 0)),
        pl.BlockSpec((1, ATT_WIDTH), lambda b: (0, 0)),
        pl.BlockSpec((1, KV_WIDTH), lambda b: (0, 0)),
        pl.BlockSpec((ATT_WIDTH, ATT_WIDTH), lambda b: (0, 0)),
    ]
    args = [za, q_gain, k_gain, group_ones]
    o_spec = pl.BlockSpec((seq, ATT_WIDTH), lambda b: (b, 0))
    o_shape = jax.ShapeDtypeStruct((batch * seq, ATT_WIDTH), BF16)
    if latent:
        cos, sin_signed = rope
        ck, cv = cache
        past = ck.shape[1]
        in_specs += [
            pl.BlockSpec((seq, ATT_WIDTH), lambda b: (0, 0)),
            pl.BlockSpec((seq, ATT_WIDTH), lambda b: (0, 0)),
            pl.BlockSpec((1, past, KV_WIDTH), lambda b: (b, 0, 0)),
            pl.BlockSpec((1, past, KV_WIDTH), lambda b: (b, 0, 0)),
        ]
        args += [cos, sin_signed, ck, cv]
        out_specs = o_spec
        out_shape = o_shape
    else:
        kv_spec = pl.BlockSpec((seq, KV_WIDTH), lambda b: (b, 0))
        kv_shape = jax.ShapeDtypeStruct((batch * seq, KV_WIDTH), F32)
        out_specs = [o_spec, kv_spec, kv_spec]
        out_shape = [o_shape, kv_shape, kv_shape]
    return pl.pallas_call(
        functools.partial(_attn_kernel, seq=seq, latent=latent),
        grid=(batch,),
        in_specs=in_specs,
        out_specs=out_specs,
        out_shape=out_shape,
        compiler_params=_params(("arbitrary",)),
    )(*args)


def _rope_tables(seq):
    quarter = HEAD_DIM // 4
    t = jnp.arange(seq)
    row = (t // GRID_W).astype(F32)
    colp = (t % GRID_W).astype(F32)
    inv_freq = ROPE_THETA ** (-jnp.arange(quarter, dtype=F32) / quarter)
    lane = jnp.arange(HEAD_DIM)
    pos = jnp.where((lane < HEAD_DIM // 2)[None, :], row[:, None], colp[:, None])
    ang = pos * inv_freq[lane % quarter][None, :]
    sign = jnp.where(lane % (2 * quarter) < quarter, -1.0, 1.0)[None, :]
    cos = jnp.tile(jnp.cos(ang), (1, N_HEADS))
    sin_signed = jnp.tile(jnp.sin(ang) * sign, (1, N_HEADS))
    return cos, sin_signed


def _layer_norm(y, g, b):
    mu = jnp.mean(y, axis=-1, keepdims=True)
    yc = y - mu
    var = jnp.mean(yc * yc, axis=-1, keepdims=True)
    return yc * lax.rsqrt(var + NORM_EPS) * g + b


def _rows_to_slabs(tile, st_ref, slab_ref):
    rows, d = tile.shape
    chunks = d // LANES
    for j in range(chunks):
        st_ref[j * TILE_STRIDE:j * TILE_STRIDE + rows, :] = tile[:, j * LANES:(j + 1) * LANES]
    for r in range(rows):
        slab_ref[r * chunks:(r + 1) * chunks, :] = st_ref[pl.ds(r, chunks, stride=TILE_STRIDE), :]


def _slabs_to_rows(slab_ref, st_ref, rows):
    chunks = SUBLANES
    for r in range(rows):
        st_ref[pl.ds(r, chunks, stride=TILE_STRIDE), :] = slab_ref[r * chunks:(r + 1) * chunks, :]
    return jnp.concatenate([st_ref[j * TILE_STRIDE:j * TILE_STRIDE + rows, :] for j in range(chunks)], axis=1)


def _post_mixer_kernel(xc_ref, xl_ref, mod_ref, oac_ref, oal_ref, obc_ref, obl_ref, zg_ref, wa_ref, wb_ref,
                       wo_ref, g_ref, b_ref, wrh_ref, wrl_ref, rb_ref,
                       x1_ref, h2_ref, slab_ref, idx_ref, rw_ref, st_ref, *, d, n_ctx_tiles):
    is_ctx = pl.program_id(0) < n_ctx_tiles
    gate1 = mod_ref[0, :, 2 * d:3 * d]
    shift2 = mod_ref[0, :, 3 * d:4 * d]
    scale2 = mod_ref[0, :, 4 * d:5 * d]
    branch_a = jnp.dot(_pick(is_ctx, oac_ref, oal_ref), wa_ref[...], preferred_element_type=F32)
    branch_b = jnp.dot(_pick(is_ctx, obc_ref, obl_ref), wb_ref[...], preferred_element_type=F32)
    merged = zg_ref[:, 0:d].astype(F32) * branch_a + zg_ref[:, d:2 * d].astype(F32) * branch_b
    mix = jnp.dot(merged.astype(BF16), wo_ref[...], preferred_element_type=F32)
    x = _pick(is_ctx, xc_ref, xl_ref)
    x1 = _layer_norm(DEEPNORM_ALPHA * x + gate1 * mix, g_ref[...], b_ref[...])
    x1_ref[...] = x1
    h2 = x1 * (1.0 + scale2) + shift2
    h2_ref[...] = h2.astype(h2_ref.dtype)
    _rows_to_slabs(h2, st_ref, slab_ref)
    idx_ref[...], rw_ref[...] = _route_tile(h2, wrh_ref[...], wrl_ref[...], rb_ref[...])


def _post_mixer(x_ctx, x_lat, mod3, oa_ctx, oa_lat, ob_ctx, ob_lat, zg, w_a, w_b, w_o, ln_g, ln_b,
                wr_hi, wr_lo, router_bias, mod_map):
    d = x_ctx.shape[1]
    n_ctx_tiles = x_ctx.shape[0] // TOKEN_TILE
    t = x_ctx.shape[0] + x_lat.shape[0]
    tm = TOKEN_TILE
    row = lambda i: (i, 0)
    full = lambda i: (0, 0)
    return pl.pallas_call(
        functools.partial(_post_mixer_kernel, d=d, n_ctx_tiles=n_ctx_tiles),
        grid=(t // tm,),
        in_specs=_pair_specs(d, n_ctx_tiles) + [
            pl.BlockSpec((1, 1, mod3.shape[2]), mod_map),
        ] + _pair_specs(oa_ctx.shape[1], n_ctx_tiles) + _pair_specs(ob_ctx.shape[1], n_ctx_tiles) + [
            pl.BlockSpec((tm, 2 * d), row),
            pl.BlockSpec(w_a.shape, full),
            pl.BlockSpec(w_b.shape, full),
            pl.BlockSpec(w_o.shape, full),
            pl.BlockSpec((1, d), full),
            pl.BlockSpec((1, d), full),
            pl.BlockSpec((N_EXPERTS, d), full),
            pl.BlockSpec((N_EXPERTS, d), full),
            pl.BlockSpec((N_EXPERTS, 1), full),
        ],
        out_specs=[pl.BlockSpec((tm, d), row), pl.BlockSpec((tm, d), row),
                   pl.BlockSpec((tm * SUBLANES, LANES), row),
                   pl.BlockSpec((TOP_K, tm), lambda i: (0, i)), pl.BlockSpec((TOP_K, tm), lambda i: (0, i))],
        out_shape=[jax.ShapeDtypeStruct((t, d), F32), jax.ShapeDtypeStruct((t, d), BF16),
                   jax.ShapeDtypeStruct((t * SUBLANES, LANES), F32),
                   jax.ShapeDtypeStruct((TOP_K, t), jnp.int32), jax.ShapeDtypeStruct((TOP_K, t), F32)],
        scratch_shapes=[pltpu.VMEM((SUBLANES * TILE_STRIDE, LANES), F32)],
        compiler_params=_params(("arbitrary",)),
    )(x_ctx, x_lat, mod3, oa_ctx, oa_lat, ob_ctx, ob_lat, zg, w_a, w_b, w_o, ln_g, ln_b, wr_hi, wr_lo, router_bias)


def _first_index_of_max(x, idx, sentinel):
    m = jnp.max(x, axis=0, keepdims=True)
    first = jnp.min(jnp.where(x == m, idx, sentinel), axis=0, keepdims=True)
    return m, first


def _route_tile(h, wh, wl, bias):
    hh = h.astype(BF16)
    hl = (h - hh.astype(F32)).astype(BF16)
    nt = (((1,), (1,)), ((), ()))
    logits = (lax.dot_general(wh, hh, nt, preferred_element_type=F32)
              + lax.dot_general(wh, hl, nt, preferred_element_type=F32)
              + lax.dot_general(wl, hh, nt, preferred_element_type=F32))
    scores = _sigmoid(logits)
    sel = scores + bias
    tm = sel.shape[1]
    neg = -jnp.inf
    gidx = lax.broadcasted_iota(jnp.int32, (GROUP_SIZE, tm), 0)
    group_scores = []
    for g in range(N_GROUPS):
        sg = sel[g * GROUP_SIZE:(g + 1) * GROUP_SIZE, :]
        m1, first = _first_index_of_max(sg, gidx, GROUP_SIZE)
        m2 = jnp.max(jnp.where(gidx == first, neg, sg), axis=0, keepdims=True)
        group_scores.append(m1 + m2)
    gs = jnp.concatenate(group_scores, axis=0)
    nidx = lax.broadcasted_iota(jnp.int32, (N_GROUPS, tm), 0)
    chosen = jnp.zeros((N_GROUPS, tm), jnp.bool_)
    for _ in range(TOPK_GROUPS):
        _, first = _first_index_of_max(gs, nidx, N_GROUPS)
        hit = nidx == first
        chosen = jnp.logical_or(chosen, hit)
        gs = jnp.where(hit, neg, gs)
    masked = jnp.concatenate(
        [jnp.where(chosen[g:g + 1, :], sel[g * GROUP_SIZE:(g + 1) * GROUP_SIZE, :], neg) for g in range(N_GROUPS)],
        axis=0)
    eidx = lax.broadcasted_iota(jnp.int32, (N_EXPERTS, tm), 0)
    picks, weights = [], []
    for _ in range(TOP_K):
        _, first = _first_index_of_max(masked, eidx, N_EXPERTS)
        hit = eidx == first
        picks.append(first)
        weights.append(jnp.sum(jnp.where(hit, scores, 0.0), axis=0, keepdims=True))
        masked = jnp.where(hit, neg, masked)
    wk = jnp.concatenate(weights, axis=0)
    return jnp.concatenate(picks, axis=0), wk / jnp.sum(wk, axis=0, keepdims=True) * ROUTED_SCALE


def _dispatch_lists(idx_t, w_t):
    k, t = idx_t.shape
    s = k * t
    n_blocks = s // MOE_BLOCK + N_EXPERTS
    flat_tok = jnp.arange(s, dtype=jnp.int32) % t
    sorted_e, sorted_tok, sorted_w = lax.sort((idx_t.reshape(s), flat_tok, w_t.reshape(s)), num_keys=1)
    experts = jnp.arange(N_EXPERTS + 1, dtype=jnp.int32)
    start = jnp.sum(sorted_e[None, :] < experts[:, None], axis=1, dtype=jnp.int32)
    counts = start[1:] - start[:-1]
    blocks_per_e = (counts + MOE_BLOCK - 1) // MOE_BLOCK
    block_end = jnp.cumsum(blocks_per_e)
    n_used = block_end[-1]
    blk = jnp.arange(n_blocks, dtype=jnp.int32)
    block_e = jnp.sum(block_end[None, :] <= blk[:, None], axis=1, dtype=jnp.int32)
    block_e = jnp.minimum(block_e, N_EXPERTS - 1)
    onehot = (block_e[:, None] == experts[None, :N_EXPERTS]).astype(jnp.int32)
    pick = lambda a: jnp.sum(onehot * a[None, :], axis=1)
    offset = (blk - (pick(block_end) - pick(blocks_per_e))) * MOE_BLOCK
    row_start = pick(start[:-1]) + offset
    n_rows = jnp.clip(pick(counts) - offset, 0, MOE_BLOCK)
    used = blk < n_used
    last_e = jnp.sum(jnp.where(blk == n_used - 1, block_e, 0))
    block_e = jnp.where(used, block_e, last_e)
    row_start = jnp.where(used, row_start, 0)
    n_rows = jnp.where(used, n_rows, 0)
    return block_e, row_start, n_rows, (n_used - 1).reshape(1), sorted_tok, sorted_w


def _block_lists_kernel(rs_ref, nr_ref, tok_ref, w_ref, off_ref, wout_ref, *, t):
    lane = lax.broadcasted_iota(jnp.int32, (LIST_ROWS, LANES), 1)
    row = lax.broadcasted_iota(jnp.int32, (LIST_ROWS, LANES), 0)

    def one_block(b, carry):
        start = rs_ref[b]
        n_rows = nr_ref[b]
        q = start // LANES
        lane0 = start % LANES
        q8 = pl.multiple_of((q // SUBLANES) * SUBLANES, SUBLANES)

        def window(ref):
            a = ref[pl.ds(q8, 2 * SUBLANES), :]
            a = pltpu.roll(a, 2 * SUBLANES - (q - q8), axis=0)
            lo = pltpu.roll(a[0:LIST_ROWS], LANES - lane0, axis=1)
            hi = pltpu.roll(a[1:LIST_ROWS + 1], LANES - lane0, axis=1)
            return jnp.where(lane < LANES - lane0, lo, hi)

        valid = row * LANES + lane < n_rows
        off_ref[b] = jnp.where(valid, window(tok_ref) * SUBLANES, t * SUBLANES)
        wout_ref[b] = jnp.where(valid, window(w_ref), 0.0)
        return carry

    lax.fori_loop(0, off_ref.shape[0], one_block, 0)


def _block_lists(row_start, n_rows, sorted_tok, sorted_w, t):
    n_blocks = row_start.shape[0]
    n_slots = sorted_tok.shape[0]
    list_rows = n_slots // LANES + 2 * SUBLANES
    pad = list_rows * LANES - n_slots
    tok2d = jnp.pad(sorted_tok, (0, pad)).reshape(list_rows, LANES)
    w2d = jnp.pad(sorted_w, (0, pad)).reshape(list_rows, LANES)
    full = lambda i, rs, nr: (0, 0)
    blk = pl.BlockSpec((n_blocks, LIST_ROWS, LANES), lambda i, rs, nr: (0, 0, 0))
    off, wts = pl.pallas_call(
        functools.partial(_block_lists_kernel, t=t),
        grid_spec=pltpu.PrefetchScalarGridSpec(
            num_scalar_prefetch=2,
            grid=(1,),
            in_specs=[pl.BlockSpec((list_rows, LANES), full), pl.BlockSpec((list_rows, LANES), full)],
            out_specs=[blk, blk],
        ),
        out_shape=[jax.ShapeDtypeStruct((n_blocks, LIST_ROWS, LANES), jnp.int32),
                   jax.ShapeDtypeStruct((n_blocks, LIST_ROWS, LANES), F32)],
        compiler_params=_params(("arbitrary",)),
    )(row_start, n_rows, tok2d, w2d)
    return off.reshape(n_blocks * LIST_LEN), wts.reshape(n_blocks * LIST_LEN)


def _expert_kernel(be_ref, nr_ref, last_ref, x_hbm, wg_hbm, wu_hbm, wd_hbm, off_ref, out_ref,
                   xs, xt, wgb, wub, wdb, sem, wsem):
    b = pl.program_id(0)
    chunks = wg_hbm.shape[1] // LANES
    n_rows = nr_ref[b]
    last = last_ref[0]
    ahead = WEIGHT_BUFFERS - 1

    def weight_copies(blk):
        e = be_ref[blk]
        slot = blk % WEIGHT_BUFFERS
        return [pltpu.make_async_copy(src.at[e], dst.at[slot], wsem.at[slot])
                for src, dst in ((wg_hbm, wgb), (wu_hbm, wub), (wd_hbm, wdb))]

    @pl.when(b == 0)
    def _load():
        for blk in range(ahead):
            @pl.when(blk <= last)
            def _(blk=blk):
                for cp in weight_copies(blk):
                    cp.start()
        rows = x_hbm.shape[0]
        cp = pltpu.make_async_copy(x_hbm, xs.at[pl.ds(0, rows)], sem)
        cp.start()
        cp.wait()
        xs[pl.ds(rows, SUBLANES), :] = jnp.zeros((SUBLANES, LANES), F32)
        xt[...] = jnp.zeros(xt.shape, F32)

    @pl.when(b + ahead <= last)
    def _prefetch():
        for cp in weight_copies(b + ahead):
            cp.start()

    @pl.when(b <= last)
    def _block():
        slot = b % WEIGHT_BUFFERS
        for cp in weight_copies(b):
            cp.wait()
        for seg in range(MOE_BLOCK // ROW_SEGMENT):
            @pl.when(n_rows > seg * ROW_SEGMENT)
            def _gather(seg=seg):
                for r in range(seg * ROW_SEGMENT, (seg + 1) * ROW_SEGMENT):
                    src = pl.multiple_of(off_ref[r], SUBLANES)
                    xt[pl.ds(r, chunks, stride=ROW_STRIDE), :] = xs[pl.ds(src, SUBLANES), :]

        x = jnp.concatenate(
            [xt[j * ROW_STRIDE:j * ROW_STRIDE + MOE_BLOCK, :] for j in range(chunks)], axis=1).astype(BF16)
        gate = jnp.dot(x, wgb[slot].astype(BF16), preferred_element_type=F32)
        up = jnp.dot(x, wub[slot].astype(BF16), preferred_element_type=F32)
        hidden = (_silu(gate) * up).astype(BF16)
        out_ref[...] = jnp.dot(hidden, wdb[slot].astype(BF16), preferred_element_type=F32).astype(out_ref.dtype)


def _combine_kernel(nr_ref, last_ref, o_ref, off_ref, w_ref, y_hbm, ys, ot, sem, *, t, n_blocks):
    b = pl.program_id(0)
    rows = t * SUBLANES
    chunks = o_ref.shape[1] // LANES
    n_rows = nr_ref[b]
    group = ROW_GROUP

    @pl.when(b == 0)
    def _init():
        ys[...] = jnp.zeros(ys.shape, F32)

    @pl.when(n_rows > 0)
    def _block():
        for j in range(chunks):
            ot[j * ROW_STRIDE:j * ROW_STRIDE + MOE_BLOCK, :] = o_ref[:, j * LANES:(j + 1) * LANES].astype(F32)
        def load_group(r0):
            dst = [pl.multiple_of(off_ref[r0 + i], SUBLANES) for i in range(group)]
            vals = [ys[pl.ds(dst[i], SUBLANES), :] + w_ref[r0 + i] * ot[pl.ds(r0 + i, chunks, stride=ROW_STRIDE), :]
                    for i in range(group)]
            return dst, vals

        def store_group(dst, vals):
            for i in range(group):
                ys[pl.ds(dst[i], SUBLANES), :] = vals[i]

        for seg in range(MOE_BLOCK // ROW_SEGMENT):
            @pl.when(n_rows > seg * ROW_SEGMENT)
            def _scatter(seg=seg):
                starts = list(range(seg * ROW_SEGMENT, (seg + 1) * ROW_SEGMENT, group))
                pending = [load_group(r0) for r0 in starts[:SCATTER_DEPTH]]
                for g in range(len(starts)):
                    store_group(*pending.pop(0))
                    if g + SCATTER_DEPTH < len(starts):
                        pending.append(load_group(starts[g + SCATTER_DEPTH]))

    @pl.when(b == n_blocks - 1)
    def _store():
        cp = pltpu.make_async_copy(ys.at[pl.ds(0, rows)], y_hbm, sem)
        cp.start()
        cp.wait()


def _routed_experts(h2_rows, block_e, row_start, n_rows, last_block, sorted_tok, sorted_w, w_gate, w_up, w_down):
    rows, lanes = h2_rows.shape
    t = rows // SUBLANES
    n_blocks = block_e.shape[0]
    d, e_dim = w_gate.shape[1], w_gate.shape[2]
    assert d == SUBLANES * LANES and lanes == LANES
    offsets, weights = _block_lists(row_start, n_rows, sorted_tok, sorted_w, t)
    staging = pltpu.VMEM((SUBLANES * ROW_STRIDE, LANES), F32)
    resident = pltpu.VMEM((rows + SUBLANES, LANES), F32)
    expert_out = pl.pallas_call(
        _expert_kernel,
        grid_spec=pltpu.PrefetchScalarGridSpec(
            num_scalar_prefetch=3,
            grid=(n_blocks,),
            in_specs=[
                pl.BlockSpec(memory_space=pl.ANY),
                pl.BlockSpec(memory_space=pl.ANY),
                pl.BlockSpec(memory_space=pl.ANY),
                pl.BlockSpec(memory_space=pl.ANY),
                pl.BlockSpec((LIST_LEN,), lambda b, be, nr, last: (jnp.minimum(b, last[0]),),
                             memory_space=pltpu.SMEM),
            ],
            out_specs=pl.BlockSpec((MOE_BLOCK, d), lambda b, be, nr, last: (jnp.minimum(b, last[0]), 0)),
            scratch_shapes=[
                resident,
                staging,
                pltpu.VMEM((WEIGHT_BUFFERS, d, e_dim), F32),
                pltpu.VMEM((WEIGHT_BUFFERS, d, e_dim), F32),
                pltpu.VMEM((WEIGHT_BUFFERS, e_dim, d), F32),
                pltpu.SemaphoreType.DMA(()),
                pltpu.SemaphoreType.DMA((WEIGHT_BUFFERS,)),
            ],
        ),
        out_shape=jax.ShapeDtypeStruct((n_blocks * MOE_BLOCK, d), BF16),
        compiler_params=_params(("arbitrary",)),
    )(block_e, n_rows, last_block, h2_rows, w_gate, w_up, w_down, offsets)
    list_spec = pl.BlockSpec((LIST_LEN,), lambda b, nr, last: (jnp.minimum(b, last[0]),), memory_space=pltpu.SMEM)
    return pl.pallas_call(
        functools.partial(_combine_kernel, t=t, n_blocks=n_blocks),
        grid_spec=pltpu.PrefetchScalarGridSpec(
            num_scalar_prefetch=2,
            grid=(n_blocks,),
            in_specs=[pl.BlockSpec((MOE_BLOCK, d), lambda b, nr, last: (jnp.minimum(b, last[0]), 0)),
                      list_spec, list_spec],
            out_specs=pl.BlockSpec(memory_space=pl.ANY),
            scratch_shapes=[resident, staging, pltpu.SemaphoreType.DMA(())],
        ),
        out_shape=jax.ShapeDtypeStruct((rows, LANES), F32),
        compiler_params=_params(("arbitrary",)),
    )(n_rows, last_block, expert_out, offsets, weights)


def _final_kernel(x1_ref, h2_ref, routed_ref, mod_ref, wg_ref, wu_ref, wd_ref, g_ref, b_ref,
                  yc_ref, yl_ref, st_ref, *, d, n_ctx_tiles):
    is_ctx = pl.program_id(0) < n_ctx_tiles
    gate2 = mod_ref[0, :, 5 * d:6 * d]
    h = h2_ref[...].astype(BF16)
    gate = jnp.dot(h, wg_ref[...], preferred_element_type=F32)
    up = jnp.dot(h, wu_ref[...], preferred_element_type=F32)
    shared = jnp.dot((_silu(gate) * up).astype(BF16), wd_ref[...], preferred_element_type=F32)
    ffn = _slabs_to_rows(routed_ref, st_ref, h.shape[0]) + shared
    y = _layer_norm(DEEPNORM_ALPHA * x1_ref[...] + gate2 * ffn, g_ref[...], b_ref[...])

    @pl.when(is_ctx)
    def _():
        yc_ref[...] = y

    @pl.when(jnp.logical_not(is_ctx))
    def _():
        yl_ref[...] = y


def _final(x1, h2, routed, mod3, w_g, w_u, w_d, ln_g, ln_b, mod_map, t_ctx):
    t, d = x1.shape
    t_lat = t - t_ctx
    n_ctx_tiles = t_ctx // TOKEN_TILE
    tm = TOKEN_TILE
    row = lambda i: (i, 0)
    full = lambda i: (0, 0)
    return pl.pallas_call(
        functools.partial(_final_kernel, d=d, n_ctx_tiles=n_ctx_tiles),
        grid=(t // tm,),
        in_specs=[
            pl.BlockSpec((tm, d), row),
            pl.BlockSpec((tm, d), row),
            pl.BlockSpec((tm * SUBLANES, LANES), row),
            pl.BlockSpec((1, 1, mod3.shape[2]), mod_map),
            pl.BlockSpec(w_g.shape, full),
            pl.BlockSpec(w_u.shape, full),
            pl.BlockSpec(w_d.shape, full),
            pl.BlockSpec((1, d), full),
            pl.BlockSpec((1, d), full),
        ],
        out_specs=_pair_specs(d, n_ctx_tiles),
        out_shape=[jax.ShapeDtypeStruct((t_ctx, d), F32), jax.ShapeDtypeStruct((t_lat, d), F32)],
        scratch_shapes=[pltpu.VMEM((SUBLANES * TILE_STRIDE, LANES), F32)],
        compiler_params=_params(("arbitrary",)),
    )(x1, h2, routed, mod3, w_g, w_u, w_d, ln_g, ln_b)


def kernel(x_prompt, x_sample, cache_k, cache_v, state_hgrn, c, c_ctx, w_mod, b_mod, w_in, hg_lb, hg_norm_g, q_norm_g, k_norm_g, w_branch_a, w_branch_b, w_out, ln1_g, ln1_b, w_router, router_bias, w_e_gate, w_e_up, w_e_down, w_s_gate, w_s_up, w_s_down, ln2_g, ln2_b):
    assert w_mod.shape[0] == DEPTH
    n_ctx, seq_ctx, d = x_prompt.shape
    n_lat, seq_lat, _ = x_sample.shape
    t_ctx = n_ctx * seq_ctx
    t_lat = n_lat * seq_lat
    assert seq_ctx == TOKEN_TILE and seq_lat % TOKEN_TILE == 0
    layer = 0

    lb = jnp.cumsum(jax.nn.softmax(hg_lb.astype(F32), axis=0), axis=0)[layer]

    cond = jnp.concatenate([c_ctx[None, :], c], axis=0)
    cond = jnp.pad(cond, ((0, (-cond.shape[0]) % SUBLANES), (0, 0)))
    mod = _modulation(cond, w_mod[layer].astype(BF16), b_mod[layer][None, :])
    mod3 = mod.reshape(mod.shape[0], 1, mod.shape[1])
    mod_map = _mod_row_map(t_ctx // TOKEN_TILE, seq_lat // TOKEN_TILE)

    x_ctx = x_prompt.reshape(t_ctx, d)
    x_lat = x_sample.reshape(t_lat, d)
    zh, zf, za, zg = _input_projection(x_ctx, x_lat, mod3, w_in[layer].astype(BF16), lb, mod_map)

    norm_g = hg_norm_g[layer][None, :]
    oa_ctx, s_ctx = _hgrn_scan(zh, zf, norm_g, None, batch=n_ctx, seq=seq_ctx, row_block0=0)
    oa_lat, _ = _hgrn_scan(zh, zf, norm_g, state_hgrn[:, layer], batch=n_lat, seq=seq_lat,
                           row_block0=t_ctx // seq_lat)

    q_gain = jnp.tile(q_norm_g[layer], N_HEADS)[None, :]
    k_gain = jnp.tile(k_norm_g[layer], N_KV_HEADS)[None, :]
    lane = jnp.arange(ATT_WIDTH)
    group_ones = (lane[:, None] // HEAD_DIM == lane[None, :] // HEAD_DIM).astype(BF16)
    ob_ctx, k_ctx, v_ctx = _attention(za, q_gain, k_gain, group_ones, batch=n_ctx, seq=seq_ctx, row_block0=0)
    past = cache_k.shape[2]
    ob_lat = _attention(
        za, q_gain, k_gain, group_ones, batch=n_lat, seq=seq_lat, row_block0=t_ctx // seq_lat,
        rope=_rope_tables(seq_lat),
        cache=(cache_k[:, layer].reshape(n_lat, past, KV_WIDTH), cache_v[:, layer].reshape(n_lat, past, KV_WIDTH)))

    wr_t = w_router[layer].T
    wr_hi = wr_t.astype(BF16)
    wr_lo = (wr_t - wr_hi.astype(F32)).astype(BF16)
    x1, h2, h2_slabs, idx_t, w_t = _post_mixer(
        x_ctx, x_lat, mod3, oa_ctx, oa_lat, ob_ctx, ob_lat, zg, w_branch_a[layer].astype(BF16),
        w_branch_b[layer].astype(BF16), w_out[layer].astype(BF16), ln1_g[layer][None, :], ln1_b[layer][None, :],
        wr_hi, wr_lo, router_bias[layer][:, None], mod_map)

    routed = _routed_experts(h2_slabs, *_dispatch_lists(idx_t, w_t), w_e_gate[layer], w_e_up[layer], w_e_down[layer])

    y_ctx, y_lat = _final(x1, h2, routed, mod3, w_s_gate[layer].astype(BF16),
                          w_s_up[layer].astype(BF16), w_s_down[layer].astype(BF16),
                          ln2_g[layer][None, :], ln2_b[layer][None, :], mod_map, t_ctx)

    y_prompt = y_ctx.reshape(n_ctx, seq_ctx, d)
    y_sample = y_lat.reshape(n_lat, seq_lat, d)
    new_cache_k = k_ctx.reshape(n_ctx, 1, seq_ctx, N_KV_HEADS, HEAD_DIM)
    new_cache_v = v_ctx.reshape(n_ctx, 1, seq_ctx, N_KV_HEADS, HEAD_DIM)
    new_state = s_ctx[:, None]
    return (y_prompt, y_sample, new_cache_k, new_cache_v, new_state)
```

```python
import functools

import jax
import jax.numpy as jnp
from jax import lax
from jax.experimental import pallas as pl
from jax.experimental.pallas import tpu as pltpu

F32 = jnp.float32
BF16 = jnp.bfloat16

GRID_W = 64
HG_HEADS = 4
HG_DK = 128
HG_DV = 128
HG_WIDTH = HG_HEADS * HG_DK
N_HEADS = 8
N_KV_HEADS = 2
HEAD_DIM = 64
ATT_WIDTH = N_HEADS * HEAD_DIM
KV_WIDTH = N_KV_HEADS * HEAD_DIM
ROPE_THETA = 10000.0
N_EXPERTS = 256
TOP_K = 8
N_GROUPS = 8
TOPK_GROUPS = 4
GROUP_SIZE = N_EXPERTS // N_GROUPS
ROUTED_SCALE = 2.5
NORM_EPS = 1e-6
DEPTH = 1
DEEPNORM_ALPHA = (2 * DEPTH) ** 0.25

LANES = 128
SUBLANES = 8
VMEM_LIMIT = 56 * 1024 * 1024

CAST_CHUNK = 512
TOKEN_TILE = 256
TILE_STRIDE = TOKEN_TILE + 1
HG_CHUNK = 32
HG_HEADS_PER_STEP_CTX = 4
HG_HEADS_PER_STEP_LATENT = 2
ATT_Q_BLOCK = 128
MOE_BLOCK = 320
ROW_STRIDE = MOE_BLOCK + 1
LIST_LEN = 512
LIST_ROWS = LIST_LEN // LANES
ROW_GROUP = 8
ROW_SEGMENT = 32
WEIGHT_BUFFERS = 3


def _sigmoid(x):
    return 1.0 / (1.0 + jnp.exp(-x))


def _silu(x):
    return x * _sigmoid(x)


def _params(sem=None):
    return pltpu.CompilerParams(dimension_semantics=sem, vmem_limit_bytes=VMEM_LIMIT)


def _resident(shape):
    return pl.BlockSpec(shape, lambda i: (0,) * len(shape), pipeline_mode=pl.Buffered(1))


def _cast_weight_once(src_ref, dst_ref):
    @pl.when(pl.program_id(0) == 0)
    def _():
        width = src_ref.shape[1]
        for lo in range(0, width, CAST_CHUNK):
            hi = min(lo + CAST_CHUNK, width)
            dst_ref[:, lo:hi] = src_ref[:, lo:hi].astype(dst_ref.dtype)


def _mod_kernel(c_ref, w_ref, b_ref, o_ref):
    s = _silu(c_ref[...]).astype(BF16)
    o_ref[...] = jnp.dot(s, w_ref[...].astype(BF16), preferred_element_type=F32) + b_ref[...]


def _modulation(cond, w_mod, b_mod):
    n, d = cond.shape
    width = w_mod.shape[1]
    tn = width // 4
    return pl.pallas_call(
        _mod_kernel,
        grid=(4,),
        in_specs=[
            pl.BlockSpec((n, d), lambda j: (0, 0)),
            pl.BlockSpec((d, tn), lambda j: (0, j)),
            pl.BlockSpec((1, tn), lambda j: (0, j)),
        ],
        out_specs=pl.BlockSpec((n, tn), lambda j: (0, j)),
        out_shape=jax.ShapeDtypeStruct((n, width), F32),
        compiler_params=_params(("arbitrary",)),
    )(cond, w_mod, b_mod)


def _inproj_kernel(xc_ref, xl_ref, mod_ref, w_ref, lb_ref, zh_ref, zf_ref, za_ref, zg_ref, wb_ref,
                   *, d, n_ctx_tiles):
    _cast_weight_once(w_ref, wb_ref)
    shift = mod_ref[0, :, 0:d]
    scale = mod_ref[0, :, d:2 * d]
    x = _pick(pl.program_id(0) < n_ctx_tiles, xc_ref, xl_ref)
    h = (x * (1.0 + scale) + shift).astype(BF16)

    def proj(lo, hi):
        return jnp.dot(h, wb_ref[:, lo:hi], preferred_element_type=F32)

    w = HG_WIDTH
    zh_ref[:, 0:w] = _silu(proj(0, w)).astype(zh_ref.dtype)
    zh_ref[:, w:2 * w] = proj(w, 2 * w).astype(zh_ref.dtype)
    for i in range(2):
        lb = lb_ref[i:i + 1, :]
        zf_ref[:, i * w:(i + 1) * w] = lb + (1.0 - lb) * _sigmoid(proj((2 + i) * w, (3 + i) * w))
    zh_ref[:, 2 * w:3 * w] = _silu(proj(4 * w, 5 * w)).astype(zh_ref.dtype)
    a0 = 5 * w
    a1 = a0 + ATT_WIDTH + 2 * KV_WIDTH
    za_ref[...] = proj(a0, a1)
    for i in range(4):
        lo = a1 + i * (d // 2)
        zg_ref[:, i * (d // 2):(i + 1) * (d // 2)] = _sigmoid(proj(lo, lo + d // 2)).astype(zg_ref.dtype)


def _mod_row_map(n_ctx_tiles, tiles_per_latent):
    def index_map(i):
        row = jnp.where(i < n_ctx_tiles, 0, 1 + (i - n_ctx_tiles) // tiles_per_latent)
        return (row, 0, 0)
    return index_map


def _pair_specs(width, n_ctx_tiles):
    return [pl.BlockSpec((TOKEN_TILE, width), lambda i: (jnp.minimum(i, n_ctx_tiles - 1), 0)),
            pl.BlockSpec((TOKEN_TILE, width), lambda i: (jnp.maximum(i - n_ctx_tiles, 0), 0))]


def _pick(is_ctx, ctx_ref, lat_ref):
    return jnp.where(is_ctx, ctx_ref[...], lat_ref[...])


def _input_projection(x_ctx, x_lat, mod3, w_in, lb, mod_map):
    d = x_ctx.shape[1]
    n_ctx_tiles = x_ctx.shape[0] // TOKEN_TILE
    t = x_ctx.shape[0] + x_lat.shape[0]
    width = w_in.shape[1]
    zh_w = 3 * HG_WIDTH
    zf_w = 2 * HG_WIDTH
    za_w = ATT_WIDTH + 2 * KV_WIDTH
    zg_w = 2 * d
    assert width == zh_w + zf_w + za_w + zg_w
    tm = TOKEN_TILE
    return pl.pallas_call(
        functools.partial(_inproj_kernel, d=d, n_ctx_tiles=n_ctx_tiles),
        grid=(t // tm,),
        in_specs=_pair_specs(d, n_ctx_tiles) + [
            pl.BlockSpec((1, 1, mod3.shape[2]), mod_map),
            _resident((d, width)),
            pl.BlockSpec((2, HG_WIDTH), lambda i: (0, 0)),
        ],
        out_specs=[
            pl.BlockSpec((tm, zh_w), lambda i: (i, 0)),
            pl.BlockSpec((tm, zf_w), lambda i: (i, 0)),
            pl.BlockSpec((tm, za_w), lambda i: (i, 0)),
            pl.BlockSpec((tm, zg_w), lambda i: (i, 0)),
        ],
        out_shape=[
            jax.ShapeDtypeStruct((t, zh_w), BF16),
            jax.ShapeDtypeStruct((t, zf_w), F32),
            jax.ShapeDtypeStruct((t, za_w), F32),
            jax.ShapeDtypeStruct((t, zg_w), BF16),
        ],
        scratch_shapes=[pltpu.VMEM((d, width), BF16)],
        compiler_params=_params(("arbitrary",)),
    )(x_ctx, x_lat, mod3, w_in, lb)


def _hgrn_kernel(*refs, seq, has_state, heads):
    if has_state:
        q_ref, v_ref, ff_ref, fb_ref, gs_ref, ng_ref, s0_ref, o_ref, sout_ref = refs
    else:
        q_ref, v_ref, ff_ref, fb_ref, gs_ref, ng_ref, o_ref, sout_ref = refs
        s0_ref = None
    for hd in range(heads):
        _hgrn_head(q_ref, v_ref, ff_ref, fb_ref, gs_ref, ng_ref, s0_ref, o_ref, sout_ref, hd, seq)


def _hgrn_head(q_ref, v_ref, ff_ref, fb_ref, gs_ref, ng_ref, s0_ref, o_ref, sout_ref, hd, seq):
    has_state = s0_ref is not None
    cols = slice(hd * HG_DK, (hd + 1) * HG_DK)
    c = HG_CHUNK
    n = seq // c
    q3 = q_ref[:, cols].astype(F32).reshape(n, c, HG_DK)
    v3 = v_ref[:, cols].reshape(n, c, HG_DV)
    pos = lax.broadcasted_iota(jnp.int32, (seq, HG_DK), 0) % c
    t_idx = lax.broadcasted_iota(jnp.int32, (c, c), 0)
    s_idx = lax.broadcasted_iota(jnp.int32, (c, c), 1)
    o_sum = None
    for direction, f_ref in enumerate((ff_ref, fb_ref)):
        reverse = direction == 1
        f = f_ref[:, cols]
        k3 = (1.0 - f).reshape(n, c, HG_DK)
        b = jnp.log(f)
        step = 1
        while step < c:
            if reverse:
                b = b + jnp.where(pos < c - step, pltpu.roll(b, seq - step, axis=0), 0.0)
            else:
                b = b + jnp.where(pos >= step, pltpu.roll(b, step, axis=0), 0.0)
            step *= 2
        b3 = b.reshape(n, c, HG_DK)
        edge = b3[:, 0:1, :] if reverse else b3[:, c - 1:c, :]
        mid = b3[:, c // 2:c // 2 + 1, :]
        q_rel = q3 * jnp.exp(b3 - mid)
        k_rel = k3 * jnp.exp(mid - b3)
        q_mid = q_rel.astype(BF16)
        k_mid = k_rel.astype(BF16)
        q_in = (q_rel * jnp.exp(mid)).astype(BF16)
        k_end = (k_rel * jnp.exp(edge - mid)).astype(BF16)
        scores = jnp.einsum('ntd,nsd->nts', q_mid, k_mid, preferred_element_type=F32)
        keep = (s_idx >= t_idx) if reverse else (s_idx <= t_idx)
        scores = jnp.where(keep[None], scores, 0.0).astype(BF16)
        o_intra = jnp.einsum('nts,nsv->ntv', scores, v3, preferred_element_type=F32)
        upd = jnp.einsum('nsv,nsd->nvd', v3, k_end, preferred_element_type=F32)
        dec = jnp.exp(edge)
        if has_state:
            st = s0_ref[0, direction, hd].T
        else:
            st = jnp.zeros((HG_DV, HG_DK), F32)
        before = [None] * n
        for ci in (range(n - 1, -1, -1) if reverse else range(n)):
            before[ci] = st.astype(BF16)
            st = st * dec[ci] + upd[ci]
        st_before = jnp.stack(before, axis=0)
        o_inter = jnp.einsum('ntd,nvd->ntv', q_in, st_before, preferred_element_type=F32)
        o_dir = (o_intra + o_inter).reshape(seq, HG_DV)
        o_sum = o_dir if o_sum is None else o_sum + o_dir
        sout_ref[0, direction, hd] = st.T
    ms = jnp.mean(o_sum * o_sum, axis=-1, keepdims=True)
    o = o_sum * lax.rsqrt(ms + NORM_EPS) * ng_ref[...]
    o_ref[:, cols] = (o * gs_ref[:, cols].astype(F32)).astype(o_ref.dtype)


def _hgrn_scan(zh, zf, norm_g, s0, *, batch, seq, row_block0, heads):
    has_state = s0 is not None
    h = HG_HEADS
    groups = h // heads
    width = heads * HG_DK

    def col(section):
        return pl.BlockSpec((seq, width), lambda b, j: (row_block0 + b, section * groups + j))

    in_specs = [col(0), col(1), col(0), col(1), col(2), pl.BlockSpec((1, HG_DV), lambda b, j: (0, 0))]
    args = [zh, zh, zf, zf, zh, norm_g]
    state_spec = pl.BlockSpec((1, 2, heads, HG_DK, HG_DV), lambda b, j: (b, 0, j, 0, 0))
    if has_state:
        in_specs.append(state_spec)
        args.append(s0)
    return pl.pallas_call(
        functools.partial(_hgrn_kernel, seq=seq, has_state=has_state, heads=heads),
        grid=(batch, groups),
        in_specs=in_specs,
        out_specs=[pl.BlockSpec((seq, width), lambda b, j: (b, j)), state_spec],
        out_shape=[
            jax.ShapeDtypeStruct((batch * seq, h * HG_DV), BF16),
            jax.ShapeDtypeStruct((batch, 2, h, HG_DK, HG_DV), F32),
        ],
        compiler_params=_params(("arbitrary", "arbitrary")),
    )(*args)


def _group_rms_norm(x, group_ones, gain):
    sq = x * x
    hi = sq.astype(BF16)
    lo = (sq - hi.astype(F32)).astype(BF16)
    total = (jnp.dot(hi, group_ones, preferred_element_type=F32)
             + jnp.dot(lo, group_ones, preferred_element_type=F32))
    return x * lax.rsqrt(total * (1.0 / HEAD_DIM) + NORM_EPS) * gain


def _rope(x, cos, sin_signed):
    width = x.shape[1]
    quarter = HEAD_DIM // 4
    lane = lax.broadcasted_iota(jnp.int32, x.shape, 1)
    partner = jnp.where(lane % (2 * quarter) < quarter,
                        pltpu.roll(x, width - quarter, axis=1),
                        pltpu.roll(x, quarter, axis=1))
    return x * cos + partner * sin_signed


def _attn_kernel(*refs, seq, latent):
    if latent:
        za_ref, qg_ref, kg_ref, gm_ref, cos_ref, sin_ref, ck_ref, cv_ref, o_ref = refs
    else:
        za_ref, qg_ref, kg_ref, gm_ref, o_ref, kout_ref, vout_ref = refs
    q = za_ref[:, 0:ATT_WIDTH]
    k = za_ref[:, ATT_WIDTH:ATT_WIDTH + KV_WIDTH]
    v = za_ref[:, ATT_WIDTH + KV_WIDTH:ATT_WIDTH + 2 * KV_WIDTH]
    qn = _group_rms_norm(q, gm_ref[...], qg_ref[...])
    kn = _group_rms_norm(k, gm_ref[0:KV_WIDTH, 0:KV_WIDTH], kg_ref[...])
    if latent:
        qn = _rope(qn, cos_ref[...], sin_ref[...])
        kr = _rope(kn, cos_ref[:, 0:KV_WIDTH], sin_ref[:, 0:KV_WIDTH])
    else:
        kout_ref[...] = kn
        vout_ref[...] = v
        kr = kn
    qb = (qn * (HEAD_DIM ** -0.5)).astype(BF16)
    kb = kr.astype(BF16)
    vb = v.astype(BF16)
    if latent:
        ckb = ck_ref[0].astype(BF16)
        cvb = cv_ref[0].astype(BF16)
    groups = N_HEADS // N_KV_HEADS
    tq = ATT_Q_BLOCK if latent else seq
    nt = (((1,), (1,)), ((), ()))
    for kh in range(N_KV_HEADS):
        ksl = slice(kh * HEAD_DIM, (kh + 1) * HEAD_DIM)
        k_new = kb[:, ksl]
        v_new = vb[:, ksl]
        for blk in range(seq // tq):
            rows = slice(blk * tq, (blk + 1) * tq)
            heads = [kh * groups + g for g in range(groups)]
            q_st = jnp.concatenate([qb[rows, hd * HEAD_DIM:(hd + 1) * HEAD_DIM] for hd in heads], axis=0)
            s_new = lax.dot_general(q_st, k_new, nt, preferred_element_type=F32)
            m = jnp.max(s_new, axis=-1, keepdims=True)
            if latent:
                s_ctx = lax.dot_general(q_st, ckb[:, ksl], nt, preferred_element_type=F32)
                m = jnp.maximum(m, jnp.max(s_ctx, axis=-1, keepdims=True))
            p_new = jnp.exp(s_new - m)
            denom = jnp.sum(p_new, axis=-1, keepdims=True)
            acc = jnp.dot(p_new.astype(BF16), v_new, preferred_element_type=F32)
            if latent:
                p_ctx = jnp.exp(s_ctx - m)
                denom = denom + jnp.sum(p_ctx, axis=-1, keepdims=True)
                acc = acc + jnp.dot(p_ctx.astype(BF16), cvb[:, ksl], preferred_element_type=F32)
            out = acc / denom
            for g in range(0, groups, 2):
                pair = jnp.concatenate([out[g * tq:(g + 1) * tq], out[(g + 1) * tq:(g + 2) * tq]], axis=1)
                lane0 = heads[g] * HEAD_DIM
                o_ref[rows, lane0:lane0 + 2 * HEAD_DIM] = pair.astype(o_ref.dtype)


def _attention(za, q_gain, k_gain, group_ones, *, batch, seq, row_block0, rope=None, cache=None):
    latent = cache is not None
    za_w = za.shape[1]
    in_specs = [
        pl.BlockSpec((seq, za_w), lambda b: (row_block0 + b, 0)),
        pl.BlockSpec((1, ATT_WIDTH), lambda b: (0, 0)),
        pl.BlockSpec((1, KV_WIDTH), lambda b: (0, 0)),
        pl.BlockSpec((ATT_WIDTH, ATT_WIDTH), lambda b: (0, 0)),
    ]
    args = [za, q_gain, k_gain, group_ones]
    o_spec = pl.BlockSpec((seq, ATT_WIDTH), lambda b: (b, 0))
    o_shape = jax.ShapeDtypeStruct((batch * seq, ATT_WIDTH), BF16)
    if latent:
        cos, sin_signed = rope
        ck, cv = cache
        past = ck.shape[1]
        in_specs += [
            pl.BlockSpec((seq, ATT_WIDTH), lambda b: (0, 0)),
            pl.BlockSpec((seq, ATT_WIDTH), lambda b: (0, 0)),
            pl.BlockSpec((1, past, KV_WIDTH), lambda b: (b, 0, 0)),
            pl.BlockSpec((1, past, KV_WIDTH), lambda b: (b, 0, 0)),
        ]
        args += [cos, sin_signed, ck, cv]
        out_specs = o_spec
        out_shape = o_shape
    else:
        kv_spec = pl.BlockSpec((seq, KV_WIDTH), lambda b: (b, 0))
        kv_shape = jax.ShapeDtypeStruct((batch * seq, KV_WIDTH), F32)
        out_specs = [o_spec, kv_spec, kv_spec]
        out_shape = [o_shape, kv_shape, kv_shape]
    return pl.pallas_call(
        functools.partial(_attn_kernel, seq=seq, latent=latent),
        grid=(batch,),
        in_specs=in_specs,
        out_specs=out_specs,
        out_shape=out_shape,
        compiler_params=_params(("arbitrary",)),
    )(*args)


def _rope_tables(seq):
    quarter = HEAD_DIM // 4
    t = jnp.arange(seq)
    row = (t // GRID_W).astype(F32)
    colp = (t % GRID_W).astype(F32)
    inv_freq = ROPE_THETA ** (-jnp.arange(quarter, dtype=F32) / quarter)
    lane = jnp.arange(HEAD_DIM)
    pos = jnp.where((lane < HEAD_DIM // 2)[None, :], row[:, None], colp[:, None])
    ang = pos * inv_freq[lane % quarter][None, :]
    sign = jnp.where(lane % (2 * quarter) < quarter, -1.0, 1.0)[None, :]
    cos = jnp.tile(jnp.cos(ang), (1, N_HEADS))
    sin_signed = jnp.tile(jnp.sin(ang) * sign, (1, N_HEADS))
    return cos, sin_signed


def _layer_norm(y, g, b):
    mu = jnp.mean(y, axis=-1, keepdims=True)
    yc = y - mu
    var = jnp.mean(yc * yc, axis=-1, keepdims=True)
    return yc * lax.rsqrt(var + NORM_EPS) * g + b


def _rows_to_slabs(tile, st_ref, slab_ref):
    rows, d = tile.shape
    chunks = d // LANES
    for j in range(chunks):
        st_ref[j * TILE_STRIDE:j * TILE_STRIDE + rows, :] = tile[:, j * LANES:(j + 1) * LANES]
    for r in range(rows):
        slab_ref[r * chunks:(r + 1) * chunks, :] = st_ref[pl.ds(r, chunks, stride=TILE_STRIDE), :]


def _slabs_to_rows(slab_ref, st_ref, rows):
    chunks = SUBLANES
    for r in range(rows):
        st_ref[pl.ds(r, chunks, stride=TILE_STRIDE), :] = slab_ref[r * chunks:(r + 1) * chunks, :]
    return jnp.concatenate([st_ref[j * TILE_STRIDE:j * TILE_STRIDE + rows, :] for j in range(chunks)], axis=1)


def _post_mixer_kernel(xc_ref, xl_ref, mod_ref, oac_ref, oal_ref, obc_ref, obl_ref, zg_ref, wa_ref, wb_ref,
                       wo_ref, g_ref, b_ref, wrh_ref, wrl_ref, rb_ref,
                       x1_ref, h2_ref, slab_ref, idx_ref, rw_ref, st_ref, wab_ref, wbb_ref, wob_ref,
                       *, d, n_ctx_tiles):
    for src, dst in ((wa_ref, wab_ref), (wb_ref, wbb_ref), (wo_ref, wob_ref)):
        _cast_weight_once(src, dst)
    is_ctx = pl.program_id(0) < n_ctx_tiles
    gate1 = mod_ref[0, :, 2 * d:3 * d]
    shift2 = mod_ref[0, :, 3 * d:4 * d]
    scale2 = mod_ref[0, :, 4 * d:5 * d]
    branch_a = jnp.dot(_pick(is_ctx, oac_ref, oal_ref), wab_ref[...], preferred_element_type=F32)
    branch_b = jnp.dot(_pick(is_ctx, obc_ref, obl_ref), wbb_ref[...], preferred_element_type=F32)
    merged = zg_ref[:, 0:d].astype(F32) * branch_a + zg_ref[:, d:2 * d].astype(F32) * branch_b
    mix = jnp.dot(merged.astype(BF16), wob_ref[...], preferred_element_type=F32)
    x = _pick(is_ctx, xc_ref, xl_ref)
    x1 = _layer_norm(DEEPNORM_ALPHA * x + gate1 * mix, g_ref[...], b_ref[...])
    x1_ref[...] = x1
    h2 = x1 * (1.0 + scale2) + shift2
    h2_ref[...] = h2.astype(h2_ref.dtype)
    _rows_to_slabs(h2, st_ref, slab_ref)
    idx_ref[...], rw_ref[...] = _route_tile(h2, wrh_ref[...], wrl_ref[...], rb_ref[...])


def _post_mixer(x_ctx, x_lat, mod3, oa_ctx, oa_lat, ob_ctx, ob_lat, zg, w_a, w_b, w_o, ln_g, ln_b,
                wr_hi, wr_lo, router_bias, mod_map):
    d = x_ctx.shape[1]
    n_ctx_tiles = x_ctx.shape[0] // TOKEN_TILE
    t = x_ctx.shape[0] + x_lat.shape[0]
    tm = TOKEN_TILE
    row = lambda i: (i, 0)
    full = lambda i: (0, 0)
    return pl.pallas_call(
        functools.partial(_post_mixer_kernel, d=d, n_ctx_tiles=n_ctx_tiles),
        grid=(t // tm,),
        in_specs=_pair_specs(d, n_ctx_tiles) + [
            pl.BlockSpec((1, 1, mod3.shape[2]), mod_map),
        ] + _pair_specs(oa_ctx.shape[1], n_ctx_tiles) + _pair_specs(ob_ctx.shape[1], n_ctx_tiles) + [
            pl.BlockSpec((tm, 2 * d), row),
            _resident(w_a.shape),
            _resident(w_b.shape),
            _resident(w_o.shape),
            pl.BlockSpec((1, d), full),
            pl.BlockSpec((1, d), full),
            pl.BlockSpec((N_EXPERTS, d), full),
            pl.BlockSpec((N_EXPERTS, d), full),
            pl.BlockSpec((N_EXPERTS, 1), full),
        ],
        out_specs=[pl.BlockSpec((tm, d), row), pl.BlockSpec((tm, d), row),
                   pl.BlockSpec((tm * SUBLANES, LANES), row),
                   pl.BlockSpec((TOP_K, tm), lambda i: (0, i)), pl.BlockSpec((TOP_K, tm), lambda i: (0, i))],
        out_shape=[jax.ShapeDtypeStruct((t, d), F32), jax.ShapeDtypeStruct((t, d), BF16),
                   jax.ShapeDtypeStruct((t * SUBLANES, LANES), F32),
                   jax.ShapeDtypeStruct((TOP_K, t), jnp.int32), jax.ShapeDtypeStruct((TOP_K, t), F32)],
        scratch_shapes=[pltpu.VMEM((SUBLANES * TILE_STRIDE, LANES), F32),
                        pltpu.VMEM(w_a.shape, BF16), pltpu.VMEM(w_b.shape, BF16), pltpu.VMEM(w_o.shape, BF16)],
        compiler_params=_params(("arbitrary",)),
    )(x_ctx, x_lat, mod3, oa_ctx, oa_lat, ob_ctx, ob_lat, zg, w_a, w_b, w_o, ln_g, ln_b, wr_hi, wr_lo, router_bias)


def _first_index_of_max(x, idx, sentinel):
    m = jnp.max(x, axis=0, keepdims=True)
    first = jnp.min(jnp.where(x == m, idx, sentinel), axis=0, keepdims=True)
    return m, first


def _route_tile(h, wh, wl, bias):
    hh = h.astype(BF16)
    hl = (h - hh.astype(F32)).astype(BF16)
    nt = (((1,), (1,)), ((), ()))
    logits = (lax.dot_general(wh, hh, nt, preferred_element_type=F32)
              + lax.dot_general(wh, hl, nt, preferred_element_type=F32)
              + lax.dot_general(wl, hh, nt, preferred_element_type=F32))
    scores = _sigmoid(logits)
    sel = scores + bias
    tm = sel.shape[1]
    neg = -jnp.inf
    gidx = lax.broadcasted_iota(jnp.int32, (GROUP_SIZE, tm), 0)
    group_scores = []
    for g in range(N_GROUPS):
        sg = sel[g * GROUP_SIZE:(g + 1) * GROUP_SIZE, :]
        m1, first = _first_index_of_max(sg, gidx, GROUP_SIZE)
        m2 = jnp.max(jnp.where(gidx == first, neg, sg), axis=0, keepdims=True)
        group_scores.append(m1 + m2)
    gs = jnp.concatenate(group_scores, axis=0)
    nidx = lax.broadcasted_iota(jnp.int32, (N_GROUPS, tm), 0)
    chosen = jnp.zeros((N_GROUPS, tm), jnp.bool_)
    for _ in range(TOPK_GROUPS):
        _, first = _first_index_of_max(gs, nidx, N_GROUPS)
        hit = nidx == first
        chosen = jnp.logical_or(chosen, hit)
        gs = jnp.where(hit, neg, gs)
    masked = jnp.concatenate(
        [jnp.where(chosen[g:g + 1, :], sel[g * GROUP_SIZE:(g + 1) * GROUP_SIZE, :], neg) for g in range(N_GROUPS)],
        axis=0)
    eidx = lax.broadcasted_iota(jnp.int32, (N_EXPERTS, tm), 0)
    picks, weights = [], []
    for _ in range(TOP_K):
        _, first = _first_index_of_max(masked, eidx, N_EXPERTS)
        hit = eidx == first
        picks.append(first)
        weights.append(jnp.sum(jnp.where(hit, scores, 0.0), axis=0, keepdims=True))
        masked = jnp.where(hit, neg, masked)
    wk = jnp.concatenate(weights, axis=0)
    return jnp.concatenate(picks, axis=0), wk / jnp.sum(wk, axis=0, keepdims=True) * ROUTED_SCALE


def _dispatch_lists(idx_t, w_t):
    k, t = idx_t.shape
    s = k * t
    n_blocks = s // MOE_BLOCK + N_EXPERTS
    flat_tok = jnp.arange(s, dtype=jnp.int32) % t
    sorted_e, sorted_tok, sorted_w = lax.sort((idx_t.reshape(s), flat_tok, w_t.reshape(s)), num_keys=1)
    experts = jnp.arange(N_EXPERTS + 1, dtype=jnp.int32)
    start = jnp.sum(sorted_e[None, :] < experts[:, None], axis=1, dtype=jnp.int32)
    counts = start[1:] - start[:-1]
    blocks_per_e = (counts + MOE_BLOCK - 1) // MOE_BLOCK
    block_end = jnp.cumsum(blocks_per_e)
    n_used = block_end[-1]
    blk = jnp.arange(n_blocks, dtype=jnp.int32)
    block_e = jnp.sum(block_end[None, :] <= blk[:, None], axis=1, dtype=jnp.int32)
    block_e = jnp.minimum(block_e, N_EXPERTS - 1)
    onehot = (block_e[:, None] == experts[None, :N_EXPERTS]).astype(jnp.int32)
    pick = lambda a: jnp.sum(onehot * a[None, :], axis=1)
    offset = (blk - (pick(block_end) - pick(blocks_per_e))) * MOE_BLOCK
    row_start = pick(start[:-1]) + offset
    n_rows = jnp.clip(pick(counts) - offset, 0, MOE_BLOCK)
    used = blk < n_used
    last_e = jnp.sum(jnp.where(blk == n_used - 1, block_e, 0))
    block_e = jnp.where(used, block_e, last_e)
    row_start = jnp.where(used, row_start, 0)
    n_rows = jnp.where(used, n_rows, 0)
    return block_e, row_start, n_rows, (n_used - 1).reshape(1), sorted_tok, sorted_w


def _block_lists_kernel(rs_ref, nr_ref, tok_ref, w_ref, off_ref, wout_ref, *, t):
    lane = lax.broadcasted_iota(jnp.int32, (LIST_ROWS, LANES), 1)
    row = lax.broadcasted_iota(jnp.int32, (LIST_ROWS, LANES), 0)

    def one_block(b, carry):
        start = rs_ref[b]
        n_rows = nr_ref[b]
        q = start // LANES
        lane0 = start % LANES
        q8 = pl.multiple_of((q // SUBLANES) * SUBLANES, SUBLANES)

        def window(ref):
            a = ref[pl.ds(q8, 2 * SUBLANES), :]
            a = pltpu.roll(a, 2 * SUBLANES - (q - q8), axis=0)
            lo = pltpu.roll(a[0:LIST_ROWS], LANES - lane0, axis=1)
            hi = pltpu.roll(a[1:LIST_ROWS + 1], LANES - lane0, axis=1)
            return jnp.where(lane < LANES - lane0, lo, hi)

        valid = row * LANES + lane < n_rows
        off_ref[b] = jnp.where(valid, window(tok_ref) * SUBLANES, t * SUBLANES)
        wout_ref[b] = jnp.where(valid, window(w_ref), 0.0)
        return carry

    lax.fori_loop(0, off_ref.shape[0], one_block, 0)


def _block_lists(row_start, n_rows, sorted_tok, sorted_w, t):
    n_blocks = row_start.shape[0]
    n_slots = sorted_tok.shape[0]
    list_rows = n_slots // LANES + 2 * SUBLANES
    pad = list_rows * LANES - n_slots
    tok2d = jnp.pad(sorted_tok, (0, pad)).reshape(list_rows, LANES)
    w2d = jnp.pad(sorted_w, (0, pad)).reshape(list_rows, LANES)
    full = lambda i, rs, nr: (0, 0)
    blk = pl.BlockSpec((n_blocks, LIST_ROWS, LANES), lambda i, rs, nr: (0, 0, 0))
    off, wts = pl.pallas_call(
        functools.partial(_block_lists_kernel, t=t),
        grid_spec=pltpu.PrefetchScalarGridSpec(
            num_scalar_prefetch=2,
            grid=(1,),
            in_specs=[pl.BlockSpec((list_rows, LANES), full), pl.BlockSpec((list_rows, LANES), full)],
            out_specs=[blk, blk],
        ),
        out_shape=[jax.ShapeDtypeStruct((n_blocks, LIST_ROWS, LANES), jnp.int32),
                   jax.ShapeDtypeStruct((n_blocks, LIST_ROWS, LANES), F32)],
        compiler_params=_params(("arbitrary",)),
    )(row_start, n_rows, tok2d, w2d)
    return off.reshape(n_blocks * LIST_LEN), wts.reshape(n_blocks * LIST_LEN)


def _expert_kernel(be_ref, nr_ref, last_ref, x_hbm, wg_hbm, wu_hbm, wd_hbm, off_ref, out_ref,
                   xs, xt, wgb, wub, wdb, sem, wsem):
    b = pl.program_id(0)
    chunks = wg_hbm.shape[1] // LANES
    n_rows = nr_ref[b]
    last = last_ref[0]
    ahead = WEIGHT_BUFFERS - 1

    def weight_copies(blk):
        e = be_ref[blk]
        slot = blk % WEIGHT_BUFFERS
        return [pltpu.make_async_copy(src.at[e], dst.at[slot], wsem.at[slot])
                for src, dst in ((wg_hbm, wgb), (wu_hbm, wub), (wd_hbm, wdb))]

    @pl.when(b == 0)
    def _load():
        for blk in range(ahead):
            @pl.when(blk <= last)
            def _(blk=blk):
                for cp in weight_copies(blk):
                    cp.start()
        rows = x_hbm.shape[0]
        cp = pltpu.make_async_copy(x_hbm, xs.at[pl.ds(0, rows)], sem)
        cp.start()
        cp.wait()
        xs[pl.ds(rows, SUBLANES), :] = jnp.zeros((SUBLANES, LANES), F32)
        xt[...] = jnp.zeros(xt.shape, F32)

    @pl.when(b + ahead <= last)
    def _prefetch():
        for cp in weight_copies(b + ahead):
            cp.start()

    @pl.when(b <= last)
    def _block():
        slot = b % WEIGHT_BUFFERS
        for cp in weight_copies(b):
            cp.wait()
        for seg in range(MOE_BLOCK // ROW_SEGMENT):
            @pl.when(n_rows > seg * ROW_SEGMENT)
            def _gather(seg=seg):
                for r in range(seg * ROW_SEGMENT, (seg + 1) * ROW_SEGMENT):
                    src = pl.multiple_of(off_ref[r], SUBLANES)
                    xt[pl.ds(r, chunks, stride=ROW_STRIDE), :] = xs[pl.ds(src, SUBLANES), :]

        x = jnp.concatenate(
            [xt[j * ROW_STRIDE:j * ROW_STRIDE + MOE_BLOCK, :] for j in range(chunks)], axis=1).astype(BF16)
        gate = jnp.dot(x, wgb[slot].astype(BF16), preferred_element_type=F32)
        up = jnp.dot(x, wub[slot].astype(BF16), preferred_element_type=F32)
        hidden = (_silu(gate) * up).astype(BF16)
        out_ref[...] = jnp.dot(hidden, wdb[slot].astype(BF16), preferred_element_type=F32).astype(out_ref.dtype)


def _combine_kernel(nr_ref, last_ref, o_ref, off_ref, w_ref, y_hbm, ys, ot, sem, *, t, n_blocks):
    b = pl.program_id(0)
    rows = t * SUBLANES
    chunks = o_ref.shape[1] // LANES
    n_rows = nr_ref[b]
    group = ROW_GROUP

    @pl.when(b == 0)
    def _init():
        ys[...] = jnp.zeros(ys.shape, F32)

    @pl.when(n_rows > 0)
    def _block():
        for j in range(chunks):
            ot[j * ROW_STRIDE:j * ROW_STRIDE + MOE_BLOCK, :] = o_ref[:, j * LANES:(j + 1) * LANES].astype(F32)
        for seg in range(MOE_BLOCK // ROW_SEGMENT):
            @pl.when(n_rows > seg * ROW_SEGMENT)
            def _scatter(seg=seg):
                for r0 in range(seg * ROW_SEGMENT, (seg + 1) * ROW_SEGMENT, group):
                    dst = [pl.multiple_of(off_ref[r0 + i], SUBLANES) for i in range(group)]
                    vals = [ys[pl.ds(dst[i], SUBLANES), :]
                            + w_ref[r0 + i] * ot[pl.ds(r0 + i, chunks, stride=ROW_STRIDE), :]
                            for i in range(group)]
                    for i in range(group):
                        ys[pl.ds(dst[i], SUBLANES), :] = vals[i]

    @pl.when(b == n_blocks - 1)
    def _store():
        cp = pltpu.make_async_copy(ys.at[pl.ds(0, rows)], y_hbm, sem)
        cp.start()
        cp.wait()


def _routed_experts(h2_rows, block_e, row_start, n_rows, last_block, sorted_tok, sorted_w, w_gate, w_up, w_down):
    rows, lanes = h2_rows.shape
    t = rows // SUBLANES
    n_blocks = block_e.shape[0]
    d, e_dim = w_gate.shape[1], w_gate.shape[2]
    assert d == SUBLANES * LANES and lanes == LANES
    offsets, weights = _block_lists(row_start, n_rows, sorted_tok, sorted_w, t)
    staging = pltpu.VMEM((SUBLANES * ROW_STRIDE, LANES), F32)
    resident = pltpu.VMEM((rows + SUBLANES, LANES), F32)
    expert_out = pl.pallas_call(
        _expert_kernel,
        grid_spec=pltpu.PrefetchScalarGridSpec(
            num_scalar_prefetch=3,
            grid=(n_blocks,),
            in_specs=[
                pl.BlockSpec(memory_space=pl.ANY),
                pl.BlockSpec(memory_space=pl.ANY),
                pl.BlockSpec(memory_space=pl.ANY),
                pl.BlockSpec(memory_space=pl.ANY),
                pl.BlockSpec((LIST_LEN,), lambda b, be, nr, last: (jnp.minimum(b, last[0]),),
                             memory_space=pltpu.SMEM),
            ],
            out_specs=pl.BlockSpec((MOE_BLOCK, d), lambda b, be, nr, last: (jnp.minimum(b, last[0]), 0)),
            scratch_shapes=[
                resident,
                staging,
                pltpu.VMEM((WEIGHT_BUFFERS, d, e_dim), F32),
                pltpu.VMEM((WEIGHT_BUFFERS, d, e_dim), F32),
                pltpu.VMEM((WEIGHT_BUFFERS, e_dim, d), F32),
                pltpu.SemaphoreType.DMA(()),
                pltpu.SemaphoreType.DMA((WEIGHT_BUFFERS,)),
            ],
        ),
        out_shape=jax.ShapeDtypeStruct((n_blocks * MOE_BLOCK, d), BF16),
        compiler_params=_params(("arbitrary",)),
    )(block_e, n_rows, last_block, h2_rows, w_gate, w_up, w_down, offsets)
    list_spec = pl.BlockSpec((LIST_LEN,), lambda b, nr, last: (jnp.minimum(b, last[0]),), memory_space=pltpu.SMEM)
    return pl.pallas_call(
        functools.partial(_combine_kernel, t=t, n_blocks=n_blocks),
        grid_spec=pltpu.PrefetchScalarGridSpec(
            num_scalar_prefetch=2,
            grid=(n_blocks,),
            in_specs=[pl.BlockSpec((MOE_BLOCK, d), lambda b, nr, last: (jnp.minimum(b, last[0]), 0)),
                      list_spec, list_spec],
            out_specs=pl.BlockSpec(memory_space=pl.ANY),
            scratch_shapes=[resident, staging, pltpu.SemaphoreType.DMA(())],
        ),
        out_shape=jax.ShapeDtypeStruct((rows, LANES), F32),
        compiler_params=_params(("arbitrary",)),
    )(n_rows, last_block, expert_out, offsets, weights)


def _final_kernel(x1_ref, h2_ref, routed_ref, mod_ref, wg_ref, wu_ref, wd_ref, g_ref, b_ref,
                  yc_ref, yl_ref, st_ref, wgb_ref, wub_ref, wdb_ref, *, d, n_ctx_tiles):
    for src, dst in ((wg_ref, wgb_ref), (wu_ref, wub_ref), (wd_ref, wdb_ref)):
        _cast_weight_once(src, dst)
    is_ctx = pl.program_id(0) < n_ctx_tiles
    gate2 = mod_ref[0, :, 5 * d:6 * d]
    h = h2_ref[...]
    gate = jnp.dot(h, wgb_ref[...], preferred_element_type=F32)
    up = jnp.dot(h, wub_ref[...], preferred_element_type=F32)
    shared = jnp.dot((_silu(gate) * up).astype(BF16), wdb_ref[...], preferred_element_type=F32)
    ffn = _slabs_to_rows(routed_ref, st_ref, h.shape[0]) + shared
    y = _layer_norm(DEEPNORM_ALPHA * x1_ref[...] + gate2 * ffn, g_ref[...], b_ref[...])

    @pl.when(is_ctx)
    def _():
        yc_ref[...] = y

    @pl.when(jnp.logical_not(is_ctx))
    def _():
        yl_ref[...] = y


def _final(x1, h2, routed, mod3, w_g, w_u, w_d, ln_g, ln_b, mod_map, t_ctx):
    t, d = x1.shape
    t_lat = t - t_ctx
    n_ctx_tiles = t_ctx // TOKEN_TILE
    tm = TOKEN_TILE
    row = lambda i: (i, 0)
    full = lambda i: (0, 0)
    return pl.pallas_call(
        functools.partial(_final_kernel, d=d, n_ctx_tiles=n_ctx_tiles),
        grid=(t // tm,),
        in_specs=[
            pl.BlockSpec((tm, d), row),
            pl.BlockSpec((tm, d), row),
            pl.BlockSpec((tm * SUBLANES, LANES), row),
            pl.BlockSpec((1, 1, mod3.shape[2]), mod_map),
            _resident(w_g.shape),
            _resident(w_u.shape),
            _resident(w_d.shape),
            pl.BlockSpec((1, d), full),
            pl.BlockSpec((1, d), full),
        ],
        out_specs=_pair_specs(d, n_ctx_tiles),
        out_shape=[jax.ShapeDtypeStruct((t_ctx, d), F32), jax.ShapeDtypeStruct((t_lat, d), F32)],
        scratch_shapes=[pltpu.VMEM((SUBLANES * TILE_STRIDE, LANES), F32),
                        pltpu.VMEM(w_g.shape, BF16), pltpu.VMEM(w_u.shape, BF16), pltpu.VMEM(w_d.shape, BF16)],
        compiler_params=_params(("arbitrary",)),
    )(x1, h2, routed, mod3, w_g, w_u, w_d, ln_g, ln_b)


def kernel(x_prompt, x_sample, cache_k, cache_v, state_hgrn, c, c_ctx, w_mod, b_mod, w_in, hg_lb, hg_norm_g, q_norm_g, k_norm_g, w_branch_a, w_branch_b, w_out, ln1_g, ln1_b, w_router, router_bias, w_e_gate, w_e_up, w_e_down, w_s_gate, w_s_up, w_s_down, ln2_g, ln2_b):
    assert w_mod.shape[0] == DEPTH
    n_ctx, seq_ctx, d = x_prompt.shape
    n_lat, seq_lat, _ = x_sample.shape
    t_ctx = n_ctx * seq_ctx
    t_lat = n_lat * seq_lat
    assert seq_ctx == TOKEN_TILE and seq_lat % TOKEN_TILE == 0
    layer = 0

    lb = jnp.cumsum(jax.nn.softmax(hg_lb.astype(F32), axis=0), axis=0)[layer]

    cond = jnp.concatenate([c_ctx[None, :], c], axis=0)
    cond = jnp.pad(cond, ((0, (-cond.shape[0]) % SUBLANES), (0, 0)))
    mod = _modulation(cond, w_mod[layer], b_mod[layer][None, :])
    mod3 = mod.reshape(mod.shape[0], 1, mod.shape[1])
    mod_map = _mod_row_map(t_ctx // TOKEN_TILE, seq_lat // TOKEN_TILE)

    x_ctx = x_prompt.reshape(t_ctx, d)
    x_lat = x_sample.reshape(t_lat, d)
    zh, zf, za, zg = _input_projection(x_ctx, x_lat, mod3, w_in[layer], lb, mod_map)

    norm_g = hg_norm_g[layer][None, :]
    oa_ctx, s_ctx = _hgrn_scan(zh, zf, norm_g, None, batch=n_ctx, seq=seq_ctx, row_block0=0,
                               heads=HG_HEADS_PER_STEP_CTX)
    oa_lat, _ = _hgrn_scan(zh, zf, norm_g, state_hgrn[:, layer], batch=n_lat, seq=seq_lat,
                           row_block0=t_ctx // seq_lat, heads=HG_HEADS_PER_STEP_LATENT)

    q_gain = jnp.tile(q_norm_g[layer], N_HEADS)[None, :]
    k_gain = jnp.tile(k_norm_g[layer], N_KV_HEADS)[None, :]
    lane = jnp.arange(ATT_WIDTH)
    group_ones = (lane[:, None] // HEAD_DIM == lane[None, :] // HEAD_DIM).astype(BF16)
    ob_ctx, k_ctx, v_ctx = _attention(za, q_gain, k_gain, group_ones, batch=n_ctx, seq=seq_ctx, row_block0=0)
    past = cache_k.shape[2]
    ob_lat = _attention(
        za, q_gain, k_gain, group_ones, batch=n_lat, seq=seq_lat, row_block0=t_ctx // seq_lat,
        rope=_rope_tables(seq_lat),
        cache=(cache_k[:, layer].reshape(n_lat, past, KV_WIDTH), cache_v[:, layer].reshape(n_lat, past, KV_WIDTH)))

    wr_t = w_router[layer].T
    wr_hi = wr_t.astype(BF16)
    wr_lo = (wr_t - wr_hi.astype(F32)).astype(BF16)
    x1, h2, h2_slabs, idx_t, w_t = _post_mixer(
        x_ctx, x_lat, mod3, oa_ctx, oa_lat, ob_ctx, ob_lat, zg, w_branch_a[layer], w_branch_b[layer], w_out[layer],
        ln1_g[layer][None, :], ln1_b[layer][None, :], wr_hi, wr_lo, router_bias[layer][:, None], mod_map)

    routed = _routed_experts(h2_slabs, *_dispatch_lists(idx_t, w_t), w_e_gate[layer], w_e_up[layer], w_e_down[layer])

    y_ctx, y_lat = _final(x1, h2, routed, mod3, w_s_gate[layer], w_s_up[layer], w_s_down[layer],
                          ln2_g[layer][None, :], ln2_b[layer][None, :], mod_map, t_ctx)

    y_prompt = y_ctx.reshape(n_ctx, seq_ctx, d)
    y_sample = y_lat.reshape(n_lat, seq_lat, d)
    new_cache_k = k_ctx.reshape(n_ctx, 1, seq_ctx, N_KV_HEADS, HEAD_DIM)
    new_cache_v = v_ctx.reshape(n_ctx, 1, seq_ctx, N_KV_HEADS, HEAD_DIM)
    new_state = s_ctx[:, None]
    return (y_prompt, y_sample, new_cache_k, new_cache_v, new_state)
```

```python
import functools

import jax
import jax.numpy as jnp
from jax import lax
from jax.experimental import pallas as pl
from jax.experimental.pallas import tpu as pltpu

F32 = jnp.float32
BF16 = jnp.bfloat16

GRID_W = 64
HG_HEADS = 4
HG_DK = 128
HG_DV = 128
HG_WIDTH = HG_HEADS * HG_DK
N_HEADS = 8
N_KV_HEADS = 2
HEAD_DIM = 64
ATT_WIDTH = N_HEADS * HEAD_DIM
KV_WIDTH = N_KV_HEADS * HEAD_DIM
ROPE_THETA = 10000.0
N_EXPERTS = 256
TOP_K = 8
N_GROUPS = 8
TOPK_GROUPS = 4
GROUP_SIZE = N_EXPERTS // N_GROUPS
ROUTED_SCALE = 2.5
NORM_EPS = 1e-6
DEPTH = 1
DEEPNORM_ALPHA = (2 * DEPTH) ** 0.25

LANES = 128
SUBLANES = 8
VMEM_LIMIT = 56 * 1024 * 1024

CAST_CHUNK = 512
TOKEN_TILE = 256
TILE_STRIDE = TOKEN_TILE + 1
HG_CHUNK = 32
HG_HEADS_PER_STEP_CTX = 4
HG_HEADS_PER_STEP_LATENT = 2
ATT_Q_BLOCK = 128
MOE_BLOCK = 320
ROW_STRIDE = MOE_BLOCK + 1
LIST_LEN = 512
LIST_ROWS = LIST_LEN // LANES
ROW_GROUP = 8
ROW_SEGMENT = 32
BLOCKS_PER_STEP = 2
WEIGHT_BUFFERS = 3


def _sigmoid(x):
    return 1.0 / (1.0 + jnp.exp(-x))


def _silu(x):
    return x * _sigmoid(x)


def _params(sem=None):
    return pltpu.CompilerParams(dimension_semantics=sem, vmem_limit_bytes=VMEM_LIMIT)


def _resident(shape):
    return pl.BlockSpec(shape, lambda i: (0,) * len(shape), pipeline_mode=pl.Buffered(1))


def _cast_weight_once(src_ref, dst_ref):
    @pl.when(pl.program_id(0) == 0)
    def _():
        width = src_ref.shape[1]
        for lo in range(0, width, CAST_CHUNK):
            hi = min(lo + CAST_CHUNK, width)
            dst_ref[:, lo:hi] = src_ref[:, lo:hi].astype(dst_ref.dtype)


def _mod_kernel(c_ref, w_ref, b_ref, o_ref):
    s = _silu(c_ref[...]).astype(BF16)
    o_ref[...] = jnp.dot(s, w_ref[...].astype(BF16), preferred_element_type=F32) + b_ref[...]


def _modulation(cond, w_mod, b_mod):
    n, d = cond.shape
    width = w_mod.shape[1]
    tn = width // 4
    return pl.pallas_call(
        _mod_kernel,
        grid=(4,),
        in_specs=[
            pl.BlockSpec((n, d), lambda j: (0, 0)),
            pl.BlockSpec((d, tn), lambda j: (0, j)),
            pl.BlockSpec((1, tn), lambda j: (0, j)),
        ],
        out_specs=pl.BlockSpec((n, tn), lambda j: (0, j)),
        out_shape=jax.ShapeDtypeStruct((n, width), F32),
        compiler_params=_params(("arbitrary",)),
    )(cond, w_mod, b_mod)


def _inproj_kernel(xc_ref, xl_ref, mod_ref, w_ref, lb_ref, zh_ref, zf_ref, za_ref, zg_ref, wb_ref,
                   *, d, n_ctx_tiles):
    _cast_weight_once(w_ref, wb_ref)
    shift = mod_ref[0, :, 0:d]
    scale = mod_ref[0, :, d:2 * d]
    x = _pick(pl.program_id(0) < n_ctx_tiles, xc_ref, xl_ref)
    h = (x * (1.0 + scale) + shift).astype(BF16)

    def proj(lo, hi):
        return jnp.dot(h, wb_ref[:, lo:hi], preferred_element_type=F32)

    w = HG_WIDTH
    zh_ref[:, 0:w] = _silu(proj(0, w)).astype(zh_ref.dtype)
    zh_ref[:, w:2 * w] = proj(w, 2 * w).astype(zh_ref.dtype)
    for i in range(2):
        lb = lb_ref[i:i + 1, :]
        zf_ref[:, i * w:(i + 1) * w] = lb + (1.0 - lb) * _sigmoid(proj((2 + i) * w, (3 + i) * w))
    zh_ref[:, 2 * w:3 * w] = _silu(proj(4 * w, 5 * w)).astype(zh_ref.dtype)
    a0 = 5 * w
    a1 = a0 + ATT_WIDTH + 2 * KV_WIDTH
    za_ref[...] = proj(a0, a1)
    for i in range(4):
        lo = a1 + i * (d // 2)
        zg_ref[:, i * (d // 2):(i + 1) * (d // 2)] = _sigmoid(proj(lo, lo + d // 2)).astype(zg_ref.dtype)


def _mod_row_map(n_ctx_tiles, tiles_per_latent):
    def index_map(i):
        row = jnp.where(i < n_ctx_tiles, 0, 1 + (i - n_ctx_tiles) // tiles_per_latent)
        return (row, 0, 0)
    return index_map


def _pair_specs(width, n_ctx_tiles):
    return [pl.BlockSpec((TOKEN_TILE, width), lambda i: (jnp.minimum(i, n_ctx_tiles - 1), 0)),
            pl.BlockSpec((TOKEN_TILE, width), lambda i: (jnp.maximum(i - n_ctx_tiles, 0), 0))]


def _pick(is_ctx, ctx_ref, lat_ref):
    return jnp.where(is_ctx, ctx_ref[...], lat_ref[...])


def _input_projection(x_ctx, x_lat, mod3, w_in, lb, mod_map):
    d = x_ctx.shape[1]
    n_ctx_tiles = x_ctx.shape[0] // TOKEN_TILE
    t = x_ctx.shape[0] + x_lat.shape[0]
    width = w_in.shape[1]
    zh_w = 3 * HG_WIDTH
    zf_w = 2 * HG_WIDTH
    za_w = ATT_WIDTH + 2 * KV_WIDTH
    zg_w = 2 * d
    assert width == zh_w + zf_w + za_w + zg_w
    tm = TOKEN_TILE
    return pl.pallas_call(
        functools.partial(_inproj_kernel, d=d, n_ctx_tiles=n_ctx_tiles),
        grid=(t // tm,),
        in_specs=_pair_specs(d, n_ctx_tiles) + [
            pl.BlockSpec((1, 1, mod3.shape[2]), mod_map),
            _resident((d, width)),
            pl.BlockSpec((2, HG_WIDTH), lambda i: (0, 0)),
        ],
        out_specs=[
            pl.BlockSpec((tm, zh_w), lambda i: (i, 0)),
            pl.BlockSpec((tm, zf_w), lambda i: (i, 0)),
            pl.BlockSpec((tm, za_w), lambda i: (i, 0)),
            pl.BlockSpec((tm, zg_w), lambda i: (i, 0)),
        ],
        out_shape=[
            jax.ShapeDtypeStruct((t, zh_w), BF16),
            jax.ShapeDtypeStruct((t, zf_w), F32),
            jax.ShapeDtypeStruct((t, za_w), F32),
            jax.ShapeDtypeStruct((t, zg_w), BF16),
        ],
        scratch_shapes=[pltpu.VMEM((d, width), BF16)],
        compiler_params=_params(("arbitrary",)),
    )(x_ctx, x_lat, mod3, w_in, lb)


def _hgrn_kernel(*refs, seq, has_state, heads):
    if has_state:
        q_ref, v_ref, ff_ref, fb_ref, gs_ref, ng_ref, s0_ref, o_ref, sout_ref = refs
    else:
        q_ref, v_ref, ff_ref, fb_ref, gs_ref, ng_ref, o_ref, sout_ref = refs
        s0_ref = None
    for hd in range(heads):
        _hgrn_head(q_ref, v_ref, ff_ref, fb_ref, gs_ref, ng_ref, s0_ref, o_ref, sout_ref, hd, seq)


def _hgrn_head(q_ref, v_ref, ff_ref, fb_ref, gs_ref, ng_ref, s0_ref, o_ref, sout_ref, hd, seq):
    has_state = s0_ref is not None
    cols = slice(hd * HG_DK, (hd + 1) * HG_DK)
    c = HG_CHUNK
    n = seq // c
    q3 = q_ref[:, cols].astype(F32).reshape(n, c, HG_DK)
    v3 = v_ref[:, cols].reshape(n, c, HG_DV)
    pos = lax.broadcasted_iota(jnp.int32, (seq, HG_DK), 0) % c
    t_idx = lax.broadcasted_iota(jnp.int32, (c, c), 0)
    s_idx = lax.broadcasted_iota(jnp.int32, (c, c), 1)
    o_sum = None
    for direction, f_ref in enumerate((ff_ref, fb_ref)):
        reverse = direction == 1
        f = f_ref[:, cols]
        k3 = (1.0 - f).reshape(n, c, HG_DK)
        b = jnp.log(f)
        step = 1
        while step < c:
            if reverse:
                b = b + jnp.where(pos < c - step, pltpu.roll(b, seq - step, axis=0), 0.0)
            else:
                b = b + jnp.where(pos >= step, pltpu.roll(b, step, axis=0), 0.0)
            step *= 2
        b3 = b.reshape(n, c, HG_DK)
        edge = b3[:, 0:1, :] if reverse else b3[:, c - 1:c, :]
        mid = b3[:, c // 2:c // 2 + 1, :]
        q_rel = q3 * jnp.exp(b3 - mid)
        k_rel = k3 * jnp.exp(mid - b3)
        q_mid = q_rel.astype(BF16)
        k_mid = k_rel.astype(BF16)
        q_in = (q_rel * jnp.exp(mid)).astype(BF16)
        k_end = (k_rel * jnp.exp(edge - mid)).astype(BF16)
        scores = jnp.einsum('ntd,nsd->nts', q_mid, k_mid, preferred_element_type=F32)
        keep = (s_idx >= t_idx) if reverse else (s_idx <= t_idx)
        scores = jnp.where(keep[None], scores, 0.0).astype(BF16)
        o_intra = jnp.einsum('nts,nsv->ntv', scores, v3, preferred_element_type=F32)
        upd = jnp.einsum('nsv,nsd->nvd', v3, k_end, preferred_element_type=F32)
        dec = jnp.exp(edge)
        if has_state:
            st = s0_ref[0, direction, hd].T
        else:
            st = jnp.zeros((HG_DV, HG_DK), F32)
        before = [None] * n
        for ci in (range(n - 1, -1, -1) if reverse else range(n)):
            before[ci] = st.astype(BF16)
            st = st * dec[ci] + upd[ci]
        st_before = jnp.stack(before, axis=0)
        o_inter = jnp.einsum('ntd,nvd->ntv', q_in, st_before, preferred_element_type=F32)
        o_dir = (o_intra + o_inter).reshape(seq, HG_DV)
        o_sum = o_dir if o_sum is None else o_sum + o_dir
        sout_ref[0, direction, hd] = st.T
    ms = jnp.mean(o_sum * o_sum, axis=-1, keepdims=True)
    o = o_sum * lax.rsqrt(ms + NORM_EPS) * ng_ref[...]
    o_ref[:, cols] = (o * gs_ref[:, cols].astype(F32)).astype(o_ref.dtype)


def _hgrn_scan(zh, zf, norm_g, s0, *, batch, seq, row_block0, heads):
    has_state = s0 is not None
    h = HG_HEADS
    groups = h // heads
    width = heads * HG_DK

    def col(section):
        return pl.BlockSpec((seq, width), lambda b, j: (row_block0 + b, section * groups + j))

    in_specs = [col(0), col(1), col(0), col(1), col(2), pl.BlockSpec((1, HG_DV), lambda b, j: (0, 0))]
    args = [zh, zh, zf, zf, zh, norm_g]
    state_spec = pl.BlockSpec((1, 2, heads, HG_DK, HG_DV), lambda b, j: (b, 0, j, 0, 0))
    if has_state:
        in_specs.append(state_spec)
        args.append(s0)
    return pl.pallas_call(
        functools.partial(_hgrn_kernel, seq=seq, has_state=has_state, heads=heads),
        grid=(batch, groups),
        in_specs=in_specs,
        out_specs=[pl.BlockSpec((seq, width), lambda b, j: (b, j)), state_spec],
        out_shape=[
            jax.ShapeDtypeStruct((batch * seq, h * HG_DV), BF16),
            jax.ShapeDtypeStruct((batch, 2, h, HG_DK, HG_DV), F32),
        ],
        compiler_params=_params(("arbitrary", "arbitrary")),
    )(*args)


def _group_rms_norm(x, group_ones, gain):
    sq = x * x
    hi = sq.astype(BF16)
    lo = (sq - hi.astype(F32)).astype(BF16)
    total = (jnp.dot(hi, group_ones, preferred_element_type=F32)
             + jnp.dot(lo, group_ones, preferred_element_type=F32))
    return x * lax.rsqrt(total * (1.0 / HEAD_DIM) + NORM_EPS) * gain


def _rope(x, cos, sin_signed):
    width = x.shape[1]
    quarter = HEAD_DIM // 4
    lane = lax.broadcasted_iota(jnp.int32, x.shape, 1)
    partner = jnp.where(lane % (2 * quarter) < quarter,
                        pltpu.roll(x, width - quarter, axis=1),
                        pltpu.roll(x, quarter, axis=1))
    return x * cos + partner * sin_signed


def _attn_kernel(*refs, seq, latent):
    if latent:
        za_ref, qg_ref, kg_ref, gm_ref, cos_ref, sin_ref, ck_ref, cv_ref, o_ref = refs
    else:
        za_ref, qg_ref, kg_ref, gm_ref, o_ref, kout_ref, vout_ref = refs
    q = za_ref[:, 0:ATT_WIDTH]
    k = za_ref[:, ATT_WIDTH:ATT_WIDTH + KV_WIDTH]
    v = za_ref[:, ATT_WIDTH + KV_WIDTH:ATT_WIDTH + 2 * KV_WIDTH]
    qn = _group_rms_norm(q, gm_ref[...], qg_ref[...])
    kn = _group_rms_norm(k, gm_ref[0:KV_WIDTH, 0:KV_WIDTH], kg_ref[...])
    if latent:
        qn = _rope(qn, cos_ref[...], sin_ref[...])
        kr = _rope(kn, cos_ref[:, 0:KV_WIDTH], sin_ref[:, 0:KV_WIDTH])
    else:
        kout_ref[...] = kn
        vout_ref[...] = v
        kr = kn
    qb = (qn * (HEAD_DIM ** -0.5)).astype(BF16)
    kb = kr.astype(BF16)
    vb = v.astype(BF16)
    if latent:
        ckb = ck_ref[0].astype(BF16)
        cvb = cv_ref[0].astype(BF16)
    groups = N_HEADS // N_KV_HEADS
    tq = ATT_Q_BLOCK if latent else seq
    nt = (((1,), (1,)), ((), ()))
    for kh in range(N_KV_HEADS):
        ksl = slice(kh * HEAD_DIM, (kh + 1) * HEAD_DIM)
        k_new = kb[:, ksl]
        v_new = vb[:, ksl]
        for blk in range(seq // tq):
            rows = slice(blk * tq, (blk + 1) * tq)
            heads = [kh * groups + g for g in range(groups)]
            q_st = jnp.concatenate([qb[rows, hd * HEAD_DIM:(hd + 1) * HEAD_DIM] for hd in heads], axis=0)
            s_new = lax.dot_general(q_st, k_new, nt, preferred_element_type=F32)
            m = jnp.max(s_new, axis=-1, keepdims=True)
            if latent:
                s_ctx = lax.dot_general(q_st, ckb[:, ksl], nt, preferred_element_type=F32)
                m = jnp.maximum(m, jnp.max(s_ctx, axis=-1, keepdims=True))
            p_new = jnp.exp(s_new - m)
            denom = jnp.sum(p_new, axis=-1, keepdims=True)
            acc = jnp.dot(p_new.astype(BF16), v_new, preferred_element_type=F32)
            if latent:
                p_ctx = jnp.exp(s_ctx - m)
                denom = denom + jnp.sum(p_ctx, axis=-1, keepdims=True)
                acc = acc + jnp.dot(p_ctx.astype(BF16), cvb[:, ksl], preferred_element_type=F32)
            out = acc / denom
            for g in range(0, groups, 2):
                pair = jnp.concatenate([out[g * tq:(g + 1) * tq], out[(g + 1) * tq:(g + 2) * tq]], axis=1)
                lane0 = heads[g] * HEAD_DIM
                o_ref[rows, lane0:lane0 + 2 * HEAD_DIM] = pair.astype(o_ref.dtype)


def _attention(za, q_gain, k_gain, group_ones, *, batch, seq, row_block0, rope=None, cache=None):
    latent = cache is not None
    za_w = za.shape[1]
    in_specs = [
        pl.BlockSpec((seq, za_w), lambda b: (row_block0 + b, 0)),
        pl.BlockSpec((1, ATT_WIDTH), lambda b: (0, 0)),
        pl.BlockSpec((1, KV_WIDTH), lambda b: (0, 0)),
        pl.BlockSpec((ATT_WIDTH, ATT_WIDTH), lambda b: (0, 0)),
    ]
    args = [za, q_gain, k_gain, group_ones]
    o_spec = pl.BlockSpec((seq, ATT_WIDTH), lambda b: (b, 0))
    o_shape = jax.ShapeDtypeStruct((batch * seq, ATT_WIDTH), BF16)
    if latent:
        cos, sin_signed = rope
        ck, cv = cache
        past = ck.shape[1]
        in_specs += [
            pl.BlockSpec((seq, ATT_WIDTH), lambda b: (0, 0)),
            pl.BlockSpec((seq, ATT_WIDTH), lambda b: (0, 0)),
            pl.BlockSpec((1, past, KV_WIDTH), lambda b: (b, 0, 0)),
            pl.BlockSpec((1, past, KV_WIDTH), lambda b: (b, 0, 0)),
        ]
        args += [cos, sin_signed, ck, cv]
        out_specs = o_spec
        out_shape = o_shape
    else:
        kv_spec = pl.BlockSpec((seq, KV_WIDTH), lambda b: (b, 0))
        kv_shape = jax.ShapeDtypeStruct((batch * seq, KV_WIDTH), F32)
        out_specs = [o_spec, kv_spec, kv_spec]
        out_shape = [o_shape, kv_shape, kv_shape]
    return pl.pallas_call(
        functools.partial(_attn_kernel, seq=seq, latent=latent),
        grid=(batch,),
        in_specs=in_specs,
        out_specs=out_specs,
        out_shape=out_shape,
        compiler_params=_params(("arbitrary",)),
    )(*args)


def _rope_tables(seq):
    quarter = HEAD_DIM // 4
    t = jnp.arange(seq)
    row = (t // GRID_W).astype(F32)
    colp = (t % GRID_W).astype(F32)
    inv_freq = ROPE_THETA ** (-jnp.arange(quarter, dtype=F32) / quarter)
    lane = jnp.arange(HEAD_DIM)
    pos = jnp.where((lane < HEAD_DIM // 2)[None, :], row[:, None], colp[:, None])
    ang = pos * inv_freq[lane % quarter][None, :]
    sign = jnp.where(lane % (2 * quarter) < quarter, -1.0, 1.0)[None, :]
    cos = jnp.tile(jnp.cos(ang), (1, N_HEADS))
    sin_signed = jnp.tile(jnp.sin(ang) * sign, (1, N_HEADS))
    return cos, sin_signed


def _layer_norm(y, g, b):
    mu = jnp.mean(y, axis=-1, keepdims=True)
    yc = y - mu
    var = jnp.mean(yc * yc, axis=-1, keepdims=True)
    return yc * lax.rsqrt(var + NORM_EPS) * g + b


def _rows_to_slabs(tile, st_ref, slab_ref):
    rows, d = tile.shape
    chunks = d // LANES
    for j in range(chunks):
        st_ref[j * TILE_STRIDE:j * TILE_STRIDE + rows, :] = tile[:, j * LANES:(j + 1) * LANES]
    for r in range(rows):
        slab_ref[r * chunks:(r + 1) * chunks, :] = st_ref[pl.ds(r, chunks, stride=TILE_STRIDE), :]


def _slabs_to_rows(slab_ref, st_ref, rows):
    chunks = SUBLANES
    for r in range(rows):
        st_ref[pl.ds(r, chunks, stride=TILE_STRIDE), :] = slab_ref[r * chunks:(r + 1) * chunks, :]
    return jnp.concatenate([st_ref[j * TILE_STRIDE:j * TILE_STRIDE + rows, :] for j in range(chunks)], axis=1)


def _post_mixer_kernel(xc_ref, xl_ref, mod_ref, oac_ref, oal_ref, obc_ref, obl_ref, zg_ref, wa_ref, wb_ref,
                       wo_ref, g_ref, b_ref, wrh_ref, wrl_ref, rb_ref,
                       x1_ref, h2_ref, slab_ref, idx_ref, rw_ref, st_ref, wab_ref, wbb_ref, wob_ref,
                       *, d, n_ctx_tiles):
    for src, dst in ((wa_ref, wab_ref), (wb_ref, wbb_ref), (wo_ref, wob_ref)):
        _cast_weight_once(src, dst)
    is_ctx = pl.program_id(0) < n_ctx_tiles
    gate1 = mod_ref[0, :, 2 * d:3 * d]
    shift2 = mod_ref[0, :, 3 * d:4 * d]
    scale2 = mod_ref[0, :, 4 * d:5 * d]
    branch_a = jnp.dot(_pick(is_ctx, oac_ref, oal_ref), wab_ref[...], preferred_element_type=F32)
    branch_b = jnp.dot(_pick(is_ctx, obc_ref, obl_ref), wbb_ref[...], preferred_element_type=F32)
    merged = zg_ref[:, 0:d].astype(F32) * branch_a + zg_ref[:, d:2 * d].astype(F32) * branch_b
    mix = jnp.dot(merged.astype(BF16), wob_ref[...], preferred_element_type=F32)
    x = _pick(is_ctx, xc_ref, xl_ref)
    x1 = _layer_norm(DEEPNORM_ALPHA * x + gate1 * mix, g_ref[...], b_ref[...])
    x1_ref[...] = x1
    h2 = x1 * (1.0 + scale2) + shift2
    h2_ref[...] = h2.astype(h2_ref.dtype)
    _rows_to_slabs(h2, st_ref, slab_ref)
    idx_ref[...], rw_ref[...] = _route_tile(h2, wrh_ref[...], wrl_ref[...], rb_ref[...])


def _post_mixer(x_ctx, x_lat, mod3, oa_ctx, oa_lat, ob_ctx, ob_lat, zg, w_a, w_b, w_o, ln_g, ln_b,
                wr_hi, wr_lo, router_bias, mod_map):
    d = x_ctx.shape[1]
    n_ctx_tiles = x_ctx.shape[0] // TOKEN_TILE
    t = x_ctx.shape[0] + x_lat.shape[0]
    tm = TOKEN_TILE
    row = lambda i: (i, 0)
    full = lambda i: (0, 0)
    return pl.pallas_call(
        functools.partial(_post_mixer_kernel, d=d, n_ctx_tiles=n_ctx_tiles),
        grid=(t // tm,),
        in_specs=_pair_specs(d, n_ctx_tiles) + [
            pl.BlockSpec((1, 1, mod3.shape[2]), mod_map),
        ] + _pair_specs(oa_ctx.shape[1], n_ctx_tiles) + _pair_specs(ob_ctx.shape[1], n_ctx_tiles) + [
            pl.BlockSpec((tm, 2 * d), row),
            _resident(w_a.shape),
            _resident(w_b.shape),
            _resident(w_o.shape),
            pl.BlockSpec((1, d), full),
            pl.BlockSpec((1, d), full),
            pl.BlockSpec((N_EXPERTS, d), full),
            pl.BlockSpec((N_EXPERTS, d), full),
            pl.BlockSpec((N_EXPERTS, 1), full),
        ],
        out_specs=[pl.BlockSpec((tm, d), row), pl.BlockSpec((tm, d), row),
                   pl.BlockSpec((tm * SUBLANES, LANES), row),
                   pl.BlockSpec((TOP_K, tm), lambda i: (0, i)), pl.BlockSpec((TOP_K, tm), lambda i: (0, i))],
        out_shape=[jax.ShapeDtypeStruct((t, d), F32), jax.ShapeDtypeStruct((t, d), BF16),
                   jax.ShapeDtypeStruct((t * SUBLANES, LANES), F32),
                   jax.ShapeDtypeStruct((TOP_K, t), jnp.int32), jax.ShapeDtypeStruct((TOP_K, t), F32)],
        scratch_shapes=[pltpu.VMEM((SUBLANES * TILE_STRIDE, LANES), F32),
                        pltpu.VMEM(w_a.shape, BF16), pltpu.VMEM(w_b.shape, BF16), pltpu.VMEM(w_o.shape, BF16)],
        compiler_params=_params(("arbitrary",)),
    )(x_ctx, x_lat, mod3, oa_ctx, oa_lat, ob_ctx, ob_lat, zg, w_a, w_b, w_o, ln_g, ln_b, wr_hi, wr_lo, router_bias)


def _first_index_of_max(x, idx, sentinel):
    m = jnp.max(x, axis=0, keepdims=True)
    first = jnp.min(jnp.where(x == m, idx, sentinel), axis=0, keepdims=True)
    return m, first


def _route_tile(h, wh, wl, bias):
    hh = h.astype(BF16)
    hl = (h - hh.astype(F32)).astype(BF16)
    nt = (((1,), (1,)), ((), ()))
    logits = (lax.dot_general(wh, hh, nt, preferred_element_type=F32)
              + lax.dot_general(wh, hl, nt, preferred_element_type=F32)
              + lax.dot_general(wl, hh, nt, preferred_element_type=F32))
    scores = _sigmoid(logits)
    sel = scores + bias
    tm = sel.shape[1]
    neg = -jnp.inf
    gidx = lax.broadcasted_iota(jnp.int32, (GROUP_SIZE, tm), 0)
    group_scores = []
    for g in range(N_GROUPS):
        sg = sel[g * GROUP_SIZE:(g + 1) * GROUP_SIZE, :]
        m1, first = _first_index_of_max(sg, gidx, GROUP_SIZE)
        m2 = jnp.max(jnp.where(gidx == first, neg, sg), axis=0, keepdims=True)
        group_scores.append(m1 + m2)
    gs = jnp.concatenate(group_scores, axis=0)
    nidx = lax.broadcasted_iota(jnp.int32, (N_GROUPS, tm), 0)
    chosen = jnp.zeros((N_GROUPS, tm), jnp.bool_)
    for _ in range(TOPK_GROUPS):
        _, first = _first_index_of_max(gs, nidx, N_GROUPS)
        hit = nidx == first
        chosen = jnp.logical_or(chosen, hit)
        gs = jnp.where(hit, neg, gs)
    masked = jnp.concatenate(
        [jnp.where(chosen[g:g + 1, :], sel[g * GROUP_SIZE:(g + 1) * GROUP_SIZE, :], neg) for g in range(N_GROUPS)],
        axis=0)
    eidx = lax.broadcasted_iota(jnp.int32, (N_EXPERTS, tm), 0)
    picks, weights = [], []
    for _ in range(TOP_K):
        _, first = _first_index_of_max(masked, eidx, N_EXPERTS)
        hit = eidx == first
        picks.append(first)
        weights.append(jnp.sum(jnp.where(hit, scores, 0.0), axis=0, keepdims=True))
        masked = jnp.where(hit, neg, masked)
    wk = jnp.concatenate(weights, axis=0)
    return jnp.concatenate(picks, axis=0), wk / jnp.sum(wk, axis=0, keepdims=True) * ROUTED_SCALE


def _dispatch_lists(idx_t, w_t):
    k, t = idx_t.shape
    s = k * t
    n_blocks = s // MOE_BLOCK + N_EXPERTS
    flat_tok = jnp.arange(s, dtype=jnp.int32) % t
    sorted_e, sorted_tok, sorted_w = lax.sort((idx_t.reshape(s), flat_tok, w_t.reshape(s)), num_keys=1)
    experts = jnp.arange(N_EXPERTS + 1, dtype=jnp.int32)
    start = jnp.sum(sorted_e[None, :] < experts[:, None], axis=1, dtype=jnp.int32)
    counts = start[1:] - start[:-1]
    blocks_per_e = (counts + MOE_BLOCK - 1) // MOE_BLOCK
    block_end = jnp.cumsum(blocks_per_e)
    n_used = block_end[-1]
    blk = jnp.arange(n_blocks, dtype=jnp.int32)
    block_e = jnp.sum(block_end[None, :] <= blk[:, None], axis=1, dtype=jnp.int32)
    block_e = jnp.minimum(block_e, N_EXPERTS - 1)
    onehot = (block_e[:, None] == experts[None, :N_EXPERTS]).astype(jnp.int32)
    pick = lambda a: jnp.sum(onehot * a[None, :], axis=1)
    offset = (blk - (pick(block_end) - pick(blocks_per_e))) * MOE_BLOCK
    row_start = pick(start[:-1]) + offset
    n_rows = jnp.clip(pick(counts) - offset, 0, MOE_BLOCK)
    used = blk < n_used
    last_e = jnp.sum(jnp.where(blk == n_used - 1, block_e, 0))
    block_e = jnp.where(used, block_e, last_e)
    row_start = jnp.where(used, row_start, 0)
    n_rows = jnp.where(used, n_rows, 0)
    return block_e, row_start, n_rows, (n_used - 1).reshape(1), sorted_tok, sorted_w


def _block_lists_kernel(rs_ref, nr_ref, tok_ref, w_ref, off_ref, wout_ref, *, t):
    lane = lax.broadcasted_iota(jnp.int32, (LIST_ROWS, LANES), 1)
    row = lax.broadcasted_iota(jnp.int32, (LIST_ROWS, LANES), 0)

    def one_block(b, carry):
        start = rs_ref[b]
        n_rows = nr_ref[b]
        q = start // LANES
        lane0 = start % LANES
        q8 = pl.multiple_of((q // SUBLANES) * SUBLANES, SUBLANES)

        def window(ref):
            a = ref[pl.ds(q8, 2 * SUBLANES), :]
            a = pltpu.roll(a, 2 * SUBLANES - (q - q8), axis=0)
            lo = pltpu.roll(a[0:LIST_ROWS], LANES - lane0, axis=1)
            hi = pltpu.roll(a[1:LIST_ROWS + 1], LANES - lane0, axis=1)
            return jnp.where(lane < LANES - lane0, lo, hi)

        valid = row * LANES + lane < n_rows
        off_ref[b] = jnp.where(valid, window(tok_ref) * SUBLANES, t * SUBLANES)
        wout_ref[b] = jnp.where(valid, window(w_ref), 0.0)
        return carry

    n_blocks = off_ref.shape[0]
    unroll = next(u for u in (4, 2, 1) if n_blocks % u == 0)

    def several_blocks(i, carry):
        for u in range(unroll):
            one_block(i * unroll + u, carry)
        return carry

    lax.fori_loop(0, n_blocks // unroll, several_blocks, 0)


def _block_lists(row_start, n_rows, sorted_tok, sorted_w, t):
    n_blocks = row_start.shape[0]
    n_slots = sorted_tok.shape[0]
    list_rows = n_slots // LANES + 2 * SUBLANES
    pad = list_rows * LANES - n_slots
    tok2d = jnp.pad(sorted_tok, (0, pad)).reshape(list_rows, LANES)
    w2d = jnp.pad(sorted_w, (0, pad)).reshape(list_rows, LANES)
    full = lambda i, rs, nr: (0, 0)
    blk = pl.BlockSpec((n_blocks, LIST_ROWS, LANES), lambda i, rs, nr: (0, 0, 0))
    off, wts = pl.pallas_call(
        functools.partial(_block_lists_kernel, t=t),
        grid_spec=pltpu.PrefetchScalarGridSpec(
            num_scalar_prefetch=2,
            grid=(1,),
            in_specs=[pl.BlockSpec((list_rows, LANES), full), pl.BlockSpec((list_rows, LANES), full)],
            out_specs=[blk, blk],
        ),
        out_shape=[jax.ShapeDtypeStruct((n_blocks, LIST_ROWS, LANES), jnp.int32),
                   jax.ShapeDtypeStruct((n_blocks, LIST_ROWS, LANES), F32)],
        compiler_params=_params(("arbitrary",)),
    )(row_start, n_rows, tok2d, w2d)
    return off.reshape(n_blocks * LIST_LEN), wts.reshape(n_blocks * LIST_LEN)


def _expert_kernel(be_ref, nr_ref, last_ref, x_hbm, wg_hbm, wu_hbm, wd_hbm, off_ref, out_ref,
                   xs, xt, wgb, wub, wdb, sem, wsem):
    step = pl.program_id(0)
    chunks = wg_hbm.shape[1] // LANES
    last = last_ref[0]
    ahead = WEIGHT_BUFFERS - 1

    def weight_copies(blk):
        e = be_ref[blk]
        slot = blk % WEIGHT_BUFFERS
        return [pltpu.make_async_copy(src.at[e], dst.at[slot], wsem.at[slot])
                for src, dst in ((wg_hbm, wgb), (wu_hbm, wub), (wd_hbm, wdb))]

    @pl.when(step == 0)
    def _load():
        for blk in range(ahead):
            @pl.when(blk <= last)
            def _(blk=blk):
                for cp in weight_copies(blk):
                    cp.start()
        rows = x_hbm.shape[0]
        cp = pltpu.make_async_copy(x_hbm, xs.at[pl.ds(0, rows)], sem)
        cp.start()
        cp.wait()
        xs[pl.ds(rows, SUBLANES), :] = jnp.zeros((SUBLANES, LANES), F32)
        xt[...] = jnp.zeros(xt.shape, F32)

    def one_block(g):
        b = step * BLOCKS_PER_STEP + g
        n_rows = nr_ref[b]

        @pl.when(b + ahead <= last)
        def _prefetch():
            for cp in weight_copies(b + ahead):
                cp.start()

        @pl.when(b <= last)
        def _block():
            slot = b % WEIGHT_BUFFERS
            for cp in weight_copies(b):
                cp.wait()
            for seg in range(MOE_BLOCK // ROW_SEGMENT):
                @pl.when(n_rows > seg * ROW_SEGMENT)
                def _gather(seg=seg):
                    for r in range(seg * ROW_SEGMENT, (seg + 1) * ROW_SEGMENT):
                        src = pl.multiple_of(off_ref[g * LIST_LEN + r], SUBLANES)
                        xt[pl.ds(r, chunks, stride=ROW_STRIDE), :] = xs[pl.ds(src, SUBLANES), :]

            x = jnp.concatenate(
                [xt[j * ROW_STRIDE:j * ROW_STRIDE + MOE_BLOCK, :] for j in range(chunks)], axis=1).astype(BF16)
            gate = jnp.dot(x, wgb[slot].astype(BF16), preferred_element_type=F32)
            up = jnp.dot(x, wub[slot].astype(BF16), preferred_element_type=F32)
            hidden = (_silu(gate) * up).astype(BF16)
            out = jnp.dot(hidden, wdb[slot].astype(BF16), preferred_element_type=F32)
            out_ref[g * MOE_BLOCK:(g + 1) * MOE_BLOCK, :] = out.astype(out_ref.dtype)

    for g in range(BLOCKS_PER_STEP):
        one_block(g)


def _combine_kernel(nr_ref, last_ref, o_ref, off_ref, w_ref, y_hbm, ys, ot, sem, *, t, n_steps):
    step = pl.program_id(0)
    rows = t * SUBLANES
    chunks = o_ref.shape[1] // LANES
    group = ROW_GROUP

    @pl.when(step == 0)
    def _init():
        ys[...] = jnp.zeros(ys.shape, F32)

    def one_block(g):
        n_rows = nr_ref[step * BLOCKS_PER_STEP + g]
        row0 = g * MOE_BLOCK
        slot0 = g * LIST_LEN

        @pl.when(n_rows > 0)
        def _block():
            for j in range(chunks):
                ot[j * ROW_STRIDE:j * ROW_STRIDE + MOE_BLOCK, :] = (
                    o_ref[row0:row0 + MOE_BLOCK, j * LANES:(j + 1) * LANES].astype(F32))
            for seg in range(MOE_BLOCK // ROW_SEGMENT):
                @pl.when(n_rows > seg * ROW_SEGMENT)
                def _scatter(seg=seg):
                    for r0 in range(seg * ROW_SEGMENT, (seg + 1) * ROW_SEGMENT, group):
                        dst = [pl.multiple_of(off_ref[slot0 + r0 + i], SUBLANES) for i in range(group)]
                        vals = [ys[pl.ds(dst[i], SUBLANES), :]
                                + w_ref[slot0 + r0 + i] * ot[pl.ds(r0 + i, chunks, stride=ROW_STRIDE), :]
                                for i in range(group)]
                        for i in range(group):
                            ys[pl.ds(dst[i], SUBLANES), :] = vals[i]

    for g in range(BLOCKS_PER_STEP):
        one_block(g)

    @pl.when(step == n_steps - 1)
    def _store():
        cp = pltpu.make_async_copy(ys.at[pl.ds(0, rows)], y_hbm, sem)
        cp.start()
        cp.wait()


def _routed_experts(h2_rows, block_e, row_start, n_rows, last_block, sorted_tok, sorted_w, w_gate, w_up, w_down):
    rows, lanes = h2_rows.shape
    t = rows // SUBLANES
    n_blocks = block_e.shape[0]
    d, e_dim = w_gate.shape[1], w_gate.shape[2]
    assert d == SUBLANES * LANES and lanes == LANES
    offsets, weights = _block_lists(row_start, n_rows, sorted_tok, sorted_w, t)
    per_step = BLOCKS_PER_STEP
    assert n_blocks % per_step == 0
    n_steps = n_blocks // per_step

    def used_step(s, last):
        return jnp.minimum(s, last[0] // per_step)

    staging = pltpu.VMEM((SUBLANES * ROW_STRIDE, LANES), F32)
    resident = pltpu.VMEM((rows + SUBLANES, LANES), F32)
    expert_out = pl.pallas_call(
        _expert_kernel,
        grid_spec=pltpu.PrefetchScalarGridSpec(
            num_scalar_prefetch=3,
            grid=(n_steps,),
            in_specs=[
                pl.BlockSpec(memory_space=pl.ANY),
                pl.BlockSpec(memory_space=pl.ANY),
                pl.BlockSpec(memory_space=pl.ANY),
                pl.BlockSpec(memory_space=pl.ANY),
                pl.BlockSpec((per_step * LIST_LEN,), lambda s, be, nr, last: (used_step(s, last),),
                             memory_space=pltpu.SMEM),
            ],
            out_specs=pl.BlockSpec((per_step * MOE_BLOCK, d), lambda s, be, nr, last: (used_step(s, last), 0)),
            scratch_shapes=[
                resident,
                staging,
                pltpu.VMEM((WEIGHT_BUFFERS, d, e_dim), F32),
                pltpu.VMEM((WEIGHT_BUFFERS, d, e_dim), F32),
                pltpu.VMEM((WEIGHT_BUFFERS, e_dim, d), F32),
                pltpu.SemaphoreType.DMA(()),
                pltpu.SemaphoreType.DMA((WEIGHT_BUFFERS,)),
            ],
        ),
        out_shape=jax.ShapeDtypeStruct((n_blocks * MOE_BLOCK, d), BF16),
        compiler_params=_params(("arbitrary",)),
    )(block_e, n_rows, last_block, h2_rows, w_gate, w_up, w_down, offsets)
    list_spec = pl.BlockSpec((per_step * LIST_LEN,), lambda s, nr, last: (used_step(s, last),),
                             memory_space=pltpu.SMEM)
    return pl.pallas_call(
        functools.partial(_combine_kernel, t=t, n_steps=n_steps),
        grid_spec=pltpu.PrefetchScalarGridSpec(
            num_scalar_prefetch=2,
            grid=(n_steps,),
            in_specs=[pl.BlockSpec((per_step * MOE_BLOCK, d), lambda s, nr, last: (used_step(s, last), 0)),
                      list_spec, list_spec],
            out_specs=pl.BlockSpec(memory_space=pl.ANY),
            scratch_shapes=[resident, staging, pltpu.SemaphoreType.DMA(())],
        ),
        out_shape=jax.ShapeDtypeStruct((rows, LANES), F32),
        compiler_params=_params(("arbitrary",)),
    )(n_rows, last_block, expert_out, offsets, weights)


def _final_kernel(x1_ref, h2_ref, routed_ref, mod_ref, wg_ref, wu_ref, wd_ref, g_ref, b_ref,
                  yc_ref, yl_ref, st_ref, wgb_ref, wub_ref, wdb_ref, *, d, n_ctx_tiles):
    for src, dst in ((wg_ref, wgb_ref), (wu_ref, wub_ref), (wd_ref, wdb_ref)):
        _cast_weight_once(src, dst)
    is_ctx = pl.program_id(0) < n_ctx_tiles
    gate2 = mod_ref[0, :, 5 * d:6 * d]
    h = h2_ref[...]
    gate = jnp.dot(h, wgb_ref[...], preferred_element_type=F32)
    up = jnp.dot(h, wub_ref[...], preferred_element_type=F32)
    shared = jnp.dot((_silu(gate) * up).astype(BF16), wdb_ref[...], preferred_element_type=F32)
    ffn = _slabs_to_rows(routed_ref, st_ref, h.shape[0]) + shared
    y = _layer_norm(DEEPNORM_ALPHA * x1_ref[...] + gate2 * ffn, g_ref[...], b_ref[...])

    @pl.when(is_ctx)
    def _():
        yc_ref[...] = y

    @pl.when(jnp.logical_not(is_ctx))
    def _():
        yl_ref[...] = y


def _final(x1, h2, routed, mod3, w_g, w_u, w_d, ln_g, ln_b, mod_map, t_ctx):
    t, d = x1.shape
    t_lat = t - t_ctx
    n_ctx_tiles = t_ctx // TOKEN_TILE
    tm = TOKEN_TILE
    row = lambda i: (i, 0)
    full = lambda i: (0, 0)
    return pl.pallas_call(
        functools.partial(_final_kernel, d=d, n_ctx_tiles=n_ctx_tiles),
        grid=(t // tm,),
        in_specs=[
            pl.BlockSpec((tm, d), row),
            pl.BlockSpec((tm, d), row),
            pl.BlockSpec((tm * SUBLANES, LANES), row),
            pl.BlockSpec((1, 1, mod3.shape[2]), mod_map),
            _resident(w_g.shape),
            _resident(w_u.shape),
            _resident(w_d.shape),
            pl.BlockSpec((1, d), full),
            pl.BlockSpec((1, d), full),
        ],
        out_specs=_pair_specs(d, n_ctx_tiles),
        out_shape=[jax.ShapeDtypeStruct((t_ctx, d), F32), jax.ShapeDtypeStruct((t_lat, d), F32)],
        scratch_shapes=[pltpu.VMEM((SUBLANES * TILE_STRIDE, LANES), F32),
                        pltpu.VMEM(w_g.shape, BF16), pltpu.VMEM(w_u.shape, BF16), pltpu.VMEM(w_d.shape, BF16)],
        compiler_params=_params(("arbitrary",)),
    )(x1, h2, routed, mod3, w_g, w_u, w_d, ln_g, ln_b)


def kernel(x_prompt, x_sample, cache_k, cache_v, state_hgrn, c, c_ctx, w_mod, b_mod, w_in, hg_lb, hg_norm_g, q_norm_g, k_norm_g, w_branch_a, w_branch_b, w_out, ln1_g, ln1_b, w_router, router_bias, w_e_gate, w_e_up, w_e_down, w_s_gate, w_s_up, w_s_down, ln2_g, ln2_b):
    assert w_mod.shape[0] == DEPTH
    n_ctx, seq_ctx, d = x_prompt.shape
    n_lat, seq_lat, _ = x_sample.shape
    t_ctx = n_ctx * seq_ctx
    t_lat = n_lat * seq_lat
    assert seq_ctx == TOKEN_TILE and seq_lat % TOKEN_TILE == 0
    layer = 0

    lb = jnp.cumsum(jax.nn.softmax(hg_lb.astype(F32), axis=0), axis=0)[layer]

    cond = jnp.concatenate([c_ctx[None, :], c], axis=0)
    cond = jnp.pad(cond, ((0, (-cond.shape[0]) % SUBLANES), (0, 0)))
    mod = _modulation(cond, w_mod[layer], b_mod[layer][None, :])
    mod3 = mod.reshape(mod.shape[0], 1, mod.shape[1])
    mod_map = _mod_row_map(t_ctx // TOKEN_TILE, seq_lat // TOKEN_TILE)

    x_ctx = x_prompt.reshape(t_ctx, d)
    x_lat = x_sample.reshape(t_lat, d)
    zh, zf, za, zg = _input_projection(x_ctx, x_lat, mod3, w_in[layer], lb, mod_map)

    norm_g = hg_norm_g[layer][None, :]
    oa_ctx, s_ctx = _hgrn_scan(zh, zf, norm_g, None, batch=n_ctx, seq=seq_ctx, row_block0=0,
                               heads=HG_HEADS_PER_STEP_CTX)
    oa_lat, _ = _hgrn_scan(zh, zf, norm_g, state_hgrn[:, layer], batch=n_lat, seq=seq_lat,
                           row_block0=t_ctx // seq_lat, heads=HG_HEADS_PER_STEP_LATENT)

    q_gain = jnp.tile(q_norm_g[layer], N_HEADS)[None, :]
    k_gain = jnp.tile(k_norm_g[layer], N_KV_HEADS)[None, :]
    lane = jnp.arange(ATT_WIDTH)
    group_ones = (lane[:, None] // HEAD_DIM == lane[None, :] // HEAD_DIM).astype(BF16)
    ob_ctx, k_ctx, v_ctx = _attention(za, q_gain, k_gain, group_ones, batch=n_ctx, seq=seq_ctx, row_block0=0)
    past = cache_k.shape[2]
    ob_lat = _attention(
        za, q_gain, k_gain, group_ones, batch=n_lat, seq=seq_lat, row_block0=t_ctx // seq_lat,
        rope=_rope_tables(seq_lat),
        cache=(cache_k[:, layer].reshape(n_lat, past, KV_WIDTH), cache_v[:, layer].reshape(n_lat, past, KV_WIDTH)))

    wr_t = w_router[layer].T
    wr_hi = wr_t.astype(BF16)
    wr_lo = (wr_t - wr_hi.astype(F32)).astype(BF16)
    x1, h2, h2_slabs, idx_t, w_t = _post_mixer(
        x_ctx, x_lat, mod3, oa_ctx, oa_lat, ob_ctx, ob_lat, zg, w_branch_a[layer], w_branch_b[layer], w_out[layer],
        ln1_g[layer][None, :], ln1_b[layer][None, :], wr_hi, wr_lo, router_bias[layer][:, None], mod_map)

    routed = _routed_experts(h2_slabs, *_dispatch_lists(idx_t, w_t), w_e_gate[layer], w_e_up[layer], w_e_down[layer])

    y_ctx, y_lat = _final(x1, h2, routed, mod3, w_s_gate[layer], w_s_up[layer], w_s_down[layer],
                          ln2_g[layer][None, :], ln2_b[layer][None, :], mod_map, t_ctx)

    y_prompt = y_ctx.reshape(n_ctx, seq_ctx, d)
    y_sample = y_lat.reshape(n_lat, seq_lat, d)
    new_cache_k = k_ctx.reshape(n_ctx, 1, seq_ctx, N_KV_HEADS, HEAD_DIM)
    new_cache_v = v_ctx.reshape(n_ctx, 1, seq_ctx, N_KV_HEADS, HEAD_DIM)
    new_state = s_ctx[:, None]
    return (y_prompt, y_sample, new_cache_k, new_cache_v, new_state)
```

```python
import functools

import jax
import jax.numpy as jnp
from jax import lax
from jax.experimental import pallas as pl
from jax.experimental.pallas import tpu as pltpu

F32 = jnp.float32
BF16 = jnp.bfloat16

GRID_W = 64
HG_HEADS = 4
HG_DK = 128
HG_DV = 128
HG_WIDTH = HG_HEADS * HG_DK
N_HEADS = 8
N_KV_HEADS = 2
HEAD_DIM = 64
ATT_WIDTH = N_HEADS * HEAD_DIM
KV_WIDTH = N_KV_HEADS * HEAD_DIM
ROPE_THETA = 10000.0
N_EXPERTS = 256
TOP_K = 8
N_GROUPS = 8
TOPK_GROUPS = 4
GROUP_SIZE = N_EXPERTS // N_GROUPS
ROUTED_SCALE = 2.5
NORM_EPS = 1e-6
DEPTH = 1
DEEPNORM_ALPHA = (2 * DEPTH) ** 0.25

LANES = 128
SUBLANES = 8
VMEM_LIMIT = 56 * 1024 * 1024

CAST_CHUNK = 512
TOKEN_TILE = 256
TILE_STRIDE = TOKEN_TILE + 1
HG_CHUNK = 32
HG_HEADS_PER_STEP_CTX = 4
HG_HEADS_PER_STEP_LATENT = 2
ATT_Q_BLOCK = 128
MOE_BLOCK = 320
ROW_STRIDE = MOE_BLOCK + 1
LIST_LEN = 512
LIST_ROWS = LIST_LEN // LANES
ROW_GROUP = 8
ROW_SEGMENT = 32
MAX_BLOCKS_PER_STEP = 4
WEIGHT_BUFFERS = 3


def _sigmoid(x):
    return 1.0 / (1.0 + jnp.exp(-x))


def _silu(x):
    return x * _sigmoid(x)


def _params(sem=None):
    return pltpu.CompilerParams(dimension_semantics=sem, vmem_limit_bytes=VMEM_LIMIT)


def _resident(shape):
    return pl.BlockSpec(shape, lambda i: (0,) * len(shape), pipeline_mode=pl.Buffered(1))


def _cast_weight_once(src_ref, dst_ref):
    @pl.when(pl.program_id(0) == 0)
    def _():
        width = src_ref.shape[1]
        for lo in range(0, width, CAST_CHUNK):
            hi = min(lo + CAST_CHUNK, width)
            dst_ref[:, lo:hi] = src_ref[:, lo:hi].astype(dst_ref.dtype)


def _mod_kernel(c_ref, w_ref, b_ref, o_ref):
    s = _silu(c_ref[...]).astype(BF16)
    o_ref[...] = jnp.dot(s, w_ref[...].astype(BF16), preferred_element_type=F32) + b_ref[...]


def _modulation(cond, w_mod, b_mod):
    n, d = cond.shape
    width = w_mod.shape[1]
    tn = width // 4
    return pl.pallas_call(
        _mod_kernel,
        grid=(4,),
        in_specs=[
            pl.BlockSpec((n, d), lambda j: (0, 0)),
            pl.BlockSpec((d, tn), lambda j: (0, j)),
            pl.BlockSpec((1, tn), lambda j: (0, j)),
        ],
        out_specs=pl.BlockSpec((n, tn), lambda j: (0, j)),
        out_shape=jax.ShapeDtypeStruct((n, width), F32),
        compiler_params=_params(("arbitrary",)),
    )(cond, w_mod, b_mod)


def _inproj_kernel(xc_ref, xl_ref, mod_ref, w_ref, lb_ref, zh_ref, zf_ref, za_ref, zg_ref, wb_ref,
                   *, d, n_ctx_tiles):
    _cast_weight_once(w_ref, wb_ref)
    shift = mod_ref[0, :, 0:d]
    scale = mod_ref[0, :, d:2 * d]
    x = _pick(pl.program_id(0) < n_ctx_tiles, xc_ref, xl_ref)
    h = (x * (1.0 + scale) + shift).astype(BF16)

    def proj(lo, hi):
        return jnp.dot(h, wb_ref[:, lo:hi], preferred_element_type=F32)

    w = HG_WIDTH
    zh_ref[:, 0:w] = _silu(proj(0, w)).astype(zh_ref.dtype)
    zh_ref[:, w:2 * w] = proj(w, 2 * w).astype(zh_ref.dtype)
    for i in range(2):
        lb = lb_ref[i:i + 1, :]
        zf_ref[:, i * w:(i + 1) * w] = lb + (1.0 - lb) * _sigmoid(proj((2 + i) * w, (3 + i) * w))
    zh_ref[:, 2 * w:3 * w] = _silu(proj(4 * w, 5 * w)).astype(zh_ref.dtype)
    a0 = 5 * w
    a1 = a0 + ATT_WIDTH + 2 * KV_WIDTH
    za_ref[...] = proj(a0, a1)
    for i in range(4):
        lo = a1 + i * (d // 2)
        zg_ref[:, i * (d // 2):(i + 1) * (d // 2)] = _sigmoid(proj(lo, lo + d // 2)).astype(zg_ref.dtype)


def _mod_row_map(n_ctx_tiles, tiles_per_latent):
    def index_map(i):
        row = jnp.where(i < n_ctx_tiles, 0, 1 + (i - n_ctx_tiles) // tiles_per_latent)
        return (row, 0, 0)
    return index_map


def _pair_specs(width, n_ctx_tiles):
    return [pl.BlockSpec((TOKEN_TILE, width), lambda i: (jnp.minimum(i, n_ctx_tiles - 1), 0)),
            pl.BlockSpec((TOKEN_TILE, width), lambda i: (jnp.maximum(i - n_ctx_tiles, 0), 0))]


def _pick(is_ctx, ctx_ref, lat_ref):
    return jnp.where(is_ctx, ctx_ref[...], lat_ref[...])


def _input_projection(x_ctx, x_lat, mod3, w_in, lb, mod_map):
    d = x_ctx.shape[1]
    n_ctx_tiles = x_ctx.shape[0] // TOKEN_TILE
    t = x_ctx.shape[0] + x_lat.shape[0]
    width = w_in.shape[1]
    zh_w = 3 * HG_WIDTH
    zf_w = 2 * HG_WIDTH
    za_w = ATT_WIDTH + 2 * KV_WIDTH
    zg_w = 2 * d
    assert width == zh_w + zf_w + za_w + zg_w
    tm = TOKEN_TILE
    return pl.pallas_call(
        functools.partial(_inproj_kernel, d=d, n_ctx_tiles=n_ctx_tiles),
        grid=(t // tm,),
        in_specs=_pair_specs(d, n_ctx_tiles) + [
            pl.BlockSpec((1, 1, mod3.shape[2]), mod_map),
            _resident((d, width)),
            pl.BlockSpec((2, HG_WIDTH), lambda i: (0, 0)),
        ],
        out_specs=[
            pl.BlockSpec((tm, zh_w), lambda i: (i, 0)),
            pl.BlockSpec((tm, zf_w), lambda i: (i, 0)),
            pl.BlockSpec((tm, za_w), lambda i: (i, 0)),
            pl.BlockSpec((tm, zg_w), lambda i: (i, 0)),
        ],
        out_shape=[
            jax.ShapeDtypeStruct((t, zh_w), BF16),
            jax.ShapeDtypeStruct((t, zf_w), F32),
            jax.ShapeDtypeStruct((t, za_w), F32),
            jax.ShapeDtypeStruct((t, zg_w), BF16),
        ],
        scratch_shapes=[pltpu.VMEM((d, width), BF16)],
        compiler_params=_params(("arbitrary",)),
    )(x_ctx, x_lat, mod3, w_in, lb)


def _hgrn_kernel(*refs, seq, has_state, heads):
    if has_state:
        q_ref, v_ref, ff_ref, fb_ref, gs_ref, ng_ref, s0_ref, o_ref, sout_ref = refs
    else:
        q_ref, v_ref, ff_ref, fb_ref, gs_ref, ng_ref, o_ref, sout_ref = refs
        s0_ref = None
    for hd in range(heads):
        _hgrn_head(q_ref, v_ref, ff_ref, fb_ref, gs_ref, ng_ref, s0_ref, o_ref, sout_ref, hd, seq)


def _hgrn_head(q_ref, v_ref, ff_ref, fb_ref, gs_ref, ng_ref, s0_ref, o_ref, sout_ref, hd, seq):
    has_state = s0_ref is not None
    cols = slice(hd * HG_DK, (hd + 1) * HG_DK)
    c = HG_CHUNK
    n = seq // c
    q3 = q_ref[:, cols].astype(F32).reshape(n, c, HG_DK)
    v3 = v_ref[:, cols].reshape(n, c, HG_DV)
    pos = lax.broadcasted_iota(jnp.int32, (seq, HG_DK), 0) % c
    t_idx = lax.broadcasted_iota(jnp.int32, (c, c), 0)
    s_idx = lax.broadcasted_iota(jnp.int32, (c, c), 1)
    o_sum = None
    for direction, f_ref in enumerate((ff_ref, fb_ref)):
        reverse = direction == 1
        f = f_ref[:, cols]
        k3 = (1.0 - f).reshape(n, c, HG_DK)
        b = jnp.log(f)
        step = 1
        while step < c:
            if reverse:
                b = b + jnp.where(pos < c - step, pltpu.roll(b, seq - step, axis=0), 0.0)
            else:
                b = b + jnp.where(pos >= step, pltpu.roll(b, step, axis=0), 0.0)
            step *= 2
        b3 = b.reshape(n, c, HG_DK)
        edge = b3[:, 0:1, :] if reverse else b3[:, c - 1:c, :]
        mid = b3[:, c // 2:c // 2 + 1, :]
        q_rel = q3 * jnp.exp(b3 - mid)
        k_rel = k3 * jnp.exp(mid - b3)
        q_mid = q_rel.astype(BF16)
        k_mid = k_rel.astype(BF16)
        q_in = (q_rel * jnp.exp(mid)).astype(BF16)
        k_end = (k_rel * jnp.exp(edge - mid)).astype(BF16)
        scores = jnp.einsum('ntd,nsd->nts', q_mid, k_mid, preferred_element_type=F32)
        keep = (s_idx >= t_idx) if reverse else (s_idx <= t_idx)
        scores = jnp.where(keep[None], scores, 0.0).astype(BF16)
        o_intra = jnp.einsum('nts,nsv->ntv', scores, v3, preferred_element_type=F32)
        upd = jnp.einsum('nsv,nsd->nvd', v3, k_end, preferred_element_type=F32)
        dec = jnp.exp(edge)
        if has_state:
            st = s0_ref[0, direction, hd].T
        else:
            st = jnp.zeros((HG_DV, HG_DK), F32)
        before = [None] * n
        for ci in (range(n - 1, -1, -1) if reverse else range(n)):
            before[ci] = st.astype(BF16)
            st = st * dec[ci] + upd[ci]
        st_before = jnp.stack(before, axis=0)
        o_inter = jnp.einsum('ntd,nvd->ntv', q_in, st_before, preferred_element_type=F32)
        o_dir = (o_intra + o_inter).reshape(seq, HG_DV)
        o_sum = o_dir if o_sum is None else o_sum + o_dir
        sout_ref[0, direction, hd] = st.T
    ms = jnp.mean(o_sum * o_sum, axis=-1, keepdims=True)
    o = o_sum * lax.rsqrt(ms + NORM_EPS) * ng_ref[...]
    o_ref[:, cols] = (o * gs_ref[:, cols].astype(F32)).astype(o_ref.dtype)


def _hgrn_scan(zh, zf, norm_g, s0, *, batch, seq, row_block0, heads):
    has_state = s0 is not None
    h = HG_HEADS
    groups = h // heads
    width = heads * HG_DK

    def col(section):
        return pl.BlockSpec((seq, width), lambda b, j: (row_block0 + b, section * groups + j))

    in_specs = [col(0), col(1), col(0), col(1), col(2), pl.BlockSpec((1, HG_DV), lambda b, j: (0, 0))]
    args = [zh, zh, zf, zf, zh, norm_g]
    state_spec = pl.BlockSpec((1, 2, heads, HG_DK, HG_DV), lambda b, j: (b, 0, j, 0, 0))
    if has_state:
        in_specs.append(state_spec)
        args.append(s0)
    return pl.pallas_call(
        functools.partial(_hgrn_kernel, seq=seq, has_state=has_state, heads=heads),
        grid=(batch, groups),
        in_specs=in_specs,
        out_specs=[pl.BlockSpec((seq, width), lambda b, j: (b, j)), state_spec],
        out_shape=[
            jax.ShapeDtypeStruct((batch * seq, h * HG_DV), BF16),
            jax.ShapeDtypeStruct((batch, 2, h, HG_DK, HG_DV), F32),
        ],
        compiler_params=_params(("arbitrary", "arbitrary")),
    )(*args)


def _group_rms_norm(x, group_ones, gain):
    sq = x * x
    hi = sq.astype(BF16)
    lo = (sq - hi.astype(F32)).astype(BF16)
    total = (jnp.dot(hi, group_ones, preferred_element_type=F32)
             + jnp.dot(lo, group_ones, preferred_element_type=F32))
    return x * lax.rsqrt(total * (1.0 / HEAD_DIM) + NORM_EPS) * gain


def _rope(x, cos, sin_signed):
    width = x.shape[1]
    quarter = HEAD_DIM // 4
    lane = lax.broadcasted_iota(jnp.int32, x.shape, 1)
    partner = jnp.where(lane % (2 * quarter) < quarter,
                        pltpu.roll(x, width - quarter, axis=1),
                        pltpu.roll(x, quarter, axis=1))
    return x * cos + partner * sin_signed


def _attn_kernel(*refs, seq, latent):
    if latent:
        za_ref, qg_ref, kg_ref, gm_ref, cos_ref, sin_ref, ck_ref, cv_ref, o_ref = refs
    else:
        za_ref, qg_ref, kg_ref, gm_ref, o_ref, kout_ref, vout_ref = refs
    q = za_ref[:, 0:ATT_WIDTH]
    k = za_ref[:, ATT_WIDTH:ATT_WIDTH + KV_WIDTH]
    v = za_ref[:, ATT_WIDTH + KV_WIDTH:ATT_WIDTH + 2 * KV_WIDTH]
    qn = _group_rms_norm(q, gm_ref[...], qg_ref[...])
    kn = _group_rms_norm(k, gm_ref[0:KV_WIDTH, 0:KV_WIDTH], kg_ref[...])
    if latent:
        qn = _rope(qn, cos_ref[...], sin_ref[...])
        kr = _rope(kn, cos_ref[:, 0:KV_WIDTH], sin_ref[:, 0:KV_WIDTH])
    else:
        kout_ref[...] = kn
        vout_ref[...] = v
        kr = kn
    qb = (qn * (HEAD_DIM ** -0.5)).astype(BF16)
    kb = kr.astype(BF16)
    vb = v.astype(BF16)
    if latent:
        ckb = ck_ref[0].astype(BF16)
        cvb = cv_ref[0].astype(BF16)
    groups = N_HEADS // N_KV_HEADS
    tq = ATT_Q_BLOCK if latent else seq
    nt = (((1,), (1,)), ((), ()))
    for kh in range(N_KV_HEADS):
        ksl = slice(kh * HEAD_DIM, (kh + 1) * HEAD_DIM)
        k_new = kb[:, ksl]
        v_new = vb[:, ksl]
        for blk in range(seq // tq):
            rows = slice(blk * tq, (blk + 1) * tq)
            heads = [kh * groups + g for g in range(groups)]
            q_st = jnp.concatenate([qb[rows, hd * HEAD_DIM:(hd + 1) * HEAD_DIM] for hd in heads], axis=0)
            s_new = lax.dot_general(q_st, k_new, nt, preferred_element_type=F32)
            m = jnp.max(s_new, axis=-1, keepdims=True)
            if latent:
                s_ctx = lax.dot_general(q_st, ckb[:, ksl], nt, preferred_element_type=F32)
                m = jnp.maximum(m, jnp.max(s_ctx, axis=-1, keepdims=True))
            p_new = jnp.exp(s_new - m)
            denom = jnp.sum(p_new, axis=-1, keepdims=True)
            acc = jnp.dot(p_new.astype(BF16), v_new, preferred_element_type=F32)
            if latent:
                p_ctx = jnp.exp(s_ctx - m)
                denom = denom + jnp.sum(p_ctx, axis=-1, keepdims=True)
                acc = acc + jnp.dot(p_ctx.astype(BF16), cvb[:, ksl], preferred_element_type=F32)
            out = acc / denom
            for g in range(0, groups, 2):
                pair = jnp.concatenate([out[g * tq:(g + 1) * tq], out[(g + 1) * tq:(g + 2) * tq]], axis=1)
                lane0 = heads[g] * HEAD_DIM
                o_ref[rows, lane0:lane0 + 2 * HEAD_DIM] = pair.astype(o_ref.dtype)


def _attention(za, q_gain, k_gain, group_ones, *, batch, seq, row_block0, rope=None, cache=None):
    latent = cache is not None
    za_w = za.shape[1]
    in_specs = [
        pl.BlockSpec((seq, za_w), lambda b: (row_block0 + b, 0)),
        pl.BlockSpec((1, ATT_WIDTH), lambda b: (0, 0)),
        pl.BlockSpec((1, KV_WIDTH), lambda b: (0, 0)),
        pl.BlockSpec((ATT_WIDTH, ATT_WIDTH), lambda b: (0, 0)),
    ]
    args = [za, q_gain, k_gain, group_ones]
    o_spec = pl.BlockSpec((seq, ATT_WIDTH), lambda b: (b, 0))
    o_shape = jax.ShapeDtypeStruct((batch * seq, ATT_WIDTH), BF16)
    if latent:
        cos, sin_signed = rope
        ck, cv = cache
        past = ck.shape[1]
        in_specs += [
            pl.BlockSpec((seq, ATT_WIDTH), lambda b: (0, 0)),
            pl.BlockSpec((seq, ATT_WIDTH), lambda b: (0, 0)),
            pl.BlockSpec((1, past, KV_WIDTH), lambda b: (b, 0, 0)),
            pl.BlockSpec((1, past, KV_WIDTH), lambda b: (b, 0, 0)),
        ]
        args += [cos, sin_signed, ck, cv]
        out_specs = o_spec
        out_shape = o_shape
    else:
        kv_spec = pl.BlockSpec((seq, KV_WIDTH), lambda b: (b, 0))
        kv_shape = jax.ShapeDtypeStruct((batch * seq, KV_WIDTH), F32)
        out_specs = [o_spec, kv_spec, kv_spec]
        out_shape = [o_shape, kv_shape, kv_shape]
    return pl.pallas_call(
        functools.partial(_attn_kernel, seq=seq, latent=latent),
        grid=(batch,),
        in_specs=in_specs,
        out_specs=out_specs,
        out_shape=out_shape,
        compiler_params=_params(("arbitrary",)),
    )(*args)


def _rope_tables(seq):
    quarter = HEAD_DIM // 4
    t = jnp.arange(seq)
    row = (t // GRID_W).astype(F32)
    colp = (t % GRID_W).astype(F32)
    inv_freq = ROPE_THETA ** (-jnp.arange(quarter, dtype=F32) / quarter)
    lane = jnp.arange(HEAD_DIM)
    pos = jnp.where((lane < HEAD_DIM // 2)[None, :], row[:, None], colp[:, None])
    ang = pos * inv_freq[lane % quarter][None, :]
    sign = jnp.where(lane % (2 * quarter) < quarter, -1.0, 1.0)[None, :]
    cos = jnp.tile(jnp.cos(ang), (1, N_HEADS))
    sin_signed = jnp.tile(jnp.sin(ang) * sign, (1, N_HEADS))
    return cos, sin_signed


def _layer_norm(y, g, b):
    mu = jnp.mean(y, axis=-1, keepdims=True)
    yc = y - mu
    var = jnp.mean(yc * yc, axis=-1, keepdims=True)
    return yc * lax.rsqrt(var + NORM_EPS) * g + b


def _rows_to_slabs(tile, st_ref, slab_ref):
    rows, d = tile.shape
    chunks = d // LANES
    for j in range(chunks):
        st_ref[j * TILE_STRIDE:j * TILE_STRIDE + rows, :] = tile[:, j * LANES:(j + 1) * LANES]
    for r in range(rows):
        slab_ref[r * chunks:(r + 1) * chunks, :] = st_ref[pl.ds(r, chunks, stride=TILE_STRIDE), :]


def _slabs_to_rows(slab_ref, st_ref, rows):
    chunks = SUBLANES
    for r in range(rows):
        st_ref[pl.ds(r, chunks, stride=TILE_STRIDE), :] = slab_ref[r * chunks:(r + 1) * chunks, :]
    return jnp.concatenate([st_ref[j * TILE_STRIDE:j * TILE_STRIDE + rows, :] for j in range(chunks)], axis=1)


def _post_mixer_kernel(xc_ref, xl_ref, mod_ref, oac_ref, oal_ref, obc_ref, obl_ref, zg_ref, wa_ref, wb_ref,
                       wo_ref, g_ref, b_ref, wrh_ref, wrl_ref, rb_ref,
                       x1_ref, h2_ref, slab_ref, idx_ref, rw_ref, st_ref, wab_ref, wbb_ref, wob_ref,
                       *, d, n_ctx_tiles):
    for src, dst in ((wa_ref, wab_ref), (wb_ref, wbb_ref), (wo_ref, wob_ref)):
        _cast_weight_once(src, dst)
    is_ctx = pl.program_id(0) < n_ctx_tiles
    gate1 = mod_ref[0, :, 2 * d:3 * d]
    shift2 = mod_ref[0, :, 3 * d:4 * d]
    scale2 = mod_ref[0, :, 4 * d:5 * d]
    branch_a = jnp.dot(_pick(is_ctx, oac_ref, oal_ref), wab_ref[...], preferred_element_type=F32)
    branch_b = jnp.dot(_pick(is_ctx, obc_ref, obl_ref), wbb_ref[...], preferred_element_type=F32)
    merged = zg_ref[:, 0:d].astype(F32) * branch_a + zg_ref[:, d:2 * d].astype(F32) * branch_b
    mix = jnp.dot(merged.astype(BF16), wob_ref[...], preferred_element_type=F32)
    x = _pick(is_ctx, xc_ref, xl_ref)
    x1 = _layer_norm(DEEPNORM_ALPHA * x + gate1 * mix, g_ref[...], b_ref[...])
    x1_ref[...] = x1
    h2 = x1 * (1.0 + scale2) + shift2
    h2_ref[...] = h2.astype(h2_ref.dtype)
    _rows_to_slabs(h2, st_ref, slab_ref)
    idx_ref[...], rw_ref[...] = _route_tile(h2, wrh_ref[...], wrl_ref[...], rb_ref[...])


def _post_mixer(x_ctx, x_lat, mod3, oa_ctx, oa_lat, ob_ctx, ob_lat, zg, w_a, w_b, w_o, ln_g, ln_b,
                wr_hi, wr_lo, router_bias, mod_map):
    d = x_ctx.shape[1]
    n_ctx_tiles = x_ctx.shape[0] // TOKEN_TILE
    t = x_ctx.shape[0] + x_lat.shape[0]
    tm = TOKEN_TILE
    row = lambda i: (i, 0)
    full = lambda i: (0, 0)
    return pl.pallas_call(
        functools.partial(_post_mixer_kernel, d=d, n_ctx_tiles=n_ctx_tiles),
        grid=(t // tm,),
        in_specs=_pair_specs(d, n_ctx_tiles) + [
            pl.BlockSpec((1, 1, mod3.shape[2]), mod_map),
        ] + _pair_specs(oa_ctx.shape[1], n_ctx_tiles) + _pair_specs(ob_ctx.shape[1], n_ctx_tiles) + [
            pl.BlockSpec((tm, 2 * d), row),
            _resident(w_a.shape),
            _resident(w_b.shape),
            _resident(w_o.shape),
            pl.BlockSpec((1, d), full),
            pl.BlockSpec((1, d), full),
            pl.BlockSpec((N_EXPERTS, d), full),
            pl.BlockSpec((N_EXPERTS, d), full),
            pl.BlockSpec((N_EXPERTS, 1), full),
        ],
        out_specs=[pl.BlockSpec((tm, d), row), pl.BlockSpec((tm, d), row),
                   pl.BlockSpec((tm * SUBLANES, LANES), row),
                   pl.BlockSpec((TOP_K, tm), lambda i: (0, i)), pl.BlockSpec((TOP_K, tm), lambda i: (0, i))],
        out_shape=[jax.ShapeDtypeStruct((t, d), F32), jax.ShapeDtypeStruct((t, d), BF16),
                   jax.ShapeDtypeStruct((t * SUBLANES, LANES), F32),
                   jax.ShapeDtypeStruct((TOP_K, t), jnp.int32), jax.ShapeDtypeStruct((TOP_K, t), F32)],
        scratch_shapes=[pltpu.VMEM((SUBLANES * TILE_STRIDE, LANES), F32),
                        pltpu.VMEM(w_a.shape, BF16), pltpu.VMEM(w_b.shape, BF16), pltpu.VMEM(w_o.shape, BF16)],
        compiler_params=_params(("arbitrary",)),
    )(x_ctx, x_lat, mod3, oa_ctx, oa_lat, ob_ctx, ob_lat, zg, w_a, w_b, w_o, ln_g, ln_b, wr_hi, wr_lo, router_bias)


def _first_index_of_max(x, idx, sentinel):
    m = jnp.max(x, axis=0, keepdims=True)
    first = jnp.min(jnp.where(x == m, idx, sentinel), axis=0, keepdims=True)
    return m, first


def _route_tile(h, wh, wl, bias):
    hh = h.astype(BF16)
    hl = (h - hh.astype(F32)).astype(BF16)
    nt = (((1,), (1,)), ((), ()))
    logits = (lax.dot_general(wh, hh, nt, preferred_element_type=F32)
              + lax.dot_general(wh, hl, nt, preferred_element_type=F32)
              + lax.dot_general(wl, hh, nt, preferred_element_type=F32))
    scores = _sigmoid(logits)
    sel = scores + bias
    tm = sel.shape[1]
    neg = -jnp.inf
    gidx = lax.broadcasted_iota(jnp.int32, (GROUP_SIZE, tm), 0)
    group_scores = []
    for g in range(N_GROUPS):
        sg = sel[g * GROUP_SIZE:(g + 1) * GROUP_SIZE, :]
        m1, first = _first_index_of_max(sg, gidx, GROUP_SIZE)
        m2 = jnp.max(jnp.where(gidx == first, neg, sg), axis=0, keepdims=True)
        group_scores.append(m1 + m2)
    gs = jnp.concatenate(group_scores, axis=0)
    nidx = lax.broadcasted_iota(jnp.int32, (N_GROUPS, tm), 0)
    chosen = jnp.zeros((N_GROUPS, tm), jnp.bool_)
    for _ in range(TOPK_GROUPS):
        _, first = _first_index_of_max(gs, nidx, N_GROUPS)
        hit = nidx == first
        chosen = jnp.logical_or(chosen, hit)
        gs = jnp.where(hit, neg, gs)
    masked = jnp.concatenate(
        [jnp.where(chosen[g:g + 1, :], sel[g * GROUP_SIZE:(g + 1) * GROUP_SIZE, :], neg) for g in range(N_GROUPS)],
        axis=0)
    eidx = lax.broadcasted_iota(jnp.int32, (N_EXPERTS, tm), 0)
    picks, weights = [], []
    for _ in range(TOP_K):
        _, first = _first_index_of_max(masked, eidx, N_EXPERTS)
        hit = eidx == first
        picks.append(first)
        weights.append(jnp.sum(jnp.where(hit, scores, 0.0), axis=0, keepdims=True))
        masked = jnp.where(hit, neg, masked)
    wk = jnp.concatenate(weights, axis=0)
    return jnp.concatenate(picks, axis=0), wk / jnp.sum(wk, axis=0, keepdims=True) * ROUTED_SCALE


def _dispatch_lists(idx_t, w_t):
    k, t = idx_t.shape
    s = k * t
    n_blocks = s // MOE_BLOCK + N_EXPERTS
    flat_tok = jnp.arange(s, dtype=jnp.int32) % t
    sorted_e, sorted_tok, sorted_w = lax.sort((idx_t.reshape(s), flat_tok, w_t.reshape(s)), num_keys=1)
    experts = jnp.arange(N_EXPERTS + 1, dtype=jnp.int32)
    start = jnp.sum(sorted_e[None, :] < experts[:, None], axis=1, dtype=jnp.int32)
    counts = start[1:] - start[:-1]
    blocks_per_e = (counts + MOE_BLOCK - 1) // MOE_BLOCK
    block_end = jnp.cumsum(blocks_per_e)
    n_used = block_end[-1]
    blk = jnp.arange(n_blocks, dtype=jnp.int32)
    block_e = jnp.sum(block_end[None, :] <= blk[:, None], axis=1, dtype=jnp.int32)
    block_e = jnp.minimum(block_e, N_EXPERTS - 1)
    onehot = (block_e[:, None] == experts[None, :N_EXPERTS]).astype(jnp.int32)
    pick = lambda a: jnp.sum(onehot * a[None, :], axis=1)
    offset = (blk - (pick(block_end) - pick(blocks_per_e))) * MOE_BLOCK
    row_start = pick(start[:-1]) + offset
    n_rows = jnp.clip(pick(counts) - offset, 0, MOE_BLOCK)
    used = blk < n_used
    last_e = jnp.sum(jnp.where(blk == n_used - 1, block_e, 0))
    block_e = jnp.where(used, block_e, last_e)
    row_start = jnp.where(used, row_start, 0)
    n_rows = jnp.where(used, n_rows, 0)
    return block_e, row_start, n_rows, (n_used - 1).reshape(1), sorted_tok, sorted_w


def _block_lists_kernel(rs_ref, nr_ref, tok_ref, w_ref, off_ref, wout_ref, *, t):
    lane = lax.broadcasted_iota(jnp.int32, (LIST_ROWS, LANES), 1)
    row = lax.broadcasted_iota(jnp.int32, (LIST_ROWS, LANES), 0)

    def one_block(b, carry):
        start = rs_ref[b]
        n_rows = nr_ref[b]
        q = start // LANES
        lane0 = start % LANES
        q8 = pl.multiple_of((q // SUBLANES) * SUBLANES, SUBLANES)

        def window(ref):
            a = ref[pl.ds(q8, 2 * SUBLANES), :]
            a = pltpu.roll(a, 2 * SUBLANES - (q - q8), axis=0)
            lo = pltpu.roll(a[0:LIST_ROWS], LANES - lane0, axis=1)
            hi = pltpu.roll(a[1:LIST_ROWS + 1], LANES - lane0, axis=1)
            return jnp.where(lane < LANES - lane0, lo, hi)

        valid = row * LANES + lane < n_rows
        off_ref[b] = jnp.where(valid, window(tok_ref) * SUBLANES, t * SUBLANES)
        wout_ref[b] = jnp.where(valid, window(w_ref), 0.0)
        return carry

    n_blocks = off_ref.shape[0]
    unroll = next(u for u in (4, 2, 1) if n_blocks % u == 0)

    def several_blocks(i, carry):
        for u in range(unroll):
            one_block(i * unroll + u, carry)
        return carry

    lax.fori_loop(0, n_blocks // unroll, several_blocks, 0)


def _block_lists(row_start, n_rows, sorted_tok, sorted_w, t):
    n_blocks = row_start.shape[0]
    n_slots = sorted_tok.shape[0]
    list_rows = n_slots // LANES + 2 * SUBLANES
    pad = list_rows * LANES - n_slots
    tok2d = jnp.pad(sorted_tok, (0, pad)).reshape(list_rows, LANES)
    w2d = jnp.pad(sorted_w, (0, pad)).reshape(list_rows, LANES)
    full = lambda i, rs, nr: (0, 0)
    blk = pl.BlockSpec((n_blocks, LIST_ROWS, LANES), lambda i, rs, nr: (0, 0, 0))
    off, wts = pl.pallas_call(
        functools.partial(_block_lists_kernel, t=t),
        grid_spec=pltpu.PrefetchScalarGridSpec(
            num_scalar_prefetch=2,
            grid=(1,),
            in_specs=[pl.BlockSpec((list_rows, LANES), full), pl.BlockSpec((list_rows, LANES), full)],
            out_specs=[blk, blk],
        ),
        out_shape=[jax.ShapeDtypeStruct((n_blocks, LIST_ROWS, LANES), jnp.int32),
                   jax.ShapeDtypeStruct((n_blocks, LIST_ROWS, LANES), F32)],
        compiler_params=_params(("arbitrary",)),
    )(row_start, n_rows, tok2d, w2d)
    return off.reshape(n_blocks * LIST_LEN), wts.reshape(n_blocks * LIST_LEN)


def _expert_kernel(be_ref, nr_ref, last_ref, x_hbm, wg_hbm, wu_hbm, wd_hbm, off_ref, out_ref,
                   xs, xt, wgb, wub, wdb, sem, wsem):
    step = pl.program_id(0)
    per_step = out_ref.shape[0] // MOE_BLOCK
    chunks = wg_hbm.shape[1] // LANES
    last = last_ref[0]
    ahead = WEIGHT_BUFFERS - 1

    def weight_copies(blk):
        e = be_ref[blk]
        slot = blk % WEIGHT_BUFFERS
        return [pltpu.make_async_copy(src.at[e], dst.at[slot], wsem.at[slot])
                for src, dst in ((wg_hbm, wgb), (wu_hbm, wub), (wd_hbm, wdb))]

    @pl.when(step == 0)
    def _load():
        for blk in range(ahead):
            @pl.when(blk <= last)
            def _(blk=blk):
                for cp in weight_copies(blk):
                    cp.start()
        rows = x_hbm.shape[0]
        cp = pltpu.make_async_copy(x_hbm, xs.at[pl.ds(0, rows)], sem)
        cp.start()
        cp.wait()
        xs[pl.ds(rows, SUBLANES), :] = jnp.zeros((SUBLANES, LANES), F32)
        xt[...] = jnp.zeros(xt.shape, F32)

    def one_block(g):
        b = step * per_step + g
        n_rows = nr_ref[b]

        @pl.when(b + ahead <= last)
        def _prefetch():
            for cp in weight_copies(b + ahead):
                cp.start()

        @pl.when(b <= last)
        def _block():
            slot = b % WEIGHT_BUFFERS
            for cp in weight_copies(b):
                cp.wait()
            for seg in range(MOE_BLOCK // ROW_SEGMENT):
                @pl.when(n_rows > seg * ROW_SEGMENT)
                def _gather(seg=seg):
                    for r in range(seg * ROW_SEGMENT, (seg + 1) * ROW_SEGMENT):
                        src = pl.multiple_of(off_ref[g * LIST_LEN + r], SUBLANES)
                        xt[pl.ds(r, chunks, stride=ROW_STRIDE), :] = xs[pl.ds(src, SUBLANES), :]

            x = jnp.concatenate(
                [xt[j * ROW_STRIDE:j * ROW_STRIDE + MOE_BLOCK, :] for j in range(chunks)], axis=1).astype(BF16)
            gate = jnp.dot(x, wgb[slot].astype(BF16), preferred_element_type=F32)
            up = jnp.dot(x, wub[slot].astype(BF16), preferred_element_type=F32)
            hidden = (_silu(gate) * up).astype(BF16)
            out = jnp.dot(hidden, wdb[slot].astype(BF16), preferred_element_type=F32)
            out_ref[g * MOE_BLOCK:(g + 1) * MOE_BLOCK, :] = out.astype(out_ref.dtype)

    for g in range(per_step):
        one_block(g)


def _combine_kernel(nr_ref, last_ref, o_ref, off_ref, w_ref, y_hbm, ys, ot, sem, *, t, n_steps):
    step = pl.program_id(0)
    per_step = o_ref.shape[0] // MOE_BLOCK
    rows = t * SUBLANES
    chunks = o_ref.shape[1] // LANES
    group = ROW_GROUP

    @pl.when(step == 0)
    def _init():
        ys[...] = jnp.zeros(ys.shape, F32)

    def one_block(g):
        n_rows = nr_ref[step * per_step + g]
        row0 = g * MOE_BLOCK
        slot0 = g * LIST_LEN

        @pl.when(n_rows > 0)
        def _block():
            for j in range(chunks):
                ot[j * ROW_STRIDE:j * ROW_STRIDE + MOE_BLOCK, :] = (
                    o_ref[row0:row0 + MOE_BLOCK, j * LANES:(j + 1) * LANES].astype(F32))
            for seg in range(MOE_BLOCK // ROW_SEGMENT):
                @pl.when(n_rows > seg * ROW_SEGMENT)
                def _scatter(seg=seg):
                    for r0 in range(seg * ROW_SEGMENT, (seg + 1) * ROW_SEGMENT, group):
                        dst = [pl.multiple_of(off_ref[slot0 + r0 + i], SUBLANES) for i in range(group)]
                        vals = [ys[pl.ds(dst[i], SUBLANES), :]
                                + w_ref[slot0 + r0 + i] * ot[pl.ds(r0 + i, chunks, stride=ROW_STRIDE), :]
                                for i in range(group)]
                        for i in range(group):
                            ys[pl.ds(dst[i], SUBLANES), :] = vals[i]

    for g in range(per_step):
        one_block(g)

    @pl.when(step == n_steps - 1)
    def _store():
        cp = pltpu.make_async_copy(ys.at[pl.ds(0, rows)], y_hbm, sem)
        cp.start()
        cp.wait()


def _routed_experts(h2_rows, block_e, row_start, n_rows, last_block, sorted_tok, sorted_w, w_gate, w_up, w_down):
    rows, lanes = h2_rows.shape
    t = rows // SUBLANES
    n_blocks = block_e.shape[0]
    d, e_dim = w_gate.shape[1], w_gate.shape[2]
    assert d == SUBLANES * LANES and lanes == LANES
    offsets, weights = _block_lists(row_start, n_rows, sorted_tok, sorted_w, t)
    per_step = next(p for p in range(MAX_BLOCKS_PER_STEP, 0, -1) if n_blocks % p == 0)
    n_steps = n_blocks // per_step

    def used_step(s, last):
        return jnp.minimum(s, last[0] // per_step)

    staging = pltpu.VMEM((SUBLANES * ROW_STRIDE, LANES), F32)
    resident = pltpu.VMEM((rows + SUBLANES, LANES), F32)
    expert_out = pl.pallas_call(
        _expert_kernel,
        grid_spec=pltpu.PrefetchScalarGridSpec(
            num_scalar_prefetch=3,
            grid=(n_steps,),
            in_specs=[
                pl.BlockSpec(memory_space=pl.ANY),
                pl.BlockSpec(memory_space=pl.ANY),
                pl.BlockSpec(memory_space=pl.ANY),
                pl.BlockSpec(memory_space=pl.ANY),
                pl.BlockSpec((per_step * LIST_LEN,), lambda s, be, nr, last: (used_step(s, last),),
                             memory_space=pltpu.SMEM),
            ],
            out_specs=pl.BlockSpec((per_step * MOE_BLOCK, d), lambda s, be, nr, last: (used_step(s, last), 0)),
            scratch_shapes=[
                resident,
                staging,
                pltpu.VMEM((WEIGHT_BUFFERS, d, e_dim), F32),
                pltpu.VMEM((WEIGHT_BUFFERS, d, e_dim), F32),
                pltpu.VMEM((WEIGHT_BUFFERS, e_dim, d), F32),
                pltpu.SemaphoreType.DMA(()),
                pltpu.SemaphoreType.DMA((WEIGHT_BUFFERS,)),
            ],
        ),
        out_shape=jax.ShapeDtypeStruct((n_blocks * MOE_BLOCK, d), BF16),
        compiler_params=_params(("arbitrary",)),
    )(block_e, n_rows, last_block, h2_rows, w_gate, w_up, w_down, offsets)
    list_spec = pl.BlockSpec((per_step * LIST_LEN,), lambda s, nr, last: (used_step(s, last),),
                             memory_space=pltpu.SMEM)
    return pl.pallas_call(
        functools.partial(_combine_kernel, t=t, n_steps=n_steps),
        grid_spec=pltpu.PrefetchScalarGridSpec(
            num_scalar_prefetch=2,
            grid=(n_steps,),
            in_specs=[pl.BlockSpec((per_step * MOE_BLOCK, d), lambda s, nr, last: (used_step(s, last), 0)),
                      list_spec, list_spec],
            out_specs=pl.BlockSpec(memory_space=pl.ANY),
            scratch_shapes=[resident, staging, pltpu.SemaphoreType.DMA(())],
        ),
        out_shape=jax.ShapeDtypeStruct((rows, LANES), F32),
        compiler_params=_params(("arbitrary",)),
    )(n_rows, last_block, expert_out, offsets, weights)


def _final_kernel(x1_ref, h2_ref, routed_ref, mod_ref, wg_ref, wu_ref, wd_ref, g_ref, b_ref,
                  yc_ref, yl_ref, st_ref, wgb_ref, wub_ref, wdb_ref, *, d, n_ctx_tiles):
    for src, dst in ((wg_ref, wgb_ref), (wu_ref, wub_ref), (wd_ref, wdb_ref)):
        _cast_weight_once(src, dst)
    is_ctx = pl.program_id(0) < n_ctx_tiles
    gate2 = mod_ref[0, :, 5 * d:6 * d]
    h = h2_ref[...]
    gate = jnp.dot(h, wgb_ref[...], preferred_element_type=F32)
    up = jnp.dot(h, wub_ref[...], preferred_element_type=F32)
    shared = jnp.dot((_silu(gate) * up).astype(BF16), wdb_ref[...], preferred_element_type=F32)
    ffn = _slabs_to_rows(routed_ref, st_ref, h.shape[0]) + shared
    y = _layer_norm(DEEPNORM_ALPHA * x1_ref[...] + gate2 * ffn, g_ref[...], b_ref[...])

    @pl.when(is_ctx)
    def _():
        yc_ref[...] = y

    @pl.when(jnp.logical_not(is_ctx))
    def _():
        yl_ref[...] = y


def _final(x1, h2, routed, mod3, w_g, w_u, w_d, ln_g, ln_b, mod_map, t_ctx):
    t, d = x1.shape
    t_lat = t - t_ctx
    n_ctx_tiles = t_ctx // TOKEN_TILE
    tm = TOKEN_TILE
    row = lambda i: (i, 0)
    full = lambda i: (0, 0)
    return pl.pallas_call(
        functools.partial(_final_kernel, d=d, n_ctx_tiles=n_ctx_tiles),
        grid=(t // tm,),
        in_specs=[
            pl.BlockSpec((tm, d), row),
            pl.BlockSpec((tm, d), row),
            pl.BlockSpec((tm * SUBLANES, LANES), row),
            pl.BlockSpec((1, 1, mod3.shape[2]), mod_map),
            _resident(w_g.shape),
            _resident(w_u.shape),
            _resident(w_d.shape),
            pl.BlockSpec((1, d), full),
            pl.BlockSpec((1, d), full),
        ],
        out_specs=_pair_specs(d, n_ctx_tiles),
        out_shape=[jax.ShapeDtypeStruct((t_ctx, d), F32), jax.ShapeDtypeStruct((t_lat, d), F32)],
        scratch_shapes=[pltpu.VMEM((SUBLANES * TILE_STRIDE, LANES), F32),
                        pltpu.VMEM(w_g.shape, BF16), pltpu.VMEM(w_u.shape, BF16), pltpu.VMEM(w_d.shape, BF16)],
        compiler_params=_params(("arbitrary",)),
    )(x1, h2, routed, mod3, w_g, w_u, w_d, ln_g, ln_b)


def kernel(x_prompt, x_sample, cache_k, cache_v, state_hgrn, c, c_ctx, w_mod, b_mod, w_in, hg_lb, hg_norm_g, q_norm_g, k_norm_g, w_branch_a, w_branch_b, w_out, ln1_g, ln1_b, w_router, router_bias, w_e_gate, w_e_up, w_e_down, w_s_gate, w_s_up, w_s_down, ln2_g, ln2_b):
    assert w_mod.shape[0] == DEPTH
    n_ctx, seq_ctx, d = x_prompt.shape
    n_lat, seq_lat, _ = x_sample.shape
    t_ctx = n_ctx * seq_ctx
    t_lat = n_lat * seq_lat
    assert seq_ctx == TOKEN_TILE and seq_lat % TOKEN_TILE == 0
    layer = 0

    lb = jnp.cumsum(jax.nn.softmax(hg_lb.astype(F32), axis=0), axis=0)[layer]

    cond = jnp.concatenate([c_ctx[None, :], c], axis=0)
    cond = jnp.pad(cond, ((0, (-cond.shape[0]) % SUBLANES), (0, 0)))
    mod = _modulation(cond, w_mod[layer], b_mod[layer][None, :])
    mod3 = mod.reshape(mod.shape[0], 1, mod.shape[1])
    mod_map = _mod_row_map(t_ctx // TOKEN_TILE, seq_lat // TOKEN_TILE)

    x_ctx = x_prompt.reshape(t_ctx, d)
    x_lat = x_sample.reshape(t_lat, d)
    zh, zf, za, zg = _input_projection(x_ctx, x_lat, mod3, w_in[layer], lb, mod_map)

    norm_g = hg_norm_g[layer][None, :]
    oa_ctx, s_ctx = _hgrn_scan(zh, zf, norm_g, None, batch=n_ctx, seq=seq_ctx, row_block0=0,
                               heads=HG_HEADS_PER_STEP_CTX)
    oa_lat, _ = _hgrn_scan(zh, zf, norm_g, state_hgrn[:, layer], batch=n_lat, seq=seq_lat,
                           row_block0=t_ctx // seq_lat, heads=HG_HEADS_PER_STEP_LATENT)

    q_gain = jnp.tile(q_norm_g[layer], N_HEADS)[None, :]
    k_gain = jnp.tile(k_norm_g[layer], N_KV_HEADS)[None, :]
    lane = jnp.arange(ATT_WIDTH)
    group_ones = (lane[:, None] // HEAD_DIM == lane[None, :] // HEAD_DIM).astype(BF16)
    ob_ctx, k_ctx, v_ctx = _attention(za, q_gain, k_gain, group_ones, batch=n_ctx, seq=seq_ctx, row_block0=0)
    past = cache_k.shape[2]
    ob_lat = _attention(
        za, q_gain, k_gain, group_ones, batch=n_lat, seq=seq_lat, row_block0=t_ctx // seq_lat,
        rope=_rope_tables(seq_lat),
        cache=(cache_k[:, layer].reshape(n_lat, past, KV_WIDTH), cache_v[:, layer].reshape(n_lat, past, KV_WIDTH)))

    wr_t = w_router[layer].T
    wr_hi = wr_t.astype(BF16)
    wr_lo = (wr_t - wr_hi.astype(F32)).astype(BF16)
    x1, h2, h2_slabs, idx_t, w_t = _post_mixer(
        x_ctx, x_lat, mod3, oa_ctx, oa_lat, ob_ctx, ob_lat, zg, w_branch_a[layer], w_branch_b[layer], w_out[layer],
        ln1_g[layer][None, :], ln1_b[layer][None, :], wr_hi, wr_lo, router_bias[layer][:, None], mod_map)

    routed = _routed_experts(h2_slabs, *_dispatch_lists(idx_t, w_t), w_e_gate[layer], w_e_up[layer], w_e_down[layer])

    y_ctx, y_lat = _final(x1, h2, routed, mod3, w_s_gate[layer], w_s_up[layer], w_s_down[layer],
                          ln2_g[layer][None, :], ln2_b[layer][None, :], mod_map, t_ctx)

    y_prompt = y_ctx.reshape(n_ctx, seq_ctx, d)
    y_sample = y_lat.reshape(n_lat, seq_lat, d)
    new_cache_k = k_ctx.reshape(n_ctx, 1, seq_ctx, N_KV_HEADS, HEAD_DIM)
    new_cache_v = v_ctx.reshape(n_ctx, 1, seq_ctx, N_KV_HEADS, HEAD_DIM)
    new_state = s_ctx[:, None]
    return (y_prompt, y_sample, new_cache_k, new_cache_v, new_state)
```

```python
import functools

import jax
import jax.numpy as jnp
from jax import lax
from jax.experimental import pallas as pl
from jax.experimental.pallas import tpu as pltpu

F32 = jnp.float32
BF16 = jnp.bfloat16

GRID_W = 64
HG_HEADS = 4
HG_DK = 128
HG_DV = 128
HG_WIDTH = HG_HEADS * HG_DK
N_HEADS = 8
N_KV_HEADS = 2
HEAD_DIM = 64
ATT_WIDTH = N_HEADS * HEAD_DIM
KV_WIDTH = N_KV_HEADS * HEAD_DIM
ROPE_THETA = 10000.0
N_EXPERTS = 256
TOP_K = 8
N_GROUPS = 8
TOPK_GROUPS = 4
GROUP_SIZE = N_EXPERTS // N_GROUPS
ROUTED_SCALE = 2.5
NORM_EPS = 1e-6
DEPTH = 1
DEEPNORM_ALPHA = (2 * DEPTH) ** 0.25

LANES = 128
SUBLANES = 8
VMEM_LIMIT = 56 * 1024 * 1024

CAST_CHUNK = 512
TOKEN_TILE = 256
TILE_STRIDE = TOKEN_TILE + 1
HG_CHUNK = 32
HG_HEADS_PER_STEP_CTX = 4
HG_HEADS_PER_STEP_LATENT = 2
ATT_Q_BLOCK = 128
MOE_BLOCK = 320
ROW_STRIDE = MOE_BLOCK + 1
LIST_LEN = 512
LIST_ROWS = LIST_LEN // LANES
ROW_GROUP = 8
ROW_SEGMENT = 32
MAX_BLOCKS_PER_STEP = 4
COMBINE_BLOCKS_PER_STEP = (10, 4, 2, 1)
WEIGHT_BUFFERS = 3


def _sigmoid(x):
    return 1.0 / (1.0 + jnp.exp(-x))


def _silu(x):
    return x * _sigmoid(x)


def _params(sem=None):
    return pltpu.CompilerParams(dimension_semantics=sem, vmem_limit_bytes=VMEM_LIMIT)


def _resident(shape):
    return pl.BlockSpec(shape, lambda i: (0,) * len(shape), pipeline_mode=pl.Buffered(1))


def _cast_weight_once(src_ref, dst_ref):
    @pl.when(pl.program_id(0) == 0)
    def _():
        width = src_ref.shape[1]
        for lo in range(0, width, CAST_CHUNK):
            hi = min(lo + CAST_CHUNK, width)
            dst_ref[:, lo:hi] = src_ref[:, lo:hi].astype(dst_ref.dtype)


def _mod_kernel(c_ref, w_ref, b_ref, o_ref):
    s = _silu(c_ref[...]).astype(BF16)
    o_ref[...] = jnp.dot(s, w_ref[...].astype(BF16), preferred_element_type=F32) + b_ref[...]


def _modulation(cond, w_mod, b_mod):
    n, d = cond.shape
    width = w_mod.shape[1]
    tn = width // 4
    return pl.pallas_call(
        _mod_kernel,
        grid=(4,),
        in_specs=[
            pl.BlockSpec((n, d), lambda j: (0, 0)),
            pl.BlockSpec((d, tn), lambda j: (0, j)),
            pl.BlockSpec((1, tn), lambda j: (0, j)),
        ],
        out_specs=pl.BlockSpec((n, tn), lambda j: (0, j)),
        out_shape=jax.ShapeDtypeStruct((n, width), F32),
        compiler_params=_params(("arbitrary",)),
    )(cond, w_mod, b_mod)


def _inproj_kernel(xc_ref, xl_ref, mod_ref, w_ref, lb_ref, zh_ref, zf_ref, za_ref, zg_ref, wb_ref,
                   *, d, n_ctx_tiles):
    _cast_weight_once(w_ref, wb_ref)
    shift = mod_ref[0, :, 0:d]
    scale = mod_ref[0, :, d:2 * d]
    x = _pick(pl.program_id(0) < n_ctx_tiles, xc_ref, xl_ref)
    h = (x * (1.0 + scale) + shift).astype(BF16)

    def proj(lo, hi):
        return jnp.dot(h, wb_ref[:, lo:hi], preferred_element_type=F32)

    w = HG_WIDTH
    zh_ref[:, 0:w] = _silu(proj(0, w)).astype(zh_ref.dtype)
    zh_ref[:, w:2 * w] = proj(w, 2 * w).astype(zh_ref.dtype)
    for i in range(2):
        lb = lb_ref[i:i + 1, :]
        zf_ref[:, i * w:(i + 1) * w] = lb + (1.0 - lb) * _sigmoid(proj((2 + i) * w, (3 + i) * w))
    zh_ref[:, 2 * w:3 * w] = _silu(proj(4 * w, 5 * w)).astype(zh_ref.dtype)
    a0 = 5 * w
    a1 = a0 + ATT_WIDTH + 2 * KV_WIDTH
    za_ref[...] = proj(a0, a1)
    for i in range(4):
        lo = a1 + i * (d // 2)
        zg_ref[:, i * (d // 2):(i + 1) * (d // 2)] = _sigmoid(proj(lo, lo + d // 2)).astype(zg_ref.dtype)


def _mod_row_map(n_ctx_tiles, tiles_per_latent):
    def index_map(i):
        row = jnp.where(i < n_ctx_tiles, 0, 1 + (i - n_ctx_tiles) // tiles_per_latent)
        return (row, 0, 0)
    return index_map


def _pair_specs(width, n_ctx_tiles):
    return [pl.BlockSpec((TOKEN_TILE, width), lambda i: (jnp.minimum(i, n_ctx_tiles - 1), 0)),
            pl.BlockSpec((TOKEN_TILE, width), lambda i: (jnp.maximum(i - n_ctx_tiles, 0), 0))]


def _pick(is_ctx, ctx_ref, lat_ref):
    return jnp.where(is_ctx, ctx_ref[...], lat_ref[...])


def _input_projection(x_ctx, x_lat, mod3, w_in, lb, mod_map):
    d = x_ctx.shape[1]
    n_ctx_tiles = x_ctx.shape[0] // TOKEN_TILE
    t = x_ctx.shape[0] + x_lat.shape[0]
    width = w_in.shape[1]
    zh_w = 3 * HG_WIDTH
    zf_w = 2 * HG_WIDTH
    za_w = ATT_WIDTH + 2 * KV_WIDTH
    zg_w = 2 * d
    assert width == zh_w + zf_w + za_w + zg_w
    tm = TOKEN_TILE
    return pl.pallas_call(
        functools.partial(_inproj_kernel, d=d, n_ctx_tiles=n_ctx_tiles),
        grid=(t // tm,),
        in_specs=_pair_specs(d, n_ctx_tiles) + [
            pl.BlockSpec((1, 1, mod3.shape[2]), mod_map),
            _resident((d, width)),
            pl.BlockSpec((2, HG_WIDTH), lambda i: (0, 0)),
        ],
        out_specs=[
            pl.BlockSpec((tm, zh_w), lambda i: (i, 0)),
            pl.BlockSpec((tm, zf_w), lambda i: (i, 0)),
            pl.BlockSpec((tm, za_w), lambda i: (i, 0)),
            pl.BlockSpec((tm, zg_w), lambda i: (i, 0)),
        ],
        out_shape=[
            jax.ShapeDtypeStruct((t, zh_w), BF16),
            jax.ShapeDtypeStruct((t, zf_w), F32),
            jax.ShapeDtypeStruct((t, za_w), F32),
            jax.ShapeDtypeStruct((t, zg_w), BF16),
        ],
        scratch_shapes=[pltpu.VMEM((d, width), BF16)],
        compiler_params=_params(("arbitrary",)),
    )(x_ctx, x_lat, mod3, w_in, lb)


def _hgrn_kernel(*refs, seq, has_state, heads):
    if has_state:
        q_ref, v_ref, ff_ref, fb_ref, gs_ref, ng_ref, s0_ref, o_ref, sout_ref = refs
    else:
        q_ref, v_ref, ff_ref, fb_ref, gs_ref, ng_ref, o_ref, sout_ref = refs
        s0_ref = None
    for hd in range(heads):
        _hgrn_head(q_ref, v_ref, ff_ref, fb_ref, gs_ref, ng_ref, s0_ref, o_ref, sout_ref, hd, seq)


def _hgrn_head(q_ref, v_ref, ff_ref, fb_ref, gs_ref, ng_ref, s0_ref, o_ref, sout_ref, hd, seq):
    has_state = s0_ref is not None
    cols = slice(hd * HG_DK, (hd + 1) * HG_DK)
    c = HG_CHUNK
    n = seq // c
    q3 = q_ref[:, cols].astype(F32).reshape(n, c, HG_DK)
    v3 = v_ref[:, cols].reshape(n, c, HG_DV)
    pos = lax.broadcasted_iota(jnp.int32, (seq, HG_DK), 0) % c
    t_idx = lax.broadcasted_iota(jnp.int32, (c, c), 0)
    s_idx = lax.broadcasted_iota(jnp.int32, (c, c), 1)
    o_sum = None
    for direction, f_ref in enumerate((ff_ref, fb_ref)):
        reverse = direction == 1
        f = f_ref[:, cols]
        k3 = (1.0 - f).reshape(n, c, HG_DK)
        b = jnp.log(f)
        step = 1
        while step < c:
            if reverse:
                b = b + jnp.where(pos < c - step, pltpu.roll(b, seq - step, axis=0), 0.0)
            else:
                b = b + jnp.where(pos >= step, pltpu.roll(b, step, axis=0), 0.0)
            step *= 2
        b3 = b.reshape(n, c, HG_DK)
        edge = b3[:, 0:1, :] if reverse else b3[:, c - 1:c, :]
        mid = b3[:, c // 2:c // 2 + 1, :]
        q_rel = q3 * jnp.exp(b3 - mid)
        k_rel = k3 * jnp.exp(mid - b3)
        q_mid = q_rel.astype(BF16)
        k_mid = k_rel.astype(BF16)
        q_in = (q_rel * jnp.exp(mid)).astype(BF16)
        k_end = (k_rel * jnp.exp(edge - mid)).astype(BF16)
        scores = jnp.einsum('ntd,nsd->nts', q_mid, k_mid, preferred_element_type=F32)
        keep = (s_idx >= t_idx) if reverse else (s_idx <= t_idx)
        scores = jnp.where(keep[None], scores, 0.0).astype(BF16)
        o_intra = jnp.einsum('nts,nsv->ntv', scores, v3, preferred_element_type=F32)
        upd = jnp.einsum('nsv,nsd->nvd', v3, k_end, preferred_element_type=F32)
        dec = jnp.exp(edge)
        if has_state:
            st = s0_ref[0, direction, hd].T
        else:
            st = jnp.zeros((HG_DV, HG_DK), F32)
        before = [None] * n
        for ci in (range(n - 1, -1, -1) if reverse else range(n)):
            before[ci] = st.astype(BF16)
            st = st * dec[ci] + upd[ci]
        st_before = jnp.stack(before, axis=0)
        o_inter = jnp.einsum('ntd,nvd->ntv', q_in, st_before, preferred_element_type=F32)
        o_dir = (o_intra + o_inter).reshape(seq, HG_DV)
        o_sum = o_dir if o_sum is None else o_sum + o_dir
        sout_ref[0, direction, hd] = st.T
    ms = jnp.mean(o_sum * o_sum, axis=-1, keepdims=True)
    o = o_sum * lax.rsqrt(ms + NORM_EPS) * ng_ref[...]
    o_ref[:, cols] = (o * gs_ref[:, cols].astype(F32)).astype(o_ref.dtype)


def _hgrn_scan(zh, zf, norm_g, s0, *, batch, seq, row_block0, heads):
    has_state = s0 is not None
    h = HG_HEADS
    groups = h // heads
    width = heads * HG_DK

    def col(section):
        return pl.BlockSpec((seq, width), lambda b, j: (row_block0 + b, section * groups + j))

    in_specs = [col(0), col(1), col(0), col(1), col(2), pl.BlockSpec((1, HG_DV), lambda b, j: (0, 0))]
    args = [zh, zh, zf, zf, zh, norm_g]
    state_spec = pl.BlockSpec((1, 2, heads, HG_DK, HG_DV), lambda b, j: (b, 0, j, 0, 0))
    if has_state:
        in_specs.append(state_spec)
        args.append(s0)
    return pl.pallas_call(
        functools.partial(_hgrn_kernel, seq=seq, has_state=has_state, heads=heads),
        grid=(batch, groups),
        in_specs=in_specs,
        out_specs=[pl.BlockSpec((seq, width), lambda b, j: (b, j)), state_spec],
        out_shape=[
            jax.ShapeDtypeStruct((batch * seq, h * HG_DV), BF16),
            jax.ShapeDtypeStruct((batch, 2, h, HG_DK, HG_DV), F32),
        ],
        compiler_params=_params(("arbitrary", "arbitrary")),
    )(*args)


def _group_rms_norm(x, group_ones, gain):
    sq = x * x
    hi = sq.astype(BF16)
    lo = (sq - hi.astype(F32)).astype(BF16)
    total = (jnp.dot(hi, group_ones, preferred_element_type=F32)
             + jnp.dot(lo, group_ones, preferred_element_type=F32))
    return x * lax.rsqrt(total * (1.0 / HEAD_DIM) + NORM_EPS) * gain


def _rope(x, cos, sin_signed):
    width = x.shape[1]
    quarter = HEAD_DIM // 4
    lane = lax.broadcasted_iota(jnp.int32, x.shape, 1)
    partner = jnp.where(lane % (2 * quarter) < quarter,
                        pltpu.roll(x, width - quarter, axis=1),
                        pltpu.roll(x, quarter, axis=1))
    return x * cos + partner * sin_signed


def _attn_kernel(*refs, seq, latent):
    if latent:
        za_ref, qg_ref, kg_ref, gm_ref, cos_ref, sin_ref, ck_ref, cv_ref, o_ref = refs
    else:
        za_ref, qg_ref, kg_ref, gm_ref, o_ref, kout_ref, vout_ref = refs
    q = za_ref[:, 0:ATT_WIDTH]
    k = za_ref[:, ATT_WIDTH:ATT_WIDTH + KV_WIDTH]
    v = za_ref[:, ATT_WIDTH + KV_WIDTH:ATT_WIDTH + 2 * KV_WIDTH]
    qn = _group_rms_norm(q, gm_ref[...], qg_ref[...])
    kn = _group_rms_norm(k, gm_ref[0:KV_WIDTH, 0:KV_WIDTH], kg_ref[...])
    if latent:
        qn = _rope(qn, cos_ref[...], sin_ref[...])
        kr = _rope(kn, cos_ref[:, 0:KV_WIDTH], sin_ref[:, 0:KV_WIDTH])
    else:
        kout_ref[...] = kn
        vout_ref[...] = v
        kr = kn
    qb = (qn * (HEAD_DIM ** -0.5)).astype(BF16)
    kb = kr.astype(BF16)
    vb = v.astype(BF16)
    if latent:
        ckb = ck_ref[0].astype(BF16)
        cvb = cv_ref[0].astype(BF16)
    groups = N_HEADS // N_KV_HEADS
    tq = ATT_Q_BLOCK if latent else seq
    nt = (((1,), (1,)), ((), ()))
    for kh in range(N_KV_HEADS):
        ksl = slice(kh * HEAD_DIM, (kh + 1) * HEAD_DIM)
        k_new = kb[:, ksl]
        v_new = vb[:, ksl]
        for blk in range(seq // tq):
            rows = slice(blk * tq, (blk + 1) * tq)
            heads = [kh * groups + g for g in range(groups)]
            q_st = jnp.concatenate([qb[rows, hd * HEAD_DIM:(hd + 1) * HEAD_DIM] for hd in heads], axis=0)
            s_new = lax.dot_general(q_st, k_new, nt, preferred_element_type=F32)
            m = jnp.max(s_new, axis=-1, keepdims=True)
            if latent:
                s_ctx = lax.dot_general(q_st, ckb[:, ksl], nt, preferred_element_type=F32)
                m = jnp.maximum(m, jnp.max(s_ctx, axis=-1, keepdims=True))
            p_new = jnp.exp(s_new - m)
            denom = jnp.sum(p_new, axis=-1, keepdims=True)
            acc = jnp.dot(p_new.astype(BF16), v_new, preferred_element_type=F32)
            if latent:
                p_ctx = jnp.exp(s_ctx - m)
                denom = denom + jnp.sum(p_ctx, axis=-1, keepdims=True)
                acc = acc + jnp.dot(p_ctx.astype(BF16), cvb[:, ksl], preferred_element_type=F32)
            out = acc / denom
            for g in range(0, groups, 2):
                pair = jnp.concatenate([out[g * tq:(g + 1) * tq], out[(g + 1) * tq:(g + 2) * tq]], axis=1)
                lane0 = heads[g] * HEAD_DIM
                o_ref[rows, lane0:lane0 + 2 * HEAD_DIM] = pair.astype(o_ref.dtype)


def _attention(za, q_gain, k_gain, group_ones, *, batch, seq, row_block0, rope=None, cache=None):
    latent = cache is not None
    za_w = za.shape[1]
    in_specs = [
        pl.BlockSpec((seq, za_w), lambda b: (row_block0 + b, 0)),
        pl.BlockSpec((1, ATT_WIDTH), lambda b: (0, 0)),
        pl.BlockSpec((1, KV_WIDTH), lambda b: (0, 0)),
        pl.BlockSpec((ATT_WIDTH, ATT_WIDTH), lambda b: (0, 0)),
    ]
    args = [za, q_gain, k_gain, group_ones]
    o_spec = pl.BlockSpec((seq, ATT_WIDTH), lambda b: (b, 0))
    o_shape = jax.ShapeDtypeStruct((batch * seq, ATT_WIDTH), BF16)
    if latent:
        cos, sin_signed = rope
        ck, cv = cache
        past = ck.shape[1]
        in_specs += [
            pl.BlockSpec((seq, ATT_WIDTH), lambda b: (0, 0)),
            pl.BlockSpec((seq, ATT_WIDTH), lambda b: (0, 0)),
            pl.BlockSpec((1, past, KV_WIDTH), lambda b: (b, 0, 0)),
            pl.BlockSpec((1, past, KV_WIDTH), lambda b: (b, 0, 0)),
        ]
        args += [cos, sin_signed, ck, cv]
        out_specs = o_spec
        out_shape = o_shape
    else:
        kv_spec = pl.BlockSpec((seq, KV_WIDTH), lambda b: (b, 0))
        kv_shape = jax.ShapeDtypeStruct((batch * seq, KV_WIDTH), F32)
        out_specs = [o_spec, kv_spec, kv_spec]
        out_shape = [o_shape, kv_shape, kv_shape]
    return pl.pallas_call(
        functools.partial(_attn_kernel, seq=seq, latent=latent),
        grid=(batch,),
        in_specs=in_specs,
        out_specs=out_specs,
        out_shape=out_shape,
        compiler_params=_params(("arbitrary",)),
    )(*args)


def _rope_tables(seq):
    quarter = HEAD_DIM // 4
    t = jnp.arange(seq)
    row = (t // GRID_W).astype(F32)
    colp = (t % GRID_W).astype(F32)
    inv_freq = ROPE_THETA ** (-jnp.arange(quarter, dtype=F32) / quarter)
    lane = jnp.arange(HEAD_DIM)
    pos = jnp.where((lane < HEAD_DIM // 2)[None, :], row[:, None], colp[:, None])
    ang = pos * inv_freq[lane % quarter][None, :]
    sign = jnp.where(lane % (2 * quarter) < quarter, -1.0, 1.0)[None, :]
    cos = jnp.tile(jnp.cos(ang), (1, N_HEADS))
    sin_signed = jnp.tile(jnp.sin(ang) * sign, (1, N_HEADS))
    return cos, sin_signed


def _layer_norm(y, g, b):
    mu = jnp.mean(y, axis=-1, keepdims=True)
    yc = y - mu
    var = jnp.mean(yc * yc, axis=-1, keepdims=True)
    return yc * lax.rsqrt(var + NORM_EPS) * g + b


def _rows_to_slabs(tile, st_ref, slab_ref):
    rows, d = tile.shape
    chunks = d // LANES
    for j in range(chunks):
        st_ref[j * TILE_STRIDE:j * TILE_STRIDE + rows, :] = tile[:, j * LANES:(j + 1) * LANES]
    for r in range(rows):
        slab_ref[r * chunks:(r + 1) * chunks, :] = st_ref[pl.ds(r, chunks, stride=TILE_STRIDE), :]


def _slabs_to_rows(slab_ref, st_ref, rows):
    chunks = SUBLANES
    for r in range(rows):
        st_ref[pl.ds(r, chunks, stride=TILE_STRIDE), :] = slab_ref[r * chunks:(r + 1) * chunks, :]
    return jnp.concatenate([st_ref[j * TILE_STRIDE:j * TILE_STRIDE + rows, :] for j in range(chunks)], axis=1)


def _post_mixer_kernel(xc_ref, xl_ref, mod_ref, oac_ref, oal_ref, obc_ref, obl_ref, zg_ref, wa_ref, wb_ref,
                       wo_ref, g_ref, b_ref, wrh_ref, wrl_ref, rb_ref,
                       x1_ref, slab_ref, idx_ref, rw_ref, st_ref, wab_ref, wbb_ref, wob_ref,
                       *, d, n_ctx_tiles):
    for src, dst in ((wa_ref, wab_ref), (wb_ref, wbb_ref), (wo_ref, wob_ref)):
        _cast_weight_once(src, dst)
    is_ctx = pl.program_id(0) < n_ctx_tiles
    gate1 = mod_ref[0, :, 2 * d:3 * d]
    shift2 = mod_ref[0, :, 3 * d:4 * d]
    scale2 = mod_ref[0, :, 4 * d:5 * d]
    branch_a = jnp.dot(_pick(is_ctx, oac_ref, oal_ref), wab_ref[...], preferred_element_type=F32)
    branch_b = jnp.dot(_pick(is_ctx, obc_ref, obl_ref), wbb_ref[...], preferred_element_type=F32)
    merged = zg_ref[:, 0:d].astype(F32) * branch_a + zg_ref[:, d:2 * d].astype(F32) * branch_b
    mix = jnp.dot(merged.astype(BF16), wob_ref[...], preferred_element_type=F32)
    x = _pick(is_ctx, xc_ref, xl_ref)
    x1 = _layer_norm(DEEPNORM_ALPHA * x + gate1 * mix, g_ref[...], b_ref[...])
    x1_ref[...] = x1
    h2 = x1 * (1.0 + scale2) + shift2
    _rows_to_slabs(h2, st_ref, slab_ref)
    idx_ref[...], rw_ref[...] = _route_tile(h2, wrh_ref[...], wrl_ref[...], rb_ref[...])


def _post_mixer(x_ctx, x_lat, mod3, oa_ctx, oa_lat, ob_ctx, ob_lat, zg, w_a, w_b, w_o, ln_g, ln_b,
                wr_hi, wr_lo, router_bias, mod_map):
    d = x_ctx.shape[1]
    n_ctx_tiles = x_ctx.shape[0] // TOKEN_TILE
    t = x_ctx.shape[0] + x_lat.shape[0]
    tm = TOKEN_TILE
    row = lambda i: (i, 0)
    full = lambda i: (0, 0)
    return pl.pallas_call(
        functools.partial(_post_mixer_kernel, d=d, n_ctx_tiles=n_ctx_tiles),
        grid=(t // tm,),
        in_specs=_pair_specs(d, n_ctx_tiles) + [
            pl.BlockSpec((1, 1, mod3.shape[2]), mod_map),
        ] + _pair_specs(oa_ctx.shape[1], n_ctx_tiles) + _pair_specs(ob_ctx.shape[1], n_ctx_tiles) + [
            pl.BlockSpec((tm, 2 * d), row),
            _resident(w_a.shape),
            _resident(w_b.shape),
            _resident(w_o.shape),
            pl.BlockSpec((1, d), full),
            pl.BlockSpec((1, d), full),
            pl.BlockSpec((N_EXPERTS, d), full),
            pl.BlockSpec((N_EXPERTS, d), full),
            pl.BlockSpec((N_EXPERTS, 1), full),
        ],
        out_specs=[pl.BlockSpec((tm, d), row),
                   pl.BlockSpec((tm * SUBLANES, LANES), row),
                   pl.BlockSpec((TOP_K, tm), lambda i: (0, i)), pl.BlockSpec((TOP_K, tm), lambda i: (0, i))],
        out_shape=[jax.ShapeDtypeStruct((t, d), F32),
                   jax.ShapeDtypeStruct((t * SUBLANES, LANES), F32),
                   jax.ShapeDtypeStruct((TOP_K, t), jnp.int32), jax.ShapeDtypeStruct((TOP_K, t), F32)],
        scratch_shapes=[pltpu.VMEM((SUBLANES * TILE_STRIDE, LANES), F32),
                        pltpu.VMEM(w_a.shape, BF16), pltpu.VMEM(w_b.shape, BF16), pltpu.VMEM(w_o.shape, BF16)],
        compiler_params=_params(("arbitrary",)),
    )(x_ctx, x_lat, mod3, oa_ctx, oa_lat, ob_ctx, ob_lat, zg, w_a, w_b, w_o, ln_g, ln_b, wr_hi, wr_lo, router_bias)


def _first_index_of_max(x, idx, sentinel):
    m = jnp.max(x, axis=0, keepdims=True)
    first = jnp.min(jnp.where(x == m, idx, sentinel), axis=0, keepdims=True)
    return m, first


def _route_tile(h, wh, wl, bias):
    hh = h.astype(BF16)
    hl = (h - hh.astype(F32)).astype(BF16)
    nt = (((1,), (1,)), ((), ()))
    logits = (lax.dot_general(wh, hh, nt, preferred_element_type=F32)
              + lax.dot_general(wh, hl, nt, preferred_element_type=F32)
              + lax.dot_general(wl, hh, nt, preferred_element_type=F32))
    scores = _sigmoid(logits)
    sel = scores + bias
    tm = sel.shape[1]
    neg = -jnp.inf
    gidx = lax.broadcasted_iota(jnp.int32, (GROUP_SIZE, tm), 0)
    group_scores = []
    for g in range(N_GROUPS):
        sg = sel[g * GROUP_SIZE:(g + 1) * GROUP_SIZE, :]
        m1, first = _first_index_of_max(sg, gidx, GROUP_SIZE)
        m2 = jnp.max(jnp.where(gidx == first, neg, sg), axis=0, keepdims=True)
        group_scores.append(m1 + m2)
    gs = jnp.concatenate(group_scores, axis=0)
    nidx = lax.broadcasted_iota(jnp.int32, (N_GROUPS, tm), 0)
    chosen = jnp.zeros((N_GROUPS, tm), jnp.bool_)
    for _ in range(TOPK_GROUPS):
        _, first = _first_index_of_max(gs, nidx, N_GROUPS)
        hit = nidx == first
        chosen = jnp.logical_or(chosen, hit)
        gs = jnp.where(hit, neg, gs)
    masked = jnp.concatenate(
        [jnp.where(chosen[g:g + 1, :], sel[g * GROUP_SIZE:(g + 1) * GROUP_SIZE, :], neg) for g in range(N_GROUPS)],
        axis=0)
    eidx = lax.broadcasted_iota(jnp.int32, (N_EXPERTS, tm), 0)
    picks, weights = [], []
    for _ in range(TOP_K):
        _, first = _first_index_of_max(masked, eidx, N_EXPERTS)
        hit = eidx == first
        picks.append(first)
        weights.append(jnp.sum(jnp.where(hit, scores, 0.0), axis=0, keepdims=True))
        masked = jnp.where(hit, neg, masked)
    wk = jnp.concatenate(weights, axis=0)
    return jnp.concatenate(picks, axis=0), wk / jnp.sum(wk, axis=0, keepdims=True) * ROUTED_SCALE


def _dispatch_lists(idx_t, w_t):
    k, t = idx_t.shape
    s = k * t
    n_blocks = s // MOE_BLOCK + N_EXPERTS
    flat_tok = jnp.arange(s, dtype=jnp.int32) % t
    sorted_e, sorted_tok, sorted_w = lax.sort((idx_t.reshape(s), flat_tok, w_t.reshape(s)), num_keys=1)
    experts = jnp.arange(N_EXPERTS + 1, dtype=jnp.int32)
    start = jnp.sum(sorted_e[None, :] < experts[:, None], axis=1, dtype=jnp.int32)
    counts = start[1:] - start[:-1]
    blocks_per_e = (counts + MOE_BLOCK - 1) // MOE_BLOCK
    block_end = jnp.cumsum(blocks_per_e)
    n_used = block_end[-1]
    blk = jnp.arange(n_blocks, dtype=jnp.int32)
    block_e = jnp.sum(block_end[None, :] <= blk[:, None], axis=1, dtype=jnp.int32)
    block_e = jnp.minimum(block_e, N_EXPERTS - 1)
    onehot = (block_e[:, None] == experts[None, :N_EXPERTS]).astype(jnp.int32)
    pick = lambda a: jnp.sum(onehot * a[None, :], axis=1)
    offset = (blk - (pick(block_end) - pick(blocks_per_e))) * MOE_BLOCK
    row_start = pick(start[:-1]) + offset
    n_rows = jnp.clip(pick(counts) - offset, 0, MOE_BLOCK)
    used = blk < n_used
    last_e = jnp.sum(jnp.where(blk == n_used - 1, block_e, 0))
    block_e = jnp.where(used, block_e, last_e)
    row_start = jnp.where(used, row_start, 0)
    n_rows = jnp.where(used, n_rows, 0)
    return block_e, row_start, n_rows, (n_used - 1).reshape(1), sorted_tok, sorted_w


def _block_lists_kernel(rs_ref, nr_ref, tok_ref, w_ref, off_ref, wout_ref, *, t):
    lane = lax.broadcasted_iota(jnp.int32, (LIST_ROWS, LANES), 1)
    row = lax.broadcasted_iota(jnp.int32, (LIST_ROWS, LANES), 0)

    def one_block(b, carry):
        start = rs_ref[b]
        n_rows = nr_ref[b]
        q = start // LANES
        lane0 = start % LANES
        q8 = pl.multiple_of((q // SUBLANES) * SUBLANES, SUBLANES)

        def window(ref):
            a = ref[pl.ds(q8, 2 * SUBLANES), :]
            a = pltpu.roll(a, 2 * SUBLANES - (q - q8), axis=0)
            lo = pltpu.roll(a[0:LIST_ROWS], LANES - lane0, axis=1)
            hi = pltpu.roll(a[1:LIST_ROWS + 1], LANES - lane0, axis=1)
            return jnp.where(lane < LANES - lane0, lo, hi)

        valid = row * LANES + lane < n_rows
        off_ref[b] = jnp.where(valid, window(tok_ref) * SUBLANES, t * SUBLANES)
        wout_ref[b] = jnp.where(valid, window(w_ref), 0.0)
        return carry

    n_blocks = off_ref.shape[0]
    unroll = next(u for u in (4, 2, 1) if n_blocks % u == 0)

    def several_blocks(i, carry):
        for u in range(unroll):
            one_block(i * unroll + u, carry)
        return carry

    lax.fori_loop(0, n_blocks // unroll, several_blocks, 0)


def _block_lists(row_start, n_rows, sorted_tok, sorted_w, t):
    n_blocks = row_start.shape[0]
    n_slots = sorted_tok.shape[0]
    list_rows = n_slots // LANES + 2 * SUBLANES
    pad = list_rows * LANES - n_slots
    tok2d = jnp.pad(sorted_tok, (0, pad)).reshape(list_rows, LANES)
    w2d = jnp.pad(sorted_w, (0, pad)).reshape(list_rows, LANES)
    full = lambda i, rs, nr: (0, 0)
    blk = pl.BlockSpec((n_blocks, LIST_ROWS, LANES), lambda i, rs, nr: (0, 0, 0))
    off, wts = pl.pallas_call(
        functools.partial(_block_lists_kernel, t=t),
        grid_spec=pltpu.PrefetchScalarGridSpec(
            num_scalar_prefetch=2,
            grid=(1,),
            in_specs=[pl.BlockSpec((list_rows, LANES), full), pl.BlockSpec((list_rows, LANES), full)],
            out_specs=[blk, blk],
        ),
        out_shape=[jax.ShapeDtypeStruct((n_blocks, LIST_ROWS, LANES), jnp.int32),
                   jax.ShapeDtypeStruct((n_blocks, LIST_ROWS, LANES), F32)],
        compiler_params=_params(("arbitrary",)),
    )(row_start, n_rows, tok2d, w2d)
    return off.reshape(n_blocks * LIST_LEN), wts.reshape(n_blocks * LIST_LEN)


def _expert_kernel(be_ref, nr_ref, last_ref, x_hbm, wg_hbm, wu_hbm, wd_hbm, off_ref, out_ref,
                   xs, xt, wgb, wub, wdb, sem, wsem):
    step = pl.program_id(0)
    per_step = out_ref.shape[0] // MOE_BLOCK
    chunks = wg_hbm.shape[1] // LANES
    last = last_ref[0]
    ahead = WEIGHT_BUFFERS - 1

    def weight_copies(blk):
        e = be_ref[blk]
        slot = blk % WEIGHT_BUFFERS
        return [pltpu.make_async_copy(src.at[e], dst.at[slot], wsem.at[slot])
                for src, dst in ((wg_hbm, wgb), (wu_hbm, wub), (wd_hbm, wdb))]

    @pl.when(step == 0)
    def _load():
        for blk in range(ahead):
            @pl.when(blk <= last)
            def _(blk=blk):
                for cp in weight_copies(blk):
                    cp.start()
        rows = x_hbm.shape[0]
        cp = pltpu.make_async_copy(x_hbm, xs.at[pl.ds(0, rows)], sem)
        cp.start()
        cp.wait()
        xs[pl.ds(rows, SUBLANES), :] = jnp.zeros((SUBLANES, LANES), F32)
        xt[...] = jnp.zeros(xt.shape, F32)

    def one_block(g):
        b = step * per_step + g
        n_rows = nr_ref[b]

        @pl.when(b + ahead <= last)
        def _prefetch():
            for cp in weight_copies(b + ahead):
                cp.start()

        @pl.when(b <= last)
        def _block():
            slot = b % WEIGHT_BUFFERS
            for cp in weight_copies(b):
                cp.wait()
            for seg in range(MOE_BLOCK // ROW_SEGMENT):
                @pl.when(n_rows > seg * ROW_SEGMENT)
                def _gather(seg=seg):
                    for r in range(seg * ROW_SEGMENT, (seg + 1) * ROW_SEGMENT):
                        src = pl.multiple_of(off_ref[g * LIST_LEN + r], SUBLANES)
                        xt[pl.ds(r, chunks, stride=ROW_STRIDE), :] = xs[pl.ds(src, SUBLANES), :]

            x = jnp.concatenate(
                [xt[j * ROW_STRIDE:j * ROW_STRIDE + MOE_BLOCK, :] for j in range(chunks)], axis=1).astype(BF16)
            gate = jnp.dot(x, wgb[slot].astype(BF16), preferred_element_type=F32)
            up = jnp.dot(x, wub[slot].astype(BF16), preferred_element_type=F32)
            hidden = (_silu(gate) * up).astype(BF16)
            out = jnp.dot(hidden, wdb[slot].astype(BF16), preferred_element_type=F32)
            out_ref[g * MOE_BLOCK:(g + 1) * MOE_BLOCK, :] = out.astype(out_ref.dtype)

    for g in range(per_step):
        one_block(g)


def _combine_kernel(nr_ref, last_ref, o_ref, off_ref, w_ref, y_hbm, ys, ot, sem, *, t, n_steps):
    step = pl.program_id(0)
    per_step = o_ref.shape[0] // MOE_BLOCK
    rows = t * SUBLANES
    chunks = o_ref.shape[1] // LANES
    group = ROW_GROUP

    @pl.when(step == 0)
    def _init():
        ys[...] = jnp.zeros(ys.shape, F32)

    def one_block(g):
        n_rows = nr_ref[step * per_step + g]
        row0 = g * MOE_BLOCK
        slot0 = g * LIST_LEN

        @pl.when(n_rows > 0)
        def _block():
            for j in range(chunks):
                ot[j * ROW_STRIDE:j * ROW_STRIDE + MOE_BLOCK, :] = (
                    o_ref[row0:row0 + MOE_BLOCK, j * LANES:(j + 1) * LANES].astype(F32))
            for seg in range(MOE_BLOCK // ROW_SEGMENT):
                @pl.when(n_rows > seg * ROW_SEGMENT)
                def _scatter(seg=seg):
                    for r0 in range(seg * ROW_SEGMENT, (seg + 1) * ROW_SEGMENT, group):
                        dst = [pl.multiple_of(off_ref[slot0 + r0 + i], SUBLANES) for i in range(group)]
                        vals = [ys[pl.ds(dst[i], SUBLANES), :]
                                + w_ref[slot0 + r0 + i] * ot[pl.ds(r0 + i, chunks, stride=ROW_STRIDE), :]
                                for i in range(group)]
                        for i in range(group):
                            ys[pl.ds(dst[i], SUBLANES), :] = vals[i]

    for g in range(per_step):
        one_block(g)

    @pl.when(step == n_steps - 1)
    def _store():
        cp = pltpu.make_async_copy(ys.at[pl.ds(0, rows)], y_hbm, sem)
        cp.start()
        cp.wait()


def _routed_experts(h2_rows, block_e, row_start, n_rows, last_block, sorted_tok, sorted_w, w_gate, w_up, w_down):
    rows, lanes = h2_rows.shape
    t = rows // SUBLANES
    n_blocks = block_e.shape[0]
    d, e_dim = w_gate.shape[1], w_gate.shape[2]
    assert d == SUBLANES * LANES and lanes == LANES
    offsets, weights = _block_lists(row_start, n_rows, sorted_tok, sorted_w, t)
    per_step = next(p for p in range(MAX_BLOCKS_PER_STEP, 0, -1) if n_blocks % p == 0)
    n_steps = n_blocks // per_step

    def used_step(s, last):
        return jnp.minimum(s, last[0] // per_step)

    staging = pltpu.VMEM((SUBLANES * ROW_STRIDE, LANES), F32)
    resident = pltpu.VMEM((rows + SUBLANES, LANES), F32)
    expert_out = pl.pallas_call(
        _expert_kernel,
        grid_spec=pltpu.PrefetchScalarGridSpec(
            num_scalar_prefetch=3,
            grid=(n_steps,),
            in_specs=[
                pl.BlockSpec(memory_space=pl.ANY),
                pl.BlockSpec(memory_space=pl.ANY),
                pl.BlockSpec(memory_space=pl.ANY),
                pl.BlockSpec(memory_space=pl.ANY),
                pl.BlockSpec((per_step * LIST_LEN,), lambda s, be, nr, last: (used_step(s, last),),
                             memory_space=pltpu.SMEM),
            ],
            out_specs=pl.BlockSpec((per_step * MOE_BLOCK, d), lambda s, be, nr, last: (used_step(s, last), 0)),
            scratch_shapes=[
                resident,
                staging,
                pltpu.VMEM((WEIGHT_BUFFERS, d, e_dim), F32),
                pltpu.VMEM((WEIGHT_BUFFERS, d, e_dim), F32),
                pltpu.VMEM((WEIGHT_BUFFERS, e_dim, d), F32),
                pltpu.SemaphoreType.DMA(()),
                pltpu.SemaphoreType.DMA((WEIGHT_BUFFERS,)),
            ],
        ),
        out_shape=jax.ShapeDtypeStruct((n_blocks * MOE_BLOCK, d), BF16),
        compiler_params=_params(("arbitrary",)),
    )(block_e, n_rows, last_block, h2_rows, w_gate, w_up, w_down, offsets)
    per_step = next(p for p in COMBINE_BLOCKS_PER_STEP if n_blocks % p == 0)
    n_steps = n_blocks // per_step
    list_spec = pl.BlockSpec((per_step * LIST_LEN,), lambda s, nr, last: (used_step(s, last),),
                             memory_space=pltpu.SMEM)
    return pl.pallas_call(
        functools.partial(_combine_kernel, t=t, n_steps=n_steps),
        grid_spec=pltpu.PrefetchScalarGridSpec(
            num_scalar_prefetch=2,
            grid=(n_steps,),
            in_specs=[pl.BlockSpec((per_step * MOE_BLOCK, d), lambda s, nr, last: (used_step(s, last), 0)),
                      list_spec, list_spec],
            out_specs=pl.BlockSpec(memory_space=pl.ANY),
            scratch_shapes=[resident, staging, pltpu.SemaphoreType.DMA(())],
        ),
        out_shape=jax.ShapeDtypeStruct((rows, LANES), F32),
        compiler_params=_params(("arbitrary",)),
    )(n_rows, last_block, expert_out, offsets, weights)


def _final_kernel(x1_ref, routed_ref, mod_ref, wg_ref, wu_ref, wd_ref, g_ref, b_ref,
                  yc_ref, yl_ref, st_ref, wgb_ref, wub_ref, wdb_ref, *, d, n_ctx_tiles):
    for src, dst in ((wg_ref, wgb_ref), (wu_ref, wub_ref), (wd_ref, wdb_ref)):
        _cast_weight_once(src, dst)
    is_ctx = pl.program_id(0) < n_ctx_tiles
    shift2 = mod_ref[0, :, 3 * d:4 * d]
    scale2 = mod_ref[0, :, 4 * d:5 * d]
    gate2 = mod_ref[0, :, 5 * d:6 * d]
    x1 = x1_ref[...]
    h = (x1 * (1.0 + scale2) + shift2).astype(BF16)
    gate = jnp.dot(h, wgb_ref[...], preferred_element_type=F32)
    up = jnp.dot(h, wub_ref[...], preferred_element_type=F32)
    shared = jnp.dot((_silu(gate) * up).astype(BF16), wdb_ref[...], preferred_element_type=F32)
    ffn = _slabs_to_rows(routed_ref, st_ref, h.shape[0]) + shared
    y = _layer_norm(DEEPNORM_ALPHA * x1 + gate2 * ffn, g_ref[...], b_ref[...])

    @pl.when(is_ctx)
    def _():
        yc_ref[...] = y

    @pl.when(jnp.logical_not(is_ctx))
    def _():
        yl_ref[...] = y


def _final(x1, routed, mod3, w_g, w_u, w_d, ln_g, ln_b, mod_map, t_ctx):
    t, d = x1.shape
    t_lat = t - t_ctx
    n_ctx_tiles = t_ctx // TOKEN_TILE
    tm = TOKEN_TILE
    row = lambda i: (i, 0)
    full = lambda i: (0, 0)
    return pl.pallas_call(
        functools.partial(_final_kernel, d=d, n_ctx_tiles=n_ctx_tiles),
        grid=(t // tm,),
        in_specs=[
            pl.BlockSpec((tm, d), row),
            pl.BlockSpec((tm * SUBLANES, LANES), row),
            pl.BlockSpec((1, 1, mod3.shape[2]), mod_map),
            _resident(w_g.shape),
            _resident(w_u.shape),
            _resident(w_d.shape),
            pl.BlockSpec((1, d), full),
            pl.BlockSpec((1, d), full),
        ],
        out_specs=_pair_specs(d, n_ctx_tiles),
        out_shape=[jax.ShapeDtypeStruct((t_ctx, d), F32), jax.ShapeDtypeStruct((t_lat, d), F32)],
        scratch_shapes=[pltpu.VMEM((SUBLANES * TILE_STRIDE, LANES), F32),
                        pltpu.VMEM(w_g.shape, BF16), pltpu.VMEM(w_u.shape, BF16), pltpu.VMEM(w_d.shape, BF16)],
        compiler_params=_params(("arbitrary",)),
    )(x1, routed, mod3, w_g, w_u, w_d, ln_g, ln_b)


def kernel(x_prompt, x_sample, cache_k, cache_v, state_hgrn, c, c_ctx, w_mod, b_mod, w_in, hg_lb, hg_norm_g, q_norm_g, k_norm_g, w_branch_a, w_branch_b, w_out, ln1_g, ln1_b, w_router, router_bias, w_e_gate, w_e_up, w_e_down, w_s_gate, w_s_up, w_s_down, ln2_g, ln2_b):
    assert w_mod.shape[0] == DEPTH
    n_ctx, seq_ctx, d = x_prompt.shape
    n_lat, seq_lat, _ = x_sample.shape
    t_ctx = n_ctx * seq_ctx
    t_lat = n_lat * seq_lat
    assert seq_ctx == TOKEN_TILE and seq_lat % TOKEN_TILE == 0
    layer = 0

    lb = jnp.cumsum(jax.nn.softmax(hg_lb.astype(F32), axis=0), axis=0)[layer]

    cond = jnp.concatenate([c_ctx[None, :], c], axis=0)
    cond = jnp.pad(cond, ((0, (-cond.shape[0]) % SUBLANES), (0, 0)))
    mod = _modulation(cond, w_mod[layer], b_mod[layer][None, :])
    mod3 = mod.reshape(mod.shape[0], 1, mod.shape[1])
    mod_map = _mod_row_map(t_ctx // TOKEN_TILE, seq_lat // TOKEN_TILE)

    x_ctx = x_prompt.reshape(t_ctx, d)
    x_lat = x_sample.reshape(t_lat, d)
    zh, zf, za, zg = _input_projection(x_ctx, x_lat, mod3, w_in[layer], lb, mod_map)

    norm_g = hg_norm_g[layer][None, :]
    oa_ctx, s_ctx = _hgrn_scan(zh, zf, norm_g, None, batch=n_ctx, seq=seq_ctx, row_block0=0,
                               heads=HG_HEADS_PER_STEP_CTX)
    oa_lat, _ = _hgrn_scan(zh, zf, norm_g, state_hgrn[:, layer], batch=n_lat, seq=seq_lat,
                           row_block0=t_ctx // seq_lat, heads=HG_HEADS_PER_STEP_LATENT)

    q_gain = jnp.tile(q_norm_g[layer], N_HEADS)[None, :]
    k_gain = jnp.tile(k_norm_g[layer], N_KV_HEADS)[None, :]
    lane = jnp.arange(ATT_WIDTH)
    group_ones = (lane[:, None] // HEAD_DIM == lane[None, :] // HEAD_DIM).astype(BF16)
    ob_ctx, k_ctx, v_ctx = _attention(za, q_gain, k_gain, group_ones, batch=n_ctx, seq=seq_ctx, row_block0=0)
    past = cache_k.shape[2]
    ob_lat = _attention(
        za, q_gain, k_gain, group_ones, batch=n_lat, seq=seq_lat, row_block0=t_ctx // seq_lat,
        rope=_rope_tables(seq_lat),
        cache=(cache_k[:, layer].reshape(n_lat, past, KV_WIDTH), cache_v[:, layer].reshape(n_lat, past, KV_WIDTH)))

    wr_t = w_router[layer].T
    wr_hi = wr_t.astype(BF16)
    wr_lo = (wr_t - wr_hi.astype(F32)).astype(BF16)
    x1, h2_slabs, idx_t, w_t = _post_mixer(
        x_ctx, x_lat, mod3, oa_ctx, oa_lat, ob_ctx, ob_lat, zg, w_branch_a[layer], w_branch_b[layer], w_out[layer],
        ln1_g[layer][None, :], ln1_b[layer][None, :], wr_hi, wr_lo, router_bias[layer][:, None], mod_map)

    routed = _routed_experts(h2_slabs, *_dispatch_lists(idx_t, w_t), w_e_gate[layer], w_e_up[layer], w_e_down[layer])

    y_ctx, y_lat = _final(x1, routed, mod3, w_s_gate[layer], w_s_up[layer], w_s_down[layer],
                          ln2_g[layer][None, :], ln2_b[layer][None, :], mod_map, t_ctx)

    y_prompt = y_ctx.reshape(n_ctx, seq_ctx, d)
    y_sample = y_lat.reshape(n_lat, seq_lat, d)
    new_cache_k = k_ctx.reshape(n_ctx, 1, seq_ctx, N_KV_HEADS, HEAD_DIM)
    new_cache_v = v_ctx.reshape(n_ctx, 1, seq_ctx, N_KV_HEADS, HEAD_DIM)
    new_state = s_ctx[:, None]
    return (y_prompt, y_sample, new_cache_k, new_cache_v, new_state)
```

```python
import functools

import jax
import jax.numpy as jnp
from jax import lax
from jax.experimental import pallas as pl
from jax.experimental.pallas import tpu as pltpu

F32 = jnp.float32
BF16 = jnp.bfloat16

GRID_W = 64
HG_HEADS = 4
HG_DK = 128
HG_DV = 128
HG_WIDTH = HG_HEADS * HG_DK
N_HEADS = 8
N_KV_HEADS = 2
HEAD_DIM = 64
ATT_WIDTH = N_HEADS * HEAD_DIM
KV_WIDTH = N_KV_HEADS * HEAD_DIM
ROPE_THETA = 10000.0
N_EXPERTS = 256
TOP_K = 8
N_GROUPS = 8
TOPK_GROUPS = 4
GROUP_SIZE = N_EXPERTS // N_GROUPS
ROUTED_SCALE = 2.5
NORM_EPS = 1e-6
DEPTH = 1
DEEPNORM_ALPHA = (2 * DEPTH) ** 0.25

LANES = 128
SUBLANES = 8
VMEM_LIMIT = 56 * 1024 * 1024

CAST_CHUNK = 512
TOKEN_TILE = 256
TILE_STRIDE = TOKEN_TILE + 1
HG_CHUNK = 32
HG_HEADS_PER_STEP_CTX = 4
HG_HEADS_PER_STEP_LATENT = 2
ATT_Q_BLOCK = 128
MOE_BLOCK = 288
BLOCK_COUNT_MULTIPLE = 20
ROW_STRIDE = MOE_BLOCK + 1
LIST_LEN = 512
LIST_ROWS = LIST_LEN // LANES
ROW_GROUP = 8
ROW_SEGMENT = 32
MAX_BLOCKS_PER_STEP = 4
COMBINE_BLOCKS_PER_STEP = (10, 4, 2, 1)
WEIGHT_BUFFERS = 3


def _sigmoid(x):
    return 1.0 / (1.0 + jnp.exp(-x))


def _silu(x):
    return x * _sigmoid(x)


def _params(sem=None):
    return pltpu.CompilerParams(dimension_semantics=sem, vmem_limit_bytes=VMEM_LIMIT)


def _resident(shape):
    return pl.BlockSpec(shape, lambda i: (0,) * len(shape), pipeline_mode=pl.Buffered(1))


def _cast_weight_once(src_ref, dst_ref):
    @pl.when(pl.program_id(0) == 0)
    def _():
        width = src_ref.shape[1]
        for lo in range(0, width, CAST_CHUNK):
            hi = min(lo + CAST_CHUNK, width)
            dst_ref[:, lo:hi] = src_ref[:, lo:hi].astype(dst_ref.dtype)


def _mod_kernel(c_ref, w_ref, b_ref, o_ref):
    s = _silu(c_ref[...]).astype(BF16)
    o_ref[...] = jnp.dot(s, w_ref[...].astype(BF16), preferred_element_type=F32) + b_ref[...]


def _modulation(cond, w_mod, b_mod):
    n, d = cond.shape
    width = w_mod.shape[1]
    tn = width // 4
    return pl.pallas_call(
        _mod_kernel,
        grid=(4,),
        in_specs=[
            pl.BlockSpec((n, d), lambda j: (0, 0)),
            pl.BlockSpec((d, tn), lambda j: (0, j)),
            pl.BlockSpec((1, tn), lambda j: (0, j)),
        ],
        out_specs=pl.BlockSpec((n, tn), lambda j: (0, j)),
        out_shape=jax.ShapeDtypeStruct((n, width), F32),
        compiler_params=_params(("arbitrary",)),
    )(cond, w_mod, b_mod)


def _inproj_kernel(xc_ref, xl_ref, mod_ref, w_ref, lb_ref, zh_ref, zf_ref, za_ref, zg_ref, wb_ref,
                   *, d, n_ctx_tiles):
    _cast_weight_once(w_ref, wb_ref)
    shift = mod_ref[0, :, 0:d]
    scale = mod_ref[0, :, d:2 * d]
    x = _pick(pl.program_id(0) < n_ctx_tiles, xc_ref, xl_ref)
    h = (x * (1.0 + scale) + shift).astype(BF16)

    def proj(lo, hi):
        return jnp.dot(h, wb_ref[:, lo:hi], preferred_element_type=F32)

    w = HG_WIDTH
    zh_ref[:, 0:w] = _silu(proj(0, w)).astype(zh_ref.dtype)
    zh_ref[:, w:2 * w] = proj(w, 2 * w).astype(zh_ref.dtype)
    for i in range(2):
        lb = lb_ref[i:i + 1, :]
        zf_ref[:, i * w:(i + 1) * w] = lb + (1.0 - lb) * _sigmoid(proj((2 + i) * w, (3 + i) * w))
    zh_ref[:, 2 * w:3 * w] = _silu(proj(4 * w, 5 * w)).astype(zh_ref.dtype)
    a0 = 5 * w
    a1 = a0 + ATT_WIDTH + 2 * KV_WIDTH
    za_ref[...] = proj(a0, a1)
    for i in range(4):
        lo = a1 + i * (d // 2)
        zg_ref[:, i * (d // 2):(i + 1) * (d // 2)] = _sigmoid(proj(lo, lo + d // 2)).astype(zg_ref.dtype)


def _mod_row_map(n_ctx_tiles, tiles_per_latent):
    def index_map(i):
        row = jnp.where(i < n_ctx_tiles, 0, 1 + (i - n_ctx_tiles) // tiles_per_latent)
        return (row, 0, 0)
    return index_map


def _pair_specs(width, n_ctx_tiles):
    return [pl.BlockSpec((TOKEN_TILE, width), lambda i: (jnp.minimum(i, n_ctx_tiles - 1), 0)),
            pl.BlockSpec((TOKEN_TILE, width), lambda i: (jnp.maximum(i - n_ctx_tiles, 0), 0))]


def _pick(is_ctx, ctx_ref, lat_ref):
    return jnp.where(is_ctx, ctx_ref[...], lat_ref[...])


def _input_projection(x_ctx, x_lat, mod3, w_in, lb, mod_map):
    d = x_ctx.shape[1]
    n_ctx_tiles = x_ctx.shape[0] // TOKEN_TILE
    t = x_ctx.shape[0] + x_lat.shape[0]
    width = w_in.shape[1]
    zh_w = 3 * HG_WIDTH
    zf_w = 2 * HG_WIDTH
    za_w = ATT_WIDTH + 2 * KV_WIDTH
    zg_w = 2 * d
    assert width == zh_w + zf_w + za_w + zg_w
    tm = TOKEN_TILE
    return pl.pallas_call(
        functools.partial(_inproj_kernel, d=d, n_ctx_tiles=n_ctx_tiles),
        grid=(t // tm,),
        in_specs=_pair_specs(d, n_ctx_tiles) + [
            pl.BlockSpec((1, 1, mod3.shape[2]), mod_map),
            _resident((d, width)),
            pl.BlockSpec((2, HG_WIDTH), lambda i: (0, 0)),
        ],
        out_specs=[
            pl.BlockSpec((tm, zh_w), lambda i: (i, 0)),
            pl.BlockSpec((tm, zf_w), lambda i: (i, 0)),
            pl.BlockSpec((tm, za_w), lambda i: (i, 0)),
            pl.BlockSpec((tm, zg_w), lambda i: (i, 0)),
        ],
        out_shape=[
            jax.ShapeDtypeStruct((t, zh_w), BF16),
            jax.ShapeDtypeStruct((t, zf_w), F32),
            jax.ShapeDtypeStruct((t, za_w), F32),
            jax.ShapeDtypeStruct((t, zg_w), BF16),
        ],
        scratch_shapes=[pltpu.VMEM((d, width), BF16)],
        compiler_params=_params(("arbitrary",)),
    )(x_ctx, x_lat, mod3, w_in, lb)


def _hgrn_kernel(*refs, seq, has_state, heads):
    if has_state:
        q_ref, v_ref, ff_ref, fb_ref, gs_ref, ng_ref, s0_ref, o_ref, sout_ref = refs
    else:
        q_ref, v_ref, ff_ref, fb_ref, gs_ref, ng_ref, o_ref, sout_ref = refs
        s0_ref = None
    for hd in range(heads):
        _hgrn_head(q_ref, v_ref, ff_ref, fb_ref, gs_ref, ng_ref, s0_ref, o_ref, sout_ref, hd, seq)


def _hgrn_head(q_ref, v_ref, ff_ref, fb_ref, gs_ref, ng_ref, s0_ref, o_ref, sout_ref, hd, seq):
    has_state = s0_ref is not None
    cols = slice(hd * HG_DK, (hd + 1) * HG_DK)
    c = HG_CHUNK
    n = seq // c
    q3 = q_ref[:, cols].astype(F32).reshape(n, c, HG_DK)
    v3 = v_ref[:, cols].reshape(n, c, HG_DV)
    pos = lax.broadcasted_iota(jnp.int32, (seq, HG_DK), 0) % c
    t_idx = lax.broadcasted_iota(jnp.int32, (c, c), 0)
    s_idx = lax.broadcasted_iota(jnp.int32, (c, c), 1)
    o_sum = None
    for direction, f_ref in enumerate((ff_ref, fb_ref)):
        reverse = direction == 1
        f = f_ref[:, cols]
        k3 = (1.0 - f).reshape(n, c, HG_DK)
        b = jnp.log(f)
        step = 1
        while step < c:
            if reverse:
                b = b + jnp.where(pos < c - step, pltpu.roll(b, seq - step, axis=0), 0.0)
            else:
                b = b + jnp.where(pos >= step, pltpu.roll(b, step, axis=0), 0.0)
            step *= 2
        b3 = b.reshape(n, c, HG_DK)
        edge = b3[:, 0:1, :] if reverse else b3[:, c - 1:c, :]
        mid = b3[:, c // 2:c // 2 + 1, :]
        q_rel = q3 * jnp.exp(b3 - mid)
        k_rel = k3 * jnp.exp(mid - b3)
        q_mid = q_rel.astype(BF16)
        k_mid = k_rel.astype(BF16)
        q_in = (q_rel * jnp.exp(mid)).astype(BF16)
        k_end = (k_rel * jnp.exp(edge - mid)).astype(BF16)
        scores = jnp.einsum('ntd,nsd->nts', q_mid, k_mid, preferred_element_type=F32)
        keep = (s_idx >= t_idx) if reverse else (s_idx <= t_idx)
        scores = jnp.where(keep[None], scores, 0.0).astype(BF16)
        o_intra = jnp.einsum('nts,nsv->ntv', scores, v3, preferred_element_type=F32)
        upd = jnp.einsum('nsv,nsd->nvd', v3, k_end, preferred_element_type=F32)
        dec = jnp.exp(edge)
        if has_state:
            st = s0_ref[0, direction, hd].T
        else:
            st = jnp.zeros((HG_DV, HG_DK), F32)
        before = [None] * n
        for ci in (range(n - 1, -1, -1) if reverse else range(n)):
            before[ci] = st.astype(BF16)
            st = st * dec[ci] + upd[ci]
        st_before = jnp.stack(before, axis=0)
        o_inter = jnp.einsum('ntd,nvd->ntv', q_in, st_before, preferred_element_type=F32)
        o_dir = (o_intra + o_inter).reshape(seq, HG_DV)
        o_sum = o_dir if o_sum is None else o_sum + o_dir
        sout_ref[0, direction, hd] = st.T
    ms = jnp.mean(o_sum * o_sum, axis=-1, keepdims=True)
    o = o_sum * lax.rsqrt(ms + NORM_EPS) * ng_ref[...]
    o_ref[:, cols] = (o * gs_ref[:, cols].astype(F32)).astype(o_ref.dtype)


def _hgrn_scan(zh, zf, norm_g, s0, *, batch, seq, row_block0, heads):
    has_state = s0 is not None
    h = HG_HEADS
    groups = h // heads
    width = heads * HG_DK

    def col(section):
        return pl.BlockSpec((seq, width), lambda b, j: (row_block0 + b, section * groups + j))

    in_specs = [col(0), col(1), col(0), col(1), col(2), pl.BlockSpec((1, HG_DV), lambda b, j: (0, 0))]
    args = [zh, zh, zf, zf, zh, norm_g]
    state_spec = pl.BlockSpec((1, 2, heads, HG_DK, HG_DV), lambda b, j: (b, 0, j, 0, 0))
    if has_state:
        in_specs.append(state_spec)
        args.append(s0)
    return pl.pallas_call(
        functools.partial(_hgrn_kernel, seq=seq, has_state=has_state, heads=heads),
        grid=(batch, groups),
        in_specs=in_specs,
        out_specs=[pl.BlockSpec((seq, width), lambda b, j: (b, j)), state_spec],
        out_shape=[
            jax.ShapeDtypeStruct((batch * seq, h * HG_DV), BF16),
            jax.ShapeDtypeStruct((batch, 2, h, HG_DK, HG_DV), F32),
        ],
        compiler_params=_params(("arbitrary", "arbitrary")),
    )(*args)


def _group_rms_norm(x, group_ones, gain):
    sq = x * x
    hi = sq.astype(BF16)
    lo = (sq - hi.astype(F32)).astype(BF16)
    total = (jnp.dot(hi, group_ones, preferred_element_type=F32)
             + jnp.dot(lo, group_ones, preferred_element_type=F32))
    return x * lax.rsqrt(total * (1.0 / HEAD_DIM) + NORM_EPS) * gain


def _rope(x, cos, sin_signed):
    width = x.shape[1]
    quarter = HEAD_DIM // 4
    lane = lax.broadcasted_iota(jnp.int32, x.shape, 1)
    partner = jnp.where(lane % (2 * quarter) < quarter,
                        pltpu.roll(x, width - quarter, axis=1),
                        pltpu.roll(x, quarter, axis=1))
    return x * cos + partner * sin_signed


def _attn_kernel(*refs, seq, latent):
    if latent:
        za_ref, qg_ref, kg_ref, gm_ref, cos_ref, sin_ref, ck_ref, cv_ref, o_ref = refs
    else:
        za_ref, qg_ref, kg_ref, gm_ref, o_ref, kout_ref, vout_ref = refs
    q = za_ref[:, 0:ATT_WIDTH]
    k = za_ref[:, ATT_WIDTH:ATT_WIDTH + KV_WIDTH]
    v = za_ref[:, ATT_WIDTH + KV_WIDTH:ATT_WIDTH + 2 * KV_WIDTH]
    qn = _group_rms_norm(q, gm_ref[...], qg_ref[...])
    kn = _group_rms_norm(k, gm_ref[0:KV_WIDTH, 0:KV_WIDTH], kg_ref[...])
    if latent:
        qn = _rope(qn, cos_ref[...], sin_ref[...])
        kr = _rope(kn, cos_ref[:, 0:KV_WIDTH], sin_ref[:, 0:KV_WIDTH])
    else:
        kout_ref[...] = kn
        vout_ref[...] = v
        kr = kn
    qb = (qn * (HEAD_DIM ** -0.5)).astype(BF16)
    kb = kr.astype(BF16)
    vb = v.astype(BF16)
    if latent:
        ckb = ck_ref[0].astype(BF16)
        cvb = cv_ref[0].astype(BF16)
    groups = N_HEADS // N_KV_HEADS
    tq = ATT_Q_BLOCK if latent else seq
    nt = (((1,), (1,)), ((), ()))
    for kh in range(N_KV_HEADS):
        ksl = slice(kh * HEAD_DIM, (kh + 1) * HEAD_DIM)
        k_new = kb[:, ksl]
        v_new = vb[:, ksl]
        for blk in range(seq // tq):
            rows = slice(blk * tq, (blk + 1) * tq)
            heads = [kh * groups + g for g in range(groups)]
            q_st = jnp.concatenate([qb[rows, hd * HEAD_DIM:(hd + 1) * HEAD_DIM] for hd in heads], axis=0)
            s_new = lax.dot_general(q_st, k_new, nt, preferred_element_type=F32)
            m = jnp.max(s_new, axis=-1, keepdims=True)
            if latent:
                s_ctx = lax.dot_general(q_st, ckb[:, ksl], nt, preferred_element_type=F32)
                m = jnp.maximum(m, jnp.max(s_ctx, axis=-1, keepdims=True))
            p_new = jnp.exp(s_new - m)
            denom = jnp.sum(p_new, axis=-1, keepdims=True)
            acc = jnp.dot(p_new.astype(BF16), v_new, preferred_element_type=F32)
            if latent:
                p_ctx = jnp.exp(s_ctx - m)
                denom = denom + jnp.sum(p_ctx, axis=-1, keepdims=True)
                acc = acc + jnp.dot(p_ctx.astype(BF16), cvb[:, ksl], preferred_element_type=F32)
            out = acc / denom
            for g in range(0, groups, 2):
                pair = jnp.concatenate([out[g * tq:(g + 1) * tq], out[(g + 1) * tq:(g + 2) * tq]], axis=1)
                lane0 = heads[g] * HEAD_DIM
                o_ref[rows, lane0:lane0 + 2 * HEAD_DIM] = pair.astype(o_ref.dtype)


def _attention(za, q_gain, k_gain, group_ones, *, batch, seq, row_block0, rope=None, cache=None):
    latent = cache is not None
    za_w = za.shape[1]
    in_specs = [
        pl.BlockSpec((seq, za_w), lambda b: (row_block0 + b, 0)),
        pl.BlockSpec((1, ATT_WIDTH), lambda b: (0, 0)),
        pl.BlockSpec((1, KV_WIDTH), lambda b: (0, 0)),
        pl.BlockSpec((ATT_WIDTH, ATT_WIDTH), lambda b: (0, 0)),
    ]
    args = [za, q_gain, k_gain, group_ones]
    o_spec = pl.BlockSpec((seq, ATT_WIDTH), lambda b: (b, 0))
    o_shape = jax.ShapeDtypeStruct((batch * seq, ATT_WIDTH), BF16)
    if latent:
        cos, sin_signed = rope
        ck, cv = cache
        past = ck.shape[1]
        in_specs += [
            pl.BlockSpec((seq, ATT_WIDTH), lambda b: (0, 0)),
            pl.BlockSpec((seq, ATT_WIDTH), lambda b: (0, 0)),
            pl.BlockSpec((1, past, KV_WIDTH), lambda b: (b, 0, 0)),
            pl.BlockSpec((1, past, KV_WIDTH), lambda b: (b, 0, 0)),
        ]
        args += [cos, sin_signed, ck, cv]
        out_specs = o_spec
        out_shape = o_shape
    else:
        kv_spec = pl.BlockSpec((seq, KV_WIDTH), lambda b: (b, 0))
        kv_shape = jax.ShapeDtypeStruct((batch * seq, KV_WIDTH), F32)
        out_specs = [o_spec, kv_spec, kv_spec]
        out_shape = [o_shape, kv_shape, kv_shape]
    return pl.pallas_call(
        functools.partial(_attn_kernel, seq=seq, latent=latent),
        grid=(batch,),
        in_specs=in_specs,
        out_specs=out_specs,
        out_shape=out_shape,
        compiler_params=_params(("arbitrary",)),
    )(*args)


def _rope_tables(seq):
    quarter = HEAD_DIM // 4
    t = jnp.arange(seq)
    row = (t // GRID_W).astype(F32)
    colp = (t % GRID_W).astype(F32)
    inv_freq = ROPE_THETA ** (-jnp.arange(quarter, dtype=F32) / quarter)
    lane = jnp.arange(HEAD_DIM)
    pos = jnp.where((lane < HEAD_DIM // 2)[None, :], row[:, None], colp[:, None])
    ang = pos * inv_freq[lane % quarter][None, :]
    sign = jnp.where(lane % (2 * quarter) < quarter, -1.0, 1.0)[None, :]
    cos = jnp.tile(jnp.cos(ang), (1, N_HEADS))
    sin_signed = jnp.tile(jnp.sin(ang) * sign, (1, N_HEADS))
    return cos, sin_signed


def _layer_norm(y, g, b):
    mu = jnp.mean(y, axis=-1, keepdims=True)
    yc = y - mu
    var = jnp.mean(yc * yc, axis=-1, keepdims=True)
    return yc * lax.rsqrt(var + NORM_EPS) * g + b


def _rows_to_slabs(tile, st_ref, slab_ref):
    rows, d = tile.shape
    chunks = d // LANES
    for j in range(chunks):
        st_ref[j * TILE_STRIDE:j * TILE_STRIDE + rows, :] = tile[:, j * LANES:(j + 1) * LANES]
    for r in range(rows):
        slab_ref[r * chunks:(r + 1) * chunks, :] = st_ref[pl.ds(r, chunks, stride=TILE_STRIDE), :]


def _slabs_to_rows(slab_ref, st_ref, rows):
    chunks = SUBLANES
    for r in range(rows):
        st_ref[pl.ds(r, chunks, stride=TILE_STRIDE), :] = slab_ref[r * chunks:(r + 1) * chunks, :]
    return jnp.concatenate([st_ref[j * TILE_STRIDE:j * TILE_STRIDE + rows, :] for j in range(chunks)], axis=1)


def _post_mixer_kernel(xc_ref, xl_ref, mod_ref, oac_ref, oal_ref, obc_ref, obl_ref, zg_ref, wa_ref, wb_ref,
                       wo_ref, g_ref, b_ref, wrh_ref, wrl_ref, rb_ref,
                       x1_ref, slab_ref, idx_ref, rw_ref, st_ref, wab_ref, wbb_ref, wob_ref,
                       *, d, n_ctx_tiles):
    for src, dst in ((wa_ref, wab_ref), (wb_ref, wbb_ref), (wo_ref, wob_ref)):
        _cast_weight_once(src, dst)
    is_ctx = pl.program_id(0) < n_ctx_tiles
    gate1 = mod_ref[0, :, 2 * d:3 * d]
    shift2 = mod_ref[0, :, 3 * d:4 * d]
    scale2 = mod_ref[0, :, 4 * d:5 * d]
    branch_a = jnp.dot(_pick(is_ctx, oac_ref, oal_ref), wab_ref[...], preferred_element_type=F32)
    branch_b = jnp.dot(_pick(is_ctx, obc_ref, obl_ref), wbb_ref[...], preferred_element_type=F32)
    merged = zg_ref[:, 0:d].astype(F32) * branch_a + zg_ref[:, d:2 * d].astype(F32) * branch_b
    mix = jnp.dot(merged.astype(BF16), wob_ref[...], preferred_element_type=F32)
    x = _pick(is_ctx, xc_ref, xl_ref)
    x1 = _layer_norm(DEEPNORM_ALPHA * x + gate1 * mix, g_ref[...], b_ref[...])
    x1_ref[...] = x1
    h2 = x1 * (1.0 + scale2) + shift2
    _rows_to_slabs(h2, st_ref, slab_ref)
    idx_ref[...], rw_ref[...] = _route_tile(h2, wrh_ref[...], wrl_ref[...], rb_ref[...])


def _post_mixer(x_ctx, x_lat, mod3, oa_ctx, oa_lat, ob_ctx, ob_lat, zg, w_a, w_b, w_o, ln_g, ln_b,
                wr_hi, wr_lo, router_bias, mod_map):
    d = x_ctx.shape[1]
    n_ctx_tiles = x_ctx.shape[0] // TOKEN_TILE
    t = x_ctx.shape[0] + x_lat.shape[0]
    tm = TOKEN_TILE
    row = lambda i: (i, 0)
    full = lambda i: (0, 0)
    return pl.pallas_call(
        functools.partial(_post_mixer_kernel, d=d, n_ctx_tiles=n_ctx_tiles),
        grid=(t // tm,),
        in_specs=_pair_specs(d, n_ctx_tiles) + [
            pl.BlockSpec((1, 1, mod3.shape[2]), mod_map),
        ] + _pair_specs(oa_ctx.shape[1], n_ctx_tiles) + _pair_specs(ob_ctx.shape[1], n_ctx_tiles) + [
            pl.BlockSpec((tm, 2 * d), row),
            _resident(w_a.shape),
            _resident(w_b.shape),
            _resident(w_o.shape),
            pl.BlockSpec((1, d), full),
            pl.BlockSpec((1, d), full),
            pl.BlockSpec((N_EXPERTS, d), full),
            pl.BlockSpec((N_EXPERTS, d), full),
            pl.BlockSpec((N_EXPERTS, 1), full),
        ],
        out_specs=[pl.BlockSpec((tm, d), row),
                   pl.BlockSpec((tm * SUBLANES, LANES), row),
                   pl.BlockSpec((TOP_K, tm), lambda i: (0, i)), pl.BlockSpec((TOP_K, tm), lambda i: (0, i))],
        out_shape=[jax.ShapeDtypeStruct((t, d), F32),
                   jax.ShapeDtypeStruct((t * SUBLANES, LANES), F32),
                   jax.ShapeDtypeStruct((TOP_K, t), jnp.int32), jax.ShapeDtypeStruct((TOP_K, t), F32)],
        scratch_shapes=[pltpu.VMEM((SUBLANES * TILE_STRIDE, LANES), F32),
                        pltpu.VMEM(w_a.shape, BF16), pltpu.VMEM(w_b.shape, BF16), pltpu.VMEM(w_o.shape, BF16)],
        compiler_params=_params(("arbitrary",)),
    )(x_ctx, x_lat, mod3, oa_ctx, oa_lat, ob_ctx, ob_lat, zg, w_a, w_b, w_o, ln_g, ln_b, wr_hi, wr_lo, router_bias)


def _first_index_of_max(x, idx, sentinel):
    m = jnp.max(x, axis=0, keepdims=True)
    first = jnp.min(jnp.where(x == m, idx, sentinel), axis=0, keepdims=True)
    return m, first


def _route_tile(h, wh, wl, bias):
    hh = h.astype(BF16)
    hl = (h - hh.astype(F32)).astype(BF16)
    nt = (((1,), (1,)), ((), ()))
    logits = (lax.dot_general(wh, hh, nt, preferred_element_type=F32)
              + lax.dot_general(wh, hl, nt, preferred_element_type=F32)
              + lax.dot_general(wl, hh, nt, preferred_element_type=F32))
    scores = _sigmoid(logits)
    sel = scores + bias
    tm = sel.shape[1]
    neg = -jnp.inf
    gidx = lax.broadcasted_iota(jnp.int32, (GROUP_SIZE, tm), 0)
    group_scores = []
    for g in range(N_GROUPS):
        sg = sel[g * GROUP_SIZE:(g + 1) * GROUP_SIZE, :]
        m1, first = _first_index_of_max(sg, gidx, GROUP_SIZE)
        m2 = jnp.max(jnp.where(gidx == first, neg, sg), axis=0, keepdims=True)
        group_scores.append(m1 + m2)
    gs = jnp.concatenate(group_scores, axis=0)
    nidx = lax.broadcasted_iota(jnp.int32, (N_GROUPS, tm), 0)
    chosen = jnp.zeros((N_GROUPS, tm), jnp.bool_)
    for _ in range(TOPK_GROUPS):
        _, first = _first_index_of_max(gs, nidx, N_GROUPS)
        hit = nidx == first
        chosen = jnp.logical_or(chosen, hit)
        gs = jnp.where(hit, neg, gs)
    masked = jnp.concatenate(
        [jnp.where(chosen[g:g + 1, :], sel[g * GROUP_SIZE:(g + 1) * GROUP_SIZE, :], neg) for g in range(N_GROUPS)],
        axis=0)
    eidx = lax.broadcasted_iota(jnp.int32, (N_EXPERTS, tm), 0)
    picks, weights = [], []
    for _ in range(TOP_K):
        _, first = _first_index_of_max(masked, eidx, N_EXPERTS)
        hit = eidx == first
        picks.append(first)
        weights.append(jnp.sum(jnp.where(hit, scores, 0.0), axis=0, keepdims=True))
        masked = jnp.where(hit, neg, masked)
    wk = jnp.concatenate(weights, axis=0)
    return jnp.concatenate(picks, axis=0), wk / jnp.sum(wk, axis=0, keepdims=True) * ROUTED_SCALE


def _dispatch_lists(idx_t, w_t):
    k, t = idx_t.shape
    s = k * t
    assert N_EXPERTS * t < 2 ** 31
    n_blocks = -(-(s // MOE_BLOCK + N_EXPERTS) // BLOCK_COUNT_MULTIPLE) * BLOCK_COUNT_MULTIPLE
    flat_tok = jnp.arange(s, dtype=jnp.int32) % t
    sorted_key, sorted_w = lax.sort((idx_t.reshape(s) * t + flat_tok, w_t.reshape(s)), num_keys=1)
    sorted_tok = sorted_key % t
    experts = jnp.arange(N_EXPERTS + 1, dtype=jnp.int32)
    start = jnp.sum(sorted_key[None, :] < experts[:, None] * t, axis=1, dtype=jnp.int32)
    counts = start[1:] - start[:-1]
    blocks_per_e = (counts + MOE_BLOCK - 1) // MOE_BLOCK
    block_end = jnp.cumsum(blocks_per_e)
    n_used = block_end[-1]
    blk = jnp.arange(n_blocks, dtype=jnp.int32)
    block_e = jnp.sum(block_end[None, :] <= blk[:, None], axis=1, dtype=jnp.int32)
    block_e = jnp.minimum(block_e, N_EXPERTS - 1)
    onehot = (block_e[:, None] == experts[None, :N_EXPERTS]).astype(jnp.int32)
    pick = lambda a: jnp.sum(onehot * a[None, :], axis=1)
    offset = (blk - (pick(block_end) - pick(blocks_per_e))) * MOE_BLOCK
    row_start = pick(start[:-1]) + offset
    n_rows = jnp.clip(pick(counts) - offset, 0, MOE_BLOCK)
    used = blk < n_used
    last_e = jnp.sum(jnp.where(blk == n_used - 1, block_e, 0))
    block_e = jnp.where(used, block_e, last_e)
    row_start = jnp.where(used, row_start, 0)
    n_rows = jnp.where(used, n_rows, 0)
    return block_e, row_start, n_rows, (n_used - 1).reshape(1), sorted_tok, sorted_w


def _block_lists_kernel(rs_ref, nr_ref, tok_ref, w_ref, off_ref, wout_ref, *, t):
    lane = lax.broadcasted_iota(jnp.int32, (LIST_ROWS, LANES), 1)
    row = lax.broadcasted_iota(jnp.int32, (LIST_ROWS, LANES), 0)

    def one_block(b, carry):
        start = rs_ref[b]
        n_rows = nr_ref[b]
        q = start // LANES
        lane0 = start % LANES
        q8 = pl.multiple_of((q // SUBLANES) * SUBLANES, SUBLANES)

        def window(ref):
            a = ref[pl.ds(q8, 2 * SUBLANES), :]
            a = pltpu.roll(a, 2 * SUBLANES - (q - q8), axis=0)
            lo = pltpu.roll(a[0:LIST_ROWS], LANES - lane0, axis=1)
            hi = pltpu.roll(a[1:LIST_ROWS + 1], LANES - lane0, axis=1)
            return jnp.where(lane < LANES - lane0, lo, hi)

        valid = row * LANES + lane < n_rows
        off_ref[b] = jnp.where(valid, window(tok_ref) * SUBLANES, t * SUBLANES)
        wout_ref[b] = jnp.where(valid, window(w_ref), 0.0)
        return carry

    n_blocks = off_ref.shape[0]
    unroll = next(u for u in (4, 2, 1) if n_blocks % u == 0)

    def several_blocks(i, carry):
        for u in range(unroll):
            one_block(i * unroll + u, carry)
        return carry

    lax.fori_loop(0, n_blocks // unroll, several_blocks, 0)


def _block_lists(row_start, n_rows, sorted_tok, sorted_w, t):
    n_blocks = row_start.shape[0]
    n_slots = sorted_tok.shape[0]
    list_rows = n_slots // LANES + 2 * SUBLANES
    pad = list_rows * LANES - n_slots
    tok2d = jnp.pad(sorted_tok, (0, pad)).reshape(list_rows, LANES)
    w2d = jnp.pad(sorted_w, (0, pad)).reshape(list_rows, LANES)
    full = lambda i, rs, nr: (0, 0)
    blk = pl.BlockSpec((n_blocks, LIST_ROWS, LANES), lambda i, rs, nr: (0, 0, 0))
    off, wts = pl.pallas_call(
        functools.partial(_block_lists_kernel, t=t),
        grid_spec=pltpu.PrefetchScalarGridSpec(
            num_scalar_prefetch=2,
            grid=(1,),
            in_specs=[pl.BlockSpec((list_rows, LANES), full), pl.BlockSpec((list_rows, LANES), full)],
            out_specs=[blk, blk],
        ),
        out_shape=[jax.ShapeDtypeStruct((n_blocks, LIST_ROWS, LANES), jnp.int32),
                   jax.ShapeDtypeStruct((n_blocks, LIST_ROWS, LANES), F32)],
        compiler_params=_params(("arbitrary",)),
    )(row_start, n_rows, tok2d, w2d)
    return off.reshape(n_blocks * LIST_LEN), wts.reshape(n_blocks * LIST_LEN)


def _expert_kernel(be_ref, nr_ref, last_ref, x_hbm, wg_hbm, wu_hbm, wd_hbm, off_ref, out_ref,
                   xs, xt, wgb, wub, wdb, sem, wsem):
    step = pl.program_id(0)
    per_step = out_ref.shape[0] // MOE_BLOCK
    chunks = wg_hbm.shape[1] // LANES
    last = last_ref[0]
    ahead = WEIGHT_BUFFERS - 1

    def weight_copies(blk):
        e = be_ref[blk]
        slot = blk % WEIGHT_BUFFERS
        return [pltpu.make_async_copy(src.at[e], dst.at[slot], wsem.at[slot])
                for src, dst in ((wg_hbm, wgb), (wu_hbm, wub), (wd_hbm, wdb))]

    @pl.when(step == 0)
    def _load():
        for blk in range(ahead):
            @pl.when(blk <= last)
            def _(blk=blk):
                for cp in weight_copies(blk):
                    cp.start()
        rows = x_hbm.shape[0]
        cp = pltpu.make_async_copy(x_hbm, xs.at[pl.ds(0, rows)], sem)
        cp.start()
        cp.wait()
        xs[pl.ds(rows, SUBLANES), :] = jnp.zeros((SUBLANES, LANES), F32)
        xt[...] = jnp.zeros(xt.shape, F32)

    def one_block(g):
        b = step * per_step + g
        n_rows = nr_ref[b]

        @pl.when(b + ahead <= last)
        def _prefetch():
            for cp in weight_copies(b + ahead):
                cp.start()

        @pl.when(b <= last)
        def _block():
            slot = b % WEIGHT_BUFFERS
            for cp in weight_copies(b):
                cp.wait()
            for seg in range(MOE_BLOCK // ROW_SEGMENT):
                @pl.when(n_rows > seg * ROW_SEGMENT)
                def _gather(seg=seg):
                    for r in range(seg * ROW_SEGMENT, (seg + 1) * ROW_SEGMENT):
                        src = pl.multiple_of(off_ref[g * LIST_LEN + r], SUBLANES)
                        xt[pl.ds(r, chunks, stride=ROW_STRIDE), :] = xs[pl.ds(src, SUBLANES), :]

            x = jnp.concatenate(
                [xt[j * ROW_STRIDE:j * ROW_STRIDE + MOE_BLOCK, :] for j in range(chunks)], axis=1).astype(BF16)
            gate = jnp.dot(x, wgb[slot].astype(BF16), preferred_element_type=F32)
            up = jnp.dot(x, wub[slot].astype(BF16), preferred_element_type=F32)
            hidden = (_silu(gate) * up).astype(BF16)
            out = jnp.dot(hidden, wdb[slot].astype(BF16), preferred_element_type=F32)
            out_ref[g * MOE_BLOCK:(g + 1) * MOE_BLOCK, :] = out.astype(out_ref.dtype)

    for g in range(per_step):
        one_block(g)


def _combine_kernel(nr_ref, last_ref, o_ref, off_ref, w_ref, y_hbm, ys, ot, sem, *, t, n_steps):
    step = pl.program_id(0)
    per_step = o_ref.shape[0] // MOE_BLOCK
    rows = t * SUBLANES
    chunks = o_ref.shape[1] // LANES
    group = ROW_GROUP

    @pl.when(step == 0)
    def _init():
        ys[...] = jnp.zeros(ys.shape, F32)

    def one_block(g):
        n_rows = nr_ref[step * per_step + g]
        row0 = g * MOE_BLOCK
        slot0 = g * LIST_LEN

        @pl.when(n_rows > 0)
        def _block():
            for j in range(chunks):
                ot[j * ROW_STRIDE:j * ROW_STRIDE + MOE_BLOCK, :] = (
                    o_ref[row0:row0 + MOE_BLOCK, j * LANES:(j + 1) * LANES].astype(F32))
            for seg in range(MOE_BLOCK // ROW_SEGMENT):
                @pl.when(n_rows > seg * ROW_SEGMENT)
                def _scatter(seg=seg):
                    for r0 in range(seg * ROW_SEGMENT, (seg + 1) * ROW_SEGMENT, group):
                        dst = [pl.multiple_of(off_ref[slot0 + r0 + i], SUBLANES) for i in range(group)]
                        vals = [ys[pl.ds(dst[i], SUBLANES), :]
                                + w_ref[slot0 + r0 + i] * ot[pl.ds(r0 + i, chunks, stride=ROW_STRIDE), :]
                                for i in range(group)]
                        for i in range(group):
                            ys[pl.ds(dst[i], SUBLANES), :] = vals[i]

    for g in range(per_step):
        one_block(g)

    @pl.when(step == n_steps - 1)
    def _store():
        cp = pltpu.make_async_copy(ys.at[pl.ds(0, rows)], y_hbm, sem)
        cp.start()
        cp.wait()


def _routed_experts(h2_rows, block_e, row_start, n_rows, last_block, sorted_tok, sorted_w, w_gate, w_up, w_down):
    rows, lanes = h2_rows.shape
    t = rows // SUBLANES
    n_blocks = block_e.shape[0]
    d, e_dim = w_gate.shape[1], w_gate.shape[2]
    assert d == SUBLANES * LANES and lanes == LANES
    offsets, weights = _block_lists(row_start, n_rows, sorted_tok, sorted_w, t)
    per_step = next(p for p in range(MAX_BLOCKS_PER_STEP, 0, -1) if n_blocks % p == 0)
    n_steps = n_blocks // per_step

    def used_step(s, last):
        return jnp.minimum(s, last[0] // per_step)

    staging = pltpu.VMEM((SUBLANES * ROW_STRIDE, LANES), F32)
    resident = pltpu.VMEM((rows + SUBLANES, LANES), F32)
    expert_out = pl.pallas_call(
        _expert_kernel,
        grid_spec=pltpu.PrefetchScalarGridSpec(
            num_scalar_prefetch=3,
            grid=(n_steps,),
            in_specs=[
                pl.BlockSpec(memory_space=pl.ANY),
                pl.BlockSpec(memory_space=pl.ANY),
                pl.BlockSpec(memory_space=pl.ANY),
                pl.BlockSpec(memory_space=pl.ANY),
                pl.BlockSpec((per_step * LIST_LEN,), lambda s, be, nr, last: (used_step(s, last),),
                             memory_space=pltpu.SMEM),
            ],
            out_specs=pl.BlockSpec((per_step * MOE_BLOCK, d), lambda s, be, nr, last: (used_step(s, last), 0)),
            scratch_shapes=[
                resident,
                staging,
                pltpu.VMEM((WEIGHT_BUFFERS, d, e_dim), F32),
                pltpu.VMEM((WEIGHT_BUFFERS, d, e_dim), F32),
                pltpu.VMEM((WEIGHT_BUFFERS, e_dim, d), F32),
                pltpu.SemaphoreType.DMA(()),
                pltpu.SemaphoreType.DMA((WEIGHT_BUFFERS,)),
            ],
        ),
        out_shape=jax.ShapeDtypeStruct((n_blocks * MOE_BLOCK, d), BF16),
        compiler_params=_params(("arbitrary",)),
    )(block_e, n_rows, last_block, h2_rows, w_gate, w_up, w_down, offsets)
    per_step = next(p for p in COMBINE_BLOCKS_PER_STEP if n_blocks % p == 0)
    n_steps = n_blocks // per_step
    list_spec = pl.BlockSpec((per_step * LIST_LEN,), lambda s, nr, last: (used_step(s, last),),
                             memory_space=pltpu.SMEM)
    return pl.pallas_call(
        functools.partial(_combine_kernel, t=t, n_steps=n_steps),
        grid_spec=pltpu.PrefetchScalarGridSpec(
            num_scalar_prefetch=2,
            grid=(n_steps,),
            in_specs=[pl.BlockSpec((per_step * MOE_BLOCK, d), lambda s, nr, last: (used_step(s, last), 0)),
                      list_spec, list_spec],
            out_specs=pl.BlockSpec(memory_space=pl.ANY),
            scratch_shapes=[resident, staging, pltpu.SemaphoreType.DMA(())],
        ),
        out_shape=jax.ShapeDtypeStruct((rows, LANES), F32),
        compiler_params=_params(("arbitrary",)),
    )(n_rows, last_block, expert_out, offsets, weights)


def _final_kernel(x1_ref, routed_ref, mod_ref, wg_ref, wu_ref, wd_ref, g_ref, b_ref,
                  yc_ref, yl_ref, st_ref, wgb_ref, wub_ref, wdb_ref, *, d, n_ctx_tiles):
    for src, dst in ((wg_ref, wgb_ref), (wu_ref, wub_ref), (wd_ref, wdb_ref)):
        _cast_weight_once(src, dst)
    is_ctx = pl.program_id(0) < n_ctx_tiles
    shift2 = mod_ref[0, :, 3 * d:4 * d]
    scale2 = mod_ref[0, :, 4 * d:5 * d]
    gate2 = mod_ref[0, :, 5 * d:6 * d]
    x1 = x1_ref[...]
    h = (x1 * (1.0 + scale2) + shift2).astype(BF16)
    gate = jnp.dot(h, wgb_ref[...], preferred_element_type=F32)
    up = jnp.dot(h, wub_ref[...], preferred_element_type=F32)
    shared = jnp.dot((_silu(gate) * up).astype(BF16), wdb_ref[...], preferred_element_type=F32)
    ffn = _slabs_to_rows(routed_ref, st_ref, h.shape[0]) + shared
    y = _layer_norm(DEEPNORM_ALPHA * x1 + gate2 * ffn, g_ref[...], b_ref[...])

    @pl.when(is_ctx)
    def _():
        yc_ref[...] = y

    @pl.when(jnp.logical_not(is_ctx))
    def _():
        yl_ref[...] = y


def _final(x1, routed, mod3, w_g, w_u, w_d, ln_g, ln_b, mod_map, t_ctx):
    t, d = x1.shape
    t_lat = t - t_ctx
    n_ctx_tiles = t_ctx // TOKEN_TILE
    tm = TOKEN_TILE
    row = lambda i: (i, 0)
    full = lambda i: (0, 0)
    return pl.pallas_call(
        functools.partial(_final_kernel, d=d, n_ctx_tiles=n_ctx_tiles),
        grid=(t // tm,),
        in_specs=[
            pl.BlockSpec((tm, d), row),
            pl.BlockSpec((tm * SUBLANES, LANES), row),
            pl.BlockSpec((1, 1, mod3.shape[2]), mod_map),
            _resident(w_g.shape),
            _resident(w_u.shape),
            _resident(w_d.shape),
            pl.BlockSpec((1, d), full),
            pl.BlockSpec((1, d), full),
        ],
        out_specs=_pair_specs(d, n_ctx_tiles),
        out_shape=[jax.ShapeDtypeStruct((t_ctx, d), F32), jax.ShapeDtypeStruct((t_lat, d), F32)],
        scratch_shapes=[pltpu.VMEM((SUBLANES * TILE_STRIDE, LANES), F32),
                        pltpu.VMEM(w_g.shape, BF16), pltpu.VMEM(w_u.shape, BF16), pltpu.VMEM(w_d.shape, BF16)],
        compiler_params=_params(("arbitrary",)),
    )(x1, routed, mod3, w_g, w_u, w_d, ln_g, ln_b)


def kernel(x_prompt, x_sample, cache_k, cache_v, state_hgrn, c, c_ctx, w_mod, b_mod, w_in, hg_lb, hg_norm_g, q_norm_g, k_norm_g, w_branch_a, w_branch_b, w_out, ln1_g, ln1_b, w_router, router_bias, w_e_gate, w_e_up, w_e_down, w_s_gate, w_s_up, w_s_down, ln2_g, ln2_b):
    assert w_mod.shape[0] == DEPTH
    n_ctx, seq_ctx, d = x_prompt.shape
    n_lat, seq_lat, _ = x_sample.shape
    t_ctx = n_ctx * seq_ctx
    t_lat = n_lat * seq_lat
    assert seq_ctx == TOKEN_TILE and seq_lat % TOKEN_TILE == 0
    layer = 0

    lb = jnp.cumsum(jax.nn.softmax(hg_lb.astype(F32), axis=0), axis=0)[layer]

    cond = jnp.concatenate([c_ctx[None, :], c], axis=0)
    cond = jnp.pad(cond, ((0, (-cond.shape[0]) % SUBLANES), (0, 0)))
    mod = _modulation(cond, w_mod[layer], b_mod[layer][None, :])
    mod3 = mod.reshape(mod.shape[0], 1, mod.shape[1])
    mod_map = _mod_row_map(t_ctx // TOKEN_TILE, seq_lat // TOKEN_TILE)

    x_ctx = x_prompt.reshape(t_ctx, d)
    x_lat = x_sample.reshape(t_lat, d)
    zh, zf, za, zg = _input_projection(x_ctx, x_lat, mod3, w_in[layer], lb, mod_map)

    norm_g = hg_norm_g[layer][None, :]
    oa_ctx, s_ctx = _hgrn_scan(zh, zf, norm_g, None, batch=n_ctx, seq=seq_ctx, row_block0=0,
                               heads=HG_HEADS_PER_STEP_CTX)
    oa_lat, _ = _hgrn_scan(zh, zf, norm_g, state_hgrn[:, layer], batch=n_lat, seq=seq_lat,
                           row_block0=t_ctx // seq_lat, heads=HG_HEADS_PER_STEP_LATENT)

    q_gain = jnp.tile(q_norm_g[layer], N_HEADS)[None, :]
    k_gain = jnp.tile(k_norm_g[layer], N_KV_HEADS)[None, :]
    lane = jnp.arange(ATT_WIDTH)
    group_ones = (lane[:, None] // HEAD_DIM == lane[None, :] // HEAD_DIM).astype(BF16)
    ob_ctx, k_ctx, v_ctx = _attention(za, q_gain, k_gain, group_ones, batch=n_ctx, seq=seq_ctx, row_block0=0)
    past = cache_k.shape[2]
    ob_lat = _attention(
        za, q_gain, k_gain, group_ones, batch=n_lat, seq=seq_lat, row_block0=t_ctx // seq_lat,
        rope=_rope_tables(seq_lat),
        cache=(cache_k[:, layer].reshape(n_lat, past, KV_WIDTH), cache_v[:, layer].reshape(n_lat, past, KV_WIDTH)))

    wr_t = w_router[layer].T
    wr_hi = wr_t.astype(BF16)
    wr_lo = (wr_t - wr_hi.astype(F32)).astype(BF16)
    x1, h2_slabs, idx_t, w_t = _post_mixer(
        x_ctx, x_lat, mod3, oa_ctx, oa_lat, ob_ctx, ob_lat, zg, w_branch_a[layer], w_branch_b[layer], w_out[layer],
        ln1_g[layer][None, :], ln1_b[layer][None, :], wr_hi, wr_lo, router_bias[layer][:, None], mod_map)

    routed = _routed_experts(h2_slabs, *_dispatch_lists(idx_t, w_t), w_e_gate[layer], w_e_up[layer], w_e_down[layer])

    y_ctx, y_lat = _final(x1, routed, mod3, w_s_gate[layer], w_s_up[layer], w_s_down[layer],
                          ln2_g[layer][None, :], ln2_b[layer][None, :], mod_map, t_ctx)

    y_prompt = y_ctx.reshape(n_ctx, seq_ctx, d)
    y_sample = y_lat.reshape(n_lat, seq_lat, d)
    new_cache_k = k_ctx.reshape(n_ctx, 1, seq_ctx, N_KV_HEADS, HEAD_DIM)
    new_cache_v = v_ctx.reshape(n_ctx, 1, seq_ctx, N_KV_HEADS, HEAD_DIM)
    new_state = s_ctx[:, None]
    return (y_prompt, y_sample, new_cache_k, new_cache_v, new_state)
```

```python
import functools

import jax
import jax.numpy as jnp
from jax import lax
from jax.experimental import pallas as pl
from jax.experimental.pallas import tpu as pltpu

F32 = jnp.float32
BF16 = jnp.bfloat16

GRID_W = 64
HG_HEADS = 4
HG_DK = 128
HG_DV = 128
HG_WIDTH = HG_HEADS * HG_DK
N_HEADS = 8
N_KV_HEADS = 2
HEAD_DIM = 64
ATT_WIDTH = N_HEADS * HEAD_DIM
KV_WIDTH = N_KV_HEADS * HEAD_DIM
ROPE_THETA = 10000.0
N_EXPERTS = 256
TOP_K = 8
N_GROUPS = 8
TOPK_GROUPS = 4
GROUP_SIZE = N_EXPERTS // N_GROUPS
ROUTED_SCALE = 2.5
NORM_EPS = 1e-6
DEPTH = 1
DEEPNORM_ALPHA = (2 * DEPTH) ** 0.25

LANES = 128
SUBLANES = 8
VMEM_LIMIT = 56 * 1024 * 1024

CAST_CHUNK = 512
TOKEN_TILE = 256
TILE_STRIDE = TOKEN_TILE + 1
HG_CHUNK = 32
HG_HEADS_PER_STEP_CTX = 4
HG_HEADS_PER_STEP_LATENT = 2
ATT_Q_BLOCK = 128
MOE_BLOCK = 320
BLOCK_COUNT_MULTIPLE = 20
ROW_STRIDE = MOE_BLOCK + 1
LIST_LEN = 512
LIST_ROWS = LIST_LEN // LANES
ROW_GROUP = 8
ROW_SEGMENT = 32
MAX_BLOCKS_PER_STEP = 4
GATHER_BLOCKS_PER_STEP = (10, 4, 2, 1)
WEIGHT_BUFFERS = 3


def _sigmoid(x):
    return 1.0 / (1.0 + jnp.exp(-x))


def _silu(x):
    return x * _sigmoid(x)


def _params(sem=None):
    return pltpu.CompilerParams(dimension_semantics=sem, vmem_limit_bytes=VMEM_LIMIT)


def _resident(shape):
    return pl.BlockSpec(shape, lambda i: (0,) * len(shape), pipeline_mode=pl.Buffered(1))


def _cast_weight_once(src_ref, dst_ref):
    @pl.when(pl.program_id(0) == 0)
    def _():
        width = src_ref.shape[1]
        for lo in range(0, width, CAST_CHUNK):
            hi = min(lo + CAST_CHUNK, width)
            dst_ref[:, lo:hi] = src_ref[:, lo:hi].astype(dst_ref.dtype)


def _mod_kernel(c_ref, w_ref, b_ref, o_ref):
    s = _silu(c_ref[...]).astype(BF16)
    o_ref[...] = jnp.dot(s, w_ref[...].astype(BF16), preferred_element_type=F32) + b_ref[...]


def _modulation(cond, w_mod, b_mod):
    n, d = cond.shape
    width = w_mod.shape[1]
    tn = width // 4
    return pl.pallas_call(
        _mod_kernel,
        grid=(4,),
        in_specs=[
            pl.BlockSpec((n, d), lambda j: (0, 0)),
            pl.BlockSpec((d, tn), lambda j: (0, j)),
            pl.BlockSpec((1, tn), lambda j: (0, j)),
        ],
        out_specs=pl.BlockSpec((n, tn), lambda j: (0, j)),
        out_shape=jax.ShapeDtypeStruct((n, width), F32),
        compiler_params=_params(("arbitrary",)),
    )(cond, w_mod, b_mod)


def _inproj_kernel(xc_ref, xl_ref, mod_ref, w_ref, lb_ref, zh_ref, zf_ref, za_ref, zg_ref, wb_ref,
                   *, d, n_ctx_tiles):
    _cast_weight_once(w_ref, wb_ref)
    shift = mod_ref[0, :, 0:d]
    scale = mod_ref[0, :, d:2 * d]
    x = _pick(pl.program_id(0) < n_ctx_tiles, xc_ref, xl_ref)
    h = (x * (1.0 + scale) + shift).astype(BF16)

    def proj(lo, hi):
        return jnp.dot(h, wb_ref[:, lo:hi], preferred_element_type=F32)

    w = HG_WIDTH
    zh_ref[:, 0:w] = _silu(proj(0, w)).astype(zh_ref.dtype)
    zh_ref[:, w:2 * w] = proj(w, 2 * w).astype(zh_ref.dtype)
    for i in range(2):
        lb = lb_ref[i:i + 1, :]
        zf_ref[:, i * w:(i + 1) * w] = lb + (1.0 - lb) * _sigmoid(proj((2 + i) * w, (3 + i) * w))
    zh_ref[:, 2 * w:3 * w] = _silu(proj(4 * w, 5 * w)).astype(zh_ref.dtype)
    a0 = 5 * w
    a1 = a0 + ATT_WIDTH + 2 * KV_WIDTH
    za_ref[...] = proj(a0, a1)
    for i in range(4):
        lo = a1 + i * (d // 2)
        zg_ref[:, i * (d // 2):(i + 1) * (d // 2)] = _sigmoid(proj(lo, lo + d // 2)).astype(zg_ref.dtype)


def _mod_row_map(n_ctx_tiles, tiles_per_latent):
    def index_map(i):
        row = jnp.where(i < n_ctx_tiles, 0, 1 + (i - n_ctx_tiles) // tiles_per_latent)
        return (row, 0, 0)
    return index_map


def _pair_specs(width, n_ctx_tiles):
    return [pl.BlockSpec((TOKEN_TILE, width), lambda i: (jnp.minimum(i, n_ctx_tiles - 1), 0)),
            pl.BlockSpec((TOKEN_TILE, width), lambda i: (jnp.maximum(i - n_ctx_tiles, 0), 0))]


def _pick(is_ctx, ctx_ref, lat_ref):
    return jnp.where(is_ctx, ctx_ref[...], lat_ref[...])


def _input_projection(x_ctx, x_lat, mod3, w_in, lb, mod_map):
    d = x_ctx.shape[1]
    n_ctx_tiles = x_ctx.shape[0] // TOKEN_TILE
    t = x_ctx.shape[0] + x_lat.shape[0]
    width = w_in.shape[1]
    zh_w = 3 * HG_WIDTH
    zf_w = 2 * HG_WIDTH
    za_w = ATT_WIDTH + 2 * KV_WIDTH
    zg_w = 2 * d
    assert width == zh_w + zf_w + za_w + zg_w
    tm = TOKEN_TILE
    return pl.pallas_call(
        functools.partial(_inproj_kernel, d=d, n_ctx_tiles=n_ctx_tiles),
        grid=(t // tm,),
        in_specs=_pair_specs(d, n_ctx_tiles) + [
            pl.BlockSpec((1, 1, mod3.shape[2]), mod_map),
            _resident((d, width)),
            pl.BlockSpec((2, HG_WIDTH), lambda i: (0, 0)),
        ],
        out_specs=[
            pl.BlockSpec((tm, zh_w), lambda i: (i, 0)),
            pl.BlockSpec((tm, zf_w), lambda i: (i, 0)),
            pl.BlockSpec((tm, za_w), lambda i: (i, 0)),
            pl.BlockSpec((tm, zg_w), lambda i: (i, 0)),
        ],
        out_shape=[
            jax.ShapeDtypeStruct((t, zh_w), BF16),
            jax.ShapeDtypeStruct((t, zf_w), F32),
            jax.ShapeDtypeStruct((t, za_w), F32),
            jax.ShapeDtypeStruct((t, zg_w), BF16),
        ],
        scratch_shapes=[pltpu.VMEM((d, width), BF16)],
        compiler_params=_params(("arbitrary",)),
    )(x_ctx, x_lat, mod3, w_in, lb)


def _hgrn_kernel(*refs, seq, has_state, heads):
    if has_state:
        q_ref, v_ref, ff_ref, fb_ref, gs_ref, ng_ref, s0_ref, o_ref, sout_ref = refs
    else:
        q_ref, v_ref, ff_ref, fb_ref, gs_ref, ng_ref, o_ref, sout_ref = refs
        s0_ref = None
    for hd in range(heads):
        _hgrn_head(q_ref, v_ref, ff_ref, fb_ref, gs_ref, ng_ref, s0_ref, o_ref, sout_ref, hd, seq)


def _hgrn_head(q_ref, v_ref, ff_ref, fb_ref, gs_ref, ng_ref, s0_ref, o_ref, sout_ref, hd, seq):
    has_state = s0_ref is not None
    cols = slice(hd * HG_DK, (hd + 1) * HG_DK)
    c = HG_CHUNK
    n = seq // c
    q3 = q_ref[:, cols].astype(F32).reshape(n, c, HG_DK)
    v3 = v_ref[:, cols].reshape(n, c, HG_DV)
    pos = lax.broadcasted_iota(jnp.int32, (seq, HG_DK), 0) % c
    t_idx = lax.broadcasted_iota(jnp.int32, (c, c), 0)
    s_idx = lax.broadcasted_iota(jnp.int32, (c, c), 1)
    o_sum = None
    for direction, f_ref in enumerate((ff_ref, fb_ref)):
        reverse = direction == 1
        f = f_ref[:, cols]
        k3 = (1.0 - f).reshape(n, c, HG_DK)
        b = jnp.log(f)
        step = 1
        while step < c:
            if reverse:
                b = b + jnp.where(pos < c - step, pltpu.roll(b, seq - step, axis=0), 0.0)
            else:
                b = b + jnp.where(pos >= step, pltpu.roll(b, step, axis=0), 0.0)
            step *= 2
        b3 = b.reshape(n, c, HG_DK)
        edge = b3[:, 0:1, :] if reverse else b3[:, c - 1:c, :]
        mid = b3[:, c // 2:c // 2 + 1, :]
        q_rel = q3 * jnp.exp(b3 - mid)
        k_rel = k3 * jnp.exp(mid - b3)
        q_mid = q_rel.astype(BF16)
        k_mid = k_rel.astype(BF16)
        q_in = (q_rel * jnp.exp(mid)).astype(BF16)
        k_end = (k_rel * jnp.exp(edge - mid)).astype(BF16)
        scores = jnp.einsum('ntd,nsd->nts', q_mid, k_mid, preferred_element_type=F32)
        keep = (s_idx >= t_idx) if reverse else (s_idx <= t_idx)
        scores = jnp.where(keep[None], scores, 0.0).astype(BF16)
        o_intra = jnp.einsum('nts,nsv->ntv', scores, v3, preferred_element_type=F32)
        upd = jnp.einsum('nsv,nsd->nvd', v3, k_end, preferred_element_type=F32)
        dec = jnp.exp(edge)
        if has_state:
            st = s0_ref[0, direction, hd].T
        else:
            st = jnp.zeros((HG_DV, HG_DK), F32)
        before = [None] * n
        for ci in (range(n - 1, -1, -1) if reverse else range(n)):
            before[ci] = st.astype(BF16)
            st = st * dec[ci] + upd[ci]
        st_before = jnp.stack(before, axis=0)
        o_inter = jnp.einsum('ntd,nvd->ntv', q_in, st_before, preferred_element_type=F32)
        o_dir = (o_intra + o_inter).reshape(seq, HG_DV)
        o_sum = o_dir if o_sum is None else o_sum + o_dir
        sout_ref[0, direction, hd] = st.T
    ms = jnp.mean(o_sum * o_sum, axis=-1, keepdims=True)
    o = o_sum * lax.rsqrt(ms + NORM_EPS) * ng_ref[...]
    o_ref[:, cols] = (o * gs_ref[:, cols].astype(F32)).astype(o_ref.dtype)


def _hgrn_scan(zh, zf, norm_g, s0, *, batch, seq, row_block0, heads):
    has_state = s0 is not None
    h = HG_HEADS
    groups = h // heads
    width = heads * HG_DK

    def col(section):
        return pl.BlockSpec((seq, width), lambda b, j: (row_block0 + b, section * groups + j))

    in_specs = [col(0), col(1), col(0), col(1), col(2), pl.BlockSpec((1, HG_DV), lambda b, j: (0, 0))]
    args = [zh, zh, zf, zf, zh, norm_g]
    state_spec = pl.BlockSpec((1, 2, heads, HG_DK, HG_DV), lambda b, j: (b, 0, j, 0, 0))
    if has_state:
        in_specs.append(state_spec)
        args.append(s0)
    return pl.pallas_call(
        functools.partial(_hgrn_kernel, seq=seq, has_state=has_state, heads=heads),
        grid=(batch, groups),
        in_specs=in_specs,
        out_specs=[pl.BlockSpec((seq, width), lambda b, j: (b, j)), state_spec],
        out_shape=[
            jax.ShapeDtypeStruct((batch * seq, h * HG_DV), BF16),
            jax.ShapeDtypeStruct((batch, 2, h, HG_DK, HG_DV), F32),
        ],
        compiler_params=_params(("arbitrary", "arbitrary")),
    )(*args)


def _group_rms_norm(x, group_ones, gain):
    sq = x * x
    hi = sq.astype(BF16)
    lo = (sq - hi.astype(F32)).astype(BF16)
    total = (jnp.dot(hi, group_ones, preferred_element_type=F32)
             + jnp.dot(lo, group_ones, preferred_element_type=F32))
    return x * lax.rsqrt(total * (1.0 / HEAD_DIM) + NORM_EPS) * gain


def _rope(x, cos, sin_signed):
    width = x.shape[1]
    quarter = HEAD_DIM // 4
    lane = lax.broadcasted_iota(jnp.int32, x.shape, 1)
    partner = jnp.where(lane % (2 * quarter) < quarter,
                        pltpu.roll(x, width - quarter, axis=1),
                        pltpu.roll(x, quarter, axis=1))
    return x * cos + partner * sin_signed


def _attn_kernel(*refs, seq, latent):
    if latent:
        za_ref, qg_ref, kg_ref, gm_ref, cos_ref, sin_ref, ck_ref, cv_ref, o_ref = refs
    else:
        za_ref, qg_ref, kg_ref, gm_ref, o_ref, kout_ref, vout_ref = refs
    q = za_ref[:, 0:ATT_WIDTH]
    k = za_ref[:, ATT_WIDTH:ATT_WIDTH + KV_WIDTH]
    v = za_ref[:, ATT_WIDTH + KV_WIDTH:ATT_WIDTH + 2 * KV_WIDTH]
    qn = _group_rms_norm(q, gm_ref[...], qg_ref[...])
    kn = _group_rms_norm(k, gm_ref[0:KV_WIDTH, 0:KV_WIDTH], kg_ref[...])
    if latent:
        qn = _rope(qn, cos_ref[...], sin_ref[...])
        kr = _rope(kn, cos_ref[:, 0:KV_WIDTH], sin_ref[:, 0:KV_WIDTH])
    else:
        kout_ref[...] = kn
        vout_ref[...] = v
        kr = kn
    qb = (qn * (HEAD_DIM ** -0.5)).astype(BF16)
    kb = kr.astype(BF16)
    vb = v.astype(BF16)
    if latent:
        ckb = ck_ref[0].astype(BF16)
        cvb = cv_ref[0].astype(BF16)
    groups = N_HEADS // N_KV_HEADS
    tq = ATT_Q_BLOCK if latent else seq
    nt = (((1,), (1,)), ((), ()))
    for kh in range(N_KV_HEADS):
        ksl = slice(kh * HEAD_DIM, (kh + 1) * HEAD_DIM)
        k_new = kb[:, ksl]
        v_new = vb[:, ksl]
        for blk in range(seq // tq):
            rows = slice(blk * tq, (blk + 1) * tq)
            heads = [kh * groups + g for g in range(groups)]
            q_st = jnp.concatenate([qb[rows, hd * HEAD_DIM:(hd + 1) * HEAD_DIM] for hd in heads], axis=0)
            s_new = lax.dot_general(q_st, k_new, nt, preferred_element_type=F32)
            m = jnp.max(s_new, axis=-1, keepdims=True)
            if latent:
                s_ctx = lax.dot_general(q_st, ckb[:, ksl], nt, preferred_element_type=F32)
                m = jnp.maximum(m, jnp.max(s_ctx, axis=-1, keepdims=True))
            p_new = jnp.exp(s_new - m)
            denom = jnp.sum(p_new, axis=-1, keepdims=True)
            acc = jnp.dot(p_new.astype(BF16), v_new, preferred_element_type=F32)
            if latent:
                p_ctx = jnp.exp(s_ctx - m)
                denom = denom + jnp.sum(p_ctx, axis=-1, keepdims=True)
                acc = acc + jnp.dot(p_ctx.astype(BF16), cvb[:, ksl], preferred_element_type=F32)
            out = acc / denom
            for g in range(0, groups, 2):
                pair = jnp.concatenate([out[g * tq:(g + 1) * tq], out[(g + 1) * tq:(g + 2) * tq]], axis=1)
                lane0 = heads[g] * HEAD_DIM
                o_ref[rows, lane0:lane0 + 2 * HEAD_DIM] = pair.astype(o_ref.dtype)


def _attention(za, q_gain, k_gain, group_ones, *, batch, seq, row_block0, rope=None, cache=None):
    latent = cache is not None
    za_w = za.shape[1]
    in_specs = [
        pl.BlockSpec((seq, za_w), lambda b: (row_block0 + b, 0)),
        pl.BlockSpec((1, ATT_WIDTH), lambda b: (0, 0)),
        pl.BlockSpec((1, KV_WIDTH), lambda b: (0, 0)),
        pl.BlockSpec((ATT_WIDTH, ATT_WIDTH), lambda b: (0, 0)),
    ]
    args = [za, q_gain, k_gain, group_ones]
    o_spec = pl.BlockSpec((seq, ATT_WIDTH), lambda b: (b, 0))
    o_shape = jax.ShapeDtypeStruct((batch * seq, ATT_WIDTH), BF16)
    if latent:
        cos, sin_signed = rope
        ck, cv = cache
        past = ck.shape[1]
        in_specs += [
            pl.BlockSpec((seq, ATT_WIDTH), lambda b: (0, 0)),
            pl.BlockSpec((seq, ATT_WIDTH), lambda b: (0, 0)),
            pl.BlockSpec((1, past, KV_WIDTH), lambda b: (b, 0, 0)),
            pl.BlockSpec((1, past, KV_WIDTH), lambda b: (b, 0, 0)),
        ]
        args += [cos, sin_signed, ck, cv]
        out_specs = o_spec
        out_shape = o_shape
    else:
        kv_spec = pl.BlockSpec((seq, KV_WIDTH), lambda b: (b, 0))
        kv_shape = jax.ShapeDtypeStruct((batch * seq, KV_WIDTH), F32)
        out_specs = [o_spec, kv_spec, kv_spec]
        out_shape = [o_shape, kv_shape, kv_shape]
    return pl.pallas_call(
        functools.partial(_attn_kernel, seq=seq, latent=latent),
        grid=(batch,),
        in_specs=in_specs,
        out_specs=out_specs,
        out_shape=out_shape,
        compiler_params=_params(("arbitrary",)),
    )(*args)


def _rope_tables(seq):
    quarter = HEAD_DIM // 4
    t = jnp.arange(seq)
    row = (t // GRID_W).astype(F32)
    colp = (t % GRID_W).astype(F32)
    inv_freq = ROPE_THETA ** (-jnp.arange(quarter, dtype=F32) / quarter)
    lane = jnp.arange(HEAD_DIM)
    pos = jnp.where((lane < HEAD_DIM // 2)[None, :], row[:, None], colp[:, None])
    ang = pos * inv_freq[lane % quarter][None, :]
    sign = jnp.where(lane % (2 * quarter) < quarter, -1.0, 1.0)[None, :]
    cos = jnp.tile(jnp.cos(ang), (1, N_HEADS))
    sin_signed = jnp.tile(jnp.sin(ang) * sign, (1, N_HEADS))
    return cos, sin_signed


def _layer_norm(y, g, b):
    mu = jnp.mean(y, axis=-1, keepdims=True)
    yc = y - mu
    var = jnp.mean(yc * yc, axis=-1, keepdims=True)
    return yc * lax.rsqrt(var + NORM_EPS) * g + b


def _rows_to_slabs(tile, st_ref, slab_ref):
    rows, d = tile.shape
    chunks = d // LANES
    for j in range(chunks):
        st_ref[j * TILE_STRIDE:j * TILE_STRIDE + rows, :] = tile[:, j * LANES:(j + 1) * LANES]
    for r in range(rows):
        slab_ref[r * chunks:(r + 1) * chunks, :] = st_ref[pl.ds(r, chunks, stride=TILE_STRIDE), :]


def _slabs_to_rows(slab_ref, st_ref, rows):
    chunks = SUBLANES
    for r in range(rows):
        st_ref[pl.ds(r, chunks, stride=TILE_STRIDE), :] = slab_ref[r * chunks:(r + 1) * chunks, :]
    return jnp.concatenate([st_ref[j * TILE_STRIDE:j * TILE_STRIDE + rows, :] for j in range(chunks)], axis=1)


def _post_mixer_kernel(xc_ref, xl_ref, mod_ref, oac_ref, oal_ref, obc_ref, obl_ref, zg_ref, wa_ref, wb_ref,
                       wo_ref, g_ref, b_ref, wrh_ref, wrl_ref, rb_ref,
                       x1_ref, slab_ref, idx_ref, rw_ref, st_ref, wab_ref, wbb_ref, wob_ref,
                       *, d, n_ctx_tiles):
    for src, dst in ((wa_ref, wab_ref), (wb_ref, wbb_ref), (wo_ref, wob_ref)):
        _cast_weight_once(src, dst)
    is_ctx = pl.program_id(0) < n_ctx_tiles
    gate1 = mod_ref[0, :, 2 * d:3 * d]
    shift2 = mod_ref[0, :, 3 * d:4 * d]
    scale2 = mod_ref[0, :, 4 * d:5 * d]
    branch_a = jnp.dot(_pick(is_ctx, oac_ref, oal_ref), wab_ref[...], preferred_element_type=F32)
    branch_b = jnp.dot(_pick(is_ctx, obc_ref, obl_ref), wbb_ref[...], preferred_element_type=F32)
    merged = zg_ref[:, 0:d].astype(F32) * branch_a + zg_ref[:, d:2 * d].astype(F32) * branch_b
    mix = jnp.dot(merged.astype(BF16), wob_ref[...], preferred_element_type=F32)
    x = _pick(is_ctx, xc_ref, xl_ref)
    x1 = _layer_norm(DEEPNORM_ALPHA * x + gate1 * mix, g_ref[...], b_ref[...])
    x1_ref[...] = x1
    h2 = x1 * (1.0 + scale2) + shift2
    _rows_to_slabs(h2, st_ref, slab_ref)
    idx_ref[...], rw_ref[...] = _route_tile(h2, wrh_ref[...], wrl_ref[...], rb_ref[...])


def _post_mixer(x_ctx, x_lat, mod3, oa_ctx, oa_lat, ob_ctx, ob_lat, zg, w_a, w_b, w_o, ln_g, ln_b,
                wr_hi, wr_lo, router_bias, mod_map):
    d = x_ctx.shape[1]
    n_ctx_tiles = x_ctx.shape[0] // TOKEN_TILE
    t = x_ctx.shape[0] + x_lat.shape[0]
    tm = TOKEN_TILE
    row = lambda i: (i, 0)
    full = lambda i: (0, 0)
    return pl.pallas_call(
        functools.partial(_post_mixer_kernel, d=d, n_ctx_tiles=n_ctx_tiles),
        grid=(t // tm,),
        in_specs=_pair_specs(d, n_ctx_tiles) + [
            pl.BlockSpec((1, 1, mod3.shape[2]), mod_map),
        ] + _pair_specs(oa_ctx.shape[1], n_ctx_tiles) + _pair_specs(ob_ctx.shape[1], n_ctx_tiles) + [
            pl.BlockSpec((tm, 2 * d), row),
            _resident(w_a.shape),
            _resident(w_b.shape),
            _resident(w_o.shape),
            pl.BlockSpec((1, d), full),
            pl.BlockSpec((1, d), full),
            pl.BlockSpec((N_EXPERTS, d), full),
            pl.BlockSpec((N_EXPERTS, d), full),
            pl.BlockSpec((N_EXPERTS, 1), full),
        ],
        out_specs=[pl.BlockSpec((tm, d), row),
                   pl.BlockSpec((tm * SUBLANES, LANES), row),
                   pl.BlockSpec((TOP_K, tm), lambda i: (0, i)), pl.BlockSpec((TOP_K, tm), lambda i: (0, i))],
        out_shape=[jax.ShapeDtypeStruct((t, d), F32),
                   jax.ShapeDtypeStruct((t * SUBLANES, LANES), F32),
                   jax.ShapeDtypeStruct((TOP_K, t), jnp.int32), jax.ShapeDtypeStruct((TOP_K, t), F32)],
        scratch_shapes=[pltpu.VMEM((SUBLANES * TILE_STRIDE, LANES), F32),
                        pltpu.VMEM(w_a.shape, BF16), pltpu.VMEM(w_b.shape, BF16), pltpu.VMEM(w_o.shape, BF16)],
        compiler_params=_params(("arbitrary",)),
    )(x_ctx, x_lat, mod3, oa_ctx, oa_lat, ob_ctx, ob_lat, zg, w_a, w_b, w_o, ln_g, ln_b, wr_hi, wr_lo, router_bias)


def _first_index_of_max(x, idx, sentinel):
    m = jnp.max(x, axis=0, keepdims=True)
    first = jnp.min(jnp.where(x == m, idx, sentinel), axis=0, keepdims=True)
    return m, first


def _route_tile(h, wh, wl, bias):
    hh = h.astype(BF16)
    hl = (h - hh.astype(F32)).astype(BF16)
    nt = (((1,), (1,)), ((), ()))
    logits = (lax.dot_general(wh, hh, nt, preferred_element_type=F32)
              + lax.dot_general(wh, hl, nt, preferred_element_type=F32)
              + lax.dot_general(wl, hh, nt, preferred_element_type=F32))
    scores = _sigmoid(logits)
    sel = scores + bias
    tm = sel.shape[1]
    neg = -jnp.inf
    gidx = lax.broadcasted_iota(jnp.int32, (GROUP_SIZE, tm), 0)
    group_scores = []
    for g in range(N_GROUPS):
        sg = sel[g * GROUP_SIZE:(g + 1) * GROUP_SIZE, :]
        m1, first = _first_index_of_max(sg, gidx, GROUP_SIZE)
        m2 = jnp.max(jnp.where(gidx == first, neg, sg), axis=0, keepdims=True)
        group_scores.append(m1 + m2)
    gs = jnp.concatenate(group_scores, axis=0)
    nidx = lax.broadcasted_iota(jnp.int32, (N_GROUPS, tm), 0)
    chosen = jnp.zeros((N_GROUPS, tm), jnp.bool_)
    for _ in range(TOPK_GROUPS):
        _, first = _first_index_of_max(gs, nidx, N_GROUPS)
        hit = nidx == first
        chosen = jnp.logical_or(chosen, hit)
        gs = jnp.where(hit, neg, gs)
    masked = jnp.concatenate(
        [jnp.where(chosen[g:g + 1, :], sel[g * GROUP_SIZE:(g + 1) * GROUP_SIZE, :], neg) for g in range(N_GROUPS)],
        axis=0)
    eidx = lax.broadcasted_iota(jnp.int32, (N_EXPERTS, tm), 0)
    picks, weights = [], []
    for _ in range(TOP_K):
        _, first = _first_index_of_max(masked, eidx, N_EXPERTS)
        hit = eidx == first
        picks.append(first)
        weights.append(jnp.sum(jnp.where(hit, scores, 0.0), axis=0, keepdims=True))
        masked = jnp.where(hit, neg, masked)
    wk = jnp.concatenate(weights, axis=0)
    return jnp.concatenate(picks, axis=0), wk / jnp.sum(wk, axis=0, keepdims=True) * ROUTED_SCALE


def _dispatch_lists(idx_t, w_t):
    k, t = idx_t.shape
    s = k * t
    assert N_EXPERTS * t < 2 ** 31
    n_blocks = -(-(s // MOE_BLOCK + N_EXPERTS) // BLOCK_COUNT_MULTIPLE) * BLOCK_COUNT_MULTIPLE
    flat_tok = jnp.arange(s, dtype=jnp.int32) % t
    sorted_key, sorted_w = lax.sort((idx_t.reshape(s) * t + flat_tok, w_t.reshape(s)), num_keys=1)
    sorted_tok = sorted_key % t
    experts = jnp.arange(N_EXPERTS + 1, dtype=jnp.int32)
    start = jnp.sum(sorted_key[None, :] < experts[:, None] * t, axis=1, dtype=jnp.int32)
    counts = start[1:] - start[:-1]
    blocks_per_e = (counts + MOE_BLOCK - 1) // MOE_BLOCK
    block_end = jnp.cumsum(blocks_per_e)
    n_used = block_end[-1]
    blk = jnp.arange(n_blocks, dtype=jnp.int32)
    block_e = jnp.sum(block_end[None, :] <= blk[:, None], axis=1, dtype=jnp.int32)
    block_e = jnp.minimum(block_e, N_EXPERTS - 1)
    onehot = (block_e[:, None] == experts[None, :N_EXPERTS]).astype(jnp.int32)
    pick = lambda a: jnp.sum(onehot * a[None, :], axis=1)
    offset = (blk - (pick(block_end) - pick(blocks_per_e))) * MOE_BLOCK
    row_start = pick(start[:-1]) + offset
    n_rows = jnp.clip(pick(counts) - offset, 0, MOE_BLOCK)
    used = blk < n_used
    last_e = jnp.sum(jnp.where(blk == n_used - 1, block_e, 0))
    block_e = jnp.where(used, block_e, last_e)
    row_start = jnp.where(used, row_start, 0)
    n_rows = jnp.where(used, n_rows, 0)
    return block_e, row_start, n_rows, (n_used - 1).reshape(1), sorted_tok, sorted_w


def _block_lists_kernel(rs_ref, nr_ref, tok_ref, w_ref, off_ref, wout_ref, *, t):
    lane = lax.broadcasted_iota(jnp.int32, (LIST_ROWS, LANES), 1)
    row = lax.broadcasted_iota(jnp.int32, (LIST_ROWS, LANES), 0)

    def one_block(b, carry):
        start = rs_ref[b]
        n_rows = nr_ref[b]
        q = start // LANES
        lane0 = start % LANES
        q8 = pl.multiple_of((q // SUBLANES) * SUBLANES, SUBLANES)

        def window(ref):
            a = ref[pl.ds(q8, 2 * SUBLANES), :]
            a = pltpu.roll(a, 2 * SUBLANES - (q - q8), axis=0)
            lo = pltpu.roll(a[0:LIST_ROWS], LANES - lane0, axis=1)
            hi = pltpu.roll(a[1:LIST_ROWS + 1], LANES - lane0, axis=1)
            return jnp.where(lane < LANES - lane0, lo, hi)

        valid = row * LANES + lane < n_rows
        off_ref[b] = jnp.where(valid, window(tok_ref) * SUBLANES, t * SUBLANES)
        wout_ref[b] = jnp.where(valid, window(w_ref), 0.0)
        return carry

    n_blocks = off_ref.shape[0]
    unroll = next(u for u in (4, 2, 1) if n_blocks % u == 0)

    def several_blocks(i, carry):
        for u in range(unroll):
            one_block(i * unroll + u, carry)
        return carry

    lax.fori_loop(0, n_blocks // unroll, several_blocks, 0)


def _block_lists(row_start, n_rows, sorted_tok, sorted_w, t):
    n_blocks = row_start.shape[0]
    n_slots = sorted_tok.shape[0]
    list_rows = n_slots // LANES + 2 * SUBLANES
    pad = list_rows * LANES - n_slots
    tok2d = jnp.pad(sorted_tok, (0, pad)).reshape(list_rows, LANES)
    w2d = jnp.pad(sorted_w, (0, pad)).reshape(list_rows, LANES)
    full = lambda i, rs, nr: (0, 0)
    blk = pl.BlockSpec((n_blocks, LIST_ROWS, LANES), lambda i, rs, nr: (0, 0, 0))
    off, wts = pl.pallas_call(
        functools.partial(_block_lists_kernel, t=t),
        grid_spec=pltpu.PrefetchScalarGridSpec(
            num_scalar_prefetch=2,
            grid=(1,),
            in_specs=[pl.BlockSpec((list_rows, LANES), full), pl.BlockSpec((list_rows, LANES), full)],
            out_specs=[blk, blk],
        ),
        out_shape=[jax.ShapeDtypeStruct((n_blocks, LIST_ROWS, LANES), jnp.int32),
                   jax.ShapeDtypeStruct((n_blocks, LIST_ROWS, LANES), F32)],
        compiler_params=_params(("arbitrary",)),
    )(row_start, n_rows, tok2d, w2d)
    return off.reshape(n_blocks * LIST_LEN), wts.reshape(n_blocks * LIST_LEN)


def _gather_kernel(nr_ref, last_ref, x_hbm, off_ref, out_ref, xs, xt, sem):
    step = pl.program_id(0)
    per_step = out_ref.shape[0] // MOE_BLOCK
    chunks = out_ref.shape[1] // LANES

    @pl.when(step == 0)
    def _load():
        rows = x_hbm.shape[0]
        cp = pltpu.make_async_copy(x_hbm, xs.at[pl.ds(0, rows)], sem)
        cp.start()
        cp.wait()
        xs[pl.ds(rows, SUBLANES), :] = jnp.zeros((SUBLANES, LANES), F32)
        xt[...] = jnp.zeros(xt.shape, F32)

    def one_block(g):
        n_rows = nr_ref[step * per_step + g]

        @pl.when(n_rows > 0)
        def _block():
            for seg in range(MOE_BLOCK // ROW_SEGMENT):
                @pl.when(n_rows > seg * ROW_SEGMENT)
                def _gather(seg=seg):
                    for r in range(seg * ROW_SEGMENT, (seg + 1) * ROW_SEGMENT):
                        src = pl.multiple_of(off_ref[g * LIST_LEN + r], SUBLANES)
                        xt[pl.ds(r, chunks, stride=ROW_STRIDE), :] = xs[pl.ds(src, SUBLANES), :]

            x = jnp.concatenate(
                [xt[j * ROW_STRIDE:j * ROW_STRIDE + MOE_BLOCK, :] for j in range(chunks)], axis=1)
            out_ref[g * MOE_BLOCK:(g + 1) * MOE_BLOCK, :] = x.astype(out_ref.dtype)

    for g in range(per_step):
        one_block(g)


def _expert_kernel(be_ref, nr_ref, last_ref, x_ref, wg_hbm, wu_hbm, wd_hbm, off_ref, w_ref, y_hbm,
                   ys, ot, wgb, wub, wdb, sem, wsem, *, t, n_steps):
    step = pl.program_id(0)
    per_step = x_ref.shape[0] // MOE_BLOCK
    rows = t * SUBLANES
    chunks = x_ref.shape[1] // LANES
    group = ROW_GROUP
    last = last_ref[0]
    ahead = WEIGHT_BUFFERS - 1

    def weight_copies(blk):
        e = be_ref[blk]
        slot = blk % WEIGHT_BUFFERS
        return [pltpu.make_async_copy(src.at[e], dst.at[slot], wsem.at[slot])
                for src, dst in ((wg_hbm, wgb), (wu_hbm, wub), (wd_hbm, wdb))]

    @pl.when(step == 0)
    def _init():
        for blk in range(ahead):
            @pl.when(blk <= last)
            def _(blk=blk):
                for cp in weight_copies(blk):
                    cp.start()
        ys[...] = jnp.zeros(ys.shape, F32)

    def one_block(g):
        b = step * per_step + g
        n_rows = nr_ref[b]
        row0 = g * MOE_BLOCK
        slot0 = g * LIST_LEN

        @pl.when(b + ahead <= last)
        def _prefetch():
            for cp in weight_copies(b + ahead):
                cp.start()

        @pl.when(b <= last)
        def _block():
            slot = b % WEIGHT_BUFFERS
            for cp in weight_copies(b):
                cp.wait()
            x = x_ref[row0:row0 + MOE_BLOCK, :]
            gate = jnp.dot(x, wgb[slot].astype(BF16), preferred_element_type=F32)
            up = jnp.dot(x, wub[slot].astype(BF16), preferred_element_type=F32)
            hidden = (_silu(gate) * up).astype(BF16)
            out = jnp.dot(hidden, wdb[slot].astype(BF16), preferred_element_type=F32)
            for j in range(chunks):
                ot[j * ROW_STRIDE:j * ROW_STRIDE + MOE_BLOCK, :] = out[:, j * LANES:(j + 1) * LANES]
            for seg in range(MOE_BLOCK // ROW_SEGMENT):
                @pl.when(n_rows > seg * ROW_SEGMENT)
                def _scatter(seg=seg):
                    for r0 in range(seg * ROW_SEGMENT, (seg + 1) * ROW_SEGMENT, group):
                        dst = [pl.multiple_of(off_ref[slot0 + r0 + i], SUBLANES) for i in range(group)]
                        vals = [ys[pl.ds(dst[i], SUBLANES), :]
                                + w_ref[slot0 + r0 + i] * ot[pl.ds(r0 + i, chunks, stride=ROW_STRIDE), :]
                                for i in range(group)]
                        for i in range(group):
                            ys[pl.ds(dst[i], SUBLANES), :] = vals[i]

    for g in range(per_step):
        one_block(g)

    @pl.when(step == n_steps - 1)
    def _store():
        cp = pltpu.make_async_copy(ys.at[pl.ds(0, rows)], y_hbm, sem)
        cp.start()
        cp.wait()


def _routed_experts(h2_rows, block_e, row_start, n_rows, last_block, sorted_tok, sorted_w, w_gate, w_up, w_down):
    rows, lanes = h2_rows.shape
    t = rows // SUBLANES
    n_blocks = block_e.shape[0]
    d, e_dim = w_gate.shape[1], w_gate.shape[2]
    assert d == SUBLANES * LANES and lanes == LANES
    offsets, weights = _block_lists(row_start, n_rows, sorted_tok, sorted_w, t)
    staging = pltpu.VMEM((SUBLANES * ROW_STRIDE, LANES), F32)
    resident = pltpu.VMEM((rows + SUBLANES, LANES), F32)

    def step_map(per_step):
        return lambda s, *prefetch: (jnp.minimum(s, prefetch[-1][0] // per_step), 0)

    def list_spec(per_step):
        index = step_map(per_step)
        return pl.BlockSpec((per_step * LIST_LEN,), lambda s, *prefetch: index(s, *prefetch)[:1],
                            memory_space=pltpu.SMEM)

    per_step = next(p for p in GATHER_BLOCKS_PER_STEP if n_blocks % p == 0)
    gathered = pl.pallas_call(
        _gather_kernel,
        grid_spec=pltpu.PrefetchScalarGridSpec(
            num_scalar_prefetch=2,
            grid=(n_blocks // per_step,),
            in_specs=[pl.BlockSpec(memory_space=pl.ANY), list_spec(per_step)],
            out_specs=pl.BlockSpec((per_step * MOE_BLOCK, d), step_map(per_step)),
            scratch_shapes=[resident, staging, pltpu.SemaphoreType.DMA(())],
        ),
        out_shape=jax.ShapeDtypeStruct((n_blocks * MOE_BLOCK, d), BF16),
        compiler_params=_params(("arbitrary",)),
    )(n_rows, last_block, h2_rows, offsets)

    per_step = next(p for p in range(MAX_BLOCKS_PER_STEP, 0, -1) if n_blocks % p == 0)
    n_steps = n_blocks // per_step
    hbm = pl.BlockSpec(memory_space=pl.ANY)
    return pl.pallas_call(
        functools.partial(_expert_kernel, t=t, n_steps=n_steps),
        grid_spec=pltpu.PrefetchScalarGridSpec(
            num_scalar_prefetch=3,
            grid=(n_steps,),
            in_specs=[pl.BlockSpec((per_step * MOE_BLOCK, d), step_map(per_step)), hbm, hbm, hbm,
                      list_spec(per_step), list_spec(per_step)],
            out_specs=hbm,
            scratch_shapes=[
                resident,
                staging,
                pltpu.VMEM((WEIGHT_BUFFERS, d, e_dim), F32),
                pltpu.VMEM((WEIGHT_BUFFERS, d, e_dim), F32),
                pltpu.VMEM((WEIGHT_BUFFERS, e_dim, d), F32),
                pltpu.SemaphoreType.DMA(()),
                pltpu.SemaphoreType.DMA((WEIGHT_BUFFERS,)),
            ],
        ),
        out_shape=jax.ShapeDtypeStruct((rows, LANES), F32),
        compiler_params=_params(("arbitrary",)),
    )(block_e, n_rows, last_block, gathered, w_gate, w_up, w_down, offsets, weights)


def _final_kernel(x1_ref, routed_ref, mod_ref, wg_ref, wu_ref, wd_ref, g_ref, b_ref,
                  yc_ref, yl_ref, st_ref, wgb_ref, wub_ref, wdb_ref, *, d, n_ctx_tiles):
    for src, dst in ((wg_ref, wgb_ref), (wu_ref, wub_ref), (wd_ref, wdb_ref)):
        _cast_weight_once(src, dst)
    is_ctx = pl.program_id(0) < n_ctx_tiles
    shift2 = mod_ref[0, :, 3 * d:4 * d]
    scale2 = mod_ref[0, :, 4 * d:5 * d]
    gate2 = mod_ref[0, :, 5 * d:6 * d]
    x1 = x1_ref[...]
    h = (x1 * (1.0 + scale2) + shift2).astype(BF16)
    gate = jnp.dot(h, wgb_ref[...], preferred_element_type=F32)
    up = jnp.dot(h, wub_ref[...], preferred_element_type=F32)
    shared = jnp.dot((_silu(gate) * up).astype(BF16), wdb_ref[...], preferred_element_type=F32)
    ffn = _slabs_to_rows(routed_ref, st_ref, h.shape[0]) + shared
    y = _layer_norm(DEEPNORM_ALPHA * x1 + gate2 * ffn, g_ref[...], b_ref[...])

    @pl.when(is_ctx)
    def _():
        yc_ref[...] = y

    @pl.when(jnp.logical_not(is_ctx))
    def _():
        yl_ref[...] = y


def _final(x1, routed, mod3, w_g, w_u, w_d, ln_g, ln_b, mod_map, t_ctx):
    t, d = x1.shape
    t_lat = t - t_ctx
    n_ctx_tiles = t_ctx // TOKEN_TILE
    tm = TOKEN_TILE
    row = lambda i: (i, 0)
    full = lambda i: (0, 0)
    return pl.pallas_call(
        functools.partial(_final_kernel, d=d, n_ctx_tiles=n_ctx_tiles),
        grid=(t // tm,),
        in_specs=[
            pl.BlockSpec((tm, d), row),
            pl.BlockSpec((tm * SUBLANES, LANES), row),
            pl.BlockSpec((1, 1, mod3.shape[2]), mod_map),
            _resident(w_g.shape),
            _resident(w_u.shape),
            _resident(w_d.shape),
            pl.BlockSpec((1, d), full),
            pl.BlockSpec((1, d), full),
        ],
        out_specs=_pair_specs(d, n_ctx_tiles),
        out_shape=[jax.ShapeDtypeStruct((t_ctx, d), F32), jax.ShapeDtypeStruct((t_lat, d), F32)],
        scratch_shapes=[pltpu.VMEM((SUBLANES * TILE_STRIDE, LANES), F32),
                        pltpu.VMEM(w_g.shape, BF16), pltpu.VMEM(w_u.shape, BF16), pltpu.VMEM(w_d.shape, BF16)],
        compiler_params=_params(("arbitrary",)),
    )(x1, routed, mod3, w_g, w_u, w_d, ln_g, ln_b)


def kernel(x_prompt, x_sample, cache_k, cache_v, state_hgrn, c, c_ctx, w_mod, b_mod, w_in, hg_lb, hg_norm_g, q_norm_g, k_norm_g, w_branch_a, w_branch_b, w_out, ln1_g, ln1_b, w_router, router_bias, w_e_gate, w_e_up, w_e_down, w_s_gate, w_s_up, w_s_down, ln2_g, ln2_b):
    assert w_mod.shape[0] == DEPTH
    n_ctx, seq_ctx, d = x_prompt.shape
    n_lat, seq_lat, _ = x_sample.shape
    t_ctx = n_ctx * seq_ctx
    t_lat = n_lat * seq_lat
    assert seq_ctx == TOKEN_TILE and seq_lat % TOKEN_TILE == 0
    layer = 0

    lb = jnp.cumsum(jax.nn.softmax(hg_lb.astype(F32), axis=0), axis=0)[layer]

    cond = jnp.concatenate([c_ctx[None, :], c], axis=0)
    cond = jnp.pad(cond, ((0, (-cond.shape[0]) % SUBLANES), (0, 0)))
    mod = _modulation(cond, w_mod[layer], b_mod[layer][None, :])
    mod3 = mod.reshape(mod.shape[0], 1, mod.shape[1])
    mod_map = _mod_row_map(t_ctx // TOKEN_TILE, seq_lat // TOKEN_TILE)

    x_ctx = x_prompt.reshape(t_ctx, d)
    x_lat = x_sample.reshape(t_lat, d)
    zh, zf, za, zg = _input_projection(x_ctx, x_lat, mod3, w_in[layer], lb, mod_map)

    norm_g = hg_norm_g[layer][None, :]
    oa_ctx, s_ctx = _hgrn_scan(zh, zf, norm_g, None, batch=n_ctx, seq=seq_ctx, row_block0=0,
                               heads=HG_HEADS_PER_STEP_CTX)
    oa_lat, _ = _hgrn_scan(zh, zf, norm_g, state_hgrn[:, layer], batch=n_lat, seq=seq_lat,
                           row_block0=t_ctx // seq_lat, heads=HG_HEADS_PER_STEP_LATENT)

    q_gain = jnp.tile(q_norm_g[layer], N_HEADS)[None, :]
    k_gain = jnp.tile(k_norm_g[layer], N_KV_HEADS)[None, :]
    lane = jnp.arange(ATT_WIDTH)
    group_ones = (lane[:, None] // HEAD_DIM == lane[None, :] // HEAD_DIM).astype(BF16)
    ob_ctx, k_ctx, v_ctx = _attention(za, q_gain, k_gain, group_ones, batch=n_ctx, seq=seq_ctx, row_block0=0)
    past = cache_k.shape[2]
    ob_lat = _attention(
        za, q_gain, k_gain, group_ones, batch=n_lat, seq=seq_lat, row_block0=t_ctx // seq_lat,
        rope=_rope_tables(seq_lat),
        cache=(cache_k[:, layer].reshape(n_lat, past, KV_WIDTH), cache_v[:, layer].reshape(n_lat, past, KV_WIDTH)))

    wr_t = w_router[layer].T
    wr_hi = wr_t.astype(BF16)
    wr_lo = (wr_t - wr_hi.astype(F32)).astype(BF16)
    x1, h2_slabs, idx_t, w_t = _post_mixer(
        x_ctx, x_lat, mod3, oa_ctx, oa_lat, ob_ctx, ob_lat, zg, w_branch_a[layer], w_branch_b[layer], w_out[layer],
        ln1_g[layer][None, :], ln1_b[layer][None, :], wr_hi, wr_lo, router_bias[layer][:, None], mod_map)

    routed = _routed_experts(h2_slabs, *_dispatch_lists(idx_t, w_t), w_e_gate[layer], w_e_up[layer], w_e_down[layer])

    y_ctx, y_lat = _final(x1, routed, mod3, w_s_gate[layer], w_s_up[layer], w_s_down[layer],
                          ln2_g[layer][None, :], ln2_b[layer][None, :], mod_map, t_ctx)

    y_prompt = y_ctx.reshape(n_ctx, seq_ctx, d)
    y_sample = y_lat.reshape(n_lat, seq_lat, d)
    new_cache_k = k_ctx.reshape(n_ctx, 1, seq_ctx, N_KV_HEADS, HEAD_DIM)
    new_cache_v = v_ctx.reshape(n_ctx, 1, seq_ctx, N_KV_HEADS, HEAD_DIM)
    new_state = s_ctx[:, None]
    return (y_prompt, y_sample, new_cache_k, new_cache_v, new_state)
```

```python
import functools

import jax
import jax.numpy as jnp
from jax import lax
from jax.experimental import pallas as pl
from jax.experimental.pallas import tpu as pltpu

F32 = jnp.float32
BF16 = jnp.bfloat16

GRID_W = 64
HG_HEADS = 4
HG_DK = 128
HG_DV = 128
HG_WIDTH = HG_HEADS * HG_DK
N_HEADS = 8
N_KV_HEADS = 2
HEAD_DIM = 64
ATT_WIDTH = N_HEADS * HEAD_DIM
KV_WIDTH = N_KV_HEADS * HEAD_DIM
ROPE_THETA = 10000.0
N_EXPERTS = 256
TOP_K = 8
N_GROUPS = 8
TOPK_GROUPS = 4
GROUP_SIZE = N_EXPERTS // N_GROUPS
ROUTED_SCALE = 2.5
NORM_EPS = 1e-6
DEPTH = 1
DEEPNORM_ALPHA = (2 * DEPTH) ** 0.25

LANES = 128
SUBLANES = 8
VMEM_LIMIT = 56 * 1024 * 1024

CAST_CHUNK = 512
TOKEN_TILE = 256
TILE_STRIDE = TOKEN_TILE + 1
HG_CHUNK = 32
HG_HEADS_PER_STEP_CTX = 4
HG_HEADS_PER_STEP_LATENT = 4
ATT_Q_BLOCK = 128
MOE_BLOCK = 320
BLOCK_COUNT_MULTIPLE = 20
ROW_STRIDE = MOE_BLOCK + 1
LIST_LEN = 512
LIST_ROWS = LIST_LEN // LANES
ROW_GROUP = 8
ROW_SEGMENT = 32
MAX_BLOCKS_PER_STEP = 4
GATHER_BLOCKS_PER_STEP = (10, 4, 2, 1)
WEIGHT_BUFFERS = 3


def _sigmoid(x):
    return 1.0 / (1.0 + jnp.exp(-x))


def _silu(x):
    return x * _sigmoid(x)


def _params(sem=None):
    return pltpu.CompilerParams(dimension_semantics=sem, vmem_limit_bytes=VMEM_LIMIT)


def _resident(shape):
    return pl.BlockSpec(shape, lambda i: (0,) * len(shape), pipeline_mode=pl.Buffered(1))


def _cast_weight_once(src_ref, dst_ref):
    @pl.when(pl.program_id(0) == 0)
    def _():
        width = src_ref.shape[1]
        for lo in range(0, width, CAST_CHUNK):
            hi = min(lo + CAST_CHUNK, width)
            dst_ref[:, lo:hi] = src_ref[:, lo:hi].astype(dst_ref.dtype)


def _mod_kernel(c_ref, w_ref, b_ref, o_ref):
    s = _silu(c_ref[...]).astype(BF16)
    o_ref[...] = jnp.dot(s, w_ref[...].astype(BF16), preferred_element_type=F32) + b_ref[...]


def _modulation(cond, w_mod, b_mod):
    n, d = cond.shape
    width = w_mod.shape[1]
    tn = width // 4
    return pl.pallas_call(
        _mod_kernel,
        grid=(4,),
        in_specs=[
            pl.BlockSpec((n, d), lambda j: (0, 0)),
            pl.BlockSpec((d, tn), lambda j: (0, j)),
            pl.BlockSpec((1, tn), lambda j: (0, j)),
        ],
        out_specs=pl.BlockSpec((n, tn), lambda j: (0, j)),
        out_shape=jax.ShapeDtypeStruct((n, width), F32),
        compiler_params=_params(("arbitrary",)),
    )(cond, w_mod, b_mod)


def _inproj_kernel(xc_ref, xl_ref, mod_ref, w_ref, lb_ref, zh_ref, zf_ref, za_ref, zg_ref, wb_ref,
                   *, d, n_ctx_tiles):
    _cast_weight_once(w_ref, wb_ref)
    shift = mod_ref[0, :, 0:d]
    scale = mod_ref[0, :, d:2 * d]
    x = _pick(pl.program_id(0) < n_ctx_tiles, xc_ref, xl_ref)
    h = (x * (1.0 + scale) + shift).astype(BF16)

    def proj(lo, hi):
        return jnp.dot(h, wb_ref[:, lo:hi], preferred_element_type=F32)

    w = HG_WIDTH
    zh_ref[:, 0:w] = _silu(proj(0, w)).astype(zh_ref.dtype)
    zh_ref[:, w:2 * w] = proj(w, 2 * w).astype(zh_ref.dtype)
    for i in range(2):
        lb = lb_ref[i:i + 1, :]
        zf_ref[:, i * w:(i + 1) * w] = lb + (1.0 - lb) * _sigmoid(proj((2 + i) * w, (3 + i) * w))
    zh_ref[:, 2 * w:3 * w] = _silu(proj(4 * w, 5 * w)).astype(zh_ref.dtype)
    a0 = 5 * w
    a1 = a0 + ATT_WIDTH + 2 * KV_WIDTH
    za_ref[...] = proj(a0, a1)
    for i in range(4):
        lo = a1 + i * (d // 2)
        zg_ref[:, i * (d // 2):(i + 1) * (d // 2)] = _sigmoid(proj(lo, lo + d // 2)).astype(zg_ref.dtype)


def _mod_row_map(n_ctx_tiles, tiles_per_latent):
    def index_map(i):
        row = jnp.where(i < n_ctx_tiles, 0, 1 + (i - n_ctx_tiles) // tiles_per_latent)
        return (row, 0, 0)
    return index_map


def _pair_specs(width, n_ctx_tiles):
    return [pl.BlockSpec((TOKEN_TILE, width), lambda i: (jnp.minimum(i, n_ctx_tiles - 1), 0)),
            pl.BlockSpec((TOKEN_TILE, width), lambda i: (jnp.maximum(i - n_ctx_tiles, 0), 0))]


def _pick(is_ctx, ctx_ref, lat_ref):
    return jnp.where(is_ctx, ctx_ref[...], lat_ref[...])


def _input_projection(x_ctx, x_lat, mod3, w_in, lb, mod_map):
    d = x_ctx.shape[1]
    n_ctx_tiles = x_ctx.shape[0] // TOKEN_TILE
    t = x_ctx.shape[0] + x_lat.shape[0]
    width = w_in.shape[1]
    zh_w = 3 * HG_WIDTH
    zf_w = 2 * HG_WIDTH
    za_w = ATT_WIDTH + 2 * KV_WIDTH
    zg_w = 2 * d
    assert width == zh_w + zf_w + za_w + zg_w
    tm = TOKEN_TILE
    return pl.pallas_call(
        functools.partial(_inproj_kernel, d=d, n_ctx_tiles=n_ctx_tiles),
        grid=(t // tm,),
        in_specs=_pair_specs(d, n_ctx_tiles) + [
            pl.BlockSpec((1, 1, mod3.shape[2]), mod_map),
            _resident((d, width)),
            pl.BlockSpec((2, HG_WIDTH), lambda i: (0, 0)),
        ],
        out_specs=[
            pl.BlockSpec((tm, zh_w), lambda i: (i, 0)),
            pl.BlockSpec((tm, zf_w), lambda i: (i, 0)),
            pl.BlockSpec((tm, za_w), lambda i: (i, 0)),
            pl.BlockSpec((tm, zg_w), lambda i: (i, 0)),
        ],
        out_shape=[
            jax.ShapeDtypeStruct((t, zh_w), BF16),
            jax.ShapeDtypeStruct((t, zf_w), F32),
            jax.ShapeDtypeStruct((t, za_w), F32),
            jax.ShapeDtypeStruct((t, zg_w), BF16),
        ],
        scratch_shapes=[pltpu.VMEM((d, width), BF16)],
        compiler_params=_params(("arbitrary",)),
    )(x_ctx, x_lat, mod3, w_in, lb)


def _hgrn_kernel(*refs, seq, has_state, heads):
    if has_state:
        q_ref, v_ref, ff_ref, fb_ref, gs_ref, ng_ref, s0_ref, o_ref, sout_ref = refs
    else:
        q_ref, v_ref, ff_ref, fb_ref, gs_ref, ng_ref, o_ref, sout_ref = refs
        s0_ref = None
    for hd in range(heads):
        _hgrn_head(q_ref, v_ref, ff_ref, fb_ref, gs_ref, ng_ref, s0_ref, o_ref, sout_ref, hd, seq)


def _hgrn_head(q_ref, v_ref, ff_ref, fb_ref, gs_ref, ng_ref, s0_ref, o_ref, sout_ref, hd, seq):
    has_state = s0_ref is not None
    cols = slice(hd * HG_DK, (hd + 1) * HG_DK)
    c = HG_CHUNK
    n = seq // c
    q3 = q_ref[:, cols].astype(F32).reshape(n, c, HG_DK)
    v3 = v_ref[:, cols].reshape(n, c, HG_DV)
    pos = lax.broadcasted_iota(jnp.int32, (seq, HG_DK), 0) % c
    t_idx = lax.broadcasted_iota(jnp.int32, (c, c), 0)
    s_idx = lax.broadcasted_iota(jnp.int32, (c, c), 1)
    o_sum = None
    for direction, f_ref in enumerate((ff_ref, fb_ref)):
        reverse = direction == 1
        f = f_ref[:, cols]
        k3 = (1.0 - f).reshape(n, c, HG_DK)
        b = jnp.log(f)
        step = 1
        while step < c:
            if reverse:
                b = b + jnp.where(pos < c - step, pltpu.roll(b, seq - step, axis=0), 0.0)
            else:
                b = b + jnp.where(pos >= step, pltpu.roll(b, step, axis=0), 0.0)
            step *= 2
        b3 = b.reshape(n, c, HG_DK)
        edge = b3[:, 0:1, :] if reverse else b3[:, c - 1:c, :]
        mid = b3[:, c // 2:c // 2 + 1, :]
        q_rel = q3 * jnp.exp(b3 - mid)
        k_rel = k3 * jnp.exp(mid - b3)
        q_mid = q_rel.astype(BF16)
        k_mid = k_rel.astype(BF16)
        q_in = (q_rel * jnp.exp(mid)).astype(BF16)
        k_end = (k_rel * jnp.exp(edge - mid)).astype(BF16)
        scores = jnp.einsum('ntd,nsd->nts', q_mid, k_mid, preferred_element_type=F32)
        keep = (s_idx >= t_idx) if reverse else (s_idx <= t_idx)
        scores = jnp.where(keep[None], scores, 0.0).astype(BF16)
        o_intra = jnp.einsum('nts,nsv->ntv', scores, v3, preferred_element_type=F32)
        upd = jnp.einsum('nsv,nsd->nvd', v3, k_end, preferred_element_type=F32)
        dec = jnp.exp(edge)
        if has_state:
            st = s0_ref[0, direction, hd].T
        else:
            st = jnp.zeros((HG_DV, HG_DK), F32)
        before = [None] * n
        for ci in (range(n - 1, -1, -1) if reverse else range(n)):
            before[ci] = st.astype(BF16)
            st = st * dec[ci] + upd[ci]
        st_before = jnp.stack(before, axis=0)
        o_inter = jnp.einsum('ntd,nvd->ntv', q_in, st_before, preferred_element_type=F32)
        o_dir = (o_intra + o_inter).reshape(seq, HG_DV)
        o_sum = o_dir if o_sum is None else o_sum + o_dir
        sout_ref[0, direction, hd] = st.T
    ms = jnp.mean(o_sum * o_sum, axis=-1, keepdims=True)
    o = o_sum * lax.rsqrt(ms + NORM_EPS) * ng_ref[...]
    o_ref[:, cols] = (o * gs_ref[:, cols].astype(F32)).astype(o_ref.dtype)


def _hgrn_scan(zh, zf, norm_g, s0, *, batch, seq, row_block0, heads):
    has_state = s0 is not None
    h = HG_HEADS
    groups = h // heads
    width = heads * HG_DK

    def col(section):
        return pl.BlockSpec((seq, width), lambda b, j: (row_block0 + b, section * groups + j))

    in_specs = [col(0), col(1), col(0), col(1), col(2), pl.BlockSpec((1, HG_DV), lambda b, j: (0, 0))]
    args = [zh, zh, zf, zf, zh, norm_g]
    state_spec = pl.BlockSpec((1, 2, heads, HG_DK, HG_DV), lambda b, j: (b, 0, j, 0, 0))
    if has_state:
        in_specs.append(state_spec)
        args.append(s0)
    return pl.pallas_call(
        functools.partial(_hgrn_kernel, seq=seq, has_state=has_state, heads=heads),
        grid=(batch, groups),
        in_specs=in_specs,
        out_specs=[pl.BlockSpec((seq, width), lambda b, j: (b, j)), state_spec],
        out_shape=[
            jax.ShapeDtypeStruct((batch * seq, h * HG_DV), BF16),
            jax.ShapeDtypeStruct((batch, 2, h, HG_DK, HG_DV), F32),
        ],
        compiler_params=_params(("arbitrary", "arbitrary")),
    )(*args)


def _group_rms_norm(x, group_ones, gain):
    sq = x * x
    hi = sq.astype(BF16)
    lo = (sq - hi.astype(F32)).astype(BF16)
    total = (jnp.dot(hi, group_ones, preferred_element_type=F32)
             + jnp.dot(lo, group_ones, preferred_element_type=F32))
    return x * lax.rsqrt(total * (1.0 / HEAD_DIM) + NORM_EPS) * gain


def _rope(x, cos, sin_signed):
    width = x.shape[1]
    quarter = HEAD_DIM // 4
    lane = lax.broadcasted_iota(jnp.int32, x.shape, 1)
    partner = jnp.where(lane % (2 * quarter) < quarter,
                        pltpu.roll(x, width - quarter, axis=1),
                        pltpu.roll(x, quarter, axis=1))
    return x * cos + partner * sin_signed


def _attn_kernel(*refs, seq, latent):
    if latent:
        za_ref, qg_ref, kg_ref, gm_ref, cos_ref, sin_ref, ck_ref, cv_ref, o_ref = refs
    else:
        za_ref, qg_ref, kg_ref, gm_ref, o_ref, kout_ref, vout_ref = refs
    q = za_ref[:, 0:ATT_WIDTH]
    k = za_ref[:, ATT_WIDTH:ATT_WIDTH + KV_WIDTH]
    v = za_ref[:, ATT_WIDTH + KV_WIDTH:ATT_WIDTH + 2 * KV_WIDTH]
    qn = _group_rms_norm(q, gm_ref[...], qg_ref[...])
    kn = _group_rms_norm(k, gm_ref[0:KV_WIDTH, 0:KV_WIDTH], kg_ref[...])
    if latent:
        qn = _rope(qn, cos_ref[...], sin_ref[...])
        kr = _rope(kn, cos_ref[:, 0:KV_WIDTH], sin_ref[:, 0:KV_WIDTH])
    else:
        kout_ref[...] = kn
        vout_ref[...] = v
        kr = kn
    qb = (qn * (HEAD_DIM ** -0.5)).astype(BF16)
    kb = kr.astype(BF16)
    vb = v.astype(BF16)
    if latent:
        ckb = ck_ref[0].astype(BF16)
        cvb = cv_ref[0].astype(BF16)
    groups = N_HEADS // N_KV_HEADS
    tq = ATT_Q_BLOCK if latent else seq
    nt = (((1,), (1,)), ((), ()))
    for kh in range(N_KV_HEADS):
        ksl = slice(kh * HEAD_DIM, (kh + 1) * HEAD_DIM)
        k_new = kb[:, ksl]
        v_new = vb[:, ksl]
        for blk in range(seq // tq):
            rows = slice(blk * tq, (blk + 1) * tq)
            heads = [kh * groups + g for g in range(groups)]
            q_st = jnp.concatenate([qb[rows, hd * HEAD_DIM:(hd + 1) * HEAD_DIM] for hd in heads], axis=0)
            s_new = lax.dot_general(q_st, k_new, nt, preferred_element_type=F32)
            m = jnp.max(s_new, axis=-1, keepdims=True)
            if latent:
                s_ctx = lax.dot_general(q_st, ckb[:, ksl], nt, preferred_element_type=F32)
                m = jnp.maximum(m, jnp.max(s_ctx, axis=-1, keepdims=True))
            p_new = jnp.exp(s_new - m)
            denom = jnp.sum(p_new, axis=-1, keepdims=True)
            acc = jnp.dot(p_new.astype(BF16), v_new, preferred_element_type=F32)
            if latent:
                p_ctx = jnp.exp(s_ctx - m)
                denom = denom + jnp.sum(p_ctx, axis=-1, keepdims=True)
                acc = acc + jnp.dot(p_ctx.astype(BF16), cvb[:, ksl], preferred_element_type=F32)
            out = acc / denom
            for g in range(0, groups, 2):
                pair = jnp.concatenate([out[g * tq:(g + 1) * tq], out[(g + 1) * tq:(g + 2) * tq]], axis=1)
                lane0 = heads[g] * HEAD_DIM
                o_ref[rows, lane0:lane0 + 2 * HEAD_DIM] = pair.astype(o_ref.dtype)


def _attention(za, q_gain, k_gain, group_ones, *, batch, seq, row_block0, rope=None, cache=None):
    latent = cache is not None
    za_w = za.shape[1]
    in_specs = [
        pl.BlockSpec((seq, za_w), lambda b: (row_block0 + b, 0)),
        pl.BlockSpec((1, ATT_WIDTH), lambda b: (0, 0)),
        pl.BlockSpec((1, KV_WIDTH), lambda b: (0, 0)),
        pl.BlockSpec((ATT_WIDTH, ATT_WIDTH), lambda b: (0, 0)),
    ]
    args = [za, q_gain, k_gain, group_ones]
    o_spec = pl.BlockSpec((seq, ATT_WIDTH), lambda b: (b, 0))
    o_shape = jax.ShapeDtypeStruct((batch * seq, ATT_WIDTH), BF16)
    if latent:
        cos, sin_signed = rope
        ck, cv = cache
        past = ck.shape[1]
        in_specs += [
            pl.BlockSpec((seq, ATT_WIDTH), lambda b: (0, 0)),
            pl.BlockSpec((seq, ATT_WIDTH), lambda b: (0, 0)),
            pl.BlockSpec((1, past, KV_WIDTH), lambda b: (b, 0, 0)),
            pl.BlockSpec((1, past, KV_WIDTH), lambda b: (b, 0, 0)),
        ]
        args += [cos, sin_signed, ck, cv]
        out_specs = o_spec
        out_shape = o_shape
    else:
        kv_spec = pl.BlockSpec((seq, KV_WIDTH), lambda b: (b, 0))
        kv_shape = jax.ShapeDtypeStruct((batch * seq, KV_WIDTH), F32)
        out_specs = [o_spec, kv_spec, kv_spec]
        out_shape = [o_shape, kv_shape, kv_shape]
    return pl.pallas_call(
        functools.partial(_attn_kernel, seq=seq, latent=latent),
        grid=(batch,),
        in_specs=in_specs,
        out_specs=out_specs,
        out_shape=out_shape,
        compiler_params=_params(("arbitrary",)),
    )(*args)


def _rope_tables(seq):
    quarter = HEAD_DIM // 4
    t = jnp.arange(seq)
    row = (t // GRID_W).astype(F32)
    colp = (t % GRID_W).astype(F32)
    inv_freq = ROPE_THETA ** (-jnp.arange(quarter, dtype=F32) / quarter)
    lane = jnp.arange(HEAD_DIM)
    pos = jnp.where((lane < HEAD_DIM // 2)[None, :], row[:, None], colp[:, None])
    ang = pos * inv_freq[lane % quarter][None, :]
    sign = jnp.where(lane % (2 * quarter) < quarter, -1.0, 1.0)[None, :]
    cos = jnp.tile(jnp.cos(ang), (1, N_HEADS))
    sin_signed = jnp.tile(jnp.sin(ang) * sign, (1, N_HEADS))
    return cos, sin_signed


def _layer_norm(y, g, b):
    mu = jnp.mean(y, axis=-1, keepdims=True)
    yc = y - mu
    var = jnp.mean(yc * yc, axis=-1, keepdims=True)
    return yc * lax.rsqrt(var + NORM_EPS) * g + b


def _rows_to_slabs(tile, st_ref, slab_ref):
    rows, d = tile.shape
    chunks = d // LANES
    for j in range(chunks):
        st_ref[j * TILE_STRIDE:j * TILE_STRIDE + rows, :] = tile[:, j * LANES:(j + 1) * LANES]
    for r in range(rows):
        slab_ref[r * chunks:(r + 1) * chunks, :] = st_ref[pl.ds(r, chunks, stride=TILE_STRIDE), :]


def _slabs_to_rows(slab_ref, st_ref, rows):
    chunks = SUBLANES
    for r in range(rows):
        st_ref[pl.ds(r, chunks, stride=TILE_STRIDE), :] = slab_ref[r * chunks:(r + 1) * chunks, :]
    return jnp.concatenate([st_ref[j * TILE_STRIDE:j * TILE_STRIDE + rows, :] for j in range(chunks)], axis=1)


def _post_mixer_kernel(xc_ref, xl_ref, mod_ref, oac_ref, oal_ref, obc_ref, obl_ref, zg_ref, wa_ref, wb_ref,
                       wo_ref, g_ref, b_ref, wrh_ref, wrl_ref, rb_ref,
                       x1_ref, slab_ref, idx_ref, rw_ref, st_ref, wab_ref, wbb_ref, wob_ref,
                       *, d, n_ctx_tiles):
    for src, dst in ((wa_ref, wab_ref), (wb_ref, wbb_ref), (wo_ref, wob_ref)):
        _cast_weight_once(src, dst)
    is_ctx = pl.program_id(0) < n_ctx_tiles
    gate1 = mod_ref[0, :, 2 * d:3 * d]
    shift2 = mod_ref[0, :, 3 * d:4 * d]
    scale2 = mod_ref[0, :, 4 * d:5 * d]
    branch_a = jnp.dot(_pick(is_ctx, oac_ref, oal_ref), wab_ref[...], preferred_element_type=F32)
    branch_b = jnp.dot(_pick(is_ctx, obc_ref, obl_ref), wbb_ref[...], preferred_element_type=F32)
    merged = zg_ref[:, 0:d].astype(F32) * branch_a + zg_ref[:, d:2 * d].astype(F32) * branch_b
    mix = jnp.dot(merged.astype(BF16), wob_ref[...], preferred_element_type=F32)
    x = _pick(is_ctx, xc_ref, xl_ref)
    x1 = _layer_norm(DEEPNORM_ALPHA * x + gate1 * mix, g_ref[...], b_ref[...])
    x1_ref[...] = x1
    h2 = x1 * (1.0 + scale2) + shift2
    _rows_to_slabs(h2, st_ref, slab_ref)
    idx_ref[...], rw_ref[...] = _route_tile(h2, wrh_ref[...], wrl_ref[...], rb_ref[...])


def _post_mixer(x_ctx, x_lat, mod3, oa_ctx, oa_lat, ob_ctx, ob_lat, zg, w_a, w_b, w_o, ln_g, ln_b,
                wr_hi, wr_lo, router_bias, mod_map):
    d = x_ctx.shape[1]
    n_ctx_tiles = x_ctx.shape[0] // TOKEN_TILE
    t = x_ctx.shape[0] + x_lat.shape[0]
    tm = TOKEN_TILE
    row = lambda i: (i, 0)
    full = lambda i: (0, 0)
    return pl.pallas_call(
        functools.partial(_post_mixer_kernel, d=d, n_ctx_tiles=n_ctx_tiles),
        grid=(t // tm,),
        in_specs=_pair_specs(d, n_ctx_tiles) + [
            pl.BlockSpec((1, 1, mod3.shape[2]), mod_map),
        ] + _pair_specs(oa_ctx.shape[1], n_ctx_tiles) + _pair_specs(ob_ctx.shape[1], n_ctx_tiles) + [
            pl.BlockSpec((tm, 2 * d), row),
            _resident(w_a.shape),
            _resident(w_b.shape),
            _resident(w_o.shape),
            pl.BlockSpec((1, d), full),
            pl.BlockSpec((1, d), full),
            pl.BlockSpec((N_EXPERTS, d), full),
            pl.BlockSpec((N_EXPERTS, d), full),
            pl.BlockSpec((N_EXPERTS, 1), full),
        ],
        out_specs=[pl.BlockSpec((tm, d), row),
                   pl.BlockSpec((tm * SUBLANES, LANES), row),
                   pl.BlockSpec((TOP_K, tm), lambda i: (0, i)), pl.BlockSpec((TOP_K, tm), lambda i: (0, i))],
        out_shape=[jax.ShapeDtypeStruct((t, d), F32),
                   jax.ShapeDtypeStruct((t * SUBLANES, LANES), F32),
                   jax.ShapeDtypeStruct((TOP_K, t), jnp.int32), jax.ShapeDtypeStruct((TOP_K, t), F32)],
        scratch_shapes=[pltpu.VMEM((SUBLANES * TILE_STRIDE, LANES), F32),
                        pltpu.VMEM(w_a.shape, BF16), pltpu.VMEM(w_b.shape, BF16), pltpu.VMEM(w_o.shape, BF16)],
        compiler_params=_params(("arbitrary",)),
    )(x_ctx, x_lat, mod3, oa_ctx, oa_lat, ob_ctx, ob_lat, zg, w_a, w_b, w_o, ln_g, ln_b, wr_hi, wr_lo, router_bias)


def _first_index_of_max(x, idx, sentinel):
    m = jnp.max(x, axis=0, keepdims=True)
    first = jnp.min(jnp.where(x == m, idx, sentinel), axis=0, keepdims=True)
    return m, first


def _route_tile(h, wh, wl, bias):
    hh = h.astype(BF16)
    hl = (h - hh.astype(F32)).astype(BF16)
    nt = (((1,), (1,)), ((), ()))
    logits = (lax.dot_general(wh, hh, nt, preferred_element_type=F32)
              + lax.dot_general(wh, hl, nt, preferred_element_type=F32)
              + lax.dot_general(wl, hh, nt, preferred_element_type=F32))
    scores = _sigmoid(logits)
    sel = scores + bias
    tm = sel.shape[1]
    neg = -jnp.inf
    gidx = lax.broadcasted_iota(jnp.int32, (GROUP_SIZE, tm), 0)
    group_scores = []
    for g in range(N_GROUPS):
        sg = sel[g * GROUP_SIZE:(g + 1) * GROUP_SIZE, :]
        m1, first = _first_index_of_max(sg, gidx, GROUP_SIZE)
        m2 = jnp.max(jnp.where(gidx == first, neg, sg), axis=0, keepdims=True)
        group_scores.append(m1 + m2)
    gs = jnp.concatenate(group_scores, axis=0)
    nidx = lax.broadcasted_iota(jnp.int32, (N_GROUPS, tm), 0)
    chosen = jnp.zeros((N_GROUPS, tm), jnp.bool_)
    for _ in range(TOPK_GROUPS):
        _, first = _first_index_of_max(gs, nidx, N_GROUPS)
        hit = nidx == first
        chosen = jnp.logical_or(chosen, hit)
        gs = jnp.where(hit, neg, gs)
    masked = jnp.concatenate(
        [jnp.where(chosen[g:g + 1, :], sel[g * GROUP_SIZE:(g + 1) * GROUP_SIZE, :], neg) for g in range(N_GROUPS)],
        axis=0)
    eidx = lax.broadcasted_iota(jnp.int32, (N_EXPERTS, tm), 0)
    picks, weights = [], []
    for _ in range(TOP_K):
        _, first = _first_index_of_max(masked, eidx, N_EXPERTS)
        hit = eidx == first
        picks.append(first)
        weights.append(jnp.sum(jnp.where(hit, scores, 0.0), axis=0, keepdims=True))
        masked = jnp.where(hit, neg, masked)
    wk = jnp.concatenate(weights, axis=0)
    return jnp.concatenate(picks, axis=0), wk / jnp.sum(wk, axis=0, keepdims=True) * ROUTED_SCALE


def _dispatch_lists(idx_t, w_t):
    k, t = idx_t.shape
    s = k * t
    assert N_EXPERTS * t < 2 ** 31
    n_blocks = -(-(s // MOE_BLOCK + N_EXPERTS) // BLOCK_COUNT_MULTIPLE) * BLOCK_COUNT_MULTIPLE
    flat_tok = jnp.arange(s, dtype=jnp.int32) % t
    sorted_key, sorted_w = lax.sort((idx_t.reshape(s) * t + flat_tok, w_t.reshape(s)), num_keys=1)
    sorted_tok = sorted_key % t
    experts = jnp.arange(N_EXPERTS + 1, dtype=jnp.int32)
    start = jnp.sum(sorted_key[None, :] < experts[:, None] * t, axis=1, dtype=jnp.int32)
    counts = start[1:] - start[:-1]
    blocks_per_e = (counts + MOE_BLOCK - 1) // MOE_BLOCK
    block_end = jnp.cumsum(blocks_per_e)
    n_used = block_end[-1]
    blk = jnp.arange(n_blocks, dtype=jnp.int32)
    block_e = jnp.sum(block_end[None, :] <= blk[:, None], axis=1, dtype=jnp.int32)
    block_e = jnp.minimum(block_e, N_EXPERTS - 1)
    onehot = (block_e[:, None] == experts[None, :N_EXPERTS]).astype(jnp.int32)
    pick = lambda a: jnp.sum(onehot * a[None, :], axis=1)
    offset = (blk - (pick(block_end) - pick(blocks_per_e))) * MOE_BLOCK
    row_start = pick(start[:-1]) + offset
    n_rows = jnp.clip(pick(counts) - offset, 0, MOE_BLOCK)
    used = blk < n_used
    last_e = jnp.sum(jnp.where(blk == n_used - 1, block_e, 0))
    block_e = jnp.where(used, block_e, last_e)
    row_start = jnp.where(used, row_start, 0)
    n_rows = jnp.where(used, n_rows, 0)
    return block_e, row_start, n_rows, (n_used - 1).reshape(1), sorted_tok, sorted_w


def _block_lists_kernel(rs_ref, nr_ref, tok_ref, w_ref, off_ref, wout_ref, *, t):
    lane = lax.broadcasted_iota(jnp.int32, (LIST_ROWS, LANES), 1)
    row = lax.broadcasted_iota(jnp.int32, (LIST_ROWS, LANES), 0)

    def one_block(b, carry):
        start = rs_ref[b]
        n_rows = nr_ref[b]
        q = start // LANES
        lane0 = start % LANES
        q8 = pl.multiple_of((q // SUBLANES) * SUBLANES, SUBLANES)

        def window(ref):
            a = ref[pl.ds(q8, 2 * SUBLANES), :]
            a = pltpu.roll(a, 2 * SUBLANES - (q - q8), axis=0)
            lo = pltpu.roll(a[0:LIST_ROWS], LANES - lane0, axis=1)
            hi = pltpu.roll(a[1:LIST_ROWS + 1], LANES - lane0, axis=1)
            return jnp.where(lane < LANES - lane0, lo, hi)

        valid = row * LANES + lane < n_rows
        off_ref[b] = jnp.where(valid, window(tok_ref) * SUBLANES, t * SUBLANES)
        wout_ref[b] = jnp.where(valid, window(w_ref), 0.0)
        return carry

    n_blocks = off_ref.shape[0]
    unroll = next(u for u in (4, 2, 1) if n_blocks % u == 0)

    def several_blocks(i, carry):
        for u in range(unroll):
            one_block(i * unroll + u, carry)
        return carry

    lax.fori_loop(0, n_blocks // unroll, several_blocks, 0)


def _block_lists(row_start, n_rows, sorted_tok, sorted_w, t):
    n_blocks = row_start.shape[0]
    n_slots = sorted_tok.shape[0]
    list_rows = n_slots // LANES + 2 * SUBLANES
    pad = list_rows * LANES - n_slots
    tok2d = jnp.pad(sorted_tok, (0, pad)).reshape(list_rows, LANES)
    w2d = jnp.pad(sorted_w, (0, pad)).reshape(list_rows, LANES)
    full = lambda i, rs, nr: (0, 0)
    blk = pl.BlockSpec((n_blocks, LIST_ROWS, LANES), lambda i, rs, nr: (0, 0, 0))
    off, wts = pl.pallas_call(
        functools.partial(_block_lists_kernel, t=t),
        grid_spec=pltpu.PrefetchScalarGridSpec(
            num_scalar_prefetch=2,
            grid=(1,),
            in_specs=[pl.BlockSpec((list_rows, LANES), full), pl.BlockSpec((list_rows, LANES), full)],
            out_specs=[blk, blk],
        ),
        out_shape=[jax.ShapeDtypeStruct((n_blocks, LIST_ROWS, LANES), jnp.int32),
                   jax.ShapeDtypeStruct((n_blocks, LIST_ROWS, LANES), F32)],
        compiler_params=_params(("arbitrary",)),
    )(row_start, n_rows, tok2d, w2d)
    return off.reshape(n_blocks * LIST_LEN), wts.reshape(n_blocks * LIST_LEN)


def _gather_kernel(nr_ref, last_ref, x_hbm, off_ref, out_ref, xs, xt, sem):
    step = pl.program_id(0)
    per_step = out_ref.shape[0] // MOE_BLOCK
    chunks = out_ref.shape[1] // LANES

    @pl.when(step == 0)
    def _load():
        rows = x_hbm.shape[0]
        cp = pltpu.make_async_copy(x_hbm, xs.at[pl.ds(0, rows)], sem)
        cp.start()
        cp.wait()
        xs[pl.ds(rows, SUBLANES), :] = jnp.zeros((SUBLANES, LANES), F32)
        xt[...] = jnp.zeros(xt.shape, F32)

    def one_block(g):
        n_rows = nr_ref[step * per_step + g]

        @pl.when(n_rows > 0)
        def _block():
            for seg in range(MOE_BLOCK // ROW_SEGMENT):
                @pl.when(n_rows > seg * ROW_SEGMENT)
                def _gather(seg=seg):
                    for r in range(seg * ROW_SEGMENT, (seg + 1) * ROW_SEGMENT):
                        src = pl.multiple_of(off_ref[g * LIST_LEN + r], SUBLANES)
                        xt[pl.ds(r, chunks, stride=ROW_STRIDE), :] = xs[pl.ds(src, SUBLANES), :]

            x = jnp.concatenate(
                [xt[j * ROW_STRIDE:j * ROW_STRIDE + MOE_BLOCK, :] for j in range(chunks)], axis=1)
            out_ref[g * MOE_BLOCK:(g + 1) * MOE_BLOCK, :] = x.astype(out_ref.dtype)

    for g in range(per_step):
        one_block(g)


def _expert_kernel(be_ref, nr_ref, last_ref, x_ref, wg_hbm, wu_hbm, wd_hbm, off_ref, w_ref, y_hbm,
                   ys, ot, wgb, wub, wdb, sem, wsem, *, t, n_steps):
    step = pl.program_id(0)
    per_step = x_ref.shape[0] // MOE_BLOCK
    rows = t * SUBLANES
    chunks = x_ref.shape[1] // LANES
    group = ROW_GROUP
    last = last_ref[0]
    ahead = WEIGHT_BUFFERS - 1

    def weight_copies(blk):
        e = be_ref[blk]
        slot = blk % WEIGHT_BUFFERS
        return [pltpu.make_async_copy(src.at[e], dst.at[slot], wsem.at[slot])
                for src, dst in ((wg_hbm, wgb), (wu_hbm, wub), (wd_hbm, wdb))]

    @pl.when(step == 0)
    def _init():
        for blk in range(ahead):
            @pl.when(blk <= last)
            def _(blk=blk):
                for cp in weight_copies(blk):
                    cp.start()
        ys[...] = jnp.zeros(ys.shape, F32)

    def one_block(g):
        b = step * per_step + g
        n_rows = nr_ref[b]
        row0 = g * MOE_BLOCK
        slot0 = g * LIST_LEN

        @pl.when(b + ahead <= last)
        def _prefetch():
            for cp in weight_copies(b + ahead):
                cp.start()

        @pl.when(b <= last)
        def _block():
            slot = b % WEIGHT_BUFFERS
            for cp in weight_copies(b):
                cp.wait()
            x = x_ref[row0:row0 + MOE_BLOCK, :]
            gate = jnp.dot(x, wgb[slot].astype(BF16), preferred_element_type=F32)
            up = jnp.dot(x, wub[slot].astype(BF16), preferred_element_type=F32)
            hidden = (_silu(gate) * up).astype(BF16)
            out = jnp.dot(hidden, wdb[slot].astype(BF16), preferred_element_type=F32)
            for j in range(chunks):
                ot[j * ROW_STRIDE:j * ROW_STRIDE + MOE_BLOCK, :] = out[:, j * LANES:(j + 1) * LANES]
            for seg in range(MOE_BLOCK // ROW_SEGMENT):
                @pl.when(n_rows > seg * ROW_SEGMENT)
                def _scatter(seg=seg):
                    for r0 in range(seg * ROW_SEGMENT, (seg + 1) * ROW_SEGMENT, group):
                        dst = [pl.multiple_of(off_ref[slot0 + r0 + i], SUBLANES) for i in range(group)]
                        vals = [ys[pl.ds(dst[i], SUBLANES), :]
                                + w_ref[slot0 + r0 + i] * ot[pl.ds(r0 + i, chunks, stride=ROW_STRIDE), :]
                                for i in range(group)]
                        for i in range(group):
                            ys[pl.ds(dst[i], SUBLANES), :] = vals[i]

    for g in range(per_step):
        one_block(g)

    @pl.when(step == n_steps - 1)
    def _store():
        cp = pltpu.make_async_copy(ys.at[pl.ds(0, rows)], y_hbm, sem)
        cp.start()
        cp.wait()


def _routed_experts(h2_rows, block_e, row_start, n_rows, last_block, sorted_tok, sorted_w, w_gate, w_up, w_down):
    rows, lanes = h2_rows.shape
    t = rows // SUBLANES
    n_blocks = block_e.shape[0]
    d, e_dim = w_gate.shape[1], w_gate.shape[2]
    assert d == SUBLANES * LANES and lanes == LANES
    offsets, weights = _block_lists(row_start, n_rows, sorted_tok, sorted_w, t)
    staging = pltpu.VMEM((SUBLANES * ROW_STRIDE, LANES), F32)
    resident = pltpu.VMEM((rows + SUBLANES, LANES), F32)

    def step_map(per_step):
        return lambda s, *prefetch: (jnp.minimum(s, prefetch[-1][0] // per_step), 0)

    def list_spec(per_step):
        index = step_map(per_step)
        return pl.BlockSpec((per_step * LIST_LEN,), lambda s, *prefetch: index(s, *prefetch)[:1],
                            memory_space=pltpu.SMEM)

    per_step = next(p for p in GATHER_BLOCKS_PER_STEP if n_blocks % p == 0)
    gathered = pl.pallas_call(
        _gather_kernel,
        grid_spec=pltpu.PrefetchScalarGridSpec(
            num_scalar_prefetch=2,
            grid=(n_blocks // per_step,),
            in_specs=[pl.BlockSpec(memory_space=pl.ANY), list_spec(per_step)],
            out_specs=pl.BlockSpec((per_step * MOE_BLOCK, d), step_map(per_step)),
            scratch_shapes=[resident, staging, pltpu.SemaphoreType.DMA(())],
        ),
        out_shape=jax.ShapeDtypeStruct((n_blocks * MOE_BLOCK, d), BF16),
        compiler_params=_params(("arbitrary",)),
    )(n_rows, last_block, h2_rows, offsets)

    per_step = next(p for p in range(MAX_BLOCKS_PER_STEP, 0, -1) if n_blocks % p == 0)
    n_steps = n_blocks // per_step
    hbm = pl.BlockSpec(memory_space=pl.ANY)
    return pl.pallas_call(
        functools.partial(_expert_kernel, t=t, n_steps=n_steps),
        grid_spec=pltpu.PrefetchScalarGridSpec(
            num_scalar_prefetch=3,
            grid=(n_steps,),
            in_specs=[pl.BlockSpec((per_step * MOE_BLOCK, d), step_map(per_step)), hbm, hbm, hbm,
                      list_spec(per_step), list_spec(per_step)],
            out_specs=hbm,
            scratch_shapes=[
                resident,
                staging,
                pltpu.VMEM((WEIGHT_BUFFERS, d, e_dim), F32),
                pltpu.VMEM((WEIGHT_BUFFERS, d, e_dim), F32),
                pltpu.VMEM((WEIGHT_BUFFERS, e_dim, d), F32),
                pltpu.SemaphoreType.DMA(()),
                pltpu.SemaphoreType.DMA((WEIGHT_BUFFERS,)),
            ],
        ),
        out_shape=jax.ShapeDtypeStruct((rows, LANES), F32),
        compiler_params=_params(("arbitrary",)),
    )(block_e, n_rows, last_block, gathered, w_gate, w_up, w_down, offsets, weights)


def _final_kernel(x1_ref, routed_ref, mod_ref, wg_ref, wu_ref, wd_ref, g_ref, b_ref,
                  yc_ref, yl_ref, st_ref, wgb_ref, wub_ref, wdb_ref, *, d, n_ctx_tiles):
    for src, dst in ((wg_ref, wgb_ref), (wu_ref, wub_ref), (wd_ref, wdb_ref)):
        _cast_weight_once(src, dst)
    is_ctx = pl.program_id(0) < n_ctx_tiles
    shift2 = mod_ref[0, :, 3 * d:4 * d]
    scale2 = mod_ref[0, :, 4 * d:5 * d]
    gate2 = mod_ref[0, :, 5 * d:6 * d]
    x1 = x1_ref[...]
    h = (x1 * (1.0 + scale2) + shift2).astype(BF16)
    gate = jnp.dot(h, wgb_ref[...], preferred_element_type=F32)
    up = jnp.dot(h, wub_ref[...], preferred_element_type=F32)
    shared = jnp.dot((_silu(gate) * up).astype(BF16), wdb_ref[...], preferred_element_type=F32)
    ffn = _slabs_to_rows(routed_ref, st_ref, h.shape[0]) + shared
    y = _layer_norm(DEEPNORM_ALPHA * x1 + gate2 * ffn, g_ref[...], b_ref[...])

    @pl.when(is_ctx)
    def _():
        yc_ref[...] = y

    @pl.when(jnp.logical_not(is_ctx))
    def _():
        yl_ref[...] = y


def _final(x1, routed, mod3, w_g, w_u, w_d, ln_g, ln_b, mod_map, t_ctx):
    t, d = x1.shape
    t_lat = t - t_ctx
    n_ctx_tiles = t_ctx // TOKEN_TILE
    tm = TOKEN_TILE
    row = lambda i: (i, 0)
    full = lambda i: (0, 0)
    return pl.pallas_call(
        functools.partial(_final_kernel, d=d, n_ctx_tiles=n_ctx_tiles),
        grid=(t // tm,),
        in_specs=[
            pl.BlockSpec((tm, d), row),
            pl.BlockSpec((tm * SUBLANES, LANES), row),
            pl.BlockSpec((1, 1, mod3.shape[2]), mod_map),
            _resident(w_g.shape),
            _resident(w_u.shape),
            _resident(w_d.shape),
            pl.BlockSpec((1, d), full),
            pl.BlockSpec((1, d), full),
        ],
        out_specs=_pair_specs(d, n_ctx_tiles),
        out_shape=[jax.ShapeDtypeStruct((t_ctx, d), F32), jax.ShapeDtypeStruct((t_lat, d), F32)],
        scratch_shapes=[pltpu.VMEM((SUBLANES * TILE_STRIDE, LANES), F32),
                        pltpu.VMEM(w_g.shape, BF16), pltpu.VMEM(w_u.shape, BF16), pltpu.VMEM(w_d.shape, BF16)],
        compiler_params=_params(("arbitrary",)),
    )(x1, routed, mod3, w_g, w_u, w_d, ln_g, ln_b)


def kernel(x_prompt, x_sample, cache_k, cache_v, state_hgrn, c, c_ctx, w_mod, b_mod, w_in, hg_lb, hg_norm_g, q_norm_g, k_norm_g, w_branch_a, w_branch_b, w_out, ln1_g, ln1_b, w_router, router_bias, w_e_gate, w_e_up, w_e_down, w_s_gate, w_s_up, w_s_down, ln2_g, ln2_b):
    assert w_mod.shape[0] == DEPTH
    n_ctx, seq_ctx, d = x_prompt.shape
    n_lat, seq_lat, _ = x_sample.shape
    t_ctx = n_ctx * seq_ctx
    t_lat = n_lat * seq_lat
    assert seq_ctx == TOKEN_TILE and seq_lat % TOKEN_TILE == 0
    layer = 0

    lb = jnp.cumsum(jax.nn.softmax(hg_lb.astype(F32), axis=0), axis=0)[layer]

    cond = jnp.concatenate([c_ctx[None, :], c], axis=0)
    cond = jnp.pad(cond, ((0, (-cond.shape[0]) % SUBLANES), (0, 0)))
    mod = _modulation(cond, w_mod[layer], b_mod[layer][None, :])
    mod3 = mod.reshape(mod.shape[0], 1, mod.shape[1])
    mod_map = _mod_row_map(t_ctx // TOKEN_TILE, seq_lat // TOKEN_TILE)

    x_ctx = x_prompt.reshape(t_ctx, d)
    x_lat = x_sample.reshape(t_lat, d)
    zh, zf, za, zg = _input_projection(x_ctx, x_lat, mod3, w_in[layer], lb, mod_map)

    norm_g = hg_norm_g[layer][None, :]
    oa_ctx, s_ctx = _hgrn_scan(zh, zf, norm_g, None, batch=n_ctx, seq=seq_ctx, row_block0=0,
                               heads=HG_HEADS_PER_STEP_CTX)
    oa_lat, _ = _hgrn_scan(zh, zf, norm_g, state_hgrn[:, layer], batch=n_lat, seq=seq_lat,
                           row_block0=t_ctx // seq_lat, heads=HG_HEADS_PER_STEP_LATENT)

    q_gain = jnp.tile(q_norm_g[layer], N_HEADS)[None, :]
    k_gain = jnp.tile(k_norm_g[layer], N_KV_HEADS)[None, :]
    lane = jnp.arange(ATT_WIDTH)
    group_ones = (lane[:, None] // HEAD_DIM == lane[None, :] // HEAD_DIM).astype(BF16)
    ob_ctx, k_ctx, v_ctx = _attention(za, q_gain, k_gain, group_ones, batch=n_ctx, seq=seq_ctx, row_block0=0)
    past = cache_k.shape[2]
    ob_lat = _attention(
        za, q_gain, k_gain, group_ones, batch=n_lat, seq=seq_lat, row_block0=t_ctx // seq_lat,
        rope=_rope_tables(seq_lat),
        cache=(cache_k[:, layer].reshape(n_lat, past, KV_WIDTH), cache_v[:, layer].reshape(n_lat, past, KV_WIDTH)))

    wr_t = w_router[layer].T
    wr_hi = wr_t.astype(BF16)
    wr_lo = (wr_t - wr_hi.astype(F32)).astype(BF16)
    x1, h2_slabs, idx_t, w_t = _post_mixer(
        x_ctx, x_lat, mod3, oa_ctx, oa_lat, ob_ctx, ob_lat, zg, w_branch_a[layer], w_branch_b[layer], w_out[layer],
        ln1_g[layer][None, :], ln1_b[layer][None, :], wr_hi, wr_lo, router_bias[layer][:, None], mod_map)

    routed = _routed_experts(h2_slabs, *_dispatch_lists(idx_t, w_t), w_e_gate[layer], w_e_up[layer], w_e_down[layer])

    y_ctx, y_lat = _final(x1, routed, mod3, w_s_gate[layer], w_s_up[layer], w_s_down[layer],
                          ln2_g[layer][None, :], ln2_b[layer][None, :], mod_map, t_ctx)

    y_prompt = y_ctx.reshape(n_ctx, seq_ctx, d)
    y_sample = y_lat.reshape(n_lat, seq_lat, d)
    new_cache_k = k_ctx.reshape(n_ctx, 1, seq_ctx, N_KV_HEADS, HEAD_DIM)
    new_cache_v = v_ctx.reshape(n_ctx, 1, seq_ctx, N_KV_HEADS, HEAD_DIM)
    new_state = s_ctx[:, None]
    return (y_prompt, y_sample, new_cache_k, new_cache_v, new_state)
```

```python
import functools

import jax
import jax.numpy as jnp
from jax import lax
from jax.experimental import pallas as pl
from jax.experimental.pallas import tpu as pltpu

F32 = jnp.float32
BF16 = jnp.bfloat16

GRID_W = 64
HG_HEADS = 4
HG_DK = 128
HG_DV = 128
HG_WIDTH = HG_HEADS * HG_DK
N_HEADS = 8
N_KV_HEADS = 2
HEAD_DIM = 64
ATT_WIDTH = N_HEADS * HEAD_DIM
KV_WIDTH = N_KV_HEADS * HEAD_DIM
ROPE_THETA = 10000.0
N_EXPERTS = 256
TOP_K = 8
N_GROUPS = 8
TOPK_GROUPS = 4
GROUP_SIZE = N_EXPERTS // N_GROUPS
ROUTED_SCALE = 2.5
NORM_EPS = 1e-6
DEPTH = 1
DEEPNORM_ALPHA = (2 * DEPTH) ** 0.25

LANES = 128
SUBLANES = 8
VMEM_LIMIT = 56 * 1024 * 1024

CAST_CHUNK = 512
TOKEN_TILE = 256
TILE_STRIDE = TOKEN_TILE + 1
HG_CHUNK = 32
HG_HEADS_PER_STEP_CTX = 4
HG_HEADS_PER_STEP_LATENT = 4
ATT_Q_BLOCK = 128
MOE_BLOCK = 288
BLOCK_COUNT_MULTIPLE = 20
ROW_STRIDE = MOE_BLOCK + 1
LIST_LEN = 512
LIST_ROWS = LIST_LEN // LANES
ROW_GROUP = 8
ROW_SEGMENT = 32
MAX_BLOCKS_PER_STEP = 4
GATHER_BLOCKS_PER_STEP = (10, 4, 2, 1)
WEIGHT_BUFFERS = 3


def _sigmoid(x):
    return 1.0 / (1.0 + jnp.exp(-x))


def _silu(x):
    return x * _sigmoid(x)


def _params(sem=None):
    return pltpu.CompilerParams(dimension_semantics=sem, vmem_limit_bytes=VMEM_LIMIT)


def _resident(shape):
    return pl.BlockSpec(shape, lambda i: (0,) * len(shape), pipeline_mode=pl.Buffered(1))


def _cast_weight_once(src_ref, dst_ref):
    @pl.when(pl.program_id(0) == 0)
    def _():
        width = src_ref.shape[1]
        for lo in range(0, width, CAST_CHUNK):
            hi = min(lo + CAST_CHUNK, width)
            dst_ref[:, lo:hi] = src_ref[:, lo:hi].astype(dst_ref.dtype)


def _mod_kernel(c_ref, w_ref, b_ref, o_ref):
    s = _silu(c_ref[...]).astype(BF16)
    o_ref[...] = jnp.dot(s, w_ref[...].astype(BF16), preferred_element_type=F32) + b_ref[...]


def _modulation(cond, w_mod, b_mod):
    n, d = cond.shape
    width = w_mod.shape[1]
    tn = width // 4
    return pl.pallas_call(
        _mod_kernel,
        grid=(4,),
        in_specs=[
            pl.BlockSpec((n, d), lambda j: (0, 0)),
            pl.BlockSpec((d, tn), lambda j: (0, j)),
            pl.BlockSpec((1, tn), lambda j: (0, j)),
        ],
        out_specs=pl.BlockSpec((n, tn), lambda j: (0, j)),
        out_shape=jax.ShapeDtypeStruct((n, width), F32),
        compiler_params=_params(("arbitrary",)),
    )(cond, w_mod, b_mod)


def _inproj_kernel(xc_ref, xl_ref, mod_ref, w_ref, lb_ref, zh_ref, zf_ref, za_ref, zg_ref, wb_ref,
                   *, d, n_ctx_tiles):
    _cast_weight_once(w_ref, wb_ref)
    shift = mod_ref[0, :, 0:d]
    scale = mod_ref[0, :, d:2 * d]
    x = _pick(pl.program_id(0) < n_ctx_tiles, xc_ref, xl_ref)
    h = (x * (1.0 + scale) + shift).astype(BF16)

    def proj(lo, hi):
        return jnp.dot(h, wb_ref[:, lo:hi], preferred_element_type=F32)

    w = HG_WIDTH
    zh_ref[:, 0:w] = _silu(proj(0, w)).astype(zh_ref.dtype)
    zh_ref[:, w:2 * w] = proj(w, 2 * w).astype(zh_ref.dtype)
    for i in range(2):
        lb = lb_ref[i:i + 1, :]
        zf_ref[:, i * w:(i + 1) * w] = lb + (1.0 - lb) * _sigmoid(proj((2 + i) * w, (3 + i) * w))
    zh_ref[:, 2 * w:3 * w] = _silu(proj(4 * w, 5 * w)).astype(zh_ref.dtype)
    a0 = 5 * w
    a1 = a0 + ATT_WIDTH + 2 * KV_WIDTH
    za_ref[...] = proj(a0, a1)
    for i in range(4):
        lo = a1 + i * (d // 2)
        zg_ref[:, i * (d // 2):(i + 1) * (d // 2)] = _sigmoid(proj(lo, lo + d // 2)).astype(zg_ref.dtype)


def _mod_row_map(n_ctx_tiles, tiles_per_latent):
    def index_map(i):
        row = jnp.where(i < n_ctx_tiles, 0, 1 + (i - n_ctx_tiles) // tiles_per_latent)
        return (row, 0, 0)
    return index_map


def _pair_specs(width, n_ctx_tiles):
    return [pl.BlockSpec((TOKEN_TILE, width), lambda i: (jnp.minimum(i, n_ctx_tiles - 1), 0)),
            pl.BlockSpec((TOKEN_TILE, width), lambda i: (jnp.maximum(i - n_ctx_tiles, 0), 0))]


def _pick(is_ctx, ctx_ref, lat_ref):
    return jnp.where(is_ctx, ctx_ref[...], lat_ref[...])


def _input_projection(x_ctx, x_lat, mod3, w_in, lb, mod_map):
    d = x_ctx.shape[1]
    n_ctx_tiles = x_ctx.shape[0] // TOKEN_TILE
    t = x_ctx.shape[0] + x_lat.shape[0]
    width = w_in.shape[1]
    zh_w = 3 * HG_WIDTH
    zf_w = 2 * HG_WIDTH
    za_w = ATT_WIDTH + 2 * KV_WIDTH
    zg_w = 2 * d
    assert width == zh_w + zf_w + za_w + zg_w
    tm = TOKEN_TILE
    return pl.pallas_call(
        functools.partial(_inproj_kernel, d=d, n_ctx_tiles=n_ctx_tiles),
        grid=(t // tm,),
        in_specs=_pair_specs(d, n_ctx_tiles) + [
            pl.BlockSpec((1, 1, mod3.shape[2]), mod_map),
            _resident((d, width)),
            pl.BlockSpec((2, HG_WIDTH), lambda i: (0, 0)),
        ],
        out_specs=[
            pl.BlockSpec((tm, zh_w), lambda i: (i, 0)),
            pl.BlockSpec((tm, zf_w), lambda i: (i, 0)),
            pl.BlockSpec((tm, za_w), lambda i: (i, 0)),
            pl.BlockSpec((tm, zg_w), lambda i: (i, 0)),
        ],
        out_shape=[
            jax.ShapeDtypeStruct((t, zh_w), BF16),
            jax.ShapeDtypeStruct((t, zf_w), F32),
            jax.ShapeDtypeStruct((t, za_w), F32),
            jax.ShapeDtypeStruct((t, zg_w), BF16),
        ],
        scratch_shapes=[pltpu.VMEM((d, width), BF16)],
        compiler_params=_params(("arbitrary",)),
    )(x_ctx, x_lat, mod3, w_in, lb)


def _hgrn_kernel(*refs, seq, has_state, heads):
    if has_state:
        q_ref, v_ref, ff_ref, fb_ref, gs_ref, ng_ref, s0_ref, o_ref, sout_ref = refs
    else:
        q_ref, v_ref, ff_ref, fb_ref, gs_ref, ng_ref, o_ref, sout_ref = refs
        s0_ref = None
    for hd in range(heads):
        _hgrn_head(q_ref, v_ref, ff_ref, fb_ref, gs_ref, ng_ref, s0_ref, o_ref, sout_ref, hd, seq)


def _hgrn_head(q_ref, v_ref, ff_ref, fb_ref, gs_ref, ng_ref, s0_ref, o_ref, sout_ref, hd, seq):
    has_state = s0_ref is not None
    cols = slice(hd * HG_DK, (hd + 1) * HG_DK)
    c = HG_CHUNK
    n = seq // c
    q3 = q_ref[:, cols].astype(F32).reshape(n, c, HG_DK)
    v3 = v_ref[:, cols].reshape(n, c, HG_DV)
    pos = lax.broadcasted_iota(jnp.int32, (seq, HG_DK), 0) % c
    t_idx = lax.broadcasted_iota(jnp.int32, (c, c), 0)
    s_idx = lax.broadcasted_iota(jnp.int32, (c, c), 1)
    o_sum = None
    for direction, f_ref in enumerate((ff_ref, fb_ref)):
        reverse = direction == 1
        f = f_ref[:, cols]
        k3 = (1.0 - f).reshape(n, c, HG_DK)
        b = jnp.log(f)
        step = 1
        while step < c:
            if reverse:
                b = b + jnp.where(pos < c - step, pltpu.roll(b, seq - step, axis=0), 0.0)
            else:
                b = b + jnp.where(pos >= step, pltpu.roll(b, step, axis=0), 0.0)
            step *= 2
        b3 = b.reshape(n, c, HG_DK)
        edge = b3[:, 0:1, :] if reverse else b3[:, c - 1:c, :]
        mid = b3[:, c // 2:c // 2 + 1, :]
        q_rel = q3 * jnp.exp(b3 - mid)
        k_rel = k3 * jnp.exp(mid - b3)
        q_mid = q_rel.astype(BF16)
        k_mid = k_rel.astype(BF16)
        q_in = (q_rel * jnp.exp(mid)).astype(BF16)
        k_end = (k_rel * jnp.exp(edge - mid)).astype(BF16)
        scores = jnp.einsum('ntd,nsd->nts', q_mid, k_mid, preferred_element_type=F32)
        keep = (s_idx >= t_idx) if reverse else (s_idx <= t_idx)
        scores = jnp.where(keep[None], scores, 0.0).astype(BF16)
        o_intra = jnp.einsum('nts,nsv->ntv', scores, v3, preferred_element_type=F32)
        upd = jnp.einsum('nsv,nsd->nvd', v3, k_end, preferred_element_type=F32)
        dec = jnp.exp(edge)
        if has_state:
            st = s0_ref[0, direction, hd].T
        else:
            st = jnp.zeros((HG_DV, HG_DK), F32)
        before = [None] * n
        for ci in (range(n - 1, -1, -1) if reverse else range(n)):
            before[ci] = st.astype(BF16)
            st = st * dec[ci] + upd[ci]
        st_before = jnp.stack(before, axis=0)
        o_inter = jnp.einsum('ntd,nvd->ntv', q_in, st_before, preferred_element_type=F32)
        o_dir = (o_intra + o_inter).reshape(seq, HG_DV)
        o_sum = o_dir if o_sum is None else o_sum + o_dir
        sout_ref[0, direction, hd] = st.T
    ms = jnp.mean(o_sum * o_sum, axis=-1, keepdims=True)
    o = o_sum * lax.rsqrt(ms + NORM_EPS) * ng_ref[...]
    o_ref[:, cols] = (o * gs_ref[:, cols].astype(F32)).astype(o_ref.dtype)


def _hgrn_scan(zh, zf, norm_g, s0, *, batch, seq, row_block0, heads):
    has_state = s0 is not None
    h = HG_HEADS
    groups = h // heads
    width = heads * HG_DK

    def col(section):
        return pl.BlockSpec((seq, width), lambda b, j: (row_block0 + b, section * groups + j))

    in_specs = [col(0), col(1), col(0), col(1), col(2), pl.BlockSpec((1, HG_DV), lambda b, j: (0, 0))]
    args = [zh, zh, zf, zf, zh, norm_g]
    state_spec = pl.BlockSpec((1, 2, heads, HG_DK, HG_DV), lambda b, j: (b, 0, j, 0, 0))
    if has_state:
        in_specs.append(state_spec)
        args.append(s0)
    return pl.pallas_call(
        functools.partial(_hgrn_kernel, seq=seq, has_state=has_state, heads=heads),
        grid=(batch, groups),
        in_specs=in_specs,
        out_specs=[pl.BlockSpec((seq, width), lambda b, j: (b, j)), state_spec],
        out_shape=[
            jax.ShapeDtypeStruct((batch * seq, h * HG_DV), BF16),
            jax.ShapeDtypeStruct((batch, 2, h, HG_DK, HG_DV), F32),
        ],
        compiler_params=_params(("arbitrary", "arbitrary")),
    )(*args)


def _group_rms_norm(x, group_ones, gain):
    sq = x * x
    hi = sq.astype(BF16)
    lo = (sq - hi.astype(F32)).astype(BF16)
    total = (jnp.dot(hi, group_ones, preferred_element_type=F32)
             + jnp.dot(lo, group_ones, preferred_element_type=F32))
    return x * lax.rsqrt(total * (1.0 / HEAD_DIM) + NORM_EPS) * gain


def _rope(x, cos, sin_signed):
    width = x.shape[1]
    quarter = HEAD_DIM // 4
    lane = lax.broadcasted_iota(jnp.int32, x.shape, 1)
    partner = jnp.where(lane % (2 * quarter) < quarter,
                        pltpu.roll(x, width - quarter, axis=1),
                        pltpu.roll(x, quarter, axis=1))
    return x * cos + partner * sin_signed


def _attn_kernel(*refs, seq, latent):
    if latent:
        za_ref, qg_ref, kg_ref, gm_ref, cos_ref, sin_ref, ck_ref, cv_ref, o_ref = refs
    else:
        za_ref, qg_ref, kg_ref, gm_ref, o_ref, kout_ref, vout_ref = refs
    q = za_ref[:, 0:ATT_WIDTH]
    k = za_ref[:, ATT_WIDTH:ATT_WIDTH + KV_WIDTH]
    v = za_ref[:, ATT_WIDTH + KV_WIDTH:ATT_WIDTH + 2 * KV_WIDTH]
    qn = _group_rms_norm(q, gm_ref[...], qg_ref[...])
    kn = _group_rms_norm(k, gm_ref[0:KV_WIDTH, 0:KV_WIDTH], kg_ref[...])
    if latent:
        qn = _rope(qn, cos_ref[...], sin_ref[...])
        kr = _rope(kn, cos_ref[:, 0:KV_WIDTH], sin_ref[:, 0:KV_WIDTH])
    else:
        kout_ref[...] = kn
        vout_ref[...] = v
        kr = kn
    qb = (qn * (HEAD_DIM ** -0.5)).astype(BF16)
    kb = kr.astype(BF16)
    vb = v.astype(BF16)
    if latent:
        ckb = ck_ref[0].astype(BF16)
        cvb = cv_ref[0].astype(BF16)
    groups = N_HEADS // N_KV_HEADS
    tq = ATT_Q_BLOCK if latent else seq
    nt = (((1,), (1,)), ((), ()))
    for kh in range(N_KV_HEADS):
        ksl = slice(kh * HEAD_DIM, (kh + 1) * HEAD_DIM)
        k_new = kb[:, ksl]
        v_new = vb[:, ksl]
        for blk in range(seq // tq):
            rows = slice(blk * tq, (blk + 1) * tq)
            heads = [kh * groups + g for g in range(groups)]
            q_st = jnp.concatenate([qb[rows, hd * HEAD_DIM:(hd + 1) * HEAD_DIM] for hd in heads], axis=0)
            s_new = lax.dot_general(q_st, k_new, nt, preferred_element_type=F32)
            m = jnp.max(s_new, axis=-1, keepdims=True)
            if latent:
                s_ctx = lax.dot_general(q_st, ckb[:, ksl], nt, preferred_element_type=F32)
                m = jnp.maximum(m, jnp.max(s_ctx, axis=-1, keepdims=True))
            p_new = jnp.exp(s_new - m)
            denom = jnp.sum(p_new, axis=-1, keepdims=True)
            acc = jnp.dot(p_new.astype(BF16), v_new, preferred_element_type=F32)
            if latent:
                p_ctx = jnp.exp(s_ctx - m)
                denom = denom + jnp.sum(p_ctx, axis=-1, keepdims=True)
                acc = acc + jnp.dot(p_ctx.astype(BF16), cvb[:, ksl], preferred_element_type=F32)
            out = acc / denom
            for g in range(0, groups, 2):
                pair = jnp.concatenate([out[g * tq:(g + 1) * tq], out[(g + 1) * tq:(g + 2) * tq]], axis=1)
                lane0 = heads[g] * HEAD_DIM
                o_ref[rows, lane0:lane0 + 2 * HEAD_DIM] = pair.astype(o_ref.dtype)


def _attention(za, q_gain, k_gain, group_ones, *, batch, seq, row_block0, rope=None, cache=None):
    latent = cache is not None
    za_w = za.shape[1]
    in_specs = [
        pl.BlockSpec((seq, za_w), lambda b: (row_block0 + b, 0)),
        pl.BlockSpec((1, ATT_WIDTH), lambda b: (0, 0)),
        pl.BlockSpec((1, KV_WIDTH), lambda b: (0, 0)),
        pl.BlockSpec((ATT_WIDTH, ATT_WIDTH), lambda b: (0, 0)),
    ]
    args = [za, q_gain, k_gain, group_ones]
    o_spec = pl.BlockSpec((seq, ATT_WIDTH), lambda b: (b, 0))
    o_shape = jax.ShapeDtypeStruct((batch * seq, ATT_WIDTH), BF16)
    if latent:
        cos, sin_signed = rope
        ck, cv = cache
        past = ck.shape[1]
        in_specs += [
            pl.BlockSpec((seq, ATT_WIDTH), lambda b: (0, 0)),
            pl.BlockSpec((seq, ATT_WIDTH), lambda b: (0, 0)),
            pl.BlockSpec((1, past, KV_WIDTH), lambda b: (b, 0, 0)),
            pl.BlockSpec((1, past, KV_WIDTH), lambda b: (b, 0, 0)),
        ]
        args += [cos, sin_signed, ck, cv]
        out_specs = o_spec
        out_shape = o_shape
    else:
        kv_spec = pl.BlockSpec((seq, KV_WIDTH), lambda b: (b, 0))
        kv_shape = jax.ShapeDtypeStruct((batch * seq, KV_WIDTH), F32)
        out_specs = [o_spec, kv_spec, kv_spec]
        out_shape = [o_shape, kv_shape, kv_shape]
    return pl.pallas_call(
        functools.partial(_attn_kernel, seq=seq, latent=latent),
        grid=(batch,),
        in_specs=in_specs,
        out_specs=out_specs,
        out_shape=out_shape,
        compiler_params=_params(("arbitrary",)),
    )(*args)


def _rope_tables(seq):
    quarter = HEAD_DIM // 4
    t = jnp.arange(seq)
    row = (t // GRID_W).astype(F32)
    colp = (t % GRID_W).astype(F32)
    inv_freq = ROPE_THETA ** (-jnp.arange(quarter, dtype=F32) / quarter)
    lane = jnp.arange(HEAD_DIM)
    pos = jnp.where((lane < HEAD_DIM // 2)[None, :], row[:, None], colp[:, None])
    ang = pos * inv_freq[lane % quarter][None, :]
    sign = jnp.where(lane % (2 * quarter) < quarter, -1.0, 1.0)[None, :]
    cos = jnp.tile(jnp.cos(ang), (1, N_HEADS))
    sin_signed = jnp.tile(jnp.sin(ang) * sign, (1, N_HEADS))
    return cos, sin_signed


def _layer_norm(y, g, b):
    mu = jnp.mean(y, axis=-1, keepdims=True)
    yc = y - mu
    var = jnp.mean(yc * yc, axis=-1, keepdims=True)
    return yc * lax.rsqrt(var + NORM_EPS) * g + b


def _rows_to_slabs(tile, st_ref, slab_ref):
    rows, d = tile.shape
    chunks = d // LANES
    for j in range(chunks):
        st_ref[j * TILE_STRIDE:j * TILE_STRIDE + rows, :] = tile[:, j * LANES:(j + 1) * LANES]
    for r in range(rows):
        slab_ref[r * chunks:(r + 1) * chunks, :] = st_ref[pl.ds(r, chunks, stride=TILE_STRIDE), :]


def _slabs_to_rows(slab_ref, st_ref, rows):
    chunks = SUBLANES
    for r in range(rows):
        st_ref[pl.ds(r, chunks, stride=TILE_STRIDE), :] = slab_ref[r * chunks:(r + 1) * chunks, :]
    return jnp.concatenate([st_ref[j * TILE_STRIDE:j * TILE_STRIDE + rows, :] for j in range(chunks)], axis=1)


def _post_mixer_kernel(xc_ref, xl_ref, mod_ref, oac_ref, oal_ref, obc_ref, obl_ref, zg_ref, wa_ref, wb_ref,
                       wo_ref, g_ref, b_ref, wrh_ref, wrl_ref, rb_ref,
                       x1_ref, slab_ref, idx_ref, rw_ref, st_ref, wab_ref, wbb_ref, wob_ref,
                       *, d, n_ctx_tiles):
    for src, dst in ((wa_ref, wab_ref), (wb_ref, wbb_ref), (wo_ref, wob_ref)):
        _cast_weight_once(src, dst)
    is_ctx = pl.program_id(0) < n_ctx_tiles
    gate1 = mod_ref[0, :, 2 * d:3 * d]
    shift2 = mod_ref[0, :, 3 * d:4 * d]
    scale2 = mod_ref[0, :, 4 * d:5 * d]
    branch_a = jnp.dot(_pick(is_ctx, oac_ref, oal_ref), wab_ref[...], preferred_element_type=F32)
    branch_b = jnp.dot(_pick(is_ctx, obc_ref, obl_ref), wbb_ref[...], preferred_element_type=F32)
    merged = zg_ref[:, 0:d].astype(F32) * branch_a + zg_ref[:, d:2 * d].astype(F32) * branch_b
    mix = jnp.dot(merged.astype(BF16), wob_ref[...], preferred_element_type=F32)
    x = _pick(is_ctx, xc_ref, xl_ref)
    x1 = _layer_norm(DEEPNORM_ALPHA * x + gate1 * mix, g_ref[...], b_ref[...])
    x1_ref[...] = x1
    h2 = x1 * (1.0 + scale2) + shift2
    _rows_to_slabs(h2, st_ref, slab_ref)
    idx_ref[...], rw_ref[...] = _route_tile(h2, wrh_ref[...], wrl_ref[...], rb_ref[...])


def _post_mixer(x_ctx, x_lat, mod3, oa_ctx, oa_lat, ob_ctx, ob_lat, zg, w_a, w_b, w_o, ln_g, ln_b,
                wr_hi, wr_lo, router_bias, mod_map):
    d = x_ctx.shape[1]
    n_ctx_tiles = x_ctx.shape[0] // TOKEN_TILE
    t = x_ctx.shape[0] + x_lat.shape[0]
    tm = TOKEN_TILE
    row = lambda i: (i, 0)
    full = lambda i: (0, 0)
    return pl.pallas_call(
        functools.partial(_post_mixer_kernel, d=d, n_ctx_tiles=n_ctx_tiles),
        grid=(t // tm,),
        in_specs=_pair_specs(d, n_ctx_tiles) + [
            pl.BlockSpec((1, 1, mod3.shape[2]), mod_map),
        ] + _pair_specs(oa_ctx.shape[1], n_ctx_tiles) + _pair_specs(ob_ctx.shape[1], n_ctx_tiles) + [
            pl.BlockSpec((tm, 2 * d), row),
            _resident(w_a.shape),
            _resident(w_b.shape),
            _resident(w_o.shape),
            pl.BlockSpec((1, d), full),
            pl.BlockSpec((1, d), full),
            pl.BlockSpec((N_EXPERTS, d), full),
            pl.BlockSpec((N_EXPERTS, d), full),
            pl.BlockSpec((N_EXPERTS, 1), full),
        ],
        out_specs=[pl.BlockSpec((tm, d), row),
                   pl.BlockSpec((tm * SUBLANES, LANES), row),
                   pl.BlockSpec((TOP_K, tm), lambda i: (0, i)), pl.BlockSpec((TOP_K, tm), lambda i: (0, i))],
        out_shape=[jax.ShapeDtypeStruct((t, d), F32),
                   jax.ShapeDtypeStruct((t * SUBLANES, LANES), F32),
                   jax.ShapeDtypeStruct((TOP_K, t), jnp.int32), jax.ShapeDtypeStruct((TOP_K, t), F32)],
        scratch_shapes=[pltpu.VMEM((SUBLANES * TILE_STRIDE, LANES), F32),
                        pltpu.VMEM(w_a.shape, BF16), pltpu.VMEM(w_b.shape, BF16), pltpu.VMEM(w_o.shape, BF16)],
        compiler_params=_params(("arbitrary",)),
    )(x_ctx, x_lat, mod3, oa_ctx, oa_lat, ob_ctx, ob_lat, zg, w_a, w_b, w_o, ln_g, ln_b, wr_hi, wr_lo, router_bias)


def _first_index_of_max(x, idx, sentinel):
    m = jnp.max(x, axis=0, keepdims=True)
    first = jnp.min(jnp.where(x == m, idx, sentinel), axis=0, keepdims=True)
    return m, first


def _route_tile(h, wh, wl, bias):
    hh = h.astype(BF16)
    hl = (h - hh.astype(F32)).astype(BF16)
    nt = (((1,), (1,)), ((), ()))
    logits = (lax.dot_general(wh, hh, nt, preferred_element_type=F32)
              + lax.dot_general(wh, hl, nt, preferred_element_type=F32)
              + lax.dot_general(wl, hh, nt, preferred_element_type=F32))
    scores = _sigmoid(logits)
    sel = scores + bias
    tm = sel.shape[1]
    neg = -jnp.inf
    gidx = lax.broadcasted_iota(jnp.int32, (GROUP_SIZE, tm), 0)
    group_scores = []
    for g in range(N_GROUPS):
        sg = sel[g * GROUP_SIZE:(g + 1) * GROUP_SIZE, :]
        m1, first = _first_index_of_max(sg, gidx, GROUP_SIZE)
        m2 = jnp.max(jnp.where(gidx == first, neg, sg), axis=0, keepdims=True)
        group_scores.append(m1 + m2)
    gs = jnp.concatenate(group_scores, axis=0)
    nidx = lax.broadcasted_iota(jnp.int32, (N_GROUPS, tm), 0)
    chosen = jnp.zeros((N_GROUPS, tm), jnp.bool_)
    for _ in range(TOPK_GROUPS):
        _, first = _first_index_of_max(gs, nidx, N_GROUPS)
        hit = nidx == first
        chosen = jnp.logical_or(chosen, hit)
        gs = jnp.where(hit, neg, gs)
    masked = jnp.concatenate(
        [jnp.where(chosen[g:g + 1, :], sel[g * GROUP_SIZE:(g + 1) * GROUP_SIZE, :], neg) for g in range(N_GROUPS)],
        axis=0)
    eidx = lax.broadcasted_iota(jnp.int32, (N_EXPERTS, tm), 0)
    picks, weights = [], []
    for _ in range(TOP_K):
        _, first = _first_index_of_max(masked, eidx, N_EXPERTS)
        hit = eidx == first
        picks.append(first)
        weights.append(jnp.sum(jnp.where(hit, scores, 0.0), axis=0, keepdims=True))
        masked = jnp.where(hit, neg, masked)
    wk = jnp.concatenate(weights, axis=0)
    return jnp.concatenate(picks, axis=0), wk / jnp.sum(wk, axis=0, keepdims=True) * ROUTED_SCALE


def _dispatch_lists(idx_t, w_t):
    k, t = idx_t.shape
    s = k * t
    assert N_EXPERTS * t < 2 ** 31
    n_blocks = -(-(s // MOE_BLOCK + N_EXPERTS) // BLOCK_COUNT_MULTIPLE) * BLOCK_COUNT_MULTIPLE
    flat_tok = jnp.arange(s, dtype=jnp.int32) % t
    sorted_key, sorted_w = lax.sort((idx_t.reshape(s) * t + flat_tok, w_t.reshape(s)), num_keys=1)
    sorted_tok = sorted_key % t
    experts = jnp.arange(N_EXPERTS + 1, dtype=jnp.int32)
    start = jnp.sum(sorted_key[None, :] < experts[:, None] * t, axis=1, dtype=jnp.int32)
    counts = start[1:] - start[:-1]
    blocks_per_e = (counts + MOE_BLOCK - 1) // MOE_BLOCK
    block_end = jnp.cumsum(blocks_per_e)
    n_used = block_end[-1]
    blk = jnp.arange(n_blocks, dtype=jnp.int32)
    block_e = jnp.sum(block_end[None, :] <= blk[:, None], axis=1, dtype=jnp.int32)
    block_e = jnp.minimum(block_e, N_EXPERTS - 1)
    onehot = (block_e[:, None] == experts[None, :N_EXPERTS]).astype(jnp.int32)
    pick = lambda a: jnp.sum(onehot * a[None, :], axis=1)
    offset = (blk - (pick(block_end) - pick(blocks_per_e))) * MOE_BLOCK
    row_start = pick(start[:-1]) + offset
    n_rows = jnp.clip(pick(counts) - offset, 0, MOE_BLOCK)
    used = blk < n_used
    last_e = jnp.sum(jnp.where(blk == n_used - 1, block_e, 0))
    block_e = jnp.where(used, block_e, last_e)
    row_start = jnp.where(used, row_start, 0)
    n_rows = jnp.where(used, n_rows, 0)
    return block_e, row_start, n_rows, (n_used - 1).reshape(1), sorted_tok, sorted_w


def _block_lists_kernel(rs_ref, nr_ref, tok_ref, w_ref, off_ref, wout_ref, *, t):
    lane = lax.broadcasted_iota(jnp.int32, (LIST_ROWS, LANES), 1)
    row = lax.broadcasted_iota(jnp.int32, (LIST_ROWS, LANES), 0)

    def one_block(b, carry):
        start = rs_ref[b]
        n_rows = nr_ref[b]
        q = start // LANES
        lane0 = start % LANES
        q8 = pl.multiple_of((q // SUBLANES) * SUBLANES, SUBLANES)

        def window(ref):
            a = ref[pl.ds(q8, 2 * SUBLANES), :]
            a = pltpu.roll(a, 2 * SUBLANES - (q - q8), axis=0)
            lo = pltpu.roll(a[0:LIST_ROWS], LANES - lane0, axis=1)
            hi = pltpu.roll(a[1:LIST_ROWS + 1], LANES - lane0, axis=1)
            return jnp.where(lane < LANES - lane0, lo, hi)

        valid = row * LANES + lane < n_rows
        off_ref[b] = jnp.where(valid, window(tok_ref) * SUBLANES, t * SUBLANES)
        wout_ref[b] = jnp.where(valid, window(w_ref), 0.0)
        return carry

    n_blocks = off_ref.shape[0]
    unroll = next(u for u in (4, 2, 1) if n_blocks % u == 0)

    def several_blocks(i, carry):
        for u in range(unroll):
            one_block(i * unroll + u, carry)
        return carry

    lax.fori_loop(0, n_blocks // unroll, several_blocks, 0)


def _block_lists(row_start, n_rows, sorted_tok, sorted_w, t):
    n_blocks = row_start.shape[0]
    n_slots = sorted_tok.shape[0]
    list_rows = n_slots // LANES + 2 * SUBLANES
    pad = list_rows * LANES - n_slots
    tok2d = jnp.pad(sorted_tok, (0, pad)).reshape(list_rows, LANES)
    w2d = jnp.pad(sorted_w, (0, pad)).reshape(list_rows, LANES)
    full = lambda i, rs, nr: (0, 0)
    blk = pl.BlockSpec((n_blocks, LIST_ROWS, LANES), lambda i, rs, nr: (0, 0, 0))
    off, wts = pl.pallas_call(
        functools.partial(_block_lists_kernel, t=t),
        grid_spec=pltpu.PrefetchScalarGridSpec(
            num_scalar_prefetch=2,
            grid=(1,),
            in_specs=[pl.BlockSpec((list_rows, LANES), full), pl.BlockSpec((list_rows, LANES), full)],
            out_specs=[blk, blk],
        ),
        out_shape=[jax.ShapeDtypeStruct((n_blocks, LIST_ROWS, LANES), jnp.int32),
                   jax.ShapeDtypeStruct((n_blocks, LIST_ROWS, LANES), F32)],
        compiler_params=_params(("arbitrary",)),
    )(row_start, n_rows, tok2d, w2d)
    return off.reshape(n_blocks * LIST_LEN), wts.reshape(n_blocks * LIST_LEN)


def _gather_kernel(nr_ref, last_ref, x_hbm, off_ref, out_ref, xs, xt, sem):
    step = pl.program_id(0)
    per_step = out_ref.shape[0] // MOE_BLOCK
    chunks = out_ref.shape[1] // LANES

    @pl.when(step == 0)
    def _load():
        rows = x_hbm.shape[0]
        cp = pltpu.make_async_copy(x_hbm, xs.at[pl.ds(0, rows)], sem)
        cp.start()
        cp.wait()
        xs[pl.ds(rows, SUBLANES), :] = jnp.zeros((SUBLANES, LANES), F32)
        xt[...] = jnp.zeros(xt.shape, F32)

    def one_block(g):
        n_rows = nr_ref[step * per_step + g]

        @pl.when(n_rows > 0)
        def _block():
            for seg in range(MOE_BLOCK // ROW_SEGMENT):
                @pl.when(n_rows > seg * ROW_SEGMENT)
                def _gather(seg=seg):
                    for r in range(seg * ROW_SEGMENT, (seg + 1) * ROW_SEGMENT):
                        src = pl.multiple_of(off_ref[g * LIST_LEN + r], SUBLANES)
                        xt[pl.ds(r, chunks, stride=ROW_STRIDE), :] = xs[pl.ds(src, SUBLANES), :]

            x = jnp.concatenate(
                [xt[j * ROW_STRIDE:j * ROW_STRIDE + MOE_BLOCK, :] for j in range(chunks)], axis=1)
            out_ref[g * MOE_BLOCK:(g + 1) * MOE_BLOCK, :] = x.astype(out_ref.dtype)

    for g in range(per_step):
        one_block(g)


def _expert_kernel(be_ref, nr_ref, last_ref, x_ref, wg_hbm, wu_hbm, wd_hbm, off_ref, w_ref, y_hbm,
                   ys, ot, wgb, wub, wdb, sem, wsem, *, t, n_steps):
    step = pl.program_id(0)
    per_step = x_ref.shape[0] // MOE_BLOCK
    rows = t * SUBLANES
    chunks = x_ref.shape[1] // LANES
    group = ROW_GROUP
    last = last_ref[0]
    ahead = WEIGHT_BUFFERS - 1

    def weight_copies(blk):
        e = be_ref[blk]
        slot = blk % WEIGHT_BUFFERS
        return [pltpu.make_async_copy(src.at[e], dst.at[slot], wsem.at[slot])
                for src, dst in ((wg_hbm, wgb), (wu_hbm, wub), (wd_hbm, wdb))]

    @pl.when(step == 0)
    def _init():
        for blk in range(ahead):
            @pl.when(blk <= last)
            def _(blk=blk):
                for cp in weight_copies(blk):
                    cp.start()
        ys[...] = jnp.zeros(ys.shape, F32)

    def one_block(g):
        b = step * per_step + g
        n_rows = nr_ref[b]
        row0 = g * MOE_BLOCK
        slot0 = g * LIST_LEN

        @pl.when(b + ahead <= last)
        def _prefetch():
            for cp in weight_copies(b + ahead):
                cp.start()

        @pl.when(b <= last)
        def _block():
            slot = b % WEIGHT_BUFFERS
            for cp in weight_copies(b):
                cp.wait()
            x = x_ref[row0:row0 + MOE_BLOCK, :]
            gate = jnp.dot(x, wgb[slot].astype(BF16), preferred_element_type=F32)
            up = jnp.dot(x, wub[slot].astype(BF16), preferred_element_type=F32)
            hidden = (_silu(gate) * up).astype(BF16)
            out = jnp.dot(hidden, wdb[slot].astype(BF16), preferred_element_type=F32)
            for j in range(chunks):
                ot[j * ROW_STRIDE:j * ROW_STRIDE + MOE_BLOCK, :] = out[:, j * LANES:(j + 1) * LANES]
            for seg in range(MOE_BLOCK // ROW_SEGMENT):
                @pl.when(n_rows > seg * ROW_SEGMENT)
                def _scatter(seg=seg):
                    for r0 in range(seg * ROW_SEGMENT, (seg + 1) * ROW_SEGMENT, group):
                        dst = [pl.multiple_of(off_ref[slot0 + r0 + i], SUBLANES) for i in range(group)]
                        vals = [ys[pl.ds(dst[i], SUBLANES), :]
                                + w_ref[slot0 + r0 + i] * ot[pl.ds(r0 + i, chunks, stride=ROW_STRIDE), :]
                                for i in range(group)]
                        for i in range(group):
                            ys[pl.ds(dst[i], SUBLANES), :] = vals[i]

    for g in range(per_step):
        one_block(g)

    @pl.when(step == n_steps - 1)
    def _store():
        cp = pltpu.make_async_copy(ys.at[pl.ds(0, rows)], y_hbm, sem)
        cp.start()
        cp.wait()


def _routed_experts(h2_rows, block_e, row_start, n_rows, last_block, sorted_tok, sorted_w, w_gate, w_up, w_down):
    rows, lanes = h2_rows.shape
    t = rows // SUBLANES
    n_blocks = block_e.shape[0]
    d, e_dim = w_gate.shape[1], w_gate.shape[2]
    assert d == SUBLANES * LANES and lanes == LANES
    offsets, weights = _block_lists(row_start, n_rows, sorted_tok, sorted_w, t)
    staging = pltpu.VMEM((SUBLANES * ROW_STRIDE, LANES), F32)
    resident = pltpu.VMEM((rows + SUBLANES, LANES), F32)

    def step_map(per_step):
        return lambda s, *prefetch: (jnp.minimum(s, prefetch[-1][0] // per_step), 0)

    def list_spec(per_step):
        index = step_map(per_step)
        return pl.BlockSpec((per_step * LIST_LEN,), lambda s, *prefetch: index(s, *prefetch)[:1],
                            memory_space=pltpu.SMEM)

    per_step = next(p for p in GATHER_BLOCKS_PER_STEP if n_blocks % p == 0)
    gathered = pl.pallas_call(
        _gather_kernel,
        grid_spec=pltpu.PrefetchScalarGridSpec(
            num_scalar_prefetch=2,
            grid=(n_blocks // per_step,),
            in_specs=[pl.BlockSpec(memory_space=pl.ANY), list_spec(per_step)],
            out_specs=pl.BlockSpec((per_step * MOE_BLOCK, d), step_map(per_step)),
            scratch_shapes=[resident, staging, pltpu.SemaphoreType.DMA(())],
        ),
        out_shape=jax.ShapeDtypeStruct((n_blocks * MOE_BLOCK, d), BF16),
        compiler_params=_params(("arbitrary",)),
    )(n_rows, last_block, h2_rows, offsets)

    per_step = next(p for p in range(MAX_BLOCKS_PER_STEP, 0, -1) if n_blocks % p == 0)
    n_steps = n_blocks // per_step
    hbm = pl.BlockSpec(memory_space=pl.ANY)
    return pl.pallas_call(
        functools.partial(_expert_kernel, t=t, n_steps=n_steps),
        grid_spec=pltpu.PrefetchScalarGridSpec(
            num_scalar_prefetch=3,
            grid=(n_steps,),
            in_specs=[pl.BlockSpec((per_step * MOE_BLOCK, d), step_map(per_step)), hbm, hbm, hbm,
                      list_spec(per_step), list_spec(per_step)],
            out_specs=hbm,
            scratch_shapes=[
                resident,
                staging,
                pltpu.VMEM((WEIGHT_BUFFERS, d, e_dim), F32),
                pltpu.VMEM((WEIGHT_BUFFERS, d, e_dim), F32),
                pltpu.VMEM((WEIGHT_BUFFERS, e_dim, d), F32),
                pltpu.SemaphoreType.DMA(()),
                pltpu.SemaphoreType.DMA((WEIGHT_BUFFERS,)),
            ],
        ),
        out_shape=jax.ShapeDtypeStruct((rows, LANES), F32),
        compiler_params=_params(("arbitrary",)),
    )(block_e, n_rows, last_block, gathered, w_gate, w_up, w_down, offsets, weights)


def _final_kernel(x1_ref, routed_ref, mod_ref, wg_ref, wu_ref, wd_ref, g_ref, b_ref,
                  yc_ref, yl_ref, st_ref, wgb_ref, wub_ref, wdb_ref, *, d, n_ctx_tiles):
    for src, dst in ((wg_ref, wgb_ref), (wu_ref, wub_ref), (wd_ref, wdb_ref)):
        _cast_weight_once(src, dst)
    is_ctx = pl.program_id(0) < n_ctx_tiles
    shift2 = mod_ref[0, :, 3 * d:4 * d]
    scale2 = mod_ref[0, :, 4 * d:5 * d]
    gate2 = mod_ref[0, :, 5 * d:6 * d]
    x1 = x1_ref[...]
    h = (x1 * (1.0 + scale2) + shift2).astype(BF16)
    gate = jnp.dot(h, wgb_ref[...], preferred_element_type=F32)
    up = jnp.dot(h, wub_ref[...], preferred_element_type=F32)
    shared = jnp.dot((_silu(gate) * up).astype(BF16), wdb_ref[...], preferred_element_type=F32)
    ffn = _slabs_to_rows(routed_ref, st_ref, h.shape[0]) + shared
    y = _layer_norm(DEEPNORM_ALPHA * x1 + gate2 * ffn, g_ref[...], b_ref[...])

    @pl.when(is_ctx)
    def _():
        yc_ref[...] = y

    @pl.when(jnp.logical_not(is_ctx))
    def _():
        yl_ref[...] = y


def _final(x1, routed, mod3, w_g, w_u, w_d, ln_g, ln_b, mod_map, t_ctx):
    t, d = x1.shape
    t_lat = t - t_ctx
    n_ctx_tiles = t_ctx // TOKEN_TILE
    tm = TOKEN_TILE
    row = lambda i: (i, 0)
    full = lambda i: (0, 0)
    return pl.pallas_call(
        functools.partial(_final_kernel, d=d, n_ctx_tiles=n_ctx_tiles),
        grid=(t // tm,),
        in_specs=[
            pl.BlockSpec((tm, d), row),
            pl.BlockSpec((tm * SUBLANES, LANES), row),
            pl.BlockSpec((1, 1, mod3.shape[2]), mod_map),
            _resident(w_g.shape),
            _resident(w_u.shape),
            _resident(w_d.shape),
            pl.BlockSpec((1, d), full),
            pl.BlockSpec((1, d), full),
        ],
        out_specs=_pair_specs(d, n_ctx_tiles),
        out_shape=[jax.ShapeDtypeStruct((t_ctx, d), F32), jax.ShapeDtypeStruct((t_lat, d), F32)],
        scratch_shapes=[pltpu.VMEM((SUBLANES * TILE_STRIDE, LANES), F32),
                        pltpu.VMEM(w_g.shape, BF16), pltpu.VMEM(w_u.shape, BF16), pltpu.VMEM(w_d.shape, BF16)],
        compiler_params=_params(("arbitrary",)),
    )(x1, routed, mod3, w_g, w_u, w_d, ln_g, ln_b)


def kernel(x_prompt, x_sample, cache_k, cache_v, state_hgrn, c, c_ctx, w_mod, b_mod, w_in, hg_lb, hg_norm_g, q_norm_g, k_norm_g, w_branch_a, w_branch_b, w_out, ln1_g, ln1_b, w_router, router_bias, w_e_gate, w_e_up, w_e_down, w_s_gate, w_s_up, w_s_down, ln2_g, ln2_b):
    assert w_mod.shape[0] == DEPTH
    n_ctx, seq_ctx, d = x_prompt.shape
    n_lat, seq_lat, _ = x_sample.shape
    t_ctx = n_ctx * seq_ctx
    t_lat = n_lat * seq_lat
    assert seq_ctx == TOKEN_TILE and seq_lat % TOKEN_TILE == 0
    layer = 0

    lb = jnp.cumsum(jax.nn.softmax(hg_lb.astype(F32), axis=0), axis=0)[layer]

    cond = jnp.concatenate([c_ctx[None, :], c], axis=0)
    cond = jnp.pad(cond, ((0, (-cond.shape[0]) % SUBLANES), (0, 0)))
    mod = _modulation(cond, w_mod[layer], b_mod[layer][None, :])
    mod3 = mod.reshape(mod.shape[0], 1, mod.shape[1])
    mod_map = _mod_row_map(t_ctx // TOKEN_TILE, seq_lat // TOKEN_TILE)

    x_ctx = x_prompt.reshape(t_ctx, d)
    x_lat = x_sample.reshape(t_lat, d)
    zh, zf, za, zg = _input_projection(x_ctx, x_lat, mod3, w_in[layer], lb, mod_map)

    norm_g = hg_norm_g[layer][None, :]
    oa_ctx, s_ctx = _hgrn_scan(zh, zf, norm_g, None, batch=n_ctx, seq=seq_ctx, row_block0=0,
                               heads=HG_HEADS_PER_STEP_CTX)
    oa_lat, _ = _hgrn_scan(zh, zf, norm_g, state_hgrn[:, layer], batch=n_lat, seq=seq_lat,
                           row_block0=t_ctx // seq_lat, heads=HG_HEADS_PER_STEP_LATENT)

    q_gain = jnp.tile(q_norm_g[layer], N_HEADS)[None, :]
    k_gain = jnp.tile(k_norm_g[layer], N_KV_HEADS)[None, :]
    lane = jnp.arange(ATT_WIDTH)
    group_ones = (lane[:, None] // HEAD_DIM == lane[None, :] // HEAD_DIM).astype(BF16)
    ob_ctx, k_ctx, v_ctx = _attention(za, q_gain, k_gain, group_ones, batch=n_ctx, seq=seq_ctx, row_block0=0)
    past = cache_k.shape[2]
    ob_lat = _attention(
        za, q_gain, k_gain, group_ones, batch=n_lat, seq=seq_lat, row_block0=t_ctx // seq_lat,
        rope=_rope_tables(seq_lat),
        cache=(cache_k[:, layer].reshape(n_lat, past, KV_WIDTH), cache_v[:, layer].reshape(n_lat, past, KV_WIDTH)))

    wr_t = w_router[layer].T
    wr_hi = wr_t.astype(BF16)
    wr_lo = (wr_t - wr_hi.astype(F32)).astype(BF16)
    x1, h2_slabs, idx_t, w_t = _post_mixer(
        x_ctx, x_lat, mod3, oa_ctx, oa_lat, ob_ctx, ob_lat, zg, w_branch_a[layer], w_branch_b[layer], w_out[layer],
        ln1_g[layer][None, :], ln1_b[layer][None, :], wr_hi, wr_lo, router_bias[layer][:, None], mod_map)

    routed = _routed_experts(h2_slabs, *_dispatch_lists(idx_t, w_t), w_e_gate[layer], w_e_up[layer], w_e_down[layer])

    y_ctx, y_lat = _final(x1, routed, mod3, w_s_gate[layer], w_s_up[layer], w_s_down[layer],
                          ln2_g[layer][None, :], ln2_b[layer][None, :], mod_map, t_ctx)

    y_prompt = y_ctx.reshape(n_ctx, seq_ctx, d)
    y_sample = y_lat.reshape(n_lat, seq_lat, d)
    new_cache_k = k_ctx.reshape(n_ctx, 1, seq_ctx, N_KV_HEADS, HEAD_DIM)
    new_cache_v = v_ctx.reshape(n_ctx, 1, seq_ctx, N_KV_HEADS, HEAD_DIM)
    new_state = s_ctx[:, None]
    return (y_prompt, y_sample, new_cache_k, new_cache_v, new_state)
```

```python
import functools

import jax
import jax.numpy as jnp
from jax import lax
from jax.experimental import pallas as pl
from jax.experimental.pallas import tpu as pltpu

F32 = jnp.float32
BF16 = jnp.bfloat16

GRID_W = 64
HG_HEADS = 4
HG_DK = 128
HG_DV = 128
HG_WIDTH = HG_HEADS * HG_DK
N_HEADS = 8
N_KV_HEADS = 2
HEAD_DIM = 64
ATT_WIDTH = N_HEADS * HEAD_DIM
KV_WIDTH = N_KV_HEADS * HEAD_DIM
ROPE_THETA = 10000.0
N_EXPERTS = 256
TOP_K = 8
N_GROUPS = 8
TOPK_GROUPS = 4
GROUP_SIZE = N_EXPERTS // N_GROUPS
ROUTED_SCALE = 2.5
NORM_EPS = 1e-6
DEPTH = 1
DEEPNORM_ALPHA = (2 * DEPTH) ** 0.25

LANES = 128
SUBLANES = 8
VMEM_LIMIT = 56 * 1024 * 1024

CAST_CHUNK = 512
TOKEN_TILE = 256
TILE_STRIDE = TOKEN_TILE + 1
HG_CHUNK = 32
HG_HEADS_PER_STEP_CTX = 4
HG_HEADS_PER_STEP_LATENT = 4
ATT_Q_BLOCK = 128
MOE_BLOCK = 320
BLOCK_COUNT_MULTIPLE = 20
ROW_STRIDE = MOE_BLOCK + 1
LIST_LEN = 512
LIST_ROWS = LIST_LEN // LANES
ROW_GROUP = 8
ROW_SEGMENT = 32
MAX_BLOCKS_PER_STEP = 2
GATHER_BLOCKS_PER_STEP = (10, 4, 2, 1)
WEIGHT_BUFFERS = 3


def _sigmoid(x):
    return 1.0 / (1.0 + jnp.exp(-x))


def _silu(x):
    return x * _sigmoid(x)


def _params(sem=None):
    return pltpu.CompilerParams(dimension_semantics=sem, vmem_limit_bytes=VMEM_LIMIT)


def _resident(shape):
    return pl.BlockSpec(shape, lambda i: (0,) * len(shape), pipeline_mode=pl.Buffered(1))


def _cast_weight_once(src_ref, dst_ref):
    @pl.when(pl.program_id(0) == 0)
    def _():
        width = src_ref.shape[1]
        for lo in range(0, width, CAST_CHUNK):
            hi = min(lo + CAST_CHUNK, width)
            dst_ref[:, lo:hi] = src_ref[:, lo:hi].astype(dst_ref.dtype)


def _mod_kernel(c_ref, w_ref, b_ref, o_ref):
    s = _silu(c_ref[...]).astype(BF16)
    o_ref[...] = jnp.dot(s, w_ref[...].astype(BF16), preferred_element_type=F32) + b_ref[...]


def _modulation(cond, w_mod, b_mod):
    n, d = cond.shape
    width = w_mod.shape[1]
    tn = width // 4
    return pl.pallas_call(
        _mod_kernel,
        grid=(4,),
        in_specs=[
            pl.BlockSpec((n, d), lambda j: (0, 0)),
            pl.BlockSpec((d, tn), lambda j: (0, j)),
            pl.BlockSpec((1, tn), lambda j: (0, j)),
        ],
        out_specs=pl.BlockSpec((n, tn), lambda j: (0, j)),
        out_shape=jax.ShapeDtypeStruct((n, width), F32),
        compiler_params=_params(("arbitrary",)),
    )(cond, w_mod, b_mod)


def _inproj_kernel(xc_ref, xl_ref, mod_ref, w_ref, lb_ref, zh_ref, zf_ref, za_ref, zg_ref, wb_ref,
                   *, d, n_ctx_tiles):
    _cast_weight_once(w_ref, wb_ref)
    shift = mod_ref[0, :, 0:d]
    scale = mod_ref[0, :, d:2 * d]
    x = _pick(pl.program_id(0) < n_ctx_tiles, xc_ref, xl_ref)
    h = (x * (1.0 + scale) + shift).astype(BF16)

    def proj(lo, hi):
        return jnp.dot(h, wb_ref[:, lo:hi], preferred_element_type=F32)

    w = HG_WIDTH
    zh_ref[:, 0:w] = _silu(proj(0, w)).astype(zh_ref.dtype)
    zh_ref[:, w:2 * w] = proj(w, 2 * w).astype(zh_ref.dtype)
    for i in range(2):
        lb = lb_ref[i:i + 1, :]
        zf_ref[:, i * w:(i + 1) * w] = lb + (1.0 - lb) * _sigmoid(proj((2 + i) * w, (3 + i) * w))
    zh_ref[:, 2 * w:3 * w] = _silu(proj(4 * w, 5 * w)).astype(zh_ref.dtype)
    a0 = 5 * w
    a1 = a0 + ATT_WIDTH + 2 * KV_WIDTH
    za_ref[...] = proj(a0, a1)
    for i in range(4):
        lo = a1 + i * (d // 2)
        zg_ref[:, i * (d // 2):(i + 1) * (d // 2)] = _sigmoid(proj(lo, lo + d // 2)).astype(zg_ref.dtype)


def _mod_row_map(n_ctx_tiles, tiles_per_latent):
    def index_map(i):
        row = jnp.where(i < n_ctx_tiles, 0, 1 + (i - n_ctx_tiles) // tiles_per_latent)
        return (row, 0, 0)
    return index_map


def _pair_specs(width, n_ctx_tiles):
    return [pl.BlockSpec((TOKEN_TILE, width), lambda i: (jnp.minimum(i, n_ctx_tiles - 1), 0)),
            pl.BlockSpec((TOKEN_TILE, width), lambda i: (jnp.maximum(i - n_ctx_tiles, 0), 0))]


def _pick(is_ctx, ctx_ref, lat_ref):
    return jnp.where(is_ctx, ctx_ref[...], lat_ref[...])


def _input_projection(x_ctx, x_lat, mod3, w_in, lb, mod_map):
    d = x_ctx.shape[1]
    n_ctx_tiles = x_ctx.shape[0] // TOKEN_TILE
    t = x_ctx.shape[0] + x_lat.shape[0]
    width = w_in.shape[1]
    zh_w = 3 * HG_WIDTH
    zf_w = 2 * HG_WIDTH
    za_w = ATT_WIDTH + 2 * KV_WIDTH
    zg_w = 2 * d
    assert width == zh_w + zf_w + za_w + zg_w
    tm = TOKEN_TILE
    return pl.pallas_call(
        functools.partial(_inproj_kernel, d=d, n_ctx_tiles=n_ctx_tiles),
        grid=(t // tm,),
        in_specs=_pair_specs(d, n_ctx_tiles) + [
            pl.BlockSpec((1, 1, mod3.shape[2]), mod_map),
            _resident((d, width)),
            pl.BlockSpec((2, HG_WIDTH), lambda i: (0, 0)),
        ],
        out_specs=[
            pl.BlockSpec((tm, zh_w), lambda i: (i, 0)),
            pl.BlockSpec((tm, zf_w), lambda i: (i, 0)),
            pl.BlockSpec((tm, za_w), lambda i: (i, 0)),
            pl.BlockSpec((tm, zg_w), lambda i: (i, 0)),
        ],
        out_shape=[
            jax.ShapeDtypeStruct((t, zh_w), BF16),
            jax.ShapeDtypeStruct((t, zf_w), F32),
            jax.ShapeDtypeStruct((t, za_w), F32),
            jax.ShapeDtypeStruct((t, zg_w), BF16),
        ],
        scratch_shapes=[pltpu.VMEM((d, width), BF16)],
        compiler_params=_params(("arbitrary",)),
    )(x_ctx, x_lat, mod3, w_in, lb)


def _hgrn_kernel(*refs, seq, has_state, heads):
    if has_state:
        q_ref, v_ref, ff_ref, fb_ref, gs_ref, ng_ref, s0_ref, o_ref, sout_ref = refs
    else:
        q_ref, v_ref, ff_ref, fb_ref, gs_ref, ng_ref, o_ref, sout_ref = refs
        s0_ref = None
    for hd in range(heads):
        _hgrn_head(q_ref, v_ref, ff_ref, fb_ref, gs_ref, ng_ref, s0_ref, o_ref, sout_ref, hd, seq)


def _hgrn_head(q_ref, v_ref, ff_ref, fb_ref, gs_ref, ng_ref, s0_ref, o_ref, sout_ref, hd, seq):
    has_state = s0_ref is not None
    cols = slice(hd * HG_DK, (hd + 1) * HG_DK)
    c = HG_CHUNK
    n = seq // c
    q3 = q_ref[:, cols].astype(F32).reshape(n, c, HG_DK)
    v3 = v_ref[:, cols].reshape(n, c, HG_DV)
    pos = lax.broadcasted_iota(jnp.int32, (seq, HG_DK), 0) % c
    t_idx = lax.broadcasted_iota(jnp.int32, (c, c), 0)
    s_idx = lax.broadcasted_iota(jnp.int32, (c, c), 1)
    o_sum = None
    for direction, f_ref in enumerate((ff_ref, fb_ref)):
        reverse = direction == 1
        f = f_ref[:, cols]
        k3 = (1.0 - f).reshape(n, c, HG_DK)
        b = jnp.log(f)
        step = 1
        while step < c:
            if reverse:
                b = b + jnp.where(pos < c - step, pltpu.roll(b, seq - step, axis=0), 0.0)
            else:
                b = b + jnp.where(pos >= step, pltpu.roll(b, step, axis=0), 0.0)
            step *= 2
        b3 = b.reshape(n, c, HG_DK)
        edge = b3[:, 0:1, :] if reverse else b3[:, c - 1:c, :]
        mid = b3[:, c // 2:c // 2 + 1, :]
        q_rel = q3 * jnp.exp(b3 - mid)
        k_rel = k3 * jnp.exp(mid - b3)
        q_mid = q_rel.astype(BF16)
        k_mid = k_rel.astype(BF16)
        q_in = (q_rel * jnp.exp(mid)).astype(BF16)
        k_end = (k_rel * jnp.exp(edge - mid)).astype(BF16)
        scores = jnp.einsum('ntd,nsd->nts', q_mid, k_mid, preferred_element_type=F32)
        keep = (s_idx >= t_idx) if reverse else (s_idx <= t_idx)
        scores = jnp.where(keep[None], scores, 0.0).astype(BF16)
        o_intra = jnp.einsum('nts,nsv->ntv', scores, v3, preferred_element_type=F32)
        upd = jnp.einsum('nsv,nsd->nvd', v3, k_end, preferred_element_type=F32)
        dec = jnp.exp(edge)
        if has_state:
            st = s0_ref[0, direction, hd].T
        else:
            st = jnp.zeros((HG_DV, HG_DK), F32)
        before = [None] * n
        for ci in (range(n - 1, -1, -1) if reverse else range(n)):
            before[ci] = st.astype(BF16)
            st = st * dec[ci] + upd[ci]
        st_before = jnp.stack(before, axis=0)
        o_inter = jnp.einsum('ntd,nvd->ntv', q_in, st_before, preferred_element_type=F32)
        o_dir = (o_intra + o_inter).reshape(seq, HG_DV)
        o_sum = o_dir if o_sum is None else o_sum + o_dir
        sout_ref[0, direction, hd] = st.T
    ms = jnp.mean(o_sum * o_sum, axis=-1, keepdims=True)
    o = o_sum * lax.rsqrt(ms + NORM_EPS) * ng_ref[...]
    o_ref[:, cols] = (o * gs_ref[:, cols].astype(F32)).astype(o_ref.dtype)


def _hgrn_scan(zh, zf, norm_g, s0, *, batch, seq, row_block0, heads):
    has_state = s0 is not None
    h = HG_HEADS
    groups = h // heads
    width = heads * HG_DK

    def col(section):
        return pl.BlockSpec((seq, width), lambda b, j: (row_block0 + b, section * groups + j))

    in_specs = [col(0), col(1), col(0), col(1), col(2), pl.BlockSpec((1, HG_DV), lambda b, j: (0, 0))]
    args = [zh, zh, zf, zf, zh, norm_g]
    state_spec = pl.BlockSpec((1, 2, heads, HG_DK, HG_DV), lambda b, j: (b, 0, j, 0, 0))
    if has_state:
        in_specs.append(state_spec)
        args.append(s0)
    return pl.pallas_call(
        functools.partial(_hgrn_kernel, seq=seq, has_state=has_state, heads=heads),
        grid=(batch, groups),
        in_specs=in_specs,
        out_specs=[pl.BlockSpec((seq, width), lambda b, j: (b, j)), state_spec],
        out_shape=[
            jax.ShapeDtypeStruct((batch * seq, h * HG_DV), BF16),
            jax.ShapeDtypeStruct((batch, 2, h, HG_DK, HG_DV), F32),
        ],
        compiler_params=_params(("arbitrary", "arbitrary")),
    )(*args)


def _group_rms_norm(x, group_ones, gain):
    sq = x * x
    hi = sq.astype(BF16)
    lo = (sq - hi.astype(F32)).astype(BF16)
    total = (jnp.dot(hi, group_ones, preferred_element_type=F32)
             + jnp.dot(lo, group_ones, preferred_element_type=F32))
    return x * lax.rsqrt(total * (1.0 / HEAD_DIM) + NORM_EPS) * gain


def _rope(x, cos, sin_signed):
    width = x.shape[1]
    quarter = HEAD_DIM // 4
    lane = lax.broadcasted_iota(jnp.int32, x.shape, 1)
    partner = jnp.where(lane % (2 * quarter) < quarter,
                        pltpu.roll(x, width - quarter, axis=1),
                        pltpu.roll(x, quarter, axis=1))
    return x * cos + partner * sin_signed


def _attn_kernel(*refs, seq, latent):
    if latent:
        za_ref, qg_ref, kg_ref, gm_ref, cos_ref, sin_ref, ck_ref, cv_ref, o_ref = refs
    else:
        za_ref, qg_ref, kg_ref, gm_ref, o_ref, kout_ref, vout_ref = refs
    q = za_ref[:, 0:ATT_WIDTH]
    k = za_ref[:, ATT_WIDTH:ATT_WIDTH + KV_WIDTH]
    v = za_ref[:, ATT_WIDTH + KV_WIDTH:ATT_WIDTH + 2 * KV_WIDTH]
    qn = _group_rms_norm(q, gm_ref[...], qg_ref[...])
    kn = _group_rms_norm(k, gm_ref[0:KV_WIDTH, 0:KV_WIDTH], kg_ref[...])
    if latent:
        qn = _rope(qn, cos_ref[...], sin_ref[...])
        kr = _rope(kn, cos_ref[:, 0:KV_WIDTH], sin_ref[:, 0:KV_WIDTH])
    else:
        kout_ref[...] = kn
        vout_ref[...] = v
        kr = kn
    qb = (qn * (HEAD_DIM ** -0.5)).astype(BF16)
    kb = kr.astype(BF16)
    vb = v.astype(BF16)
    if latent:
        ckb = ck_ref[0].astype(BF16)
        cvb = cv_ref[0].astype(BF16)
    groups = N_HEADS // N_KV_HEADS
    tq = ATT_Q_BLOCK if latent else seq
    nt = (((1,), (1,)), ((), ()))
    for kh in range(N_KV_HEADS):
        ksl = slice(kh * HEAD_DIM, (kh + 1) * HEAD_DIM)
        k_new = kb[:, ksl]
        v_new = vb[:, ksl]
        for blk in range(seq // tq):
            rows = slice(blk * tq, (blk + 1) * tq)
            heads = [kh * groups + g for g in range(groups)]
            q_st = jnp.concatenate([qb[rows, hd * HEAD_DIM:(hd + 1) * HEAD_DIM] for hd in heads], axis=0)
            s_new = lax.dot_general(q_st, k_new, nt, preferred_element_type=F32)
            m = jnp.max(s_new, axis=-1, keepdims=True)
            if latent:
                s_ctx = lax.dot_general(q_st, ckb[:, ksl], nt, preferred_element_type=F32)
                m = jnp.maximum(m, jnp.max(s_ctx, axis=-1, keepdims=True))
            p_new = jnp.exp(s_new - m)
            denom = jnp.sum(p_new, axis=-1, keepdims=True)
            acc = jnp.dot(p_new.astype(BF16), v_new, preferred_element_type=F32)
            if latent:
                p_ctx = jnp.exp(s_ctx - m)
                denom = denom + jnp.sum(p_ctx, axis=-1, keepdims=True)
                acc = acc + jnp.dot(p_ctx.astype(BF16), cvb[:, ksl], preferred_element_type=F32)
            out = acc / denom
            for g in range(0, groups, 2):
                pair = jnp.concatenate([out[g * tq:(g + 1) * tq], out[(g + 1) * tq:(g + 2) * tq]], axis=1)
                lane0 = heads[g] * HEAD_DIM
                o_ref[rows, lane0:lane0 + 2 * HEAD_DIM] = pair.astype(o_ref.dtype)


def _attention(za, q_gain, k_gain, group_ones, *, batch, seq, row_block0, rope=None, cache=None):
    latent = cache is not None
    za_w = za.shape[1]
    in_specs = [
        pl.BlockSpec((seq, za_w), lambda b: (row_block0 + b, 0)),
        pl.BlockSpec((1, ATT_WIDTH), lambda b: (0, 0)),
        pl.BlockSpec((1, KV_WIDTH), lambda b: (0, 0)),
        pl.BlockSpec((ATT_WIDTH, ATT_WIDTH), lambda b: (0, 0)),
    ]
    args = [za, q_gain, k_gain, group_ones]
    o_spec = pl.BlockSpec((seq, ATT_WIDTH), lambda b: (b, 0))
    o_shape = jax.ShapeDtypeStruct((batch * seq, ATT_WIDTH), BF16)
    if latent:
        cos, sin_signed = rope
        ck, cv = cache
        past = ck.shape[1]
        in_specs += [
            pl.BlockSpec((seq, ATT_WIDTH), lambda b: (0, 0)),
            pl.BlockSpec((seq, ATT_WIDTH), lambda b: (0, 0)),
            pl.BlockSpec((1, past, KV_WIDTH), lambda b: (b, 0, 0)),
            pl.BlockSpec((1, past, KV_WIDTH), lambda b: (b, 0, 0)),
        ]
        args += [cos, sin_signed, ck, cv]
        out_specs = o_spec
        out_shape = o_shape
    else:
        kv_spec = pl.BlockSpec((seq, KV_WIDTH), lambda b: (b, 0))
        kv_shape = jax.ShapeDtypeStruct((batch * seq, KV_WIDTH), F32)
        out_specs = [o_spec, kv_spec, kv_spec]
        out_shape = [o_shape, kv_shape, kv_shape]
    return pl.pallas_call(
        functools.partial(_attn_kernel, seq=seq, latent=latent),
        grid=(batch,),
        in_specs=in_specs,
        out_specs=out_specs,
        out_shape=out_shape,
        compiler_params=_params(("arbitrary",)),
    )(*args)


def _rope_tables(seq):
    quarter = HEAD_DIM // 4
    t = jnp.arange(seq)
    row = (t // GRID_W).astype(F32)
    colp = (t % GRID_W).astype(F32)
    inv_freq = ROPE_THETA ** (-jnp.arange(quarter, dtype=F32) / quarter)
    lane = jnp.arange(HEAD_DIM)
    pos = jnp.where((lane < HEAD_DIM // 2)[None, :], row[:, None], colp[:, None])
    ang = pos * inv_freq[lane % quarter][None, :]
    sign = jnp.where(lane % (2 * quarter) < quarter, -1.0, 1.0)[None, :]
    cos = jnp.tile(jnp.cos(ang), (1, N_HEADS))
    sin_signed = jnp.tile(jnp.sin(ang) * sign, (1, N_HEADS))
    return cos, sin_signed


def _layer_norm(y, g, b):
    mu = jnp.mean(y, axis=-1, keepdims=True)
    yc = y - mu
    var = jnp.mean(yc * yc, axis=-1, keepdims=True)
    return yc * lax.rsqrt(var + NORM_EPS) * g + b


def _rows_to_slabs(tile, st_ref, slab_ref):
    rows, d = tile.shape
    chunks = d // LANES
    for j in range(chunks):
        st_ref[j * TILE_STRIDE:j * TILE_STRIDE + rows, :] = tile[:, j * LANES:(j + 1) * LANES]
    for r in range(rows):
        slab_ref[r * chunks:(r + 1) * chunks, :] = st_ref[pl.ds(r, chunks, stride=TILE_STRIDE), :]


def _slabs_to_rows(slab_ref, st_ref, rows):
    chunks = SUBLANES
    for r in range(rows):
        st_ref[pl.ds(r, chunks, stride=TILE_STRIDE), :] = slab_ref[r * chunks:(r + 1) * chunks, :]
    return jnp.concatenate([st_ref[j * TILE_STRIDE:j * TILE_STRIDE + rows, :] for j in range(chunks)], axis=1)


def _post_mixer_kernel(xc_ref, xl_ref, mod_ref, oac_ref, oal_ref, obc_ref, obl_ref, zg_ref, wa_ref, wb_ref,
                       wo_ref, g_ref, b_ref, wrh_ref, wrl_ref, rb_ref,
                       x1_ref, slab_ref, idx_ref, rw_ref, st_ref, wab_ref, wbb_ref, wob_ref,
                       *, d, n_ctx_tiles):
    for src, dst in ((wa_ref, wab_ref), (wb_ref, wbb_ref), (wo_ref, wob_ref)):
        _cast_weight_once(src, dst)
    is_ctx = pl.program_id(0) < n_ctx_tiles
    gate1 = mod_ref[0, :, 2 * d:3 * d]
    shift2 = mod_ref[0, :, 3 * d:4 * d]
    scale2 = mod_ref[0, :, 4 * d:5 * d]
    branch_a = jnp.dot(_pick(is_ctx, oac_ref, oal_ref), wab_ref[...], preferred_element_type=F32)
    branch_b = jnp.dot(_pick(is_ctx, obc_ref, obl_ref), wbb_ref[...], preferred_element_type=F32)
    merged = zg_ref[:, 0:d].astype(F32) * branch_a + zg_ref[:, d:2 * d].astype(F32) * branch_b
    mix = jnp.dot(merged.astype(BF16), wob_ref[...], preferred_element_type=F32)
    x = _pick(is_ctx, xc_ref, xl_ref)
    x1 = _layer_norm(DEEPNORM_ALPHA * x + gate1 * mix, g_ref[...], b_ref[...])
    x1_ref[...] = x1
    h2 = x1 * (1.0 + scale2) + shift2
    _rows_to_slabs(h2, st_ref, slab_ref)
    idx_ref[...], rw_ref[...] = _route_tile(h2, wrh_ref[...], wrl_ref[...], rb_ref[...])


def _post_mixer(x_ctx, x_lat, mod3, oa_ctx, oa_lat, ob_ctx, ob_lat, zg, w_a, w_b, w_o, ln_g, ln_b,
                wr_hi, wr_lo, router_bias, mod_map):
    d = x_ctx.shape[1]
    n_ctx_tiles = x_ctx.shape[0] // TOKEN_TILE
    t = x_ctx.shape[0] + x_lat.shape[0]
    tm = TOKEN_TILE
    row = lambda i: (i, 0)
    full = lambda i: (0, 0)
    return pl.pallas_call(
        functools.partial(_post_mixer_kernel, d=d, n_ctx_tiles=n_ctx_tiles),
        grid=(t // tm,),
        in_specs=_pair_specs(d, n_ctx_tiles) + [
            pl.BlockSpec((1, 1, mod3.shape[2]), mod_map),
        ] + _pair_specs(oa_ctx.shape[1], n_ctx_tiles) + _pair_specs(ob_ctx.shape[1], n_ctx_tiles) + [
            pl.BlockSpec((tm, 2 * d), row),
            _resident(w_a.shape),
            _resident(w_b.shape),
            _resident(w_o.shape),
            pl.BlockSpec((1, d), full),
            pl.BlockSpec((1, d), full),
            pl.BlockSpec((N_EXPERTS, d), full),
            pl.BlockSpec((N_EXPERTS, d), full),
            pl.BlockSpec((N_EXPERTS, 1), full),
        ],
        out_specs=[pl.BlockSpec((tm, d), row),
                   pl.BlockSpec((tm * SUBLANES, LANES), row),
                   pl.BlockSpec((TOP_K, tm), lambda i: (0, i)), pl.BlockSpec((TOP_K, tm), lambda i: (0, i))],
        out_shape=[jax.ShapeDtypeStruct((t, d), F32),
                   jax.ShapeDtypeStruct((t * SUBLANES, LANES), F32),
                   jax.ShapeDtypeStruct((TOP_K, t), jnp.int32), jax.ShapeDtypeStruct((TOP_K, t), F32)],
        scratch_shapes=[pltpu.VMEM((SUBLANES * TILE_STRIDE, LANES), F32),
                        pltpu.VMEM(w_a.shape, BF16), pltpu.VMEM(w_b.shape, BF16), pltpu.VMEM(w_o.shape, BF16)],
        compiler_params=_params(("arbitrary",)),
    )(x_ctx, x_lat, mod3, oa_ctx, oa_lat, ob_ctx, ob_lat, zg, w_a, w_b, w_o, ln_g, ln_b, wr_hi, wr_lo, router_bias)


def _first_index_of_max(x, idx, sentinel):
    m = jnp.max(x, axis=0, keepdims=True)
    first = jnp.min(jnp.where(x == m, idx, sentinel), axis=0, keepdims=True)
    return m, first


def _route_tile(h, wh, wl, bias):
    hh = h.astype(BF16)
    hl = (h - hh.astype(F32)).astype(BF16)
    nt = (((1,), (1,)), ((), ()))
    logits = (lax.dot_general(wh, hh, nt, preferred_element_type=F32)
              + lax.dot_general(wh, hl, nt, preferred_element_type=F32)
              + lax.dot_general(wl, hh, nt, preferred_element_type=F32))
    scores = _sigmoid(logits)
    sel = scores + bias
    tm = sel.shape[1]
    neg = -jnp.inf
    gidx = lax.broadcasted_iota(jnp.int32, (GROUP_SIZE, tm), 0)
    group_scores = []
    for g in range(N_GROUPS):
        sg = sel[g * GROUP_SIZE:(g + 1) * GROUP_SIZE, :]
        m1, first = _first_index_of_max(sg, gidx, GROUP_SIZE)
        m2 = jnp.max(jnp.where(gidx == first, neg, sg), axis=0, keepdims=True)
        group_scores.append(m1 + m2)
    gs = jnp.concatenate(group_scores, axis=0)
    nidx = lax.broadcasted_iota(jnp.int32, (N_GROUPS, tm), 0)
    chosen = jnp.zeros((N_GROUPS, tm), jnp.bool_)
    for _ in range(TOPK_GROUPS):
        _, first = _first_index_of_max(gs, nidx, N_GROUPS)
        hit = nidx == first
        chosen = jnp.logical_or(chosen, hit)
        gs = jnp.where(hit, neg, gs)
    masked = jnp.concatenate(
        [jnp.where(chosen[g:g + 1, :], sel[g * GROUP_SIZE:(g + 1) * GROUP_SIZE, :], neg) for g in range(N_GROUPS)],
        axis=0)
    eidx = lax.broadcasted_iota(jnp.int32, (N_EXPERTS, tm), 0)
    picks, weights = [], []
    for _ in range(TOP_K):
        _, first = _first_index_of_max(masked, eidx, N_EXPERTS)
        hit = eidx == first
        picks.append(first)
        weights.append(jnp.sum(jnp.where(hit, scores, 0.0), axis=0, keepdims=True))
        masked = jnp.where(hit, neg, masked)
    wk = jnp.concatenate(weights, axis=0)
    return jnp.concatenate(picks, axis=0), wk / jnp.sum(wk, axis=0, keepdims=True) * ROUTED_SCALE


def _dispatch_lists(idx_t, w_t):
    k, t = idx_t.shape
    s = k * t
    assert N_EXPERTS * t < 2 ** 31
    n_blocks = -(-(s // MOE_BLOCK + N_EXPERTS) // BLOCK_COUNT_MULTIPLE) * BLOCK_COUNT_MULTIPLE
    flat_tok = jnp.arange(s, dtype=jnp.int32) % t
    sorted_key, sorted_w = lax.sort((idx_t.reshape(s) * t + flat_tok, w_t.reshape(s)), num_keys=1)
    sorted_tok = sorted_key % t
    experts = jnp.arange(N_EXPERTS + 1, dtype=jnp.int32)
    start = jnp.sum(sorted_key[None, :] < experts[:, None] * t, axis=1, dtype=jnp.int32)
    counts = start[1:] - start[:-1]
    blocks_per_e = (counts + MOE_BLOCK - 1) // MOE_BLOCK
    block_end = jnp.cumsum(blocks_per_e)
    n_used = block_end[-1]
    blk = jnp.arange(n_blocks, dtype=jnp.int32)
    block_e = jnp.sum(block_end[None, :] <= blk[:, None], axis=1, dtype=jnp.int32)
    block_e = jnp.minimum(block_e, N_EXPERTS - 1)
    onehot = (block_e[:, None] == experts[None, :N_EXPERTS]).astype(jnp.int32)
    pick = lambda a: jnp.sum(onehot * a[None, :], axis=1)
    offset = (blk - (pick(block_end) - pick(blocks_per_e))) * MOE_BLOCK
    row_start = pick(start[:-1]) + offset
    n_rows = jnp.clip(pick(counts) - offset, 0, MOE_BLOCK)
    used = blk < n_used
    last_e = jnp.sum(jnp.where(blk == n_used - 1, block_e, 0))
    block_e = jnp.where(used, block_e, last_e)
    row_start = jnp.where(used, row_start, 0)
    n_rows = jnp.where(used, n_rows, 0)
    return block_e, row_start, n_rows, (n_used - 1).reshape(1), sorted_tok, sorted_w


def _block_lists_kernel(rs_ref, nr_ref, tok_ref, w_ref, off_ref, wout_ref, *, t):
    lane = lax.broadcasted_iota(jnp.int32, (LIST_ROWS, LANES), 1)
    row = lax.broadcasted_iota(jnp.int32, (LIST_ROWS, LANES), 0)

    def one_block(b, carry):
        start = rs_ref[b]
        n_rows = nr_ref[b]
        q = start // LANES
        lane0 = start % LANES
        q8 = pl.multiple_of((q // SUBLANES) * SUBLANES, SUBLANES)

        def window(ref):
            a = ref[pl.ds(q8, 2 * SUBLANES), :]
            a = pltpu.roll(a, 2 * SUBLANES - (q - q8), axis=0)
            lo = pltpu.roll(a[0:LIST_ROWS], LANES - lane0, axis=1)
            hi = pltpu.roll(a[1:LIST_ROWS + 1], LANES - lane0, axis=1)
            return jnp.where(lane < LANES - lane0, lo, hi)

        valid = row * LANES + lane < n_rows
        off_ref[b] = jnp.where(valid, window(tok_ref) * SUBLANES, t * SUBLANES)
        wout_ref[b] = jnp.where(valid, window(w_ref), 0.0)
        return carry

    n_blocks = off_ref.shape[0]
    unroll = next(u for u in (4, 2, 1) if n_blocks % u == 0)

    def several_blocks(i, carry):
        for u in range(unroll):
            one_block(i * unroll + u, carry)
        return carry

    lax.fori_loop(0, n_blocks // unroll, several_blocks, 0)


def _block_lists(row_start, n_rows, sorted_tok, sorted_w, t):
    n_blocks = row_start.shape[0]
    n_slots = sorted_tok.shape[0]
    list_rows = n_slots // LANES + 2 * SUBLANES
    pad = list_rows * LANES - n_slots
    tok2d = jnp.pad(sorted_tok, (0, pad)).reshape(list_rows, LANES)
    w2d = jnp.pad(sorted_w, (0, pad)).reshape(list_rows, LANES)
    full = lambda i, rs, nr: (0, 0)
    blk = pl.BlockSpec((n_blocks, LIST_ROWS, LANES), lambda i, rs, nr: (0, 0, 0))
    off, wts = pl.pallas_call(
        functools.partial(_block_lists_kernel, t=t),
        grid_spec=pltpu.PrefetchScalarGridSpec(
            num_scalar_prefetch=2,
            grid=(1,),
            in_specs=[pl.BlockSpec((list_rows, LANES), full), pl.BlockSpec((list_rows, LANES), full)],
            out_specs=[blk, blk],
        ),
        out_shape=[jax.ShapeDtypeStruct((n_blocks, LIST_ROWS, LANES), jnp.int32),
                   jax.ShapeDtypeStruct((n_blocks, LIST_ROWS, LANES), F32)],
        compiler_params=_params(("arbitrary",)),
    )(row_start, n_rows, tok2d, w2d)
    return off.reshape(n_blocks * LIST_LEN), wts.reshape(n_blocks * LIST_LEN)


def _gather_kernel(nr_ref, last_ref, x_hbm, off_ref, out_ref, xs, xt, sem):
    step = pl.program_id(0)
    per_step = out_ref.shape[0] // MOE_BLOCK
    chunks = out_ref.shape[1] // LANES

    @pl.when(step == 0)
    def _load():
        rows = x_hbm.shape[0]
        cp = pltpu.make_async_copy(x_hbm, xs.at[pl.ds(0, rows)], sem)
        cp.start()
        cp.wait()
        xs[pl.ds(rows, SUBLANES), :] = jnp.zeros((SUBLANES, LANES), F32)
        xt[...] = jnp.zeros(xt.shape, F32)

    def one_block(g):
        n_rows = nr_ref[step * per_step + g]

        @pl.when(n_rows > 0)
        def _block():
            for seg in range(MOE_BLOCK // ROW_SEGMENT):
                @pl.when(n_rows > seg * ROW_SEGMENT)
                def _gather(seg=seg):
                    for r in range(seg * ROW_SEGMENT, (seg + 1) * ROW_SEGMENT):
                        src = pl.multiple_of(off_ref[g * LIST_LEN + r], SUBLANES)
                        xt[pl.ds(r, chunks, stride=ROW_STRIDE), :] = xs[pl.ds(src, SUBLANES), :]

            x = jnp.concatenate(
                [xt[j * ROW_STRIDE:j * ROW_STRIDE + MOE_BLOCK, :] for j in range(chunks)], axis=1)
            out_ref[g * MOE_BLOCK:(g + 1) * MOE_BLOCK, :] = x.astype(out_ref.dtype)

    for g in range(per_step):
        one_block(g)


def _expert_kernel(be_ref, nr_ref, last_ref, x_ref, wg_hbm, wu_hbm, wd_hbm, off_ref, w_ref, y_hbm,
                   ys, ot, wgb, wub, wdb, sem, wsem, *, t, n_steps):
    step = pl.program_id(0)
    per_step = x_ref.shape[0] // MOE_BLOCK
    rows = t * SUBLANES
    chunks = x_ref.shape[1] // LANES
    group = ROW_GROUP
    last = last_ref[0]
    ahead = WEIGHT_BUFFERS - 1

    def weight_copies(blk):
        e = be_ref[blk]
        slot = blk % WEIGHT_BUFFERS
        return [pltpu.make_async_copy(src.at[e], dst.at[slot], wsem.at[slot])
                for src, dst in ((wg_hbm, wgb), (wu_hbm, wub), (wd_hbm, wdb))]

    @pl.when(step == 0)
    def _init():
        for blk in range(ahead):
            @pl.when(blk <= last)
            def _(blk=blk):
                for cp in weight_copies(blk):
                    cp.start()
        ys[...] = jnp.zeros(ys.shape, F32)

    def one_block(g):
        b = step * per_step + g
        n_rows = nr_ref[b]
        row0 = g * MOE_BLOCK
        slot0 = g * LIST_LEN

        @pl.when(b + ahead <= last)
        def _prefetch():
            for cp in weight_copies(b + ahead):
                cp.start()

        @pl.when(b <= last)
        def _block():
            slot = b % WEIGHT_BUFFERS
            for cp in weight_copies(b):
                cp.wait()
            x = x_ref[row0:row0 + MOE_BLOCK, :]
            gate = jnp.dot(x, wgb[slot].astype(BF16), preferred_element_type=F32)
            up = jnp.dot(x, wub[slot].astype(BF16), preferred_element_type=F32)
            hidden = (_silu(gate) * up).astype(BF16)
            out = jnp.dot(hidden, wdb[slot].astype(BF16), preferred_element_type=F32)
            for j in range(chunks):
                ot[j * ROW_STRIDE:j * ROW_STRIDE + MOE_BLOCK, :] = out[:, j * LANES:(j + 1) * LANES]
            for seg in range(MOE_BLOCK // ROW_SEGMENT):
                @pl.when(n_rows > seg * ROW_SEGMENT)
                def _scatter(seg=seg):
                    for r0 in range(seg * ROW_SEGMENT, (seg + 1) * ROW_SEGMENT, group):
                        dst = [pl.multiple_of(off_ref[slot0 + r0 + i], SUBLANES) for i in range(group)]
                        vals = [ys[pl.ds(dst[i], SUBLANES), :]
                                + w_ref[slot0 + r0 + i] * ot[pl.ds(r0 + i, chunks, stride=ROW_STRIDE), :]
                                for i in range(group)]
                        for i in range(group):
                            ys[pl.ds(dst[i], SUBLANES), :] = vals[i]

    for g in range(per_step):
        one_block(g)

    @pl.when(step == n_steps - 1)
    def _store():
        cp = pltpu.make_async_copy(ys.at[pl.ds(0, rows)], y_hbm, sem)
        cp.start()
        cp.wait()


def _routed_experts(h2_rows, block_e, row_start, n_rows, last_block, sorted_tok, sorted_w, w_gate, w_up, w_down):
    rows, lanes = h2_rows.shape
    t = rows // SUBLANES
    n_blocks = block_e.shape[0]
    d, e_dim = w_gate.shape[1], w_gate.shape[2]
    assert d == SUBLANES * LANES and lanes == LANES
    offsets, weights = _block_lists(row_start, n_rows, sorted_tok, sorted_w, t)
    staging = pltpu.VMEM((SUBLANES * ROW_STRIDE, LANES), F32)
    resident = pltpu.VMEM((rows + SUBLANES, LANES), F32)

    def step_map(per_step):
        return lambda s, *prefetch: (jnp.minimum(s, prefetch[-1][0] // per_step), 0)

    def list_spec(per_step):
        index = step_map(per_step)
        return pl.BlockSpec((per_step * LIST_LEN,), lambda s, *prefetch: index(s, *prefetch)[:1],
                            memory_space=pltpu.SMEM)

    per_step = next(p for p in GATHER_BLOCKS_PER_STEP if n_blocks % p == 0)
    gathered = pl.pallas_call(
        _gather_kernel,
        grid_spec=pltpu.PrefetchScalarGridSpec(
            num_scalar_prefetch=2,
            grid=(n_blocks // per_step,),
            in_specs=[pl.BlockSpec(memory_space=pl.ANY), list_spec(per_step)],
            out_specs=pl.BlockSpec((per_step * MOE_BLOCK, d), step_map(per_step)),
            scratch_shapes=[resident, staging, pltpu.SemaphoreType.DMA(())],
        ),
        out_shape=jax.ShapeDtypeStruct((n_blocks * MOE_BLOCK, d), BF16),
        compiler_params=_params(("arbitrary",)),
    )(n_rows, last_block, h2_rows, offsets)

    per_step = next(p for p in range(MAX_BLOCKS_PER_STEP, 0, -1) if n_blocks % p == 0)
    n_steps = n_blocks // per_step
    hbm = pl.BlockSpec(memory_space=pl.ANY)
    return pl.pallas_call(
        functools.partial(_expert_kernel, t=t, n_steps=n_steps),
        grid_spec=pltpu.PrefetchScalarGridSpec(
            num_scalar_prefetch=3,
            grid=(n_steps,),
            in_specs=[pl.BlockSpec((per_step * MOE_BLOCK, d), step_map(per_step)), hbm, hbm, hbm,
                      list_spec(per_step), list_spec(per_step)],
            out_specs=hbm,
            scratch_shapes=[
                resident,
                staging,
                pltpu.VMEM((WEIGHT_BUFFERS, d, e_dim), F32),
                pltpu.VMEM((WEIGHT_BUFFERS, d, e_dim), F32),
                pltpu.VMEM((WEIGHT_BUFFERS, e_dim, d), F32),
                pltpu.SemaphoreType.DMA(()),
                pltpu.SemaphoreType.DMA((WEIGHT_BUFFERS,)),
            ],
        ),
        out_shape=jax.ShapeDtypeStruct((rows, LANES), F32),
        compiler_params=_params(("arbitrary",)),
    )(block_e, n_rows, last_block, gathered, w_gate, w_up, w_down, offsets, weights)


def _final_kernel(x1_ref, routed_ref, mod_ref, wg_ref, wu_ref, wd_ref, g_ref, b_ref,
                  yc_ref, yl_ref, st_ref, wgb_ref, wub_ref, wdb_ref, *, d, n_ctx_tiles):
    for src, dst in ((wg_ref, wgb_ref), (wu_ref, wub_ref), (wd_ref, wdb_ref)):
        _cast_weight_once(src, dst)
    is_ctx = pl.program_id(0) < n_ctx_tiles
    shift2 = mod_ref[0, :, 3 * d:4 * d]
    scale2 = mod_ref[0, :, 4 * d:5 * d]
    gate2 = mod_ref[0, :, 5 * d:6 * d]
    x1 = x1_ref[...]
    h = (x1 * (1.0 + scale2) + shift2).astype(BF16)
    gate = jnp.dot(h, wgb_ref[...], preferred_element_type=F32)
    up = jnp.dot(h, wub_ref[...], preferred_element_type=F32)
    shared = jnp.dot((_silu(gate) * up).astype(BF16), wdb_ref[...], preferred_element_type=F32)
    ffn = _slabs_to_rows(routed_ref, st_ref, h.shape[0]) + shared
    y = _layer_norm(DEEPNORM_ALPHA * x1 + gate2 * ffn, g_ref[...], b_ref[...])

    @pl.when(is_ctx)
    def _():
        yc_ref[...] = y

    @pl.when(jnp.logical_not(is_ctx))
    def _():
        yl_ref[...] = y


def _final(x1, routed, mod3, w_g, w_u, w_d, ln_g, ln_b, mod_map, t_ctx):
    t, d = x1.shape
    t_lat = t - t_ctx
    n_ctx_tiles = t_ctx // TOKEN_TILE
    tm = TOKEN_TILE
    row = lambda i: (i, 0)
    full = lambda i: (0, 0)
    return pl.pallas_call(
        functools.partial(_final_kernel, d=d, n_ctx_tiles=n_ctx_tiles),
        grid=(t // tm,),
        in_specs=[
            pl.BlockSpec((tm, d), row),
            pl.BlockSpec((tm * SUBLANES, LANES), row),
            pl.BlockSpec((1, 1, mod3.shape[2]), mod_map),
            _resident(w_g.shape),
            _resident(w_u.shape),
            _resident(w_d.shape),
            pl.BlockSpec((1, d), full),
            pl.BlockSpec((1, d), full),
        ],
        out_specs=_pair_specs(d, n_ctx_tiles),
        out_shape=[jax.ShapeDtypeStruct((t_ctx, d), F32), jax.ShapeDtypeStruct((t_lat, d), F32)],
        scratch_shapes=[pltpu.VMEM((SUBLANES * TILE_STRIDE, LANES), F32),
                        pltpu.VMEM(w_g.shape, BF16), pltpu.VMEM(w_u.shape, BF16), pltpu.VMEM(w_d.shape, BF16)],
        compiler_params=_params(("arbitrary",)),
    )(x1, routed, mod3, w_g, w_u, w_d, ln_g, ln_b)


def kernel(x_prompt, x_sample, cache_k, cache_v, state_hgrn, c, c_ctx, w_mod, b_mod, w_in, hg_lb, hg_norm_g, q_norm_g, k_norm_g, w_branch_a, w_branch_b, w_out, ln1_g, ln1_b, w_router, router_bias, w_e_gate, w_e_up, w_e_down, w_s_gate, w_s_up, w_s_down, ln2_g, ln2_b):
    assert w_mod.shape[0] == DEPTH
    n_ctx, seq_ctx, d = x_prompt.shape
    n_lat, seq_lat, _ = x_sample.shape
    t_ctx = n_ctx * seq_ctx
    t_lat = n_lat * seq_lat
    assert seq_ctx == TOKEN_TILE and seq_lat % TOKEN_TILE == 0
    layer = 0

    lb = jnp.cumsum(jax.nn.softmax(hg_lb.astype(F32), axis=0), axis=0)[layer]

    cond = jnp.concatenate([c_ctx[None, :], c], axis=0)
    cond = jnp.pad(cond, ((0, (-cond.shape[0]) % SUBLANES), (0, 0)))
    mod = _modulation(cond, w_mod[layer], b_mod[layer][None, :])
    mod3 = mod.reshape(mod.shape[0], 1, mod.shape[1])
    mod_map = _mod_row_map(t_ctx // TOKEN_TILE, seq_lat // TOKEN_TILE)

    x_ctx = x_prompt.reshape(t_ctx, d)
    x_lat = x_sample.reshape(t_lat, d)
    zh, zf, za, zg = _input_projection(x_ctx, x_lat, mod3, w_in[layer], lb, mod_map)

    norm_g = hg_norm_g[layer][None, :]
    oa_ctx, s_ctx = _hgrn_scan(zh, zf, norm_g, None, batch=n_ctx, seq=seq_ctx, row_block0=0,
                               heads=HG_HEADS_PER_STEP_CTX)
    oa_lat, _ = _hgrn_scan(zh, zf, norm_g, state_hgrn[:, layer], batch=n_lat, seq=seq_lat,
                           row_block0=t_ctx // seq_lat, heads=HG_HEADS_PER_STEP_LATENT)

    q_gain = jnp.tile(q_norm_g[layer], N_HEADS)[None, :]
    k_gain = jnp.tile(k_norm_g[layer], N_KV_HEADS)[None, :]
    lane = jnp.arange(ATT_WIDTH)
    group_ones = (lane[:, None] // HEAD_DIM == lane[None, :] // HEAD_DIM).astype(BF16)
    ob_ctx, k_ctx, v_ctx = _attention(za, q_gain, k_gain, group_ones, batch=n_ctx, seq=seq_ctx, row_block0=0)
    past = cache_k.shape[2]
    ob_lat = _attention(
        za, q_gain, k_gain, group_ones, batch=n_lat, seq=seq_lat, row_block0=t_ctx // seq_lat,
        rope=_rope_tables(seq_lat),
        cache=(cache_k[:, layer].reshape(n_lat, past, KV_WIDTH), cache_v[:, layer].reshape(n_lat, past, KV_WIDTH)))

    wr_t = w_router[layer].T
    wr_hi = wr_t.astype(BF16)
    wr_lo = (wr_t - wr_hi.astype(F32)).astype(BF16)
    x1, h2_slabs, idx_t, w_t = _post_mixer(
        x_ctx, x_lat, mod3, oa_ctx, oa_lat, ob_ctx, ob_lat, zg, w_branch_a[layer], w_branch_b[layer], w_out[layer],
        ln1_g[layer][None, :], ln1_b[layer][None, :], wr_hi, wr_lo, router_bias[layer][:, None], mod_map)

    routed = _routed_experts(h2_slabs, *_dispatch_lists(idx_t, w_t), w_e_gate[layer], w_e_up[layer], w_e_down[layer])

    y_ctx, y_lat = _final(x1, routed, mod3, w_s_gate[layer], w_s_up[layer], w_s_down[layer],
                          ln2_g[layer][None, :], ln2_b[layer][None, :], mod_map, t_ctx)

    y_prompt = y_ctx.reshape(n_ctx, seq_ctx, d)
    y_sample = y_lat.reshape(n_lat, seq_lat, d)
    new_cache_k = k_ctx.reshape(n_ctx, 1, seq_ctx, N_KV_HEADS, HEAD_DIM)
    new_cache_v = v_ctx.reshape(n_ctx, 1, seq_ctx, N_KV_HEADS, HEAD_DIM)
    new_state = s_ctx[:, None]
    return (y_prompt, y_sample, new_cache_k, new_cache_v, new_state)
```

```python
import functools

import jax
import jax.numpy as jnp
from jax import lax
from jax.experimental import pallas as pl
from jax.experimental.pallas import tpu as pltpu

F32 = jnp.float32
BF16 = jnp.bfloat16

GRID_W = 64
HG_HEADS = 4
HG_DK = 128
HG_DV = 128
HG_WIDTH = HG_HEADS * HG_DK
N_HEADS = 8
N_KV_HEADS = 2
HEAD_DIM = 64
ATT_WIDTH = N_HEADS * HEAD_DIM
KV_WIDTH = N_KV_HEADS * HEAD_DIM
ROPE_THETA = 10000.0
N_EXPERTS = 256
TOP_K = 8
N_GROUPS = 8
TOPK_GROUPS = 4
GROUP_SIZE = N_EXPERTS // N_GROUPS
ROUTED_SCALE = 2.5
NORM_EPS = 1e-6
DEPTH = 1
DEEPNORM_ALPHA = (2 * DEPTH) ** 0.25

LANES = 128
SUBLANES = 8
VMEM_LIMIT = 56 * 1024 * 1024

CAST_CHUNK = 512
TOKEN_TILE = 256
TILE_STRIDE = TOKEN_TILE + 1
HG_CHUNK = 32
HG_HEADS_PER_STEP_CTX = 4
HG_HEADS_PER_STEP_LATENT = 4
ATT_Q_BLOCK = 128
MOE_BLOCK = 320
BLOCK_COUNT_MULTIPLE = 20
ROW_STRIDE = MOE_BLOCK + 1
LIST_LEN = 512
LIST_ROWS = LIST_LEN // LANES
ROW_GROUP = 8
ROW_SEGMENT = 32
MAX_BLOCKS_PER_STEP = 1
GATHER_BLOCKS_PER_STEP = (10, 4, 2, 1)
WEIGHT_BUFFERS = 3


def _sigmoid(x):
    return 1.0 / (1.0 + jnp.exp(-x))


def _silu(x):
    return x * _sigmoid(x)


def _params(sem=None):
    return pltpu.CompilerParams(dimension_semantics=sem, vmem_limit_bytes=VMEM_LIMIT)


def _resident(shape):
    return pl.BlockSpec(shape, lambda i: (0,) * len(shape), pipeline_mode=pl.Buffered(1))


def _cast_weight_once(src_ref, dst_ref):
    @pl.when(pl.program_id(0) == 0)
    def _():
        width = src_ref.shape[1]
        for lo in range(0, width, CAST_CHUNK):
            hi = min(lo + CAST_CHUNK, width)
            dst_ref[:, lo:hi] = src_ref[:, lo:hi].astype(dst_ref.dtype)


def _mod_kernel(c_ref, w_ref, b_ref, o_ref):
    s = _silu(c_ref[...]).astype(BF16)
    o_ref[...] = jnp.dot(s, w_ref[...].astype(BF16), preferred_element_type=F32) + b_ref[...]


def _modulation(cond, w_mod, b_mod):
    n, d = cond.shape
    width = w_mod.shape[1]
    tn = width // 4
    return pl.pallas_call(
        _mod_kernel,
        grid=(4,),
        in_specs=[
            pl.BlockSpec((n, d), lambda j: (0, 0)),
            pl.BlockSpec((d, tn), lambda j: (0, j)),
            pl.BlockSpec((1, tn), lambda j: (0, j)),
        ],
        out_specs=pl.BlockSpec((n, tn), lambda j: (0, j)),
        out_shape=jax.ShapeDtypeStruct((n, width), F32),
        compiler_params=_params(("arbitrary",)),
    )(cond, w_mod, b_mod)


def _inproj_kernel(xc_ref, xl_ref, mod_ref, w_ref, lb_ref, zh_ref, zf_ref, za_ref, zg_ref, wb_ref,
                   *, d, n_ctx_tiles):
    _cast_weight_once(w_ref, wb_ref)
    shift = mod_ref[0, :, 0:d]
    scale = mod_ref[0, :, d:2 * d]
    x = _pick(pl.program_id(0) < n_ctx_tiles, xc_ref, xl_ref)
    h = (x * (1.0 + scale) + shift).astype(BF16)

    def proj(lo, hi):
        return jnp.dot(h, wb_ref[:, lo:hi], preferred_element_type=F32)

    w = HG_WIDTH
    zh_ref[:, 0:w] = _silu(proj(0, w)).astype(zh_ref.dtype)
    zh_ref[:, w:2 * w] = proj(w, 2 * w).astype(zh_ref.dtype)
    for i in range(2):
        lb = lb_ref[i:i + 1, :]
        zf_ref[:, i * w:(i + 1) * w] = lb + (1.0 - lb) * _sigmoid(proj((2 + i) * w, (3 + i) * w))
    zh_ref[:, 2 * w:3 * w] = _silu(proj(4 * w, 5 * w)).astype(zh_ref.dtype)
    a0 = 5 * w
    a1 = a0 + ATT_WIDTH + 2 * KV_WIDTH
    za_ref[...] = proj(a0, a1)
    for i in range(4):
        lo = a1 + i * (d // 2)
        zg_ref[:, i * (d // 2):(i + 1) * (d // 2)] = _sigmoid(proj(lo, lo + d // 2)).astype(zg_ref.dtype)


def _mod_row_map(n_ctx_tiles, tiles_per_latent):
    def index_map(i):
        row = jnp.where(i < n_ctx_tiles, 0, 1 + (i - n_ctx_tiles) // tiles_per_latent)
        return (row, 0, 0)
    return index_map


def _pair_specs(width, n_ctx_tiles):
    return [pl.BlockSpec((TOKEN_TILE, width), lambda i: (jnp.minimum(i, n_ctx_tiles - 1), 0)),
            pl.BlockSpec((TOKEN_TILE, width), lambda i: (jnp.maximum(i - n_ctx_tiles, 0), 0))]


def _pick(is_ctx, ctx_ref, lat_ref):
    return jnp.where(is_ctx, ctx_ref[...], lat_ref[...])


def _input_projection(x_ctx, x_lat, mod3, w_in, lb, mod_map):
    d = x_ctx.shape[1]
    n_ctx_tiles = x_ctx.shape[0] // TOKEN_TILE
    t = x_ctx.shape[0] + x_lat.shape[0]
    width = w_in.shape[1]
    zh_w = 3 * HG_WIDTH
    zf_w = 2 * HG_WIDTH
    za_w = ATT_WIDTH + 2 * KV_WIDTH
    zg_w = 2 * d
    assert width == zh_w + zf_w + za_w + zg_w
    tm = TOKEN_TILE
    return pl.pallas_call(
        functools.partial(_inproj_kernel, d=d, n_ctx_tiles=n_ctx_tiles),
        grid=(t // tm,),
        in_specs=_pair_specs(d, n_ctx_tiles) + [
            pl.BlockSpec((1, 1, mod3.shape[2]), mod_map),
            _resident((d, width)),
            pl.BlockSpec((2, HG_WIDTH), lambda i: (0, 0)),
        ],
        out_specs=[
            pl.BlockSpec((tm, zh_w), lambda i: (i, 0)),
            pl.BlockSpec((tm, zf_w), lambda i: (i, 0)),
            pl.BlockSpec((tm, za_w), lambda i: (i, 0)),
            pl.BlockSpec((tm, zg_w), lambda i: (i, 0)),
        ],
        out_shape=[
            jax.ShapeDtypeStruct((t, zh_w), BF16),
            jax.ShapeDtypeStruct((t, zf_w), F32),
            jax.ShapeDtypeStruct((t, za_w), F32),
            jax.ShapeDtypeStruct((t, zg_w), BF16),
        ],
        scratch_shapes=[pltpu.VMEM((d, width), BF16)],
        compiler_params=_params(("arbitrary",)),
    )(x_ctx, x_lat, mod3, w_in, lb)


def _hgrn_kernel(*refs, seq, has_state, heads):
    if has_state:
        q_ref, v_ref, ff_ref, fb_ref, gs_ref, ng_ref, s0_ref, o_ref, sout_ref = refs
    else:
        q_ref, v_ref, ff_ref, fb_ref, gs_ref, ng_ref, o_ref, sout_ref = refs
        s0_ref = None
    for hd in range(heads):
        _hgrn_head(q_ref, v_ref, ff_ref, fb_ref, gs_ref, ng_ref, s0_ref, o_ref, sout_ref, hd, seq)


def _hgrn_head(q_ref, v_ref, ff_ref, fb_ref, gs_ref, ng_ref, s0_ref, o_ref, sout_ref, hd, seq):
    has_state = s0_ref is not None
    cols = slice(hd * HG_DK, (hd + 1) * HG_DK)
    c = HG_CHUNK
    n = seq // c
    q3 = q_ref[:, cols].astype(F32).reshape(n, c, HG_DK)
    v3 = v_ref[:, cols].reshape(n, c, HG_DV)
    pos = lax.broadcasted_iota(jnp.int32, (seq, HG_DK), 0) % c
    t_idx = lax.broadcasted_iota(jnp.int32, (c, c), 0)
    s_idx = lax.broadcasted_iota(jnp.int32, (c, c), 1)
    o_sum = None
    for direction, f_ref in enumerate((ff_ref, fb_ref)):
        reverse = direction == 1
        f = f_ref[:, cols]
        k3 = (1.0 - f).reshape(n, c, HG_DK)
        b = jnp.log(f)
        step = 1
        while step < c:
            if reverse:
                b = b + jnp.where(pos < c - step, pltpu.roll(b, seq - step, axis=0), 0.0)
            else:
                b = b + jnp.where(pos >= step, pltpu.roll(b, step, axis=0), 0.0)
            step *= 2
        b3 = b.reshape(n, c, HG_DK)
        edge = b3[:, 0:1, :] if reverse else b3[:, c - 1:c, :]
        mid = b3[:, c // 2:c // 2 + 1, :]
        q_rel = q3 * jnp.exp(b3 - mid)
        k_rel = k3 * jnp.exp(mid - b3)
        q_mid = q_rel.astype(BF16)
        k_mid = k_rel.astype(BF16)
        q_in = (q_rel * jnp.exp(mid)).astype(BF16)
        k_end = (k_rel * jnp.exp(edge - mid)).astype(BF16)
        scores = jnp.einsum('ntd,nsd->nts', q_mid, k_mid, preferred_element_type=F32)
        keep = (s_idx >= t_idx) if reverse else (s_idx <= t_idx)
        scores = jnp.where(keep[None], scores, 0.0).astype(BF16)
        o_intra = jnp.einsum('nts,nsv->ntv', scores, v3, preferred_element_type=F32)
        upd = jnp.einsum('nsv,nsd->nvd', v3, k_end, preferred_element_type=F32)
        dec = jnp.exp(edge)
        if has_state:
            st = s0_ref[0, direction, hd].T
        else:
            st = jnp.zeros((HG_DV, HG_DK), F32)
        before = [None] * n
        for ci in (range(n - 1, -1, -1) if reverse else range(n)):
            before[ci] = st.astype(BF16)
            st = st * dec[ci] + upd[ci]
        st_before = jnp.stack(before, axis=0)
        o_inter = jnp.einsum('ntd,nvd->ntv', q_in, st_before, preferred_element_type=F32)
        o_dir = (o_intra + o_inter).reshape(seq, HG_DV)
        o_sum = o_dir if o_sum is None else o_sum + o_dir
        sout_ref[0, direction, hd] = st.T
    ms = jnp.mean(o_sum * o_sum, axis=-1, keepdims=True)
    o = o_sum * lax.rsqrt(ms + NORM_EPS) * ng_ref[...]
    o_ref[:, cols] = (o * gs_ref[:, cols].astype(F32)).astype(o_ref.dtype)


def _hgrn_scan(zh, zf, norm_g, s0, *, batch, seq, row_block0, heads):
    has_state = s0 is not None
    h = HG_HEADS
    groups = h // heads
    width = heads * HG_DK

    def col(section):
        return pl.BlockSpec((seq, width), lambda b, j: (row_block0 + b, section * groups + j))

    in_specs = [col(0), col(1), col(0), col(1), col(2), pl.BlockSpec((1, HG_DV), lambda b, j: (0, 0))]
    args = [zh, zh, zf, zf, zh, norm_g]
    state_spec = pl.BlockSpec((1, 2, heads, HG_DK, HG_DV), lambda b, j: (b, 0, j, 0, 0))
    if has_state:
        in_specs.append(state_spec)
        args.append(s0)
    return pl.pallas_call(
        functools.partial(_hgrn_kernel, seq=seq, has_state=has_state, heads=heads),
        grid=(batch, groups),
        in_specs=in_specs,
        out_specs=[pl.BlockSpec((seq, width), lambda b, j: (b, j)), state_spec],
        out_shape=[
            jax.ShapeDtypeStruct((batch * seq, h * HG_DV), BF16),
            jax.ShapeDtypeStruct((batch, 2, h, HG_DK, HG_DV), F32),
        ],
        compiler_params=_params(("arbitrary", "arbitrary")),
    )(*args)


def _group_rms_norm(x, group_ones, gain):
    sq = x * x
    hi = sq.astype(BF16)
    lo = (sq - hi.astype(F32)).astype(BF16)
    total = (jnp.dot(hi, group_ones, preferred_element_type=F32)
             + jnp.dot(lo, group_ones, preferred_element_type=F32))
    return x * lax.rsqrt(total * (1.0 / HEAD_DIM) + NORM_EPS) * gain


def _rope(x, cos, sin_signed):
    width = x.shape[1]
    quarter = HEAD_DIM // 4
    lane = lax.broadcasted_iota(jnp.int32, x.shape, 1)
    partner = jnp.where(lane % (2 * quarter) < quarter,
                        pltpu.roll(x, width - quarter, axis=1),
                        pltpu.roll(x, quarter, axis=1))
    return x * cos + partner * sin_signed


def _attn_kernel(*refs, seq, latent):
    if latent:
        za_ref, qg_ref, kg_ref, gm_ref, cos_ref, sin_ref, ck_ref, cv_ref, o_ref = refs
    else:
        za_ref, qg_ref, kg_ref, gm_ref, o_ref, kout_ref, vout_ref = refs
    q = za_ref[:, 0:ATT_WIDTH]
    k = za_ref[:, ATT_WIDTH:ATT_WIDTH + KV_WIDTH]
    v = za_ref[:, ATT_WIDTH + KV_WIDTH:ATT_WIDTH + 2 * KV_WIDTH]
    qn = _group_rms_norm(q, gm_ref[...], qg_ref[...])
    kn = _group_rms_norm(k, gm_ref[0:KV_WIDTH, 0:KV_WIDTH], kg_ref[...])
    if latent:
        qn = _rope(qn, cos_ref[...], sin_ref[...])
        kr = _rope(kn, cos_ref[:, 0:KV_WIDTH], sin_ref[:, 0:KV_WIDTH])
    else:
        kout_ref[...] = kn
        vout_ref[...] = v
        kr = kn
    qb = (qn * (HEAD_DIM ** -0.5)).astype(BF16)
    kb = kr.astype(BF16)
    vb = v.astype(BF16)
    if latent:
        ckb = ck_ref[0].astype(BF16)
        cvb = cv_ref[0].astype(BF16)
    groups = N_HEADS // N_KV_HEADS
    tq = ATT_Q_BLOCK if latent else seq
    nt = (((1,), (1,)), ((), ()))
    for kh in range(N_KV_HEADS):
        ksl = slice(kh * HEAD_DIM, (kh + 1) * HEAD_DIM)
        k_new = kb[:, ksl]
        v_new = vb[:, ksl]
        for blk in range(seq // tq):
            rows = slice(blk * tq, (blk + 1) * tq)
            heads = [kh * groups + g for g in range(groups)]
            q_st = jnp.concatenate([qb[rows, hd * HEAD_DIM:(hd + 1) * HEAD_DIM] for hd in heads], axis=0)
            s_new = lax.dot_general(q_st, k_new, nt, preferred_element_type=F32)
            m = jnp.max(s_new, axis=-1, keepdims=True)
            if latent:
                s_ctx = lax.dot_general(q_st, ckb[:, ksl], nt, preferred_element_type=F32)
                m = jnp.maximum(m, jnp.max(s_ctx, axis=-1, keepdims=True))
            p_new = jnp.exp(s_new - m)
            denom = jnp.sum(p_new, axis=-1, keepdims=True)
            acc = jnp.dot(p_new.astype(BF16), v_new, preferred_element_type=F32)
            if latent:
                p_ctx = jnp.exp(s_ctx - m)
                denom = denom + jnp.sum(p_ctx, axis=-1, keepdims=True)
                acc = acc + jnp.dot(p_ctx.astype(BF16), cvb[:, ksl], preferred_element_type=F32)
            out = acc / denom
            for g in range(0, groups, 2):
                pair = jnp.concatenate([out[g * tq:(g + 1) * tq], out[(g + 1) * tq:(g + 2) * tq]], axis=1)
                lane0 = heads[g] * HEAD_DIM
                o_ref[rows, lane0:lane0 + 2 * HEAD_DIM] = pair.astype(o_ref.dtype)


def _attention(za, q_gain, k_gain, group_ones, *, batch, seq, row_block0, rope=None, cache=None):
    latent = cache is not None
    za_w = za.shape[1]
    in_specs = [
        pl.BlockSpec((seq, za_w), lambda b: (row_block0 + b, 0)),
        pl.BlockSpec((1, ATT_WIDTH), lambda b: (0, 0)),
        pl.BlockSpec((1, KV_WIDTH), lambda b: (0, 0)),
        pl.BlockSpec((ATT_WIDTH, ATT_WIDTH), lambda b: (0, 0)),
    ]
    args = [za, q_gain, k_gain, group_ones]
    o_spec = pl.BlockSpec((seq, ATT_WIDTH), lambda b: (b, 0))
    o_shape = jax.ShapeDtypeStruct((batch * seq, ATT_WIDTH), BF16)
    if latent:
        cos, sin_signed = rope
        ck, cv = cache
        past = ck.shape[1]
        in_specs += [
            pl.BlockSpec((seq, ATT_WIDTH), lambda b: (0, 0)),
            pl.BlockSpec((seq, ATT_WIDTH), lambda b: (0, 0)),
            pl.BlockSpec((1, past, KV_WIDTH), lambda b: (b, 0, 0)),
            pl.BlockSpec((1, past, KV_WIDTH), lambda b: (b, 0, 0)),
        ]
        args += [cos, sin_signed, ck, cv]
        out_specs = o_spec
        out_shape = o_shape
    else:
        kv_spec = pl.BlockSpec((seq, KV_WIDTH), lambda b: (b, 0))
        kv_shape = jax.ShapeDtypeStruct((batch * seq, KV_WIDTH), F32)
        out_specs = [o_spec, kv_spec, kv_spec]
        out_shape = [o_shape, kv_shape, kv_shape]
    return pl.pallas_call(
        functools.partial(_attn_kernel, seq=seq, latent=latent),
        grid=(batch,),
        in_specs=in_specs,
        out_specs=out_specs,
        out_shape=out_shape,
        compiler_params=_params(("arbitrary",)),
    )(*args)


def _rope_tables(seq):
    quarter = HEAD_DIM // 4
    t = jnp.arange(seq)
    row = (t // GRID_W).astype(F32)
    colp = (t % GRID_W).astype(F32)
    inv_freq = ROPE_THETA ** (-jnp.arange(quarter, dtype=F32) / quarter)
    lane = jnp.arange(HEAD_DIM)
    pos = jnp.where((lane < HEAD_DIM // 2)[None, :], row[:, None], colp[:, None])
    ang = pos * inv_freq[lane % quarter][None, :]
    sign = jnp.where(lane % (2 * quarter) < quarter, -1.0, 1.0)[None, :]
    cos = jnp.tile(jnp.cos(ang), (1, N_HEADS))
    sin_signed = jnp.tile(jnp.sin(ang) * sign, (1, N_HEADS))
    return cos, sin_signed


def _layer_norm(y, g, b):
    mu = jnp.mean(y, axis=-1, keepdims=True)
    yc = y - mu
    var = jnp.mean(yc * yc, axis=-1, keepdims=True)
    return yc * lax.rsqrt(var + NORM_EPS) * g + b


def _rows_to_slabs(tile, st_ref, slab_ref):
    rows, d = tile.shape
    chunks = d // LANES
    for j in range(chunks):
        st_ref[j * TILE_STRIDE:j * TILE_STRIDE + rows, :] = tile[:, j * LANES:(j + 1) * LANES]
    for r in range(rows):
        slab_ref[r * chunks:(r + 1) * chunks, :] = st_ref[pl.ds(r, chunks, stride=TILE_STRIDE), :]


def _slabs_to_rows(slab_ref, st_ref, rows):
    chunks = SUBLANES
    for r in range(rows):
        st_ref[pl.ds(r, chunks, stride=TILE_STRIDE), :] = slab_ref[r * chunks:(r + 1) * chunks, :]
    return jnp.concatenate([st_ref[j * TILE_STRIDE:j * TILE_STRIDE + rows, :] for j in range(chunks)], axis=1)


def _post_mixer_kernel(xc_ref, xl_ref, mod_ref, oac_ref, oal_ref, obc_ref, obl_ref, zg_ref, wa_ref, wb_ref,
                       wo_ref, g_ref, b_ref, wrh_ref, wrl_ref, rb_ref,
                       x1_ref, slab_ref, idx_ref, rw_ref, st_ref, wab_ref, wbb_ref, wob_ref,
                       *, d, n_ctx_tiles):
    for src, dst in ((wa_ref, wab_ref), (wb_ref, wbb_ref), (wo_ref, wob_ref)):
        _cast_weight_once(src, dst)
    is_ctx = pl.program_id(0) < n_ctx_tiles
    gate1 = mod_ref[0, :, 2 * d:3 * d]
    shift2 = mod_ref[0, :, 3 * d:4 * d]
    scale2 = mod_ref[0, :, 4 * d:5 * d]
    branch_a = jnp.dot(_pick(is_ctx, oac_ref, oal_ref), wab_ref[...], preferred_element_type=F32)
    branch_b = jnp.dot(_pick(is_ctx, obc_ref, obl_ref), wbb_ref[...], preferred_element_type=F32)
    merged = zg_ref[:, 0:d].astype(F32) * branch_a + zg_ref[:, d:2 * d].astype(F32) * branch_b
    mix = jnp.dot(merged.astype(BF16), wob_ref[...], preferred_element_type=F32)
    x = _pick(is_ctx, xc_ref, xl_ref)
    x1 = _layer_norm(DEEPNORM_ALPHA * x + gate1 * mix, g_ref[...], b_ref[...])
    x1_ref[...] = x1
    h2 = x1 * (1.0 + scale2) + shift2
    _rows_to_slabs(h2, st_ref, slab_ref)
    idx_ref[...], rw_ref[...] = _route_tile(h2, wrh_ref[...], wrl_ref[...], rb_ref[...])


def _post_mixer(x_ctx, x_lat, mod3, oa_ctx, oa_lat, ob_ctx, ob_lat, zg, w_a, w_b, w_o, ln_g, ln_b,
                wr_hi, wr_lo, router_bias, mod_map):
    d = x_ctx.shape[1]
    n_ctx_tiles = x_ctx.shape[0] // TOKEN_TILE
    t = x_ctx.shape[0] + x_lat.shape[0]
    tm = TOKEN_TILE
    row = lambda i: (i, 0)
    full = lambda i: (0, 0)
    return pl.pallas_call(
        functools.partial(_post_mixer_kernel, d=d, n_ctx_tiles=n_ctx_tiles),
        grid=(t // tm,),
        in_specs=_pair_specs(d, n_ctx_tiles) + [
            pl.BlockSpec((1, 1, mod3.shape[2]), mod_map),
        ] + _pair_specs(oa_ctx.shape[1], n_ctx_tiles) + _pair_specs(ob_ctx.shape[1], n_ctx_tiles) + [
            pl.BlockSpec((tm, 2 * d), row),
            _resident(w_a.shape),
            _resident(w_b.shape),
            _resident(w_o.shape),
            pl.BlockSpec((1, d), full),
            pl.BlockSpec((1, d), full),
            pl.BlockSpec((N_EXPERTS, d), full),
            pl.BlockSpec((N_EXPERTS, d), full),
            pl.BlockSpec((N_EXPERTS, 1), full),
        ],
        out_specs=[pl.BlockSpec((tm, d), row),
                   pl.BlockSpec((tm * SUBLANES, LANES), row),
                   pl.BlockSpec((TOP_K, tm), lambda i: (0, i)), pl.BlockSpec((TOP_K, tm), lambda i: (0, i))],
        out_shape=[jax.ShapeDtypeStruct((t, d), F32),
                   jax.ShapeDtypeStruct((t * SUBLANES, LANES), F32),
                   jax.ShapeDtypeStruct((TOP_K, t), jnp.int32), jax.ShapeDtypeStruct((TOP_K, t), F32)],
        scratch_shapes=[pltpu.VMEM((SUBLANES * TILE_STRIDE, LANES), F32),
                        pltpu.VMEM(w_a.shape, BF16), pltpu.VMEM(w_b.shape, BF16), pltpu.VMEM(w_o.shape, BF16)],
        compiler_params=_params(("arbitrary",)),
    )(x_ctx, x_lat, mod3, oa_ctx, oa_lat, ob_ctx, ob_lat, zg, w_a, w_b, w_o, ln_g, ln_b, wr_hi, wr_lo, router_bias)


def _first_index_of_max(x, idx, sentinel):
    m = jnp.max(x, axis=0, keepdims=True)
    first = jnp.min(jnp.where(x == m, idx, sentinel), axis=0, keepdims=True)
    return m, first


def _route_tile(h, wh, wl, bias):
    hh = h.astype(BF16)
    hl = (h - hh.astype(F32)).astype(BF16)
    nt = (((1,), (1,)), ((), ()))
    logits = (lax.dot_general(wh, hh, nt, preferred_element_type=F32)
              + lax.dot_general(wh, hl, nt, preferred_element_type=F32)
              + lax.dot_general(wl, hh, nt, preferred_element_type=F32))
    scores = _sigmoid(logits)
    sel = scores + bias
    tm = sel.shape[1]
    neg = -jnp.inf
    gidx = lax.broadcasted_iota(jnp.int32, (GROUP_SIZE, tm), 0)
    group_scores = []
    for g in range(N_GROUPS):
        sg = sel[g * GROUP_SIZE:(g + 1) * GROUP_SIZE, :]
        m1, first = _first_index_of_max(sg, gidx, GROUP_SIZE)
        m2 = jnp.max(jnp.where(gidx == first, neg, sg), axis=0, keepdims=True)
        group_scores.append(m1 + m2)
    gs = jnp.concatenate(group_scores, axis=0)
    nidx = lax.broadcasted_iota(jnp.int32, (N_GROUPS, tm), 0)
    chosen = jnp.zeros((N_GROUPS, tm), jnp.bool_)
    for _ in range(TOPK_GROUPS):
        _, first = _first_index_of_max(gs, nidx, N_GROUPS)
        hit = nidx == first
        chosen = jnp.logical_or(chosen, hit)
        gs = jnp.where(hit, neg, gs)
    masked = jnp.concatenate(
        [jnp.where(chosen[g:g + 1, :], sel[g * GROUP_SIZE:(g + 1) * GROUP_SIZE, :], neg) for g in range(N_GROUPS)],
        axis=0)
    eidx = lax.broadcasted_iota(jnp.int32, (N_EXPERTS, tm), 0)
    picks, weights = [], []
    for _ in range(TOP_K):
        _, first = _first_index_of_max(masked, eidx, N_EXPERTS)
        hit = eidx == first
        picks.append(first)
        weights.append(jnp.sum(jnp.where(hit, scores, 0.0), axis=0, keepdims=True))
        masked = jnp.where(hit, neg, masked)
    wk = jnp.concatenate(weights, axis=0)
    return jnp.concatenate(picks, axis=0), wk / jnp.sum(wk, axis=0, keepdims=True) * ROUTED_SCALE


def _dispatch_lists(idx_t, w_t):
    k, t = idx_t.shape
    s = k * t
    assert N_EXPERTS * t < 2 ** 31
    n_blocks = -(-(s // MOE_BLOCK + N_EXPERTS) // BLOCK_COUNT_MULTIPLE) * BLOCK_COUNT_MULTIPLE
    flat_tok = jnp.arange(s, dtype=jnp.int32) % t
    sorted_key, sorted_w = lax.sort((idx_t.reshape(s) * t + flat_tok, w_t.reshape(s)), num_keys=1)
    sorted_tok = sorted_key % t
    experts = jnp.arange(N_EXPERTS + 1, dtype=jnp.int32)
    start = jnp.sum(sorted_key[None, :] < experts[:, None] * t, axis=1, dtype=jnp.int32)
    counts = start[1:] - start[:-1]
    blocks_per_e = (counts + MOE_BLOCK - 1) // MOE_BLOCK
    block_end = jnp.cumsum(blocks_per_e)
    n_used = block_end[-1]
    blk = jnp.arange(n_blocks, dtype=jnp.int32)
    block_e = jnp.sum(block_end[None, :] <= blk[:, None], axis=1, dtype=jnp.int32)
    block_e = jnp.minimum(block_e, N_EXPERTS - 1)
    onehot = (block_e[:, None] == experts[None, :N_EXPERTS]).astype(jnp.int32)
    pick = lambda a: jnp.sum(onehot * a[None, :], axis=1)
    offset = (blk - (pick(block_end) - pick(blocks_per_e))) * MOE_BLOCK
    row_start = pick(start[:-1]) + offset
    n_rows = jnp.clip(pick(counts) - offset, 0, MOE_BLOCK)
    used = blk < n_used
    last_e = jnp.sum(jnp.where(blk == n_used - 1, block_e, 0))
    block_e = jnp.where(used, block_e, last_e)
    row_start = jnp.where(used, row_start, 0)
    n_rows = jnp.where(used, n_rows, 0)
    return block_e, row_start, n_rows, (n_used - 1).reshape(1), sorted_tok, sorted_w


def _block_lists_kernel(rs_ref, nr_ref, tok_ref, w_ref, off_ref, wout_ref, *, t):
    lane = lax.broadcasted_iota(jnp.int32, (LIST_ROWS, LANES), 1)
    row = lax.broadcasted_iota(jnp.int32, (LIST_ROWS, LANES), 0)

    def one_block(b, carry):
        start = rs_ref[b]
        n_rows = nr_ref[b]
        q = start // LANES
        lane0 = start % LANES
        q8 = pl.multiple_of((q // SUBLANES) * SUBLANES, SUBLANES)

        def window(ref):
            a = ref[pl.ds(q8, 2 * SUBLANES), :]
            a = pltpu.roll(a, 2 * SUBLANES - (q - q8), axis=0)
            lo = pltpu.roll(a[0:LIST_ROWS], LANES - lane0, axis=1)
            hi = pltpu.roll(a[1:LIST_ROWS + 1], LANES - lane0, axis=1)
            return jnp.where(lane < LANES - lane0, lo, hi)

        valid = row * LANES + lane < n_rows
        off_ref[b] = jnp.where(valid, window(tok_ref) * SUBLANES, t * SUBLANES)
        wout_ref[b] = jnp.where(valid, window(w_ref), 0.0)
        return carry

    n_blocks = off_ref.shape[0]
    unroll = next(u for u in (4, 2, 1) if n_blocks % u == 0)

    def several_blocks(i, carry):
        for u in range(unroll):
            one_block(i * unroll + u, carry)
        return carry

    lax.fori_loop(0, n_blocks // unroll, several_blocks, 0)


def _block_lists(row_start, n_rows, sorted_tok, sorted_w, t):
    n_blocks = row_start.shape[0]
    n_slots = sorted_tok.shape[0]
    list_rows = n_slots // LANES + 2 * SUBLANES
    pad = list_rows * LANES - n_slots
    tok2d = jnp.pad(sorted_tok, (0, pad)).reshape(list_rows, LANES)
    w2d = jnp.pad(sorted_w, (0, pad)).reshape(list_rows, LANES)
    full = lambda i, rs, nr: (0, 0)
    blk = pl.BlockSpec((n_blocks, LIST_ROWS, LANES), lambda i, rs, nr: (0, 0, 0))
    off, wts = pl.pallas_call(
        functools.partial(_block_lists_kernel, t=t),
        grid_spec=pltpu.PrefetchScalarGridSpec(
            num_scalar_prefetch=2,
            grid=(1,),
            in_specs=[pl.BlockSpec((list_rows, LANES), full), pl.BlockSpec((list_rows, LANES), full)],
            out_specs=[blk, blk],
        ),
        out_shape=[jax.ShapeDtypeStruct((n_blocks, LIST_ROWS, LANES), jnp.int32),
                   jax.ShapeDtypeStruct((n_blocks, LIST_ROWS, LANES), F32)],
        compiler_params=_params(("arbitrary",)),
    )(row_start, n_rows, tok2d, w2d)
    return off.reshape(n_blocks * LIST_LEN), wts.reshape(n_blocks * LIST_LEN)


def _gather_kernel(nr_ref, last_ref, x_hbm, off_ref, out_ref, xs, xt, sem):
    step = pl.program_id(0)
    per_step = out_ref.shape[0] // MOE_BLOCK
    chunks = out_ref.shape[1] // LANES

    @pl.when(step == 0)
    def _load():
        rows = x_hbm.shape[0]
        cp = pltpu.make_async_copy(x_hbm, xs.at[pl.ds(0, rows)], sem)
        cp.start()
        cp.wait()
        xs[pl.ds(rows, SUBLANES), :] = jnp.zeros((SUBLANES, LANES), F32)
        xt[...] = jnp.zeros(xt.shape, F32)

    def one_block(g):
        n_rows = nr_ref[step * per_step + g]

        @pl.when(n_rows > 0)
        def _block():
            for seg in range(MOE_BLOCK // ROW_SEGMENT):
                @pl.when(n_rows > seg * ROW_SEGMENT)
                def _gather(seg=seg):
                    for r in range(seg * ROW_SEGMENT, (seg + 1) * ROW_SEGMENT):
                        src = pl.multiple_of(off_ref[g * LIST_LEN + r], SUBLANES)
                        xt[pl.ds(r, chunks, stride=ROW_STRIDE), :] = xs[pl.ds(src, SUBLANES), :]

            x = jnp.concatenate(
                [xt[j * ROW_STRIDE:j * ROW_STRIDE + MOE_BLOCK, :] for j in range(chunks)], axis=1)
            out_ref[g * MOE_BLOCK:(g + 1) * MOE_BLOCK, :] = x.astype(out_ref.dtype)

    for g in range(per_step):
        one_block(g)


def _expert_kernel(be_ref, nr_ref, last_ref, x_ref, wg_hbm, wu_hbm, wd_hbm, off_ref, w_ref, y_hbm,
                   ys, ot, wgb, wub, wdb, sem, wsem, *, t, n_steps):
    step = pl.program_id(0)
    per_step = x_ref.shape[0] // MOE_BLOCK
    rows = t * SUBLANES
    chunks = x_ref.shape[1] // LANES
    group = ROW_GROUP
    last = last_ref[0]
    ahead = WEIGHT_BUFFERS - 1

    def weight_copies(blk):
        e = be_ref[blk]
        slot = blk % WEIGHT_BUFFERS
        return [pltpu.make_async_copy(src.at[e], dst.at[slot], wsem.at[slot])
                for src, dst in ((wg_hbm, wgb), (wu_hbm, wub), (wd_hbm, wdb))]

    @pl.when(step == 0)
    def _init():
        for blk in range(ahead):
            @pl.when(blk <= last)
            def _(blk=blk):
                for cp in weight_copies(blk):
                    cp.start()
        ys[...] = jnp.zeros(ys.shape, F32)

    def one_block(g):
        b = step * per_step + g
        n_rows = nr_ref[b]
        row0 = g * MOE_BLOCK
        slot0 = g * LIST_LEN

        @pl.when(b + ahead <= last)
        def _prefetch():
            for cp in weight_copies(b + ahead):
                cp.start()

        @pl.when(b <= last)
        def _block():
            slot = b % WEIGHT_BUFFERS
            for cp in weight_copies(b):
                cp.wait()
            x = x_ref[row0:row0 + MOE_BLOCK, :]
            gate = jnp.dot(x, wgb[slot].astype(BF16), preferred_element_type=F32)
            up = jnp.dot(x, wub[slot].astype(BF16), preferred_element_type=F32)
            hidden = (_silu(gate) * up).astype(BF16)
            out = jnp.dot(hidden, wdb[slot].astype(BF16), preferred_element_type=F32)
            for j in range(chunks):
                ot[j * ROW_STRIDE:j * ROW_STRIDE + MOE_BLOCK, :] = out[:, j * LANES:(j + 1) * LANES]
            for seg in range(MOE_BLOCK // ROW_SEGMENT):
                @pl.when(n_rows > seg * ROW_SEGMENT)
                def _scatter(seg=seg):
                    for r0 in range(seg * ROW_SEGMENT, (seg + 1) * ROW_SEGMENT, group):
                        dst = [pl.multiple_of(off_ref[slot0 + r0 + i], SUBLANES) for i in range(group)]
                        vals = [ys[pl.ds(dst[i], SUBLANES), :]
                                + w_ref[slot0 + r0 + i] * ot[pl.ds(r0 + i, chunks, stride=ROW_STRIDE), :]
                                for i in range(group)]
                        for i in range(group):
                            ys[pl.ds(dst[i], SUBLANES), :] = vals[i]

    for g in range(per_step):
        one_block(g)

    @pl.when(step == n_steps - 1)
    def _store():
        cp = pltpu.make_async_copy(ys.at[pl.ds(0, rows)], y_hbm, sem)
        cp.start()
        cp.wait()


def _routed_experts(h2_rows, block_e, row_start, n_rows, last_block, sorted_tok, sorted_w, w_gate, w_up, w_down):
    rows, lanes = h2_rows.shape
    t = rows // SUBLANES
    n_blocks = block_e.shape[0]
    d, e_dim = w_gate.shape[1], w_gate.shape[2]
    assert d == SUBLANES * LANES and lanes == LANES
    offsets, weights = _block_lists(row_start, n_rows, sorted_tok, sorted_w, t)
    staging = pltpu.VMEM((SUBLANES * ROW_STRIDE, LANES), F32)
    resident = pltpu.VMEM((rows + SUBLANES, LANES), F32)

    def step_map(per_step):
        return lambda s, *prefetch: (jnp.minimum(s, prefetch[-1][0] // per_step), 0)

    def list_spec(per_step):
        index = step_map(per_step)
        return pl.BlockSpec((per_step * LIST_LEN,), lambda s, *prefetch: index(s, *prefetch)[:1],
                            memory_space=pltpu.SMEM)

    per_step = next(p for p in GATHER_BLOCKS_PER_STEP if n_blocks % p == 0)
    gathered = pl.pallas_call(
        _gather_kernel,
        grid_spec=pltpu.PrefetchScalarGridSpec(
            num_scalar_prefetch=2,
            grid=(n_blocks // per_step,),
            in_specs=[pl.BlockSpec(memory_space=pl.ANY), list_spec(per_step)],
            out_specs=pl.BlockSpec((per_step * MOE_BLOCK, d), step_map(per_step)),
            scratch_shapes=[resident, staging, pltpu.SemaphoreType.DMA(())],
        ),
        out_shape=jax.ShapeDtypeStruct((n_blocks * MOE_BLOCK, d), BF16),
        compiler_params=_params(("arbitrary",)),
    )(n_rows, last_block, h2_rows, offsets)

    per_step = next(p for p in range(MAX_BLOCKS_PER_STEP, 0, -1) if n_blocks % p == 0)
    n_steps = n_blocks // per_step
    hbm = pl.BlockSpec(memory_space=pl.ANY)
    return pl.pallas_call(
        functools.partial(_expert_kernel, t=t, n_steps=n_steps),
        grid_spec=pltpu.PrefetchScalarGridSpec(
            num_scalar_prefetch=3,
            grid=(n_steps,),
            in_specs=[pl.BlockSpec((per_step * MOE_BLOCK, d), step_map(per_step)), hbm, hbm, hbm,
                      list_spec(per_step), list_spec(per_step)],
            out_specs=hbm,
            scratch_shapes=[
                resident,
                staging,
                pltpu.VMEM((WEIGHT_BUFFERS, d, e_dim), F32),
                pltpu.VMEM((WEIGHT_BUFFERS, d, e_dim), F32),
                pltpu.VMEM((WEIGHT_BUFFERS, e_dim, d), F32),
                pltpu.SemaphoreType.DMA(()),
                pltpu.SemaphoreType.DMA((WEIGHT_BUFFERS,)),
            ],
        ),
        out_shape=jax.ShapeDtypeStruct((rows, LANES), F32),
        compiler_params=_params(("arbitrary",)),
    )(block_e, n_rows, last_block, gathered, w_gate, w_up, w_down, offsets, weights)


def _final_kernel(x1_ref, routed_ref, mod_ref, wg_ref, wu_ref, wd_ref, g_ref, b_ref,
                  yc_ref, yl_ref, st_ref, wgb_ref, wub_ref, wdb_ref, *, d, n_ctx_tiles):
    for src, dst in ((wg_ref, wgb_ref), (wu_ref, wub_ref), (wd_ref, wdb_ref)):
        _cast_weight_once(src, dst)
    is_ctx = pl.program_id(0) < n_ctx_tiles
    shift2 = mod_ref[0, :, 3 * d:4 * d]
    scale2 = mod_ref[0, :, 4 * d:5 * d]
    gate2 = mod_ref[0, :, 5 * d:6 * d]
    x1 = x1_ref[...]
    h = (x1 * (1.0 + scale2) + shift2).astype(BF16)
    gate = jnp.dot(h, wgb_ref[...], preferred_element_type=F32)
    up = jnp.dot(h, wub_ref[...], preferred_element_type=F32)
    shared = jnp.dot((_silu(gate) * up).astype(BF16), wdb_ref[...], preferred_element_type=F32)
    ffn = _slabs_to_rows(routed_ref, st_ref, h.shape[0]) + shared
    y = _layer_norm(DEEPNORM_ALPHA * x1 + gate2 * ffn, g_ref[...], b_ref[...])

    @pl.when(is_ctx)
    def _():
        yc_ref[...] = y

    @pl.when(jnp.logical_not(is_ctx))
    def _():
        yl_ref[...] = y


def _final(x1, routed, mod3, w_g, w_u, w_d, ln_g, ln_b, mod_map, t_ctx):
    t, d = x1.shape
    t_lat = t - t_ctx
    n_ctx_tiles = t_ctx // TOKEN_TILE
    tm = TOKEN_TILE
    row = lambda i: (i, 0)
    full = lambda i: (0, 0)
    return pl.pallas_call(
        functools.partial(_final_kernel, d=d, n_ctx_tiles=n_ctx_tiles),
        grid=(t // tm,),
        in_specs=[
            pl.BlockSpec((tm, d), row),
            pl.BlockSpec((tm * SUBLANES, LANES), row),
            pl.BlockSpec((1, 1, mod3.shape[2]), mod_map),
            _resident(w_g.shape),
            _resident(w_u.shape),
            _resident(w_d.shape),
            pl.BlockSpec((1, d), full),
            pl.BlockSpec((1, d), full),
        ],
        out_specs=_pair_specs(d, n_ctx_tiles),
        out_shape=[jax.ShapeDtypeStruct((t_ctx, d), F32), jax.ShapeDtypeStruct((t_lat, d), F32)],
        scratch_shapes=[pltpu.VMEM((SUBLANES * TILE_STRIDE, LANES), F32),
                        pltpu.VMEM(w_g.shape, BF16), pltpu.VMEM(w_u.shape, BF16), pltpu.VMEM(w_d.shape, BF16)],
        compiler_params=_params(("arbitrary",)),
    )(x1, routed, mod3, w_g, w_u, w_d, ln_g, ln_b)


def kernel(x_prompt, x_sample, cache_k, cache_v, state_hgrn, c, c_ctx, w_mod, b_mod, w_in, hg_lb, hg_norm_g, q_norm_g, k_norm_g, w_branch_a, w_branch_b, w_out, ln1_g, ln1_b, w_router, router_bias, w_e_gate, w_e_up, w_e_down, w_s_gate, w_s_up, w_s_down, ln2_g, ln2_b):
    assert w_mod.shape[0] == DEPTH
    n_ctx, seq_ctx, d = x_prompt.shape
    n_lat, seq_lat, _ = x_sample.shape
    t_ctx = n_ctx * seq_ctx
    t_lat = n_lat * seq_lat
    assert seq_ctx == TOKEN_TILE and seq_lat % TOKEN_TILE == 0
    layer = 0

    lb = jnp.cumsum(jax.nn.softmax(hg_lb.astype(F32), axis=0), axis=0)[layer]

    cond = jnp.concatenate([c_ctx[None, :], c], axis=0)
    cond = jnp.pad(cond, ((0, (-cond.shape[0]) % SUBLANES), (0, 0)))
    mod = _modulation(cond, w_mod[layer], b_mod[layer][None, :])
    mod3 = mod.reshape(mod.shape[0], 1, mod.shape[1])
    mod_map = _mod_row_map(t_ctx // TOKEN_TILE, seq_lat // TOKEN_TILE)

    x_ctx = x_prompt.reshape(t_ctx, d)
    x_lat = x_sample.reshape(t_lat, d)
    zh, zf, za, zg = _input_projection(x_ctx, x_lat, mod3, w_in[layer], lb, mod_map)

    norm_g = hg_norm_g[layer][None, :]
    oa_ctx, s_ctx = _hgrn_scan(zh, zf, norm_g, None, batch=n_ctx, seq=seq_ctx, row_block0=0,
                               heads=HG_HEADS_PER_STEP_CTX)
    oa_lat, _ = _hgrn_scan(zh, zf, norm_g, state_hgrn[:, layer], batch=n_lat, seq=seq_lat,
                           row_block0=t_ctx // seq_lat, heads=HG_HEADS_PER_STEP_LATENT)

    q_gain = jnp.tile(q_norm_g[layer], N_HEADS)[None, :]
    k_gain = jnp.tile(k_norm_g[layer], N_KV_HEADS)[None, :]
    lane = jnp.arange(ATT_WIDTH)
    group_ones = (lane[:, None] // HEAD_DIM == lane[None, :] // HEAD_DIM).astype(BF16)
    ob_ctx, k_ctx, v_ctx = _attention(za, q_gain, k_gain, group_ones, batch=n_ctx, seq=seq_ctx, row_block0=0)
    past = cache_k.shape[2]
    ob_lat = _attention(
        za, q_gain, k_gain, group_ones, batch=n_lat, seq=seq_lat, row_block0=t_ctx // seq_lat,
        rope=_rope_tables(seq_lat),
        cache=(cache_k[:, layer].reshape(n_lat, past, KV_WIDTH), cache_v[:, layer].reshape(n_lat, past, KV_WIDTH)))

    wr_t = w_router[layer].T
    wr_hi = wr_t.astype(BF16)
    wr_lo = (wr_t - wr_hi.astype(F32)).astype(BF16)
    x1, h2_slabs, idx_t, w_t = _post_mixer(
        x_ctx, x_lat, mod3, oa_ctx, oa_lat, ob_ctx, ob_lat, zg, w_branch_a[layer], w_branch_b[layer], w_out[layer],
        ln1_g[layer][None, :], ln1_b[layer][None, :], wr_hi, wr_lo, router_bias[layer][:, None], mod_map)

    routed = _routed_experts(h2_slabs, *_dispatch_lists(idx_t, w_t), w_e_gate[layer], w_e_up[layer], w_e_down[layer])

    y_ctx, y_lat = _final(x1, routed, mod3, w_s_gate[layer], w_s_up[layer], w_s_down[layer],
                          ln2_g[layer][None, :], ln2_b[layer][None, :], mod_map, t_ctx)

    y_prompt = y_ctx.reshape(n_ctx, seq_ctx, d)
    y_sample = y_lat.reshape(n_lat, seq_lat, d)
    new_cache_k = k_ctx.reshape(n_ctx, 1, seq_ctx, N_KV_HEADS, HEAD_DIM)
    new_cache_v = v_ctx.reshape(n_ctx, 1, seq_ctx, N_KV_HEADS, HEAD_DIM)
    new_state = s_ctx[:, None]
    return (y_prompt, y_sample, new_cache_k, new_cache_v, new_state)
```
